```python
import jax, jax.numpy as jnp
from jax import lax
import numpy as np

D_MODEL = 2048
BATCH = 8
SEQ = 4096
DEPTH = 1

EPS = 1e-6
A_HEADS = 16
A_KV_HEADS = 4
A_HEAD_DIM = 64
A_WIDTH = A_HEADS * A_HEAD_DIM
A_KV_WIDTH = A_KV_HEADS * A_HEAD_DIM
WINDOW = 128
BLOCK = 128
ROT_DIM = A_HEAD_DIM // 4
ROPE_THETA = 500000.0
B_HEADS = 8
B_KEY_DIM = 128
B_VAL_DIM = 128
B_KEY_WIDTH = B_HEADS * B_KEY_DIM
B_WIDTH = B_HEADS * B_VAL_DIM
CHUNK = 64
FF_DIM = 5632
PLE_DIM = 256
SPLITS = (A_WIDTH, A_KV_WIDTH, A_KV_WIDTH, B_KEY_WIDTH, B_KEY_WIDTH, B_WIDTH, B_WIDTH, D_MODEL, D_MODEL)
IN_DIM = A_WIDTH + 2 * A_KV_WIDTH + 2 * B_KEY_WIDTH + 2 * B_WIDTH + 2 * D_MODEL

kernel_name = "hybrid_swa_sink_hgrn2_macaron_ple"


def rmsnorm(x, w):
    x32 = x.astype(jnp.float32)
    y = x32 * lax.rsqrt(jnp.mean(x32 * x32, axis=-1, keepdims=True) + EPS)
    return (y * w.astype(jnp.float32)).astype(x.dtype)


def swiglu(h, w_gate, w_up, w_down):
    return (jax.nn.silu(h @ w_gate) * (h @ w_up)) @ w_down


def partial_rope(t, positions):
    t32 = t.astype(jnp.float32)
    rot, rest = t32[..., :ROT_DIM], t32[..., ROT_DIM:]
    inv_freq = jnp.power(jnp.float32(ROPE_THETA), -jnp.arange(0, ROT_DIM, 2, dtype=jnp.float32) / ROT_DIM)
    ang = positions.astype(jnp.float32)[..., None] * inv_freq
    cos, sin = jnp.cos(ang)[:, :, None, :], jnp.sin(ang)[:, :, None, :]
    x1, x2 = rot[..., :ROT_DIM // 2], rot[..., ROT_DIM // 2:]
    out = jnp.concatenate([x1 * cos - x2 * sin, x2 * cos + x1 * sin, rest], axis=-1)
    return out.astype(t.dtype)


def sliding_window_attention(q, k, v, sinks):
    b, s = q.shape[0], q.shape[1]
    nb = s // BLOCK
    g = A_HEADS // A_KV_HEADS
    qb = q.reshape(b, nb, BLOCK, A_KV_HEADS, g, A_HEAD_DIM)
    kb = k.reshape(b, nb, BLOCK, A_KV_HEADS, A_HEAD_DIM)
    vb = v.reshape(b, nb, BLOCK, A_KV_HEADS, A_HEAD_DIM)

    def with_prev(t):
        prev = jnp.pad(t[:, :-1], ((0, 0), (1, 0), (0, 0), (0, 0), (0, 0)))
        return jnp.concatenate([prev, t], axis=2)

    kk, vv = with_prev(kb), with_prev(vb)
    scale = A_HEAD_DIM ** -0.5
    scores = jnp.einsum('bnqhgd,bnkhd->bnhgqk', qb, kk).astype(jnp.float32) * scale
    qi = jnp.arange(BLOCK)[:, None]
    kj = jnp.arange(2 * BLOCK)[None, :]
    dist = qi + BLOCK - kj
    blk = jnp.arange(nb)[:, None, None]
    allowed = (dist >= 0) & (dist < WINDOW) & ((blk > 0) | (kj >= BLOCK))
    scores = jnp.where(allowed[None, :, None, None], scores, -jnp.inf)
    sink = sinks.astype(jnp.float32).reshape(A_KV_HEADS, g)[None, None, :, :, None, None]
    m = jnp.maximum(jnp.max(scores, axis=-1, keepdims=True), sink)
    e = jnp.exp(scores - m)
    denom = jnp.sum(e, axis=-1, keepdims=True) + jnp.exp(sink - m)
    probs = (e / denom).astype(v.dtype)
    o = jnp.einsum('bnhgqk,bnkhd->bnqhgd', probs, vv)
    return o.reshape(b, s, A_WIDTH)


def hgrn2_chunk_step(state, inp):
    q, k, v, log_f = inp
    cum = jnp.cumsum(log_f, axis=2)
    causal = jnp.tril(jnp.ones((CHUNK, CHUNK), dtype=bool))
    diff = cum[:, :, :, None, :] - cum[:, :, None, :, :]
    decay = jnp.exp(jnp.where(causal[None, None, :, :, None], diff, -jnp.inf))
    scores = jnp.einsum('bhtd,bhsd,bhtsd->bhts', q, k, decay)
    o = scores @ v + jnp.einsum('bhtd,bhde->bhte', q * jnp.exp(cum), state)
    last = cum[:, :, -1:, :]
    new_state = jnp.exp(last[:, :, 0, :])[..., None] * state + jnp.einsum('bhsd,bhse->bhde', k * jnp.exp(last - cum), v)
    return new_state, o


def hgrn2(q_pre, f_pre, i_in, lb):
    b, s = q_pre.shape[0], q_pre.shape[1]
    nc = s // CHUNK
    f32 = f_pre.astype(jnp.float32)
    q = jax.nn.silu(q_pre.astype(jnp.float32))
    log_f = jnp.logaddexp(jnp.log(lb), jnp.log1p(-lb) + jax.nn.log_sigmoid(f32))
    k = (1.0 - lb) * jax.nn.sigmoid(-f32)
    v = i_in.astype(jnp.float32)

    def to_chunks(t):
        return t.reshape(b, nc, CHUNK, B_HEADS, t.shape[-1]).transpose(1, 0, 3, 2, 4)

    state0 = jnp.zeros((b, B_HEADS, B_KEY_DIM, B_VAL_DIM), jnp.float32)
    _, o = lax.scan(hgrn2_chunk_step, state0, (to_chunks(q), to_chunks(k), to_chunks(v), to_chunks(log_f)))
    return o.transpose(1, 0, 3, 2, 4).reshape(b, s, B_HEADS, B_VAL_DIM)


def _fwd_setup_inputs(seed: int = 0) -> dict:
    key = jax.random.key(seed)
    ks = jax.random.split(key, 24)

    def nrm(k, shape, fan_in):
        return jax.random.normal(k, shape, jnp.float32) * fan_in ** -0.5

    def gain(k, shape):
        return 1.0 + 0.02 * jax.random.normal(k, shape, jnp.float32)

    return {
        "x": jax.random.normal(ks[0], (BATCH, SEQ, D_MODEL), jnp.float32),
        "p": jax.random.normal(ks[1], (DEPTH, BATCH, SEQ, PLE_DIM), jnp.float32),
        "positions": jnp.broadcast_to(jnp.arange(SEQ, dtype=jnp.int32), (BATCH, SEQ)),
        "ffn1_norm": gain(ks[2], (DEPTH, D_MODEL)),
        "ffn1_w_gate": nrm(ks[3], (DEPTH, D_MODEL, FF_DIM), D_MODEL),
        "ffn1_w_up": nrm(ks[4], (DEPTH, D_MODEL, FF_DIM), D_MODEL),
        "ffn1_w_down": nrm(ks[5], (DEPTH, FF_DIM, D_MODEL), FF_DIM),
        "mix_norm": gain(ks[6], (DEPTH, D_MODEL)),
        "w_in": nrm(ks[7], (DEPTH, D_MODEL, IN_DIM), D_MODEL),
        "attn_sinks": 0.5 * jax.random.normal(ks[8], (DEPTH, A_HEADS), jnp.float32),
        "hgrn_lower_bound": 0.1 * jax.random.normal(ks[9], (DEPTH + 1, B_KEY_WIDTH), jnp.float32),
        "hgrn_norm": gain(ks[10], (DEPTH, B_WIDTH)),
        "w_up_a": nrm(ks[11], (DEPTH, A_WIDTH, D_MODEL), A_WIDTH),
        "w_up_b": nrm(ks[12], (DEPTH, B_WIDTH, D_MODEL), B_WIDTH),
        "w_out": nrm(ks[13], (DEPTH, D_MODEL, D_MODEL), D_MODEL),
        "ffn2_norm": gain(ks[14], (DEPTH, D_MODEL)),
        "ffn2_w_gate": nrm(ks[15], (DEPTH, D_MODEL, FF_DIM), D_MODEL),
        "ffn2_w_up": nrm(ks[16], (DEPTH, D_MODEL, FF_DIM), D_MODEL),
        "ffn2_w_down": nrm(ks[17], (DEPTH, FF_DIM, D_MODEL), FF_DIM),
        "ple_norm": gain(ks[18], (DEPTH, D_MODEL)),
        "ple_w_gate": nrm(ks[19], (DEPTH, D_MODEL, D_MODEL), D_MODEL),
        "ple_w_proj": nrm(ks[20], (DEPTH, PLE_DIM, D_MODEL), PLE_DIM),
        "final_norm": gain(ks[21], (D_MODEL,)),
    }


def _fwd_reference(x, p, positions, ffn1_norm, ffn1_w_gate, ffn1_w_up, ffn1_w_down, mix_norm, w_in,
              attn_sinks, hgrn_lower_bound, hgrn_norm, w_up_a, w_up_b, w_out, ffn2_norm,
              ffn2_w_gate, ffn2_w_up, ffn2_w_down, ple_norm, ple_w_gate, ple_w_proj, final_norm):
    b, s = x.shape[0], x.shape[1]
    offsets = np.cumsum(SPLITS)[:-1].tolist()
    lb_all = jnp.cumsum(jax.nn.softmax(hgrn_lower_bound.astype(jnp.float32), axis=0), axis=0)
    for l in range(DEPTH):
        h = rmsnorm(x, ffn1_norm[l])
        x = x + 0.5 * swiglu(h, ffn1_w_gate[l], ffn1_w_up[l], ffn1_w_down[l])

        h = rmsnorm(x, mix_norm[l])
        proj = h @ w_in[l]
        q_a, k_a, v_a, q_b, f_b, i_b, og_b, gate_a, gate_b = jnp.split(proj, offsets, axis=-1)

        q_a = partial_rope(q_a.reshape(b, s, A_HEADS, A_HEAD_DIM), positions)
        k_a = partial_rope(k_a.reshape(b, s, A_KV_HEADS, A_HEAD_DIM), positions)
        v_a = v_a.reshape(b, s, A_KV_HEADS, A_HEAD_DIM)
        out_a = sliding_window_attention(q_a, k_a, v_a, attn_sinks[l])

        lb = lb_all[l].reshape(B_HEADS, B_KEY_DIM)
        o_b = hgrn2(q_b.reshape(b, s, B_HEADS, B_KEY_DIM), f_b.reshape(b, s, B_HEADS, B_KEY_DIM),
                    i_b.reshape(b, s, B_HEADS, B_VAL_DIM), lb)
        o_b = rmsnorm(o_b, hgrn_norm[l].reshape(B_HEADS, B_VAL_DIM)).astype(x.dtype)
        out_b = (o_b * jax.nn.silu(og_b.reshape(b, s, B_HEADS, B_VAL_DIM))).reshape(b, s, B_WIDTH)

        merged = jax.nn.sigmoid(gate_a) * (out_a @ w_up_a[l]) + jax.nn.sigmoid(gate_b) * (out_b @ w_up_b[l])
        x = x + merged @ w_out[l]

        h = rmsnorm(x, ffn2_norm[l])
        x = x + 0.5 * swiglu(h, ffn2_w_gate[l], ffn2_w_up[l], ffn2_w_down[l])

        g = jax.nn.sigmoid(rmsnorm(x, ple_norm[l]) @ ple_w_gate[l])
        x = x + g * (p[l].astype(x.dtype) @ ple_w_proj[l])
    return rmsnorm(x, final_norm)


import jax as _jax
import jax.numpy as _jnp

TWIN_FORMAT = 'train_step'
FWD_PARAMS = ['x', 'p', 'positions', 'ffn1_norm', 'ffn1_w_gate', 'ffn1_w_up', 'ffn1_w_down', 'mix_norm', 'w_in', 'attn_sinks', 'hgrn_lower_bound', 'hgrn_norm', 'w_up_a', 'w_up_b', 'w_out', 'ffn2_norm', 'ffn2_w_gate', 'ffn2_w_up', 'ffn2_w_down', 'ple_norm', 'ple_w_gate', 'ple_w_proj', 'final_norm']
TWIN_WEIGHTS = ['ffn1_norm', 'ffn1_w_gate', 'ffn1_w_up', 'ffn1_w_down', 'mix_norm', 'w_in', 'attn_sinks', 'hgrn_lower_bound', 'hgrn_norm', 'w_up_a', 'w_up_b', 'w_out', 'ffn2_norm', 'ffn2_w_gate', 'ffn2_w_up', 'ffn2_w_down', 'ple_norm', 'ple_w_gate', 'ple_w_proj', 'final_norm']
TWIN_DIFF_INPUT = 'x'
TWIN_INPUTS = ['x', 'p', 'positions', 'ffn1_norm', 'ffn1_w_gate', 'ffn1_w_up', 'ffn1_w_down', 'mix_norm', 'w_in', 'attn_sinks', 'hgrn_lower_bound', 'hgrn_norm', 'w_up_a', 'w_up_b', 'w_out', 'ffn2_norm', 'ffn2_w_gate', 'ffn2_w_up', 'ffn2_w_down', 'ple_norm', 'ple_w_gate', 'ple_w_proj', 'final_norm', 'loss_target', 'm_ffn1_norm', 'm_ffn1_w_gate', 'm_ffn1_w_up', 'm_ffn1_w_down', 'm_mix_norm', 'm_w_in', 'm_attn_sinks', 'm_hgrn_lower_bound', 'm_hgrn_norm', 'm_w_up_a', 'm_w_up_b', 'm_w_out', 'm_ffn2_norm', 'm_ffn2_w_gate', 'm_ffn2_w_up', 'm_ffn2_w_down', 'm_ple_norm', 'm_ple_w_gate', 'm_ple_w_proj', 'm_final_norm', 'v_ffn1_norm', 'v_ffn1_w_gate', 'v_ffn1_w_up', 'v_ffn1_w_down', 'v_mix_norm', 'v_w_in', 'v_attn_sinks', 'v_hgrn_lower_bound', 'v_hgrn_norm', 'v_w_up_a', 'v_w_up_b', 'v_w_out', 'v_ffn2_norm', 'v_ffn2_w_gate', 'v_ffn2_w_up', 'v_ffn2_w_down', 'v_ple_norm', 'v_ple_w_gate', 'v_ple_w_proj', 'v_final_norm']
TWIN_OUTPUTS = ['loss', 'grad_x', 'grad_ffn1_norm', 'grad_ffn1_w_gate', 'grad_ffn1_w_up', 'grad_ffn1_w_down', 'grad_mix_norm', 'grad_w_in', 'grad_attn_sinks', 'grad_hgrn_lower_bound', 'grad_hgrn_norm', 'grad_w_up_a', 'grad_w_up_b', 'grad_w_out', 'grad_ffn2_norm', 'grad_ffn2_w_gate', 'grad_ffn2_w_up', 'grad_ffn2_w_down', 'grad_ple_norm', 'grad_ple_w_gate', 'grad_ple_w_proj', 'grad_final_norm', 'delta_ffn1_norm', 'delta_ffn1_w_gate', 'delta_ffn1_w_up', 'delta_ffn1_w_down', 'delta_mix_norm', 'delta_w_in', 'delta_attn_sinks', 'delta_hgrn_lower_bound', 'delta_hgrn_norm', 'delta_w_up_a', 'delta_w_up_b', 'delta_w_out', 'delta_ffn2_norm', 'delta_ffn2_w_gate', 'delta_ffn2_w_up', 'delta_ffn2_w_down', 'delta_ple_norm', 'delta_ple_w_gate', 'delta_ple_w_proj', 'delta_final_norm', 'new_m_ffn1_norm', 'new_m_ffn1_w_gate', 'new_m_ffn1_w_up', 'new_m_ffn1_w_down', 'new_m_mix_norm', 'new_m_w_in', 'new_m_attn_sinks', 'new_m_hgrn_lower_bound', 'new_m_hgrn_norm', 'new_m_w_up_a', 'new_m_w_up_b', 'new_m_w_out', 'new_m_ffn2_norm', 'new_m_ffn2_w_gate', 'new_m_ffn2_w_up', 'new_m_ffn2_w_down', 'new_m_ple_norm', 'new_m_ple_w_gate', 'new_m_ple_w_proj', 'new_m_final_norm', 'new_v_ffn1_norm', 'new_v_ffn1_w_gate', 'new_v_ffn1_w_up', 'new_v_ffn1_w_down', 'new_v_mix_norm', 'new_v_w_in', 'new_v_attn_sinks', 'new_v_hgrn_lower_bound', 'new_v_hgrn_norm', 'new_v_w_up_a', 'new_v_w_up_b', 'new_v_w_out', 'new_v_ffn2_norm', 'new_v_ffn2_w_gate', 'new_v_ffn2_w_up', 'new_v_ffn2_w_down', 'new_v_ple_norm', 'new_v_ple_w_gate', 'new_v_ple_w_proj', 'new_v_final_norm']
TWIN_LEAF_KINDS = {'loss': 'loss', 'grad_x': 'grad_x', 'grad_ffn1_norm': 'grad_w', 'grad_ffn1_w_gate': 'grad_w', 'grad_ffn1_w_up': 'grad_w', 'grad_ffn1_w_down': 'grad_w', 'grad_mix_norm': 'grad_w', 'grad_w_in': 'grad_w', 'grad_attn_sinks': 'grad_w', 'grad_hgrn_lower_bound': 'grad_w', 'grad_hgrn_norm': 'grad_w', 'grad_w_up_a': 'grad_w', 'grad_w_up_b': 'grad_w', 'grad_w_out': 'grad_w', 'grad_ffn2_norm': 'grad_w', 'grad_ffn2_w_gate': 'grad_w', 'grad_ffn2_w_up': 'grad_w', 'grad_ffn2_w_down': 'grad_w', 'grad_ple_norm': 'grad_w', 'grad_ple_w_gate': 'grad_w', 'grad_ple_w_proj': 'grad_w', 'grad_final_norm': 'grad_w', 'delta_ffn1_norm': 'delta_w', 'delta_ffn1_w_gate': 'delta_w', 'delta_ffn1_w_up': 'delta_w', 'delta_ffn1_w_down': 'delta_w', 'delta_mix_norm': 'delta_w', 'delta_w_in': 'delta_w', 'delta_attn_sinks': 'delta_w', 'delta_hgrn_lower_bound': 'delta_w', 'delta_hgrn_norm': 'delta_w', 'delta_w_up_a': 'delta_w', 'delta_w_up_b': 'delta_w', 'delta_w_out': 'delta_w', 'delta_ffn2_norm': 'delta_w', 'delta_ffn2_w_gate': 'delta_w', 'delta_ffn2_w_up': 'delta_w', 'delta_ffn2_w_down': 'delta_w', 'delta_ple_norm': 'delta_w', 'delta_ple_w_gate': 'delta_w', 'delta_ple_w_proj': 'delta_w', 'delta_final_norm': 'delta_w', 'new_m_ffn1_norm': 'new_m', 'new_m_ffn1_w_gate': 'new_m', 'new_m_ffn1_w_up': 'new_m', 'new_m_ffn1_w_down': 'new_m', 'new_m_mix_norm': 'new_m', 'new_m_w_in': 'new_m', 'new_m_attn_sinks': 'new_m', 'new_m_hgrn_lower_bound': 'new_m', 'new_m_hgrn_norm': 'new_m', 'new_m_w_up_a': 'new_m', 'new_m_w_up_b': 'new_m', 'new_m_w_out': 'new_m', 'new_m_ffn2_norm': 'new_m', 'new_m_ffn2_w_gate': 'new_m', 'new_m_ffn2_w_up': 'new_m', 'new_m_ffn2_w_down': 'new_m', 'new_m_ple_norm': 'new_m', 'new_m_ple_w_gate': 'new_m', 'new_m_ple_w_proj': 'new_m', 'new_m_final_norm': 'new_m', 'new_v_ffn1_norm': 'new_v', 'new_v_ffn1_w_gate': 'new_v', 'new_v_ffn1_w_up': 'new_v', 'new_v_ffn1_w_down': 'new_v', 'new_v_mix_norm': 'new_v', 'new_v_w_in': 'new_v', 'new_v_attn_sinks': 'new_v', 'new_v_hgrn_lower_bound': 'new_v', 'new_v_hgrn_norm': 'new_v', 'new_v_w_up_a': 'new_v', 'new_v_w_up_b': 'new_v', 'new_v_w_out': 'new_v', 'new_v_ffn2_norm': 'new_v', 'new_v_ffn2_w_gate': 'new_v', 'new_v_ffn2_w_up': 'new_v', 'new_v_ffn2_w_down': 'new_v', 'new_v_ple_norm': 'new_v', 'new_v_ple_w_gate': 'new_v', 'new_v_ple_w_proj': 'new_v', 'new_v_final_norm': 'new_v'}


def _forward(args):
    return _fwd_reference(*[args[k] for k in FWD_PARAMS])


def _output_shape():
    def fwd():
        inp = _fwd_setup_inputs(0)
        return _fwd_reference(*[inp[k] for k in FWD_PARAMS])
    out = _jax.eval_shape(fwd)
    return out.shape, out.dtype

N_MICROBATCH = 1
ADAM_LR = 0.001
ADAM_B1 = 0.9
ADAM_B2 = 0.999
ADAM_EPS = 1e-08
ADAM_WD = 0.01
ADAM_STEP = 10
PER_EXAMPLE_BATCH_AXIS = {'x': 0, 'p': 1, 'positions': 0, 'loss_target': 0}
SHARED_INPUTS = []
_WEIGHT_DTYPES = {'ffn1_norm': _jnp.float32, 'ffn1_w_gate': _jnp.float32, 'ffn1_w_up': _jnp.float32, 'ffn1_w_down': _jnp.float32, 'mix_norm': _jnp.float32, 'w_in': _jnp.float32, 'attn_sinks': _jnp.float32, 'hgrn_lower_bound': _jnp.float32, 'hgrn_norm': _jnp.float32, 'w_up_a': _jnp.float32, 'w_up_b': _jnp.float32, 'w_out': _jnp.float32, 'ffn2_norm': _jnp.float32, 'ffn2_w_gate': _jnp.float32, 'ffn2_w_up': _jnp.float32, 'ffn2_w_down': _jnp.float32, 'ple_norm': _jnp.float32, 'ple_w_gate': _jnp.float32, 'ple_w_proj': _jnp.float32, 'final_norm': _jnp.float32}
MOMENT_SCALE = {'ffn1_norm': 3.922057e-02, 'ffn1_w_gate': 1.592479e-02, 'ffn1_w_up': 1.542149e-02, 'ffn1_w_down': 2.555851e-02, 'mix_norm': 3.925367e-02, 'w_in': 1.821259e-02, 'attn_sinks': 1.276746e-02, 'hgrn_lower_bound': 3.294470e-03, 'hgrn_norm': 3.537554e-02, 'w_up_a': 8.481854e-03, 'w_up_b': 2.501120e-02, 'w_out': 2.621708e-02, 'ffn2_norm': 3.090224e-02, 'ffn2_w_gate': 1.334494e-02, 'ffn2_w_up': 1.291210e-02, 'ffn2_w_down': 2.142161e-02, 'ple_norm': 1.499753e-02, 'ple_w_gate': 1.493668e-02, 'ple_w_proj': 3.850710e-02, 'final_norm': 1.599999e+01}


def _to_microbatches(a, axis):
    t = _jnp.moveaxis(a, axis, 0)
    t = t.reshape((N_MICROBATCH, t.shape[0] // N_MICROBATCH) + t.shape[1:])
    return _jnp.moveaxis(t, 1, axis + 1)


def setup_inputs(seed: int = 0) -> dict:
    inp = _fwd_setup_inputs(seed)
    key = _jax.random.fold_in(_jax.random.key(seed), 7919)
    shape, _ = _output_shape()
    out = dict(inp)
    out["loss_target"] = _jax.random.normal(_jax.random.fold_in(key, 0), shape, _jnp.float32)
    for i, name in enumerate(TWIN_WEIGHTS):
        w = inp[name].astype(_jnp.float32)
        if MOMENT_SCALE is None:
            s = _jnp.sqrt(_jnp.mean(_jnp.square(w)) + 1e-30)
        else:
            s = MOMENT_SCALE[name]
        km, kv = _jax.random.split(_jax.random.fold_in(key, i + 1))
        out[name] = w
        out["m_" + name] = s * _jax.random.normal(km, w.shape, _jnp.float32)
        out["v_" + name] = (s * s) * _jax.random.uniform(kv, w.shape, _jnp.float32, 0.5, 1.5)
    if N_MICROBATCH > 1:
        for name, axis in PER_EXAMPLE_BATCH_AXIS.items():
            out[name] = _to_microbatches(out[name], axis)
    return {'x': out['x'], 'p': out['p'], 'positions': out['positions'], 'ffn1_norm': out['ffn1_norm'], 'ffn1_w_gate': out['ffn1_w_gate'], 'ffn1_w_up': out['ffn1_w_up'], 'ffn1_w_down': out['ffn1_w_down'], 'mix_norm': out['mix_norm'], 'w_in': out['w_in'], 'attn_sinks': out['attn_sinks'], 'hgrn_lower_bound': out['hgrn_lower_bound'], 'hgrn_norm': out['hgrn_norm'], 'w_up_a': out['w_up_a'], 'w_up_b': out['w_up_b'], 'w_out': out['w_out'], 'ffn2_norm': out['ffn2_norm'], 'ffn2_w_gate': out['ffn2_w_gate'], 'ffn2_w_up': out['ffn2_w_up'], 'ffn2_w_down': out['ffn2_w_down'], 'ple_norm': out['ple_norm'], 'ple_w_gate': out['ple_w_gate'], 'ple_w_proj': out['ple_w_proj'], 'final_norm': out['final_norm'], 'loss_target': out['loss_target'], 'm_ffn1_norm': out['m_ffn1_norm'], 'm_ffn1_w_gate': out['m_ffn1_w_gate'], 'm_ffn1_w_up': out['m_ffn1_w_up'], 'm_ffn1_w_down': out['m_ffn1_w_down'], 'm_mix_norm': out['m_mix_norm'], 'm_w_in': out['m_w_in'], 'm_attn_sinks': out['m_attn_sinks'], 'm_hgrn_lower_bound': out['m_hgrn_lower_bound'], 'm_hgrn_norm': out['m_hgrn_norm'], 'm_w_up_a': out['m_w_up_a'], 'm_w_up_b': out['m_w_up_b'], 'm_w_out': out['m_w_out'], 'm_ffn2_norm': out['m_ffn2_norm'], 'm_ffn2_w_gate': out['m_ffn2_w_gate'], 'm_ffn2_w_up': out['m_ffn2_w_up'], 'm_ffn2_w_down': out['m_ffn2_w_down'], 'm_ple_norm': out['m_ple_norm'], 'm_ple_w_gate': out['m_ple_w_gate'], 'm_ple_w_proj': out['m_ple_w_proj'], 'm_final_norm': out['m_final_norm'], 'v_ffn1_norm': out['v_ffn1_norm'], 'v_ffn1_w_gate': out['v_ffn1_w_gate'], 'v_ffn1_w_up': out['v_ffn1_w_up'], 'v_ffn1_w_down': out['v_ffn1_w_down'], 'v_mix_norm': out['v_mix_norm'], 'v_w_in': out['v_w_in'], 'v_attn_sinks': out['v_attn_sinks'], 'v_hgrn_lower_bound': out['v_hgrn_lower_bound'], 'v_hgrn_norm': out['v_hgrn_norm'], 'v_w_up_a': out['v_w_up_a'], 'v_w_up_b': out['v_w_up_b'], 'v_w_out': out['v_w_out'], 'v_ffn2_norm': out['v_ffn2_norm'], 'v_ffn2_w_gate': out['v_ffn2_w_gate'], 'v_ffn2_w_up': out['v_ffn2_w_up'], 'v_ffn2_w_down': out['v_ffn2_w_down'], 'v_ple_norm': out['v_ple_norm'], 'v_ple_w_gate': out['v_ple_w_gate'], 'v_ple_w_proj': out['v_ple_w_proj'], 'v_final_norm': out['v_final_norm']}


def _loss(weights, diff, rest, loss_target):
    with _jax.named_scope("forward"):
        args = {**rest, TWIN_DIFF_INPUT: diff, **{k: w.astype(_WEIGHT_DTYPES[k]) for k, w in weights.items()}}
        y = _forward(args)
    with _jax.named_scope("loss_head"):
        err = _jnp.square(y.astype(_jnp.float32) - loss_target)
        return 0.5 * _jnp.sum(_jnp.mean(err, axis=-1)) if err.ndim else 0.5 * err


def _adamw(w, g, m, v):
    m = ADAM_B1 * m + (1.0 - ADAM_B1) * g
    v = ADAM_B2 * v + (1.0 - ADAM_B2) * _jnp.square(g)
    m_hat = m / (1.0 - ADAM_B1 ** ADAM_STEP)
    v_hat = v / (1.0 - ADAM_B2 ** ADAM_STEP)
    delta = -ADAM_LR * (m_hat / (_jnp.sqrt(v_hat) + ADAM_EPS) + ADAM_WD * w)
    return delta, m, v


def reference(x, p, positions, ffn1_norm, ffn1_w_gate, ffn1_w_up, ffn1_w_down, mix_norm, w_in, attn_sinks, hgrn_lower_bound, hgrn_norm, w_up_a, w_up_b, w_out, ffn2_norm, ffn2_w_gate, ffn2_w_up, ffn2_w_down, ple_norm, ple_w_gate, ple_w_proj, final_norm, loss_target, m_ffn1_norm, m_ffn1_w_gate, m_ffn1_w_up, m_ffn1_w_down, m_mix_norm, m_w_in, m_attn_sinks, m_hgrn_lower_bound, m_hgrn_norm, m_w_up_a, m_w_up_b, m_w_out, m_ffn2_norm, m_ffn2_w_gate, m_ffn2_w_up, m_ffn2_w_down, m_ple_norm, m_ple_w_gate, m_ple_w_proj, m_final_norm, v_ffn1_norm, v_ffn1_w_gate, v_ffn1_w_up, v_ffn1_w_down, v_mix_norm, v_w_in, v_attn_sinks, v_hgrn_lower_bound, v_hgrn_norm, v_w_up_a, v_w_up_b, v_w_out, v_ffn2_norm, v_ffn2_w_gate, v_ffn2_w_up, v_ffn2_w_down, v_ple_norm, v_ple_w_gate, v_ple_w_proj, v_final_norm):
    given = dict(x=x, p=p, positions=positions, ffn1_norm=ffn1_norm, ffn1_w_gate=ffn1_w_gate, ffn1_w_up=ffn1_w_up, ffn1_w_down=ffn1_w_down, mix_norm=mix_norm, w_in=w_in, attn_sinks=attn_sinks, hgrn_lower_bound=hgrn_lower_bound, hgrn_norm=hgrn_norm, w_up_a=w_up_a, w_up_b=w_up_b, w_out=w_out, ffn2_norm=ffn2_norm, ffn2_w_gate=ffn2_w_gate, ffn2_w_up=ffn2_w_up, ffn2_w_down=ffn2_w_down, ple_norm=ple_norm, ple_w_gate=ple_w_gate, ple_w_proj=ple_w_proj, final_norm=final_norm, loss_target=loss_target, m_ffn1_norm=m_ffn1_norm, m_ffn1_w_gate=m_ffn1_w_gate, m_ffn1_w_up=m_ffn1_w_up, m_ffn1_w_down=m_ffn1_w_down, m_mix_norm=m_mix_norm, m_w_in=m_w_in, m_attn_sinks=m_attn_sinks, m_hgrn_lower_bound=m_hgrn_lower_bound, m_hgrn_norm=m_hgrn_norm, m_w_up_a=m_w_up_a, m_w_up_b=m_w_up_b, m_w_out=m_w_out, m_ffn2_norm=m_ffn2_norm, m_ffn2_w_gate=m_ffn2_w_gate, m_ffn2_w_up=m_ffn2_w_up, m_ffn2_w_down=m_ffn2_w_down, m_ple_norm=m_ple_norm, m_ple_w_gate=m_ple_w_gate, m_ple_w_proj=m_ple_w_proj, m_final_norm=m_final_norm, v_ffn1_norm=v_ffn1_norm, v_ffn1_w_gate=v_ffn1_w_gate, v_ffn1_w_up=v_ffn1_w_up, v_ffn1_w_down=v_ffn1_w_down, v_mix_norm=v_mix_norm, v_w_in=v_w_in, v_attn_sinks=v_attn_sinks, v_hgrn_lower_bound=v_hgrn_lower_bound, v_hgrn_norm=v_hgrn_norm, v_w_up_a=v_w_up_a, v_w_up_b=v_w_up_b, v_w_out=v_w_out, v_ffn2_norm=v_ffn2_norm, v_ffn2_w_gate=v_ffn2_w_gate, v_ffn2_w_up=v_ffn2_w_up, v_ffn2_w_down=v_ffn2_w_down, v_ple_norm=v_ple_norm, v_ple_w_gate=v_ple_w_gate, v_ple_w_proj=v_ple_w_proj, v_final_norm=v_final_norm)
    weights = {n: given[n] for n in TWIN_WEIGHTS}
    shared = {n: given[n] for n in SHARED_INPUTS}
    per_example = {n: given[n] for n in ['x', 'p', 'positions']}
    grad_fn = _jax.value_and_grad(_loss, argnums=(0, 1))

    def one_microbatch(ex, loss_target):
        ex = dict(ex)
        diff = ex.pop(TWIN_DIFF_INPUT)
        return grad_fn(weights, diff, {**shared, **ex}, loss_target)

    if N_MICROBATCH == 1:
        loss, (grad_w, grad_x) = one_microbatch(per_example, given["loss_target"])
    else:
        def body(carry, xs):
            loss_sum, grad_sum = carry
            l_k, (gw_k, gx_k) = one_microbatch(xs[0], xs[1])
            with _jax.named_scope("update"):
                return (loss_sum + l_k, _jax.tree.map(_jnp.add, grad_sum, gw_k)), gx_k

        init = (_jnp.zeros((), _jnp.float32), _jax.tree.map(_jnp.zeros_like, weights))
        (loss, grad_w), grad_x = _jax.lax.scan(body, init, (per_example, given["loss_target"]))
    with _jax.named_scope("update"):
        delta_w, new_m, new_v = {}, {}, {}
        for n in TWIN_WEIGHTS:
            delta_w[n], new_m[n], new_v[n] = _adamw(weights[n], grad_w[n], given["m_" + n], given["v_" + n])
    return (loss, grad_x, *[grad_w[n] for n in TWIN_WEIGHTS], *[delta_w[n] for n in TWIN_WEIGHTS],
            *[new_m[n] for n in TWIN_WEIGHTS], *[new_v[n] for n in TWIN_WEIGHTS])
```

```python
import jax
import jax.numpy as jnp
from jax import lax
from jax.experimental import pallas as pl
from jax.experimental.pallas import tpu as pltpu

F32 = jnp.float32
BF16 = jnp.bfloat16
MESH = pl.DeviceIdType.MESH

EPS = 1e-6
N_DEV = 8
A_HEADS = 16
A_HEAD_DIM = 64
A_WIDTH = 1024
A_KV_WIDTH = 256
ATT_BLOCK = 128
ROT_DIM = 16
ROPE_THETA = 500000.0
B_WIDTH = 1024
B_HEAD = 128
CHUNK = 64
SUB = 16
MAX_DECAY_EXP = 60.0
PLE_DIM = 256
LANES = 128

ADAM_LR = 0.001
ADAM_B1 = 0.9
ADAM_B2 = 0.999
ADAM_EPS = 1e-08
ADAM_WD = 0.01
ADAM_STEP = 10

VMEM_LIMIT = 56 * 1024 * 1024

NN = (((1,), (0,)), ((), ()))
NT = (((1,), (1,)), ((), ()))
TN = (((0,), (0,)), ((), ()))

OFF_QA = 0
OFF_KA = 1024
OFF_VA = 1280
OFF_QB = 1536
OFF_FB = 2560
OFF_IB = 3584
OFF_OG = 4608
OFF_GA = 5632


def _sig(v):
    return 1.0 / (1.0 + jnp.exp(-v))


def _tile(n, pref, mult):
    t = min(n, pref)
    t -= t % mult
    while n % t:
        t -= mult
    return t


def _params(sem):
    return pltpu.CompilerParams(dimension_semantics=sem, vmem_limit_bytes=VMEM_LIMIT)


def _mm(name, grid, ins, prods, acc_shapes, epi, outs):
    n_in, n_out, nk = len(ins), len(outs), grid[2]

    def body(*refs):
        in_refs = refs[:n_in]
        out_refs = refs[n_in:n_in + n_out]
        acc_refs = refs[n_in + n_out:]
        first_tile = jnp.logical_and(pl.program_id(0) == 0, pl.program_id(1) == 0)

        def product(lhs, rhs, dims):
            a = in_refs[lhs][...].astype(BF16)
            b = in_refs[rhs][...].astype(BF16)
            return lax.dot_general(a, b, dims, preferred_element_type=F32)

        def write(accs):
            vals = epi(accs, in_refs)
            for o, v, spec in zip(out_refs, vals, outs):
                if spec[4]:
                    @pl.when(first_tile)
                    def _():
                        o[...] = v.astype(o.dtype)

                    @pl.when(jnp.logical_not(first_tile))
                    def _():
                        o[...] += v.astype(o.dtype)
                else:
                    o[...] = v.astype(o.dtype)

        if nk == 1:
            accs = [None] * len(acc_shapes)
            for lhs, rhs, acc, dims in prods:
                t = product(lhs, rhs, dims)
                accs[acc] = t if accs[acc] is None else accs[acc] + t
            write(accs)
        else:
            k = pl.program_id(2)

            @pl.when(k == 0)
            def _():
                for r in acc_refs:
                    r[...] = jnp.zeros(r.shape, F32)

            for lhs, rhs, acc, dims in prods:
                acc_refs[acc][...] += product(lhs, rhs, dims)

            @pl.when(k == nk - 1)
            def _():
                write([r[...] for r in acc_refs])

    res = pl.pallas_call(
        body,
        name=name,
        grid=grid,
        in_specs=[pl.BlockSpec(b, im) for _, b, im in ins],
        out_specs=[pl.BlockSpec(o[2], o[3]) for o in outs],
        out_shape=[jax.ShapeDtypeStruct(o[0], o[1]) for o in outs],
        scratch_shapes=[] if nk == 1 else [pltpu.VMEM(s, F32) for s in acc_shapes],
        compiler_params=_params(("arbitrary", "arbitrary", "arbitrary")),
    )(*[a for a, _, _ in ins])
    return res


def _rms_bwd(dh, xv, w, dres):
    r = lax.rsqrt(jnp.mean(xv * xv, axis=-1, keepdims=True) + EPS)
    g = dh * w
    dx = dres + r * (g - xv * (r * r) * jnp.mean(g * xv, axis=-1, keepdims=True))
    dw = jnp.sum(dh * xv * r, axis=0, keepdims=True)
    return dx, dw


def _rows(name, fn, ins, outs, tm):
    n_in = len(ins)
    rows = outs[0][0][0]

    def body(*refs):
        vals = fn(*[r[...] for r in refs[:n_in]])
        for o, v in zip(refs[n_in:], vals):
            o[...] = v.astype(o.dtype)

    in_specs = []
    for a, tiled in ins:
        if tiled:
            in_specs.append(pl.BlockSpec((tm, a.shape[1]), lambda i: (i, 0)))
        else:
            in_specs.append(pl.BlockSpec(a.shape, lambda i, nd=a.ndim: (0,) * nd))
    return pl.pallas_call(
        body,
        name=name,
        grid=(rows // tm,),
        in_specs=in_specs,
        out_specs=[pl.BlockSpec((tm, s[1]), lambda i: (i, 0)) for s, _ in outs],
        out_shape=[jax.ShapeDtypeStruct(s, d) for s, d in outs],
        compiler_params=_params(("arbitrary",)),
    )(*[a for a, _ in ins])


def _rmsnorm(name, x, w):
    def fn(xv, wv):
        r = lax.rsqrt(jnp.mean(xv * xv, axis=-1, keepdims=True) + EPS)
        return (xv * r * wv,)

    return _rows(name, fn, [(x, True), (w, False)], [(x.shape, BF16)], _tile(x.shape[0], 512, 16))[0]


def _ffn_fwd(name, x, h, wg, wu, wd):
    t, d = x.shape
    n = wg.shape[2]
    tm = _tile(t, 1024, 16)

    def epi_up(accs, refs):
        g, u = accs
        return [g, u, g * _sig(g) * u]

    hid = ((N_DEV, t, n), BF16, (None, tm, n), lambda i, j, k: (j, i, 0), False)
    gate, up, act = _mm(
        name + "_up", (t // tm, N_DEV, 1),
        [(h, (tm, d), lambda i, j, k: (i, 0)),
         (wg, (None, d, n), lambda i, j, k: (j, 0, 0)),
         (wu, (None, d, n), lambda i, j, k: (j, 0, 0))],
        [(0, 1, 0, NN), (0, 2, 1, NN)], [(tm, n)] * 2, epi_up, [hid] * 3)

    tm2 = _tile(t, 512, 8)

    def epi_down(accs, refs):
        return [refs[2][...] + 0.5 * accs[0]]

    (x_out,) = _mm(
        name + "_down", (t // tm2, 1, N_DEV),
        [(act, (None, tm2, n), lambda i, j, k: (k, i, 0)),
         (wd, (None, n, d), lambda i, j, k: (k, 0, 0)),
         (x, (tm2, d), lambda i, j, k: (i, 0))],
        [(0, 1, 0, NN)], [(tm2, d)], epi_down,
        [((t, d), F32, (tm2, d), lambda i, j, k: (i, 0), False)])
    return x_out, gate, up, act


def _ffn_bwd(name, dxo, x, h, gate, up, act, wg, wu, wd, norm_w):
    t, d = x.shape
    n = wg.shape[2]
    tm = _tile(t, 512, 16)

    def epi_act(accs, refs):
        da = 0.5 * accs[0]
        g = refs[2][...].astype(F32)
        u = refs[3][...].astype(F32)
        s = _sig(g)
        return [da * u * (s * (1.0 + g * (1.0 - s))), da * g * s]

    hid_in = lambda a: (a, (None, tm, n), lambda i, j, k: (j, i, 0))
    hid_out = ((N_DEV, t, n), BF16, (None, tm, n), lambda i, j, k: (j, i, 0), False)
    dgate, dup = _mm(
        name + "_dact", (t // tm, N_DEV, 1),
        [(dxo, (tm, d), lambda i, j, k: (i, 0)),
         (wd, (None, n, d), lambda i, j, k: (j, 0, 0)),
         hid_in(gate), hid_in(up)],
        [(0, 1, 0, NT)], [(tm, n)], epi_act, [hid_out] * 2)

    tk = _tile(t, 512, 16)
    (dwd,) = _mm(
        name + "_dwd", (N_DEV, 1, t // tk),
        [(act, (None, tk, n), lambda i, j, k: (i, k, 0)),
         (dxo, (tk, d), lambda i, j, k: (k, 0))],
        [(0, 1, 0, TN)], [(n, d)], lambda accs, refs: [0.5 * accs[0]],
        [((N_DEV, n, d), BF16, (None, n, d), lambda i, j, k: (i, 0, 0), False)])

    td = _tile(d, 1024, 128)
    w_out = ((N_DEV, d, n), BF16, (None, td, n), lambda i, j, k: (j, i, 0), False)
    dwg, dwu = _mm(
        name + "_dwgu", (d // td, N_DEV, t // tk),
        [(h, (tk, td), lambda i, j, k: (k, i)),
         (dgate, (None, tk, n), lambda i, j, k: (j, k, 0)),
         (dup, (None, tk, n), lambda i, j, k: (j, k, 0))],
        [(0, 1, 0, TN), (0, 2, 1, TN)], [(td, n)] * 2, lambda accs, refs: accs, [w_out] * 2)

    tm3 = _tile(t, 256, 8)

    def epi_dx(accs, refs):
        dx, dw = _rms_bwd(accs[0], refs[4][...], refs[5][...], refs[6][...])
        return [dx, dw]

    dx, dnorm = _mm(
        name + "_dx", (t // tm3, 1, N_DEV),
        [(dgate, (None, tm3, n), lambda i, j, k: (k, i, 0)),
         (wg, (None, d, n), lambda i, j, k: (k, 0, 0)),
         (dup, (None, tm3, n), lambda i, j, k: (k, i, 0)),
         (wu, (None, d, n), lambda i, j, k: (k, 0, 0)),
         (x, (tm3, d), lambda i, j, k: (i, 0)),
         (norm_w, (1, d), lambda i, j, k: (0, 0)),
         (dxo, (tm3, d), lambda i, j, k: (i, 0))],
        [(0, 1, 0, NT), (2, 3, 0, NT)], [(tm3, d)], epi_dx,
        [((t, d), F32, (tm3, d), lambda i, j, k: (i, 0), False),
         ((1, d), F32, (1, d), lambda i, j, k: (0, 0), True)])
    return dx, dwg, dwu, dwd, dnorm


def _rope(tv, cos, s1, s2):
    return tv * cos + pltpu.roll(tv, LANES - 8, 1) * s1 + pltpu.roll(tv, 8, 1) * s2


def _rope_bwd(dr, cos, s1, s2):
    return dr * cos + pltpu.roll(dr * s1, 8, 1) + pltpu.roll(dr * s2, LANES - 8, 1)


def _rope_tables(positions):
    inv_freq = jnp.power(jnp.float32(ROPE_THETA), -jnp.arange(0, ROT_DIM, 2, dtype=F32) / ROT_DIM)
    ang = positions.reshape(-1).astype(F32)[:, None] * inv_freq
    cos, sin = jnp.cos(ang), jnp.sin(ang)
    t = ang.shape[0]
    pad = A_HEAD_DIM - ROT_DIM
    cos_t = jnp.concatenate([cos, cos, jnp.ones((t, pad), F32)], axis=1)
    s1_t = jnp.concatenate([-sin, jnp.zeros((t, pad + 8), F32)], axis=1)
    s2_t = jnp.concatenate([jnp.zeros((t, 8), F32), sin, jnp.zeros((t, pad), F32)], axis=1)
    return tuple(jnp.tile(v, (1, 2)) for v in (cos_t, s1_t, s2_t))


def _att_common(i, k_p, k_c, v_p, v_c, tabs_p, tabs_c):
    kcat = jnp.concatenate([k_p, k_c], axis=0)
    vcat = jnp.concatenate([v_p, v_c], axis=0)
    tabs_k = [jnp.concatenate([a, b], axis=0) for a, b in zip(tabs_p, tabs_c)]
    kvar, vvar = [], []
    for ks in range(2):
        sl = slice(ks * LANES, (ks + 1) * LANES)
        kr = _rope(kcat[:, sl], *tabs_k)
        kvar.append([kr.astype(BF16), pltpu.roll(kr, 64, 1).astype(BF16)])
        vvar.append([vcat[:, sl].astype(BF16), pltpu.roll(vcat[:, sl], 64, 1).astype(BF16)])
    qi = lax.broadcasted_iota(jnp.int32, (ATT_BLOCK, 2 * ATT_BLOCK), 0)
    kj = lax.broadcasted_iota(jnp.int32, (ATT_BLOCK, 2 * ATT_BLOCK), 1)
    dist = qi + ATT_BLOCK - kj
    allowed = (dist >= 0) & (dist < ATT_BLOCK) & ((i > 0) | (kj >= ATT_BLOCK))
    lane = lax.broadcasted_iota(jnp.int32, (1, LANES), 1)
    halves = [lane < 64, lane >= 64]
    return kvar, vvar, allowed, halves


def _att_head_probs(qm, kslab, allowed, sink):
    s = lax.dot_general(qm, kslab, NT, preferred_element_type=F32)
    s = jnp.where(allowed, s, -jnp.inf)
    m = jnp.maximum(jnp.max(s, axis=-1, keepdims=True), sink)
    e = jnp.exp(s - m)
    es = jnp.exp(sink - m)
    den = jnp.sum(e, axis=-1, keepdims=True) + es
    return e / den, es / den


def _head_slots(h):
    pair, half = h // 2, h % 2
    kvh = h // 4
    return pair, half, kvh // 2, int(kvh % 2 != half)


def _attention_fwd(proj, tables, sinks):
    t = proj.shape[0]
    nb = t // ATT_BLOCK
    b = ATT_BLOCK
    scale = A_HEAD_DIM ** -0.5

    def body(sink_ref, q_ref, kp_ref, kc_ref, vp_ref, vc_ref,
             cp_ref, ap_ref, bp_ref, cc_ref, ac_ref, bc_ref, o_ref):
        i = pl.program_id(0)
        tabs_p = (cp_ref[...], ap_ref[...], bp_ref[...])
        tabs_c = (cc_ref[...], ac_ref[...], bc_ref[...])
        kvar, vvar, allowed, halves = _att_common(
            i, kp_ref[...], kc_ref[...], vp_ref[...], vc_ref[...], tabs_p, tabs_c)
        for pair in range(A_HEADS // 2):
            sl = slice(pair * LANES, (pair + 1) * LANES)
            q2 = _rope(q_ref[:, sl], *tabs_c) * scale
            acc = jnp.zeros((b, LANES), F32)
            for half in range(2):
                h = 2 * pair + half
                _, _, ks, var = _head_slots(h)
                qm = jnp.where(halves[half], q2, 0.0).astype(BF16)
                p, _ = _att_head_probs(qm, kvar[ks][var], allowed, sink_ref[h])
                o = lax.dot_general(p.astype(BF16), vvar[ks][var], NN, preferred_element_type=F32)
                acc = acc + jnp.where(halves[half], o, 0.0)
            o_ref[:, sl] = acc.astype(o_ref.dtype)

    cur = lambda col: (lambda i: (i, col))
    prev = lambda col: (lambda i: (jnp.maximum(i - 1, 0), col))
    tab = [pl.BlockSpec((b, LANES), prev(0))] * 3 + [pl.BlockSpec((b, LANES), cur(0))] * 3
    return pl.pallas_call(
        body,
        name="attention_fwd",
        grid=(nb,),
        in_specs=[pl.BlockSpec(memory_space=pltpu.SMEM),
                  pl.BlockSpec((b, A_WIDTH), cur(0)),
                  pl.BlockSpec((b, A_KV_WIDTH), prev(OFF_KA // A_KV_WIDTH)),
                  pl.BlockSpec((b, A_KV_WIDTH), cur(OFF_KA // A_KV_WIDTH)),
                  pl.BlockSpec((b, A_KV_WIDTH), prev(OFF_VA // A_KV_WIDTH)),
                  pl.BlockSpec((b, A_KV_WIDTH), cur(OFF_VA // A_KV_WIDTH))] + tab,
        out_specs=pl.BlockSpec((b, A_WIDTH), cur(0)),
        out_shape=jax.ShapeDtypeStruct((t, A_WIDTH), BF16),
        compiler_params=_params(("arbitrary",)),
    )(sinks, proj, proj, proj, proj, proj, *tables, *tables)


def _attention_bwd(proj, tables, sinks, d_out):
    t = proj.shape[0]
    nb = t // ATT_BLOCK
    b = ATT_BLOCK
    scale = A_HEAD_DIM ** -0.5

    def body(sink_ref, q_ref, kp_ref, kc_ref, vp_ref, vc_ref,
             cp_ref, ap_ref, bp_ref, cc_ref, ac_ref, bc_ref, do_ref,
             dq_ref, dk_ref, dv_ref, dsink_ref, dk_carry, dv_carry):
        step = pl.program_id(0)
        i = nb - 1 - step

        @pl.when(step == 0)
        def _():
            dk_carry[...] = jnp.zeros(dk_carry.shape, F32)
            dv_carry[...] = jnp.zeros(dv_carry.shape, F32)
            dsink_ref[...] = jnp.zeros(dsink_ref.shape, F32)

        tabs_p = (cp_ref[...], ap_ref[...], bp_ref[...])
        tabs_c = (cc_ref[...], ac_ref[...], bc_ref[...])
        kvar, vvar, allowed, halves = _att_common(
            i, kp_ref[...], kc_ref[...], vp_ref[...], vc_ref[...], tabs_p, tabs_c)
        lane = lax.broadcasted_iota(jnp.int32, (1, LANES), 1)
        dk_slab = [jnp.zeros((2 * b, LANES), F32) for _ in range(2)]
        dv_slab = [jnp.zeros((2 * b, LANES), F32) for _ in range(2)]
        dsink = jnp.zeros((1, LANES), F32)
        for pair in range(A_HEADS // 2):
            sl = slice(pair * LANES, (pair + 1) * LANES)
            q2 = _rope(q_ref[:, sl], *tabs_c) * scale
            do2 = do_ref[:, sl].astype(F32)
            dq2 = jnp.zeros((b, LANES), F32)
            for half in range(2):
                h = 2 * pair + half
                _, _, ks, var = _head_slots(h)
                qm = jnp.where(halves[half], q2, 0.0).astype(BF16)
                dom = jnp.where(halves[half], do2, 0.0).astype(BF16)
                p, p_sink = _att_head_probs(qm, kvar[ks][var], allowed, sink_ref[h])
                dp = lax.dot_general(dom, vvar[ks][var], NT, preferred_element_type=F32)
                dsum = jnp.sum(p * dp, axis=-1, keepdims=True)
                ds = (p * (dp - dsum)).astype(BF16)
                dsink = dsink + jnp.where(lane == h, -jnp.sum(p_sink * dsum), 0.0)
                dqh = lax.dot_general(ds, kvar[ks][var], NN, preferred_element_type=F32)
                dq2 = dq2 + jnp.where(halves[half], dqh, 0.0)
                dkx = lax.dot_general(ds, qm, TN, preferred_element_type=F32)
                dvx = lax.dot_general(p.astype(BF16), dom, TN, preferred_element_type=F32)
                if var:
                    dkx = pltpu.roll(dkx, 64, 1)
                    dvx = pltpu.roll(dvx, 64, 1)
                dk_slab[ks] = dk_slab[ks] + dkx
                dv_slab[ks] = dv_slab[ks] + dvx
            dq_ref[:, sl] = _rope_bwd(dq2 * scale, *tabs_c).astype(dq_ref.dtype)
        for ks in range(2):
            sl = slice(ks * LANES, (ks + 1) * LANES)
            dk_cur = dk_slab[ks][b:] + dk_carry[:, sl]
            dk_ref[:, sl] = _rope_bwd(dk_cur, *tabs_c).astype(dk_ref.dtype)
            dv_ref[:, sl] = (dv_slab[ks][b:] + dv_carry[:, sl]).astype(dv_ref.dtype)
            dk_carry[:, sl] = dk_slab[ks][:b]
            dv_carry[:, sl] = dv_slab[ks][:b]
        dsink_ref[...] += dsink

    cur = lambda col: (lambda s: (nb - 1 - s, col))
    prev = lambda col: (lambda s: (jnp.maximum(nb - 2 - s, 0), col))
    tab = [pl.BlockSpec((b, LANES), prev(0))] * 3 + [pl.BlockSpec((b, LANES), cur(0))] * 3
    return pl.pallas_call(
        body,
        name="attention_bwd",
        grid=(nb,),
        in_specs=[pl.BlockSpec(memory_space=pltpu.SMEM),
                  pl.BlockSpec((b, A_WIDTH), cur(0)),
                  pl.BlockSpec((b, A_KV_WIDTH), prev(OFF_KA // A_KV_WIDTH)),
                  pl.BlockSpec((b, A_KV_WIDTH), cur(OFF_KA // A_KV_WIDTH)),
                  pl.BlockSpec((b, A_KV_WIDTH), prev(OFF_VA // A_KV_WIDTH)),
                  pl.BlockSpec((b, A_KV_WIDTH), cur(OFF_VA // A_KV_WIDTH))] + tab
                 + [pl.BlockSpec((b, A_WIDTH), cur(0))],
        out_specs=[pl.BlockSpec((b, A_WIDTH), cur(0)),
                   pl.BlockSpec((b, A_KV_WIDTH), cur(0)),
                   pl.BlockSpec((b, A_KV_WIDTH), cur(0)),
                   pl.BlockSpec((1, LANES), lambda s: (0, 0))],
        out_shape=[jax.ShapeDtypeStruct((t, A_WIDTH), BF16),
                   jax.ShapeDtypeStruct((t, A_KV_WIDTH), BF16),
                   jax.ShapeDtypeStruct((t, A_KV_WIDTH), BF16),
                   jax.ShapeDtypeStruct((1, LANES), F32)],
        scratch_shapes=[pltpu.VMEM((b, A_KV_WIDTH), F32), pltpu.VMEM((b, A_KV_WIDTH), F32)],
        compiler_params=_params(("arbitrary",)),
    )(sinks, proj, proj, proj, proj, proj, *tables, *tables, d_out)


HGRN_HEADS_PER_STEP = 4
HGRN_W = HGRN_HEADS_PER_STEP * B_HEAD


def _hgrn_gates(qp, fp, hlb):
    lb = _sig(hlb[0:1] - hlb[1:2])
    sg = _sig(fp)
    big_f = lb + (1.0 - lb) * sg
    kk = (1.0 - lb) * (1.0 - sg)
    sq = _sig(qp)
    return lb, sg, big_f, kk, sq


def _tri(lower):
    r = lax.broadcasted_iota(jnp.int32, (CHUNK, CHUNK), 0)
    c = lax.broadcasted_iota(jnp.int32, (CHUNK, CHUNK), 1)
    return (r >= c) if lower else (r <= c)


def _hgrn_factors(q_ref, k_ref, b_ref, sl):
    out = []
    bh = b_ref[:, sl]
    for i in range(CHUNK // SUB):
        rows = slice(i * SUB, (i + 1) * SUB)
        b0 = b_ref[i * SUB - 1:i * SUB, sl] if i else jnp.zeros((1, B_HEAD), F32)
        eq = jnp.exp(b_ref[rows, sl] - b0)
        ek = jnp.exp(jnp.minimum(b0 - bh, MAX_DECAY_EXP))
        out.append((q_ref[rows, sl] * eq, k_ref[:, sl] * ek, eq, ek))
    return out


def _hgrn_scores(factors):
    rows = [lax.dot_general(qd.astype(BF16), kx.astype(BF16), NT, preferred_element_type=F32)
            for qd, kx, _, _ in factors]
    return jnp.where(_tri(True), jnp.concatenate(rows, axis=0), 0.0)


def _hgrn_specs(nc, reverse):
    w = HGRN_W
    chunk = (lambda c: nc - 1 - c) if reverse else (lambda c: c)
    col = lambda off: pl.BlockSpec((CHUNK, w), lambda hh, c: (chunk(c), off // w + hh))
    return chunk, col


def _hgrn_fwd(proj, hlb, wn):
    t = proj.shape[0]
    nc = t // CHUNK
    w = HGRN_W
    chunk, col = _hgrn_specs(nc, False)

    def body(q_ref, f_ref, v_ref, og_ref, hlb_ref, wn_ref, ob_ref, oraw_ref, st_ref,
             state, qs, ks, bs):
        c = pl.program_id(1)

        @pl.when(c == 0)
        def _():
            state[...] = jnp.zeros(state.shape, F32)

        qp = q_ref[...]
        lb, sg, big_f, kk, sq = _hgrn_gates(qp, f_ref[...], hlb_ref[...])
        qs[...] = qp * sq
        ks[...] = kk
        bs[...] = lax.dot_general(_tri(True).astype(F32), jnp.log(big_f), NN,
                                  precision=lax.Precision.HIGHEST, preferred_element_type=F32)
        for h in range(HGRN_HEADS_PER_STEP):
            sl = slice(h * B_HEAD, (h + 1) * B_HEAD)
            bh = bs[:, sl]
            qh = qs[:, sl]
            kh = ks[:, sl]
            vh = v_ref[:, sl].astype(BF16)
            s0 = state[h]
            st_ref[h] = s0.astype(st_ref.dtype)
            a = _hgrn_scores(_hgrn_factors(qs, ks, bs, sl))
            o = lax.dot_general(a.astype(BF16), vh, NN, preferred_element_type=F32)
            o = o + lax.dot_general((qh * jnp.exp(bh)).astype(BF16), s0.astype(BF16), NT,
                                    preferred_element_type=F32)
            bl = bs[CHUNK - 1:CHUNK, sl]
            khat = (kh * jnp.exp(bl - bh)).astype(BF16)
            state[h] = s0 * jnp.exp(bl) + lax.dot_general(vh, khat, TN, preferred_element_type=F32)
            oraw_ref[:, sl] = o
            r = lax.rsqrt(jnp.mean(o * o, axis=-1, keepdims=True) + EPS)
            og = og_ref[:, sl]
            ob_ref[:, sl] = (o * r * wn_ref[:, sl] * (og * _sig(og))).astype(ob_ref.dtype)

    vec = lambda rows: pl.BlockSpec((rows, w), lambda hh, c: (0, hh))
    return pl.pallas_call(
        body,
        name="hgrn_fwd",
        grid=(B_WIDTH // w, nc),
        in_specs=[col(OFF_QB), col(OFF_FB), col(OFF_IB), col(OFF_OG), vec(2), vec(1)],
        out_specs=[pl.BlockSpec((CHUNK, w), lambda hh, c: (c, hh)),
                   pl.BlockSpec((CHUNK, w), lambda hh, c: (c, hh)),
                   pl.BlockSpec((None, HGRN_HEADS_PER_STEP, B_HEAD, B_HEAD), lambda hh, c: (c, hh, 0, 0))],
        out_shape=[jax.ShapeDtypeStruct((t, B_WIDTH), BF16),
                   jax.ShapeDtypeStruct((t, B_WIDTH), F32),
                   jax.ShapeDtypeStruct((nc, B_WIDTH // B_HEAD, B_HEAD, B_HEAD), BF16)],
        scratch_shapes=[pltpu.VMEM((HGRN_HEADS_PER_STEP, B_HEAD, B_HEAD), F32),
                        pltpu.VMEM((CHUNK, w), F32), pltpu.VMEM((CHUNK, w), F32),
                        pltpu.VMEM((CHUNK, w), F32)],
        compiler_params=_params(("arbitrary", "arbitrary")),
    )(proj, proj, proj, proj, hlb, wn)


def _hgrn_bwd(proj, hlb, wn, o_raw, states, d_out):
    t = proj.shape[0]
    nc = t // CHUNK
    w = HGRN_W
    chunk, col = _hgrn_specs(nc, True)

    def body(q_ref, f_ref, v_ref, og_ref, hlb_ref, wn_ref, oraw_ref, st_ref, dout_ref,
             dq_ref, df_ref, dv_ref, dog_ref, dlb_ref, dwn_ref,
             dstate, qs, ks, bs, dbs, dks, dqs):
        c = pl.program_id(1)

        @pl.when(c == 0)
        def _():
            dstate[...] = jnp.zeros(dstate.shape, F32)
            dlb_ref[...] = jnp.zeros(dlb_ref.shape, F32)
            dwn_ref[...] = jnp.zeros(dwn_ref.shape, F32)

        qp = q_ref[...]
        lb, sg, big_f, kk, sq = _hgrn_gates(qp, f_ref[...], hlb_ref[...])
        qs[...] = qp * sq
        ks[...] = kk
        bs[...] = lax.dot_general(_tri(True).astype(F32), jnp.log(big_f), NN,
                                  precision=lax.Precision.HIGHEST, preferred_element_type=F32)
        last_row = lax.broadcasted_iota(jnp.int32, (CHUNK, 1), 0) == CHUNK - 1
        for h in range(HGRN_HEADS_PER_STEP):
            sl = slice(h * B_HEAD, (h + 1) * B_HEAD)
            bh = bs[:, sl]
            qh = qs[:, sl]
            kh = ks[:, sl]
            vh = v_ref[:, sl].astype(BF16)
            s0 = st_ref[h]
            ds1 = dstate[h]
            o = oraw_ref[:, sl]
            og = og_ref[:, sl]
            wnh = wn_ref[:, sl]
            sog = _sig(og)
            r = lax.rsqrt(jnp.mean(o * o, axis=-1, keepdims=True) + EPS)
            dob = dout_ref[:, sl].astype(F32)
            don = dob * (og * sog)
            dog_ref[:, sl] = (dob * (o * r * wnh) * (sog * (1.0 + og * (1.0 - sog)))).astype(dog_ref.dtype)
            dwn_ref[:, sl] += jnp.sum(don * o * r, axis=0, keepdims=True)
            gp = don * wnh
            do = r * (gp - o * (r * r) * jnp.mean(gp * o, axis=-1, keepdims=True))
            do16 = do.astype(BF16)
            factors = _hgrn_factors(qs, ks, bs, sl)
            a = _hgrn_scores(factors)
            da = jnp.where(_tri(True), lax.dot_general(do16, vh, NT, preferred_element_type=F32), 0.0)
            dv = lax.dot_general(a.astype(BF16), do16, TN, preferred_element_type=F32)
            dq_rows = []
            dk = jnp.zeros((CHUNK, B_HEAD), F32)
            for i, (qd, kx, eq, ek) in enumerate(factors):
                da_i = da[i * SUB:(i + 1) * SUB]
                dq_rows.append(lax.dot_general(da_i, kx, NN, precision=lax.Precision.HIGHEST,
                                               preferred_element_type=F32) * eq)
                dk = dk + lax.dot_general(da_i, qd, TN, precision=lax.Precision.HIGHEST,
                                          preferred_element_type=F32) * ek
            eb = jnp.exp(bh)
            bl = bs[CHUNK - 1:CHUNK, sl]
            ebl = jnp.exp(bl)
            ekl = jnp.exp(bl - bh)
            qe = (qh * eb).astype(BF16)
            khat = (kh * ekl).astype(BF16)
            ds1_16 = ds1.astype(BF16)
            dq = jnp.concatenate(dq_rows, axis=0) + eb * lax.dot_general(
                do16, s0, NN, preferred_element_type=F32)
            dv = dv + lax.dot_general(khat, ds1_16, NT, preferred_element_type=F32)
            dk_state = ekl * lax.dot_general(vh, ds1_16, NN, preferred_element_type=F32)
            dk = dk + dk_state
            db_last = (jnp.sum(kh * dk_state, axis=0, keepdims=True)
                       + ebl * jnp.sum(s0.astype(F32) * ds1, axis=0, keepdims=True))
            dbs[:, sl] = qh * dq - kh * dk + jnp.where(last_row, db_last, 0.0)
            dks[:, sl] = dk
            dqs[:, sl] = dq
            dstate[h] = ds1 * ebl + lax.dot_general(do16, qe, TN, preferred_element_type=F32)
            dv_ref[:, sl] = dv.astype(dv_ref.dtype)
        dg = lax.dot_general(_tri(False).astype(F32), dbs[...], NN,
                             precision=lax.Precision.HIGHEST, preferred_element_type=F32)
        dq_ref[...] = (dqs[...] * (sq * (1.0 + qp * (1.0 - sq)))).astype(dq_ref.dtype)
        df_minus_dk = dg / big_f - dks[...]
        df_ref[...] = ((1.0 - lb) * df_minus_dk * sg * (1.0 - sg)).astype(df_ref.dtype)
        dlb_ref[...] += jnp.sum((1.0 - sg) * df_minus_dk, axis=0, keepdims=True) * (lb * (1.0 - lb))

    vec = lambda rows: pl.BlockSpec((rows, w), lambda hh, c: (0, hh))
    blk = pl.BlockSpec((CHUNK, w), lambda hh, c: (chunk(c), hh))
    act = jax.ShapeDtypeStruct((t, B_WIDTH), BF16)
    row = jax.ShapeDtypeStruct((1, B_WIDTH), F32)
    return pl.pallas_call(
        body,
        name="hgrn_bwd",
        grid=(B_WIDTH // w, nc),
        in_specs=[col(OFF_QB), col(OFF_FB), col(OFF_IB), col(OFF_OG), vec(2), vec(1), blk,
                  pl.BlockSpec((None, HGRN_HEADS_PER_STEP, B_HEAD, B_HEAD),
                               lambda hh, c: (chunk(c), hh, 0, 0)),
                  blk],
        out_specs=[blk, blk, blk, blk, vec(1), vec(1)],
        out_shape=[act, act, act, act, row, row],
        scratch_shapes=[pltpu.VMEM((HGRN_HEADS_PER_STEP, B_HEAD, B_HEAD), F32)]
                       + [pltpu.VMEM((CHUNK, w), F32)] * 6,
        compiler_params=_params(("arbitrary", "arbitrary")),
    )(proj, proj, proj, proj, hlb, wn, o_raw, states, d_out)


def _mixer_fwd(x1, mix_norm, w_in, sinks, hlb, wn, wua, wub, w_out, tables):
    t, d = x1.shape
    n_in = w_in.shape[1]
    nb = d // N_DEV
    h2 = _rmsnorm("mix_norm", x1, mix_norm)
    tm = _tile(t, 1024, 16)
    tn = _tile(n_in, 512, 128)
    (proj,) = _mm(
        "mix_proj", (t // tm, n_in // tn, 1),
        [(h2, (tm, d), lambda i, j, k: (i, 0)), (w_in, (d, tn), lambda i, j, k: (0, j))],
        [(0, 1, 0, NN)], [(tm, tn)], lambda accs, refs: accs,
        [((t, n_in), F32, (tm, tn), lambda i, j, k: (i, j), False)])
    out_a = _attention_fwd(proj, tables, sinks)
    out_b, o_raw, states = _hgrn_fwd(proj, hlb, wn)

    def epi_merge(accs, refs):
        ua, ub = accs
        return [_sig(refs[4][...]) * ua + _sig(refs[5][...]) * ub, ua, ub]

    wide = ((t, d), BF16, (tm, nb), lambda i, j, k: (i, j), False)
    merged, ua, ub = _mm(
        "mix_merge", (t // tm, N_DEV, 1),
        [(out_a, (tm, A_WIDTH), lambda i, j, k: (i, 0)),
         (wua, (None, A_WIDTH, nb), lambda i, j, k: (j, 0, 0)),
         (out_b, (tm, B_WIDTH), lambda i, j, k: (i, 0)),
         (wub, (None, B_WIDTH, nb), lambda i, j, k: (j, 0, 0)),
         (proj, (tm, nb), lambda i, j, k: (i, OFF_GA // nb + j)),
         (proj, (tm, nb), lambda i, j, k: (i, (OFF_GA + d) // nb + j))],
        [(0, 1, 0, NN), (2, 3, 1, NN)], [(tm, nb)] * 2, epi_merge, [wide] * 3)

    tm2 = _tile(t, 512, 8)
    (x2,) = _mm(
        "mix_out", (t // tm2, 1, 1),
        [(merged, (tm2, d), lambda i, j, k: (i, 0)), (w_out, (d, d), lambda i, j, k: (0, 0)),
         (x1, (tm2, d), lambda i, j, k: (i, 0))],
        [(0, 1, 0, NN)], [(tm2, d)], lambda accs, refs: [refs[2][...] + accs[0]],
        [((t, d), F32, (tm2, d), lambda i, j, k: (i, 0), False)])
    saved = dict(h2=h2, proj=proj, out_a=out_a, out_b=out_b, o_raw=o_raw, states=states,
                 merged=merged, ua=ua, ub=ub)
    return x2, saved


def _mixer_bwd(dx2, x1, mix_norm, w_in, sinks, hlb, wn, wua, wub, w_out, tables, sv):
    t, d = x1.shape
    n_in = w_in.shape[1]
    nb = d // N_DEV
    proj = sv["proj"]
    tm = _tile(t, 512, 16)
    tn = _tile(d, 512, 128)

    def epi_gate(accs, refs):
        dm = accs[0]
        sa = _sig(refs[2][...])
        sb = _sig(refs[3][...])
        ua = refs[4][...].astype(F32)
        ub = refs[5][...].astype(F32)
        return [dm * sa, dm * sb, dm * ua * sa * (1.0 - sa), dm * ub * sb * (1.0 - sb)]

    wide = ((t, d), BF16, (tm, tn), lambda i, j, k: (i, j), False)
    dua, dub, dga, dgb = _mm(
        "mix_dmerge", (t // tm, d // tn, 1),
        [(dx2, (tm, d), lambda i, j, k: (i, 0)),
         (w_out, (tn, d), lambda i, j, k: (j, 0)),
         (proj, (tm, tn), lambda i, j, k: (i, OFF_GA // tn + j)),
         (proj, (tm, tn), lambda i, j, k: (i, (OFF_GA + d) // tn + j)),
         (sv["ua"], (tm, tn), lambda i, j, k: (i, j)),
         (sv["ub"], (tm, tn), lambda i, j, k: (i, j))],
        [(0, 1, 0, NT)], [(tm, tn)], epi_gate, [wide] * 4)

    tk = _tile(t, 512, 16)
    (dw_out,) = _mm(
        "mix_dwout", (d // tn, 1, t // tk),
        [(sv["merged"], (tk, tn), lambda i, j, k: (k, i)), (dx2, (tk, d), lambda i, j, k: (k, 0))],
        [(0, 1, 0, TN)], [(tn, d)], lambda accs, refs: accs,
        [((d, d), BF16, (tn, d), lambda i, j, k: (i, 0), False)])

    w_spec = lambda rows: ((N_DEV, rows, nb), BF16, (None, rows, nb), lambda i, j, k: (j, 0, 0), False)
    dwua, dwub = _mm(
        "mix_dwup", (1, N_DEV, t // tk),
        [(sv["out_a"], (tk, A_WIDTH), lambda i, j, k: (k, 0)),
         (dua, (tk, nb), lambda i, j, k: (k, j)),
         (sv["out_b"], (tk, B_WIDTH), lambda i, j, k: (k, 0)),
         (dub, (tk, nb), lambda i, j, k: (k, j))],
        [(0, 1, 0, TN), (2, 3, 1, TN)], [(A_WIDTH, nb), (B_WIDTH, nb)], lambda accs, refs: accs,
        [w_spec(A_WIDTH), w_spec(B_WIDTH)])

    d_out_a, d_out_b = _mm(
        "mix_dup", (t // tm, 1, N_DEV),
        [(dua, (tm, nb), lambda i, j, k: (i, k)),
         (wua, (None, A_WIDTH, nb), lambda i, j, k: (k, 0, 0)),
         (dub, (tm, nb), lambda i, j, k: (i, k)),
         (wub, (None, B_WIDTH, nb), lambda i, j, k: (k, 0, 0))],
        [(0, 1, 0, NT), (2, 3, 1, NT)], [(tm, A_WIDTH), (tm, B_WIDTH)], lambda accs, refs: accs,
        [((t, A_WIDTH), BF16, (tm, A_WIDTH), lambda i, j, k: (i, 0), False),
         ((t, B_WIDTH), BF16, (tm, B_WIDTH), lambda i, j, k: (i, 0), False)])

    dq_a, dk_a, dv_a, dsinks = _attention_bwd(proj, tables, sinks, d_out_a)
    dq_b, df_b, di_b, dog_b, dlb, dwn = _hgrn_bwd(proj, hlb, wn, sv["o_raw"], sv["states"], d_out_b)
    dproj = jnp.concatenate([dq_a, dk_a, dv_a, dq_b, df_b, di_b, dog_b, dga, dgb], axis=1)

    tdm = _tile(d, 1024, 128)
    tpn = _tile(n_in, 512, 128)
    (dw_in,) = _mm(
        "mix_dwin", (d // tdm, n_in // tpn, t // tk),
        [(sv["h2"], (tk, tdm), lambda i, j, k: (k, i)), (dproj, (tk, tpn), lambda i, j, k: (k, j))],
        [(0, 1, 0, TN)], [(tdm, tpn)], lambda accs, refs: accs,
        [((d, n_in), BF16, (tdm, tpn), lambda i, j, k: (i, j), False)])

    tm3 = _tile(t, 256, 8)

    def epi_dx(accs, refs):
        dx, dw = _rms_bwd(accs[0], refs[2][...], refs[3][...], refs[4][...])
        return [dx, dw]

    dx1, dmix = _mm(
        "mix_dx", (t // tm3, 1, n_in // tpn),
        [(dproj, (tm3, tpn), lambda i, j, k: (i, k)), (w_in, (d, tpn), lambda i, j, k: (0, k)),
         (x1, (tm3, d), lambda i, j, k: (i, 0)), (mix_norm, (1, d), lambda i, j, k: (0, 0)),
         (dx2, (tm3, d), lambda i, j, k: (i, 0))],
        [(0, 1, 0, NT)], [(tm3, d)], epi_dx,
        [((t, d), F32, (tm3, d), lambda i, j, k: (i, 0), False),
         ((1, d), F32, (1, d), lambda i, j, k: (0, 0), True)])
    grads = dict(w_in=dw_in, w_up_a=dwua, w_up_b=dwub, w_out=dw_out, mix_norm=dmix,
                 attn_sinks=dsinks, hgrn_lb=dlb, hgrn_norm=dwn)
    return dx1, grads


def _ple_fwd(x3, ple_norm, p16, wpg, wpp):
    t, d = x3.shape
    nb = d // N_DEV
    h4 = _rmsnorm("ple_norm", x3, ple_norm)
    tm = _tile(t, 1024, 16)

    def epi(accs, refs):
        zg, pp = accs
        return [refs[4][...] + _sig(zg) * pp, zg, pp]

    blk = lambda dt: ((t, d), dt, (tm, nb), lambda i, j, k: (i, j), False)
    x4, zg, pp = _mm(
        "ple_fwd", (t // tm, N_DEV, 1),
        [(h4, (tm, d), lambda i, j, k: (i, 0)), (wpg, (d, nb), lambda i, j, k: (0, j)),
         (p16, (tm, PLE_DIM), lambda i, j, k: (i, 0)),
         (wpp, (None, PLE_DIM, nb), lambda i, j, k: (j, 0, 0)),
         (x3, (tm, nb), lambda i, j, k: (i, j))],
        [(0, 1, 0, NN), (2, 3, 1, NN)], [(tm, nb)] * 2, epi, [blk(F32), blk(BF16), blk(BF16)])
    return x4, dict(h4=h4, zg=zg, pp=pp)


def _ple_bwd(dx4, x3, ple_norm, p16, wpg, sv):
    t, d = x3.shape
    nb = d // N_DEV

    def gate_grads(dx, zg, pp):
        s = _sig(zg.astype(F32))
        return dx * pp.astype(F32) * s * (1.0 - s), dx * s

    dzg, dpp = _rows("ple_dgate", gate_grads, [(dx4, True), (sv["zg"], True), (sv["pp"], True)],
                     [((t, d), BF16), ((t, d), BF16)], _tile(t, 512, 16))
    tk = _tile(t, 512, 16)
    (dwpp,) = _mm(
        "ple_dwproj", (1, N_DEV, t // tk),
        [(p16, (tk, PLE_DIM), lambda i, j, k: (k, 0)), (dpp, (tk, nb), lambda i, j, k: (k, j))],
        [(0, 1, 0, TN)], [(PLE_DIM, nb)], lambda accs, refs: accs,
        [((N_DEV, PLE_DIM, nb), BF16, (None, PLE_DIM, nb), lambda i, j, k: (j, 0, 0), False)])
    tn = _tile(d, 1024, 128)
    (dwpg,) = _mm(
        "ple_dwgate", (d // tn, 1, t // tk),
        [(sv["h4"], (tk, tn), lambda i, j, k: (k, i)), (dzg, (tk, d), lambda i, j, k: (k, 0))],
        [(0, 1, 0, TN)], [(tn, d)], lambda accs, refs: accs,
        [((d, d), BF16, (tn, d), lambda i, j, k: (i, 0), False)])
    tm3 = _tile(t, 256, 8)

    def epi_dx(accs, refs):
        dx, dw = _rms_bwd(accs[0], refs[2][...], refs[3][...], refs[4][...])
        return [dx, dw]

    dx3, dnorm = _mm(
        "ple_dx", (t // tm3, 1, 1),
        [(dzg, (tm3, d), lambda i, j, k: (i, 0)), (wpg, (d, d), lambda i, j, k: (0, 0)),
         (x3, (tm3, d), lambda i, j, k: (i, 0)), (ple_norm, (1, d), lambda i, j, k: (0, 0)),
         (dx4, (tm3, d), lambda i, j, k: (i, 0))],
        [(0, 1, 0, NT)], [(tm3, d)], epi_dx,
        [((t, d), F32, (tm3, d), lambda i, j, k: (i, 0), False),
         ((1, d), F32, (1, d), lambda i, j, k: (0, 0), True)])
    return dx3, dwpp, dwpg, dnorm


def _loss_head(x4, final_norm, target):
    t, d = x4.shape
    tm = _tile(t, 256, 8)

    def body(x_ref, w_ref, tgt_ref, dx_ref, dw_ref, loss_ref):
        i = pl.program_id(0)
        xv = x_ref[...]
        wv = w_ref[...]
        r = lax.rsqrt(jnp.mean(xv * xv, axis=-1, keepdims=True) + EPS)
        err = xv * r * wv - tgt_ref[...]
        part = 0.5 * jnp.sum(jnp.mean(err * err, axis=-1, keepdims=True), axis=0, keepdims=True)
        dx, dw = _rms_bwd(err * (1.0 / d), xv, wv, jnp.zeros_like(xv))
        dx_ref[...] = dx

        @pl.when(i == 0)
        def _():
            dw_ref[...] = dw
            loss_ref[...] = jnp.broadcast_to(part, loss_ref.shape)

        @pl.when(i > 0)
        def _():
            dw_ref[...] += dw
            loss_ref[...] += jnp.broadcast_to(part, loss_ref.shape)

    return pl.pallas_call(
        body,
        name="loss_head",
        grid=(t // tm,),
        in_specs=[pl.BlockSpec((tm, d), lambda i: (i, 0)), pl.BlockSpec((1, d), lambda i: (0, 0)),
                  pl.BlockSpec((tm, d), lambda i: (i, 0))],
        out_specs=[pl.BlockSpec((tm, d), lambda i: (i, 0)), pl.BlockSpec((1, d), lambda i: (0, 0)),
                   pl.BlockSpec((1, LANES), lambda i: (0, 0))],
        out_shape=[jax.ShapeDtypeStruct((t, d), F32), jax.ShapeDtypeStruct((1, d), F32),
                   jax.ShapeDtypeStruct((1, LANES), F32)],
        compiler_params=_params(("arbitrary",)),
    )(x4, final_norm, target)


HBM_SPEC = pl.BlockSpec(memory_space=pltpu.HBM)
RELATIONS = ((0, 0), (1, 0), (0, 1), (1, 1))


def _place():
    return lax.axis_index("x"), lax.axis_index("y"), lax.axis_index("c")


def _flip(v, bit):
    return 1 - v if bit else v


def _all_gather(shards):
    n = len(shards)

    def body(*refs):
        ins, outs = refs[:n], refs[n:2 * n]
        send_sems, recv_sems, local_sems = refs[2 * n:]
        x, y, c = _place()
        sibling = (x, y, 1 - c)

        def block(a, rel, core):
            return outs[a].at[4 * _flip(x, rel[0]) + 2 * _flip(y, rel[1]) + core]

        def copy(a, slot, rel, core, to, src=None):
            dst = block(a, rel, core)
            return pltpu.make_async_remote_copy(
                src_ref=dst if src is None else src, dst_ref=dst,
                send_sem=send_sems.at[7 * a + slot], recv_sem=recv_sems.at[7 * a + slot],
                device_id=to, device_id_type=MESH)

        own, first, passed = [], [], []
        for a in range(n):
            mine = pltpu.make_async_copy(ins[a], block(a, RELATIONS[0], c), local_sems.at[a])
            mine.start()
            own.append(mine)
            first.append(copy(a, 0, RELATIONS[0], c, sibling, src=ins[a]))
            for r in (1, 2, 3):
                rel = RELATIONS[r]
                first.append(copy(a, r, RELATIONS[0], c, (_flip(x, rel[0]), _flip(y, rel[1]), c), src=ins[a]))
        for cp in first:
            cp.start()
        for r in (1, 2, 3):
            for a in range(n):
                copy(a, r, RELATIONS[r], c, sibling).wait_recv()
                fwd = copy(a, 3 + r, RELATIONS[r], c, sibling)
                fwd.start()
                passed.append(fwd)
        for a in range(n):
            copy(a, 0, RELATIONS[0], 1 - c, sibling).wait_recv()
            for r in (1, 2, 3):
                copy(a, 3 + r, RELATIONS[r], 1 - c, sibling).wait_recv()
        for cp in first + passed:
            cp.wait_send()
        for mine in own:
            mine.wait()

    return pl.pallas_call(
        body,
        name="weights_all_gather",
        in_specs=[HBM_SPEC] * n,
        out_specs=[HBM_SPEC] * n,
        out_shape=[jax.ShapeDtypeStruct((N_DEV,) + s.shape, s.dtype) for s in shards],
        scratch_shapes=[pltpu.SemaphoreType.DMA((7 * n,)), pltpu.SemaphoreType.DMA((7 * n,)),
                        pltpu.SemaphoreType.DMA((n,))],
    )(*shards)


def _exchange_sibling(grads):
    n = len(grads)

    def body(*refs):
        ins, outs = refs[:n], refs[n:2 * n]
        send_sems, recv_sems = refs[2 * n:]
        x, y, c = _place()
        copies = []
        for a in range(n):
            for r, rel in enumerate(RELATIONS):
                blk = 4 * _flip(x, rel[0]) + 2 * _flip(y, rel[1]) + (1 - c)
                copies.append(pltpu.make_async_remote_copy(
                    src_ref=ins[a].at[blk], dst_ref=outs[a].at[r],
                    send_sem=send_sems.at[4 * a + r], recv_sem=recv_sems.at[4 * a + r],
                    device_id=(x, y, 1 - c), device_id_type=MESH))
        for cp in copies:
            cp.start()
        for cp in copies:
            cp.wait()

    return pl.pallas_call(
        body,
        name="grads_to_sibling",
        in_specs=[HBM_SPEC] * n,
        out_specs=[HBM_SPEC] * n,
        out_shape=[jax.ShapeDtypeStruct((4,) + g.shape[1:], g.dtype) for g in grads],
        scratch_shapes=[pltpu.SemaphoreType.DMA((4 * n,)), pltpu.SemaphoreType.DMA((4 * n,))],
    )(*grads)


def _pair_sum(name, grad, from_sibling, block_ids):
    _, rows, cols = grad.shape
    tr = _tile(rows, 512, 16)

    def body(ids_ref, mine_ref, sib_ref, o_ref):
        o_ref[...] = (mine_ref[...].astype(F32) + sib_ref[...].astype(F32)).astype(o_ref.dtype)

    return pl.pallas_call(
        body,
        name=name,
        grid_spec=pltpu.PrefetchScalarGridSpec(
            num_scalar_prefetch=1,
            grid=(4, rows // tr),
            in_specs=[pl.BlockSpec((None, tr, cols), lambda r, i, ids: (ids[r], i, 0)),
                      pl.BlockSpec((None, tr, cols), lambda r, i, ids: (r, i, 0))],
            out_specs=pl.BlockSpec((None, tr, cols), lambda r, i, ids: (r, i, 0))),
        out_shape=jax.ShapeDtypeStruct((4, rows, cols), BF16),
        compiler_params=_params(("arbitrary", "arbitrary")),
    )(block_ids, grad, from_sibling)


def _exchange_chips(sums):
    n = len(sums)

    def body(*refs):
        ins, outs = refs[:n], refs[n:2 * n]
        send_sems, recv_sems = refs[2 * n:]
        x, y, c = _place()
        copies = []
        for a in range(n):
            for r in (1, 2, 3):
                rel = RELATIONS[r]
                copies.append(pltpu.make_async_remote_copy(
                    src_ref=ins[a].at[r], dst_ref=outs[a].at[r - 1],
                    send_sem=send_sems.at[3 * a + r - 1], recv_sem=recv_sems.at[3 * a + r - 1],
                    device_id=(_flip(x, rel[0]), _flip(y, rel[1]), c), device_id_type=MESH))
        for cp in copies:
            cp.start()
        for cp in copies:
            cp.wait()

    return pl.pallas_call(
        body,
        name="grads_to_chips",
        in_specs=[HBM_SPEC] * n,
        out_specs=[HBM_SPEC] * n,
        out_shape=[jax.ShapeDtypeStruct((3,) + s.shape[1:], s.dtype) for s in sums],
        scratch_shapes=[pltpu.SemaphoreType.DMA((3 * n,)), pltpu.SemaphoreType.DMA((3 * n,))],
    )(*sums)


def _adam(w, g, m, v):
    m = ADAM_B1 * m + (1.0 - ADAM_B1) * g
    v = ADAM_B2 * v + (1.0 - ADAM_B2) * (g * g)
    m_hat = m / (1.0 - ADAM_B1 ** ADAM_STEP)
    v_hat = v / (1.0 - ADAM_B2 ** ADAM_STEP)
    delta = -ADAM_LR * (m_hat / (jnp.sqrt(v_hat) + ADAM_EPS) + ADAM_WD * w)
    return delta, m, v


def _adam_sharded(name, w, m, v, sums, from_chips):
    rows, cols = w.shape
    tr = _tile(rows, 256, 8)

    def body(w_ref, m_ref, v_ref, s_ref, r1_ref, r2_ref, r3_ref, g_ref, d_ref, nm_ref, nv_ref):
        g = ((s_ref[...].astype(F32) + r1_ref[...].astype(F32)) + r2_ref[...].astype(F32)) + r3_ref[...].astype(F32)
        delta, nm, nv = _adam(w_ref[...], g, m_ref[...], v_ref[...])
        g_ref[...] = g
        d_ref[...] = delta
        nm_ref[...] = nm
        nv_ref[...] = nv

    flat = pl.BlockSpec((tr, cols), lambda i: (i, 0))
    slot = lambda s: pl.BlockSpec((None, tr, cols), lambda i: (s, i, 0))
    return pl.pallas_call(
        body,
        name=name,
        grid=(rows // tr,),
        in_specs=[flat, flat, flat, slot(0), slot(0), slot(1), slot(2)],
        out_specs=[flat] * 4,
        out_shape=[jax.ShapeDtypeStruct((rows, cols), F32)] * 4,
        compiler_params=_params(("arbitrary",)),
    )(w, m, v, sums, from_chips, from_chips, from_chips)


def _all_reduce_small(part):
    rows = part.shape[0]

    def body(x_ref, o_ref, gathered, send_sems, recv_sems):
        x, y, c = _place()
        me = 4 * x + 2 * y + c
        gathered[me] = x_ref[...]
        copies = []
        for k in range(1, N_DEV):
            bits = (k >> 2 & 1, k >> 1 & 1, k & 1)
            peer = (_flip(x, bits[0]), _flip(y, bits[1]), _flip(c, bits[2]))
            copies.append(pltpu.make_async_remote_copy(
                src_ref=x_ref, dst_ref=gathered.at[me],
                send_sem=send_sems.at[k - 1], recv_sem=recv_sems.at[k - 1],
                device_id=peer, device_id_type=MESH))
        for cp in copies:
            cp.start()
        for cp in copies:
            cp.wait()
        total = gathered[0]
        for b in range(1, N_DEV):
            total = total + gathered[b]
        o_ref[...] = total

    return pl.pallas_call(
        body,
        name="small_all_reduce",
        in_specs=[pl.BlockSpec(memory_space=pltpu.VMEM)],
        out_specs=pl.BlockSpec(memory_space=pltpu.VMEM),
        out_shape=jax.ShapeDtypeStruct((rows, LANES), F32),
        scratch_shapes=[pltpu.VMEM((N_DEV, rows, LANES), F32),
                        pltpu.SemaphoreType.DMA((N_DEV - 1,)), pltpu.SemaphoreType.DMA((N_DEV - 1,))],
    )(part)


def _adam_small(w, g, m, v):
    def fn(wv, gv, mv, vv):
        return _adam(wv, gv, mv, vv)

    shape = (w.shape, F32)
    return _rows("adam_small", fn, [(w, True), (g, True), (m, True), (v, True)], [shape] * 3, w.shape[0])


SMALL = ("ffn1_norm", "mix_norm", "attn_sinks", "hgrn_lower_bound", "hgrn_norm", "ffn2_norm", "ple_norm",
         "final_norm")
LARGE = ("ffn1_w_gate", "ffn1_w_up", "ffn1_w_down", "w_in", "w_up_a", "w_up_b", "w_out",
         "ffn2_w_gate", "ffn2_w_up", "ffn2_w_down", "ple_w_gate", "ple_w_proj")
WEIGHTS = ("ffn1_norm", "ffn1_w_gate", "ffn1_w_up", "ffn1_w_down", "mix_norm", "w_in", "attn_sinks",
           "hgrn_lower_bound", "hgrn_norm", "w_up_a", "w_up_b", "w_out", "ffn2_norm", "ffn2_w_gate",
           "ffn2_w_up", "ffn2_w_down", "ple_norm", "ple_w_gate", "ple_w_proj", "final_norm")


def _pack_rows(arrays):
    rows = []
    for a in arrays:
        flat = a.reshape(-1).astype(F32)
        pad = -flat.shape[0] % LANES
        rows.append(jnp.pad(flat, (0, pad)).reshape(-1, LANES))
    packed = jnp.concatenate(rows, axis=0)
    return jnp.pad(packed, ((0, -packed.shape[0] % 8), (0, 0)))


def _unpack_rows(packed, like):
    out, at = [], 0
    for a in like:
        size = a.size
        n_rows = -(-size // LANES)
        out.append(packed[at:at + n_rows].reshape(-1)[:size].reshape(a.shape))
        at += n_rows
    return out


def _local_step(x, p16, tables, target, small, full):
    h1 = _rmsnorm("ffn1_norm", x, small["ffn1_norm"])
    x1, g1, u1, a1 = _ffn_fwd("ffn1", x, h1, full["ffn1_w_gate"], full["ffn1_w_up"], full["ffn1_w_down"])
    mix_args = (small["mix_norm"], full["w_in"], small["attn_sinks"], small["hgrn_lower_bound"],
                small["hgrn_norm"], full["w_up_a"], full["w_up_b"], full["w_out"], tables)
    x2, mix_saved = _mixer_fwd(x1, *mix_args)
    h3 = _rmsnorm("ffn2_norm", x2, small["ffn2_norm"])
    x3, g2, u2, a2 = _ffn_fwd("ffn2", x2, h3, full["ffn2_w_gate"], full["ffn2_w_up"], full["ffn2_w_down"])
    x4, ple_saved = _ple_fwd(x3, small["ple_norm"], p16, full["ple_w_gate"], full["ple_w_proj"])
    dx4, d_final, loss_row = _loss_head(x4, small["final_norm"], target)

    dx3, dwpp, dwpg, d_ple = _ple_bwd(dx4, x3, small["ple_norm"], p16, full["ple_w_gate"], ple_saved)
    dx2, dwg2, dwu2, dwd2, d_ffn2 = _ffn_bwd("ffn2b", dx3, x2, h3, g2, u2, a2, full["ffn2_w_gate"],
                                              full["ffn2_w_up"], full["ffn2_w_down"], small["ffn2_norm"])
    dx1, mix_grads = _mixer_bwd(dx2, x1, *mix_args, mix_saved)
    dx0, dwg1, dwu1, dwd1, d_ffn1 = _ffn_bwd("ffn1b", dx1, x, h1, g1, u1, a1, full["ffn1_w_gate"],
                                              full["ffn1_w_up"], full["ffn1_w_down"], small["ffn1_norm"])
    d = x.shape[1]
    n_in = mix_grads["w_in"].shape[1]
    large = dict(
        ffn1_w_gate=dwg1, ffn1_w_up=dwu1, ffn1_w_down=dwd1,
        w_in=mix_grads["w_in"].reshape(d, N_DEV, n_in // N_DEV).transpose(1, 0, 2),
        w_up_a=mix_grads["w_up_a"], w_up_b=mix_grads["w_up_b"],
        w_out=mix_grads["w_out"].reshape(N_DEV, d // N_DEV, d),
        ffn2_w_gate=dwg2, ffn2_w_up=dwu2, ffn2_w_down=dwd2,
        ple_w_gate=dwpg.reshape(N_DEV, d // N_DEV, d), ple_w_proj=dwpp)
    dlb = mix_grads["hgrn_lb"]
    small_grads = dict(
        ffn1_norm=d_ffn1, mix_norm=mix_grads["mix_norm"], attn_sinks=mix_grads["attn_sinks"][:, :A_HEADS],
        hgrn_lower_bound=jnp.concatenate([dlb, -dlb], axis=0), hgrn_norm=mix_grads["hgrn_norm"],
        ffn2_norm=d_ffn2, ple_norm=d_ple, final_norm=d_final)
    return loss_row, dx0, large, small_grads


def kernel(x, p, positions, ffn1_norm, ffn1_w_gate, ffn1_w_up, ffn1_w_down, mix_norm, w_in, attn_sinks, hgrn_lower_bound, hgrn_norm, w_up_a, w_up_b, w_out, ffn2_norm, ffn2_w_gate, ffn2_w_up, ffn2_w_down, ple_norm, ple_w_gate, ple_w_proj, final_norm, loss_target, m_ffn1_norm, m_ffn1_w_gate, m_ffn1_w_up, m_ffn1_w_down, m_mix_norm, m_w_in, m_attn_sinks, m_hgrn_lower_bound, m_hgrn_norm, m_w_up_a, m_w_up_b, m_w_out, m_ffn2_norm, m_ffn2_w_gate, m_ffn2_w_up, m_ffn2_w_down, m_ple_norm, m_ple_w_gate, m_ple_w_proj, m_final_norm, v_ffn1_norm, v_ffn1_w_gate, v_ffn1_w_up, v_ffn1_w_down, v_mix_norm, v_w_in, v_attn_sinks, v_hgrn_lower_bound, v_hgrn_norm, v_w_up_a, v_w_up_b, v_w_out, v_ffn2_norm, v_ffn2_w_gate, v_ffn2_w_up, v_ffn2_w_down, v_ple_norm, v_ple_w_gate, v_ple_w_proj, v_final_norm):
    given = dict(locals())
    w = {n: given[n] for n in WEIGHTS}
    mom = {n: given["m_" + n] for n in WEIGHTS}
    var = {n: given["v_" + n] for n in WEIGHTS}
    t, d = x.shape[1], x.shape[2]

    shards = [w[n][0].astype(BF16) for n in LARGE]
    gathered = dict(zip(LARGE, _all_gather(shards)))
    full = dict(gathered)
    full["w_in"] = gathered["w_in"].transpose(1, 0, 2).reshape(d, -1)
    full["w_out"] = gathered["w_out"].reshape(d, d)
    full["ple_w_gate"] = gathered["ple_w_gate"].reshape(d, d)

    small = dict(ffn1_norm=ffn1_norm, mix_norm=mix_norm, attn_sinks=attn_sinks.reshape(-1),
                 hgrn_lower_bound=hgrn_lower_bound, hgrn_norm=hgrn_norm, ffn2_norm=ffn2_norm,
                 ple_norm=ple_norm, final_norm=final_norm.reshape(1, d))
    loss_row, grad_x, large_grads, small_grads = _local_step(
        x[0], p[0, 0].astype(BF16), _rope_tables(positions), loss_target[0], small, full)

    xi, yi, ci = _place()
    block_ids = jnp.stack([4 * _flip(xi, rx) + 2 * _flip(yi, ry) + ci for rx, ry in RELATIONS]).astype(jnp.int32)
    grads = [large_grads[n] for n in LARGE]
    from_sibling = _exchange_sibling(grads)
    sums = [_pair_sum("pair_sum_" + n, g, s, block_ids) for n, g, s in zip(LARGE, grads, from_sibling)]
    from_chips = _exchange_chips(sums)
    out = {}
    for n, s, r in zip(LARGE, sums, from_chips):
        shape = w[n].shape
        flat = lambda a: a.reshape(shape[1], shape[2])
        res = _adam_sharded("adam_" + n, flat(w[n]), flat(mom[n]), flat(var[n]), s, r)
        out[n] = [a.reshape(shape) for a in res]

    small_like = [w[n] for n in SMALL]
    packed = _pack_rows([small_grads[n].reshape(w[n].shape) for n in SMALL] + [loss_row[:, :1]])
    total = _all_reduce_small(packed)
    n_rows = total.shape[0]
    pack_w = _pack_rows(small_like + [jnp.zeros((1, 1), F32)])
    pack_m = _pack_rows([mom[n] for n in SMALL] + [jnp.zeros((1, 1), F32)])
    pack_v = _pack_rows([var[n] for n in SMALL] + [jnp.zeros((1, 1), F32)])
    deltas = _adam_small(pack_w, total, pack_m, pack_v)
    like = small_like + [jnp.zeros((1, 1), F32)]
    g_small = _unpack_rows(total, like)
    d_small, m_small, v_small = (_unpack_rows(a, like) for a in deltas)
    for k, n in enumerate(SMALL):
        out[n] = [g_small[k], d_small[k], m_small[k], v_small[k]]
    loss = g_small[-1].reshape(())

    return (loss, grad_x.reshape(x.shape),
            *[out[n][0] for n in WEIGHTS], *[out[n][1] for n in WEIGHTS],
            *[out[n][2] for n in WEIGHTS], *[out[n][3] for n in WEIGHTS])
```

```python
import functools

import jax
import jax.numpy as jnp
from jax import lax
from jax.experimental import pallas as pl
from jax.experimental.pallas import tpu as pltpu

F32 = jnp.float32
BF16 = jnp.bfloat16
MESH = pl.DeviceIdType.MESH

EPS = 1e-6
N_DEV = 8
A_HEADS = 16
A_HEAD_DIM = 64
A_WIDTH = 1024
A_KV_WIDTH = 256
ATT_BLOCK = 128
ROT_DIM = 16
ROPE_THETA = 500000.0
B_WIDTH = 1024
B_HEAD = 128
CHUNK = 64
SUB = 16
MAX_DECAY_EXP = 60.0
PLE_DIM = 256
LANES = 128

ADAM_LR = 0.001
ADAM_B1 = 0.9
ADAM_B2 = 0.999
ADAM_EPS = 1e-08
ADAM_WD = 0.01
ADAM_STEP = 10

VMEM_LIMIT = 56 * 1024 * 1024

NN = (((1,), (0,)), ((), ()))
NT = (((1,), (1,)), ((), ()))
TN = (((0,), (0,)), ((), ()))

OFF_QA = 0
OFF_KA = 1024
OFF_VA = 1280
OFF_QB = 1536
OFF_FB = 2560
OFF_IB = 3584
OFF_OG = 4608
OFF_GA = 5632


def _sig(v):
    return 1.0 / (1.0 + jnp.exp(-v))


def _tile(n, pref, mult):
    t = min(n, pref)
    t -= t % mult
    while n % t:
        t -= mult
    return t


def _params(sem):
    return pltpu.CompilerParams(dimension_semantics=sem, vmem_limit_bytes=VMEM_LIMIT)


HBM_SPEC = pl.BlockSpec(memory_space=pltpu.HBM)


class _Comm:
    def __init__(self, ins, out_shapes, n_sems, pre, post):
        self.ins, self.out_shapes, self.n_sems, self.pre, self.post = ins, out_shapes, n_sems, pre, post
        self.parts = [len(out_shapes)]

    def split(self, outs):
        res, at = [], 0
        for n in self.parts:
            res.append(list(outs[at:at + n]))
            at += n
        return res


def _merge(progs):
    spans, ins, shapes, sems = [], [], [], [0, 0, 0]
    for p in progs:
        spans.append((len(ins), len(shapes), tuple(sems)))
        ins += list(p.ins)
        shapes += list(p.out_shapes)
        sems = [a + b for a, b in zip(sems, p.n_sems)]

    def run(which):
        def go(cin, cout, send, recv, local, base):
            for p, (i0, o0, s0) in zip(progs, spans):
                getattr(p, which)(cin[i0:i0 + len(p.ins)], cout[o0:o0 + len(p.out_shapes)], send, recv, local,
                                  tuple(b + s for b, s in zip(base, s0)))
        return go

    merged = _Comm(ins, shapes, tuple(sems), run("pre"), run("post"))
    merged.parts = [len(p.out_shapes) for p in progs]
    return merged


def _call(name, body, grid, in_specs, out_specs, out_shape, scratch, args, comm=None):
    sem = ("arbitrary",) * len(grid)
    if comm is None:
        res = pl.pallas_call(body, name=name, grid=grid, in_specs=in_specs, out_specs=out_specs,
                             out_shape=out_shape, scratch_shapes=scratch, compiler_params=_params(sem))(*args)
        return list(res), []
    n_in, n_out, n_scr = len(in_specs), len(out_specs), len(scratch)
    c_in, c_out = len(comm.ins), len(comm.out_shapes)

    def carrier(*refs):
        at = [0]

        def take(n):
            at[0] += n
            return refs[at[0] - n:at[0]]

        ins, cins, outs, couts, scr, sems = take(n_in), take(c_in), take(n_out), take(c_out), take(n_scr), take(3)
        ids = [pl.program_id(a) for a in range(len(grid))]
        first = functools.reduce(jnp.logical_and, [i == 0 for i in ids])
        last = functools.reduce(jnp.logical_and, [i == g - 1 for i, g in zip(ids, grid)])

        @pl.when(first)
        def _():
            comm.pre(cins, couts, *sems, (0, 0, 0))

        body(*ins, *outs, *scr)

        @pl.when(last)
        def _():
            comm.post(cins, couts, *sems, (0, 0, 0))

    res = pl.pallas_call(
        carrier, name=name, grid=grid,
        in_specs=list(in_specs) + [HBM_SPEC] * c_in,
        out_specs=list(out_specs) + [HBM_SPEC] * c_out,
        out_shape=list(out_shape) + list(comm.out_shapes),
        scratch_shapes=list(scratch) + [pltpu.SemaphoreType.DMA((max(n, 1),)) for n in comm.n_sems],
        compiler_params=_params(sem),
    )(*args, *comm.ins)
    return list(res[:n_out]), list(res[n_out:])


def _mm(name, grid, ins, prods, acc_shapes, epi, outs, comm=None):
    n_in, n_out, nk = len(ins), len(outs), grid[2]

    def body(*refs):
        in_refs = refs[:n_in]
        out_refs = refs[n_in:n_in + n_out]
        acc_refs = refs[n_in + n_out:]
        first_tile = jnp.logical_and(pl.program_id(0) == 0, pl.program_id(1) == 0)

        def product(lhs, rhs, dims):
            a = in_refs[lhs][...].astype(BF16)
            b = in_refs[rhs][...].astype(BF16)
            return lax.dot_general(a, b, dims, preferred_element_type=F32)

        def write(accs):
            vals = epi(accs, in_refs)
            for o, v, spec in zip(out_refs, vals, outs):
                if spec[4]:
                    @pl.when(first_tile)
                    def _():
                        o[...] = v.astype(o.dtype)

                    @pl.when(jnp.logical_not(first_tile))
                    def _():
                        o[...] += v.astype(o.dtype)
                else:
                    o[...] = v.astype(o.dtype)

        if nk == 1:
            accs = [None] * len(acc_shapes)
            for lhs, rhs, acc, dims in prods:
                t = product(lhs, rhs, dims)
                accs[acc] = t if accs[acc] is None else accs[acc] + t
            write(accs)
        else:
            k = pl.program_id(2)

            @pl.when(k == 0)
            def _():
                for r in acc_refs:
                    r[...] = jnp.zeros(r.shape, F32)

            for lhs, rhs, acc, dims in prods:
                acc_refs[acc][...] += product(lhs, rhs, dims)

            @pl.when(k == nk - 1)
            def _():
                write([r[...] for r in acc_refs])

    res, sent = _call(
        name, body, grid,
        [pl.BlockSpec(b, im) for _, b, im in ins],
        [pl.BlockSpec(o[2], o[3]) for o in outs],
        [jax.ShapeDtypeStruct(o[0], o[1]) for o in outs],
        [] if nk == 1 else [pltpu.VMEM(s, F32) for s in acc_shapes],
        [a for a, _, _ in ins], comm)
    return res if comm is None else (res, sent)


def _rms_bwd(dh, xv, w, dres):
    r = lax.rsqrt(jnp.mean(xv * xv, axis=-1, keepdims=True) + EPS)
    g = dh * w
    dx = dres + r * (g - xv * (r * r) * jnp.mean(g * xv, axis=-1, keepdims=True))
    dw = jnp.sum(dh * xv * r, axis=0, keepdims=True)
    return dx, dw


def _rows(name, fn, ins, outs, tm):
    n_in = len(ins)
    rows = outs[0][0][0]

    def body(*refs):
        vals = fn(*[r[...] for r in refs[:n_in]])
        for o, v in zip(refs[n_in:], vals):
            o[...] = v.astype(o.dtype)

    in_specs = []
    for a, tiled in ins:
        if tiled:
            in_specs.append(pl.BlockSpec((tm, a.shape[1]), lambda i: (i, 0)))
        else:
            in_specs.append(pl.BlockSpec(a.shape, lambda i, nd=a.ndim: (0,) * nd))
    return pl.pallas_call(
        body,
        name=name,
        grid=(rows // tm,),
        in_specs=in_specs,
        out_specs=[pl.BlockSpec((tm, s[1]), lambda i: (i, 0)) for s, _ in outs],
        out_shape=[jax.ShapeDtypeStruct(s, d) for s, d in outs],
        compiler_params=_params(("arbitrary",)),
    )(*[a for a, _ in ins])


def _rmsnorm(name, x, w):
    def fn(xv, wv):
        r = lax.rsqrt(jnp.mean(xv * xv, axis=-1, keepdims=True) + EPS)
        return (xv * r * wv,)

    return _rows(name, fn, [(x, True), (w, False)], [(x.shape, BF16)], _tile(x.shape[0], 512, 16))[0]


def _mmc(*args, comm=None):
    if comm is None:
        return _mm(*args), []
    return _mm(*args, comm=comm)


def _carried(comms, key, avail):
    c = comms.get(key) if comms else None
    return c(avail) if callable(c) else c


def _ffn_up(name, h, wg, wu, comm=None):
    t, d = h.shape
    n = wg.shape[2]
    tm = _tile(t, 1024, 16)

    def epi_up(accs, refs):
        g, u = accs
        return [g, u, g * _sig(g) * u]

    hid = ((N_DEV, t, n), BF16, (None, tm, n), lambda i, j, k: (j, i, 0), False)
    return _mmc(
        name + "_up", (t // tm, N_DEV, 1),
        [(h, (tm, d), lambda i, j, k: (i, 0)),
         (wg, (None, d, n), lambda i, j, k: (j, 0, 0)),
         (wu, (None, d, n), lambda i, j, k: (j, 0, 0))],
        [(0, 1, 0, NN), (0, 2, 1, NN)], [(tm, n)] * 2, epi_up, [hid] * 3, comm=comm)


def _ffn_down(name, x, act, wd, comm=None):
    t, d = x.shape
    n = wd.shape[1]
    tm2 = _tile(t, 512, 8)

    def epi_down(accs, refs):
        return [refs[2][...] + 0.5 * accs[0]]

    (x_out,), sent = _mmc(
        name + "_down", (t // tm2, 1, N_DEV),
        [(act, (None, tm2, n), lambda i, j, k: (k, i, 0)),
         (wd, (None, n, d), lambda i, j, k: (k, 0, 0)),
         (x, (tm2, d), lambda i, j, k: (i, 0))],
        [(0, 1, 0, NN)], [(tm2, d)], epi_down,
        [((t, d), F32, (tm2, d), lambda i, j, k: (i, 0), False)], comm=comm)
    return x_out, sent


def _ffn_bwd(name, dxo, x, h, gate, up, act, wg, wu, wd, norm_w, comms=None):
    t, d = x.shape
    n = wg.shape[2]
    tm = _tile(t, 512, 16)
    sent = {}
    avail = dict(sent=sent)

    def epi_act(accs, refs):
        da = 0.5 * accs[0]
        g = refs[2][...].astype(F32)
        u = refs[3][...].astype(F32)
        s = _sig(g)
        return [da * u * (s * (1.0 + g * (1.0 - s))), da * g * s]

    hid_in = lambda a: (a, (None, tm, n), lambda i, j, k: (j, i, 0))
    hid_out = ((N_DEV, t, n), BF16, (None, tm, n), lambda i, j, k: (j, i, 0), False)
    (dgate, dup), sent["dact"] = _mmc(
        name + "_dact", (t // tm, N_DEV, 1),
        [(dxo, (tm, d), lambda i, j, k: (i, 0)),
         (wd, (None, n, d), lambda i, j, k: (j, 0, 0)),
         hid_in(gate), hid_in(up)],
        [(0, 1, 0, NT)], [(tm, n)], epi_act, [hid_out] * 2, comm=_carried(comms, "dact", avail))

    tk = _tile(t, 512, 16)
    td = _tile(d, 1024, 128)
    w_out = ((N_DEV, d, n), BF16, (None, td, n), lambda i, j, k: (j, i, 0), False)
    (dwg, dwu), sent["dwgu"] = _mmc(
        name + "_dwgu", (d // td, N_DEV, t // tk),
        [(h, (tk, td), lambda i, j, k: (k, i)),
         (dgate, (None, tk, n), lambda i, j, k: (j, k, 0)),
         (dup, (None, tk, n), lambda i, j, k: (j, k, 0))],
        [(0, 1, 0, TN), (0, 2, 1, TN)], [(td, n)] * 2, lambda accs, refs: accs, [w_out] * 2,
        comm=_carried(comms, "dwgu", avail))
    avail.update(dwg=dwg, dwu=dwu)

    (dwd,), sent["dwd"] = _mmc(
        name + "_dwd", (N_DEV, 1, t // tk),
        [(act, (None, tk, n), lambda i, j, k: (i, k, 0)),
         (dxo, (tk, d), lambda i, j, k: (k, 0))],
        [(0, 1, 0, TN)], [(n, d)], lambda accs, refs: [0.5 * accs[0]],
        [((N_DEV, n, d), BF16, (None, n, d), lambda i, j, k: (i, 0, 0), False)],
        comm=_carried(comms, "dwd", avail))
    avail.update(dwd=dwd)

    tm3 = _tile(t, 256, 8)

    def epi_dx(accs, refs):
        dx, dw = _rms_bwd(accs[0], refs[4][...], refs[5][...], refs[6][...])
        return [dx, dw]

    (dx, dnorm), sent["dx"] = _mmc(
        name + "_dx", (t // tm3, 1, N_DEV),
        [(dgate, (None, tm3, n), lambda i, j, k: (k, i, 0)),
         (wg, (None, d, n), lambda i, j, k: (k, 0, 0)),
         (dup, (None, tm3, n), lambda i, j, k: (k, i, 0)),
         (wu, (None, d, n), lambda i, j, k: (k, 0, 0)),
         (x, (tm3, d), lambda i, j, k: (i, 0)),
         (norm_w, (1, d), lambda i, j, k: (0, 0)),
         (dxo, (tm3, d), lambda i, j, k: (i, 0))],
        [(0, 1, 0, NT), (2, 3, 0, NT)], [(tm3, d)], epi_dx,
        [((t, d), F32, (tm3, d), lambda i, j, k: (i, 0), False),
         ((1, d), F32, (1, d), lambda i, j, k: (0, 0), True)],
        comm=_carried(comms, "dx", avail))
    return dx, dwg, dwu, dwd, dnorm, sent


def _rope(tv, cos, s1, s2):
    return tv * cos + pltpu.roll(tv, LANES - 8, 1) * s1 + pltpu.roll(tv, 8, 1) * s2


def _rope_bwd(dr, cos, s1, s2):
    return dr * cos + pltpu.roll(dr * s1, 8, 1) + pltpu.roll(dr * s2, LANES - 8, 1)


def _rope_tables(positions):
    inv_freq = jnp.power(jnp.float32(ROPE_THETA), -jnp.arange(0, ROT_DIM, 2, dtype=F32) / ROT_DIM)
    ang = positions.reshape(-1).astype(F32)[:, None] * inv_freq
    cos, sin = jnp.cos(ang), jnp.sin(ang)
    t = ang.shape[0]
    pad = A_HEAD_DIM - ROT_DIM
    cos_t = jnp.concatenate([cos, cos, jnp.ones((t, pad), F32)], axis=1)
    s1_t = jnp.concatenate([-sin, jnp.zeros((t, pad + 8), F32)], axis=1)
    s2_t = jnp.concatenate([jnp.zeros((t, 8), F32), sin, jnp.zeros((t, pad), F32)], axis=1)
    return tuple(jnp.tile(v, (1, 2)) for v in (cos_t, s1_t, s2_t))


def _att_common(i, k_p, k_c, v_p, v_c, tabs_p, tabs_c):
    kcat = jnp.concatenate([k_p, k_c], axis=0)
    vcat = jnp.concatenate([v_p, v_c], axis=0)
    tabs_k = [jnp.concatenate([a, b], axis=0) for a, b in zip(tabs_p, tabs_c)]
    kvar, vvar = [], []
    for ks in range(2):
        sl = slice(ks * LANES, (ks + 1) * LANES)
        kr = _rope(kcat[:, sl], *tabs_k)
        kvar.append([kr.astype(BF16), pltpu.roll(kr, 64, 1).astype(BF16)])
        vvar.append([vcat[:, sl].astype(BF16), pltpu.roll(vcat[:, sl], 64, 1).astype(BF16)])
    qi = lax.broadcasted_iota(jnp.int32, (ATT_BLOCK, 2 * ATT_BLOCK), 0)
    kj = lax.broadcasted_iota(jnp.int32, (ATT_BLOCK, 2 * ATT_BLOCK), 1)
    dist = qi + ATT_BLOCK - kj
    allowed = (dist >= 0) & (dist < ATT_BLOCK) & ((i > 0) | (kj >= ATT_BLOCK))
    lane = lax.broadcasted_iota(jnp.int32, (1, LANES), 1)
    halves = [lane < 64, lane >= 64]
    return kvar, vvar, allowed, halves


def _att_head_probs(qm, kslab, allowed, sink):
    s = lax.dot_general(qm, kslab, NT, preferred_element_type=F32)
    s = jnp.where(allowed, s, -jnp.inf)
    m = jnp.maximum(jnp.max(s, axis=-1, keepdims=True), sink)
    e = jnp.exp(s - m)
    es = jnp.exp(sink - m)
    den = jnp.sum(e, axis=-1, keepdims=True) + es
    return e / den, es / den


def _head_slots(h):
    pair, half = h // 2, h % 2
    kvh = h // 4
    return pair, half, kvh // 2, int(kvh % 2 != half)


def _attention_fwd(proj, tables, sinks):
    t = proj.shape[0]
    nb = t // ATT_BLOCK
    b = ATT_BLOCK
    scale = A_HEAD_DIM ** -0.5

    def body(sink_ref, q_ref, kp_ref, kc_ref, vp_ref, vc_ref,
             cp_ref, ap_ref, bp_ref, cc_ref, ac_ref, bc_ref, o_ref):
        i = pl.program_id(0)
        tabs_p = (cp_ref[...], ap_ref[...], bp_ref[...])
        tabs_c = (cc_ref[...], ac_ref[...], bc_ref[...])
        kvar, vvar, allowed, halves = _att_common(
            i, kp_ref[...], kc_ref[...], vp_ref[...], vc_ref[...], tabs_p, tabs_c)
        for pair in range(A_HEADS // 2):
            sl = slice(pair * LANES, (pair + 1) * LANES)
            q2 = _rope(q_ref[:, sl], *tabs_c) * scale
            acc = jnp.zeros((b, LANES), F32)
            for half in range(2):
                h = 2 * pair + half
                _, _, ks, var = _head_slots(h)
                qm = jnp.where(halves[half], q2, 0.0).astype(BF16)
                p, _ = _att_head_probs(qm, kvar[ks][var], allowed, sink_ref[h])
                o = lax.dot_general(p.astype(BF16), vvar[ks][var], NN, preferred_element_type=F32)
                acc = acc + jnp.where(halves[half], o, 0.0)
            o_ref[:, sl] = acc.astype(o_ref.dtype)

    cur = lambda col: (lambda i: (i, col))
    prev = lambda col: (lambda i: (jnp.maximum(i - 1, 0), col))
    tab = [pl.BlockSpec((b, LANES), prev(0))] * 3 + [pl.BlockSpec((b, LANES), cur(0))] * 3
    return pl.pallas_call(
        body,
        name="attention_fwd",
        grid=(nb,),
        in_specs=[pl.BlockSpec(memory_space=pltpu.SMEM),
                  pl.BlockSpec((b, A_WIDTH), cur(0)),
                  pl.BlockSpec((b, A_KV_WIDTH), prev(OFF_KA // A_KV_WIDTH)),
                  pl.BlockSpec((b, A_KV_WIDTH), cur(OFF_KA // A_KV_WIDTH)),
                  pl.BlockSpec((b, A_KV_WIDTH), prev(OFF_VA // A_KV_WIDTH)),
                  pl.BlockSpec((b, A_KV_WIDTH), cur(OFF_VA // A_KV_WIDTH))] + tab,
        out_specs=pl.BlockSpec((b, A_WIDTH), cur(0)),
        out_shape=jax.ShapeDtypeStruct((t, A_WIDTH), BF16),
        compiler_params=_params(("arbitrary",)),
    )(sinks, proj, proj, proj, proj, proj, *tables, *tables)


def _attention_bwd(proj, tables, sinks, d_out, comm=None):
    t = proj.shape[0]
    nb = t // ATT_BLOCK
    b = ATT_BLOCK
    scale = A_HEAD_DIM ** -0.5

    def body(sink_ref, q_ref, kp_ref, kc_ref, vp_ref, vc_ref,
             cp_ref, ap_ref, bp_ref, cc_ref, ac_ref, bc_ref, do_ref,
             dq_ref, dk_ref, dv_ref, dsink_ref, dk_carry, dv_carry):
        step = pl.program_id(0)
        i = nb - 1 - step

        @pl.when(step == 0)
        def _():
            dk_carry[...] = jnp.zeros(dk_carry.shape, F32)
            dv_carry[...] = jnp.zeros(dv_carry.shape, F32)
            dsink_ref[...] = jnp.zeros(dsink_ref.shape, F32)

        tabs_p = (cp_ref[...], ap_ref[...], bp_ref[...])
        tabs_c = (cc_ref[...], ac_ref[...], bc_ref[...])
        kvar, vvar, allowed, halves = _att_common(
            i, kp_ref[...], kc_ref[...], vp_ref[...], vc_ref[...], tabs_p, tabs_c)
        lane = lax.broadcasted_iota(jnp.int32, (1, LANES), 1)
        dk_slab = [jnp.zeros((2 * b, LANES), F32) for _ in range(2)]
        dv_slab = [jnp.zeros((2 * b, LANES), F32) for _ in range(2)]
        dsink = jnp.zeros((1, LANES), F32)
        for pair in range(A_HEADS // 2):
            sl = slice(pair * LANES, (pair + 1) * LANES)
            q2 = _rope(q_ref[:, sl], *tabs_c) * scale
            do2 = do_ref[:, sl].astype(F32)
            dq2 = jnp.zeros((b, LANES), F32)
            for half in range(2):
                h = 2 * pair + half
                _, _, ks, var = _head_slots(h)
                qm = jnp.where(halves[half], q2, 0.0).astype(BF16)
                dom = jnp.where(halves[half], do2, 0.0).astype(BF16)
                p, p_sink = _att_head_probs(qm, kvar[ks][var], allowed, sink_ref[h])
                dp = lax.dot_general(dom, vvar[ks][var], NT, preferred_element_type=F32)
                dsum = jnp.sum(p * dp, axis=-1, keepdims=True)
                ds = (p * (dp - dsum)).astype(BF16)
                dsink = dsink + jnp.where(lane == h, -jnp.sum(p_sink * dsum), 0.0)
                dqh = lax.dot_general(ds, kvar[ks][var], NN, preferred_element_type=F32)
                dq2 = dq2 + jnp.where(halves[half], dqh, 0.0)
                dkx = lax.dot_general(ds, qm, TN, preferred_element_type=F32)
                dvx = lax.dot_general(p.astype(BF16), dom, TN, preferred_element_type=F32)
                if var:
                    dkx = pltpu.roll(dkx, 64, 1)
                    dvx = pltpu.roll(dvx, 64, 1)
                dk_slab[ks] = dk_slab[ks] + dkx
                dv_slab[ks] = dv_slab[ks] + dvx
            dq_ref[:, sl] = _rope_bwd(dq2 * scale, *tabs_c).astype(dq_ref.dtype)
        for ks in range(2):
            sl = slice(ks * LANES, (ks + 1) * LANES)
            dk_cur = dk_slab[ks][b:] + dk_carry[:, sl]
            dk_ref[:, sl] = _rope_bwd(dk_cur, *tabs_c).astype(dk_ref.dtype)
            dv_ref[:, sl] = (dv_slab[ks][b:] + dv_carry[:, sl]).astype(dv_ref.dtype)
            dk_carry[:, sl] = dk_slab[ks][:b]
            dv_carry[:, sl] = dv_slab[ks][:b]
        dsink_ref[...] += dsink

    cur = lambda col: (lambda s: (nb - 1 - s, col))
    prev = lambda col: (lambda s: (jnp.maximum(nb - 2 - s, 0), col))
    tab = [pl.BlockSpec((b, LANES), prev(0))] * 3 + [pl.BlockSpec((b, LANES), cur(0))] * 3
    return _call(
        "attention_bwd", body, (nb,),
        [pl.BlockSpec(memory_space=pltpu.SMEM),
         pl.BlockSpec((b, A_WIDTH), cur(0)),
         pl.BlockSpec((b, A_KV_WIDTH), prev(OFF_KA // A_KV_WIDTH)),
         pl.BlockSpec((b, A_KV_WIDTH), cur(OFF_KA // A_KV_WIDTH)),
         pl.BlockSpec((b, A_KV_WIDTH), prev(OFF_VA // A_KV_WIDTH)),
         pl.BlockSpec((b, A_KV_WIDTH), cur(OFF_VA // A_KV_WIDTH))] + tab
        + [pl.BlockSpec((b, A_WIDTH), cur(0))],
        [pl.BlockSpec((b, A_WIDTH), cur(0)),
         pl.BlockSpec((b, A_KV_WIDTH), cur(0)),
         pl.BlockSpec((b, A_KV_WIDTH), cur(0)),
         pl.BlockSpec((1, LANES), lambda s: (0, 0))],
        [jax.ShapeDtypeStruct((t, A_WIDTH), BF16),
         jax.ShapeDtypeStruct((t, A_KV_WIDTH), BF16),
         jax.ShapeDtypeStruct((t, A_KV_WIDTH), BF16),
         jax.ShapeDtypeStruct((1, LANES), F32)],
        [pltpu.VMEM((b, A_KV_WIDTH), F32), pltpu.VMEM((b, A_KV_WIDTH), F32)],
        [sinks, proj, proj, proj, proj, proj, *tables, *tables, d_out], comm)


HGRN_HEADS_PER_STEP = 4
HGRN_W = HGRN_HEADS_PER_STEP * B_HEAD


def _hgrn_gates(qp, fp, hlb):
    lb = _sig(hlb[0:1] - hlb[1:2])
    sg = _sig(fp)
    big_f = lb + (1.0 - lb) * sg
    kk = (1.0 - lb) * (1.0 - sg)
    sq = _sig(qp)
    return lb, sg, big_f, kk, sq


def _tri(lower):
    r = lax.broadcasted_iota(jnp.int32, (CHUNK, CHUNK), 0)
    c = lax.broadcasted_iota(jnp.int32, (CHUNK, CHUNK), 1)
    return (r >= c) if lower else (r <= c)


def _hgrn_factors(q_ref, k_ref, b_ref, sl):
    out = []
    bh = b_ref[:, sl]
    for i in range(CHUNK // SUB):
        rows = slice(i * SUB, (i + 1) * SUB)
        b0 = b_ref[i * SUB - 1:i * SUB, sl] if i else jnp.zeros((1, B_HEAD), F32)
        eq = jnp.exp(b_ref[rows, sl] - b0)
        ek = jnp.exp(jnp.minimum(b0 - bh, MAX_DECAY_EXP))
        out.append((q_ref[rows, sl] * eq, k_ref[:, sl] * ek, eq, ek))
    return out


def _hgrn_scores(factors):
    rows = [lax.dot_general(qd.astype(BF16), kx.astype(BF16), NT, preferred_element_type=F32)
            for qd, kx, _, _ in factors]
    return jnp.where(_tri(True), jnp.concatenate(rows, axis=0), 0.0)


def _hgrn_specs(nc, reverse):
    w = HGRN_W
    chunk = (lambda c: nc - 1 - c) if reverse else (lambda c: c)
    col = lambda off: pl.BlockSpec((CHUNK, w), lambda hh, c: (chunk(c), off // w + hh))
    return chunk, col


def _hgrn_fwd(proj, hlb, wn, comm=None):
    t = proj.shape[0]
    nc = t // CHUNK
    w = HGRN_W
    chunk, col = _hgrn_specs(nc, False)

    def body(q_ref, f_ref, v_ref, og_ref, hlb_ref, wn_ref, ob_ref, oraw_ref, st_ref,
             state, qs, ks, bs):
        c = pl.program_id(1)

        @pl.when(c == 0)
        def _():
            state[...] = jnp.zeros(state.shape, F32)

        qp = q_ref[...]
        lb, sg, big_f, kk, sq = _hgrn_gates(qp, f_ref[...], hlb_ref[...])
        qs[...] = qp * sq
        ks[...] = kk
        bs[...] = lax.dot_general(_tri(True).astype(F32), jnp.log(big_f), NN,
                                  precision=lax.Precision.HIGHEST, preferred_element_type=F32)
        for h in range(HGRN_HEADS_PER_STEP):
            sl = slice(h * B_HEAD, (h + 1) * B_HEAD)
            bh = bs[:, sl]
            qh = qs[:, sl]
            kh = ks[:, sl]
            vh = v_ref[:, sl].astype(BF16)
            s0 = state[h]
            st_ref[h] = s0.astype(st_ref.dtype)
            a = _hgrn_scores(_hgrn_factors(qs, ks, bs, sl))
            o = lax.dot_general(a.astype(BF16), vh, NN, preferred_element_type=F32)
            o = o + lax.dot_general((qh * jnp.exp(bh)).astype(BF16), s0.astype(BF16), NT,
                                    preferred_element_type=F32)
            bl = bs[CHUNK - 1:CHUNK, sl]
            khat = (kh * jnp.exp(bl - bh)).astype(BF16)
            state[h] = s0 * jnp.exp(bl) + lax.dot_general(vh, khat, TN, preferred_element_type=F32)
            oraw_ref[:, sl] = o
            r = lax.rsqrt(jnp.mean(o * o, axis=-1, keepdims=True) + EPS)
            og = og_ref[:, sl]
            ob_ref[:, sl] = (o * r * wn_ref[:, sl] * (og * _sig(og))).astype(ob_ref.dtype)

    vec = lambda rows: pl.BlockSpec((rows, w), lambda hh, c: (0, hh))
    return _call(
        "hgrn_fwd", body, (B_WIDTH // w, nc),
        [col(OFF_QB), col(OFF_FB), col(OFF_IB), col(OFF_OG), vec(2), vec(1)],
        [pl.BlockSpec((CHUNK, w), lambda hh, c: (c, hh)),
         pl.BlockSpec((CHUNK, w), lambda hh, c: (c, hh)),
         pl.BlockSpec((None, HGRN_HEADS_PER_STEP, B_HEAD, B_HEAD), lambda hh, c: (c, hh, 0, 0))],
        [jax.ShapeDtypeStruct((t, B_WIDTH), BF16),
         jax.ShapeDtypeStruct((t, B_WIDTH), F32),
         jax.ShapeDtypeStruct((nc, B_WIDTH // B_HEAD, B_HEAD, B_HEAD), BF16)],
        [pltpu.VMEM((HGRN_HEADS_PER_STEP, B_HEAD, B_HEAD), F32),
         pltpu.VMEM((CHUNK, w), F32), pltpu.VMEM((CHUNK, w), F32), pltpu.VMEM((CHUNK, w), F32)],
        [proj, proj, proj, proj, hlb, wn], comm)


def _hgrn_bwd(proj, hlb, wn, o_raw, states, d_out, comm=None):
    t = proj.shape[0]
    nc = t // CHUNK
    w = HGRN_W
    chunk, col = _hgrn_specs(nc, True)

    def body(q_ref, f_ref, v_ref, og_ref, hlb_ref, wn_ref, oraw_ref, st_ref, dout_ref,
             dq_ref, df_ref, dv_ref, dog_ref, dlb_ref, dwn_ref,
             dstate, qs, ks, bs, dbs, dks, dqs):
        c = pl.program_id(1)

        @pl.when(c == 0)
        def _():
            dstate[...] = jnp.zeros(dstate.shape, F32)
            dlb_ref[...] = jnp.zeros(dlb_ref.shape, F32)
            dwn_ref[...] = jnp.zeros(dwn_ref.shape, F32)

        qp = q_ref[...]
        lb, sg, big_f, kk, sq = _hgrn_gates(qp, f_ref[...], hlb_ref[...])
        qs[...] = qp * sq
        ks[...] = kk
        bs[...] = lax.dot_general(_tri(True).astype(F32), jnp.log(big_f), NN,
                                  precision=lax.Precision.HIGHEST, preferred_element_type=F32)
        last_row = lax.broadcasted_iota(jnp.int32, (CHUNK, 1), 0) == CHUNK - 1
        for h in range(HGRN_HEADS_PER_STEP):
            sl = slice(h * B_HEAD, (h + 1) * B_HEAD)
            bh = bs[:, sl]
            qh = qs[:, sl]
            kh = ks[:, sl]
            vh = v_ref[:, sl].astype(BF16)
            s0 = st_ref[h]
            ds1 = dstate[h]
            o = oraw_ref[:, sl]
            og = og_ref[:, sl]
            wnh = wn_ref[:, sl]
            sog = _sig(og)
            r = lax.rsqrt(jnp.mean(o * o, axis=-1, keepdims=True) + EPS)
            dob = dout_ref[:, sl].astype(F32)
            don = dob * (og * sog)
            dog_ref[:, sl] = (dob * (o * r * wnh) * (sog * (1.0 + og * (1.0 - sog)))).astype(dog_ref.dtype)
            dwn_ref[:, sl] += jnp.sum(don * o * r, axis=0, keepdims=True)
            gp = don * wnh
            do = r * (gp - o * (r * r) * jnp.mean(gp * o, axis=-1, keepdims=True))
            do16 = do.astype(BF16)
            factors = _hgrn_factors(qs, ks, bs, sl)
            a = _hgrn_scores(factors)
            da = jnp.where(_tri(True), lax.dot_general(do16, vh, NT, preferred_element_type=F32), 0.0)
            dv = lax.dot_general(a.astype(BF16), do16, TN, preferred_element_type=F32)
            dq_rows = []
            dk = jnp.zeros((CHUNK, B_HEAD), F32)
            for i, (qd, kx, eq, ek) in enumerate(factors):
                da_i = da[i * SUB:(i + 1) * SUB]
                dq_rows.append(lax.dot_general(da_i, kx, NN, precision=lax.Precision.HIGHEST,
                                               preferred_element_type=F32) * eq)
                dk = dk + lax.dot_general(da_i, qd, TN, precision=lax.Precision.HIGHEST,
                                          preferred_element_type=F32) * ek
            eb = jnp.exp(bh)
            bl = bs[CHUNK - 1:CHUNK, sl]
            ebl = jnp.exp(bl)
            ekl = jnp.exp(bl - bh)
            qe = (qh * eb).astype(BF16)
            khat = (kh * ekl).astype(BF16)
            ds1_16 = ds1.astype(BF16)
            dq = jnp.concatenate(dq_rows, axis=0) + eb * lax.dot_general(
                do16, s0, NN, preferred_element_type=F32)
            dv = dv + lax.dot_general(khat, ds1_16, NT, preferred_element_type=F32)
            dk_state = ekl * lax.dot_general(vh, ds1_16, NN, preferred_element_type=F32)
            dk = dk + dk_state
            db_last = (jnp.sum(kh * dk_state, axis=0, keepdims=True)
                       + ebl * jnp.sum(s0.astype(F32) * ds1, axis=0, keepdims=True))
            dbs[:, sl] = qh * dq - kh * dk + jnp.where(last_row, db_last, 0.0)
            dks[:, sl] = dk
            dqs[:, sl] = dq
            dstate[h] = ds1 * ebl + lax.dot_general(do16, qe, TN, preferred_element_type=F32)
            dv_ref[:, sl] = dv.astype(dv_ref.dtype)
        dg = lax.dot_general(_tri(False).astype(F32), dbs[...], NN,
                             precision=lax.Precision.HIGHEST, preferred_element_type=F32)
        dq_ref[...] = (dqs[...] * (sq * (1.0 + qp * (1.0 - sq)))).astype(dq_ref.dtype)
        df_minus_dk = dg / big_f - dks[...]
        df_ref[...] = ((1.0 - lb) * df_minus_dk * sg * (1.0 - sg)).astype(df_ref.dtype)
        dlb_ref[...] += jnp.sum((1.0 - sg) * df_minus_dk, axis=0, keepdims=True) * (lb * (1.0 - lb))

    vec = lambda rows: pl.BlockSpec((rows, w), lambda hh, c: (0, hh))
    blk = pl.BlockSpec((CHUNK, w), lambda hh, c: (chunk(c), hh))
    act = jax.ShapeDtypeStruct((t, B_WIDTH), BF16)
    row = jax.ShapeDtypeStruct((1, B_WIDTH), F32)
    return _call(
        "hgrn_bwd", body, (B_WIDTH // w, nc),
        [col(OFF_QB), col(OFF_FB), col(OFF_IB), col(OFF_OG), vec(2), vec(1), blk,
         pl.BlockSpec((None, HGRN_HEADS_PER_STEP, B_HEAD, B_HEAD), lambda hh, c: (chunk(c), hh, 0, 0)),
         blk],
        [blk, blk, blk, blk, vec(1), vec(1)],
        [act, act, act, act, row, row],
        [pltpu.VMEM((HGRN_HEADS_PER_STEP, B_HEAD, B_HEAD), F32)] + [pltpu.VMEM((CHUNK, w), F32)] * 6,
        [proj, proj, proj, proj, hlb, wn, o_raw, states, d_out], comm)


def _mixer_fwd(x1, mix_norm, w_in, sinks, hlb, wn, weights, tables, comms):
    t, d = x1.shape
    n_in = w_in.shape[1]
    nb = d // N_DEV
    sent = {}
    h2 = _rmsnorm("mix_norm", x1, mix_norm)
    tm = _tile(t, 1024, 16)
    tn = _tile(n_in, 512, 128)
    (proj,), sent["proj"] = _mmc(
        "mix_proj", (t // tm, n_in // tn, 1),
        [(h2, (tm, d), lambda i, j, k: (i, 0)), (w_in, (d, tn), lambda i, j, k: (0, j))],
        [(0, 1, 0, NN)], [(tm, tn)], lambda accs, refs: accs,
        [((t, n_in), F32, (tm, tn), lambda i, j, k: (i, j), False)], comm=comms.get("proj"))
    out_a = _attention_fwd(proj, tables, sinks)
    (out_b, o_raw, states), sent["hgrn"] = _hgrn_fwd(proj, hlb, wn, comms.get("hgrn"))
    wua, wub, w_out = weights(sent)

    def epi_merge(accs, refs):
        ua, ub = accs
        return [_sig(refs[4][...]) * ua + _sig(refs[5][...]) * ub, ua, ub]

    wide = ((t, d), BF16, (tm, nb), lambda i, j, k: (i, j), False)
    merged, ua, ub = _mm(
        "mix_merge", (t // tm, N_DEV, 1),
        [(out_a, (tm, A_WIDTH), lambda i, j, k: (i, 0)),
         (wua, (None, A_WIDTH, nb), lambda i, j, k: (j, 0, 0)),
         (out_b, (tm, B_WIDTH), lambda i, j, k: (i, 0)),
         (wub, (None, B_WIDTH, nb), lambda i, j, k: (j, 0, 0)),
         (proj, (tm, nb), lambda i, j, k: (i, OFF_GA // nb + j)),
         (proj, (tm, nb), lambda i, j, k: (i, (OFF_GA + d) // nb + j))],
        [(0, 1, 0, NN), (2, 3, 1, NN)], [(tm, nb)] * 2, epi_merge, [wide] * 3)

    tm2 = _tile(t, 512, 8)
    (x2,) = _mm(
        "mix_out", (t // tm2, 1, 1),
        [(merged, (tm2, d), lambda i, j, k: (i, 0)), (w_out, (d, d), lambda i, j, k: (0, 0)),
         (x1, (tm2, d), lambda i, j, k: (i, 0))],
        [(0, 1, 0, NN)], [(tm2, d)], lambda accs, refs: [refs[2][...] + accs[0]],
        [((t, d), F32, (tm2, d), lambda i, j, k: (i, 0), False)])
    saved = dict(h2=h2, proj=proj, out_a=out_a, out_b=out_b, o_raw=o_raw, states=states,
                 merged=merged, ua=ua, ub=ub)
    return x2, saved, sent


def _mixer_bwd(dx2, x1, mix_norm, w_in, sinks, hlb, wn, wua, wub, w_out, tables, sv, comms):
    t, d = x1.shape
    sent = {}
    avail = dict(sent=sent)
    n_in = w_in.shape[1]
    nb = d // N_DEV
    proj = sv["proj"]
    tm = _tile(t, 512, 16)
    tn = _tile(d, 512, 128)

    def epi_gate(accs, refs):
        dm = accs[0]
        sa = _sig(refs[2][...])
        sb = _sig(refs[3][...])
        ua = refs[4][...].astype(F32)
        ub = refs[5][...].astype(F32)
        return [dm * sa, dm * sb, dm * ua * sa * (1.0 - sa), dm * ub * sb * (1.0 - sb)]

    wide = ((t, d), BF16, (tm, tn), lambda i, j, k: (i, j), False)
    dua, dub, dga, dgb = _mm(
        "mix_dmerge", (t // tm, d // tn, 1),
        [(dx2, (tm, d), lambda i, j, k: (i, 0)),
         (w_out, (tn, d), lambda i, j, k: (j, 0)),
         (proj, (tm, tn), lambda i, j, k: (i, OFF_GA // tn + j)),
         (proj, (tm, tn), lambda i, j, k: (i, (OFF_GA + d) // tn + j)),
         (sv["ua"], (tm, tn), lambda i, j, k: (i, j)),
         (sv["ub"], (tm, tn), lambda i, j, k: (i, j))],
        [(0, 1, 0, NT)], [(tm, tn)], epi_gate, [wide] * 4)

    tk = _tile(t, 512, 16)
    (dw_out,) = _mm(
        "mix_dwout", (d // tn, 1, t // tk),
        [(sv["merged"], (tk, tn), lambda i, j, k: (k, i)), (dx2, (tk, d), lambda i, j, k: (k, 0))],
        [(0, 1, 0, TN)], [(tn, d)], lambda accs, refs: accs,
        [((d, d), BF16, (tn, d), lambda i, j, k: (i, 0), False)])

    w_spec = lambda rows: ((N_DEV, rows, nb), BF16, (None, rows, nb), lambda i, j, k: (j, 0, 0), False)
    dwua, dwub = _mm(
        "mix_dwup", (1, N_DEV, t // tk),
        [(sv["out_a"], (tk, A_WIDTH), lambda i, j, k: (k, 0)),
         (dua, (tk, nb), lambda i, j, k: (k, j)),
         (sv["out_b"], (tk, B_WIDTH), lambda i, j, k: (k, 0)),
         (dub, (tk, nb), lambda i, j, k: (k, j))],
        [(0, 1, 0, TN), (2, 3, 1, TN)], [(A_WIDTH, nb), (B_WIDTH, nb)], lambda accs, refs: accs,
        [w_spec(A_WIDTH), w_spec(B_WIDTH)])

    d_out_a, d_out_b = _mm(
        "mix_dup", (t // tm, 1, N_DEV),
        [(dua, (tm, nb), lambda i, j, k: (i, k)),
         (wua, (None, A_WIDTH, nb), lambda i, j, k: (k, 0, 0)),
         (dub, (tm, nb), lambda i, j, k: (i, k)),
         (wub, (None, B_WIDTH, nb), lambda i, j, k: (k, 0, 0))],
        [(0, 1, 0, NT), (2, 3, 1, NT)], [(tm, A_WIDTH), (tm, B_WIDTH)], lambda accs, refs: accs,
        [((t, A_WIDTH), BF16, (tm, A_WIDTH), lambda i, j, k: (i, 0), False),
         ((t, B_WIDTH), BF16, (tm, B_WIDTH), lambda i, j, k: (i, 0), False)])

    avail.update(w_out=dw_out.reshape(N_DEV, d // N_DEV, d), w_up_a=dwua, w_up_b=dwub)
    (dq_a, dk_a, dv_a, dsinks), sent["attention"] = _attention_bwd(
        proj, tables, sinks, d_out_a, _carried(comms, "attention", avail))
    (dq_b, df_b, di_b, dog_b, dlb, dwn), sent["hgrn"] = _hgrn_bwd(
        proj, hlb, wn, sv["o_raw"], sv["states"], d_out_b, _carried(comms, "hgrn", avail))
    dproj = jnp.concatenate([dq_a, dk_a, dv_a, dq_b, df_b, di_b, dog_b, dga, dgb], axis=1)

    tdm = _tile(d, 1024, 128)
    tpn = _tile(n_in, 512, 128)
    (dw_in,) = _mm(
        "mix_dwin", (d // tdm, n_in // tpn, t // tk),
        [(sv["h2"], (tk, tdm), lambda i, j, k: (k, i)), (dproj, (tk, tpn), lambda i, j, k: (k, j))],
        [(0, 1, 0, TN)], [(tdm, tpn)], lambda accs, refs: accs,
        [((d, n_in), BF16, (tdm, tpn), lambda i, j, k: (i, j), False)])

    avail.update(w_in=dw_in.reshape(d, N_DEV, n_in // N_DEV).transpose(1, 0, 2))
    tm3 = _tile(t, 256, 8)

    def epi_dx(accs, refs):
        dx, dw = _rms_bwd(accs[0], refs[2][...], refs[3][...], refs[4][...])
        return [dx, dw]

    (dx1, dmix), sent["dx"] = _mmc(
        "mix_dx", (t // tm3, 1, n_in // tpn),
        [(dproj, (tm3, tpn), lambda i, j, k: (i, k)), (w_in, (d, tpn), lambda i, j, k: (0, k)),
         (x1, (tm3, d), lambda i, j, k: (i, 0)), (mix_norm, (1, d), lambda i, j, k: (0, 0)),
         (dx2, (tm3, d), lambda i, j, k: (i, 0))],
        [(0, 1, 0, NT)], [(tm3, d)], epi_dx,
        [((t, d), F32, (tm3, d), lambda i, j, k: (i, 0), False),
         ((1, d), F32, (1, d), lambda i, j, k: (0, 0), True)],
        comm=_carried(comms, "dx", avail))
    small = dict(mix_norm=dmix, attn_sinks=dsinks, hgrn_lb=dlb, hgrn_norm=dwn)
    return dx1, small, avail


def _ple_fwd(x3, ple_norm, p16, wpg, wpp):
    t, d = x3.shape
    nb = d // N_DEV
    h4 = _rmsnorm("ple_norm", x3, ple_norm)
    tm = _tile(t, 1024, 16)

    def epi(accs, refs):
        zg, pp = accs
        return [refs[4][...] + _sig(zg) * pp, zg, pp]

    blk = lambda dt: ((t, d), dt, (tm, nb), lambda i, j, k: (i, j), False)
    x4, zg, pp = _mm(
        "ple_fwd", (t // tm, N_DEV, 1),
        [(h4, (tm, d), lambda i, j, k: (i, 0)), (wpg, (d, nb), lambda i, j, k: (0, j)),
         (p16, (tm, PLE_DIM), lambda i, j, k: (i, 0)),
         (wpp, (None, PLE_DIM, nb), lambda i, j, k: (j, 0, 0)),
         (x3, (tm, nb), lambda i, j, k: (i, j))],
        [(0, 1, 0, NN), (2, 3, 1, NN)], [(tm, nb)] * 2, epi, [blk(F32), blk(BF16), blk(BF16)])
    return x4, dict(h4=h4, zg=zg, pp=pp)


def _ple_bwd(dx4, x3, ple_norm, p16, wpg, sv):
    t, d = x3.shape
    nb = d // N_DEV

    def gate_grads(dx, zg, pp):
        s = _sig(zg.astype(F32))
        return dx * pp.astype(F32) * s * (1.0 - s), dx * s

    dzg, dpp = _rows("ple_dgate", gate_grads, [(dx4, True), (sv["zg"], True), (sv["pp"], True)],
                     [((t, d), BF16), ((t, d), BF16)], _tile(t, 512, 16))
    tk = _tile(t, 512, 16)
    (dwpp,) = _mm(
        "ple_dwproj", (1, N_DEV, t // tk),
        [(p16, (tk, PLE_DIM), lambda i, j, k: (k, 0)), (dpp, (tk, nb), lambda i, j, k: (k, j))],
        [(0, 1, 0, TN)], [(PLE_DIM, nb)], lambda accs, refs: accs,
        [((N_DEV, PLE_DIM, nb), BF16, (None, PLE_DIM, nb), lambda i, j, k: (j, 0, 0), False)])
    tn = _tile(d, 1024, 128)
    (dwpg,) = _mm(
        "ple_dwgate", (d // tn, 1, t // tk),
        [(sv["h4"], (tk, tn), lambda i, j, k: (k, i)), (dzg, (tk, d), lambda i, j, k: (k, 0))],
        [(0, 1, 0, TN)], [(tn, d)], lambda accs, refs: accs,
        [((d, d), BF16, (tn, d), lambda i, j, k: (i, 0), False)])
    tm3 = _tile(t, 256, 8)

    def epi_dx(accs, refs):
        dx, dw = _rms_bwd(accs[0], refs[2][...], refs[3][...], refs[4][...])
        return [dx, dw]

    dx3, dnorm = _mm(
        "ple_dx", (t // tm3, 1, 1),
        [(dzg, (tm3, d), lambda i, j, k: (i, 0)), (wpg, (d, d), lambda i, j, k: (0, 0)),
         (x3, (tm3, d), lambda i, j, k: (i, 0)), (ple_norm, (1, d), lambda i, j, k: (0, 0)),
         (dx4, (tm3, d), lambda i, j, k: (i, 0))],
        [(0, 1, 0, NT)], [(tm3, d)], epi_dx,
        [((t, d), F32, (tm3, d), lambda i, j, k: (i, 0), False),
         ((1, d), F32, (1, d), lambda i, j, k: (0, 0), True)])
    return dx3, dwpp, dwpg, dnorm


def _loss_head(x4, final_norm, target):
    t, d = x4.shape
    tm = _tile(t, 256, 8)

    def body(x_ref, w_ref, tgt_ref, dx_ref, dw_ref, loss_ref):
        i = pl.program_id(0)
        xv = x_ref[...]
        wv = w_ref[...]
        r = lax.rsqrt(jnp.mean(xv * xv, axis=-1, keepdims=True) + EPS)
        err = xv * r * wv - tgt_ref[...]
        part = 0.5 * jnp.sum(jnp.mean(err * err, axis=-1, keepdims=True), axis=0, keepdims=True)
        dx, dw = _rms_bwd(err * (1.0 / d), xv, wv, jnp.zeros_like(xv))
        dx_ref[...] = dx

        @pl.when(i == 0)
        def _():
            dw_ref[...] = dw
            loss_ref[...] = jnp.broadcast_to(part, loss_ref.shape)

        @pl.when(i > 0)
        def _():
            dw_ref[...] += dw
            loss_ref[...] += jnp.broadcast_to(part, loss_ref.shape)

    return pl.pallas_call(
        body,
        name="loss_head",
        grid=(t // tm,),
        in_specs=[pl.BlockSpec((tm, d), lambda i: (i, 0)), pl.BlockSpec((1, d), lambda i: (0, 0)),
                  pl.BlockSpec((tm, d), lambda i: (i, 0))],
        out_specs=[pl.BlockSpec((tm, d), lambda i: (i, 0)), pl.BlockSpec((1, d), lambda i: (0, 0)),
                   pl.BlockSpec((1, LANES), lambda i: (0, 0))],
        out_shape=[jax.ShapeDtypeStruct((t, d), F32), jax.ShapeDtypeStruct((1, d), F32),
                   jax.ShapeDtypeStruct((1, LANES), F32)],
        compiler_params=_params(("arbitrary",)),
    )(x4, final_norm, target)


RELATIONS = ((0, 0), (1, 0), (0, 1), (1, 1))


def _place():
    return lax.axis_index("x"), lax.axis_index("y"), lax.axis_index("c")


def _flip(v, bit):
    return 1 - v if bit else v


def _gather_program(shards):
    n = len(shards)

    def copies(ins, outs, send_sems, recv_sems, local_sems, base):
        x, y, c = _place()
        sibling = (x, y, 1 - c)

        def block(a, rel, core):
            return outs[a].at[4 * _flip(x, rel[0]) + 2 * _flip(y, rel[1]) + core]

        def copy(a, slot, rel, core, to, src=None):
            dst = block(a, rel, core)
            return pltpu.make_async_remote_copy(
                src_ref=dst if src is None else src, dst_ref=dst,
                send_sem=send_sems.at[base[0] + 7 * a + slot], recv_sem=recv_sems.at[base[1] + 7 * a + slot],
                device_id=to, device_id_type=MESH)

        own, first = [], []
        for a in range(n):
            own.append(pltpu.make_async_copy(ins[a], block(a, RELATIONS[0], c), local_sems.at[base[2] + a]))
            first.append(copy(a, 0, RELATIONS[0], c, sibling, src=ins[a]))
            for r in (1, 2, 3):
                rel = RELATIONS[r]
                first.append(copy(a, r, RELATIONS[0], c, (_flip(x, rel[0]), _flip(y, rel[1]), c), src=ins[a]))
        return own, first, copy, sibling, c

    def pre(*args):
        own, first, _, _, _ = copies(*args)
        for cp in own + first:
            cp.start()

    def post(*args):
        own, first, copy, sibling, c = copies(*args)
        passed = []
        for r in (1, 2, 3):
            for a in range(n):
                copy(a, r, RELATIONS[r], c, sibling).wait_recv()
                fwd = copy(a, 3 + r, RELATIONS[r], c, sibling)
                fwd.start()
                passed.append(fwd)
        for a in range(n):
            copy(a, 0, RELATIONS[0], 1 - c, sibling).wait_recv()
            for r in (1, 2, 3):
                copy(a, 3 + r, RELATIONS[r], 1 - c, sibling).wait_recv()
        for cp in first + passed:
            cp.wait_send()
        for mine in own:
            mine.wait()

    return _Comm(list(shards), [jax.ShapeDtypeStruct((N_DEV,) + s.shape, s.dtype) for s in shards],
                 (7 * n, 7 * n, n), pre, post)


def _exchange_program(arrays, n_slots, source, target, slot):
    n = len(arrays)

    def copies(ins, outs, send_sems, recv_sems, local_sems, base):
        place = _place()
        res = []
        for a in range(n):
            for r in range(n_slots):
                src = source(r, place)
                if src is None:
                    continue
                res.append(pltpu.make_async_remote_copy(
                    src_ref=ins[a].at[src], dst_ref=outs[a].at[slot(r)],
                    send_sem=send_sems.at[base[0] + n_slots * a + r],
                    recv_sem=recv_sems.at[base[1] + n_slots * a + r],
                    device_id=target(r, place), device_id_type=MESH))
        return res

    def pre(*args):
        for cp in copies(*args):
            cp.start()

    def post(*args):
        for cp in copies(*args):
            cp.wait()

    n_out = len({slot(r) for r in range(n_slots) if source(r, (0, 0, 0)) is not None})
    return _Comm(list(arrays), [jax.ShapeDtypeStruct((n_out,) + g.shape[1:], g.dtype) for g in arrays],
                 (n_slots * n, n_slots * n, 0), pre, post)


def _sibling_program(grads):
    def source(r, place):
        x, y, c = place
        return 4 * _flip(x, RELATIONS[r][0]) + 2 * _flip(y, RELATIONS[r][1]) + (1 - c)

    return _exchange_program(grads, 4, source, lambda r, p: (p[0], p[1], 1 - p[2]), lambda r: r)


def _chips_program(sums):
    def target(r, place):
        x, y, c = place
        return (_flip(x, RELATIONS[r][0]), _flip(y, RELATIONS[r][1]), c)

    return _exchange_program(sums, 4, lambda r, p: r if r else None, target, lambda r: r - 1)


def _standalone(name, comm):
    return _call(name, lambda: None, (1,), [], [], [], [], [], comm)[1]


def _pair_sum(name, grad, from_sibling, block_ids):
    _, rows, cols = grad.shape
    tr = _tile(rows, 512, 16)

    def body(ids_ref, mine_ref, sib_ref, o_ref):
        o_ref[...] = (mine_ref[...].astype(F32) + sib_ref[...].astype(F32)).astype(o_ref.dtype)

    return pl.pallas_call(
        body,
        name=name,
        grid_spec=pltpu.PrefetchScalarGridSpec(
            num_scalar_prefetch=1,
            grid=(4, rows // tr),
            in_specs=[pl.BlockSpec((None, tr, cols), lambda r, i, ids: (ids[r], i, 0)),
                      pl.BlockSpec((None, tr, cols), lambda r, i, ids: (r, i, 0))],
            out_specs=pl.BlockSpec((None, tr, cols), lambda r, i, ids: (r, i, 0))),
        out_shape=jax.ShapeDtypeStruct((4, rows, cols), BF16),
        compiler_params=_params(("arbitrary", "arbitrary")),
    )(block_ids, grad, from_sibling)


def _adam(w, g, m, v):
    m = ADAM_B1 * m + (1.0 - ADAM_B1) * g
    v = ADAM_B2 * v + (1.0 - ADAM_B2) * (g * g)
    m_hat = m / (1.0 - ADAM_B1 ** ADAM_STEP)
    v_hat = v / (1.0 - ADAM_B2 ** ADAM_STEP)
    delta = -ADAM_LR * (m_hat / (jnp.sqrt(v_hat) + ADAM_EPS) + ADAM_WD * w)
    return delta, m, v


def _adam_sharded(name, items, comm=None):
    rows, cols = items[0][0].shape
    n = len(items)
    tr = _tile(rows, max(64, 256 // n), 8)

    def body(*refs):
        for k in range(n):
            w_ref, m_ref, v_ref, s_ref, r1_ref, r2_ref, r3_ref = refs[7 * k:7 * k + 7]
            g_ref, d_ref, nm_ref, nv_ref = refs[7 * n + 4 * k:7 * n + 4 * k + 4]
            g = ((s_ref[...].astype(F32) + r1_ref[...].astype(F32)) + r2_ref[...].astype(F32)) + r3_ref[...].astype(F32)
            delta, nm, nv = _adam(w_ref[...], g, m_ref[...], v_ref[...])
            g_ref[...] = g
            d_ref[...] = delta
            nm_ref[...] = nm
            nv_ref[...] = nv

    flat = pl.BlockSpec((tr, cols), lambda i: (i, 0))
    slot = lambda s: pl.BlockSpec((None, tr, cols), lambda i: (s, i, 0))
    args = []
    for w, m, v, sums, from_chips in items:
        args += [w, m, v, sums, from_chips, from_chips, from_chips]
    res, sent = _call(
        name, body, (rows // tr,),
        [flat, flat, flat, slot(0), slot(0), slot(1), slot(2)] * n, [flat] * (4 * n),
        [jax.ShapeDtypeStruct((rows, cols), F32)] * (4 * n), [], args, comm)
    return [res[4 * k:4 * k + 4] for k in range(n)], sent


def _all_reduce_small(part):
    rows = part.shape[0]

    def body(x_ref, o_ref, gathered, send_sems, recv_sems):
        x, y, c = _place()
        me = 4 * x + 2 * y + c
        gathered[me] = x_ref[...]
        copies = []
        for k in range(1, N_DEV):
            bits = (k >> 2 & 1, k >> 1 & 1, k & 1)
            peer = (_flip(x, bits[0]), _flip(y, bits[1]), _flip(c, bits[2]))
            copies.append(pltpu.make_async_remote_copy(
                src_ref=x_ref, dst_ref=gathered.at[me],
                send_sem=send_sems.at[k - 1], recv_sem=recv_sems.at[k - 1],
                device_id=peer, device_id_type=MESH))
        for cp in copies:
            cp.start()
        for cp in copies:
            cp.wait()
        total = gathered[0]
        for b in range(1, N_DEV):
            total = total + gathered[b]
        o_ref[...] = total

    return pl.pallas_call(
        body,
        name="small_all_reduce",
        in_specs=[pl.BlockSpec(memory_space=pltpu.VMEM)],
        out_specs=pl.BlockSpec(memory_space=pltpu.VMEM),
        out_shape=jax.ShapeDtypeStruct((rows, LANES), F32),
        scratch_shapes=[pltpu.VMEM((N_DEV, rows, LANES), F32),
                        pltpu.SemaphoreType.DMA((N_DEV - 1,)), pltpu.SemaphoreType.DMA((N_DEV - 1,))],
    )(part)


def _adam_small(w, g, m, v):
    def fn(wv, gv, mv, vv):
        return _adam(wv, gv, mv, vv)

    shape = (w.shape, F32)
    return _rows("adam_small", fn, [(w, True), (g, True), (m, True), (v, True)], [shape] * 3, w.shape[0])


SMALL = ("ffn1_norm", "mix_norm", "attn_sinks", "hgrn_lower_bound", "hgrn_norm", "ffn2_norm", "ple_norm",
         "final_norm")
LARGE = ("ffn1_w_gate", "ffn1_w_up", "ffn1_w_down", "w_in", "w_up_a", "w_up_b", "w_out",
         "ffn2_w_gate", "ffn2_w_up", "ffn2_w_down", "ple_w_gate", "ple_w_proj")
WEIGHTS = ("ffn1_norm", "ffn1_w_gate", "ffn1_w_up", "ffn1_w_down", "mix_norm", "w_in", "attn_sinks",
           "hgrn_lower_bound", "hgrn_norm", "w_up_a", "w_up_b", "w_out", "ffn2_norm", "ffn2_w_gate",
           "ffn2_w_up", "ffn2_w_down", "ple_norm", "ple_w_gate", "ple_w_proj", "final_norm")


def _pack_rows(arrays):
    rows = []
    for a in arrays:
        flat = a.reshape(-1).astype(F32)
        pad = -flat.shape[0] % LANES
        rows.append(jnp.pad(flat, (0, pad)).reshape(-1, LANES))
    packed = jnp.concatenate(rows, axis=0)
    return jnp.pad(packed, ((0, -packed.shape[0] % 8), (0, 0)))


def _unpack_rows(packed, like):
    out, at = [], 0
    for a in like:
        size = a.size
        n_rows = -(-size // LANES)
        out.append(packed[at:at + n_rows].reshape(-1)[:size].reshape(a.shape))
        at += n_rows
    return out


def kernel(x, p, positions, ffn1_norm, ffn1_w_gate, ffn1_w_up, ffn1_w_down, mix_norm, w_in, attn_sinks, hgrn_lower_bound, hgrn_norm, w_up_a, w_up_b, w_out, ffn2_norm, ffn2_w_gate, ffn2_w_up, ffn2_w_down, ple_norm, ple_w_gate, ple_w_proj, final_norm, loss_target, m_ffn1_norm, m_ffn1_w_gate, m_ffn1_w_up, m_ffn1_w_down, m_mix_norm, m_w_in, m_attn_sinks, m_hgrn_lower_bound, m_hgrn_norm, m_w_up_a, m_w_up_b, m_w_out, m_ffn2_norm, m_ffn2_w_gate, m_ffn2_w_up, m_ffn2_w_down, m_ple_norm, m_ple_w_gate, m_ple_w_proj, m_final_norm, v_ffn1_norm, v_ffn1_w_gate, v_ffn1_w_up, v_ffn1_w_down, v_mix_norm, v_w_in, v_attn_sinks, v_hgrn_lower_bound, v_hgrn_norm, v_w_up_a, v_w_up_b, v_w_out, v_ffn2_norm, v_ffn2_w_gate, v_ffn2_w_up, v_ffn2_w_down, v_ple_norm, v_ple_w_gate, v_ple_w_proj, v_final_norm):
    given = dict(locals())
    w = {n: given[n] for n in WEIGHTS}
    mom = {n: given["m_" + n] for n in WEIGHTS}
    var = {n: given["v_" + n] for n in WEIGHTS}
    t, d = x.shape[1], x.shape[2]

    x0, p16, target = x[0], p[0, 0].astype(BF16), loss_target[0]
    tables = _rope_tables(positions)
    sinks = attn_sinks.reshape(-1)
    final_w = final_norm.reshape(1, d)
    shard = {n: w[n][0].astype(BF16) for n in LARGE}
    xi, yi, ci = _place()
    block_ids = jnp.stack([4 * _flip(xi, rx) + 2 * _flip(yi, ry) + ci for rx, ry in RELATIONS]).astype(jnp.int32)

    def gather(*names):
        return _gather_program([shard[n] for n in names])

    wg1, wu1 = _standalone("gather_ffn1", gather("ffn1_w_gate", "ffn1_w_up"))
    h1 = _rmsnorm("ffn1_norm", x0, ffn1_norm)
    (g1, u1, a1), (wd1, win_g) = _ffn_up("ffn1", h1, wg1, wu1, gather("ffn1_w_down", "w_in"))
    win = win_g.transpose(1, 0, 2).reshape(d, -1)
    x1, (wua, wub, wout_g) = _ffn_down("ffn1", x0, a1, wd1, gather("w_up_a", "w_up_b", "w_out"))
    wout = wout_g.reshape(d, d)
    mix_args = (mix_norm, win, sinks, hgrn_lower_bound, hgrn_norm)
    x2, mix_saved, got_w = _mixer_fwd(
        x1, *mix_args, lambda sent: (wua, wub, wout), tables,
        dict(proj=gather("ffn2_w_gate", "ffn2_w_up"), hgrn=gather("ffn2_w_down", "ple_w_gate", "ple_w_proj")))
    wg2, wu2 = got_w["proj"]
    wd2, wpg_g, wpp = got_w["hgrn"]
    wpg = wpg_g.reshape(d, d)
    h3 = _rmsnorm("ffn2_norm", x2, ffn2_norm)
    (g2, u2, a2), _ = _ffn_up("ffn2", h3, wg2, wu2)
    x3, _ = _ffn_down("ffn2", x2, a2, wd2)
    x4, ple_saved = _ple_fwd(x3, ple_norm, p16, wpg, wpp)
    dx4, d_final, loss_row = _loss_head(x4, final_w, target)

    sums, from_chips = {}, {}

    def to_chips(names, grads, from_sibling):
        for n, g, s in zip(names, grads, from_sibling):
            sums[n] = _pair_sum("pair_sum_" + n, g, s, block_ids)
        return _chips_program([sums[n] for n in names])

    def arrived(names, results):
        from_chips.update(zip(names, results))

    dx3, dwpp, dwpg, d_ple = _ple_bwd(dx4, x3, ple_norm, p16, wpg, ple_saved)
    ple_names = ("ple_w_proj", "ple_w_gate")
    ple_grads = [dwpp, dwpg.reshape(N_DEV, d // N_DEV, d)]
    gu2 = ("ffn2_w_gate", "ffn2_w_up")
    dx2, dwg2, dwu2, dwd2, d_ffn2, sent2 = _ffn_bwd(
        "ffn2b", dx3, x2, h3, g2, u2, a2, wg2, wu2, wd2, ffn2_norm,
        dict(dact=_sibling_program(ple_grads),
             dwgu=lambda av: to_chips(ple_names, ple_grads, av["sent"]["dact"]),
             dwd=lambda av: _sibling_program([av["dwg"], av["dwu"]]),
             dx=lambda av: _merge([to_chips(gu2, [av["dwg"], av["dwu"]], av["sent"]["dwd"]),
                                   _sibling_program([av["dwd"]])])))
    arrived(ple_names, sent2["dwgu"])
    arrived(gu2, sent2["dx"][:2])
    mix3 = ("w_out", "w_up_a", "w_up_b")
    dx1, mix_small, mix_made = _mixer_bwd(
        dx2, x1, *mix_args, wua, wub, wout, tables, mix_saved,
        dict(attention=lambda av: _merge([to_chips(("ffn2_w_down",), [dwd2], sent2["dx"][2:]),
                                          _sibling_program([av[n] for n in mix3])]),
             hgrn=lambda av: to_chips(mix3, [av[n] for n in mix3], av["sent"]["attention"][1:]),
             dx=lambda av: _sibling_program([av["w_in"]])))
    sent_mix = mix_made["sent"]
    arrived(("ffn2_w_down",), sent_mix["attention"][:1])
    arrived(mix3, sent_mix["hgrn"])
    gu1 = ("ffn1_w_gate", "ffn1_w_up")
    grad_x, dwg1, dwu1, dwd1, d_ffn1, sent1 = _ffn_bwd(
        "ffn1b", dx1, x0, h1, g1, u1, a1, wg1, wu1, wd1, ffn1_norm,
        dict(dact=to_chips(("w_in",), [mix_made["w_in"]], sent_mix["dx"]),
             dwd=lambda av: _sibling_program([av["dwg"], av["dwu"]]),
             dx=lambda av: _merge([to_chips(gu1, [av["dwg"], av["dwu"]], av["sent"]["dwd"]),
                                   _sibling_program([av["dwd"]])])))
    arrived(("w_in",), sent1["dact"])
    arrived(gu1, sent1["dx"][:2])
    arrived(("ffn1_w_down",), _standalone("grads_tail", to_chips(("ffn1_w_down",), [dwd1], sent1["dx"][2:])))

    out = {}
    for n in LARGE:
        shape = w[n].shape
        flat = lambda a: a.reshape(shape[1], shape[2])
        (res,), _ = _adam_sharded("adam_" + n, [(flat(w[n]), flat(mom[n]), flat(var[n]), sums[n], from_chips[n])])
        out[n] = [a.reshape(shape) for a in res]

    dlb = mix_small["hgrn_lb"]
    small_grads = dict(
        ffn1_norm=d_ffn1, mix_norm=mix_small["mix_norm"], attn_sinks=mix_small["attn_sinks"][:, :A_HEADS],
        hgrn_lower_bound=jnp.concatenate([dlb, -dlb], axis=0), hgrn_norm=mix_small["hgrn_norm"],
        ffn2_norm=d_ffn2, ple_norm=d_ple, final_norm=d_final)

    small_like = [w[n] for n in SMALL]
    packed = _pack_rows([small_grads[n].reshape(w[n].shape) for n in SMALL] + [loss_row[:, :1]])
    total = _all_reduce_small(packed)
    n_rows = total.shape[0]
    pack_w = _pack_rows(small_like + [jnp.zeros((1, 1), F32)])
    pack_m = _pack_rows([mom[n] for n in SMALL] + [jnp.zeros((1, 1), F32)])
    pack_v = _pack_rows([var[n] for n in SMALL] + [jnp.zeros((1, 1), F32)])
    deltas = _adam_small(pack_w, total, pack_m, pack_v)
    like = small_like + [jnp.zeros((1, 1), F32)]
    g_small = _unpack_rows(total, like)
    d_small, m_small, v_small = (_unpack_rows(a, like) for a in deltas)
    for k, n in enumerate(SMALL):
        out[n] = [g_small[k], d_small[k], m_small[k], v_small[k]]
    loss = g_small[-1].reshape(())

    return (loss, grad_x.reshape(x.shape),
            *[out[n][0] for n in WEIGHTS], *[out[n][1] for n in WEIGHTS],
            *[out[n][2] for n in WEIGHTS], *[out[n][3] for n in WEIGHTS])
```

```python
import functools

import jax
import jax.numpy as jnp
from jax import lax
from jax.experimental import pallas as pl
from jax.experimental.pallas import tpu as pltpu

F32 = jnp.float32
BF16 = jnp.bfloat16
MESH = pl.DeviceIdType.MESH

EPS = 1e-6
N_DEV = 8
A_HEADS = 16
A_HEAD_DIM = 64
A_WIDTH = 1024
A_KV_WIDTH = 256
ATT_BLOCK = 128
ROT_DIM = 16
ROPE_THETA = 500000.0
B_WIDTH = 1024
B_HEAD = 128
CHUNK = 64
SUB = 16
MAX_DECAY_EXP = 60.0
PLE_DIM = 256
LANES = 128

ADAM_LR = 0.001
ADAM_B1 = 0.9
ADAM_B2 = 0.999
ADAM_EPS = 1e-08
ADAM_WD = 0.01
ADAM_STEP = 10

VMEM_LIMIT = 56 * 1024 * 1024
EPILOGUE_ELEMS = 256 * 1024

NN = (((1,), (0,)), ((), ()))
NT = (((1,), (1,)), ((), ()))
TN = (((0,), (0,)), ((), ()))

OFF_QA = 0
OFF_KA = 1024
OFF_VA = 1280
OFF_QB = 1536
OFF_FB = 2560
OFF_IB = 3584
OFF_OG = 4608
OFF_GA = 5632


def _sig(v):
    return 1.0 / (1.0 + jnp.exp(-v))


def _tile(n, pref, mult):
    t = min(n, pref)
    t -= t % mult
    while n % t:
        t -= mult
    return t


def _params(sem):
    return pltpu.CompilerParams(dimension_semantics=sem, vmem_limit_bytes=VMEM_LIMIT)


HBM_SPEC = pl.BlockSpec(memory_space=pltpu.HBM)


class _Comm:
    def __init__(self, ins, out_shapes, n_sems, pre, post):
        self.ins, self.out_shapes, self.n_sems, self.pre, self.post = ins, out_shapes, n_sems, pre, post
        self.parts = [len(out_shapes)]

    def split(self, outs):
        res, at = [], 0
        for n in self.parts:
            res.append(list(outs[at:at + n]))
            at += n
        return res


def _merge(progs):
    spans, ins, shapes, sems = [], [], [], [0, 0, 0]
    for p in progs:
        spans.append((len(ins), len(shapes), tuple(sems)))
        ins += list(p.ins)
        shapes += list(p.out_shapes)
        sems = [a + b for a, b in zip(sems, p.n_sems)]

    def run(which):
        def go(cin, cout, send, recv, local, base):
            for p, (i0, o0, s0) in zip(progs, spans):
                getattr(p, which)(cin[i0:i0 + len(p.ins)], cout[o0:o0 + len(p.out_shapes)], send, recv, local,
                                  tuple(b + s for b, s in zip(base, s0)))
        return go

    merged = _Comm(ins, shapes, tuple(sems), run("pre"), run("post"))
    merged.parts = [len(p.out_shapes) for p in progs]
    return merged


def _call(name, body, grid, in_specs, out_specs, out_shape, scratch, args, comm=None):
    sem = ("arbitrary",) * len(grid)
    if comm is None:
        res = pl.pallas_call(body, name=name, grid=grid, in_specs=in_specs, out_specs=out_specs,
                             out_shape=out_shape, scratch_shapes=scratch, compiler_params=_params(sem))(*args)
        return list(res), []
    n_in, n_out, n_scr = len(in_specs), len(out_specs), len(scratch)
    c_in, c_out = len(comm.ins), len(comm.out_shapes)

    def carrier(*refs):
        at = [0]

        def take(n):
            at[0] += n
            return refs[at[0] - n:at[0]]

        ins, cins, outs, couts, scr, sems = take(n_in), take(c_in), take(n_out), take(c_out), take(n_scr), take(3)
        ids = [pl.program_id(a) for a in range(len(grid))]
        first = functools.reduce(jnp.logical_and, [i == 0 for i in ids])
        last = functools.reduce(jnp.logical_and, [i == g - 1 for i, g in zip(ids, grid)])

        @pl.when(first)
        def _():
            comm.pre(cins, couts, *sems, (0, 0, 0))

        body(*ins, *outs, *scr)

        @pl.when(last)
        def _():
            comm.post(cins, couts, *sems, (0, 0, 0))

    res = pl.pallas_call(
        carrier, name=name, grid=grid,
        in_specs=list(in_specs) + [HBM_SPEC] * c_in,
        out_specs=list(out_specs) + [HBM_SPEC] * c_out,
        out_shape=list(out_shape) + list(comm.out_shapes),
        scratch_shapes=list(scratch) + [pltpu.SemaphoreType.DMA((max(n, 1),)) for n in comm.n_sems],
        compiler_params=_params(sem),
    )(*args, *comm.ins)
    return list(res[:n_out]), list(res[n_out:])


def _mm(name, grid, ins, prods, acc_shapes, epi, outs, comm=None):
    n_in, n_out, nk = len(ins), len(outs), grid[2]
    tile_rows, tile_cols = acc_shapes[0]
    chunk = tile_rows
    while chunk * tile_cols > EPILOGUE_ELEMS and chunk % 32 == 0:
        chunk //= 2

    def body(*refs):
        in_refs = refs[:n_in]
        out_refs = refs[n_in:n_in + n_out]
        acc_refs = refs[n_in + n_out:]
        first_tile = jnp.logical_and(pl.program_id(0) == 0, pl.program_id(1) == 0)

        def product(lhs, rhs, dims):
            a = in_refs[lhs][...].astype(BF16)
            b = in_refs[rhs][...].astype(BF16)
            return lax.dot_general(a, b, dims, preferred_element_type=F32)

        def write(accs):
            totals = [None] * n_out
            for c in range(tile_rows // chunk):
                rows = slice(c * chunk, (c + 1) * chunk)
                vals = epi([a[rows] for a in accs], in_refs, rows)
                for idx, (o, v, spec) in enumerate(zip(out_refs, vals, outs)):
                    if spec[4]:
                        totals[idx] = v if totals[idx] is None else totals[idx] + v
                    else:
                        o[rows] = v.astype(o.dtype)
            for o, v, spec in zip(out_refs, totals, outs):
                if spec[4]:
                    @pl.when(first_tile)
                    def _():
                        o[...] = v.astype(o.dtype)

                    @pl.when(jnp.logical_not(first_tile))
                    def _():
                        o[...] += v.astype(o.dtype)

        if nk == 1:
            accs = [None] * len(acc_shapes)
            for lhs, rhs, acc, dims in prods:
                t = product(lhs, rhs, dims)
                accs[acc] = t if accs[acc] is None else accs[acc] + t
            write(accs)
        else:
            k = pl.program_id(2)

            @pl.when(k == 0)
            def _():
                for r in acc_refs:
                    r[...] = jnp.zeros(r.shape, F32)

            for lhs, rhs, acc, dims in prods:
                acc_refs[acc][...] += product(lhs, rhs, dims)

            @pl.when(k == nk - 1)
            def _():
                write(acc_refs)

    res, sent = _call(
        name, body, grid,
        [pl.BlockSpec(b, im) for _, b, im in ins],
        [pl.BlockSpec(o[2], o[3]) for o in outs],
        [jax.ShapeDtypeStruct(o[0], o[1]) for o in outs],
        [] if nk == 1 else [pltpu.VMEM(s, F32) for s in acc_shapes],
        [a for a, _, _ in ins], comm)
    return res if comm is None else (res, sent)


def _rms_bwd(dh, xv, w, dres):
    r = lax.rsqrt(jnp.mean(xv * xv, axis=-1, keepdims=True) + EPS)
    g = dh * w
    dx = dres + r * (g - xv * (r * r) * jnp.mean(g * xv, axis=-1, keepdims=True))
    dw = jnp.sum(dh * xv * r, axis=0, keepdims=True)
    return dx, dw


def _rows(name, fn, ins, outs, tm):
    n_in = len(ins)
    rows = outs[0][0][0]

    def body(*refs):
        vals = fn(*[r[...] for r in refs[:n_in]])
        for o, v in zip(refs[n_in:], vals):
            o[...] = v.astype(o.dtype)

    in_specs = []
    for a, tiled in ins:
        if tiled:
            in_specs.append(pl.BlockSpec((tm, a.shape[1]), lambda i: (i, 0)))
        else:
            in_specs.append(pl.BlockSpec(a.shape, lambda i, nd=a.ndim: (0,) * nd))
    return pl.pallas_call(
        body,
        name=name,
        grid=(rows // tm,),
        in_specs=in_specs,
        out_specs=[pl.BlockSpec((tm, s[1]), lambda i: (i, 0)) for s, _ in outs],
        out_shape=[jax.ShapeDtypeStruct(s, d) for s, d in outs],
        compiler_params=_params(("arbitrary",)),
    )(*[a for a, _ in ins])


def _rmsnorm(name, x, w):
    def fn(xv, wv):
        r = lax.rsqrt(jnp.mean(xv * xv, axis=-1, keepdims=True) + EPS)
        return (xv * r * wv,)

    return _rows(name, fn, [(x, True), (w, False)], [(x.shape, BF16)], _tile(x.shape[0], 512, 16))[0]


def _mmc(*args, comm=None):
    if comm is None:
        return _mm(*args), []
    return _mm(*args, comm=comm)


def _carried(comms, key, avail):
    c = comms.get(key) if comms else None
    return c(avail) if callable(c) else c


def _ffn_up(name, h, wg, wu, comm=None):
    t, d = h.shape
    n = wg.shape[2]
    tm = _tile(t, 1024, 16)

    def epi_up(accs, refs, rows):
        g, u = accs
        return [g, u, g * _sig(g) * u]

    hid = ((N_DEV, t, n), BF16, (None, tm, n), lambda i, j, k: (j, i, 0), False)
    return _mmc(
        name + "_up", (t // tm, N_DEV, 1),
        [(h, (tm, d), lambda i, j, k: (i, 0)),
         (wg, (None, d, n), lambda i, j, k: (j, 0, 0)),
         (wu, (None, d, n), lambda i, j, k: (j, 0, 0))],
        [(0, 1, 0, NN), (0, 2, 1, NN)], [(tm, n)] * 2, epi_up, [hid] * 3, comm=comm)


def _ffn_down(name, x, act, wd, comm=None):
    t, d = x.shape
    n = wd.shape[1]
    tm2 = _tile(t, 1024, 8)
    tn = _tile(d, 1024, 128)

    def epi_down(accs, refs, rows):
        return [refs[2][rows] + 0.5 * accs[0]]

    (x_out,), sent = _mmc(
        name + "_down", (t // tm2, d // tn, N_DEV),
        [(act, (None, tm2, n), lambda i, j, k: (k, i, 0)),
         (wd, (None, n, tn), lambda i, j, k: (k, 0, j)),
         (x, (tm2, tn), lambda i, j, k: (i, j))],
        [(0, 1, 0, NN)], [(tm2, tn)], epi_down,
        [((t, d), F32, (tm2, tn), lambda i, j, k: (i, j), False)], comm=comm)
    return x_out, sent


def _ffn_bwd(name, dxo, dxo16, x, h, gate, up, act, wg, wu, wd, norm_w, comms=None):
    t, d = x.shape
    n = wg.shape[2]
    tm = _tile(t, 1024, 16)
    sent = {}
    avail = dict(sent=sent)

    def epi_act(accs, refs, rows):
        da = 0.5 * accs[0]
        g = refs[2][rows].astype(F32)
        u = refs[3][rows].astype(F32)
        s = _sig(g)
        return [da * u * (s * (1.0 + g * (1.0 - s))), da * g * s]

    hid_in = lambda a: (a, (None, tm, n), lambda i, j, k: (j, i, 0))
    hid_out = ((N_DEV, t, n), BF16, (None, tm, n), lambda i, j, k: (j, i, 0), False)
    (dgate, dup), sent["dact"] = _mmc(
        name + "_dact", (t // tm, N_DEV, 1),
        [(dxo16, (tm, d), lambda i, j, k: (i, 0)),
         (wd, (None, n, d), lambda i, j, k: (j, 0, 0)),
         hid_in(gate), hid_in(up)],
        [(0, 1, 0, NT)], [(tm, n)], epi_act, [hid_out] * 2, comm=_carried(comms, "dact", avail))

    tk = _tile(t, 1024, 16)
    td = _tile(d, 1024, 128)
    w_out = ((N_DEV, d, n), BF16, (None, td, n), lambda i, j, k: (j, i, 0), False)
    (dwg, dwu), sent["dwgu"] = _mmc(
        name + "_dwgu", (d // td, N_DEV, t // tk),
        [(h, (tk, td), lambda i, j, k: (k, i)),
         (dgate, (None, tk, n), lambda i, j, k: (j, k, 0)),
         (dup, (None, tk, n), lambda i, j, k: (j, k, 0))],
        [(0, 1, 0, TN), (0, 2, 1, TN)], [(td, n)] * 2, lambda accs, refs, rows: accs, [w_out] * 2,
        comm=_carried(comms, "dwgu", avail))
    avail.update(dwg=dwg, dwu=dwu)

    (dwd,), sent["dwd"] = _mmc(
        name + "_dwd", (N_DEV, 1, t // tk),
        [(act, (None, tk, n), lambda i, j, k: (i, k, 0)),
         (dxo16, (tk, d), lambda i, j, k: (k, 0))],
        [(0, 1, 0, TN)], [(n, d)], lambda accs, refs, rows: [0.5 * accs[0]],
        [((N_DEV, n, d), BF16, (None, n, d), lambda i, j, k: (i, 0, 0), False)],
        comm=_carried(comms, "dwd", avail))
    avail.update(dwd=dwd)

    tm3 = _tile(t, 512, 16)

    def epi_dx(accs, refs, rows):
        dx, dw = _rms_bwd(accs[0], refs[4][rows], refs[5][...], refs[6][rows])
        return [dx, dx, dw]

    (dx, dx16, dnorm), sent["dx"] = _mmc(
        name + "_dx", (t // tm3, 1, N_DEV),
        [(dgate, (None, tm3, n), lambda i, j, k: (k, i, 0)),
         (wg, (None, d, n), lambda i, j, k: (k, 0, 0)),
         (dup, (None, tm3, n), lambda i, j, k: (k, i, 0)),
         (wu, (None, d, n), lambda i, j, k: (k, 0, 0)),
         (x, (tm3, d), lambda i, j, k: (i, 0)),
         (norm_w, (1, d), lambda i, j, k: (0, 0)),
         (dxo, (tm3, d), lambda i, j, k: (i, 0))],
        [(0, 1, 0, NT), (2, 3, 0, NT)], [(tm3, d)], epi_dx,
        [((t, d), F32, (tm3, d), lambda i, j, k: (i, 0), False),
         ((t, d), BF16, (tm3, d), lambda i, j, k: (i, 0), False),
         ((1, d), F32, (1, d), lambda i, j, k: (0, 0), True)],
        comm=_carried(comms, "dx", avail))
    return dx, dx16, dwg, dwu, dwd, dnorm, sent


def _rope(tv, cos, s1, s2):
    return tv * cos + pltpu.roll(tv, LANES - 8, 1) * s1 + pltpu.roll(tv, 8, 1) * s2


def _rope_bwd(dr, cos, s1, s2):
    return dr * cos + pltpu.roll(dr * s1, 8, 1) + pltpu.roll(dr * s2, LANES - 8, 1)


def _rope_tables(positions):
    inv_freq = jnp.power(jnp.float32(ROPE_THETA), -jnp.arange(0, ROT_DIM, 2, dtype=F32) / ROT_DIM)
    ang = positions.reshape(-1).astype(F32)[:, None] * inv_freq
    cos, sin = jnp.cos(ang), jnp.sin(ang)
    t = ang.shape[0]
    pad = A_HEAD_DIM - ROT_DIM
    cos_t = jnp.concatenate([cos, cos, jnp.ones((t, pad), F32)], axis=1)
    s1_t = jnp.concatenate([-sin, jnp.zeros((t, pad + 8), F32)], axis=1)
    s2_t = jnp.concatenate([jnp.zeros((t, 8), F32), sin, jnp.zeros((t, pad), F32)], axis=1)
    return tuple(jnp.tile(v, (1, 2)) for v in (cos_t, s1_t, s2_t))


def _att_common(i, k_p, k_c, v_p, v_c, tabs_p, tabs_c):
    kcat = jnp.concatenate([k_p, k_c], axis=0)
    vcat = jnp.concatenate([v_p, v_c], axis=0)
    tabs_k = [jnp.concatenate([a, b], axis=0) for a, b in zip(tabs_p, tabs_c)]
    kvar, vvar = [], []
    for ks in range(2):
        sl = slice(ks * LANES, (ks + 1) * LANES)
        kr = _rope(kcat[:, sl], *tabs_k)
        kvar.append([kr.astype(BF16), pltpu.roll(kr, 64, 1).astype(BF16)])
        vvar.append([vcat[:, sl].astype(BF16), pltpu.roll(vcat[:, sl], 64, 1).astype(BF16)])
    qi = lax.broadcasted_iota(jnp.int32, (ATT_BLOCK, 2 * ATT_BLOCK), 0)
    kj = lax.broadcasted_iota(jnp.int32, (ATT_BLOCK, 2 * ATT_BLOCK), 1)
    dist = qi + ATT_BLOCK - kj
    allowed = (dist >= 0) & (dist < ATT_BLOCK) & ((i > 0) | (kj >= ATT_BLOCK))
    lane = lax.broadcasted_iota(jnp.int32, (1, LANES), 1)
    halves = [lane < 64, lane >= 64]
    return kvar, vvar, allowed, halves


def _att_head_probs(qm, kslab, allowed, sink):
    s = lax.dot_general(qm, kslab, NT, preferred_element_type=F32)
    s = jnp.where(allowed, s, -jnp.inf)
    m = jnp.maximum(jnp.max(s, axis=-1, keepdims=True), sink)
    e = jnp.exp(s - m)
    es = jnp.exp(sink - m)
    den = jnp.sum(e, axis=-1, keepdims=True) + es
    return e / den, es / den


def _head_slots(h):
    pair, half = h // 2, h % 2
    kvh = h // 4
    return pair, half, kvh // 2, int(kvh % 2 != half)


def _attention_fwd(proj, tables, sinks):
    t = proj.shape[0]
    nb = t // ATT_BLOCK
    b = ATT_BLOCK
    scale = A_HEAD_DIM ** -0.5

    def body(sink_ref, q_ref, kp_ref, kc_ref, vp_ref, vc_ref,
             cp_ref, ap_ref, bp_ref, cc_ref, ac_ref, bc_ref, o_ref):
        i = pl.program_id(0)
        tabs_p = (cp_ref[...], ap_ref[...], bp_ref[...])
        tabs_c = (cc_ref[...], ac_ref[...], bc_ref[...])
        kvar, vvar, allowed, halves = _att_common(
            i, kp_ref[...], kc_ref[...], vp_ref[...], vc_ref[...], tabs_p, tabs_c)
        for pair in range(A_HEADS // 2):
            sl = slice(pair * LANES, (pair + 1) * LANES)
            q2 = _rope(q_ref[:, sl], *tabs_c) * scale
            acc = jnp.zeros((b, LANES), F32)
            for half in range(2):
                h = 2 * pair + half
                _, _, ks, var = _head_slots(h)
                qm = jnp.where(halves[half], q2, 0.0).astype(BF16)
                p, _ = _att_head_probs(qm, kvar[ks][var], allowed, sink_ref[h])
                o = lax.dot_general(p.astype(BF16), vvar[ks][var], NN, preferred_element_type=F32)
                acc = acc + jnp.where(halves[half], o, 0.0)
            o_ref[:, sl] = acc.astype(o_ref.dtype)

    cur = lambda col: (lambda i: (i, col))
    prev = lambda col: (lambda i: (jnp.maximum(i - 1, 0), col))
    tab = [pl.BlockSpec((b, LANES), prev(0))] * 3 + [pl.BlockSpec((b, LANES), cur(0))] * 3
    return pl.pallas_call(
        body,
        name="attention_fwd",
        grid=(nb,),
        in_specs=[pl.BlockSpec(memory_space=pltpu.SMEM),
                  pl.BlockSpec((b, A_WIDTH), cur(0)),
                  pl.BlockSpec((b, A_KV_WIDTH), prev(OFF_KA // A_KV_WIDTH)),
                  pl.BlockSpec((b, A_KV_WIDTH), cur(OFF_KA // A_KV_WIDTH)),
                  pl.BlockSpec((b, A_KV_WIDTH), prev(OFF_VA // A_KV_WIDTH)),
                  pl.BlockSpec((b, A_KV_WIDTH), cur(OFF_VA // A_KV_WIDTH))] + tab,
        out_specs=pl.BlockSpec((b, A_WIDTH), cur(0)),
        out_shape=jax.ShapeDtypeStruct((t, A_WIDTH), BF16),
        compiler_params=_params(("arbitrary",)),
    )(sinks, proj, proj, proj, proj, proj, *tables, *tables)


def _attention_bwd(proj, tables, sinks, d_out, comm=None):
    t = proj.shape[0]
    nb = t // ATT_BLOCK
    b = ATT_BLOCK
    scale = A_HEAD_DIM ** -0.5

    def body(sink_ref, q_ref, kp_ref, kc_ref, vp_ref, vc_ref,
             cp_ref, ap_ref, bp_ref, cc_ref, ac_ref, bc_ref, do_ref,
             dq_ref, dk_ref, dv_ref, dsink_ref, dk_carry, dv_carry):
        step = pl.program_id(0)
        i = nb - 1 - step

        @pl.when(step == 0)
        def _():
            dk_carry[...] = jnp.zeros(dk_carry.shape, F32)
            dv_carry[...] = jnp.zeros(dv_carry.shape, F32)
            dsink_ref[...] = jnp.zeros(dsink_ref.shape, F32)

        tabs_p = (cp_ref[...], ap_ref[...], bp_ref[...])
        tabs_c = (cc_ref[...], ac_ref[...], bc_ref[...])
        kvar, vvar, allowed, halves = _att_common(
            i, kp_ref[...], kc_ref[...], vp_ref[...], vc_ref[...], tabs_p, tabs_c)
        lane = lax.broadcasted_iota(jnp.int32, (1, LANES), 1)
        dk_slab = [jnp.zeros((2 * b, LANES), F32) for _ in range(2)]
        dv_slab = [jnp.zeros((2 * b, LANES), F32) for _ in range(2)]
        dsink = jnp.zeros((1, LANES), F32)
        for pair in range(A_HEADS // 2):
            sl = slice(pair * LANES, (pair + 1) * LANES)
            q2 = _rope(q_ref[:, sl], *tabs_c) * scale
            do2 = do_ref[:, sl].astype(F32)
            dq2 = jnp.zeros((b, LANES), F32)
            for half in range(2):
                h = 2 * pair + half
                _, _, ks, var = _head_slots(h)
                qm = jnp.where(halves[half], q2, 0.0).astype(BF16)
                dom = jnp.where(halves[half], do2, 0.0).astype(BF16)
                p, p_sink = _att_head_probs(qm, kvar[ks][var], allowed, sink_ref[h])
                dp = lax.dot_general(dom, vvar[ks][var], NT, preferred_element_type=F32)
                dsum = jnp.sum(p * dp, axis=-1, keepdims=True)
                ds = (p * (dp - dsum)).astype(BF16)
                dsink = dsink + jnp.where(lane == h, -jnp.sum(p_sink * dsum), 0.0)
                dqh = lax.dot_general(ds, kvar[ks][var], NN, preferred_element_type=F32)
                dq2 = dq2 + jnp.where(halves[half], dqh, 0.0)
                dkx = lax.dot_general(ds, qm, TN, preferred_element_type=F32)
                dvx = lax.dot_general(p.astype(BF16), dom, TN, preferred_element_type=F32)
                if var:
                    dkx = pltpu.roll(dkx, 64, 1)
                    dvx = pltpu.roll(dvx, 64, 1)
                dk_slab[ks] = dk_slab[ks] + dkx
                dv_slab[ks] = dv_slab[ks] + dvx
            dq_ref[:, sl] = _rope_bwd(dq2 * scale, *tabs_c).astype(dq_ref.dtype)
        for ks in range(2):
            sl = slice(ks * LANES, (ks + 1) * LANES)
            dk_cur = dk_slab[ks][b:] + dk_carry[:, sl]
            dk_ref[:, sl] = _rope_bwd(dk_cur, *tabs_c).astype(dk_ref.dtype)
            dv_ref[:, sl] = (dv_slab[ks][b:] + dv_carry[:, sl]).astype(dv_ref.dtype)
            dk_carry[:, sl] = dk_slab[ks][:b]
            dv_carry[:, sl] = dv_slab[ks][:b]
        dsink_ref[...] += dsink

    cur = lambda col: (lambda s: (nb - 1 - s, col))
    prev = lambda col: (lambda s: (jnp.maximum(nb - 2 - s, 0), col))
    tab = [pl.BlockSpec((b, LANES), prev(0))] * 3 + [pl.BlockSpec((b, LANES), cur(0))] * 3
    return _call(
        "attention_bwd", body, (nb,),
        [pl.BlockSpec(memory_space=pltpu.SMEM),
         pl.BlockSpec((b, A_WIDTH), cur(0)),
         pl.BlockSpec((b, A_KV_WIDTH), prev(OFF_KA // A_KV_WIDTH)),
         pl.BlockSpec((b, A_KV_WIDTH), cur(OFF_KA // A_KV_WIDTH)),
         pl.BlockSpec((b, A_KV_WIDTH), prev(OFF_VA // A_KV_WIDTH)),
         pl.BlockSpec((b, A_KV_WIDTH), cur(OFF_VA // A_KV_WIDTH))] + tab
        + [pl.BlockSpec((b, A_WIDTH), cur(0))],
        [pl.BlockSpec((b, A_WIDTH), cur(0)),
         pl.BlockSpec((b, A_KV_WIDTH), cur(0)),
         pl.BlockSpec((b, A_KV_WIDTH), cur(0)),
         pl.BlockSpec((1, LANES), lambda s: (0, 0))],
        [jax.ShapeDtypeStruct((t, A_WIDTH), BF16),
         jax.ShapeDtypeStruct((t, A_KV_WIDTH), BF16),
         jax.ShapeDtypeStruct((t, A_KV_WIDTH), BF16),
         jax.ShapeDtypeStruct((1, LANES), F32)],
        [pltpu.VMEM((b, A_KV_WIDTH), F32), pltpu.VMEM((b, A_KV_WIDTH), F32)],
        [sinks, proj, proj, proj, proj, proj, *tables, *tables, d_out], comm)


HGRN_HEADS_PER_STEP = 4
HGRN_W = HGRN_HEADS_PER_STEP * B_HEAD


def _hgrn_gates(qp, fp, hlb):
    lb = _sig(hlb[0:1] - hlb[1:2])
    sg = _sig(fp)
    big_f = lb + (1.0 - lb) * sg
    kk = (1.0 - lb) * (1.0 - sg)
    sq = _sig(qp)
    return lb, sg, big_f, kk, sq


def _tri(lower):
    r = lax.broadcasted_iota(jnp.int32, (CHUNK, CHUNK), 0)
    c = lax.broadcasted_iota(jnp.int32, (CHUNK, CHUNK), 1)
    return (r >= c) if lower else (r <= c)


def _hgrn_factors(q_ref, k_ref, b_ref, sl):
    out = []
    bh = b_ref[:, sl]
    for i in range(CHUNK // SUB):
        rows = slice(i * SUB, (i + 1) * SUB)
        b0 = b_ref[i * SUB - 1:i * SUB, sl] if i else jnp.zeros((1, B_HEAD), F32)
        eq = jnp.exp(b_ref[rows, sl] - b0)
        ek = jnp.exp(jnp.minimum(b0 - bh, MAX_DECAY_EXP))
        out.append((q_ref[rows, sl] * eq, k_ref[:, sl] * ek, eq, ek))
    return out


def _hgrn_scores(factors):
    rows = [lax.dot_general(qd.astype(BF16), kx.astype(BF16), NT, preferred_element_type=F32)
            for qd, kx, _, _ in factors]
    return jnp.where(_tri(True), jnp.concatenate(rows, axis=0), 0.0)


def _hgrn_specs(nc, reverse):
    w = HGRN_W
    chunk = (lambda c: nc - 1 - c) if reverse else (lambda c: c)
    col = lambda off: pl.BlockSpec((CHUNK, w), lambda hh, c: (chunk(c), off // w + hh))
    return chunk, col


def _hgrn_fwd(proj, hlb, wn, comm=None):
    t = proj.shape[0]
    nc = t // CHUNK
    w = HGRN_W
    chunk, col = _hgrn_specs(nc, False)

    def body(q_ref, f_ref, v_ref, og_ref, hlb_ref, wn_ref, ob_ref, oraw_ref, st_ref,
             state, qs, ks, bs):
        c = pl.program_id(1)

        @pl.when(c == 0)
        def _():
            state[...] = jnp.zeros(state.shape, F32)

        qp = q_ref[...]
        lb, sg, big_f, kk, sq = _hgrn_gates(qp, f_ref[...], hlb_ref[...])
        qs[...] = qp * sq
        ks[...] = kk
        bs[...] = lax.dot_general(_tri(True).astype(F32), jnp.log(big_f), NN,
                                  precision=lax.Precision.HIGHEST, preferred_element_type=F32)
        for h in range(HGRN_HEADS_PER_STEP):
            sl = slice(h * B_HEAD, (h + 1) * B_HEAD)
            bh = bs[:, sl]
            qh = qs[:, sl]
            kh = ks[:, sl]
            vh = v_ref[:, sl].astype(BF16)
            s0 = state[h]
            st_ref[h] = s0.astype(st_ref.dtype)
            a = _hgrn_scores(_hgrn_factors(qs, ks, bs, sl))
            o = lax.dot_general(a.astype(BF16), vh, NN, preferred_element_type=F32)
            o = o + lax.dot_general((qh * jnp.exp(bh)).astype(BF16), s0.astype(BF16), NT,
                                    preferred_element_type=F32)
            bl = bs[CHUNK - 1:CHUNK, sl]
            khat = (kh * jnp.exp(bl - bh)).astype(BF16)
            state[h] = s0 * jnp.exp(bl) + lax.dot_general(vh, khat, TN, preferred_element_type=F32)
            oraw_ref[:, sl] = o
            r = lax.rsqrt(jnp.mean(o * o, axis=-1, keepdims=True) + EPS)
            og = og_ref[:, sl]
            ob_ref[:, sl] = (o * r * wn_ref[:, sl] * (og * _sig(og))).astype(ob_ref.dtype)

    vec = lambda rows: pl.BlockSpec((rows, w), lambda hh, c: (0, hh))
    return _call(
        "hgrn_fwd", body, (B_WIDTH // w, nc),
        [col(OFF_QB), col(OFF_FB), col(OFF_IB), col(OFF_OG), vec(2), vec(1)],
        [pl.BlockSpec((CHUNK, w), lambda hh, c: (c, hh)),
         pl.BlockSpec((CHUNK, w), lambda hh, c: (c, hh)),
         pl.BlockSpec((None, HGRN_HEADS_PER_STEP, B_HEAD, B_HEAD), lambda hh, c: (c, hh, 0, 0))],
        [jax.ShapeDtypeStruct((t, B_WIDTH), BF16),
         jax.ShapeDtypeStruct((t, B_WIDTH), F32),
         jax.ShapeDtypeStruct((nc, B_WIDTH // B_HEAD, B_HEAD, B_HEAD), BF16)],
        [pltpu.VMEM((HGRN_HEADS_PER_STEP, B_HEAD, B_HEAD), F32),
         pltpu.VMEM((CHUNK, w), F32), pltpu.VMEM((CHUNK, w), F32), pltpu.VMEM((CHUNK, w), F32)],
        [proj, proj, proj, proj, hlb, wn], comm)


def _hgrn_bwd(proj, hlb, wn, o_raw, states, d_out, comm=None):
    t = proj.shape[0]
    nc = t // CHUNK
    w = HGRN_W
    chunk, col = _hgrn_specs(nc, True)

    def body(q_ref, f_ref, v_ref, og_ref, hlb_ref, wn_ref, oraw_ref, st_ref, dout_ref,
             dq_ref, df_ref, dv_ref, dog_ref, dlb_ref, dwn_ref,
             dstate, qs, ks, bs, dbs, dks, dqs):
        c = pl.program_id(1)

        @pl.when(c == 0)
        def _():
            dstate[...] = jnp.zeros(dstate.shape, F32)
            dlb_ref[...] = jnp.zeros(dlb_ref.shape, F32)
            dwn_ref[...] = jnp.zeros(dwn_ref.shape, F32)

        qp = q_ref[...]
        lb, sg, big_f, kk, sq = _hgrn_gates(qp, f_ref[...], hlb_ref[...])
        qs[...] = qp * sq
        ks[...] = kk
        bs[...] = lax.dot_general(_tri(True).astype(F32), jnp.log(big_f), NN,
                                  precision=lax.Precision.HIGHEST, preferred_element_type=F32)
        last_row = lax.broadcasted_iota(jnp.int32, (CHUNK, 1), 0) == CHUNK - 1
        for h in range(HGRN_HEADS_PER_STEP):
            sl = slice(h * B_HEAD, (h + 1) * B_HEAD)
            bh = bs[:, sl]
            qh = qs[:, sl]
            kh = ks[:, sl]
            vh = v_ref[:, sl].astype(BF16)
            s0 = st_ref[h]
            ds1 = dstate[h]
            o = oraw_ref[:, sl]
            og = og_ref[:, sl]
            wnh = wn_ref[:, sl]
            sog = _sig(og)
            r = lax.rsqrt(jnp.mean(o * o, axis=-1, keepdims=True) + EPS)
            dob = dout_ref[:, sl].astype(F32)
            don = dob * (og * sog)
            dog_ref[:, sl] = (dob * (o * r * wnh) * (sog * (1.0 + og * (1.0 - sog)))).astype(dog_ref.dtype)
            dwn_ref[:, sl] += jnp.sum(don * o * r, axis=0, keepdims=True)
            gp = don * wnh
            do = r * (gp - o * (r * r) * jnp.mean(gp * o, axis=-1, keepdims=True))
            do16 = do.astype(BF16)
            factors = _hgrn_factors(qs, ks, bs, sl)
            a = _hgrn_scores(factors)
            da = jnp.where(_tri(True), lax.dot_general(do16, vh, NT, preferred_element_type=F32), 0.0)
            dv = lax.dot_general(a.astype(BF16), do16, TN, preferred_element_type=F32)
            dq_rows = []
            dk = jnp.zeros((CHUNK, B_HEAD), F32)
            for i, (qd, kx, eq, ek) in enumerate(factors):
                da_i = da[i * SUB:(i + 1) * SUB]
                dq_rows.append(lax.dot_general(da_i, kx, NN, precision=lax.Precision.HIGHEST,
                                               preferred_element_type=F32) * eq)
                dk = dk + lax.dot_general(da_i, qd, TN, precision=lax.Precision.HIGHEST,
                                          preferred_element_type=F32) * ek
            eb = jnp.exp(bh)
            bl = bs[CHUNK - 1:CHUNK, sl]
            ebl = jnp.exp(bl)
            ekl = jnp.exp(bl - bh)
            qe = (qh * eb).astype(BF16)
            khat = (kh * ekl).astype(BF16)
            ds1_16 = ds1.astype(BF16)
            dq = jnp.concatenate(dq_rows, axis=0) + eb * lax.dot_general(
                do16, s0, NN, preferred_element_type=F32)
            dv = dv + lax.dot_general(khat, ds1_16, NT, preferred_element_type=F32)
            dk_state = ekl * lax.dot_general(vh, ds1_16, NN, preferred_element_type=F32)
            dk = dk + dk_state
            db_last = (jnp.sum(kh * dk_state, axis=0, keepdims=True)
                       + ebl * jnp.sum(s0.astype(F32) * ds1, axis=0, keepdims=True))
            dbs[:, sl] = qh * dq - kh * dk + jnp.where(last_row, db_last, 0.0)
            dks[:, sl] = dk
            dqs[:, sl] = dq
            dstate[h] = ds1 * ebl + lax.dot_general(do16, qe, TN, preferred_element_type=F32)
            dv_ref[:, sl] = dv.astype(dv_ref.dtype)
        dg = lax.dot_general(_tri(False).astype(F32), dbs[...], NN,
                             precision=lax.Precision.HIGHEST, preferred_element_type=F32)
        dq_ref[...] = (dqs[...] * (sq * (1.0 + qp * (1.0 - sq)))).astype(dq_ref.dtype)
        df_minus_dk = dg / big_f - dks[...]
        df_ref[...] = ((1.0 - lb) * df_minus_dk * sg * (1.0 - sg)).astype(df_ref.dtype)
        dlb_ref[...] += jnp.sum((1.0 - sg) * df_minus_dk, axis=0, keepdims=True) * (lb * (1.0 - lb))

    vec = lambda rows: pl.BlockSpec((rows, w), lambda hh, c: (0, hh))
    blk = pl.BlockSpec((CHUNK, w), lambda hh, c: (chunk(c), hh))
    act = jax.ShapeDtypeStruct((t, B_WIDTH), BF16)
    row = jax.ShapeDtypeStruct((1, B_WIDTH), F32)
    return _call(
        "hgrn_bwd", body, (B_WIDTH // w, nc),
        [col(OFF_QB), col(OFF_FB), col(OFF_IB), col(OFF_OG), vec(2), vec(1), blk,
         pl.BlockSpec((None, HGRN_HEADS_PER_STEP, B_HEAD, B_HEAD), lambda hh, c: (chunk(c), hh, 0, 0)),
         blk],
        [blk, blk, blk, blk, vec(1), vec(1)],
        [act, act, act, act, row, row],
        [pltpu.VMEM((HGRN_HEADS_PER_STEP, B_HEAD, B_HEAD), F32)] + [pltpu.VMEM((CHUNK, w), F32)] * 6,
        [proj, proj, proj, proj, hlb, wn, o_raw, states, d_out], comm)


def _mixer_fwd(x1, mix_norm, w_in, sinks, hlb, wn, weights, tables, comms):
    t, d = x1.shape
    n_in = w_in.shape[1]
    nb = d // N_DEV
    sent = {}
    h2 = _rmsnorm("mix_norm", x1, mix_norm)
    tm = _tile(t, 1024, 16)
    tn = _tile(n_in, 512, 128)
    (proj,), sent["proj"] = _mmc(
        "mix_proj", (t // tm, n_in // tn, 1),
        [(h2, (tm, d), lambda i, j, k: (i, 0)), (w_in, (d, tn), lambda i, j, k: (0, j))],
        [(0, 1, 0, NN)], [(tm, tn)], lambda accs, refs, rows: accs,
        [((t, n_in), F32, (tm, tn), lambda i, j, k: (i, j), False)], comm=comms.get("proj"))
    out_a = _attention_fwd(proj, tables, sinks)
    (out_b, o_raw, states), sent["hgrn"] = _hgrn_fwd(proj, hlb, wn, comms.get("hgrn"))
    wua, wub, w_out = weights(sent)

    def epi_merge(accs, refs, rows):
        ua, ub = accs
        return [_sig(refs[4][rows]) * ua + _sig(refs[5][rows]) * ub, ua, ub]

    wide = ((t, d), BF16, (tm, nb), lambda i, j, k: (i, j), False)
    merged, ua, ub = _mm(
        "mix_merge", (t // tm, N_DEV, 1),
        [(out_a, (tm, A_WIDTH), lambda i, j, k: (i, 0)),
         (wua, (None, A_WIDTH, nb), lambda i, j, k: (j, 0, 0)),
         (out_b, (tm, B_WIDTH), lambda i, j, k: (i, 0)),
         (wub, (None, B_WIDTH, nb), lambda i, j, k: (j, 0, 0)),
         (proj, (tm, nb), lambda i, j, k: (i, OFF_GA // nb + j)),
         (proj, (tm, nb), lambda i, j, k: (i, (OFF_GA + d) // nb + j))],
        [(0, 1, 0, NN), (2, 3, 1, NN)], [(tm, nb)] * 2, epi_merge, [wide] * 3)

    tn2 = _tile(d, 1024, 128)
    (x2,) = _mm(
        "mix_out", (t // tm, d // tn2, 1),
        [(merged, (tm, d), lambda i, j, k: (i, 0)), (w_out, (d, tn2), lambda i, j, k: (0, j)),
         (x1, (tm, tn2), lambda i, j, k: (i, j))],
        [(0, 1, 0, NN)], [(tm, tn2)], lambda accs, refs, rows: [refs[2][rows] + accs[0]],
        [((t, d), F32, (tm, tn2), lambda i, j, k: (i, j), False)])
    saved = dict(h2=h2, proj=proj, out_a=out_a, out_b=out_b, o_raw=o_raw, states=states,
                 merged=merged, ua=ua, ub=ub)
    return x2, saved, sent


def _mixer_bwd(dx2, dx2_16, x1, mix_norm, w_in, sinks, hlb, wn, wua, wub, w_out, tables, sv, comms):
    t, d = x1.shape
    sent = {}
    avail = dict(sent=sent)
    n_in = w_in.shape[1]
    nb = d // N_DEV
    proj = sv["proj"]
    tm = _tile(t, 1024, 16)
    tn = _tile(d, 512, 128)

    def epi_gate(accs, refs, rows):
        dm = accs[0]
        sa = _sig(refs[2][rows])
        sb = _sig(refs[3][rows])
        ua = refs[4][rows].astype(F32)
        ub = refs[5][rows].astype(F32)
        return [dm * sa, dm * sb, dm * ua * sa * (1.0 - sa), dm * ub * sb * (1.0 - sb)]

    wide = ((t, d), BF16, (tm, tn), lambda i, j, k: (i, j), False)
    dua, dub, dga, dgb = _mm(
        "mix_dmerge", (t // tm, d // tn, 1),
        [(dx2_16, (tm, d), lambda i, j, k: (i, 0)),
         (w_out, (tn, d), lambda i, j, k: (j, 0)),
         (proj, (tm, tn), lambda i, j, k: (i, OFF_GA // tn + j)),
         (proj, (tm, tn), lambda i, j, k: (i, (OFF_GA + d) // tn + j)),
         (sv["ua"], (tm, tn), lambda i, j, k: (i, j)),
         (sv["ub"], (tm, tn), lambda i, j, k: (i, j))],
        [(0, 1, 0, NT)], [(tm, tn)], epi_gate, [wide] * 4)

    tk = _tile(t, 1024, 16)
    two = _tile(d, 1024, 128)
    (dw_out,) = _mm(
        "mix_dwout", (d // two, 1, t // tk),
        [(sv["merged"], (tk, two), lambda i, j, k: (k, i)), (dx2_16, (tk, d), lambda i, j, k: (k, 0))],
        [(0, 1, 0, TN)], [(two, d)], lambda accs, refs, rows: accs,
        [((d, d), BF16, (two, d), lambda i, j, k: (i, 0), False)])

    w_spec = lambda rows: ((N_DEV, rows, nb), BF16, (None, rows, nb), lambda i, j, k: (j, 0, 0), False)
    tk2 = _tile(t, 2048, 16)
    dwua, dwub = _mm(
        "mix_dwup", (1, N_DEV, t // tk2),
        [(sv["out_a"], (tk2, A_WIDTH), lambda i, j, k: (k, 0)),
         (dua, (tk2, nb), lambda i, j, k: (k, j)),
         (sv["out_b"], (tk2, B_WIDTH), lambda i, j, k: (k, 0)),
         (dub, (tk2, nb), lambda i, j, k: (k, j))],
        [(0, 1, 0, TN), (2, 3, 1, TN)], [(A_WIDTH, nb), (B_WIDTH, nb)], lambda accs, refs, rows: accs,
        [w_spec(A_WIDTH), w_spec(B_WIDTH)])

    d_out_a, d_out_b = _mm(
        "mix_dup", (t // tm, 1, N_DEV),
        [(dua, (tm, nb), lambda i, j, k: (i, k)),
         (wua, (None, A_WIDTH, nb), lambda i, j, k: (k, 0, 0)),
         (dub, (tm, nb), lambda i, j, k: (i, k)),
         (wub, (None, B_WIDTH, nb), lambda i, j, k: (k, 0, 0))],
        [(0, 1, 0, NT), (2, 3, 1, NT)], [(tm, A_WIDTH), (tm, B_WIDTH)], lambda accs, refs, rows: accs,
        [((t, A_WIDTH), BF16, (tm, A_WIDTH), lambda i, j, k: (i, 0), False),
         ((t, B_WIDTH), BF16, (tm, B_WIDTH), lambda i, j, k: (i, 0), False)])

    avail.update(w_out=dw_out.reshape(N_DEV, d // N_DEV, d), w_up_a=dwua, w_up_b=dwub)
    (dq_a, dk_a, dv_a, dsinks), sent["attention"] = _attention_bwd(
        proj, tables, sinks, d_out_a, _carried(comms, "attention", avail))
    (dq_b, df_b, di_b, dog_b, dlb, dwn), sent["hgrn"] = _hgrn_bwd(
        proj, hlb, wn, sv["o_raw"], sv["states"], d_out_b, _carried(comms, "hgrn", avail))
    dproj = jnp.concatenate([dq_a, dk_a, dv_a, dq_b, df_b, di_b, dog_b, dga, dgb], axis=1)

    tdm = _tile(d, 1024, 128)
    twn = _tile(n_in, 2432, 128)
    tkw = _tile(t, 512, 16)
    (dw_in,) = _mm(
        "mix_dwin", (d // tdm, n_in // twn, t // tkw),
        [(sv["h2"], (tkw, tdm), lambda i, j, k: (k, i)), (dproj, (tkw, twn), lambda i, j, k: (k, j))],
        [(0, 1, 0, TN)], [(tdm, twn)], lambda accs, refs, rows: accs,
        [((d, n_in), BF16, (tdm, twn), lambda i, j, k: (i, j), False)])

    avail.update(w_in=dw_in.reshape(d, N_DEV, n_in // N_DEV).transpose(1, 0, 2))
    tm3 = _tile(t, 512, 16)
    tpn = _tile(n_in, 512, 128)

    def epi_dx(accs, refs, rows):
        dx, dw = _rms_bwd(accs[0], refs[2][rows], refs[3][...], refs[4][rows])
        return [dx, dx, dw]

    (dx1, dx1_16, dmix), sent["dx"] = _mmc(
        "mix_dx", (t // tm3, 1, n_in // tpn),
        [(dproj, (tm3, tpn), lambda i, j, k: (i, k)), (w_in, (d, tpn), lambda i, j, k: (0, k)),
         (x1, (tm3, d), lambda i, j, k: (i, 0)), (mix_norm, (1, d), lambda i, j, k: (0, 0)),
         (dx2, (tm3, d), lambda i, j, k: (i, 0))],
        [(0, 1, 0, NT)], [(tm3, d)], epi_dx,
        [((t, d), F32, (tm3, d), lambda i, j, k: (i, 0), False),
         ((t, d), BF16, (tm3, d), lambda i, j, k: (i, 0), False),
         ((1, d), F32, (1, d), lambda i, j, k: (0, 0), True)],
        comm=_carried(comms, "dx", avail))
    small = dict(mix_norm=dmix, attn_sinks=dsinks, hgrn_lb=dlb, hgrn_norm=dwn)
    return dx1, dx1_16, small, avail


def _ple_fwd(x3, ple_norm, p16, wpg, wpp):
    t, d = x3.shape
    nb = d // N_DEV
    h4 = _rmsnorm("ple_norm", x3, ple_norm)
    tm = _tile(t, 1024, 16)

    def epi(accs, refs, rows):
        zg, pp = accs
        return [refs[4][rows] + _sig(zg) * pp, zg, pp]

    blk = lambda dt: ((t, d), dt, (tm, nb), lambda i, j, k: (i, j), False)
    x4, zg, pp = _mm(
        "ple_fwd", (t // tm, N_DEV, 1),
        [(h4, (tm, d), lambda i, j, k: (i, 0)), (wpg, (d, nb), lambda i, j, k: (0, j)),
         (p16, (tm, PLE_DIM), lambda i, j, k: (i, 0)),
         (wpp, (None, PLE_DIM, nb), lambda i, j, k: (j, 0, 0)),
         (x3, (tm, nb), lambda i, j, k: (i, j))],
        [(0, 1, 0, NN), (2, 3, 1, NN)], [(tm, nb)] * 2, epi, [blk(F32), blk(BF16), blk(BF16)])
    return x4, dict(h4=h4, zg=zg, pp=pp)


def _ple_bwd(dx4, x3, ple_norm, p16, wpg, sv):
    t, d = x3.shape
    nb = d // N_DEV

    def gate_grads(dx, zg, pp):
        s = _sig(zg.astype(F32))
        return dx * pp.astype(F32) * s * (1.0 - s), dx * s

    dzg, dpp = _rows("ple_dgate", gate_grads, [(dx4, True), (sv["zg"], True), (sv["pp"], True)],
                     [((t, d), BF16), ((t, d), BF16)], _tile(t, 512, 16))
    (dwpp,) = _mm(
        "ple_dwproj", (1, N_DEV, 1),
        [(p16, (t, PLE_DIM), lambda i, j, k: (0, 0)), (dpp, (t, nb), lambda i, j, k: (0, j))],
        [(0, 1, 0, TN)], [(PLE_DIM, nb)], lambda accs, refs, rows: accs,
        [((N_DEV, PLE_DIM, nb), BF16, (None, PLE_DIM, nb), lambda i, j, k: (j, 0, 0), False)])
    tk = _tile(t, 1024, 16)
    tn = _tile(d, 1024, 128)
    (dwpg,) = _mm(
        "ple_dwgate", (d // tn, 1, t // tk),
        [(sv["h4"], (tk, tn), lambda i, j, k: (k, i)), (dzg, (tk, d), lambda i, j, k: (k, 0))],
        [(0, 1, 0, TN)], [(tn, d)], lambda accs, refs, rows: accs,
        [((d, d), BF16, (tn, d), lambda i, j, k: (i, 0), False)])
    tm3 = _tile(t, 256, 16)

    def epi_dx(accs, refs, rows):
        dx, dw = _rms_bwd(accs[0], refs[2][rows], refs[3][...], refs[4][rows])
        return [dx, dx, dw]

    dx3, dx3_16, dnorm = _mm(
        "ple_dx", (t // tm3, 1, 1),
        [(dzg, (tm3, d), lambda i, j, k: (i, 0)), (wpg, (d, d), lambda i, j, k: (0, 0)),
         (x3, (tm3, d), lambda i, j, k: (i, 0)), (ple_norm, (1, d), lambda i, j, k: (0, 0)),
         (dx4, (tm3, d), lambda i, j, k: (i, 0))],
        [(0, 1, 0, NT)], [(tm3, d)], epi_dx,
        [((t, d), F32, (tm3, d), lambda i, j, k: (i, 0), False),
         ((t, d), BF16, (tm3, d), lambda i, j, k: (i, 0), False),
         ((1, d), F32, (1, d), lambda i, j, k: (0, 0), True)])
    return dx3, dx3_16, dwpp, dwpg, dnorm


def _loss_head(x4, final_norm, target):
    t, d = x4.shape
    tm = _tile(t, 256, 8)

    def body(x_ref, w_ref, tgt_ref, dx_ref, dw_ref, loss_ref):
        i = pl.program_id(0)
        xv = x_ref[...]
        wv = w_ref[...]
        r = lax.rsqrt(jnp.mean(xv * xv, axis=-1, keepdims=True) + EPS)
        err = xv * r * wv - tgt_ref[...]
        part = 0.5 * jnp.sum(jnp.mean(err * err, axis=-1, keepdims=True), axis=0, keepdims=True)
        dx, dw = _rms_bwd(err * (1.0 / d), xv, wv, jnp.zeros_like(xv))
        dx_ref[...] = dx

        @pl.when(i == 0)
        def _():
            dw_ref[...] = dw
            loss_ref[...] = jnp.broadcast_to(part, loss_ref.shape)

        @pl.when(i > 0)
        def _():
            dw_ref[...] += dw
            loss_ref[...] += jnp.broadcast_to(part, loss_ref.shape)

    return pl.pallas_call(
        body,
        name="loss_head",
        grid=(t // tm,),
        in_specs=[pl.BlockSpec((tm, d), lambda i: (i, 0)), pl.BlockSpec((1, d), lambda i: (0, 0)),
                  pl.BlockSpec((tm, d), lambda i: (i, 0))],
        out_specs=[pl.BlockSpec((tm, d), lambda i: (i, 0)), pl.BlockSpec((1, d), lambda i: (0, 0)),
                   pl.BlockSpec((1, LANES), lambda i: (0, 0))],
        out_shape=[jax.ShapeDtypeStruct((t, d), F32), jax.ShapeDtypeStruct((1, d), F32),
                   jax.ShapeDtypeStruct((1, LANES), F32)],
        compiler_params=_params(("arbitrary",)),
    )(x4, final_norm, target)


RELATIONS = ((0, 0), (1, 0), (0, 1), (1, 1))


def _place():
    return lax.axis_index("x"), lax.axis_index("y"), lax.axis_index("c")


def _flip(v, bit):
    return 1 - v if bit else v


def _gather_program(shards):
    n = len(shards)

    def copies(ins, outs, send_sems, recv_sems, local_sems, base):
        x, y, c = _place()
        sibling = (x, y, 1 - c)

        def block(a, rel, core):
            return outs[a].at[4 * _flip(x, rel[0]) + 2 * _flip(y, rel[1]) + core]

        def copy(a, slot, rel, core, to, src=None):
            dst = block(a, rel, core)
            return pltpu.make_async_remote_copy(
                src_ref=dst if src is None else src, dst_ref=dst,
                send_sem=send_sems.at[base[0] + 7 * a + slot], recv_sem=recv_sems.at[base[1] + 7 * a + slot],
                device_id=to, device_id_type=MESH)

        own, first = [], []
        for a in range(n):
            own.append(pltpu.make_async_copy(ins[a], block(a, RELATIONS[0], c), local_sems.at[base[2] + a]))
            first.append(copy(a, 0, RELATIONS[0], c, sibling, src=ins[a]))
            for r in (1, 2, 3):
                rel = RELATIONS[r]
                first.append(copy(a, r, RELATIONS[0], c, (_flip(x, rel[0]), _flip(y, rel[1]), c), src=ins[a]))
        return own, first, copy, sibling, c

    def pre(*args):
        own, first, _, _, _ = copies(*args)
        for cp in own + first:
            cp.start()

    def post(*args):
        own, first, copy, sibling, c = copies(*args)
        passed = []
        for r in (1, 2, 3):
            for a in range(n):
                copy(a, r, RELATIONS[r], c, sibling).wait_recv()
                fwd = copy(a, 3 + r, RELATIONS[r], c, sibling)
                fwd.start()
                passed.append(fwd)
        for a in range(n):
            copy(a, 0, RELATIONS[0], 1 - c, sibling).wait_recv()
            for r in (1, 2, 3):
                copy(a, 3 + r, RELATIONS[r], 1 - c, sibling).wait_recv()
        for cp in first + passed:
            cp.wait_send()
        for mine in own:
            mine.wait()

    return _Comm(list(shards), [jax.ShapeDtypeStruct((N_DEV,) + s.shape, s.dtype) for s in shards],
                 (7 * n, 7 * n, n), pre, post)


def _exchange_program(arrays, n_slots, source, target, slot):
    n = len(arrays)

    def copies(ins, outs, send_sems, recv_sems, local_sems, base):
        place = _place()
        res = []
        for a in range(n):
            for r in range(n_slots):
                src = source(r, place)
                if src is None:
                    continue
                res.append(pltpu.make_async_remote_copy(
                    src_ref=ins[a].at[src], dst_ref=outs[a].at[slot(r)],
                    send_sem=send_sems.at[base[0] + n_slots * a + r],
                    recv_sem=recv_sems.at[base[1] + n_slots * a + r],
                    device_id=target(r, place), device_id_type=MESH))
        return res

    def pre(*args):
        for cp in copies(*args):
            cp.start()

    def post(*args):
        for cp in copies(*args):
            cp.wait()

    n_out = len({slot(r) for r in range(n_slots) if source(r, (0, 0, 0)) is not None})
    return _Comm(list(arrays), [jax.ShapeDtypeStruct((n_out,) + g.shape[1:], g.dtype) for g in arrays],
                 (n_slots * n, n_slots * n, 0), pre, post)


def _sibling_program(grads):
    def source(r, place):
        x, y, c = place
        return 4 * _flip(x, RELATIONS[r][0]) + 2 * _flip(y, RELATIONS[r][1]) + (1 - c)

    return _exchange_program(grads, 4, source, lambda r, p: (p[0], p[1], 1 - p[2]), lambda r: r)


def _chips_program(sums):
    def target(r, place):
        x, y, c = place
        return (_flip(x, RELATIONS[r][0]), _flip(y, RELATIONS[r][1]), c)

    return _exchange_program(sums, 4, lambda r, p: r if r else None, target, lambda r: r - 1)


def _standalone(name, comm):
    return _call(name, lambda: None, (1,), [], [], [], [], [], comm)[1]


def _pair_sum(name, grad, from_sibling, block_ids):
    _, rows, cols = grad.shape
    tr = _tile(rows, 512, 16)

    def body(ids_ref, mine_ref, sib_ref, o_ref):
        o_ref[...] = (mine_ref[...].astype(F32) + sib_ref[...].astype(F32)).astype(o_ref.dtype)

    return pl.pallas_call(
        body,
        name=name,
        grid_spec=pltpu.PrefetchScalarGridSpec(
            num_scalar_prefetch=1,
            grid=(4, rows // tr),
            in_specs=[pl.BlockSpec((None, tr, cols), lambda r, i, ids: (ids[r], i, 0)),
                      pl.BlockSpec((None, tr, cols), lambda r, i, ids: (r, i, 0))],
            out_specs=pl.BlockSpec((None, tr, cols), lambda r, i, ids: (r, i, 0))),
        out_shape=jax.ShapeDtypeStruct((4, rows, cols), BF16),
        compiler_params=_params(("arbitrary", "arbitrary")),
    )(block_ids, grad, from_sibling)


def _adam(w, g, m, v):
    m = ADAM_B1 * m + (1.0 - ADAM_B1) * g
    v = ADAM_B2 * v + (1.0 - ADAM_B2) * (g * g)
    m_hat = m / (1.0 - ADAM_B1 ** ADAM_STEP)
    v_hat = v / (1.0 - ADAM_B2 ** ADAM_STEP)
    delta = -ADAM_LR * (m_hat / (jnp.sqrt(v_hat) + ADAM_EPS) + ADAM_WD * w)
    return delta, m, v


def _adam_sharded(name, items, comm=None):
    rows, cols = items[0][0].shape
    n = len(items)
    tr = _tile(rows, max(64, 256 // n), 8)

    def body(*refs):
        for k in range(n):
            w_ref, m_ref, v_ref, s_ref, r1_ref, r2_ref, r3_ref = refs[7 * k:7 * k + 7]
            g_ref, d_ref, nm_ref, nv_ref = refs[7 * n + 4 * k:7 * n + 4 * k + 4]
            g = ((s_ref[...].astype(F32) + r1_ref[...].astype(F32)) + r2_ref[...].astype(F32)) + r3_ref[...].astype(F32)
            delta, nm, nv = _adam(w_ref[...], g, m_ref[...], v_ref[...])
            g_ref[...] = g
            d_ref[...] = delta
            nm_ref[...] = nm
            nv_ref[...] = nv

    flat = pl.BlockSpec((tr, cols), lambda i: (i, 0))
    slot = lambda s: pl.BlockSpec((None, tr, cols), lambda i: (s, i, 0))
    args = []
    for w, m, v, sums, from_chips in items:
        args += [w, m, v, sums, from_chips, from_chips, from_chips]
    res, sent = _call(
        name, body, (rows // tr,),
        [flat, flat, flat, slot(0), slot(0), slot(1), slot(2)] * n, [flat] * (4 * n),
        [jax.ShapeDtypeStruct((rows, cols), F32)] * (4 * n), [], args, comm)
    return [res[4 * k:4 * k + 4] for k in range(n)], sent


def _all_reduce_small(part):
    rows = part.shape[0]

    def body(x_ref, o_ref, gathered, send_sems, recv_sems):
        x, y, c = _place()
        me = 4 * x + 2 * y + c
        gathered[me] = x_ref[...]
        copies = []
        for k in range(1, N_DEV):
            bits = (k >> 2 & 1, k >> 1 & 1, k & 1)
            peer = (_flip(x, bits[0]), _flip(y, bits[1]), _flip(c, bits[2]))
            copies.append(pltpu.make_async_remote_copy(
                src_ref=x_ref, dst_ref=gathered.at[me],
                send_sem=send_sems.at[k - 1], recv_sem=recv_sems.at[k - 1],
                device_id=peer, device_id_type=MESH))
        for cp in copies:
            cp.start()
        for cp in copies:
            cp.wait()
        total = gathered[0]
        for b in range(1, N_DEV):
            total = total + gathered[b]
        o_ref[...] = total

    return pl.pallas_call(
        body,
        name="small_all_reduce",
        in_specs=[pl.BlockSpec(memory_space=pltpu.VMEM)],
        out_specs=pl.BlockSpec(memory_space=pltpu.VMEM),
        out_shape=jax.ShapeDtypeStruct((rows, LANES), F32),
        scratch_shapes=[pltpu.VMEM((N_DEV, rows, LANES), F32),
                        pltpu.SemaphoreType.DMA((N_DEV - 1,)), pltpu.SemaphoreType.DMA((N_DEV - 1,))],
    )(part)


def _adam_small(w, g, m, v):
    def fn(wv, gv, mv, vv):
        return _adam(wv, gv, mv, vv)

    shape = (w.shape, F32)
    return _rows("adam_small", fn, [(w, True), (g, True), (m, True), (v, True)], [shape] * 3, w.shape[0])


SMALL = ("ffn1_norm", "mix_norm", "attn_sinks", "hgrn_lower_bound", "hgrn_norm", "ffn2_norm", "ple_norm",
         "final_norm")
LARGE = ("ffn1_w_gate", "ffn1_w_up", "ffn1_w_down", "w_in", "w_up_a", "w_up_b", "w_out",
         "ffn2_w_gate", "ffn2_w_up", "ffn2_w_down", "ple_w_gate", "ple_w_proj")
WEIGHTS = ("ffn1_norm", "ffn1_w_gate", "ffn1_w_up", "ffn1_w_down", "mix_norm", "w_in", "attn_sinks",
           "hgrn_lower_bound", "hgrn_norm", "w_up_a", "w_up_b", "w_out", "ffn2_norm", "ffn2_w_gate",
           "ffn2_w_up", "ffn2_w_down", "ple_norm", "ple_w_gate", "ple_w_proj", "final_norm")


def _pack_rows(arrays):
    rows = []
    for a in arrays:
        flat = a.reshape(-1).astype(F32)
        pad = -flat.shape[0] % LANES
        rows.append(jnp.pad(flat, (0, pad)).reshape(-1, LANES))
    packed = jnp.concatenate(rows, axis=0)
    return jnp.pad(packed, ((0, -packed.shape[0] % 8), (0, 0)))


def _unpack_rows(packed, like):
    out, at = [], 0
    for a in like:
        size = a.size
        n_rows = -(-size // LANES)
        out.append(packed[at:at + n_rows].reshape(-1)[:size].reshape(a.shape))
        at += n_rows
    return out


def kernel(x, p, positions, ffn1_norm, ffn1_w_gate, ffn1_w_up, ffn1_w_down, mix_norm, w_in, attn_sinks, hgrn_lower_bound, hgrn_norm, w_up_a, w_up_b, w_out, ffn2_norm, ffn2_w_gate, ffn2_w_up, ffn2_w_down, ple_norm, ple_w_gate, ple_w_proj, final_norm, loss_target, m_ffn1_norm, m_ffn1_w_gate, m_ffn1_w_up, m_ffn1_w_down, m_mix_norm, m_w_in, m_attn_sinks, m_hgrn_lower_bound, m_hgrn_norm, m_w_up_a, m_w_up_b, m_w_out, m_ffn2_norm, m_ffn2_w_gate, m_ffn2_w_up, m_ffn2_w_down, m_ple_norm, m_ple_w_gate, m_ple_w_proj, m_final_norm, v_ffn1_norm, v_ffn1_w_gate, v_ffn1_w_up, v_ffn1_w_down, v_mix_norm, v_w_in, v_attn_sinks, v_hgrn_lower_bound, v_hgrn_norm, v_w_up_a, v_w_up_b, v_w_out, v_ffn2_norm, v_ffn2_w_gate, v_ffn2_w_up, v_ffn2_w_down, v_ple_norm, v_ple_w_gate, v_ple_w_proj, v_final_norm):
    given = dict(locals())
    w = {n: given[n] for n in WEIGHTS}
    mom = {n: given["m_" + n] for n in WEIGHTS}
    var = {n: given["v_" + n] for n in WEIGHTS}
    t, d = x.shape[1], x.shape[2]

    x0, p16, target = x[0], p[0, 0].astype(BF16), loss_target[0]
    tables = _rope_tables(positions)
    sinks = attn_sinks.reshape(-1)
    final_w = final_norm.reshape(1, d)
    shard = {n: w[n][0].astype(BF16) for n in LARGE}
    xi, yi, ci = _place()
    block_ids = jnp.stack([4 * _flip(xi, rx) + 2 * _flip(yi, ry) + ci for rx, ry in RELATIONS]).astype(jnp.int32)

    def gather(*names):
        return _gather_program([shard[n] for n in names])

    wg1, wu1 = _standalone("gather_ffn1", gather("ffn1_w_gate", "ffn1_w_up"))
    h1 = _rmsnorm("ffn1_norm", x0, ffn1_norm)
    (g1, u1, a1), (wd1, win_g) = _ffn_up("ffn1", h1, wg1, wu1, gather("ffn1_w_down", "w_in"))
    win = win_g.transpose(1, 0, 2).reshape(d, -1)
    x1, (wua, wub, wout_g) = _ffn_down("ffn1", x0, a1, wd1, gather("w_up_a", "w_up_b", "w_out"))
    wout = wout_g.reshape(d, d)
    mix_args = (mix_norm, win, sinks, hgrn_lower_bound, hgrn_norm)
    x2, mix_saved, got_w = _mixer_fwd(
        x1, *mix_args, lambda sent: (wua, wub, wout), tables,
        dict(proj=gather("ffn2_w_gate", "ffn2_w_up"), hgrn=gather("ffn2_w_down", "ple_w_gate", "ple_w_proj")))
    wg2, wu2 = got_w["proj"]
    wd2, wpg_g, wpp = got_w["hgrn"]
    wpg = wpg_g.reshape(d, d)
    h3 = _rmsnorm("ffn2_norm", x2, ffn2_norm)
    (g2, u2, a2), _ = _ffn_up("ffn2", h3, wg2, wu2)
    x3, _ = _ffn_down("ffn2", x2, a2, wd2)
    x4, ple_saved = _ple_fwd(x3, ple_norm, p16, wpg, wpp)
    dx4, d_final, loss_row = _loss_head(x4, final_w, target)

    sums, from_chips = {}, {}

    def to_chips(names, grads, from_sibling):
        for n, g, s in zip(names, grads, from_sibling):
            sums[n] = _pair_sum("pair_sum_" + n, g, s, block_ids)
        return _chips_program([sums[n] for n in names])

    def arrived(names, results):
        from_chips.update(zip(names, results))

    dx3, dx3_16, dwpp, dwpg, d_ple = _ple_bwd(dx4, x3, ple_norm, p16, wpg, ple_saved)
    ple_names = ("ple_w_proj", "ple_w_gate")
    ple_grads = [dwpp, dwpg.reshape(N_DEV, d // N_DEV, d)]
    gu2 = ("ffn2_w_gate", "ffn2_w_up")
    dx2, dx2_16, dwg2, dwu2, dwd2, d_ffn2, sent2 = _ffn_bwd(
        "ffn2b", dx3, dx3_16, x2, h3, g2, u2, a2, wg2, wu2, wd2, ffn2_norm,
        dict(dact=_sibling_program(ple_grads),
             dwgu=lambda av: to_chips(ple_names, ple_grads, av["sent"]["dact"]),
             dwd=lambda av: _sibling_program([av["dwg"], av["dwu"]]),
             dx=lambda av: _merge([to_chips(gu2, [av["dwg"], av["dwu"]], av["sent"]["dwd"]),
                                   _sibling_program([av["dwd"]])])))
    arrived(ple_names, sent2["dwgu"])
    arrived(gu2, sent2["dx"][:2])
    mix3 = ("w_out", "w_up_a", "w_up_b")
    dx1, dx1_16, mix_small, mix_made = _mixer_bwd(
        dx2, dx2_16, x1, *mix_args, wua, wub, wout, tables, mix_saved,
        dict(attention=lambda av: _merge([to_chips(("ffn2_w_down",), [dwd2], sent2["dx"][2:]),
                                          _sibling_program([av[n] for n in mix3])]),
             hgrn=lambda av: to_chips(mix3, [av[n] for n in mix3], av["sent"]["attention"][1:]),
             dx=lambda av: _sibling_program([av["w_in"]])))
    sent_mix = mix_made["sent"]
    arrived(("ffn2_w_down",), sent_mix["attention"][:1])
    arrived(mix3, sent_mix["hgrn"])
    gu1 = ("ffn1_w_gate", "ffn1_w_up")
    grad_x, _, dwg1, dwu1, dwd1, d_ffn1, sent1 = _ffn_bwd(
        "ffn1b", dx1, dx1_16, x0, h1, g1, u1, a1, wg1, wu1, wd1, ffn1_norm,
        dict(dact=to_chips(("w_in",), [mix_made["w_in"]], sent_mix["dx"]),
             dwd=lambda av: _sibling_program([av["dwg"], av["dwu"]]),
             dx=lambda av: _merge([to_chips(gu1, [av["dwg"], av["dwu"]], av["sent"]["dwd"]),
                                   _sibling_program([av["dwd"]])])))
    arrived(("w_in",), sent1["dact"])
    arrived(gu1, sent1["dx"][:2])
    arrived(("ffn1_w_down",), _standalone("grads_tail", to_chips(("ffn1_w_down",), [dwd1], sent1["dx"][2:])))

    out = {}
    for n in LARGE:
        shape = w[n].shape
        flat = lambda a: a.reshape(shape[1], shape[2])
        (res,), _ = _adam_sharded("adam_" + n, [(flat(w[n]), flat(mom[n]), flat(var[n]), sums[n], from_chips[n])])
        out[n] = [a.reshape(shape) for a in res]

    dlb = mix_small["hgrn_lb"]
    small_grads = dict(
        ffn1_norm=d_ffn1, mix_norm=mix_small["mix_norm"], attn_sinks=mix_small["attn_sinks"][:, :A_HEADS],
        hgrn_lower_bound=jnp.concatenate([dlb, -dlb], axis=0), hgrn_norm=mix_small["hgrn_norm"],
        ffn2_norm=d_ffn2, ple_norm=d_ple, final_norm=d_final)

    small_like = [w[n] for n in SMALL]
    packed = _pack_rows([small_grads[n].reshape(w[n].shape) for n in SMALL] + [loss_row[:, :1]])
    total = _all_reduce_small(packed)
    n_rows = total.shape[0]
    pack_w = _pack_rows(small_like + [jnp.zeros((1, 1), F32)])
    pack_m = _pack_rows([mom[n] for n in SMALL] + [jnp.zeros((1, 1), F32)])
    pack_v = _pack_rows([var[n] for n in SMALL] + [jnp.zeros((1, 1), F32)])
    deltas = _adam_small(pack_w, total, pack_m, pack_v)
    like = small_like + [jnp.zeros((1, 1), F32)]
    g_small = _unpack_rows(total, like)
    d_small, m_small, v_small = (_unpack_rows(a, like) for a in deltas)
    for k, n in enumerate(SMALL):
        out[n] = [g_small[k], d_small[k], m_small[k], v_small[k]]
    loss = g_small[-1].reshape(())

    return (loss, grad_x.reshape(x.shape),
            *[out[n][0] for n in WEIGHTS], *[out[n][1] for n in WEIGHTS],
            *[out[n][2] for n in WEIGHTS], *[out[n][3] for n in WEIGHTS])
```

```python
import functools

import jax
import jax.numpy as jnp
from jax import lax
from jax.experimental import pallas as pl
from jax.experimental.pallas import tpu as pltpu

F32 = jnp.float32
BF16 = jnp.bfloat16
MESH = pl.DeviceIdType.MESH

EPS = 1e-6
N_DEV = 8
A_HEADS = 16
A_HEAD_DIM = 64
A_WIDTH = 1024
A_KV_WIDTH = 256
ATT_BLOCK = 128
ROT_DIM = 16
ROPE_THETA = 500000.0
B_WIDTH = 1024
B_HEAD = 128
CHUNK = 64
SUB = 16
MAX_DECAY_EXP = 60.0
PLE_DIM = 256
LANES = 128

ADAM_LR = 0.001
ADAM_B1 = 0.9
ADAM_B2 = 0.999
ADAM_EPS = 1e-08
ADAM_WD = 0.01
ADAM_STEP = 10

VMEM_LIMIT = 56 * 1024 * 1024
EPILOGUE_ELEMS = 256 * 1024

NN = (((1,), (0,)), ((), ()))
NT = (((1,), (1,)), ((), ()))
TN = (((0,), (0,)), ((), ()))

OFF_QA = 0
OFF_KA = 1024
OFF_VA = 1280
OFF_QB = 1536
OFF_FB = 2560
OFF_IB = 3584
OFF_OG = 4608
OFF_GA = 5632


def _sig(v):
    return 1.0 / (1.0 + jnp.exp(-v))


def _tile(n, pref, mult):
    t = min(n, pref)
    t -= t % mult
    while n % t:
        t -= mult
    return t


def _params(sem):
    return pltpu.CompilerParams(dimension_semantics=sem, vmem_limit_bytes=VMEM_LIMIT)


HBM_SPEC = pl.BlockSpec(memory_space=pltpu.HBM)


class _Comm:
    def __init__(self, ins, out_shapes, n_sems, pre, post):
        self.ins, self.out_shapes, self.n_sems, self.pre, self.post = ins, out_shapes, n_sems, pre, post
        self.parts = [len(out_shapes)]

    def split(self, outs):
        res, at = [], 0
        for n in self.parts:
            res.append(list(outs[at:at + n]))
            at += n
        return res


def _merge(progs):
    spans, ins, shapes, sems = [], [], [], [0, 0, 0]
    for p in progs:
        spans.append((len(ins), len(shapes), tuple(sems)))
        ins += list(p.ins)
        shapes += list(p.out_shapes)
        sems = [a + b for a, b in zip(sems, p.n_sems)]

    def run(which):
        def go(cin, cout, send, recv, local, base):
            for p, (i0, o0, s0) in zip(progs, spans):
                getattr(p, which)(cin[i0:i0 + len(p.ins)], cout[o0:o0 + len(p.out_shapes)], send, recv, local,
                                  tuple(b + s for b, s in zip(base, s0)))
        return go

    merged = _Comm(ins, shapes, tuple(sems), run("pre"), run("post"))
    merged.parts = [len(p.out_shapes) for p in progs]
    return merged


def _call(name, body, grid, in_specs, out_specs, out_shape, scratch, args, comm=None):
    sem = ("arbitrary",) * len(grid)
    if comm is None:
        res = pl.pallas_call(body, name=name, grid=grid, in_specs=in_specs, out_specs=out_specs,
                             out_shape=out_shape, scratch_shapes=scratch, compiler_params=_params(sem))(*args)
        return list(res), []
    n_in, n_out, n_scr = len(in_specs), len(out_specs), len(scratch)
    c_in, c_out = len(comm.ins), len(comm.out_shapes)

    def carrier(*refs):
        at = [0]

        def take(n):
            at[0] += n
            return refs[at[0] - n:at[0]]

        ins, cins, outs, couts, scr, sems = take(n_in), take(c_in), take(n_out), take(c_out), take(n_scr), take(3)
        ids = [pl.program_id(a) for a in range(len(grid))]
        first = functools.reduce(jnp.logical_and, [i == 0 for i in ids])
        last = functools.reduce(jnp.logical_and, [i == g - 1 for i, g in zip(ids, grid)])

        @pl.when(first)
        def _():
            comm.pre(cins, couts, *sems, (0, 0, 0))

        body(*ins, *outs, *scr)

        @pl.when(last)
        def _():
            comm.post(cins, couts, *sems, (0, 0, 0))

    res = pl.pallas_call(
        carrier, name=name, grid=grid,
        in_specs=list(in_specs) + [HBM_SPEC] * c_in,
        out_specs=list(out_specs) + [HBM_SPEC] * c_out,
        out_shape=list(out_shape) + list(comm.out_shapes),
        scratch_shapes=list(scratch) + [pltpu.SemaphoreType.DMA((max(n, 1),)) for n in comm.n_sems],
        compiler_params=_params(sem),
    )(*args, *comm.ins)
    return list(res[:n_out]), list(res[n_out:])


def _mm(name, grid, ins, prods, acc_shapes, epi, outs, comm=None):
    n_in, n_out, nk = len(ins), len(outs), grid[2]
    tile_rows, tile_cols = acc_shapes[0]
    chunk = tile_rows
    while chunk * tile_cols > EPILOGUE_ELEMS and chunk % 32 == 0:
        chunk //= 2

    def body(*refs):
        in_refs = refs[:n_in]
        out_refs = refs[n_in:n_in + n_out]
        acc_refs = refs[n_in + n_out:]
        first_tile = jnp.logical_and(pl.program_id(0) == 0, pl.program_id(1) == 0)

        def product(lhs, rhs, dims):
            a = in_refs[lhs][...].astype(BF16)
            b = in_refs[rhs][...].astype(BF16)
            return lax.dot_general(a, b, dims, preferred_element_type=F32)

        def write(accs):
            totals = [None] * n_out
            for c in range(tile_rows // chunk):
                rows = slice(c * chunk, (c + 1) * chunk)
                vals = epi([a[rows] for a in accs], in_refs, rows)
                for idx, (o, v, spec) in enumerate(zip(out_refs, vals, outs)):
                    if spec[4]:
                        totals[idx] = v if totals[idx] is None else totals[idx] + v
                    else:
                        o[rows] = v.astype(o.dtype)
            for o, v, spec in zip(out_refs, totals, outs):
                if spec[4]:
                    @pl.when(first_tile)
                    def _():
                        o[...] = v.astype(o.dtype)

                    @pl.when(jnp.logical_not(first_tile))
                    def _():
                        o[...] += v.astype(o.dtype)

        if nk == 1:
            accs = [None] * len(acc_shapes)
            for lhs, rhs, acc, dims in prods:
                t = product(lhs, rhs, dims)
                accs[acc] = t if accs[acc] is None else accs[acc] + t
            write(accs)
        else:
            k = pl.program_id(2)

            @pl.when(k == 0)
            def _():
                for r in acc_refs:
                    r[...] = jnp.zeros(r.shape, F32)

            for lhs, rhs, acc, dims in prods:
                acc_refs[acc][...] += product(lhs, rhs, dims)

            @pl.when(k == nk - 1)
            def _():
                write(acc_refs)

    res, sent = _call(
        name, body, grid,
        [pl.BlockSpec(b, im) for _, b, im in ins],
        [pl.BlockSpec(o[2], o[3]) for o in outs],
        [jax.ShapeDtypeStruct(o[0], o[1]) for o in outs],
        [] if nk == 1 else [pltpu.VMEM(s, F32) for s in acc_shapes],
        [a for a, _, _ in ins], comm)
    return res if comm is None else (res, sent)


def _rms_bwd(dh, xv, w, dres):
    r = lax.rsqrt(jnp.mean(xv * xv, axis=-1, keepdims=True) + EPS)
    g = dh * w
    dx = dres + r * (g - xv * (r * r) * jnp.mean(g * xv, axis=-1, keepdims=True))
    dw = jnp.sum(dh * xv * r, axis=0, keepdims=True)
    return dx, dw


def _rows(name, fn, ins, outs, tm):
    n_in = len(ins)
    rows = outs[0][0][0]

    def body(*refs):
        vals = fn(*[r[...] for r in refs[:n_in]])
        for o, v in zip(refs[n_in:], vals):
            o[...] = v.astype(o.dtype)

    in_specs = []
    for a, tiled in ins:
        if tiled:
            in_specs.append(pl.BlockSpec((tm, a.shape[1]), lambda i: (i, 0)))
        else:
            in_specs.append(pl.BlockSpec(a.shape, lambda i, nd=a.ndim: (0,) * nd))
    return pl.pallas_call(
        body,
        name=name,
        grid=(rows // tm,),
        in_specs=in_specs,
        out_specs=[pl.BlockSpec((tm, s[1]), lambda i: (i, 0)) for s, _ in outs],
        out_shape=[jax.ShapeDtypeStruct(s, d) for s, d in outs],
        compiler_params=_params(("arbitrary",)),
    )(*[a for a, _ in ins])


def _rmsnorm(name, x, w):
    def fn(xv, wv):
        r = lax.rsqrt(jnp.mean(xv * xv, axis=-1, keepdims=True) + EPS)
        return (xv * r * wv,)

    return _rows(name, fn, [(x, True), (w, False)], [(x.shape, BF16)], _tile(x.shape[0], 512, 16))[0]


def _mmc(*args, comm=None):
    if comm is None:
        return _mm(*args), []
    return _mm(*args, comm=comm)


def _carried(comms, key, avail):
    c = comms.get(key) if comms else None
    return c(avail) if callable(c) else c


def _ffn_up(name, h, wg, wu, comm=None):
    t, d = h.shape
    n = wg.shape[2]
    tm = _tile(t, 1024, 16)

    def epi_up(accs, refs, rows):
        g, u = accs
        return [g, u, g * _sig(g) * u]

    hid = ((N_DEV, t, n), BF16, (None, tm, n), lambda i, j, k: (j, i, 0), False)
    return _mmc(
        name + "_up", (t // tm, N_DEV, 1),
        [(h, (tm, d), lambda i, j, k: (i, 0)),
         (wg, (None, d, n), lambda i, j, k: (j, 0, 0)),
         (wu, (None, d, n), lambda i, j, k: (j, 0, 0))],
        [(0, 1, 0, NN), (0, 2, 1, NN)], [(tm, n)] * 2, epi_up, [hid] * 3, comm=comm)


def _ffn_down(name, x, act, wd, comm=None):
    t, d = x.shape
    n = wd.shape[1]
    tm2 = _tile(t, 1024, 8)
    tn = _tile(d, 1024, 128)

    def epi_down(accs, refs, rows):
        return [refs[2][rows] + 0.5 * accs[0]]

    (x_out,), sent = _mmc(
        name + "_down", (t // tm2, d // tn, N_DEV),
        [(act, (None, tm2, n), lambda i, j, k: (k, i, 0)),
         (wd, (None, n, tn), lambda i, j, k: (k, 0, j)),
         (x, (tm2, tn), lambda i, j, k: (i, j))],
        [(0, 1, 0, NN)], [(tm2, tn)], epi_down,
        [((t, d), F32, (tm2, tn), lambda i, j, k: (i, j), False)], comm=comm)
    return x_out, sent


def _ffn_bwd(name, dxo, dxo16, x, h, gate, up, act, wg, wu, wd, norm_w, comms=None):
    t, d = x.shape
    n = wg.shape[2]
    tm = _tile(t, 1024, 16)
    sent = {}
    avail = dict(sent=sent)

    def epi_act(accs, refs, rows):
        da = 0.5 * accs[0]
        g = refs[2][rows].astype(F32)
        u = refs[3][rows].astype(F32)
        s = _sig(g)
        return [da * u * (s * (1.0 + g * (1.0 - s))), da * g * s]

    hid_in = lambda a: (a, (None, tm, n), lambda i, j, k: (j, i, 0))
    hid_out = ((N_DEV, t, n), BF16, (None, tm, n), lambda i, j, k: (j, i, 0), False)
    (dgate, dup), sent["dact"] = _mmc(
        name + "_dact", (t // tm, N_DEV, 1),
        [(dxo16, (tm, d), lambda i, j, k: (i, 0)),
         (wd, (None, n, d), lambda i, j, k: (j, 0, 0)),
         hid_in(gate), hid_in(up)],
        [(0, 1, 0, NT)], [(tm, n)], epi_act, [hid_out] * 2, comm=_carried(comms, "dact", avail))

    tk = _tile(t, 1024, 16)
    td = _tile(d, 1024, 128)
    w_out = ((N_DEV, d, n), BF16, (None, td, n), lambda i, j, k: (j, i, 0), False)
    (dwg, dwu), sent["dwgu"] = _mmc(
        name + "_dwgu", (d // td, N_DEV, t // tk),
        [(h, (tk, td), lambda i, j, k: (k, i)),
         (dgate, (None, tk, n), lambda i, j, k: (j, k, 0)),
         (dup, (None, tk, n), lambda i, j, k: (j, k, 0))],
        [(0, 1, 0, TN), (0, 2, 1, TN)], [(td, n)] * 2, lambda accs, refs, rows: accs, [w_out] * 2,
        comm=_carried(comms, "dwgu", avail))
    avail.update(dwg=dwg, dwu=dwu)

    (dwd,), sent["dwd"] = _mmc(
        name + "_dwd", (N_DEV, 1, t // tk),
        [(act, (None, tk, n), lambda i, j, k: (i, k, 0)),
         (dxo16, (tk, d), lambda i, j, k: (k, 0))],
        [(0, 1, 0, TN)], [(n, d)], lambda accs, refs, rows: [0.5 * accs[0]],
        [((N_DEV, n, d), BF16, (None, n, d), lambda i, j, k: (i, 0, 0), False)],
        comm=_carried(comms, "dwd", avail))
    avail.update(dwd=dwd)

    tm3 = _tile(t, 512, 16)

    def epi_dx(accs, refs, rows):
        dx, dw = _rms_bwd(accs[0], refs[4][rows], refs[5][...], refs[6][rows])
        return [dx, dx, dw]

    (dx, dx16, dnorm), sent["dx"] = _mmc(
        name + "_dx", (t // tm3, 1, N_DEV),
        [(dgate, (None, tm3, n), lambda i, j, k: (k, i, 0)),
         (wg, (None, d, n), lambda i, j, k: (k, 0, 0)),
         (dup, (None, tm3, n), lambda i, j, k: (k, i, 0)),
         (wu, (None, d, n), lambda i, j, k: (k, 0, 0)),
         (x, (tm3, d), lambda i, j, k: (i, 0)),
         (norm_w, (1, d), lambda i, j, k: (0, 0)),
         (dxo, (tm3, d), lambda i, j, k: (i, 0))],
        [(0, 1, 0, NT), (2, 3, 0, NT)], [(tm3, d)], epi_dx,
        [((t, d), F32, (tm3, d), lambda i, j, k: (i, 0), False),
         ((t, d), BF16, (tm3, d), lambda i, j, k: (i, 0), False),
         ((1, d), F32, (1, d), lambda i, j, k: (0, 0), True)],
        comm=_carried(comms, "dx", avail))
    return dx, dx16, dwg, dwu, dwd, dnorm, sent


def _rope(tv, cos, s1, s2):
    return tv * cos + pltpu.roll(tv, LANES - 8, 1) * s1 + pltpu.roll(tv, 8, 1) * s2


def _rope_bwd(dr, cos, s1, s2):
    return dr * cos + pltpu.roll(dr * s1, 8, 1) + pltpu.roll(dr * s2, LANES - 8, 1)


def _rope_tables(positions):
    inv_freq = jnp.power(jnp.float32(ROPE_THETA), -jnp.arange(0, ROT_DIM, 2, dtype=F32) / ROT_DIM)
    ang = positions.reshape(-1).astype(F32)[:, None] * inv_freq
    cos, sin = jnp.cos(ang), jnp.sin(ang)
    t = ang.shape[0]
    pad = A_HEAD_DIM - ROT_DIM
    cos_t = jnp.concatenate([cos, cos, jnp.ones((t, pad), F32)], axis=1)
    s1_t = jnp.concatenate([-sin, jnp.zeros((t, pad + 8), F32)], axis=1)
    s2_t = jnp.concatenate([jnp.zeros((t, 8), F32), sin, jnp.zeros((t, pad), F32)], axis=1)
    return tuple(jnp.tile(v, (1, 2)) for v in (cos_t, s1_t, s2_t))


def _att_common(i, k_p, k_c, v_p, v_c, tabs_p, tabs_c):
    kcat = jnp.concatenate([k_p, k_c], axis=0)
    vcat = jnp.concatenate([v_p, v_c], axis=0)
    tabs_k = [jnp.concatenate([a, b], axis=0) for a, b in zip(tabs_p, tabs_c)]
    kvar, vvar = [], []
    for ks in range(2):
        sl = slice(ks * LANES, (ks + 1) * LANES)
        kr = _rope(kcat[:, sl], *tabs_k)
        kvar.append([kr.astype(BF16), pltpu.roll(kr, 64, 1).astype(BF16)])
        vvar.append([vcat[:, sl].astype(BF16), pltpu.roll(vcat[:, sl], 64, 1).astype(BF16)])
    qi = lax.broadcasted_iota(jnp.int32, (ATT_BLOCK, 2 * ATT_BLOCK), 0)
    kj = lax.broadcasted_iota(jnp.int32, (ATT_BLOCK, 2 * ATT_BLOCK), 1)
    dist = qi + ATT_BLOCK - kj
    allowed = (dist >= 0) & (dist < ATT_BLOCK) & ((i > 0) | (kj >= ATT_BLOCK))
    lane = lax.broadcasted_iota(jnp.int32, (1, LANES), 1)
    halves = [lane < 64, lane >= 64]
    return kvar, vvar, allowed, halves


def _att_head_probs(qm, kslab, allowed, sink):
    s = lax.dot_general(qm, kslab, NT, preferred_element_type=F32)
    s = jnp.where(allowed, s, -jnp.inf)
    m = jnp.maximum(jnp.max(s, axis=-1, keepdims=True), sink)
    e = jnp.exp(s - m)
    es = jnp.exp(sink - m)
    den = jnp.sum(e, axis=-1, keepdims=True) + es
    return e / den, es / den


def _head_slots(h):
    pair, half = h // 2, h % 2
    kvh = h // 4
    return pair, half, kvh // 2, int(kvh % 2 != half)


def _attention_fwd(proj, tables, sinks, comm=None):
    t = proj.shape[0]
    nb = t // ATT_BLOCK
    b = ATT_BLOCK
    scale = A_HEAD_DIM ** -0.5

    def body(sink_ref, q_ref, kp_ref, kc_ref, vp_ref, vc_ref,
             cp_ref, ap_ref, bp_ref, cc_ref, ac_ref, bc_ref, o_ref):
        i = pl.program_id(0)
        tabs_p = (cp_ref[...], ap_ref[...], bp_ref[...])
        tabs_c = (cc_ref[...], ac_ref[...], bc_ref[...])
        kvar, vvar, allowed, halves = _att_common(
            i, kp_ref[...], kc_ref[...], vp_ref[...], vc_ref[...], tabs_p, tabs_c)
        for pair in range(A_HEADS // 2):
            sl = slice(pair * LANES, (pair + 1) * LANES)
            q2 = _rope(q_ref[:, sl], *tabs_c) * scale
            acc = jnp.zeros((b, LANES), F32)
            for half in range(2):
                h = 2 * pair + half
                _, _, ks, var = _head_slots(h)
                qm = jnp.where(halves[half], q2, 0.0).astype(BF16)
                p, _ = _att_head_probs(qm, kvar[ks][var], allowed, sink_ref[h])
                o = lax.dot_general(p.astype(BF16), vvar[ks][var], NN, preferred_element_type=F32)
                acc = acc + jnp.where(halves[half], o, 0.0)
            o_ref[:, sl] = acc.astype(o_ref.dtype)

    cur = lambda col: (lambda i: (i, col))
    prev = lambda col: (lambda i: (jnp.maximum(i - 1, 0), col))
    tab = [pl.BlockSpec((b, LANES), prev(0))] * 3 + [pl.BlockSpec((b, LANES), cur(0))] * 3
    (out,), sent = _call(
        "attention_fwd", body, (nb,),
        [pl.BlockSpec(memory_space=pltpu.SMEM),
         pl.BlockSpec((b, A_WIDTH), cur(0)),
         pl.BlockSpec((b, A_KV_WIDTH), prev(OFF_KA // A_KV_WIDTH)),
         pl.BlockSpec((b, A_KV_WIDTH), cur(OFF_KA // A_KV_WIDTH)),
         pl.BlockSpec((b, A_KV_WIDTH), prev(OFF_VA // A_KV_WIDTH)),
         pl.BlockSpec((b, A_KV_WIDTH), cur(OFF_VA // A_KV_WIDTH))] + tab,
        [pl.BlockSpec((b, A_WIDTH), cur(0))],
        [jax.ShapeDtypeStruct((t, A_WIDTH), BF16)], [],
        [sinks, proj, proj, proj, proj, proj, *tables, *tables], comm)
    return out, sent


def _attention_bwd(proj, tables, sinks, d_out, comm=None):
    t = proj.shape[0]
    nb = t // ATT_BLOCK
    b = ATT_BLOCK
    scale = A_HEAD_DIM ** -0.5

    def body(sink_ref, q_ref, kp_ref, kc_ref, vp_ref, vc_ref,
             cp_ref, ap_ref, bp_ref, cc_ref, ac_ref, bc_ref, do_ref,
             dq_ref, dk_ref, dv_ref, dsink_ref, dk_carry, dv_carry):
        step = pl.program_id(0)
        i = nb - 1 - step

        @pl.when(step == 0)
        def _():
            dk_carry[...] = jnp.zeros(dk_carry.shape, F32)
            dv_carry[...] = jnp.zeros(dv_carry.shape, F32)
            dsink_ref[...] = jnp.zeros(dsink_ref.shape, F32)

        tabs_p = (cp_ref[...], ap_ref[...], bp_ref[...])
        tabs_c = (cc_ref[...], ac_ref[...], bc_ref[...])
        kvar, vvar, allowed, halves = _att_common(
            i, kp_ref[...], kc_ref[...], vp_ref[...], vc_ref[...], tabs_p, tabs_c)
        lane = lax.broadcasted_iota(jnp.int32, (1, LANES), 1)
        dk_slab = [jnp.zeros((2 * b, LANES), F32) for _ in range(2)]
        dv_slab = [jnp.zeros((2 * b, LANES), F32) for _ in range(2)]
        dsink = jnp.zeros((1, LANES), F32)
        for pair in range(A_HEADS // 2):
            sl = slice(pair * LANES, (pair + 1) * LANES)
            q2 = _rope(q_ref[:, sl], *tabs_c) * scale
            do2 = do_ref[:, sl].astype(F32)
            dq2 = jnp.zeros((b, LANES), F32)
            for half in range(2):
                h = 2 * pair + half
                _, _, ks, var = _head_slots(h)
                qm = jnp.where(halves[half], q2, 0.0).astype(BF16)
                dom = jnp.where(halves[half], do2, 0.0).astype(BF16)
                p, p_sink = _att_head_probs(qm, kvar[ks][var], allowed, sink_ref[h])
                dp = lax.dot_general(dom, vvar[ks][var], NT, preferred_element_type=F32)
                dsum = jnp.sum(p * dp, axis=-1, keepdims=True)
                ds = (p * (dp - dsum)).astype(BF16)
                dsink = dsink + jnp.where(lane == h, -jnp.sum(p_sink * dsum), 0.0)
                dqh = lax.dot_general(ds, kvar[ks][var], NN, preferred_element_type=F32)
                dq2 = dq2 + jnp.where(halves[half], dqh, 0.0)
                dkx = lax.dot_general(ds, qm, TN, preferred_element_type=F32)
                dvx = lax.dot_general(p.astype(BF16), dom, TN, preferred_element_type=F32)
                if var:
                    dkx = pltpu.roll(dkx, 64, 1)
                    dvx = pltpu.roll(dvx, 64, 1)
                dk_slab[ks] = dk_slab[ks] + dkx
                dv_slab[ks] = dv_slab[ks] + dvx
            dq_ref[:, sl] = _rope_bwd(dq2 * scale, *tabs_c).astype(dq_ref.dtype)
        for ks in range(2):
            sl = slice(ks * LANES, (ks + 1) * LANES)
            dk_cur = dk_slab[ks][b:] + dk_carry[:, sl]
            dk_ref[:, sl] = _rope_bwd(dk_cur, *tabs_c).astype(dk_ref.dtype)
            dv_ref[:, sl] = (dv_slab[ks][b:] + dv_carry[:, sl]).astype(dv_ref.dtype)
            dk_carry[:, sl] = dk_slab[ks][:b]
            dv_carry[:, sl] = dv_slab[ks][:b]
        dsink_ref[...] += dsink

    cur = lambda col: (lambda s: (nb - 1 - s, col))
    prev = lambda col: (lambda s: (jnp.maximum(nb - 2 - s, 0), col))
    tab = [pl.BlockSpec((b, LANES), prev(0))] * 3 + [pl.BlockSpec((b, LANES), cur(0))] * 3
    return _call(
        "attention_bwd", body, (nb,),
        [pl.BlockSpec(memory_space=pltpu.SMEM),
         pl.BlockSpec((b, A_WIDTH), cur(0)),
         pl.BlockSpec((b, A_KV_WIDTH), prev(OFF_KA // A_KV_WIDTH)),
         pl.BlockSpec((b, A_KV_WIDTH), cur(OFF_KA // A_KV_WIDTH)),
         pl.BlockSpec((b, A_KV_WIDTH), prev(OFF_VA // A_KV_WIDTH)),
         pl.BlockSpec((b, A_KV_WIDTH), cur(OFF_VA // A_KV_WIDTH))] + tab
        + [pl.BlockSpec((b, A_WIDTH), cur(0))],
        [pl.BlockSpec((b, A_WIDTH), cur(0)),
         pl.BlockSpec((b, A_KV_WIDTH), cur(0)),
         pl.BlockSpec((b, A_KV_WIDTH), cur(0)),
         pl.BlockSpec((1, LANES), lambda s: (0, 0))],
        [jax.ShapeDtypeStruct((t, A_WIDTH), BF16),
         jax.ShapeDtypeStruct((t, A_KV_WIDTH), BF16),
         jax.ShapeDtypeStruct((t, A_KV_WIDTH), BF16),
         jax.ShapeDtypeStruct((1, LANES), F32)],
        [pltpu.VMEM((b, A_KV_WIDTH), F32), pltpu.VMEM((b, A_KV_WIDTH), F32)],
        [sinks, proj, proj, proj, proj, proj, *tables, *tables, d_out], comm)


HGRN_HEADS_PER_STEP = 4
HGRN_W = HGRN_HEADS_PER_STEP * B_HEAD


def _hgrn_gates(qp, fp, hlb):
    lb = _sig(hlb[0:1] - hlb[1:2])
    sg = _sig(fp)
    big_f = lb + (1.0 - lb) * sg
    kk = (1.0 - lb) * (1.0 - sg)
    sq = _sig(qp)
    return lb, sg, big_f, kk, sq


def _tri(lower):
    r = lax.broadcasted_iota(jnp.int32, (CHUNK, CHUNK), 0)
    c = lax.broadcasted_iota(jnp.int32, (CHUNK, CHUNK), 1)
    return (r >= c) if lower else (r <= c)


def _hgrn_factors(q_ref, k_ref, b_ref, sl):
    out = []
    bh = b_ref[:, sl]
    for i in range(CHUNK // SUB):
        rows = slice(i * SUB, (i + 1) * SUB)
        b0 = b_ref[i * SUB - 1:i * SUB, sl] if i else jnp.zeros((1, B_HEAD), F32)
        eq = jnp.exp(b_ref[rows, sl] - b0)
        ek = jnp.exp(jnp.minimum(b0 - bh, MAX_DECAY_EXP))
        out.append((q_ref[rows, sl] * eq, k_ref[:, sl] * ek, eq, ek))
    return out


def _hgrn_scores(factors):
    rows = [lax.dot_general(qd.astype(BF16), kx.astype(BF16), NT, preferred_element_type=F32)
            for qd, kx, _, _ in factors]
    return jnp.where(_tri(True), jnp.concatenate(rows, axis=0), 0.0)


def _hgrn_specs(nc, reverse):
    w = HGRN_W
    chunk = (lambda c: nc - 1 - c) if reverse else (lambda c: c)
    col = lambda off: pl.BlockSpec((CHUNK, w), lambda hh, c: (chunk(c), off // w + hh))
    return chunk, col


def _hgrn_fwd(proj, hlb, wn, comm=None):
    t = proj.shape[0]
    nc = t // CHUNK
    w = HGRN_W
    chunk, col = _hgrn_specs(nc, False)

    def body(q_ref, f_ref, v_ref, og_ref, hlb_ref, wn_ref, ob_ref, oraw_ref, st_ref,
             state, qs, ks, bs):
        c = pl.program_id(1)

        @pl.when(c == 0)
        def _():
            state[...] = jnp.zeros(state.shape, F32)

        qp = q_ref[...]
        lb, sg, big_f, kk, sq = _hgrn_gates(qp, f_ref[...], hlb_ref[...])
        qs[...] = qp * sq
        ks[...] = kk
        bs[...] = lax.dot_general(_tri(True).astype(F32), jnp.log(big_f), NN,
                                  precision=lax.Precision.HIGHEST, preferred_element_type=F32)
        for h in range(HGRN_HEADS_PER_STEP):
            sl = slice(h * B_HEAD, (h + 1) * B_HEAD)
            bh = bs[:, sl]
            qh = qs[:, sl]
            kh = ks[:, sl]
            vh = v_ref[:, sl].astype(BF16)
            s0 = state[h]
            st_ref[h] = s0.astype(st_ref.dtype)
            a = _hgrn_scores(_hgrn_factors(qs, ks, bs, sl))
            o = lax.dot_general(a.astype(BF16), vh, NN, preferred_element_type=F32)
            o = o + lax.dot_general((qh * jnp.exp(bh)).astype(BF16), s0.astype(BF16), NT,
                                    preferred_element_type=F32)
            bl = bs[CHUNK - 1:CHUNK, sl]
            khat = (kh * jnp.exp(bl - bh)).astype(BF16)
            state[h] = s0 * jnp.exp(bl) + lax.dot_general(vh, khat, TN, preferred_element_type=F32)
            oraw_ref[:, sl] = o
            r = lax.rsqrt(jnp.mean(o * o, axis=-1, keepdims=True) + EPS)
            og = og_ref[:, sl]
            ob_ref[:, sl] = (o * r * wn_ref[:, sl] * (og * _sig(og))).astype(ob_ref.dtype)

    vec = lambda rows: pl.BlockSpec((rows, w), lambda hh, c: (0, hh))
    return _call(
        "hgrn_fwd", body, (B_WIDTH // w, nc),
        [col(OFF_QB), col(OFF_FB), col(OFF_IB), col(OFF_OG), vec(2), vec(1)],
        [pl.BlockSpec((CHUNK, w), lambda hh, c: (c, hh)),
         pl.BlockSpec((CHUNK, w), lambda hh, c: (c, hh)),
         pl.BlockSpec((None, HGRN_HEADS_PER_STEP, B_HEAD, B_HEAD), lambda hh, c: (c, hh, 0, 0))],
        [jax.ShapeDtypeStruct((t, B_WIDTH), BF16),
         jax.ShapeDtypeStruct((t, B_WIDTH), F32),
         jax.ShapeDtypeStruct((nc, B_WIDTH // B_HEAD, B_HEAD, B_HEAD), BF16)],
        [pltpu.VMEM((HGRN_HEADS_PER_STEP, B_HEAD, B_HEAD), F32),
         pltpu.VMEM((CHUNK, w), F32), pltpu.VMEM((CHUNK, w), F32), pltpu.VMEM((CHUNK, w), F32)],
        [proj, proj, proj, proj, hlb, wn], comm)


def _hgrn_bwd(proj, hlb, wn, o_raw, states, d_out, comm=None):
    t = proj.shape[0]
    nc = t // CHUNK
    w = HGRN_W
    chunk, col = _hgrn_specs(nc, True)

    def body(q_ref, f_ref, v_ref, og_ref, hlb_ref, wn_ref, oraw_ref, st_ref, dout_ref,
             dq_ref, df_ref, dv_ref, dog_ref, dlb_ref, dwn_ref,
             dstate, qs, ks, bs, dbs, dks, dqs):
        c = pl.program_id(1)

        @pl.when(c == 0)
        def _():
            dstate[...] = jnp.zeros(dstate.shape, F32)
            dlb_ref[...] = jnp.zeros(dlb_ref.shape, F32)
            dwn_ref[...] = jnp.zeros(dwn_ref.shape, F32)

        qp = q_ref[...]
        lb, sg, big_f, kk, sq = _hgrn_gates(qp, f_ref[...], hlb_ref[...])
        qs[...] = qp * sq
        ks[...] = kk
        bs[...] = lax.dot_general(_tri(True).astype(F32), jnp.log(big_f), NN,
                                  precision=lax.Precision.HIGHEST, preferred_element_type=F32)
        last_row = lax.broadcasted_iota(jnp.int32, (CHUNK, 1), 0) == CHUNK - 1
        for h in range(HGRN_HEADS_PER_STEP):
            sl = slice(h * B_HEAD, (h + 1) * B_HEAD)
            bh = bs[:, sl]
            qh = qs[:, sl]
            kh = ks[:, sl]
            vh = v_ref[:, sl].astype(BF16)
            s0 = st_ref[h]
            ds1 = dstate[h]
            o = oraw_ref[:, sl]
            og = og_ref[:, sl]
            wnh = wn_ref[:, sl]
            sog = _sig(og)
            r = lax.rsqrt(jnp.mean(o * o, axis=-1, keepdims=True) + EPS)
            dob = dout_ref[:, sl].astype(F32)
            don = dob * (og * sog)
            dog_ref[:, sl] = (dob * (o * r * wnh) * (sog * (1.0 + og * (1.0 - sog)))).astype(dog_ref.dtype)
            dwn_ref[:, sl] += jnp.sum(don * o * r, axis=0, keepdims=True)
            gp = don * wnh
            do = r * (gp - o * (r * r) * jnp.mean(gp * o, axis=-1, keepdims=True))
            do16 = do.astype(BF16)
            factors = _hgrn_factors(qs, ks, bs, sl)
            a = _hgrn_scores(factors)
            da = jnp.where(_tri(True), lax.dot_general(do16, vh, NT, preferred_element_type=F32), 0.0)
            dv = lax.dot_general(a.astype(BF16), do16, TN, preferred_element_type=F32)
            dq_rows = []
            dk = jnp.zeros((CHUNK, B_HEAD), F32)
            for i, (qd, kx, eq, ek) in enumerate(factors):
                da_i = da[i * SUB:(i + 1) * SUB]
                dq_rows.append(lax.dot_general(da_i, kx, NN, precision=lax.Precision.HIGHEST,
                                               preferred_element_type=F32) * eq)
                dk = dk + lax.dot_general(da_i, qd, TN, precision=lax.Precision.HIGHEST,
                                          preferred_element_type=F32) * ek
            eb = jnp.exp(bh)
            bl = bs[CHUNK - 1:CHUNK, sl]
            ebl = jnp.exp(bl)
            ekl = jnp.exp(bl - bh)
            qe = (qh * eb).astype(BF16)
            khat = (kh * ekl).astype(BF16)
            ds1_16 = ds1.astype(BF16)
            dq = jnp.concatenate(dq_rows, axis=0) + eb * lax.dot_general(
                do16, s0, NN, preferred_element_type=F32)
            dv = dv + lax.dot_general(khat, ds1_16, NT, preferred_element_type=F32)
            dk_state = ekl * lax.dot_general(vh, ds1_16, NN, preferred_element_type=F32)
            dk = dk + dk_state
            db_last = (jnp.sum(kh * dk_state, axis=0, keepdims=True)
                       + ebl * jnp.sum(s0.astype(F32) * ds1, axis=0, keepdims=True))
            dbs[:, sl] = qh * dq - kh * dk + jnp.where(last_row, db_last, 0.0)
            dks[:, sl] = dk
            dqs[:, sl] = dq
            dstate[h] = ds1 * ebl + lax.dot_general(do16, qe, TN, preferred_element_type=F32)
            dv_ref[:, sl] = dv.astype(dv_ref.dtype)
        dg = lax.dot_general(_tri(False).astype(F32), dbs[...], NN,
                             precision=lax.Precision.HIGHEST, preferred_element_type=F32)
        dq_ref[...] = (dqs[...] * (sq * (1.0 + qp * (1.0 - sq)))).astype(dq_ref.dtype)
        df_minus_dk = dg / big_f - dks[...]
        df_ref[...] = ((1.0 - lb) * df_minus_dk * sg * (1.0 - sg)).astype(df_ref.dtype)
        dlb_ref[...] += jnp.sum((1.0 - sg) * df_minus_dk, axis=0, keepdims=True) * (lb * (1.0 - lb))

    vec = lambda rows: pl.BlockSpec((rows, w), lambda hh, c: (0, hh))
    blk = pl.BlockSpec((CHUNK, w), lambda hh, c: (chunk(c), hh))
    act = jax.ShapeDtypeStruct((t, B_WIDTH), BF16)
    row = jax.ShapeDtypeStruct((1, B_WIDTH), F32)
    return _call(
        "hgrn_bwd", body, (B_WIDTH // w, nc),
        [col(OFF_QB), col(OFF_FB), col(OFF_IB), col(OFF_OG), vec(2), vec(1), blk,
         pl.BlockSpec((None, HGRN_HEADS_PER_STEP, B_HEAD, B_HEAD), lambda hh, c: (chunk(c), hh, 0, 0)),
         blk],
        [blk, blk, blk, blk, vec(1), vec(1)],
        [act, act, act, act, row, row],
        [pltpu.VMEM((HGRN_HEADS_PER_STEP, B_HEAD, B_HEAD), F32)] + [pltpu.VMEM((CHUNK, w), F32)] * 6,
        [proj, proj, proj, proj, hlb, wn, o_raw, states, d_out], comm)


def _mixer_fwd(x1, mix_norm, w_in, sinks, hlb, wn, weights, tables, comms):
    t, d = x1.shape
    n_in = w_in.shape[1]
    nb = d // N_DEV
    sent = {}
    h2 = _rmsnorm("mix_norm", x1, mix_norm)
    tm = _tile(t, 1024, 16)
    tn = _tile(n_in, 512, 128)
    (proj,), sent["proj"] = _mmc(
        "mix_proj", (t // tm, n_in // tn, 1),
        [(h2, (tm, d), lambda i, j, k: (i, 0)), (w_in, (d, tn), lambda i, j, k: (0, j))],
        [(0, 1, 0, NN)], [(tm, tn)], lambda accs, refs, rows: accs,
        [((t, n_in), F32, (tm, tn), lambda i, j, k: (i, j), False)], comm=comms.get("proj"))
    out_a, sent["attention"] = _attention_fwd(proj, tables, sinks, comms.get("attention"))
    (out_b, o_raw, states), sent["hgrn"] = _hgrn_fwd(proj, hlb, wn, comms.get("hgrn"))
    wua, wub, w_out = weights(sent)

    def epi_merge(accs, refs, rows):
        ua, ub = accs
        return [_sig(refs[4][rows]) * ua + _sig(refs[5][rows]) * ub, ua, ub]

    wide = ((t, d), BF16, (tm, nb), lambda i, j, k: (i, j), False)
    (merged, ua, ub), sent["merge"] = _mmc(
        "mix_merge", (t // tm, N_DEV, 1),
        [(out_a, (tm, A_WIDTH), lambda i, j, k: (i, 0)),
         (wua, (None, A_WIDTH, nb), lambda i, j, k: (j, 0, 0)),
         (out_b, (tm, B_WIDTH), lambda i, j, k: (i, 0)),
         (wub, (None, B_WIDTH, nb), lambda i, j, k: (j, 0, 0)),
         (proj, (tm, nb), lambda i, j, k: (i, OFF_GA // nb + j)),
         (proj, (tm, nb), lambda i, j, k: (i, (OFF_GA + d) // nb + j))],
        [(0, 1, 0, NN), (2, 3, 1, NN)], [(tm, nb)] * 2, epi_merge, [wide] * 3, comm=comms.get("merge"))

    tn2 = _tile(d, 1024, 128)
    (x2,) = _mm(
        "mix_out", (t // tm, d // tn2, 1),
        [(merged, (tm, d), lambda i, j, k: (i, 0)), (w_out, (d, tn2), lambda i, j, k: (0, j)),
         (x1, (tm, tn2), lambda i, j, k: (i, j))],
        [(0, 1, 0, NN)], [(tm, tn2)], lambda accs, refs, rows: [refs[2][rows] + accs[0]],
        [((t, d), F32, (tm, tn2), lambda i, j, k: (i, j), False)])
    saved = dict(h2=h2, proj=proj, out_a=out_a, out_b=out_b, o_raw=o_raw, states=states,
                 merged=merged, ua=ua, ub=ub)
    return x2, saved, sent


def _mixer_bwd(dx2, dx2_16, x1, mix_norm, w_in, sinks, hlb, wn, wua, wub, w_out, tables, sv, comms):
    t, d = x1.shape
    sent = {}
    avail = dict(sent=sent)
    n_in = w_in.shape[1]
    nb = d // N_DEV
    proj = sv["proj"]
    tm = _tile(t, 1024, 16)
    tn = _tile(d, 512, 128)

    def epi_gate(accs, refs, rows):
        dm = accs[0]
        sa = _sig(refs[2][rows])
        sb = _sig(refs[3][rows])
        ua = refs[4][rows].astype(F32)
        ub = refs[5][rows].astype(F32)
        return [dm * sa, dm * sb, dm * ua * sa * (1.0 - sa), dm * ub * sb * (1.0 - sb)]

    wide = ((t, d), BF16, (tm, tn), lambda i, j, k: (i, j), False)
    dua, dub, dga, dgb = _mm(
        "mix_dmerge", (t // tm, d // tn, 1),
        [(dx2_16, (tm, d), lambda i, j, k: (i, 0)),
         (w_out, (tn, d), lambda i, j, k: (j, 0)),
         (proj, (tm, tn), lambda i, j, k: (i, OFF_GA // tn + j)),
         (proj, (tm, tn), lambda i, j, k: (i, (OFF_GA + d) // tn + j)),
         (sv["ua"], (tm, tn), lambda i, j, k: (i, j)),
         (sv["ub"], (tm, tn), lambda i, j, k: (i, j))],
        [(0, 1, 0, NT)], [(tm, tn)], epi_gate, [wide] * 4)

    tk = _tile(t, 1024, 16)
    two = _tile(d, 1024, 128)
    (dw_out,) = _mm(
        "mix_dwout", (d // two, 1, t // tk),
        [(sv["merged"], (tk, two), lambda i, j, k: (k, i)), (dx2_16, (tk, d), lambda i, j, k: (k, 0))],
        [(0, 1, 0, TN)], [(two, d)], lambda accs, refs, rows: accs,
        [((d, d), BF16, (two, d), lambda i, j, k: (i, 0), False)])

    w_spec = lambda rows: ((N_DEV, rows, nb), BF16, (None, rows, nb), lambda i, j, k: (j, 0, 0), False)
    tk2 = _tile(t, 2048, 16)
    dwua, dwub = _mm(
        "mix_dwup", (1, N_DEV, t // tk2),
        [(sv["out_a"], (tk2, A_WIDTH), lambda i, j, k: (k, 0)),
         (dua, (tk2, nb), lambda i, j, k: (k, j)),
         (sv["out_b"], (tk2, B_WIDTH), lambda i, j, k: (k, 0)),
         (dub, (tk2, nb), lambda i, j, k: (k, j))],
        [(0, 1, 0, TN), (2, 3, 1, TN)], [(A_WIDTH, nb), (B_WIDTH, nb)], lambda accs, refs, rows: accs,
        [w_spec(A_WIDTH), w_spec(B_WIDTH)])

    d_out_a, d_out_b = _mm(
        "mix_dup", (t // tm, 1, N_DEV),
        [(dua, (tm, nb), lambda i, j, k: (i, k)),
         (wua, (None, A_WIDTH, nb), lambda i, j, k: (k, 0, 0)),
         (dub, (tm, nb), lambda i, j, k: (i, k)),
         (wub, (None, B_WIDTH, nb), lambda i, j, k: (k, 0, 0))],
        [(0, 1, 0, NT), (2, 3, 1, NT)], [(tm, A_WIDTH), (tm, B_WIDTH)], lambda accs, refs, rows: accs,
        [((t, A_WIDTH), BF16, (tm, A_WIDTH), lambda i, j, k: (i, 0), False),
         ((t, B_WIDTH), BF16, (tm, B_WIDTH), lambda i, j, k: (i, 0), False)])

    avail.update(w_out=dw_out.reshape(N_DEV, d // N_DEV, d), w_up_a=dwua, w_up_b=dwub)
    (dq_a, dk_a, dv_a, dsinks), sent["attention"] = _attention_bwd(
        proj, tables, sinks, d_out_a, _carried(comms, "attention", avail))
    (dq_b, df_b, di_b, dog_b, dlb, dwn), sent["hgrn"] = _hgrn_bwd(
        proj, hlb, wn, sv["o_raw"], sv["states"], d_out_b, _carried(comms, "hgrn", avail))
    dproj = jnp.concatenate([dq_a, dk_a, dv_a, dq_b, df_b, di_b, dog_b, dga, dgb], axis=1)

    tdm = _tile(d, 1024, 128)
    twn = _tile(n_in, 2432, 128)
    tkw = _tile(t, 512, 16)
    (dw_in,) = _mm(
        "mix_dwin", (d // tdm, n_in // twn, t // tkw),
        [(sv["h2"], (tkw, tdm), lambda i, j, k: (k, i)), (dproj, (tkw, twn), lambda i, j, k: (k, j))],
        [(0, 1, 0, TN)], [(tdm, twn)], lambda accs, refs, rows: accs,
        [((d, n_in), BF16, (tdm, twn), lambda i, j, k: (i, j), False)])

    avail.update(w_in=dw_in.reshape(d, N_DEV, n_in // N_DEV).transpose(1, 0, 2))
    tm3 = _tile(t, 512, 16)
    tpn = _tile(n_in, 512, 128)

    def epi_dx(accs, refs, rows):
        dx, dw = _rms_bwd(accs[0], refs[2][rows], refs[3][...], refs[4][rows])
        return [dx, dx, dw]

    (dx1, dx1_16, dmix), sent["dx"] = _mmc(
        "mix_dx", (t // tm3, 1, n_in // tpn),
        [(dproj, (tm3, tpn), lambda i, j, k: (i, k)), (w_in, (d, tpn), lambda i, j, k: (0, k)),
         (x1, (tm3, d), lambda i, j, k: (i, 0)), (mix_norm, (1, d), lambda i, j, k: (0, 0)),
         (dx2, (tm3, d), lambda i, j, k: (i, 0))],
        [(0, 1, 0, NT)], [(tm3, d)], epi_dx,
        [((t, d), F32, (tm3, d), lambda i, j, k: (i, 0), False),
         ((t, d), BF16, (tm3, d), lambda i, j, k: (i, 0), False),
         ((1, d), F32, (1, d), lambda i, j, k: (0, 0), True)],
        comm=_carried(comms, "dx", avail))
    small = dict(mix_norm=dmix, attn_sinks=dsinks, hgrn_lb=dlb, hgrn_norm=dwn)
    return dx1, dx1_16, small, avail


def _ple_fwd(x3, ple_norm, p16, wpg, wpp):
    t, d = x3.shape
    nb = d // N_DEV
    h4 = _rmsnorm("ple_norm", x3, ple_norm)
    tm = _tile(t, 1024, 16)

    def epi(accs, refs, rows):
        zg, pp = accs
        return [refs[4][rows] + _sig(zg) * pp, zg, pp]

    blk = lambda dt: ((t, d), dt, (tm, nb), lambda i, j, k: (i, j), False)
    x4, zg, pp = _mm(
        "ple_fwd", (t // tm, N_DEV, 1),
        [(h4, (tm, d), lambda i, j, k: (i, 0)), (wpg, (d, nb), lambda i, j, k: (0, j)),
         (p16, (tm, PLE_DIM), lambda i, j, k: (i, 0)),
         (wpp, (None, PLE_DIM, nb), lambda i, j, k: (j, 0, 0)),
         (x3, (tm, nb), lambda i, j, k: (i, j))],
        [(0, 1, 0, NN), (2, 3, 1, NN)], [(tm, nb)] * 2, epi, [blk(F32), blk(BF16), blk(BF16)])
    return x4, dict(h4=h4, zg=zg, pp=pp)


def _ple_bwd(dx4, x3, ple_norm, p16, wpg, sv):
    t, d = x3.shape
    nb = d // N_DEV

    def gate_grads(dx, zg, pp):
        s = _sig(zg.astype(F32))
        return dx * pp.astype(F32) * s * (1.0 - s), dx * s

    dzg, dpp = _rows("ple_dgate", gate_grads, [(dx4, True), (sv["zg"], True), (sv["pp"], True)],
                     [((t, d), BF16), ((t, d), BF16)], _tile(t, 512, 16))
    (dwpp,) = _mm(
        "ple_dwproj", (1, N_DEV, 1),
        [(p16, (t, PLE_DIM), lambda i, j, k: (0, 0)), (dpp, (t, nb), lambda i, j, k: (0, j))],
        [(0, 1, 0, TN)], [(PLE_DIM, nb)], lambda accs, refs, rows: accs,
        [((N_DEV, PLE_DIM, nb), BF16, (None, PLE_DIM, nb), lambda i, j, k: (j, 0, 0), False)])
    tk = _tile(t, 1024, 16)
    tn = _tile(d, 1024, 128)
    (dwpg,) = _mm(
        "ple_dwgate", (d // tn, 1, t // tk),
        [(sv["h4"], (tk, tn), lambda i, j, k: (k, i)), (dzg, (tk, d), lambda i, j, k: (k, 0))],
        [(0, 1, 0, TN)], [(tn, d)], lambda accs, refs, rows: accs,
        [((d, d), BF16, (tn, d), lambda i, j, k: (i, 0), False)])
    tm3 = _tile(t, 256, 16)

    def epi_dx(accs, refs, rows):
        dx, dw = _rms_bwd(accs[0], refs[2][rows], refs[3][...], refs[4][rows])
        return [dx, dx, dw]

    dx3, dx3_16, dnorm = _mm(
        "ple_dx", (t // tm3, 1, 1),
        [(dzg, (tm3, d), lambda i, j, k: (i, 0)), (wpg, (d, d), lambda i, j, k: (0, 0)),
         (x3, (tm3, d), lambda i, j, k: (i, 0)), (ple_norm, (1, d), lambda i, j, k: (0, 0)),
         (dx4, (tm3, d), lambda i, j, k: (i, 0))],
        [(0, 1, 0, NT)], [(tm3, d)], epi_dx,
        [((t, d), F32, (tm3, d), lambda i, j, k: (i, 0), False),
         ((t, d), BF16, (tm3, d), lambda i, j, k: (i, 0), False),
         ((1, d), F32, (1, d), lambda i, j, k: (0, 0), True)])
    return dx3, dx3_16, dwpp, dwpg, dnorm


def _loss_head(x4, final_norm, target):
    t, d = x4.shape
    tm = _tile(t, 256, 8)

    def body(x_ref, w_ref, tgt_ref, dx_ref, dw_ref, loss_ref):
        i = pl.program_id(0)
        xv = x_ref[...]
        wv = w_ref[...]
        r = lax.rsqrt(jnp.mean(xv * xv, axis=-1, keepdims=True) + EPS)
        err = xv * r * wv - tgt_ref[...]
        part = 0.5 * jnp.sum(jnp.mean(err * err, axis=-1, keepdims=True), axis=0, keepdims=True)
        dx, dw = _rms_bwd(err * (1.0 / d), xv, wv, jnp.zeros_like(xv))
        dx_ref[...] = dx

        @pl.when(i == 0)
        def _():
            dw_ref[...] = dw
            loss_ref[...] = jnp.broadcast_to(part, loss_ref.shape)

        @pl.when(i > 0)
        def _():
            dw_ref[...] += dw
            loss_ref[...] += jnp.broadcast_to(part, loss_ref.shape)

    return pl.pallas_call(
        body,
        name="loss_head",
        grid=(t // tm,),
        in_specs=[pl.BlockSpec((tm, d), lambda i: (i, 0)), pl.BlockSpec((1, d), lambda i: (0, 0)),
                  pl.BlockSpec((tm, d), lambda i: (i, 0))],
        out_specs=[pl.BlockSpec((tm, d), lambda i: (i, 0)), pl.BlockSpec((1, d), lambda i: (0, 0)),
                   pl.BlockSpec((1, LANES), lambda i: (0, 0))],
        out_shape=[jax.ShapeDtypeStruct((t, d), F32), jax.ShapeDtypeStruct((1, d), F32),
                   jax.ShapeDtypeStruct((1, LANES), F32)],
        compiler_params=_params(("arbitrary",)),
    )(x4, final_norm, target)


RELATIONS = ((0, 0), (1, 0), (0, 1), (1, 1))


def _place():
    return lax.axis_index("x"), lax.axis_index("y"), lax.axis_index("c")


def _flip(v, bit):
    return 1 - v if bit else v


def _gather_program(shards):
    n = len(shards)

    def copies(ins, outs, send_sems, recv_sems, local_sems, base):
        x, y, c = _place()
        sibling = (x, y, 1 - c)

        def block(a, rel, core):
            return outs[a].at[4 * _flip(x, rel[0]) + 2 * _flip(y, rel[1]) + core]

        def copy(a, slot, rel, core, to, src=None):
            dst = block(a, rel, core)
            return pltpu.make_async_remote_copy(
                src_ref=dst if src is None else src, dst_ref=dst,
                send_sem=send_sems.at[base[0] + 7 * a + slot], recv_sem=recv_sems.at[base[1] + 7 * a + slot],
                device_id=to, device_id_type=MESH)

        own, first = [], []
        for a in range(n):
            own.append(pltpu.make_async_copy(ins[a], block(a, RELATIONS[0], c), local_sems.at[base[2] + a]))
            first.append(copy(a, 0, RELATIONS[0], c, sibling, src=ins[a]))
            for r in (1, 2, 3):
                rel = RELATIONS[r]
                first.append(copy(a, r, RELATIONS[0], c, (_flip(x, rel[0]), _flip(y, rel[1]), c), src=ins[a]))
        return own, first, copy, sibling, c

    def pre(*args):
        own, first, _, _, _ = copies(*args)
        for cp in own + first:
            cp.start()

    def post(*args):
        own, first, copy, sibling, c = copies(*args)
        passed = []
        for r in (1, 2, 3):
            for a in range(n):
                copy(a, r, RELATIONS[r], c, sibling).wait_recv()
                fwd = copy(a, 3 + r, RELATIONS[r], c, sibling)
                fwd.start()
                passed.append(fwd)
        for a in range(n):
            copy(a, 0, RELATIONS[0], 1 - c, sibling).wait_recv()
            for r in (1, 2, 3):
                copy(a, 3 + r, RELATIONS[r], 1 - c, sibling).wait_recv()
        for cp in first + passed:
            cp.wait_send()
        for mine in own:
            mine.wait()

    return _Comm(list(shards), [jax.ShapeDtypeStruct((N_DEV,) + s.shape, s.dtype) for s in shards],
                 (7 * n, 7 * n, n), pre, post)


def _exchange_program(arrays, n_slots, source, target, slot):
    n = len(arrays)

    def copies(ins, outs, send_sems, recv_sems, local_sems, base):
        place = _place()
        res = []
        for a in range(n):
            for r in range(n_slots):
                src = source(r, place)
                if src is None:
                    continue
                res.append(pltpu.make_async_remote_copy(
                    src_ref=ins[a].at[src], dst_ref=outs[a].at[slot(r)],
                    send_sem=send_sems.at[base[0] + n_slots * a + r],
                    recv_sem=recv_sems.at[base[1] + n_slots * a + r],
                    device_id=target(r, place), device_id_type=MESH))
        return res

    def pre(*args):
        for cp in copies(*args):
            cp.start()

    def post(*args):
        for cp in copies(*args):
            cp.wait()

    n_out = len({slot(r) for r in range(n_slots) if source(r, (0, 0, 0)) is not None})
    return _Comm(list(arrays), [jax.ShapeDtypeStruct((n_out,) + g.shape[1:], g.dtype) for g in arrays],
                 (n_slots * n, n_slots * n, 0), pre, post)


def _sibling_program(grads):
    def source(r, place):
        x, y, c = place
        return 4 * _flip(x, RELATIONS[r][0]) + 2 * _flip(y, RELATIONS[r][1]) + (1 - c)

    return _exchange_program(grads, 4, source, lambda r, p: (p[0], p[1], 1 - p[2]), lambda r: r)


def _chips_program(sums):
    def target(r, place):
        x, y, c = place
        return (_flip(x, RELATIONS[r][0]), _flip(y, RELATIONS[r][1]), c)

    return _exchange_program(sums, 4, lambda r, p: r if r else None, target, lambda r: r - 1)


def _standalone(name, comm):
    return _call(name, lambda: None, (1,), [], [], [], [], [], comm)[1]


def _pair_sum(name, grad, from_sibling, block_ids):
    _, rows, cols = grad.shape
    tr = _tile(rows, 512, 16)

    def body(ids_ref, mine_ref, sib_ref, o_ref):
        o_ref[...] = (mine_ref[...].astype(F32) + sib_ref[...].astype(F32)).astype(o_ref.dtype)

    return pl.pallas_call(
        body,
        name=name,
        grid_spec=pltpu.PrefetchScalarGridSpec(
            num_scalar_prefetch=1,
            grid=(4, rows // tr),
            in_specs=[pl.BlockSpec((None, tr, cols), lambda r, i, ids: (ids[r], i, 0)),
                      pl.BlockSpec((None, tr, cols), lambda r, i, ids: (r, i, 0))],
            out_specs=pl.BlockSpec((None, tr, cols), lambda r, i, ids: (r, i, 0))),
        out_shape=jax.ShapeDtypeStruct((4, rows, cols), BF16),
        compiler_params=_params(("arbitrary", "arbitrary")),
    )(block_ids, grad, from_sibling)


def _adam(w, g, m, v):
    m = ADAM_B1 * m + (1.0 - ADAM_B1) * g
    v = ADAM_B2 * v + (1.0 - ADAM_B2) * (g * g)
    m_hat = m / (1.0 - ADAM_B1 ** ADAM_STEP)
    v_hat = v / (1.0 - ADAM_B2 ** ADAM_STEP)
    delta = -ADAM_LR * (m_hat / (jnp.sqrt(v_hat) + ADAM_EPS) + ADAM_WD * w)
    return delta, m, v


def _adam_sharded(name, items, comm=None):
    _, rows, cols = items[0][0].shape
    n = len(items)
    tr = _tile(rows, max(64, 256 // n), 8)

    def body(*refs):
        for k in range(n):
            w_ref, m_ref, v_ref, s_ref, r1_ref, r2_ref, r3_ref = refs[7 * k:7 * k + 7]
            g_ref, d_ref, nm_ref, nv_ref = refs[7 * n + 4 * k:7 * n + 4 * k + 4]
            g = ((s_ref[...].astype(F32) + r1_ref[...].astype(F32)) + r2_ref[...].astype(F32)) + r3_ref[...].astype(F32)
            delta, nm, nv = _adam(w_ref[...], g, m_ref[...], v_ref[...])
            g_ref[...] = g
            d_ref[...] = delta
            nm_ref[...] = nm
            nv_ref[...] = nv

    slot = lambda s: pl.BlockSpec((None, tr, cols), lambda i: (s, i, 0))
    flat = slot(0)
    args = []
    for w, m, v, sums, from_chips in items:
        args += [w, m, v, sums, from_chips, from_chips, from_chips]
    res, sent = _call(
        name, body, (rows // tr,),
        [flat, flat, flat, slot(0), slot(0), slot(1), slot(2)] * n, [flat] * (4 * n),
        [jax.ShapeDtypeStruct((1, rows, cols), F32)] * (4 * n), [], args, comm)
    return [res[4 * k:4 * k + 4] for k in range(n)], sent


def _all_reduce_small(part):
    rows = part.shape[0]

    def body(x_ref, o_ref, gathered, send_sems, recv_sems):
        x, y, c = _place()
        me = 4 * x + 2 * y + c
        gathered[me] = x_ref[...]
        copies = []
        for k in range(1, N_DEV):
            bits = (k >> 2 & 1, k >> 1 & 1, k & 1)
            peer = (_flip(x, bits[0]), _flip(y, bits[1]), _flip(c, bits[2]))
            copies.append(pltpu.make_async_remote_copy(
                src_ref=x_ref, dst_ref=gathered.at[me],
                send_sem=send_sems.at[k - 1], recv_sem=recv_sems.at[k - 1],
                device_id=peer, device_id_type=MESH))
        for cp in copies:
            cp.start()
        for cp in copies:
            cp.wait()
        total = gathered[0]
        for b in range(1, N_DEV):
            total = total + gathered[b]
        o_ref[...] = total

    return pl.pallas_call(
        body,
        name="small_all_reduce",
        in_specs=[pl.BlockSpec(memory_space=pltpu.VMEM)],
        out_specs=pl.BlockSpec(memory_space=pltpu.VMEM),
        out_shape=jax.ShapeDtypeStruct((rows, LANES), F32),
        scratch_shapes=[pltpu.VMEM((N_DEV, rows, LANES), F32),
                        pltpu.SemaphoreType.DMA((N_DEV - 1,)), pltpu.SemaphoreType.DMA((N_DEV - 1,))],
    )(part)


def _adam_small(w, g, m, v):
    def fn(wv, gv, mv, vv):
        return _adam(wv, gv, mv, vv)

    shape = (w.shape, F32)
    return _rows("adam_small", fn, [(w, True), (g, True), (m, True), (v, True)], [shape] * 3, w.shape[0])


SMALL = ("ffn1_norm", "mix_norm", "attn_sinks", "hgrn_lower_bound", "hgrn_norm", "ffn2_norm", "ple_norm",
         "final_norm")
LARGE = ("ffn1_w_gate", "ffn1_w_up", "ffn1_w_down", "w_in", "w_up_a", "w_up_b", "w_out",
         "ffn2_w_gate", "ffn2_w_up", "ffn2_w_down", "ple_w_gate", "ple_w_proj")
WEIGHTS = ("ffn1_norm", "ffn1_w_gate", "ffn1_w_up", "ffn1_w_down", "mix_norm", "w_in", "attn_sinks",
           "hgrn_lower_bound", "hgrn_norm", "w_up_a", "w_up_b", "w_out", "ffn2_norm", "ffn2_w_gate",
           "ffn2_w_up", "ffn2_w_down", "ple_norm", "ple_w_gate", "ple_w_proj", "final_norm")


def _pack_rows(arrays):
    rows = []
    for a in arrays:
        flat = a.reshape(-1).astype(F32)
        pad = -flat.shape[0] % LANES
        rows.append(jnp.pad(flat, (0, pad)).reshape(-1, LANES))
    packed = jnp.concatenate(rows, axis=0)
    return jnp.pad(packed, ((0, -packed.shape[0] % 8), (0, 0)))


def _unpack_rows(packed, like):
    out, at = [], 0
    for a in like:
        size = a.size
        n_rows = -(-size // LANES)
        out.append(packed[at:at + n_rows].reshape(-1)[:size].reshape(a.shape))
        at += n_rows
    return out


def kernel(x, p, positions, ffn1_norm, ffn1_w_gate, ffn1_w_up, ffn1_w_down, mix_norm, w_in, attn_sinks, hgrn_lower_bound, hgrn_norm, w_up_a, w_up_b, w_out, ffn2_norm, ffn2_w_gate, ffn2_w_up, ffn2_w_down, ple_norm, ple_w_gate, ple_w_proj, final_norm, loss_target, m_ffn1_norm, m_ffn1_w_gate, m_ffn1_w_up, m_ffn1_w_down, m_mix_norm, m_w_in, m_attn_sinks, m_hgrn_lower_bound, m_hgrn_norm, m_w_up_a, m_w_up_b, m_w_out, m_ffn2_norm, m_ffn2_w_gate, m_ffn2_w_up, m_ffn2_w_down, m_ple_norm, m_ple_w_gate, m_ple_w_proj, m_final_norm, v_ffn1_norm, v_ffn1_w_gate, v_ffn1_w_up, v_ffn1_w_down, v_mix_norm, v_w_in, v_attn_sinks, v_hgrn_lower_bound, v_hgrn_norm, v_w_up_a, v_w_up_b, v_w_out, v_ffn2_norm, v_ffn2_w_gate, v_ffn2_w_up, v_ffn2_w_down, v_ple_norm, v_ple_w_gate, v_ple_w_proj, v_final_norm):
    given = dict(locals())
    w = {n: given[n] for n in WEIGHTS}
    mom = {n: given["m_" + n] for n in WEIGHTS}
    var = {n: given["v_" + n] for n in WEIGHTS}
    t, d = x.shape[1], x.shape[2]

    x0, p16, target = x[0], p[0, 0].astype(BF16), loss_target[0]
    tables = _rope_tables(positions)
    sinks = attn_sinks.reshape(-1)
    final_w = final_norm.reshape(1, d)
    shard = {n: w[n][0].astype(BF16) for n in LARGE}
    shard["w_in_top"], shard["w_in_bottom"] = shard["w_in"][:d // 2], shard["w_in"][d // 2:]
    xi, yi, ci = _place()
    block_ids = jnp.stack([4 * _flip(xi, rx) + 2 * _flip(yi, ry) + ci for rx, ry in RELATIONS]).astype(jnp.int32)

    def gather(*names):
        return _gather_program([shard[n] for n in names])

    wg1, wu1 = _standalone("gather_ffn1", gather("ffn1_w_gate", "ffn1_w_up"))
    h1 = _rmsnorm("ffn1_norm", x0, ffn1_norm)
    (g1, u1, a1), (wd1, win_top) = _ffn_up("ffn1", h1, wg1, wu1, gather("ffn1_w_down", "w_in_top"))
    x1, (win_bottom,) = _ffn_down("ffn1", x0, a1, wd1, gather("w_in_bottom"))
    win = jnp.concatenate([win_top.transpose(1, 0, 2).reshape(d // 2, -1),
                           win_bottom.transpose(1, 0, 2).reshape(d // 2, -1)], axis=0)
    mix_args = (mix_norm, win, sinks, hgrn_lower_bound, hgrn_norm)

    def mixer_weights(sent):
        wua, wub, wout_g = sent["proj"]
        return wua, wub, wout_g.reshape(d, d)

    x2, mix_saved, got_w = _mixer_fwd(
        x1, *mix_args, mixer_weights, tables,
        dict(proj=gather("w_up_a", "w_up_b", "w_out"), attention=gather("ffn2_w_up"),
             hgrn=gather("ffn2_w_gate"), merge=gather("ple_w_gate", "ple_w_proj")))
    wua, wub, wout = mixer_weights(got_w)
    (wu2,), (wg2,) = got_w["attention"], got_w["hgrn"]
    wpg_g, wpp = got_w["merge"]
    wpg = wpg_g.reshape(d, d)
    h3 = _rmsnorm("ffn2_norm", x2, ffn2_norm)
    (g2, u2, a2), (wd2,) = _ffn_up("ffn2", h3, wg2, wu2, gather("ffn2_w_down"))
    x3, _ = _ffn_down("ffn2", x2, a2, wd2)
    x4, ple_saved = _ple_fwd(x3, ple_norm, p16, wpg, wpp)
    dx4, d_final, loss_row = _loss_head(x4, final_w, target)

    sums, from_chips = {}, {}

    def to_chips(names, grads, from_sibling):
        for n, g, s in zip(names, grads, from_sibling):
            sums[n] = _pair_sum("pair_sum_" + n, g, s, block_ids)
        return _chips_program([sums[n] for n in names])

    def arrived(names, results):
        from_chips.update(zip(names, results))

    dx3, dx3_16, dwpp, dwpg, d_ple = _ple_bwd(dx4, x3, ple_norm, p16, wpg, ple_saved)
    ple_names = ("ple_w_proj", "ple_w_gate")
    ple_grads = [dwpp, dwpg.reshape(N_DEV, d // N_DEV, d)]
    gu2 = ("ffn2_w_gate", "ffn2_w_up")
    dx2, dx2_16, dwg2, dwu2, dwd2, d_ffn2, sent2 = _ffn_bwd(
        "ffn2b", dx3, dx3_16, x2, h3, g2, u2, a2, wg2, wu2, wd2, ffn2_norm,
        dict(dact=_sibling_program(ple_grads),
             dwgu=lambda av: to_chips(ple_names, ple_grads, av["sent"]["dact"]),
             dwd=lambda av: _sibling_program([av["dwg"], av["dwu"]]),
             dx=lambda av: _merge([to_chips(gu2, [av["dwg"], av["dwu"]], av["sent"]["dwd"]),
                                   _sibling_program([av["dwd"]])])))
    arrived(ple_names, sent2["dwgu"])
    arrived(gu2, sent2["dx"][:2])
    mix3 = ("w_out", "w_up_a", "w_up_b")
    dx1, dx1_16, mix_small, mix_made = _mixer_bwd(
        dx2, dx2_16, x1, *mix_args, wua, wub, wout, tables, mix_saved,
        dict(attention=lambda av: _merge([to_chips(("ffn2_w_down",), [dwd2], sent2["dx"][2:]),
                                          _sibling_program([av[n] for n in mix3])]),
             hgrn=lambda av: to_chips(mix3, [av[n] for n in mix3], av["sent"]["attention"][1:]),
             dx=lambda av: _sibling_program([av["w_in"]])))
    sent_mix = mix_made["sent"]
    arrived(("ffn2_w_down",), sent_mix["attention"][:1])
    arrived(mix3, sent_mix["hgrn"])
    gu1 = ("ffn1_w_gate", "ffn1_w_up")
    grad_x, _, dwg1, dwu1, dwd1, d_ffn1, sent1 = _ffn_bwd(
        "ffn1b", dx1, dx1_16, x0, h1, g1, u1, a1, wg1, wu1, wd1, ffn1_norm,
        dict(dwgu=to_chips(("w_in",), [mix_made["w_in"]], sent_mix["dx"]),
             dwd=lambda av: _sibling_program([av["dwg"], av["dwu"]]),
             dx=lambda av: _merge([to_chips(gu1, [av["dwg"], av["dwu"]], av["sent"]["dwd"]),
                                   _sibling_program([av["dwd"]])])))
    arrived(("w_in",), sent1["dwgu"])
    arrived(gu1, sent1["dx"][:2])
    arrived(("ffn1_w_down",), _standalone("grads_tail", to_chips(("ffn1_w_down",), [dwd1], sent1["dx"][2:])))

    out = {}
    for n in LARGE:
        (out[n],), _ = _adam_sharded("adam_" + n, [(w[n], mom[n], var[n], sums[n], from_chips[n])])

    dlb = mix_small["hgrn_lb"]
    small_grads = dict(
        ffn1_norm=d_ffn1, mix_norm=mix_small["mix_norm"], attn_sinks=mix_small["attn_sinks"][:, :A_HEADS],
        hgrn_lower_bound=jnp.concatenate([dlb, -dlb], axis=0), hgrn_norm=mix_small["hgrn_norm"],
        ffn2_norm=d_ffn2, ple_norm=d_ple, final_norm=d_final)

    small_like = [w[n] for n in SMALL]
    packed = _pack_rows([small_grads[n].reshape(w[n].shape) for n in SMALL] + [loss_row[:, :1]])
    total = _all_reduce_small(packed)
    n_rows = total.shape[0]
    pack_w = _pack_rows(small_like + [jnp.zeros((1, 1), F32)])
    pack_m = _pack_rows([mom[n] for n in SMALL] + [jnp.zeros((1, 1), F32)])
    pack_v = _pack_rows([var[n] for n in SMALL] + [jnp.zeros((1, 1), F32)])
    deltas = _adam_small(pack_w, total, pack_m, pack_v)
    like = small_like + [jnp.zeros((1, 1), F32)]
    g_small = _unpack_rows(total, like)
    d_small, m_small, v_small = (_unpack_rows(a, like) for a in deltas)
    for k, n in enumerate(SMALL):
        out[n] = [g_small[k], d_small[k], m_small[k], v_small[k]]
    loss = g_small[-1].reshape(())

    return (loss, grad_x.reshape(x.shape),
            *[out[n][0] for n in WEIGHTS], *[out[n][1] for n in WEIGHTS],
            *[out[n][2] for n in WEIGHTS], *[out[n][3] for n in WEIGHTS])
```

```python
import functools

import jax
import jax.numpy as jnp
from jax import lax
from jax.experimental import pallas as pl
from jax.experimental.pallas import tpu as pltpu

F32 = jnp.float32
BF16 = jnp.bfloat16
MESH = pl.DeviceIdType.MESH

EPS = 1e-6
N_DEV = 8
A_HEADS = 16
A_HEAD_DIM = 64
A_WIDTH = 1024
A_KV_WIDTH = 256
ATT_BLOCK = 128
ROT_DIM = 16
ROPE_THETA = 500000.0
B_WIDTH = 1024
B_HEAD = 128
CHUNK = 64
SUB = 16
MAX_DECAY_EXP = 60.0
PLE_DIM = 256
LANES = 128

ADAM_LR = 0.001
ADAM_B1 = 0.9
ADAM_B2 = 0.999
ADAM_EPS = 1e-08
ADAM_WD = 0.01
ADAM_STEP = 10

VMEM_LIMIT = 56 * 1024 * 1024
EPILOGUE_ELEMS = 256 * 1024

NN = (((1,), (0,)), ((), ()))
NT = (((1,), (1,)), ((), ()))
TN = (((0,), (0,)), ((), ()))

OFF_QA = 0
OFF_KA = 1024
OFF_VA = 1280
OFF_QB = 1536
OFF_FB = 2560
OFF_IB = 3584
OFF_OG = 4608
OFF_GA = 5632


def _sig(v):
    return 1.0 / (1.0 + jnp.exp(-v))


def _tile(n, pref, mult):
    t = min(n, pref)
    t -= t % mult
    while n % t:
        t -= mult
    return t


def _params(sem):
    return pltpu.CompilerParams(dimension_semantics=sem, vmem_limit_bytes=VMEM_LIMIT)


HBM_SPEC = pl.BlockSpec(memory_space=pltpu.HBM)


class _Comm:
    def __init__(self, ins, out_shapes, n_sems, pre, post):
        self.ins, self.out_shapes, self.n_sems, self.pre, self.post = ins, out_shapes, n_sems, pre, post
        self.parts = [len(out_shapes)]

    def split(self, outs):
        res, at = [], 0
        for n in self.parts:
            res.append(list(outs[at:at + n]))
            at += n
        return res


def _merge(progs):
    spans, ins, shapes, sems = [], [], [], [0, 0, 0]
    for p in progs:
        spans.append((len(ins), len(shapes), tuple(sems)))
        ins += list(p.ins)
        shapes += list(p.out_shapes)
        sems = [a + b for a, b in zip(sems, p.n_sems)]

    def run(which):
        def go(cin, cout, send, recv, local, base):
            for p, (i0, o0, s0) in zip(progs, spans):
                getattr(p, which)(cin[i0:i0 + len(p.ins)], cout[o0:o0 + len(p.out_shapes)], send, recv, local,
                                  tuple(b + s for b, s in zip(base, s0)))
        return go

    merged = _Comm(ins, shapes, tuple(sems), run("pre"), run("post"))
    merged.parts = [len(p.out_shapes) for p in progs]
    return merged


def _call(name, body, grid, in_specs, out_specs, out_shape, scratch, args, comm=None):
    sem = ("arbitrary",) * len(grid)
    if comm is None:
        res = pl.pallas_call(body, name=name, grid=grid, in_specs=in_specs, out_specs=out_specs,
                             out_shape=out_shape, scratch_shapes=scratch, compiler_params=_params(sem))(*args)
        return list(res), []
    n_in, n_out, n_scr = len(in_specs), len(out_specs), len(scratch)
    c_in, c_out = len(comm.ins), len(comm.out_shapes)

    def carrier(*refs):
        at = [0]

        def take(n):
            at[0] += n
            return refs[at[0] - n:at[0]]

        ins, cins, outs, couts, scr, sems = take(n_in), take(c_in), take(n_out), take(c_out), take(n_scr), take(3)
        ids = [pl.program_id(a) for a in range(len(grid))]
        first = functools.reduce(jnp.logical_and, [i == 0 for i in ids])
        last = functools.reduce(jnp.logical_and, [i == g - 1 for i, g in zip(ids, grid)])

        @pl.when(first)
        def _():
            comm.pre(cins, couts, *sems, (0, 0, 0))

        body(*ins, *outs, *scr)

        @pl.when(last)
        def _():
            comm.post(cins, couts, *sems, (0, 0, 0))

    res = pl.pallas_call(
        carrier, name=name, grid=grid,
        in_specs=list(in_specs) + [HBM_SPEC] * c_in,
        out_specs=list(out_specs) + [HBM_SPEC] * c_out,
        out_shape=list(out_shape) + list(comm.out_shapes),
        scratch_shapes=list(scratch) + [pltpu.SemaphoreType.DMA((max(n, 1),)) for n in comm.n_sems],
        compiler_params=_params(sem),
    )(*args, *comm.ins)
    return list(res[:n_out]), list(res[n_out:])


def _mm(name, grid, ins, prods, acc_shapes, epi, outs, comm=None):
    n_in, n_out, nk = len(ins), len(outs), grid[2]
    tile_rows, tile_cols = acc_shapes[0]
    chunk = tile_rows
    while chunk * tile_cols > EPILOGUE_ELEMS and chunk % 32 == 0:
        chunk //= 2

    def body(*refs):
        in_refs = refs[:n_in]
        out_refs = refs[n_in:n_in + n_out]
        acc_refs = refs[n_in + n_out:]
        first_tile = jnp.logical_and(pl.program_id(0) == 0, pl.program_id(1) == 0)

        def product(lhs, rhs, dims):
            a = in_refs[lhs][...].astype(BF16)
            b = in_refs[rhs][...].astype(BF16)
            return lax.dot_general(a, b, dims, preferred_element_type=F32)

        def write(accs):
            totals = [None] * n_out
            for c in range(tile_rows // chunk):
                rows = slice(c * chunk, (c + 1) * chunk)
                vals = epi([a[rows] for a in accs], in_refs, rows)
                for idx, (o, v, spec) in enumerate(zip(out_refs, vals, outs)):
                    if spec[4]:
                        totals[idx] = v if totals[idx] is None else totals[idx] + v
                    else:
                        o[rows] = v.astype(o.dtype)
            for o, v, spec in zip(out_refs, totals, outs):
                if spec[4]:
                    @pl.when(first_tile)
                    def _():
                        o[...] = v.astype(o.dtype)

                    @pl.when(jnp.logical_not(first_tile))
                    def _():
                        o[...] += v.astype(o.dtype)

        if nk == 1:
            accs = [None] * len(acc_shapes)
            for lhs, rhs, acc, dims in prods:
                t = product(lhs, rhs, dims)
                accs[acc] = t if accs[acc] is None else accs[acc] + t
            write(accs)
        else:
            k = pl.program_id(2)

            @pl.when(k == 0)
            def _():
                for r in acc_refs:
                    r[...] = jnp.zeros(r.shape, F32)

            for lhs, rhs, acc, dims in prods:
                acc_refs[acc][...] += product(lhs, rhs, dims)

            @pl.when(k == nk - 1)
            def _():
                write(acc_refs)

    res, sent = _call(
        name, body, grid,
        [pl.BlockSpec(b, im) for _, b, im in ins],
        [pl.BlockSpec(o[2], o[3]) for o in outs],
        [jax.ShapeDtypeStruct(o[0], o[1]) for o in outs],
        [] if nk == 1 else [pltpu.VMEM(s, F32) for s in acc_shapes],
        [a for a, _, _ in ins], comm)
    return res if comm is None else (res, sent)


def _rms_bwd(dh, xv, w, dres):
    r = lax.rsqrt(jnp.mean(xv * xv, axis=-1, keepdims=True) + EPS)
    g = dh * w
    dx = dres + r * (g - xv * (r * r) * jnp.mean(g * xv, axis=-1, keepdims=True))
    dw = jnp.sum(dh * xv * r, axis=0, keepdims=True)
    return dx, dw


def _rows(name, fn, ins, outs, tm):
    n_in = len(ins)
    rows = outs[0][0][0]

    def body(*refs):
        vals = fn(*[r[...] for r in refs[:n_in]])
        for o, v in zip(refs[n_in:], vals):
            o[...] = v.astype(o.dtype)

    in_specs = []
    for a, tiled in ins:
        if tiled:
            in_specs.append(pl.BlockSpec((tm, a.shape[1]), lambda i: (i, 0)))
        else:
            in_specs.append(pl.BlockSpec(a.shape, lambda i, nd=a.ndim: (0,) * nd))
    return pl.pallas_call(
        body,
        name=name,
        grid=(rows // tm,),
        in_specs=in_specs,
        out_specs=[pl.BlockSpec((tm, s[1]), lambda i: (i, 0)) for s, _ in outs],
        out_shape=[jax.ShapeDtypeStruct(s, d) for s, d in outs],
        compiler_params=_params(("arbitrary",)),
    )(*[a for a, _ in ins])


def _rmsnorm(name, x, w):
    def fn(xv, wv):
        r = lax.rsqrt(jnp.mean(xv * xv, axis=-1, keepdims=True) + EPS)
        return (xv * r * wv,)

    return _rows(name, fn, [(x, True), (w, False)], [(x.shape, BF16)], _tile(x.shape[0], 512, 16))[0]


def _mmc(*args, comm=None):
    if comm is None:
        return _mm(*args), []
    return _mm(*args, comm=comm)


def _carried(comms, key, avail):
    c = comms.get(key) if comms else None
    return c(avail) if callable(c) else c


def _ffn_up(name, h, wg, wu, comm=None):
    t, d = h.shape
    n = wg.shape[1]
    tm = _tile(t, 1024, 16)

    def epi_up(accs, refs, rows):
        g, u = accs
        return [g, u, g * _sig(g) * u]

    hid = ((N_DEV, t, n), BF16, (None, tm, n), lambda i, j, k: (j, i, 0), False)
    return _mmc(
        name + "_up", (t // tm, N_DEV, 1),
        [(h, (tm, d), lambda i, j, k: (i, 0)),
         (wg, (None, n, d), lambda i, j, k: (j, 0, 0)),
         (wu, (None, n, d), lambda i, j, k: (j, 0, 0))],
        [(0, 1, 0, NT), (0, 2, 1, NT)], [(tm, n)] * 2, epi_up, [hid] * 3, comm=comm)


def _ffn_down(name, x, act, wd, comm=None):
    t, d = x.shape
    n = wd.shape[1]
    tm2 = _tile(t, 1024, 8)
    tn = _tile(d, 1024, 128)

    def epi_down(accs, refs, rows):
        return [refs[2][rows] + 0.5 * accs[0]]

    (x_out,), sent = _mmc(
        name + "_down", (t // tm2, d // tn, N_DEV),
        [(act, (None, tm2, n), lambda i, j, k: (k, i, 0)),
         (wd, (None, n, tn), lambda i, j, k: (k, 0, j)),
         (x, (tm2, tn), lambda i, j, k: (i, j))],
        [(0, 1, 0, NN)], [(tm2, tn)], epi_down,
        [((t, d), F32, (tm2, tn), lambda i, j, k: (i, j), False)], comm=comm)
    return x_out, sent


def _ffn_bwd(name, dxo, dxo16, x, h, gate, up, act, wg, wu, wd, norm_w, comms=None):
    t, d = x.shape
    n = wg.shape[1]
    tm = _tile(t, 1024, 16)
    sent = {}
    avail = dict(sent=sent)

    def epi_act(accs, refs, rows):
        da = 0.5 * accs[0]
        g = refs[2][rows].astype(F32)
        u = refs[3][rows].astype(F32)
        s = _sig(g)
        return [da * u * (s * (1.0 + g * (1.0 - s))), da * g * s]

    hid_in = lambda a: (a, (None, tm, n), lambda i, j, k: (j, i, 0))
    hid_out = ((N_DEV, t, n), BF16, (None, tm, n), lambda i, j, k: (j, i, 0), False)
    (dgate, dup), sent["dact"] = _mmc(
        name + "_dact", (t // tm, N_DEV, 1),
        [(dxo16, (tm, d), lambda i, j, k: (i, 0)),
         (wd, (None, n, d), lambda i, j, k: (j, 0, 0)),
         hid_in(gate), hid_in(up)],
        [(0, 1, 0, NT)], [(tm, n)], epi_act, [hid_out] * 2, comm=_carried(comms, "dact", avail))

    tk = _tile(t, 1024, 16)
    tk2 = _tile(t, 512, 16)
    w_out = ((N_DEV, n, d), BF16, (None, n, d), lambda i, j, k: (i, 0, 0), False)
    (dwg, dwu), sent["dwgu"] = _mmc(
        name + "_dwgu", (N_DEV, 1, t // tk2),
        [(dgate, (None, tk2, n), lambda i, j, k: (i, k, 0)),
         (dup, (None, tk2, n), lambda i, j, k: (i, k, 0)),
         (h, (tk2, d), lambda i, j, k: (k, 0))],
        [(0, 2, 0, TN), (1, 2, 1, TN)], [(n, d)] * 2, lambda accs, refs, rows: accs, [w_out] * 2,
        comm=_carried(comms, "dwgu", avail))
    avail.update(dwg=dwg, dwu=dwu)

    (dwd,), sent["dwd"] = _mmc(
        name + "_dwd", (N_DEV, 1, t // tk),
        [(act, (None, tk, n), lambda i, j, k: (i, k, 0)),
         (dxo16, (tk, d), lambda i, j, k: (k, 0))],
        [(0, 1, 0, TN)], [(n, d)], lambda accs, refs, rows: [0.5 * accs[0]],
        [((N_DEV, n, d), BF16, (None, n, d), lambda i, j, k: (i, 0, 0), False)],
        comm=_carried(comms, "dwd", avail))
    avail.update(dwd=dwd)

    tm3 = _tile(t, 512, 16)

    def epi_dx(accs, refs, rows):
        dx, dw = _rms_bwd(accs[0], refs[4][rows], refs[5][...], refs[6][rows])
        return [dx, dx, dw]

    (dx, dx16, dnorm), sent["dx"] = _mmc(
        name + "_dx", (t // tm3, 1, N_DEV),
        [(dgate, (None, tm3, n), lambda i, j, k: (k, i, 0)),
         (wg, (None, n, d), lambda i, j, k: (k, 0, 0)),
         (dup, (None, tm3, n), lambda i, j, k: (k, i, 0)),
         (wu, (None, n, d), lambda i, j, k: (k, 0, 0)),
         (x, (tm3, d), lambda i, j, k: (i, 0)),
         (norm_w, (1, d), lambda i, j, k: (0, 0)),
         (dxo, (tm3, d), lambda i, j, k: (i, 0))],
        [(0, 1, 0, NN), (2, 3, 0, NN)], [(tm3, d)], epi_dx,
        [((t, d), F32, (tm3, d), lambda i, j, k: (i, 0), False),
         ((t, d), BF16, (tm3, d), lambda i, j, k: (i, 0), False),
         ((1, d), F32, (1, d), lambda i, j, k: (0, 0), True)],
        comm=_carried(comms, "dx", avail))
    return dx, dx16, dwg, dwu, dwd, dnorm, sent


def _rope(tv, cos, s1, s2):
    return tv * cos + pltpu.roll(tv, LANES - 8, 1) * s1 + pltpu.roll(tv, 8, 1) * s2


def _rope_bwd(dr, cos, s1, s2):
    return dr * cos + pltpu.roll(dr * s1, 8, 1) + pltpu.roll(dr * s2, LANES - 8, 1)


def _rope_tables(positions):
    inv_freq = jnp.power(jnp.float32(ROPE_THETA), -jnp.arange(0, ROT_DIM, 2, dtype=F32) / ROT_DIM)
    ang = positions.reshape(-1).astype(F32)[:, None] * inv_freq
    cos, sin = jnp.cos(ang), jnp.sin(ang)
    t = ang.shape[0]
    pad = A_HEAD_DIM - ROT_DIM
    cos_t = jnp.concatenate([cos, cos, jnp.ones((t, pad), F32)], axis=1)
    s1_t = jnp.concatenate([-sin, jnp.zeros((t, pad + 8), F32)], axis=1)
    s2_t = jnp.concatenate([jnp.zeros((t, 8), F32), sin, jnp.zeros((t, pad), F32)], axis=1)
    return tuple(jnp.tile(v, (1, 2)) for v in (cos_t, s1_t, s2_t))


def _att_common(i, k_p, k_c, v_p, v_c, tabs_p, tabs_c):
    kcat = jnp.concatenate([k_p, k_c], axis=0)
    vcat = jnp.concatenate([v_p, v_c], axis=0)
    tabs_k = [jnp.concatenate([a, b], axis=0) for a, b in zip(tabs_p, tabs_c)]
    kvar, vvar = [], []
    for ks in range(2):
        sl = slice(ks * LANES, (ks + 1) * LANES)
        kr = _rope(kcat[:, sl], *tabs_k)
        kvar.append([kr.astype(BF16), pltpu.roll(kr, 64, 1).astype(BF16)])
        vvar.append([vcat[:, sl].astype(BF16), pltpu.roll(vcat[:, sl], 64, 1).astype(BF16)])
    qi = lax.broadcasted_iota(jnp.int32, (ATT_BLOCK, 2 * ATT_BLOCK), 0)
    kj = lax.broadcasted_iota(jnp.int32, (ATT_BLOCK, 2 * ATT_BLOCK), 1)
    dist = qi + ATT_BLOCK - kj
    allowed = (dist >= 0) & (dist < ATT_BLOCK) & ((i > 0) | (kj >= ATT_BLOCK))
    lane = lax.broadcasted_iota(jnp.int32, (1, LANES), 1)
    halves = [lane < 64, lane >= 64]
    return kvar, vvar, allowed, halves


def _att_head_probs(qm, kslab, allowed, sink):
    s = lax.dot_general(qm, kslab, NT, preferred_element_type=F32)
    s = jnp.where(allowed, s, -jnp.inf)
    m = jnp.maximum(jnp.max(s, axis=-1, keepdims=True), sink)
    e = jnp.exp(s - m)
    es = jnp.exp(sink - m)
    den = jnp.sum(e, axis=-1, keepdims=True) + es
    return e / den, es / den


def _head_slots(h):
    pair, half = h // 2, h % 2
    kvh = h // 4
    return pair, half, kvh // 2, int(kvh % 2 != half)


def _attention_fwd(proj, tables, sinks, comm=None):
    t = proj.shape[0]
    nb = t // ATT_BLOCK
    b = ATT_BLOCK
    scale = A_HEAD_DIM ** -0.5

    def body(sink_ref, q_ref, kp_ref, kc_ref, vp_ref, vc_ref,
             cp_ref, ap_ref, bp_ref, cc_ref, ac_ref, bc_ref, o_ref):
        i = pl.program_id(0)
        tabs_p = (cp_ref[...], ap_ref[...], bp_ref[...])
        tabs_c = (cc_ref[...], ac_ref[...], bc_ref[...])
        kvar, vvar, allowed, halves = _att_common(
            i, kp_ref[...], kc_ref[...], vp_ref[...], vc_ref[...], tabs_p, tabs_c)
        for pair in range(A_HEADS // 2):
            sl = slice(pair * LANES, (pair + 1) * LANES)
            q2 = _rope(q_ref[:, sl], *tabs_c) * scale
            acc = jnp.zeros((b, LANES), F32)
            for half in range(2):
                h = 2 * pair + half
                _, _, ks, var = _head_slots(h)
                qm = jnp.where(halves[half], q2, 0.0).astype(BF16)
                p, _ = _att_head_probs(qm, kvar[ks][var], allowed, sink_ref[h])
                o = lax.dot_general(p.astype(BF16), vvar[ks][var], NN, preferred_element_type=F32)
                acc = acc + jnp.where(halves[half], o, 0.0)
            o_ref[:, sl] = acc.astype(o_ref.dtype)

    cur = lambda col: (lambda i: (i, col))
    prev = lambda col: (lambda i: (jnp.maximum(i - 1, 0), col))
    tab = [pl.BlockSpec((b, LANES), prev(0))] * 3 + [pl.BlockSpec((b, LANES), cur(0))] * 3
    (out,), sent = _call(
        "attention_fwd", body, (nb,),
        [pl.BlockSpec(memory_space=pltpu.SMEM),
         pl.BlockSpec((b, A_WIDTH), cur(0)),
         pl.BlockSpec((b, A_KV_WIDTH), prev(OFF_KA // A_KV_WIDTH)),
         pl.BlockSpec((b, A_KV_WIDTH), cur(OFF_KA // A_KV_WIDTH)),
         pl.BlockSpec((b, A_KV_WIDTH), prev(OFF_VA // A_KV_WIDTH)),
         pl.BlockSpec((b, A_KV_WIDTH), cur(OFF_VA // A_KV_WIDTH))] + tab,
        [pl.BlockSpec((b, A_WIDTH), cur(0))],
        [jax.ShapeDtypeStruct((t, A_WIDTH), BF16)], [],
        [sinks, proj, proj, proj, proj, proj, *tables, *tables], comm)
    return out, sent


def _attention_bwd(proj, tables, sinks, d_out, comm=None):
    t = proj.shape[0]
    nb = t // ATT_BLOCK
    b = ATT_BLOCK
    scale = A_HEAD_DIM ** -0.5

    def body(sink_ref, q_ref, kp_ref, kc_ref, vp_ref, vc_ref,
             cp_ref, ap_ref, bp_ref, cc_ref, ac_ref, bc_ref, do_ref,
             dq_ref, dk_ref, dv_ref, dsink_ref, dk_carry, dv_carry):
        step = pl.program_id(0)
        i = nb - 1 - step

        @pl.when(step == 0)
        def _():
            dk_carry[...] = jnp.zeros(dk_carry.shape, F32)
            dv_carry[...] = jnp.zeros(dv_carry.shape, F32)
            dsink_ref[...] = jnp.zeros(dsink_ref.shape, F32)

        tabs_p = (cp_ref[...], ap_ref[...], bp_ref[...])
        tabs_c = (cc_ref[...], ac_ref[...], bc_ref[...])
        kvar, vvar, allowed, halves = _att_common(
            i, kp_ref[...], kc_ref[...], vp_ref[...], vc_ref[...], tabs_p, tabs_c)
        lane = lax.broadcasted_iota(jnp.int32, (1, LANES), 1)
        dk_slab = [jnp.zeros((2 * b, LANES), F32) for _ in range(2)]
        dv_slab = [jnp.zeros((2 * b, LANES), F32) for _ in range(2)]
        dsink = jnp.zeros((1, LANES), F32)
        for pair in range(A_HEADS // 2):
            sl = slice(pair * LANES, (pair + 1) * LANES)
            q2 = _rope(q_ref[:, sl], *tabs_c) * scale
            do2 = do_ref[:, sl].astype(F32)
            dq2 = jnp.zeros((b, LANES), F32)
            for half in range(2):
                h = 2 * pair + half
                _, _, ks, var = _head_slots(h)
                qm = jnp.where(halves[half], q2, 0.0).astype(BF16)
                dom = jnp.where(halves[half], do2, 0.0).astype(BF16)
                p, p_sink = _att_head_probs(qm, kvar[ks][var], allowed, sink_ref[h])
                dp = lax.dot_general(dom, vvar[ks][var], NT, preferred_element_type=F32)
                dsum = jnp.sum(p * dp, axis=-1, keepdims=True)
                ds = (p * (dp - dsum)).astype(BF16)
                dsink = dsink + jnp.where(lane == h, -jnp.sum(p_sink * dsum), 0.0)
                dqh = lax.dot_general(ds, kvar[ks][var], NN, preferred_element_type=F32)
                dq2 = dq2 + jnp.where(halves[half], dqh, 0.0)
                dkx = lax.dot_general(ds, qm, TN, preferred_element_type=F32)
                dvx = lax.dot_general(p.astype(BF16), dom, TN, preferred_element_type=F32)
                if var:
                    dkx = pltpu.roll(dkx, 64, 1)
                    dvx = pltpu.roll(dvx, 64, 1)
                dk_slab[ks] = dk_slab[ks] + dkx
                dv_slab[ks] = dv_slab[ks] + dvx
            dq_ref[:, sl] = _rope_bwd(dq2 * scale, *tabs_c).astype(dq_ref.dtype)
        for ks in range(2):
            sl = slice(ks * LANES, (ks + 1) * LANES)
            dk_cur = dk_slab[ks][b:] + dk_carry[:, sl]
            dk_ref[:, sl] = _rope_bwd(dk_cur, *tabs_c).astype(dk_ref.dtype)
            dv_ref[:, sl] = (dv_slab[ks][b:] + dv_carry[:, sl]).astype(dv_ref.dtype)
            dk_carry[:, sl] = dk_slab[ks][:b]
            dv_carry[:, sl] = dv_slab[ks][:b]
        dsink_ref[...] += dsink

    cur = lambda col: (lambda s: (nb - 1 - s, col))
    prev = lambda col: (lambda s: (jnp.maximum(nb - 2 - s, 0), col))
    tab = [pl.BlockSpec((b, LANES), prev(0))] * 3 + [pl.BlockSpec((b, LANES), cur(0))] * 3
    return _call(
        "attention_bwd", body, (nb,),
        [pl.BlockSpec(memory_space=pltpu.SMEM),
         pl.BlockSpec((b, A_WIDTH), cur(0)),
         pl.BlockSpec((b, A_KV_WIDTH), prev(OFF_KA // A_KV_WIDTH)),
         pl.BlockSpec((b, A_KV_WIDTH), cur(OFF_KA // A_KV_WIDTH)),
         pl.BlockSpec((b, A_KV_WIDTH), prev(OFF_VA // A_KV_WIDTH)),
         pl.BlockSpec((b, A_KV_WIDTH), cur(OFF_VA // A_KV_WIDTH))] + tab
        + [pl.BlockSpec((b, A_WIDTH), cur(0))],
        [pl.BlockSpec((b, A_WIDTH), cur(0)),
         pl.BlockSpec((b, A_KV_WIDTH), cur(0)),
         pl.BlockSpec((b, A_KV_WIDTH), cur(0)),
         pl.BlockSpec((1, LANES), lambda s: (0, 0))],
        [jax.ShapeDtypeStruct((t, A_WIDTH), BF16),
         jax.ShapeDtypeStruct((t, A_KV_WIDTH), BF16),
         jax.ShapeDtypeStruct((t, A_KV_WIDTH), BF16),
         jax.ShapeDtypeStruct((1, LANES), F32)],
        [pltpu.VMEM((b, A_KV_WIDTH), F32), pltpu.VMEM((b, A_KV_WIDTH), F32)],
        [sinks, proj, proj, proj, proj, proj, *tables, *tables, d_out], comm)


HGRN_HEADS_PER_STEP = 4
HGRN_W = HGRN_HEADS_PER_STEP * B_HEAD


def _hgrn_gates(qp, fp, hlb):
    lb = _sig(hlb[0:1] - hlb[1:2])
    sg = _sig(fp)
    big_f = lb + (1.0 - lb) * sg
    kk = (1.0 - lb) * (1.0 - sg)
    sq = _sig(qp)
    return lb, sg, big_f, kk, sq


def _tri(lower):
    r = lax.broadcasted_iota(jnp.int32, (CHUNK, CHUNK), 0)
    c = lax.broadcasted_iota(jnp.int32, (CHUNK, CHUNK), 1)
    return (r >= c) if lower else (r <= c)


def _hgrn_factors(q_ref, k_ref, b_ref, sl):
    out = []
    bh = b_ref[:, sl]
    for i in range(CHUNK // SUB):
        rows = slice(i * SUB, (i + 1) * SUB)
        b0 = b_ref[i * SUB - 1:i * SUB, sl] if i else jnp.zeros((1, B_HEAD), F32)
        eq = jnp.exp(b_ref[rows, sl] - b0)
        ek = jnp.exp(jnp.minimum(b0 - bh, MAX_DECAY_EXP))
        out.append((q_ref[rows, sl] * eq, k_ref[:, sl] * ek, eq, ek))
    return out


def _hgrn_scores(factors):
    rows = [lax.dot_general(qd.astype(BF16), kx.astype(BF16), NT, preferred_element_type=F32)
            for qd, kx, _, _ in factors]
    return jnp.where(_tri(True), jnp.concatenate(rows, axis=0), 0.0)


def _hgrn_specs(nc, reverse):
    w = HGRN_W
    chunk = (lambda c: nc - 1 - c) if reverse else (lambda c: c)
    col = lambda off: pl.BlockSpec((CHUNK, w), lambda hh, c: (chunk(c), off // w + hh))
    return chunk, col


def _hgrn_fwd(proj, hlb, wn, comm=None):
    t = proj.shape[0]
    nc = t // CHUNK
    w = HGRN_W
    chunk, col = _hgrn_specs(nc, False)

    def body(q_ref, f_ref, v_ref, og_ref, hlb_ref, wn_ref, ob_ref, oraw_ref, st_ref,
             state, qs, ks, bs):
        c = pl.program_id(1)

        @pl.when(c == 0)
        def _():
            state[...] = jnp.zeros(state.shape, F32)

        qp = q_ref[...]
        lb, sg, big_f, kk, sq = _hgrn_gates(qp, f_ref[...], hlb_ref[...])
        qs[...] = qp * sq
        ks[...] = kk
        bs[...] = lax.dot_general(_tri(True).astype(F32), jnp.log(big_f), NN,
                                  precision=lax.Precision.HIGHEST, preferred_element_type=F32)
        for h in range(HGRN_HEADS_PER_STEP):
            sl = slice(h * B_HEAD, (h + 1) * B_HEAD)
            bh = bs[:, sl]
            qh = qs[:, sl]
            kh = ks[:, sl]
            vh = v_ref[:, sl].astype(BF16)
            s0 = state[h]
            st_ref[h] = s0.astype(st_ref.dtype)
            a = _hgrn_scores(_hgrn_factors(qs, ks, bs, sl))
            o = lax.dot_general(a.astype(BF16), vh, NN, preferred_element_type=F32)
            o = o + lax.dot_general((qh * jnp.exp(bh)).astype(BF16), s0.astype(BF16), NT,
                                    preferred_element_type=F32)
            bl = bs[CHUNK - 1:CHUNK, sl]
            khat = (kh * jnp.exp(bl - bh)).astype(BF16)
            state[h] = s0 * jnp.exp(bl) + lax.dot_general(vh, khat, TN, preferred_element_type=F32)
            oraw_ref[:, sl] = o
            r = lax.rsqrt(jnp.mean(o * o, axis=-1, keepdims=True) + EPS)
            og = og_ref[:, sl]
            ob_ref[:, sl] = (o * r * wn_ref[:, sl] * (og * _sig(og))).astype(ob_ref.dtype)

    vec = lambda rows: pl.BlockSpec((rows, w), lambda hh, c: (0, hh))
    return _call(
        "hgrn_fwd", body, (B_WIDTH // w, nc),
        [col(OFF_QB), col(OFF_FB), col(OFF_IB), col(OFF_OG), vec(2), vec(1)],
        [pl.BlockSpec((CHUNK, w), lambda hh, c: (c, hh)),
         pl.BlockSpec((CHUNK, w), lambda hh, c: (c, hh)),
         pl.BlockSpec((None, HGRN_HEADS_PER_STEP, B_HEAD, B_HEAD), lambda hh, c: (c, hh, 0, 0))],
        [jax.ShapeDtypeStruct((t, B_WIDTH), BF16),
         jax.ShapeDtypeStruct((t, B_WIDTH), F32),
         jax.ShapeDtypeStruct((nc, B_WIDTH // B_HEAD, B_HEAD, B_HEAD), BF16)],
        [pltpu.VMEM((HGRN_HEADS_PER_STEP, B_HEAD, B_HEAD), F32),
         pltpu.VMEM((CHUNK, w), F32), pltpu.VMEM((CHUNK, w), F32), pltpu.VMEM((CHUNK, w), F32)],
        [proj, proj, proj, proj, hlb, wn], comm)


def _hgrn_bwd(proj, hlb, wn, o_raw, states, d_out, comm=None):
    t = proj.shape[0]
    nc = t // CHUNK
    w = HGRN_W
    chunk, col = _hgrn_specs(nc, True)

    def body(q_ref, f_ref, v_ref, og_ref, hlb_ref, wn_ref, oraw_ref, st_ref, dout_ref,
             dq_ref, df_ref, dv_ref, dog_ref, dlb_ref, dwn_ref,
             dstate, qs, ks, bs, dbs, dks, dqs):
        c = pl.program_id(1)

        @pl.when(c == 0)
        def _():
            dstate[...] = jnp.zeros(dstate.shape, F32)
            dlb_ref[...] = jnp.zeros(dlb_ref.shape, F32)
            dwn_ref[...] = jnp.zeros(dwn_ref.shape, F32)

        qp = q_ref[...]
        lb, sg, big_f, kk, sq = _hgrn_gates(qp, f_ref[...], hlb_ref[...])
        qs[...] = qp * sq
        ks[...] = kk
        bs[...] = lax.dot_general(_tri(True).astype(F32), jnp.log(big_f), NN,
                                  precision=lax.Precision.HIGHEST, preferred_element_type=F32)
        last_row = lax.broadcasted_iota(jnp.int32, (CHUNK, 1), 0) == CHUNK - 1
        for h in range(HGRN_HEADS_PER_STEP):
            sl = slice(h * B_HEAD, (h + 1) * B_HEAD)
            bh = bs[:, sl]
            qh = qs[:, sl]
            kh = ks[:, sl]
            vh = v_ref[:, sl].astype(BF16)
            s0 = st_ref[h]
            ds1 = dstate[h]
            o = oraw_ref[:, sl]
            og = og_ref[:, sl]
            wnh = wn_ref[:, sl]
            sog = _sig(og)
            r = lax.rsqrt(jnp.mean(o * o, axis=-1, keepdims=True) + EPS)
            dob = dout_ref[:, sl].astype(F32)
            don = dob * (og * sog)
            dog_ref[:, sl] = (dob * (o * r * wnh) * (sog * (1.0 + og * (1.0 - sog)))).astype(dog_ref.dtype)
            dwn_ref[:, sl] += jnp.sum(don * o * r, axis=0, keepdims=True)
            gp = don * wnh
            do = r * (gp - o * (r * r) * jnp.mean(gp * o, axis=-1, keepdims=True))
            do16 = do.astype(BF16)
            factors = _hgrn_factors(qs, ks, bs, sl)
            a = _hgrn_scores(factors)
            da = jnp.where(_tri(True), lax.dot_general(do16, vh, NT, preferred_element_type=F32), 0.0)
            dv = lax.dot_general(a.astype(BF16), do16, TN, preferred_element_type=F32)
            dq_rows = []
            dk = jnp.zeros((CHUNK, B_HEAD), F32)
            for i, (qd, kx, eq, ek) in enumerate(factors):
                da_i = da[i * SUB:(i + 1) * SUB]
                dq_rows.append(lax.dot_general(da_i, kx, NN, precision=lax.Precision.HIGHEST,
                                               preferred_element_type=F32) * eq)
                dk = dk + lax.dot_general(da_i, qd, TN, precision=lax.Precision.HIGHEST,
                                          preferred_element_type=F32) * ek
            eb = jnp.exp(bh)
            bl = bs[CHUNK - 1:CHUNK, sl]
            ebl = jnp.exp(bl)
            ekl = jnp.exp(bl - bh)
            qe = (qh * eb).astype(BF16)
            khat = (kh * ekl).astype(BF16)
            ds1_16 = ds1.astype(BF16)
            dq = jnp.concatenate(dq_rows, axis=0) + eb * lax.dot_general(
                do16, s0, NN, preferred_element_type=F32)
            dv = dv + lax.dot_general(khat, ds1_16, NT, preferred_element_type=F32)
            dk_state = ekl * lax.dot_general(vh, ds1_16, NN, preferred_element_type=F32)
            dk = dk + dk_state
            db_last = (jnp.sum(kh * dk_state, axis=0, keepdims=True)
                       + ebl * jnp.sum(s0.astype(F32) * ds1, axis=0, keepdims=True))
            dbs[:, sl] = qh * dq - kh * dk + jnp.where(last_row, db_last, 0.0)
            dks[:, sl] = dk
            dqs[:, sl] = dq
            dstate[h] = ds1 * ebl + lax.dot_general(do16, qe, TN, preferred_element_type=F32)
            dv_ref[:, sl] = dv.astype(dv_ref.dtype)
        dg = lax.dot_general(_tri(False).astype(F32), dbs[...], NN,
                             precision=lax.Precision.HIGHEST, preferred_element_type=F32)
        dq_ref[...] = (dqs[...] * (sq * (1.0 + qp * (1.0 - sq)))).astype(dq_ref.dtype)
        df_minus_dk = dg / big_f - dks[...]
        df_ref[...] = ((1.0 - lb) * df_minus_dk * sg * (1.0 - sg)).astype(df_ref.dtype)
        dlb_ref[...] += jnp.sum((1.0 - sg) * df_minus_dk, axis=0, keepdims=True) * (lb * (1.0 - lb))

    vec = lambda rows: pl.BlockSpec((rows, w), lambda hh, c: (0, hh))
    blk = pl.BlockSpec((CHUNK, w), lambda hh, c: (chunk(c), hh))
    act = jax.ShapeDtypeStruct((t, B_WIDTH), BF16)
    row = jax.ShapeDtypeStruct((1, B_WIDTH), F32)
    return _call(
        "hgrn_bwd", body, (B_WIDTH // w, nc),
        [col(OFF_QB), col(OFF_FB), col(OFF_IB), col(OFF_OG), vec(2), vec(1), blk,
         pl.BlockSpec((None, HGRN_HEADS_PER_STEP, B_HEAD, B_HEAD), lambda hh, c: (chunk(c), hh, 0, 0)),
         blk],
        [blk, blk, blk, blk, vec(1), vec(1)],
        [act, act, act, act, row, row],
        [pltpu.VMEM((HGRN_HEADS_PER_STEP, B_HEAD, B_HEAD), F32)] + [pltpu.VMEM((CHUNK, w), F32)] * 6,
        [proj, proj, proj, proj, hlb, wn, o_raw, states, d_out], comm)


def _mixer_fwd(x1, mix_norm, w_in, sinks, hlb, wn, weights, tables, comms):
    t, d = x1.shape
    n_in = w_in[0].shape[0]
    nb = d // N_DEV
    sent = {}
    h2 = _rmsnorm("mix_norm", x1, mix_norm)
    tm = _tile(t, 1024, 16)
    tn = _tile(n_in, 512, 128)
    (proj,), sent["proj"] = _mmc(
        "mix_proj", (t // tm, n_in // tn, 1),
        [(h2, (tm, d // 2), lambda i, j, k: (i, 0)), (w_in[0], (tn, d // 2), lambda i, j, k: (j, 0)),
         (h2, (tm, d // 2), lambda i, j, k: (i, 1)), (w_in[1], (tn, d // 2), lambda i, j, k: (j, 0))],
        [(0, 1, 0, NT), (2, 3, 0, NT)], [(tm, tn)], lambda accs, refs, rows: accs,
        [((t, n_in), F32, (tm, tn), lambda i, j, k: (i, j), False)], comm=comms.get("proj"))
    out_a, sent["attention"] = _attention_fwd(proj, tables, sinks, comms.get("attention"))
    (out_b, o_raw, states), sent["hgrn"] = _hgrn_fwd(proj, hlb, wn, comms.get("hgrn"))
    wua, wub, w_out = weights(sent)

    def epi_merge(accs, refs, rows):
        ua, ub = accs
        return [_sig(refs[4][rows]) * ua + _sig(refs[5][rows]) * ub, ua, ub]

    wide = ((t, d), BF16, (tm, nb), lambda i, j, k: (i, j), False)
    (merged, ua, ub), sent["merge"] = _mmc(
        "mix_merge", (t // tm, N_DEV, 1),
        [(out_a, (tm, A_WIDTH), lambda i, j, k: (i, 0)),
         (wua, (None, A_WIDTH, nb), lambda i, j, k: (j, 0, 0)),
         (out_b, (tm, B_WIDTH), lambda i, j, k: (i, 0)),
         (wub, (None, B_WIDTH, nb), lambda i, j, k: (j, 0, 0)),
         (proj, (tm, nb), lambda i, j, k: (i, OFF_GA // nb + j)),
         (proj, (tm, nb), lambda i, j, k: (i, (OFF_GA + d) // nb + j))],
        [(0, 1, 0, NN), (2, 3, 1, NN)], [(tm, nb)] * 2, epi_merge, [wide] * 3, comm=comms.get("merge"))

    tn2 = _tile(d, 1024, 128)
    (x2,) = _mm(
        "mix_out", (t // tm, d // tn2, 1),
        [(merged, (tm, d), lambda i, j, k: (i, 0)), (w_out, (d, tn2), lambda i, j, k: (0, j)),
         (x1, (tm, tn2), lambda i, j, k: (i, j))],
        [(0, 1, 0, NN)], [(tm, tn2)], lambda accs, refs, rows: [refs[2][rows] + accs[0]],
        [((t, d), F32, (tm, tn2), lambda i, j, k: (i, j), False)])
    saved = dict(h2=h2, proj=proj, out_a=out_a, out_b=out_b, o_raw=o_raw, states=states,
                 merged=merged, ua=ua, ub=ub)
    return x2, saved, sent


def _mixer_bwd(dx2, dx2_16, x1, mix_norm, w_in, sinks, hlb, wn, wua, wub, w_out, tables, sv, comms):
    t, d = x1.shape
    sent = {}
    avail = dict(sent=sent)
    n_in = w_in[0].shape[0]
    nb = d // N_DEV
    proj = sv["proj"]
    tm = _tile(t, 1024, 16)
    tn = _tile(d, 512, 128)

    def epi_gate(accs, refs, rows):
        dm = accs[0]
        sa = _sig(refs[2][rows])
        sb = _sig(refs[3][rows])
        ua = refs[4][rows].astype(F32)
        ub = refs[5][rows].astype(F32)
        return [dm * sa, dm * sb, dm * ua * sa * (1.0 - sa), dm * ub * sb * (1.0 - sb)]

    wide = ((t, d), BF16, (tm, tn), lambda i, j, k: (i, j), False)
    dua, dub, dga, dgb = _mm(
        "mix_dmerge", (t // tm, d // tn, 1),
        [(dx2_16, (tm, d), lambda i, j, k: (i, 0)),
         (w_out, (tn, d), lambda i, j, k: (j, 0)),
         (proj, (tm, tn), lambda i, j, k: (i, OFF_GA // tn + j)),
         (proj, (tm, tn), lambda i, j, k: (i, (OFF_GA + d) // tn + j)),
         (sv["ua"], (tm, tn), lambda i, j, k: (i, j)),
         (sv["ub"], (tm, tn), lambda i, j, k: (i, j))],
        [(0, 1, 0, NT)], [(tm, tn)], epi_gate, [wide] * 4)

    tk = _tile(t, 1024, 16)
    two = _tile(d, 1024, 128)
    (dw_out,) = _mm(
        "mix_dwout", (d // two, 1, t // tk),
        [(sv["merged"], (tk, two), lambda i, j, k: (k, i)), (dx2_16, (tk, d), lambda i, j, k: (k, 0))],
        [(0, 1, 0, TN)], [(two, d)], lambda accs, refs, rows: accs,
        [((d, d), BF16, (two, d), lambda i, j, k: (i, 0), False)])

    w_spec = lambda rows: ((N_DEV, rows, nb), BF16, (None, rows, nb), lambda i, j, k: (j, 0, 0), False)
    tk2 = _tile(t, 2048, 16)
    dwua, dwub = _mm(
        "mix_dwup", (1, N_DEV, t // tk2),
        [(sv["out_a"], (tk2, A_WIDTH), lambda i, j, k: (k, 0)),
         (dua, (tk2, nb), lambda i, j, k: (k, j)),
         (sv["out_b"], (tk2, B_WIDTH), lambda i, j, k: (k, 0)),
         (dub, (tk2, nb), lambda i, j, k: (k, j))],
        [(0, 1, 0, TN), (2, 3, 1, TN)], [(A_WIDTH, nb), (B_WIDTH, nb)], lambda accs, refs, rows: accs,
        [w_spec(A_WIDTH), w_spec(B_WIDTH)])

    d_out_a, d_out_b = _mm(
        "mix_dup", (t // tm, 1, N_DEV),
        [(dua, (tm, nb), lambda i, j, k: (i, k)),
         (wua, (None, A_WIDTH, nb), lambda i, j, k: (k, 0, 0)),
         (dub, (tm, nb), lambda i, j, k: (i, k)),
         (wub, (None, B_WIDTH, nb), lambda i, j, k: (k, 0, 0))],
        [(0, 1, 0, NT), (2, 3, 1, NT)], [(tm, A_WIDTH), (tm, B_WIDTH)], lambda accs, refs, rows: accs,
        [((t, A_WIDTH), BF16, (tm, A_WIDTH), lambda i, j, k: (i, 0), False),
         ((t, B_WIDTH), BF16, (tm, B_WIDTH), lambda i, j, k: (i, 0), False)])

    avail.update(w_out=dw_out.reshape(N_DEV, d // N_DEV, d), w_up_a=dwua, w_up_b=dwub)
    (dq_a, dk_a, dv_a, dsinks), sent["attention"] = _attention_bwd(
        proj, tables, sinks, d_out_a, _carried(comms, "attention", avail))
    (dq_b, df_b, di_b, dog_b, dlb, dwn), sent["hgrn"] = _hgrn_bwd(
        proj, hlb, wn, sv["o_raw"], sv["states"], d_out_b, _carried(comms, "hgrn", avail))
    dproj = jnp.concatenate([dq_a, dk_a, dv_a, dq_b, df_b, di_b, dog_b, dga, dgb], axis=1)

    tdm = _tile(d, 1024, 128)
    twn = _tile(n_in, 2432, 128)
    tkw = _tile(t, 512, 16)
    (dw_in,) = _mm(
        "mix_dwin", (n_in // twn, d // tdm, t // tkw),
        [(dproj, (tkw, twn), lambda i, j, k: (k, i)), (sv["h2"], (tkw, tdm), lambda i, j, k: (k, j))],
        [(0, 1, 0, TN)], [(twn, tdm)], lambda accs, refs, rows: accs,
        [((n_in, d), BF16, (twn, tdm), lambda i, j, k: (i, j), False)])

    avail.update(w_in=dw_in.reshape(N_DEV, n_in // N_DEV, d))
    tm3 = _tile(t, 512, 16)
    tpn = _tile(n_in, 512, 128)

    def epi_dx(accs, refs, rows):
        dh = jnp.concatenate(accs, axis=1)
        dx, dw = _rms_bwd(dh, refs[3][rows], refs[4][...], refs[5][rows])
        return [dx, dx, dw]

    (dx1, dx1_16, dmix), sent["dx"] = _mmc(
        "mix_dx", (t // tm3, 1, n_in // tpn),
        [(dproj, (tm3, tpn), lambda i, j, k: (i, k)),
         (w_in[0], (tpn, d // 2), lambda i, j, k: (k, 0)), (w_in[1], (tpn, d // 2), lambda i, j, k: (k, 0)),
         (x1, (tm3, d), lambda i, j, k: (i, 0)), (mix_norm, (1, d), lambda i, j, k: (0, 0)),
         (dx2, (tm3, d), lambda i, j, k: (i, 0))],
        [(0, 1, 0, NN), (0, 2, 1, NN)], [(tm3, d // 2)] * 2, epi_dx,
        [((t, d), F32, (tm3, d), lambda i, j, k: (i, 0), False),
         ((t, d), BF16, (tm3, d), lambda i, j, k: (i, 0), False),
         ((1, d), F32, (1, d), lambda i, j, k: (0, 0), True)],
        comm=_carried(comms, "dx", avail))
    small = dict(mix_norm=dmix, attn_sinks=dsinks, hgrn_lb=dlb, hgrn_norm=dwn)
    return dx1, dx1_16, small, avail


def _ple_fwd(x3, ple_norm, p16, wpg, wpp):
    t, d = x3.shape
    nb = d // N_DEV
    h4 = _rmsnorm("ple_norm", x3, ple_norm)
    tm = _tile(t, 1024, 16)

    def epi(accs, refs, rows):
        zg, pp = accs
        return [refs[4][rows] + _sig(zg) * pp, zg, pp]

    blk = lambda dt: ((t, d), dt, (tm, nb), lambda i, j, k: (i, j), False)
    x4, zg, pp = _mm(
        "ple_fwd", (t // tm, N_DEV, 1),
        [(h4, (tm, d), lambda i, j, k: (i, 0)), (wpg, (d, nb), lambda i, j, k: (0, j)),
         (p16, (tm, PLE_DIM), lambda i, j, k: (i, 0)),
         (wpp, (None, PLE_DIM, nb), lambda i, j, k: (j, 0, 0)),
         (x3, (tm, nb), lambda i, j, k: (i, j))],
        [(0, 1, 0, NN), (2, 3, 1, NN)], [(tm, nb)] * 2, epi, [blk(F32), blk(BF16), blk(BF16)])
    return x4, dict(h4=h4, zg=zg, pp=pp)


def _ple_bwd(dx4, x3, ple_norm, p16, wpg, sv):
    t, d = x3.shape
    nb = d // N_DEV

    def gate_grads(dx, zg, pp):
        s = _sig(zg.astype(F32))
        return dx * pp.astype(F32) * s * (1.0 - s), dx * s

    dzg, dpp = _rows("ple_dgate", gate_grads, [(dx4, True), (sv["zg"], True), (sv["pp"], True)],
                     [((t, d), BF16), ((t, d), BF16)], _tile(t, 512, 16))
    (dwpp,) = _mm(
        "ple_dwproj", (1, N_DEV, 1),
        [(p16, (t, PLE_DIM), lambda i, j, k: (0, 0)), (dpp, (t, nb), lambda i, j, k: (0, j))],
        [(0, 1, 0, TN)], [(PLE_DIM, nb)], lambda accs, refs, rows: accs,
        [((N_DEV, PLE_DIM, nb), BF16, (None, PLE_DIM, nb), lambda i, j, k: (j, 0, 0), False)])
    tk = _tile(t, 1024, 16)
    tn = _tile(d, 1024, 128)
    (dwpg,) = _mm(
        "ple_dwgate", (d // tn, 1, t // tk),
        [(sv["h4"], (tk, tn), lambda i, j, k: (k, i)), (dzg, (tk, d), lambda i, j, k: (k, 0))],
        [(0, 1, 0, TN)], [(tn, d)], lambda accs, refs, rows: accs,
        [((d, d), BF16, (tn, d), lambda i, j, k: (i, 0), False)])
    tm3 = _tile(t, 256, 16)

    def epi_dx(accs, refs, rows):
        dx, dw = _rms_bwd(accs[0], refs[2][rows], refs[3][...], refs[4][rows])
        return [dx, dx, dw]

    dx3, dx3_16, dnorm = _mm(
        "ple_dx", (t // tm3, 1, 1),
        [(dzg, (tm3, d), lambda i, j, k: (i, 0)), (wpg, (d, d), lambda i, j, k: (0, 0)),
         (x3, (tm3, d), lambda i, j, k: (i, 0)), (ple_norm, (1, d), lambda i, j, k: (0, 0)),
         (dx4, (tm3, d), lambda i, j, k: (i, 0))],
        [(0, 1, 0, NT)], [(tm3, d)], epi_dx,
        [((t, d), F32, (tm3, d), lambda i, j, k: (i, 0), False),
         ((t, d), BF16, (tm3, d), lambda i, j, k: (i, 0), False),
         ((1, d), F32, (1, d), lambda i, j, k: (0, 0), True)])
    return dx3, dx3_16, dwpp, dwpg, dnorm


def _loss_head(x4, final_norm, target):
    t, d = x4.shape
    tm = _tile(t, 256, 8)

    def body(x_ref, w_ref, tgt_ref, dx_ref, dw_ref, loss_ref):
        i = pl.program_id(0)
        xv = x_ref[...]
        wv = w_ref[...]
        r = lax.rsqrt(jnp.mean(xv * xv, axis=-1, keepdims=True) + EPS)
        err = xv * r * wv - tgt_ref[...]
        part = 0.5 * jnp.sum(jnp.mean(err * err, axis=-1, keepdims=True), axis=0, keepdims=True)
        dx, dw = _rms_bwd(err * (1.0 / d), xv, wv, jnp.zeros_like(xv))
        dx_ref[...] = dx

        @pl.when(i == 0)
        def _():
            dw_ref[...] = dw
            loss_ref[...] = jnp.broadcast_to(part, loss_ref.shape)

        @pl.when(i > 0)
        def _():
            dw_ref[...] += dw
            loss_ref[...] += jnp.broadcast_to(part, loss_ref.shape)

    return pl.pallas_call(
        body,
        name="loss_head",
        grid=(t // tm,),
        in_specs=[pl.BlockSpec((tm, d), lambda i: (i, 0)), pl.BlockSpec((1, d), lambda i: (0, 0)),
                  pl.BlockSpec((tm, d), lambda i: (i, 0))],
        out_specs=[pl.BlockSpec((tm, d), lambda i: (i, 0)), pl.BlockSpec((1, d), lambda i: (0, 0)),
                   pl.BlockSpec((1, LANES), lambda i: (0, 0))],
        out_shape=[jax.ShapeDtypeStruct((t, d), F32), jax.ShapeDtypeStruct((1, d), F32),
                   jax.ShapeDtypeStruct((1, LANES), F32)],
        compiler_params=_params(("arbitrary",)),
    )(x4, final_norm, target)


RELATIONS = ((0, 0), (1, 0), (0, 1), (1, 1))


def _place():
    return lax.axis_index("x"), lax.axis_index("y"), lax.axis_index("c")


def _flip(v, bit):
    return 1 - v if bit else v


def _gather_program(shards):
    n = len(shards)

    def copies(ins, outs, send_sems, recv_sems, local_sems, base):
        x, y, c = _place()
        sibling = (x, y, 1 - c)

        def block(a, rel, core):
            return outs[a].at[4 * _flip(x, rel[0]) + 2 * _flip(y, rel[1]) + core]

        def copy(a, slot, rel, core, to, src=None):
            dst = block(a, rel, core)
            return pltpu.make_async_remote_copy(
                src_ref=dst if src is None else src, dst_ref=dst,
                send_sem=send_sems.at[base[0] + 7 * a + slot], recv_sem=recv_sems.at[base[1] + 7 * a + slot],
                device_id=to, device_id_type=MESH)

        own, first = [], []
        for a in range(n):
            own.append(pltpu.make_async_copy(ins[a], block(a, RELATIONS[0], c), local_sems.at[base[2] + a]))
            first.append(copy(a, 0, RELATIONS[0], c, sibling, src=ins[a]))
            for r in (1, 2, 3):
                rel = RELATIONS[r]
                first.append(copy(a, r, RELATIONS[0], c, (_flip(x, rel[0]), _flip(y, rel[1]), c), src=ins[a]))
        return own, first, copy, sibling, c

    def pre(*args):
        own, first, _, _, _ = copies(*args)
        for cp in own + first:
            cp.start()

    def post(*args):
        own, first, copy, sibling, c = copies(*args)
        passed = []
        for r in (1, 2, 3):
            for a in range(n):
                copy(a, r, RELATIONS[r], c, sibling).wait_recv()
                fwd = copy(a, 3 + r, RELATIONS[r], c, sibling)
                fwd.start()
                passed.append(fwd)
        for a in range(n):
            copy(a, 0, RELATIONS[0], 1 - c, sibling).wait_recv()
            for r in (1, 2, 3):
                copy(a, 3 + r, RELATIONS[r], 1 - c, sibling).wait_recv()
        for cp in first + passed:
            cp.wait_send()
        for mine in own:
            mine.wait()

    return _Comm(list(shards), [jax.ShapeDtypeStruct((N_DEV,) + s.shape, s.dtype) for s in shards],
                 (7 * n, 7 * n, n), pre, post)


def _exchange_program(arrays, n_slots, source, target, slot):
    n = len(arrays)

    def copies(ins, outs, send_sems, recv_sems, local_sems, base):
        place = _place()
        res = []
        for a in range(n):
            for r in range(n_slots):
                src = source(r, place)
                if src is None:
                    continue
                res.append(pltpu.make_async_remote_copy(
                    src_ref=ins[a].at[src], dst_ref=outs[a].at[slot(r)],
                    send_sem=send_sems.at[base[0] + n_slots * a + r],
                    recv_sem=recv_sems.at[base[1] + n_slots * a + r],
                    device_id=target(r, place), device_id_type=MESH))
        return res

    def pre(*args):
        for cp in copies(*args):
            cp.start()

    def post(*args):
        for cp in copies(*args):
            cp.wait()

    n_out = len({slot(r) for r in range(n_slots) if source(r, (0, 0, 0)) is not None})
    return _Comm(list(arrays), [jax.ShapeDtypeStruct((n_out,) + g.shape[1:], g.dtype) for g in arrays],
                 (n_slots * n, n_slots * n, 0), pre, post)


def _sibling_program(grads):
    def source(r, place):
        x, y, c = place
        return 4 * _flip(x, RELATIONS[r][0]) + 2 * _flip(y, RELATIONS[r][1]) + (1 - c)

    return _exchange_program(grads, 4, source, lambda r, p: (p[0], p[1], 1 - p[2]), lambda r: r)


def _chips_program(sums):
    def target(r, place):
        x, y, c = place
        return (_flip(x, RELATIONS[r][0]), _flip(y, RELATIONS[r][1]), c)

    return _exchange_program(sums, 4, lambda r, p: r if r else None, target, lambda r: r - 1)


def _standalone(name, comm):
    return _call(name, lambda: None, (1,), [], [], [], [], [], comm)[1]


def _pair_sum(name, grad, from_sibling, block_ids):
    _, rows, cols = grad.shape
    tr = _tile(rows, 512, 16)

    def body(ids_ref, mine_ref, sib_ref, o_ref):
        o_ref[...] = (mine_ref[...].astype(F32) + sib_ref[...].astype(F32)).astype(o_ref.dtype)

    return pl.pallas_call(
        body,
        name=name,
        grid_spec=pltpu.PrefetchScalarGridSpec(
            num_scalar_prefetch=1,
            grid=(4, rows // tr),
            in_specs=[pl.BlockSpec((None, tr, cols), lambda r, i, ids: (ids[r], i, 0)),
                      pl.BlockSpec((None, tr, cols), lambda r, i, ids: (r, i, 0))],
            out_specs=pl.BlockSpec((None, tr, cols), lambda r, i, ids: (r, i, 0))),
        out_shape=jax.ShapeDtypeStruct((4, rows, cols), BF16),
        compiler_params=_params(("arbitrary", "arbitrary")),
    )(block_ids, grad, from_sibling)


def _adam(w, g, m, v):
    m = ADAM_B1 * m + (1.0 - ADAM_B1) * g
    v = ADAM_B2 * v + (1.0 - ADAM_B2) * (g * g)
    m_hat = m / (1.0 - ADAM_B1 ** ADAM_STEP)
    v_hat = v / (1.0 - ADAM_B2 ** ADAM_STEP)
    delta = -ADAM_LR * (m_hat / (jnp.sqrt(v_hat) + ADAM_EPS) + ADAM_WD * w)
    return delta, m, v


def _adam_sharded(name, items, comm=None):
    _, rows, cols = items[0][0].shape
    n = len(items)
    tr = _tile(rows, max(64, 256 // n), 8)

    def body(*refs):
        for k in range(n):
            w_ref, m_ref, v_ref, s_ref, r1_ref, r2_ref, r3_ref = refs[7 * k:7 * k + 7]
            g_ref, d_ref, nm_ref, nv_ref = refs[7 * n + 4 * k:7 * n + 4 * k + 4]
            g = ((s_ref[...].astype(F32) + r1_ref[...].astype(F32)) + r2_ref[...].astype(F32)) + r3_ref[...].astype(F32)
            delta, nm, nv = _adam(w_ref[...], g, m_ref[...], v_ref[...])
            g_ref[...] = g
            d_ref[...] = delta
            nm_ref[...] = nm
            nv_ref[...] = nv

    slot = lambda s: pl.BlockSpec((None, tr, cols), lambda i: (s, i, 0))
    flat = slot(0)
    args = []
    for w, m, v, sums, from_chips in items:
        args += [w, m, v, sums, from_chips, from_chips, from_chips]
    res, sent = _call(
        name, body, (rows // tr,),
        [flat, flat, flat, slot(0), slot(0), slot(1), slot(2)] * n, [flat] * (4 * n),
        [jax.ShapeDtypeStruct((1, rows, cols), F32)] * (4 * n), [], args, comm)
    return [res[4 * k:4 * k + 4] for k in range(n)], sent


def _all_reduce_small(part):
    rows = part.shape[0]

    def body(x_ref, o_ref, gathered, send_sems, recv_sems):
        x, y, c = _place()
        me = 4 * x + 2 * y + c
        gathered[me] = x_ref[...]
        copies = []
        for k in range(1, N_DEV):
            bits = (k >> 2 & 1, k >> 1 & 1, k & 1)
            peer = (_flip(x, bits[0]), _flip(y, bits[1]), _flip(c, bits[2]))
            copies.append(pltpu.make_async_remote_copy(
                src_ref=x_ref, dst_ref=gathered.at[me],
                send_sem=send_sems.at[k - 1], recv_sem=recv_sems.at[k - 1],
                device_id=peer, device_id_type=MESH))
        for cp in copies:
            cp.start()
        for cp in copies:
            cp.wait()
        total = gathered[0]
        for b in range(1, N_DEV):
            total = total + gathered[b]
        o_ref[...] = total

    return pl.pallas_call(
        body,
        name="small_all_reduce",
        in_specs=[pl.BlockSpec(memory_space=pltpu.VMEM)],
        out_specs=pl.BlockSpec(memory_space=pltpu.VMEM),
        out_shape=jax.ShapeDtypeStruct((rows, LANES), F32),
        scratch_shapes=[pltpu.VMEM((N_DEV, rows, LANES), F32),
                        pltpu.SemaphoreType.DMA((N_DEV - 1,)), pltpu.SemaphoreType.DMA((N_DEV - 1,))],
    )(part)


def _adam_small(w, g, m, v):
    def fn(wv, gv, mv, vv):
        return _adam(wv, gv, mv, vv)

    shape = (w.shape, F32)
    return _rows("adam_small", fn, [(w, True), (g, True), (m, True), (v, True)], [shape] * 3, w.shape[0])


SMALL = ("ffn1_norm", "mix_norm", "attn_sinks", "hgrn_lower_bound", "hgrn_norm", "ffn2_norm", "ple_norm",
         "final_norm")
LARGE = ("ffn1_w_gate", "ffn1_w_up", "ffn1_w_down", "w_in", "w_up_a", "w_up_b", "w_out",
         "ffn2_w_gate", "ffn2_w_up", "ffn2_w_down", "ple_w_gate", "ple_w_proj")
TRANSPOSED = ("ffn1_w_gate", "ffn1_w_up", "ffn2_w_gate", "ffn2_w_up", "w_in")
WEIGHTS = ("ffn1_norm", "ffn1_w_gate", "ffn1_w_up", "ffn1_w_down", "mix_norm", "w_in", "attn_sinks",
           "hgrn_lower_bound", "hgrn_norm", "w_up_a", "w_up_b", "w_out", "ffn2_norm", "ffn2_w_gate",
           "ffn2_w_up", "ffn2_w_down", "ple_norm", "ple_w_gate", "ple_w_proj", "final_norm")


def _pack_rows(arrays):
    rows = []
    for a in arrays:
        flat = a.reshape(-1).astype(F32)
        pad = -flat.shape[0] % LANES
        rows.append(jnp.pad(flat, (0, pad)).reshape(-1, LANES))
    packed = jnp.concatenate(rows, axis=0)
    return jnp.pad(packed, ((0, -packed.shape[0] % 8), (0, 0)))


def _unpack_rows(packed, like):
    out, at = [], 0
    for a in like:
        size = a.size
        n_rows = -(-size // LANES)
        out.append(packed[at:at + n_rows].reshape(-1)[:size].reshape(a.shape))
        at += n_rows
    return out


def kernel(x, p, positions, ffn1_norm, ffn1_w_gate, ffn1_w_up, ffn1_w_down, mix_norm, w_in, attn_sinks, hgrn_lower_bound, hgrn_norm, w_up_a, w_up_b, w_out, ffn2_norm, ffn2_w_gate, ffn2_w_up, ffn2_w_down, ple_norm, ple_w_gate, ple_w_proj, final_norm, loss_target, m_ffn1_norm, m_ffn1_w_gate, m_ffn1_w_up, m_ffn1_w_down, m_mix_norm, m_w_in, m_attn_sinks, m_hgrn_lower_bound, m_hgrn_norm, m_w_up_a, m_w_up_b, m_w_out, m_ffn2_norm, m_ffn2_w_gate, m_ffn2_w_up, m_ffn2_w_down, m_ple_norm, m_ple_w_gate, m_ple_w_proj, m_final_norm, v_ffn1_norm, v_ffn1_w_gate, v_ffn1_w_up, v_ffn1_w_down, v_mix_norm, v_w_in, v_attn_sinks, v_hgrn_lower_bound, v_hgrn_norm, v_w_up_a, v_w_up_b, v_w_out, v_ffn2_norm, v_ffn2_w_gate, v_ffn2_w_up, v_ffn2_w_down, v_ple_norm, v_ple_w_gate, v_ple_w_proj, v_final_norm):
    given = dict(locals())
    w = {n: given[n] for n in WEIGHTS}
    mom = {n: given["m_" + n] for n in WEIGHTS}
    var = {n: given["v_" + n] for n in WEIGHTS}
    t, d = x.shape[1], x.shape[2]

    x0, p16, target = x[0], p[0, 0].astype(BF16), loss_target[0]
    tables = _rope_tables(positions)
    sinks = attn_sinks.reshape(-1)
    final_w = final_norm.reshape(1, d)
    as_stored = lambda n, a: jnp.swapaxes(a, 1, 2) if n in TRANSPOSED else a
    shard = {n: as_stored(n, w[n])[0].astype(BF16) for n in LARGE}
    shard["w_in_left"], shard["w_in_right"] = shard["w_in"][:, :d // 2], shard["w_in"][:, d // 2:]
    xi, yi, ci = _place()
    block_ids = jnp.stack([4 * _flip(xi, rx) + 2 * _flip(yi, ry) + ci for rx, ry in RELATIONS]).astype(jnp.int32)

    def gather(*names):
        return _gather_program([shard[n] for n in names])

    wg1, wu1 = _standalone("gather_ffn1", gather("ffn1_w_gate", "ffn1_w_up"))
    h1 = _rmsnorm("ffn1_norm", x0, ffn1_norm)
    (g1, u1, a1), (wd1, win_left) = _ffn_up("ffn1", h1, wg1, wu1, gather("ffn1_w_down", "w_in_left"))
    x1, (win_right,) = _ffn_down("ffn1", x0, a1, wd1, gather("w_in_right"))
    win = (win_left.reshape(-1, d // 2), win_right.reshape(-1, d // 2))
    mix_args = (mix_norm, win, sinks, hgrn_lower_bound, hgrn_norm)

    def mixer_weights(sent):
        wua, wub, wout_g = sent["proj"]
        return wua, wub, wout_g.reshape(d, d)

    x2, mix_saved, got_w = _mixer_fwd(
        x1, *mix_args, mixer_weights, tables,
        dict(proj=gather("w_up_a", "w_up_b", "w_out"), attention=gather("ffn2_w_up"),
             hgrn=gather("ffn2_w_gate"), merge=gather("ple_w_gate", "ple_w_proj")))
    wua, wub, wout = mixer_weights(got_w)
    (wu2,), (wg2,) = got_w["attention"], got_w["hgrn"]
    wpg_g, wpp = got_w["merge"]
    wpg = wpg_g.reshape(d, d)
    h3 = _rmsnorm("ffn2_norm", x2, ffn2_norm)
    (g2, u2, a2), (wd2,) = _ffn_up("ffn2", h3, wg2, wu2, gather("ffn2_w_down"))
    x3, _ = _ffn_down("ffn2", x2, a2, wd2)
    x4, ple_saved = _ple_fwd(x3, ple_norm, p16, wpg, wpp)
    dx4, d_final, loss_row = _loss_head(x4, final_w, target)

    sums, from_chips = {}, {}

    def to_chips(names, grads, from_sibling):
        for n, g, s in zip(names, grads, from_sibling):
            sums[n] = _pair_sum("pair_sum_" + n, g, s, block_ids)
        return _chips_program([sums[n] for n in names])

    def arrived(names, results):
        from_chips.update(zip(names, results))

    dx3, dx3_16, dwpp, dwpg, d_ple = _ple_bwd(dx4, x3, ple_norm, p16, wpg, ple_saved)
    ple_names = ("ple_w_proj", "ple_w_gate")
    ple_grads = [dwpp, dwpg.reshape(N_DEV, d // N_DEV, d)]
    gu2 = ("ffn2_w_gate", "ffn2_w_up")
    dx2, dx2_16, dwg2, dwu2, dwd2, d_ffn2, sent2 = _ffn_bwd(
        "ffn2b", dx3, dx3_16, x2, h3, g2, u2, a2, wg2, wu2, wd2, ffn2_norm,
        dict(dact=_sibling_program(ple_grads),
             dwgu=lambda av: to_chips(ple_names, ple_grads, av["sent"]["dact"]),
             dwd=lambda av: _sibling_program([av["dwg"], av["dwu"]]),
             dx=lambda av: _merge([to_chips(gu2, [av["dwg"], av["dwu"]], av["sent"]["dwd"]),
                                   _sibling_program([av["dwd"]])])))
    arrived(ple_names, sent2["dwgu"])
    arrived(gu2, sent2["dx"][:2])
    mix3 = ("w_out", "w_up_a", "w_up_b")
    dx1, dx1_16, mix_small, mix_made = _mixer_bwd(
        dx2, dx2_16, x1, *mix_args, wua, wub, wout, tables, mix_saved,
        dict(attention=lambda av: _merge([to_chips(("ffn2_w_down",), [dwd2], sent2["dx"][2:]),
                                          _sibling_program([av[n] for n in mix3])]),
             hgrn=lambda av: to_chips(mix3, [av[n] for n in mix3], av["sent"]["attention"][1:]),
             dx=lambda av: _sibling_program([av["w_in"]])))
    sent_mix = mix_made["sent"]
    arrived(("ffn2_w_down",), sent_mix["attention"][:1])
    arrived(mix3, sent_mix["hgrn"])
    gu1 = ("ffn1_w_gate", "ffn1_w_up")
    grad_x, _, dwg1, dwu1, dwd1, d_ffn1, sent1 = _ffn_bwd(
        "ffn1b", dx1, dx1_16, x0, h1, g1, u1, a1, wg1, wu1, wd1, ffn1_norm,
        dict(dwgu=to_chips(("w_in",), [mix_made["w_in"]], sent_mix["dx"]),
             dwd=lambda av: _sibling_program([av["dwg"], av["dwu"]]),
             dx=lambda av: _merge([to_chips(gu1, [av["dwg"], av["dwu"]], av["sent"]["dwd"]),
                                   _sibling_program([av["dwd"]])])))
    arrived(("w_in",), sent1["dwgu"])
    arrived(gu1, sent1["dx"][:2])
    arrived(("ffn1_w_down",), _standalone("grads_tail", to_chips(("ffn1_w_down",), [dwd1], sent1["dx"][2:])))

    out = {}
    for n in LARGE:
        (res,), _ = _adam_sharded("adam_" + n, [(as_stored(n, w[n]), as_stored(n, mom[n]), as_stored(n, var[n]),
                                                  sums[n], from_chips[n])])
        out[n] = [as_stored(n, a) for a in res]

    dlb = mix_small["hgrn_lb"]
    small_grads = dict(
        ffn1_norm=d_ffn1, mix_norm=mix_small["mix_norm"], attn_sinks=mix_small["attn_sinks"][:, :A_HEADS],
        hgrn_lower_bound=jnp.concatenate([dlb, -dlb], axis=0), hgrn_norm=mix_small["hgrn_norm"],
        ffn2_norm=d_ffn2, ple_norm=d_ple, final_norm=d_final)

    small_like = [w[n] for n in SMALL]
    packed = _pack_rows([small_grads[n].reshape(w[n].shape) for n in SMALL] + [loss_row[:, :1]])
    total = _all_reduce_small(packed)
    n_rows = total.shape[0]
    pack_w = _pack_rows(small_like + [jnp.zeros((1, 1), F32)])
    pack_m = _pack_rows([mom[n] for n in SMALL] + [jnp.zeros((1, 1), F32)])
    pack_v = _pack_rows([var[n] for n in SMALL] + [jnp.zeros((1, 1), F32)])
    deltas = _adam_small(pack_w, total, pack_m, pack_v)
    like = small_like + [jnp.zeros((1, 1), F32)]
    g_small = _unpack_rows(total, like)
    d_small, m_small, v_small = (_unpack_rows(a, like) for a in deltas)
    for k, n in enumerate(SMALL):
        out[n] = [g_small[k], d_small[k], m_small[k], v_small[k]]
    loss = g_small[-1].reshape(())

    return (loss, grad_x.reshape(x.shape),
            *[out[n][0] for n in WEIGHTS], *[out[n][1] for n in WEIGHTS],
            *[out[n][2] for n in WEIGHTS], *[out[n][3] for n in WEIGHTS])
```

```python
import functools

import jax
import jax.numpy as jnp
from jax import lax
from jax.experimental import pallas as pl
from jax.experimental.pallas import tpu as pltpu

F32 = jnp.float32
BF16 = jnp.bfloat16
MESH = pl.DeviceIdType.MESH

EPS = 1e-6
N_DEV = 8
A_HEADS = 16
A_HEAD_DIM = 64
A_WIDTH = 1024
A_KV_WIDTH = 256
ATT_BLOCK = 128
ROT_DIM = 16
ROPE_THETA = 500000.0
B_WIDTH = 1024
B_HEAD = 128
CHUNK = 64
SUB = 16
MAX_DECAY_EXP = 60.0
PLE_DIM = 256
LANES = 128

ADAM_LR = 0.001
ADAM_B1 = 0.9
ADAM_B2 = 0.999
ADAM_EPS = 1e-08
ADAM_WD = 0.01
ADAM_STEP = 10

VMEM_LIMIT = 56 * 1024 * 1024
EPILOGUE_ELEMS = 256 * 1024

NN = (((1,), (0,)), ((), ()))
NT = (((1,), (1,)), ((), ()))
TN = (((0,), (0,)), ((), ()))

OFF_QA = 0
OFF_KA = 1024
OFF_VA = 1280
OFF_QB = 1536
OFF_FB = 2560
OFF_IB = 3584
OFF_OG = 4608
OFF_GA = 5632


def _sig(v):
    return 1.0 / (1.0 + jnp.exp(-v))


def _tile(n, pref, mult):
    t = min(n, pref)
    t -= t % mult
    while n % t:
        t -= mult
    return t


def _params(sem):
    return pltpu.CompilerParams(dimension_semantics=sem, vmem_limit_bytes=VMEM_LIMIT)


HBM_SPEC = pl.BlockSpec(memory_space=pltpu.HBM)


class _Comm:
    def __init__(self, ins, out_shapes, n_sems, pre, post, mid=None):
        self.ins, self.out_shapes, self.n_sems, self.pre, self.post = ins, out_shapes, n_sems, pre, post
        self.mid = mid if mid is not None else (lambda *args: None)
        self.parts = [len(out_shapes)]

    def split(self, outs):
        res, at = [], 0
        for n in self.parts:
            res.append(list(outs[at:at + n]))
            at += n
        return res


def _merge(progs):
    spans, ins, shapes, sems = [], [], [], [0, 0, 0]
    for p in progs:
        spans.append((len(ins), len(shapes), tuple(sems)))
        ins += list(p.ins)
        shapes += list(p.out_shapes)
        sems = [a + b for a, b in zip(sems, p.n_sems)]

    def run(which):
        def go(cin, cout, send, recv, local, base):
            for p, (i0, o0, s0) in zip(progs, spans):
                getattr(p, which)(cin[i0:i0 + len(p.ins)], cout[o0:o0 + len(p.out_shapes)], send, recv, local,
                                  tuple(b + s for b, s in zip(base, s0)))
        return go

    merged = _Comm(ins, shapes, tuple(sems), run("pre"), run("post"), run("mid"))
    merged.parts = [len(p.out_shapes) for p in progs]
    return merged


def _call(name, body, grid, in_specs, out_specs, out_shape, scratch, args, comm=None, aliases=None):
    sem = ("arbitrary",) * len(grid)
    aliases = aliases or {}
    if comm is None:
        res = pl.pallas_call(body, name=name, grid=grid, in_specs=in_specs, out_specs=out_specs,
                             out_shape=out_shape, scratch_shapes=scratch, input_output_aliases=aliases,
                             compiler_params=_params(sem))(*args)
        return list(res), []
    n_in, n_out, n_scr = len(in_specs), len(out_specs), len(scratch)
    c_in, c_out = len(comm.ins), len(comm.out_shapes)
    steps = functools.reduce(lambda a, b: a * b, grid)
    mid_step = (3 * steps) // 4

    def carrier(*refs):
        at = [0]

        def take(n):
            at[0] += n
            return refs[at[0] - n:at[0]]

        ins, cins, outs, couts, scr, sems = take(n_in), take(c_in), take(n_out), take(c_out), take(n_scr), take(3)
        ids = [pl.program_id(a) for a in range(len(grid))]
        step = functools.reduce(lambda acc, ig: acc * ig[1] + ig[0], zip(ids, grid), 0)

        @pl.when(step == 0)
        def _():
            comm.pre(cins, couts, *sems, (0, 0, 0))

        body(*ins, *outs, *scr)

        @pl.when(step == mid_step)
        def _():
            comm.mid(cins, couts, *sems, (0, 0, 0))

        @pl.when(step == steps - 1)
        def _():
            comm.post(cins, couts, *sems, (0, 0, 0))

    res = pl.pallas_call(
        carrier, name=name, grid=grid,
        in_specs=list(in_specs) + [HBM_SPEC] * c_in,
        out_specs=list(out_specs) + [HBM_SPEC] * c_out,
        out_shape=list(out_shape) + list(comm.out_shapes),
        scratch_shapes=list(scratch) + [pltpu.SemaphoreType.DMA((max(n, 1),)) for n in comm.n_sems],
        input_output_aliases=aliases,
        compiler_params=_params(sem),
    )(*args, *comm.ins)
    return list(res[:n_out]), list(res[n_out:])


def _mm(name, grid, ins, prods, acc_shapes, epi, outs, comm=None, aliases=None):
    n_in, n_out, nk = len(ins), len(outs), grid[2]
    tile_rows, tile_cols = acc_shapes[0]
    chunk = tile_rows
    while chunk * tile_cols > EPILOGUE_ELEMS and chunk % 32 == 0:
        chunk //= 2

    def body(*refs):
        in_refs = refs[:n_in]
        out_refs = refs[n_in:n_in + n_out]
        acc_refs = refs[n_in + n_out:]
        first_tile = jnp.logical_and(pl.program_id(0) == 0, pl.program_id(1) == 0)

        def product(lhs, rhs, dims):
            a = in_refs[lhs][...].astype(BF16)
            b = in_refs[rhs][...].astype(BF16)
            return lax.dot_general(a, b, dims, preferred_element_type=F32)

        def write(accs):
            totals = [None] * n_out
            for c in range(tile_rows // chunk):
                rows = slice(c * chunk, (c + 1) * chunk)
                vals = epi([a[rows] for a in accs], in_refs, rows)
                for idx, (o, v, spec) in enumerate(zip(out_refs, vals, outs)):
                    if spec[4]:
                        totals[idx] = v if totals[idx] is None else totals[idx] + v
                    else:
                        o[rows] = v.astype(o.dtype)
            for o, v, spec in zip(out_refs, totals, outs):
                if spec[4]:
                    @pl.when(first_tile)
                    def _():
                        o[...] = v.astype(o.dtype)

                    @pl.when(jnp.logical_not(first_tile))
                    def _():
                        o[...] += v.astype(o.dtype)

        if nk == 1:
            accs = [None] * len(acc_shapes)
            for lhs, rhs, acc, dims in prods:
                t = product(lhs, rhs, dims)
                accs[acc] = t if accs[acc] is None else accs[acc] + t
            write(accs)
        else:
            k = pl.program_id(2)

            @pl.when(k == 0)
            def _():
                for r in acc_refs:
                    r[...] = jnp.zeros(r.shape, F32)

            for lhs, rhs, acc, dims in prods:
                acc_refs[acc][...] += product(lhs, rhs, dims)

            @pl.when(k == nk - 1)
            def _():
                write(acc_refs)

    res, sent = _call(
        name, body, grid,
        [pl.BlockSpec(memory_space=pl.ANY) if b is None else pl.BlockSpec(b, im) for _, b, im in ins],
        [pl.BlockSpec(o[2], o[3]) for o in outs],
        [jax.ShapeDtypeStruct(o[0], o[1]) for o in outs],
        [] if nk == 1 else [pltpu.VMEM(s, F32) for s in acc_shapes],
        [a for a, _, _ in ins], comm, aliases)
    return res if comm is None else (res, sent)


def _rms_bwd(dh, xv, w, dres):
    r = lax.rsqrt(jnp.mean(xv * xv, axis=-1, keepdims=True) + EPS)
    g = dh * w
    dx = dres + r * (g - xv * (r * r) * jnp.mean(g * xv, axis=-1, keepdims=True))
    dw = jnp.sum(dh * xv * r, axis=0, keepdims=True)
    return dx, dw


def _rows(name, fn, ins, outs, tm):
    n_in = len(ins)
    rows = outs[0][0][0]

    def body(*refs):
        vals = fn(*[r[...] for r in refs[:n_in]])
        for o, v in zip(refs[n_in:], vals):
            o[...] = v.astype(o.dtype)

    in_specs = []
    for a, tiled in ins:
        if tiled:
            in_specs.append(pl.BlockSpec((tm, a.shape[1]), lambda i: (i, 0)))
        else:
            in_specs.append(pl.BlockSpec(a.shape, lambda i, nd=a.ndim: (0,) * nd))
    return pl.pallas_call(
        body,
        name=name,
        grid=(rows // tm,),
        in_specs=in_specs,
        out_specs=[pl.BlockSpec((tm, s[1]), lambda i: (i, 0)) for s, _ in outs],
        out_shape=[jax.ShapeDtypeStruct(s, d) for s, d in outs],
        compiler_params=_params(("arbitrary",)),
    )(*[a for a, _ in ins])


def _rmsnorm(name, x, w):
    def fn(xv, wv):
        r = lax.rsqrt(jnp.mean(xv * xv, axis=-1, keepdims=True) + EPS)
        return (xv * r * wv,)

    return _rows(name, fn, [(x, True), (w, False)], [(x.shape, BF16)], _tile(x.shape[0], 512, 16))[0]


def _mmc(*args, comm=None, aliases=None):
    if comm is None:
        return _mm(*args, aliases=aliases), []
    return _mm(*args, comm=comm, aliases=aliases)


def _carried(comms, key, avail):
    c = comms.get(key) if comms else None
    return c(avail) if callable(c) else c


def _ffn_up(name, h, wg, wu, comm=None):
    t, d = h.shape
    n = wg.shape[1]
    tm = _tile(t, 1024, 16)

    def epi_up(accs, refs, rows):
        g, u = accs
        return [g, u, g * _sig(g) * u]

    hid = ((N_DEV, t, n), BF16, (None, tm, n), lambda i, j, k: (j, i, 0), False)
    return _mmc(
        name + "_up", (t // tm, N_DEV, 1),
        [(h, (tm, d), lambda i, j, k: (i, 0)),
         (wg, (None, n, d), lambda i, j, k: (j, 0, 0)),
         (wu, (None, n, d), lambda i, j, k: (j, 0, 0))],
        [(0, 1, 0, NT), (0, 2, 1, NT)], [(tm, n)] * 2, epi_up, [hid] * 3, comm=comm)


def _ffn_down(name, x, act, wd, comm=None):
    t, d = x.shape
    n = wd.shape[1]
    tm2 = _tile(t, 1024, 8)
    tn = _tile(d, 1024, 128)

    def epi_down(accs, refs, rows):
        return [refs[2][rows] + 0.5 * accs[0]]

    (x_out,), sent = _mmc(
        name + "_down", (t // tm2, d // tn, N_DEV),
        [(act, (None, tm2, n), lambda i, j, k: (k, i, 0)),
         (wd, (None, n, tn), lambda i, j, k: (k, 0, j)),
         (x, (tm2, tn), lambda i, j, k: (i, j))],
        [(0, 1, 0, NN)], [(tm2, tn)], epi_down,
        [((t, d), F32, (tm2, tn), lambda i, j, k: (i, j), False)], comm=comm)
    return x_out, sent


def _ffn_bwd(name, dxo, dxo16, x, h, gate, up, act, wg, wu, wd, norm_w, comms=None):
    t, d = x.shape
    n = wg.shape[1]
    tm = _tile(t, 1024, 16)
    sent = {}
    avail = dict(sent=sent)

    def epi_act(accs, refs, rows):
        da = 0.5 * accs[0]
        g = refs[2][rows].astype(F32)
        u = refs[3][rows].astype(F32)
        s = _sig(g)
        return [da * u * (s * (1.0 + g * (1.0 - s))), da * g * s]

    hid_in = lambda a: (a, (None, tm, n), lambda i, j, k: (j, i, 0))
    hid_out = ((N_DEV, t, n), BF16, (None, tm, n), lambda i, j, k: (j, i, 0), False)
    (dgate, dup), sent["dact"] = _mmc(
        name + "_dact", (t // tm, N_DEV, 1),
        [(dxo16, (tm, d), lambda i, j, k: (i, 0)),
         (wd, (None, n, d), lambda i, j, k: (j, 0, 0)),
         hid_in(gate), hid_in(up)],
        [(0, 1, 0, NT)], [(tm, n)], epi_act, [hid_out] * 2, comm=_carried(comms, "dact", avail))

    tk = _tile(t, 1024, 16)
    w_out = ((N_DEV, n, d), BF16, (None, n, d), lambda i, j, k: (i, 0, 0), False)

    def weight_grad(key, acts, stream, scale):
        (dw,), sent[key] = _mmc(
            name + "_" + key, (N_DEV, 1, t // tk),
            [(acts, (None, tk, n), lambda i, j, k: (i, k, 0)), (stream, (tk, d), lambda i, j, k: (k, 0))],
            [(0, 1, 0, TN)], [(n, d)], lambda accs, refs, rows: [scale * accs[0]], [w_out],
            comm=_carried(comms, key, avail))
        avail[key] = dw
        return dw

    dwd = weight_grad("dwd", act, dxo16, 0.5)
    dwg = weight_grad("dwg", dgate, h, 1.0)
    dwu = weight_grad("dwu", dup, h, 1.0)

    tm3 = _tile(t // 2, 512, 16)
    half = t // 2 // tm3

    def epi_dx(accs, refs, rows):
        dx, dw = _rms_bwd(accs[0], refs[4][rows], refs[5][...], refs[6][rows])
        return [dx, dx, dw]

    def dx_half(key, first, filled):
        at = 0 if first else half
        row = lambda i, j, k: (i + at, 0)
        hid = lambda i, j, k: (k, i + at, 0)
        ins = [(dgate, (None, tm3, n), hid), (wg, (None, n, d), lambda i, j, k: (k, 0, 0)),
               (dup, (None, tm3, n), hid), (wu, (None, n, d), lambda i, j, k: (k, 0, 0)),
               (x, (tm3, d), row), (norm_w, (1, d), lambda i, j, k: (0, 0)), (dxo, (tm3, d), row)]
        ins += [(a, None, None) for a in filled]
        res, sent[key] = _mmc(
            name + "_" + key, (half, 1, N_DEV), ins,
            [(0, 1, 0, NN), (2, 3, 0, NN)], [(tm3, d)], epi_dx,
            [((t, d), F32, (tm3, d), row, False), ((t, d), BF16, (tm3, d), row, False),
             ((1, d), F32, (1, d), lambda i, j, k: (0, 0), True)],
            comm=_carried(comms, key, avail), aliases={7 + k: k for k in range(len(filled))})
        return res

    dx_a, dx16_a, dnorm_a = dx_half("dxa", True, [])
    dx, dx16, dnorm_b = dx_half("dxb", False, [dx_a, dx16_a])
    return dx, dx16, dwg, dwu, dwd, dnorm_a + dnorm_b, sent


def _rope(tv, cos, s1, s2):
    return tv * cos + pltpu.roll(tv, LANES - 8, 1) * s1 + pltpu.roll(tv, 8, 1) * s2


def _rope_bwd(dr, cos, s1, s2):
    return dr * cos + pltpu.roll(dr * s1, 8, 1) + pltpu.roll(dr * s2, LANES - 8, 1)


def _rope_tables(positions):
    inv_freq = jnp.power(jnp.float32(ROPE_THETA), -jnp.arange(0, ROT_DIM, 2, dtype=F32) / ROT_DIM)
    ang = positions.reshape(-1).astype(F32)[:, None] * inv_freq
    cos, sin = jnp.cos(ang), jnp.sin(ang)
    t = ang.shape[0]
    pad = A_HEAD_DIM - ROT_DIM
    cos_t = jnp.concatenate([cos, cos, jnp.ones((t, pad), F32)], axis=1)
    s1_t = jnp.concatenate([-sin, jnp.zeros((t, pad + 8), F32)], axis=1)
    s2_t = jnp.concatenate([jnp.zeros((t, 8), F32), sin, jnp.zeros((t, pad), F32)], axis=1)
    return tuple(jnp.tile(v, (1, 2)) for v in (cos_t, s1_t, s2_t))


def _att_common(i, k_p, k_c, v_p, v_c, tabs_p, tabs_c):
    kcat = jnp.concatenate([k_p, k_c], axis=0)
    vcat = jnp.concatenate([v_p, v_c], axis=0)
    tabs_k = [jnp.concatenate([a, b], axis=0) for a, b in zip(tabs_p, tabs_c)]
    kvar, vvar = [], []
    for ks in range(2):
        sl = slice(ks * LANES, (ks + 1) * LANES)
        kr = _rope(kcat[:, sl], *tabs_k)
        kvar.append([kr.astype(BF16), pltpu.roll(kr, 64, 1).astype(BF16)])
        vvar.append([vcat[:, sl].astype(BF16), pltpu.roll(vcat[:, sl], 64, 1).astype(BF16)])
    qi = lax.broadcasted_iota(jnp.int32, (ATT_BLOCK, 2 * ATT_BLOCK), 0)
    kj = lax.broadcasted_iota(jnp.int32, (ATT_BLOCK, 2 * ATT_BLOCK), 1)
    dist = qi + ATT_BLOCK - kj
    allowed = (dist >= 0) & (dist < ATT_BLOCK) & ((i > 0) | (kj >= ATT_BLOCK))
    lane = lax.broadcasted_iota(jnp.int32, (1, LANES), 1)
    halves = [lane < 64, lane >= 64]
    return kvar, vvar, allowed, halves


def _att_head_probs(qm, kslab, allowed, sink):
    s = lax.dot_general(qm, kslab, NT, preferred_element_type=F32)
    s = jnp.where(allowed, s, -jnp.inf)
    m = jnp.maximum(jnp.max(s, axis=-1, keepdims=True), sink)
    e = jnp.exp(s - m)
    es = jnp.exp(sink - m)
    den = jnp.sum(e, axis=-1, keepdims=True) + es
    return e / den, es / den


def _head_slots(h):
    pair, half = h // 2, h % 2
    kvh = h // 4
    return pair, half, kvh // 2, int(kvh % 2 != half)


def _attention_fwd(proj, tables, sinks, comm=None):
    t = proj.shape[0]
    nb = t // ATT_BLOCK
    b = ATT_BLOCK
    scale = A_HEAD_DIM ** -0.5

    def body(sink_ref, q_ref, kp_ref, kc_ref, vp_ref, vc_ref,
             cp_ref, ap_ref, bp_ref, cc_ref, ac_ref, bc_ref, o_ref):
        i = pl.program_id(0)
        tabs_p = (cp_ref[...], ap_ref[...], bp_ref[...])
        tabs_c = (cc_ref[...], ac_ref[...], bc_ref[...])
        kvar, vvar, allowed, halves = _att_common(
            i, kp_ref[...], kc_ref[...], vp_ref[...], vc_ref[...], tabs_p, tabs_c)
        for pair in range(A_HEADS // 2):
            sl = slice(pair * LANES, (pair + 1) * LANES)
            q2 = _rope(q_ref[:, sl], *tabs_c) * scale
            acc = jnp.zeros((b, LANES), F32)
            for half in range(2):
                h = 2 * pair + half
                _, _, ks, var = _head_slots(h)
                qm = jnp.where(halves[half], q2, 0.0).astype(BF16)
                p, _ = _att_head_probs(qm, kvar[ks][var], allowed, sink_ref[h])
                o = lax.dot_general(p.astype(BF16), vvar[ks][var], NN, preferred_element_type=F32)
                acc = acc + jnp.where(halves[half], o, 0.0)
            o_ref[:, sl] = acc.astype(o_ref.dtype)

    cur = lambda col: (lambda i: (i, col))
    prev = lambda col: (lambda i: (jnp.maximum(i - 1, 0), col))
    tab = [pl.BlockSpec((b, LANES), prev(0))] * 3 + [pl.BlockSpec((b, LANES), cur(0))] * 3
    (out,), sent = _call(
        "attention_fwd", body, (nb,),
        [pl.BlockSpec(memory_space=pltpu.SMEM),
         pl.BlockSpec((b, A_WIDTH), cur(0)),
         pl.BlockSpec((b, A_KV_WIDTH), prev(OFF_KA // A_KV_WIDTH)),
         pl.BlockSpec((b, A_KV_WIDTH), cur(OFF_KA // A_KV_WIDTH)),
         pl.BlockSpec((b, A_KV_WIDTH), prev(OFF_VA // A_KV_WIDTH)),
         pl.BlockSpec((b, A_KV_WIDTH), cur(OFF_VA // A_KV_WIDTH))] + tab,
        [pl.BlockSpec((b, A_WIDTH), cur(0))],
        [jax.ShapeDtypeStruct((t, A_WIDTH), BF16)], [],
        [sinks, proj, proj, proj, proj, proj, *tables, *tables], comm)
    return out, sent


def _attention_bwd(proj, tables, sinks, d_out, comm=None):
    t = proj.shape[0]
    nb = t // ATT_BLOCK
    b = ATT_BLOCK
    scale = A_HEAD_DIM ** -0.5

    def body(sink_ref, q_ref, kp_ref, kc_ref, vp_ref, vc_ref,
             cp_ref, ap_ref, bp_ref, cc_ref, ac_ref, bc_ref, do_ref,
             dq_ref, dk_ref, dv_ref, dsink_ref, dk_carry, dv_carry):
        step = pl.program_id(0)
        i = nb - 1 - step

        @pl.when(step == 0)
        def _():
            dk_carry[...] = jnp.zeros(dk_carry.shape, F32)
            dv_carry[...] = jnp.zeros(dv_carry.shape, F32)
            dsink_ref[...] = jnp.zeros(dsink_ref.shape, F32)

        tabs_p = (cp_ref[...], ap_ref[...], bp_ref[...])
        tabs_c = (cc_ref[...], ac_ref[...], bc_ref[...])
        kvar, vvar, allowed, halves = _att_common(
            i, kp_ref[...], kc_ref[...], vp_ref[...], vc_ref[...], tabs_p, tabs_c)
        lane = lax.broadcasted_iota(jnp.int32, (1, LANES), 1)
        dk_slab = [jnp.zeros((2 * b, LANES), F32) for _ in range(2)]
        dv_slab = [jnp.zeros((2 * b, LANES), F32) for _ in range(2)]
        dsink = jnp.zeros((1, LANES), F32)
        for pair in range(A_HEADS // 2):
            sl = slice(pair * LANES, (pair + 1) * LANES)
            q2 = _rope(q_ref[:, sl], *tabs_c) * scale
            do2 = do_ref[:, sl].astype(F32)
            dq2 = jnp.zeros((b, LANES), F32)
            for half in range(2):
                h = 2 * pair + half
                _, _, ks, var = _head_slots(h)
                qm = jnp.where(halves[half], q2, 0.0).astype(BF16)
                dom = jnp.where(halves[half], do2, 0.0).astype(BF16)
                p, p_sink = _att_head_probs(qm, kvar[ks][var], allowed, sink_ref[h])
                dp = lax.dot_general(dom, vvar[ks][var], NT, preferred_element_type=F32)
                dsum = jnp.sum(p * dp, axis=-1, keepdims=True)
                ds = (p * (dp - dsum)).astype(BF16)
                dsink = dsink + jnp.where(lane == h, -jnp.sum(p_sink * dsum), 0.0)
                dqh = lax.dot_general(ds, kvar[ks][var], NN, preferred_element_type=F32)
                dq2 = dq2 + jnp.where(halves[half], dqh, 0.0)
                dkx = lax.dot_general(ds, qm, TN, preferred_element_type=F32)
                dvx = lax.dot_general(p.astype(BF16), dom, TN, preferred_element_type=F32)
                if var:
                    dkx = pltpu.roll(dkx, 64, 1)
                    dvx = pltpu.roll(dvx, 64, 1)
                dk_slab[ks] = dk_slab[ks] + dkx
                dv_slab[ks] = dv_slab[ks] + dvx
            dq_ref[:, sl] = _rope_bwd(dq2 * scale, *tabs_c).astype(dq_ref.dtype)
        for ks in range(2):
            sl = slice(ks * LANES, (ks + 1) * LANES)
            dk_cur = dk_slab[ks][b:] + dk_carry[:, sl]
            dk_ref[:, sl] = _rope_bwd(dk_cur, *tabs_c).astype(dk_ref.dtype)
            dv_ref[:, sl] = (dv_slab[ks][b:] + dv_carry[:, sl]).astype(dv_ref.dtype)
            dk_carry[:, sl] = dk_slab[ks][:b]
            dv_carry[:, sl] = dv_slab[ks][:b]
        dsink_ref[...] += dsink

    cur = lambda col: (lambda s: (nb - 1 - s, col))
    prev = lambda col: (lambda s: (jnp.maximum(nb - 2 - s, 0), col))
    tab = [pl.BlockSpec((b, LANES), prev(0))] * 3 + [pl.BlockSpec((b, LANES), cur(0))] * 3
    return _call(
        "attention_bwd", body, (nb,),
        [pl.BlockSpec(memory_space=pltpu.SMEM),
         pl.BlockSpec((b, A_WIDTH), cur(0)),
         pl.BlockSpec((b, A_KV_WIDTH), prev(OFF_KA // A_KV_WIDTH)),
         pl.BlockSpec((b, A_KV_WIDTH), cur(OFF_KA // A_KV_WIDTH)),
         pl.BlockSpec((b, A_KV_WIDTH), prev(OFF_VA // A_KV_WIDTH)),
         pl.BlockSpec((b, A_KV_WIDTH), cur(OFF_VA // A_KV_WIDTH))] + tab
        + [pl.BlockSpec((b, A_WIDTH), cur(0))],
        [pl.BlockSpec((b, A_WIDTH), cur(0)),
         pl.BlockSpec((b, A_KV_WIDTH), cur(0)),
         pl.BlockSpec((b, A_KV_WIDTH), cur(0)),
         pl.BlockSpec((1, LANES), lambda s: (0, 0))],
        [jax.ShapeDtypeStruct((t, A_WIDTH), BF16),
         jax.ShapeDtypeStruct((t, A_KV_WIDTH), BF16),
         jax.ShapeDtypeStruct((t, A_KV_WIDTH), BF16),
         jax.ShapeDtypeStruct((1, LANES), F32)],
        [pltpu.VMEM((b, A_KV_WIDTH), F32), pltpu.VMEM((b, A_KV_WIDTH), F32)],
        [sinks, proj, proj, proj, proj, proj, *tables, *tables, d_out], comm)


HGRN_HEADS_PER_STEP = 4
HGRN_W = HGRN_HEADS_PER_STEP * B_HEAD


def _hgrn_gates(qp, fp, hlb):
    lb = _sig(hlb[0:1] - hlb[1:2])
    sg = _sig(fp)
    big_f = lb + (1.0 - lb) * sg
    kk = (1.0 - lb) * (1.0 - sg)
    sq = _sig(qp)
    return lb, sg, big_f, kk, sq


def _tri(lower):
    r = lax.broadcasted_iota(jnp.int32, (CHUNK, CHUNK), 0)
    c = lax.broadcasted_iota(jnp.int32, (CHUNK, CHUNK), 1)
    return (r >= c) if lower else (r <= c)


def _hgrn_factors(q_ref, k_ref, b_ref, sl):
    out = []
    bh = b_ref[:, sl]
    for i in range(CHUNK // SUB):
        rows = slice(i * SUB, (i + 1) * SUB)
        b0 = b_ref[i * SUB - 1:i * SUB, sl] if i else jnp.zeros((1, B_HEAD), F32)
        eq = jnp.exp(b_ref[rows, sl] - b0)
        ek = jnp.exp(jnp.minimum(b0 - bh, MAX_DECAY_EXP))
        out.append((q_ref[rows, sl] * eq, k_ref[:, sl] * ek, eq, ek))
    return out


def _hgrn_scores(factors):
    rows = [lax.dot_general(qd.astype(BF16), kx.astype(BF16), NT, preferred_element_type=F32)
            for qd, kx, _, _ in factors]
    return jnp.where(_tri(True), jnp.concatenate(rows, axis=0), 0.0)


def _hgrn_specs(nc, reverse):
    w = HGRN_W
    chunk = (lambda c: nc - 1 - c) if reverse else (lambda c: c)
    col = lambda off: pl.BlockSpec((CHUNK, w), lambda hh, c: (chunk(c), off // w + hh))
    return chunk, col


def _hgrn_fwd(proj, hlb, wn, comm=None):
    t = proj.shape[0]
    nc = t // CHUNK
    w = HGRN_W
    chunk, col = _hgrn_specs(nc, False)

    def body(q_ref, f_ref, v_ref, og_ref, hlb_ref, wn_ref, ob_ref, oraw_ref, st_ref,
             state, qs, ks, bs):
        c = pl.program_id(1)

        @pl.when(c == 0)
        def _():
            state[...] = jnp.zeros(state.shape, F32)

        qp = q_ref[...]
        lb, sg, big_f, kk, sq = _hgrn_gates(qp, f_ref[...], hlb_ref[...])
        qs[...] = qp * sq
        ks[...] = kk
        bs[...] = lax.dot_general(_tri(True).astype(F32), jnp.log(big_f), NN,
                                  precision=lax.Precision.HIGHEST, preferred_element_type=F32)
        for h in range(HGRN_HEADS_PER_STEP):
            sl = slice(h * B_HEAD, (h + 1) * B_HEAD)
            bh = bs[:, sl]
            qh = qs[:, sl]
            kh = ks[:, sl]
            vh = v_ref[:, sl].astype(BF16)
            s0 = state[h]
            st_ref[h] = s0.astype(st_ref.dtype)
            a = _hgrn_scores(_hgrn_factors(qs, ks, bs, sl))
            o = lax.dot_general(a.astype(BF16), vh, NN, preferred_element_type=F32)
            o = o + lax.dot_general((qh * jnp.exp(bh)).astype(BF16), s0.astype(BF16), NT,
                                    preferred_element_type=F32)
            bl = bs[CHUNK - 1:CHUNK, sl]
            khat = (kh * jnp.exp(bl - bh)).astype(BF16)
            state[h] = s0 * jnp.exp(bl) + lax.dot_general(vh, khat, TN, preferred_element_type=F32)
            oraw_ref[:, sl] = o
            r = lax.rsqrt(jnp.mean(o * o, axis=-1, keepdims=True) + EPS)
            og = og_ref[:, sl]
            ob_ref[:, sl] = (o * r * wn_ref[:, sl] * (og * _sig(og))).astype(ob_ref.dtype)

    vec = lambda rows: pl.BlockSpec((rows, w), lambda hh, c: (0, hh))
    return _call(
        "hgrn_fwd", body, (B_WIDTH // w, nc),
        [col(OFF_QB), col(OFF_FB), col(OFF_IB), col(OFF_OG), vec(2), vec(1)],
        [pl.BlockSpec((CHUNK, w), lambda hh, c: (c, hh)),
         pl.BlockSpec((CHUNK, w), lambda hh, c: (c, hh)),
         pl.BlockSpec((None, HGRN_HEADS_PER_STEP, B_HEAD, B_HEAD), lambda hh, c: (c, hh, 0, 0))],
        [jax.ShapeDtypeStruct((t, B_WIDTH), BF16),
         jax.ShapeDtypeStruct((t, B_WIDTH), F32),
         jax.ShapeDtypeStruct((nc, B_WIDTH // B_HEAD, B_HEAD, B_HEAD), BF16)],
        [pltpu.VMEM((HGRN_HEADS_PER_STEP, B_HEAD, B_HEAD), F32),
         pltpu.VMEM((CHUNK, w), F32), pltpu.VMEM((CHUNK, w), F32), pltpu.VMEM((CHUNK, w), F32)],
        [proj, proj, proj, proj, hlb, wn], comm)


def _hgrn_bwd(proj, hlb, wn, o_raw, states, d_out, comm=None):
    t = proj.shape[0]
    nc = t // CHUNK
    w = HGRN_W
    chunk, col = _hgrn_specs(nc, True)

    def body(q_ref, f_ref, v_ref, og_ref, hlb_ref, wn_ref, oraw_ref, st_ref, dout_ref,
             dq_ref, df_ref, dv_ref, dog_ref, dlb_ref, dwn_ref,
             dstate, qs, ks, bs, dbs, dks, dqs):
        c = pl.program_id(1)

        @pl.when(c == 0)
        def _():
            dstate[...] = jnp.zeros(dstate.shape, F32)
            dlb_ref[...] = jnp.zeros(dlb_ref.shape, F32)
            dwn_ref[...] = jnp.zeros(dwn_ref.shape, F32)

        qp = q_ref[...]
        lb, sg, big_f, kk, sq = _hgrn_gates(qp, f_ref[...], hlb_ref[...])
        qs[...] = qp * sq
        ks[...] = kk
        bs[...] = lax.dot_general(_tri(True).astype(F32), jnp.log(big_f), NN,
                                  precision=lax.Precision.HIGHEST, preferred_element_type=F32)
        last_row = lax.broadcasted_iota(jnp.int32, (CHUNK, 1), 0) == CHUNK - 1
        for h in range(HGRN_HEADS_PER_STEP):
            sl = slice(h * B_HEAD, (h + 1) * B_HEAD)
            bh = bs[:, sl]
            qh = qs[:, sl]
            kh = ks[:, sl]
            vh = v_ref[:, sl].astype(BF16)
            s0 = st_ref[h]
            ds1 = dstate[h]
            o = oraw_ref[:, sl]
            og = og_ref[:, sl]
            wnh = wn_ref[:, sl]
            sog = _sig(og)
            r = lax.rsqrt(jnp.mean(o * o, axis=-1, keepdims=True) + EPS)
            dob = dout_ref[:, sl].astype(F32)
            don = dob * (og * sog)
            dog_ref[:, sl] = (dob * (o * r * wnh) * (sog * (1.0 + og * (1.0 - sog)))).astype(dog_ref.dtype)
            dwn_ref[:, sl] += jnp.sum(don * o * r, axis=0, keepdims=True)
            gp = don * wnh
            do = r * (gp - o * (r * r) * jnp.mean(gp * o, axis=-1, keepdims=True))
            do16 = do.astype(BF16)
            factors = _hgrn_factors(qs, ks, bs, sl)
            a = _hgrn_scores(factors)
            da = jnp.where(_tri(True), lax.dot_general(do16, vh, NT, preferred_element_type=F32), 0.0)
            dv = lax.dot_general(a.astype(BF16), do16, TN, preferred_element_type=F32)
            dq_rows = []
            dk = jnp.zeros((CHUNK, B_HEAD), F32)
            for i, (qd, kx, eq, ek) in enumerate(factors):
                da_i = da[i * SUB:(i + 1) * SUB]
                dq_rows.append(lax.dot_general(da_i, kx, NN, precision=lax.Precision.HIGHEST,
                                               preferred_element_type=F32) * eq)
                dk = dk + lax.dot_general(da_i, qd, TN, precision=lax.Precision.HIGHEST,
                                          preferred_element_type=F32) * ek
            eb = jnp.exp(bh)
            bl = bs[CHUNK - 1:CHUNK, sl]
            ebl = jnp.exp(bl)
            ekl = jnp.exp(bl - bh)
            qe = (qh * eb).astype(BF16)
            khat = (kh * ekl).astype(BF16)
            ds1_16 = ds1.astype(BF16)
            dq = jnp.concatenate(dq_rows, axis=0) + eb * lax.dot_general(
                do16, s0, NN, preferred_element_type=F32)
            dv = dv + lax.dot_general(khat, ds1_16, NT, preferred_element_type=F32)
            dk_state = ekl * lax.dot_general(vh, ds1_16, NN, preferred_element_type=F32)
            dk = dk + dk_state
            db_last = (jnp.sum(kh * dk_state, axis=0, keepdims=True)
                       + ebl * jnp.sum(s0.astype(F32) * ds1, axis=0, keepdims=True))
            dbs[:, sl] = qh * dq - kh * dk + jnp.where(last_row, db_last, 0.0)
            dks[:, sl] = dk
            dqs[:, sl] = dq
            dstate[h] = ds1 * ebl + lax.dot_general(do16, qe, TN, preferred_element_type=F32)
            dv_ref[:, sl] = dv.astype(dv_ref.dtype)
        dg = lax.dot_general(_tri(False).astype(F32), dbs[...], NN,
                             precision=lax.Precision.HIGHEST, preferred_element_type=F32)
        dq_ref[...] = (dqs[...] * (sq * (1.0 + qp * (1.0 - sq)))).astype(dq_ref.dtype)
        df_minus_dk = dg / big_f - dks[...]
        df_ref[...] = ((1.0 - lb) * df_minus_dk * sg * (1.0 - sg)).astype(df_ref.dtype)
        dlb_ref[...] += jnp.sum((1.0 - sg) * df_minus_dk, axis=0, keepdims=True) * (lb * (1.0 - lb))

    vec = lambda rows: pl.BlockSpec((rows, w), lambda hh, c: (0, hh))
    blk = pl.BlockSpec((CHUNK, w), lambda hh, c: (chunk(c), hh))
    act = jax.ShapeDtypeStruct((t, B_WIDTH), BF16)
    row = jax.ShapeDtypeStruct((1, B_WIDTH), F32)
    return _call(
        "hgrn_bwd", body, (B_WIDTH // w, nc),
        [col(OFF_QB), col(OFF_FB), col(OFF_IB), col(OFF_OG), vec(2), vec(1), blk,
         pl.BlockSpec((None, HGRN_HEADS_PER_STEP, B_HEAD, B_HEAD), lambda hh, c: (chunk(c), hh, 0, 0)),
         blk],
        [blk, blk, blk, blk, vec(1), vec(1)],
        [act, act, act, act, row, row],
        [pltpu.VMEM((HGRN_HEADS_PER_STEP, B_HEAD, B_HEAD), F32)] + [pltpu.VMEM((CHUNK, w), F32)] * 6,
        [proj, proj, proj, proj, hlb, wn, o_raw, states, d_out], comm)


def _mixer_fwd(x1, mix_norm, w_in, sinks, hlb, wn, weights, tables, comms):
    t, d = x1.shape
    n_in = w_in[0].shape[0]
    nb = d // N_DEV
    sent = {}
    h2 = _rmsnorm("mix_norm", x1, mix_norm)
    tm = _tile(t, 1024, 16)
    tn = _tile(n_in, 512, 128)
    (proj,), sent["proj"] = _mmc(
        "mix_proj", (t // tm, n_in // tn, 1),
        [(h2, (tm, d // 2), lambda i, j, k: (i, 0)), (w_in[0], (tn, d // 2), lambda i, j, k: (j, 0)),
         (h2, (tm, d // 2), lambda i, j, k: (i, 1)), (w_in[1], (tn, d // 2), lambda i, j, k: (j, 0))],
        [(0, 1, 0, NT), (2, 3, 0, NT)], [(tm, tn)], lambda accs, refs, rows: accs,
        [((t, n_in), F32, (tm, tn), lambda i, j, k: (i, j), False)], comm=comms.get("proj"))
    out_a, sent["attention"] = _attention_fwd(proj, tables, sinks, comms.get("attention"))
    (out_b, o_raw, states), sent["hgrn"] = _hgrn_fwd(proj, hlb, wn, comms.get("hgrn"))
    wua, wub, w_out = weights(sent)

    def epi_merge(accs, refs, rows):
        ua, ub = accs
        return [_sig(refs[4][rows]) * ua + _sig(refs[5][rows]) * ub, ua, ub]

    wide = ((t, d), BF16, (tm, nb), lambda i, j, k: (i, j), False)
    (merged, ua, ub), sent["merge"] = _mmc(
        "mix_merge", (t // tm, N_DEV, 1),
        [(out_a, (tm, A_WIDTH), lambda i, j, k: (i, 0)),
         (wua, (None, A_WIDTH, nb), lambda i, j, k: (j, 0, 0)),
         (out_b, (tm, B_WIDTH), lambda i, j, k: (i, 0)),
         (wub, (None, B_WIDTH, nb), lambda i, j, k: (j, 0, 0)),
         (proj, (tm, nb), lambda i, j, k: (i, OFF_GA // nb + j)),
         (proj, (tm, nb), lambda i, j, k: (i, (OFF_GA + d) // nb + j))],
        [(0, 1, 0, NN), (2, 3, 1, NN)], [(tm, nb)] * 2, epi_merge, [wide] * 3, comm=comms.get("merge"))

    tn2 = _tile(d, 1024, 128)
    (x2,) = _mm(
        "mix_out", (t // tm, d // tn2, 1),
        [(merged, (tm, d), lambda i, j, k: (i, 0)), (w_out, (d, tn2), lambda i, j, k: (0, j)),
         (x1, (tm, tn2), lambda i, j, k: (i, j))],
        [(0, 1, 0, NN)], [(tm, tn2)], lambda accs, refs, rows: [refs[2][rows] + accs[0]],
        [((t, d), F32, (tm, tn2), lambda i, j, k: (i, j), False)])
    saved = dict(h2=h2, proj=proj, out_a=out_a, out_b=out_b, o_raw=o_raw, states=states,
                 merged=merged, ua=ua, ub=ub)
    return x2, saved, sent


def _mixer_bwd(dx2, dx2_16, x1, mix_norm, w_in, sinks, hlb, wn, wua, wub, w_out, tables, sv, comms):
    t, d = x1.shape
    sent = {}
    avail = dict(sent=sent)
    n_in = w_in[0].shape[0]
    nb = d // N_DEV
    proj = sv["proj"]
    tm = _tile(t, 1024, 16)
    tn = _tile(d, 512, 128)

    def epi_gate(accs, refs, rows):
        dm = accs[0]
        sa = _sig(refs[2][rows])
        sb = _sig(refs[3][rows])
        ua = refs[4][rows].astype(F32)
        ub = refs[5][rows].astype(F32)
        return [dm * sa, dm * sb, dm * ua * sa * (1.0 - sa), dm * ub * sb * (1.0 - sb)]

    wide = ((t, d), BF16, (tm, tn), lambda i, j, k: (i, j), False)
    dua, dub, dga, dgb = _mm(
        "mix_dmerge", (t // tm, d // tn, 1),
        [(dx2_16, (tm, d), lambda i, j, k: (i, 0)),
         (w_out, (tn, d), lambda i, j, k: (j, 0)),
         (proj, (tm, tn), lambda i, j, k: (i, OFF_GA // tn + j)),
         (proj, (tm, tn), lambda i, j, k: (i, (OFF_GA + d) // tn + j)),
         (sv["ua"], (tm, tn), lambda i, j, k: (i, j)),
         (sv["ub"], (tm, tn), lambda i, j, k: (i, j))],
        [(0, 1, 0, NT)], [(tm, tn)], epi_gate, [wide] * 4)

    tk = _tile(t, 1024, 16)
    two = _tile(d, 1024, 128)
    (dw_out,) = _mm(
        "mix_dwout", (d // two, 1, t // tk),
        [(sv["merged"], (tk, two), lambda i, j, k: (k, i)), (dx2_16, (tk, d), lambda i, j, k: (k, 0))],
        [(0, 1, 0, TN)], [(two, d)], lambda accs, refs, rows: accs,
        [((d, d), BF16, (two, d), lambda i, j, k: (i, 0), False)])

    w_spec = lambda rows: ((N_DEV, rows, nb), BF16, (None, rows, nb), lambda i, j, k: (j, 0, 0), False)
    tk2 = _tile(t, 2048, 16)
    dwua, dwub = _mm(
        "mix_dwup", (1, N_DEV, t // tk2),
        [(sv["out_a"], (tk2, A_WIDTH), lambda i, j, k: (k, 0)),
         (dua, (tk2, nb), lambda i, j, k: (k, j)),
         (sv["out_b"], (tk2, B_WIDTH), lambda i, j, k: (k, 0)),
         (dub, (tk2, nb), lambda i, j, k: (k, j))],
        [(0, 1, 0, TN), (2, 3, 1, TN)], [(A_WIDTH, nb), (B_WIDTH, nb)], lambda accs, refs, rows: accs,
        [w_spec(A_WIDTH), w_spec(B_WIDTH)])

    d_out_a, d_out_b = _mm(
        "mix_dup", (t // tm, 1, N_DEV),
        [(dua, (tm, nb), lambda i, j, k: (i, k)),
         (wua, (None, A_WIDTH, nb), lambda i, j, k: (k, 0, 0)),
         (dub, (tm, nb), lambda i, j, k: (i, k)),
         (wub, (None, B_WIDTH, nb), lambda i, j, k: (k, 0, 0))],
        [(0, 1, 0, NT), (2, 3, 1, NT)], [(tm, A_WIDTH), (tm, B_WIDTH)], lambda accs, refs, rows: accs,
        [((t, A_WIDTH), BF16, (tm, A_WIDTH), lambda i, j, k: (i, 0), False),
         ((t, B_WIDTH), BF16, (tm, B_WIDTH), lambda i, j, k: (i, 0), False)])

    avail.update(w_out=dw_out.reshape(N_DEV, d // N_DEV, d), w_up_a=dwua, w_up_b=dwub)
    (dq_a, dk_a, dv_a, dsinks), sent["attention"] = _attention_bwd(
        proj, tables, sinks, d_out_a, _carried(comms, "attention", avail))
    (dq_b, df_b, di_b, dog_b, dlb, dwn), sent["hgrn"] = _hgrn_bwd(
        proj, hlb, wn, sv["o_raw"], sv["states"], d_out_b, _carried(comms, "hgrn", avail))
    dproj = jnp.concatenate([dq_a, dk_a, dv_a, dq_b, df_b, di_b, dog_b, dga, dgb], axis=1)

    tdm = _tile(d, 1024, 128)
    twn = _tile(n_in, 2432, 128)
    tkw = _tile(t, 512, 16)
    (dw_in,) = _mm(
        "mix_dwin", (n_in // twn, d // tdm, t // tkw),
        [(dproj, (tkw, twn), lambda i, j, k: (k, i)), (sv["h2"], (tkw, tdm), lambda i, j, k: (k, j))],
        [(0, 1, 0, TN)], [(twn, tdm)], lambda accs, refs, rows: accs,
        [((n_in, d), BF16, (twn, tdm), lambda i, j, k: (i, j), False)])

    avail.update(w_in=dw_in.reshape(N_DEV, n_in // N_DEV, d))
    tm3 = _tile(t, 512, 16)
    tpn = _tile(n_in, 512, 128)

    def epi_dx(accs, refs, rows):
        dh = jnp.concatenate(accs, axis=1)
        dx, dw = _rms_bwd(dh, refs[3][rows], refs[4][...], refs[5][rows])
        return [dx, dx, dw]

    (dx1, dx1_16, dmix), sent["dx"] = _mmc(
        "mix_dx", (t // tm3, 1, n_in // tpn),
        [(dproj, (tm3, tpn), lambda i, j, k: (i, k)),
         (w_in[0], (tpn, d // 2), lambda i, j, k: (k, 0)), (w_in[1], (tpn, d // 2), lambda i, j, k: (k, 0)),
         (x1, (tm3, d), lambda i, j, k: (i, 0)), (mix_norm, (1, d), lambda i, j, k: (0, 0)),
         (dx2, (tm3, d), lambda i, j, k: (i, 0))],
        [(0, 1, 0, NN), (0, 2, 1, NN)], [(tm3, d // 2)] * 2, epi_dx,
        [((t, d), F32, (tm3, d), lambda i, j, k: (i, 0), False),
         ((t, d), BF16, (tm3, d), lambda i, j, k: (i, 0), False),
         ((1, d), F32, (1, d), lambda i, j, k: (0, 0), True)],
        comm=_carried(comms, "dx", avail))
    small = dict(mix_norm=dmix, attn_sinks=dsinks, hgrn_lb=dlb, hgrn_norm=dwn)
    return dx1, dx1_16, small, avail


def _ple_fwd(x3, ple_norm, p16, wpg, wpp):
    t, d = x3.shape
    nb = d // N_DEV
    h4 = _rmsnorm("ple_norm", x3, ple_norm)
    tm = _tile(t, 1024, 16)

    def epi(accs, refs, rows):
        zg, pp = accs
        return [refs[4][rows] + _sig(zg) * pp, zg, pp]

    blk = lambda dt: ((t, d), dt, (tm, nb), lambda i, j, k: (i, j), False)
    x4, zg, pp = _mm(
        "ple_fwd", (t // tm, N_DEV, 1),
        [(h4, (tm, d), lambda i, j, k: (i, 0)), (wpg, (d, nb), lambda i, j, k: (0, j)),
         (p16, (tm, PLE_DIM), lambda i, j, k: (i, 0)),
         (wpp, (None, PLE_DIM, nb), lambda i, j, k: (j, 0, 0)),
         (x3, (tm, nb), lambda i, j, k: (i, j))],
        [(0, 1, 0, NN), (2, 3, 1, NN)], [(tm, nb)] * 2, epi, [blk(F32), blk(BF16), blk(BF16)])
    return x4, dict(h4=h4, zg=zg, pp=pp)


def _ple_bwd(dx4, x3, ple_norm, p16, wpg, sv):
    t, d = x3.shape
    nb = d // N_DEV

    def gate_grads(dx, zg, pp):
        s = _sig(zg.astype(F32))
        return dx * pp.astype(F32) * s * (1.0 - s), dx * s

    dzg, dpp = _rows("ple_dgate", gate_grads, [(dx4, True), (sv["zg"], True), (sv["pp"], True)],
                     [((t, d), BF16), ((t, d), BF16)], _tile(t, 512, 16))
    (dwpp,) = _mm(
        "ple_dwproj", (1, N_DEV, 1),
        [(p16, (t, PLE_DIM), lambda i, j, k: (0, 0)), (dpp, (t, nb), lambda i, j, k: (0, j))],
        [(0, 1, 0, TN)], [(PLE_DIM, nb)], lambda accs, refs, rows: accs,
        [((N_DEV, PLE_DIM, nb), BF16, (None, PLE_DIM, nb), lambda i, j, k: (j, 0, 0), False)])
    tk = _tile(t, 1024, 16)
    tn = _tile(d, 1024, 128)
    (dwpg,) = _mm(
        "ple_dwgate", (d // tn, 1, t // tk),
        [(sv["h4"], (tk, tn), lambda i, j, k: (k, i)), (dzg, (tk, d), lambda i, j, k: (k, 0))],
        [(0, 1, 0, TN)], [(tn, d)], lambda accs, refs, rows: accs,
        [((d, d), BF16, (tn, d), lambda i, j, k: (i, 0), False)])
    tm3 = _tile(t, 256, 16)

    def epi_dx(accs, refs, rows):
        dx, dw = _rms_bwd(accs[0], refs[2][rows], refs[3][...], refs[4][rows])
        return [dx, dx, dw]

    dx3, dx3_16, dnorm = _mm(
        "ple_dx", (t // tm3, 1, 1),
        [(dzg, (tm3, d), lambda i, j, k: (i, 0)), (wpg, (d, d), lambda i, j, k: (0, 0)),
         (x3, (tm3, d), lambda i, j, k: (i, 0)), (ple_norm, (1, d), lambda i, j, k: (0, 0)),
         (dx4, (tm3, d), lambda i, j, k: (i, 0))],
        [(0, 1, 0, NT)], [(tm3, d)], epi_dx,
        [((t, d), F32, (tm3, d), lambda i, j, k: (i, 0), False),
         ((t, d), BF16, (tm3, d), lambda i, j, k: (i, 0), False),
         ((1, d), F32, (1, d), lambda i, j, k: (0, 0), True)])
    return dx3, dx3_16, dwpp, dwpg, dnorm


def _loss_head(x4, final_norm, target):
    t, d = x4.shape
    tm = _tile(t, 256, 8)

    def body(x_ref, w_ref, tgt_ref, dx_ref, dw_ref, loss_ref):
        i = pl.program_id(0)
        xv = x_ref[...]
        wv = w_ref[...]
        r = lax.rsqrt(jnp.mean(xv * xv, axis=-1, keepdims=True) + EPS)
        err = xv * r * wv - tgt_ref[...]
        part = 0.5 * jnp.sum(jnp.mean(err * err, axis=-1, keepdims=True), axis=0, keepdims=True)
        dx, dw = _rms_bwd(err * (1.0 / d), xv, wv, jnp.zeros_like(xv))
        dx_ref[...] = dx

        @pl.when(i == 0)
        def _():
            dw_ref[...] = dw
            loss_ref[...] = jnp.broadcast_to(part, loss_ref.shape)

        @pl.when(i > 0)
        def _():
            dw_ref[...] += dw
            loss_ref[...] += jnp.broadcast_to(part, loss_ref.shape)

    return pl.pallas_call(
        body,
        name="loss_head",
        grid=(t // tm,),
        in_specs=[pl.BlockSpec((tm, d), lambda i: (i, 0)), pl.BlockSpec((1, d), lambda i: (0, 0)),
                  pl.BlockSpec((tm, d), lambda i: (i, 0))],
        out_specs=[pl.BlockSpec((tm, d), lambda i: (i, 0)), pl.BlockSpec((1, d), lambda i: (0, 0)),
                   pl.BlockSpec((1, LANES), lambda i: (0, 0))],
        out_shape=[jax.ShapeDtypeStruct((t, d), F32), jax.ShapeDtypeStruct((1, d), F32),
                   jax.ShapeDtypeStruct((1, LANES), F32)],
        compiler_params=_params(("arbitrary",)),
    )(x4, final_norm, target)


RELATIONS = ((0, 0), (1, 0), (0, 1), (1, 1))


def _place():
    return lax.axis_index("x"), lax.axis_index("y"), lax.axis_index("c")


def _flip(v, bit):
    return 1 - v if bit else v


def _gather_program(shards):
    n = len(shards)

    def copies(ins, outs, send_sems, recv_sems, local_sems, base):
        x, y, c = _place()
        sibling = (x, y, 1 - c)

        def block(a, rel, core):
            return outs[a].at[4 * _flip(x, rel[0]) + 2 * _flip(y, rel[1]) + core]

        def copy(a, slot, rel, core, to, src=None):
            dst = block(a, rel, core)
            return pltpu.make_async_remote_copy(
                src_ref=dst if src is None else src, dst_ref=dst,
                send_sem=send_sems.at[base[0] + 7 * a + slot], recv_sem=recv_sems.at[base[1] + 7 * a + slot],
                device_id=to, device_id_type=MESH)

        own, first = [], []
        for a in range(n):
            own.append(pltpu.make_async_copy(ins[a], block(a, RELATIONS[0], c), local_sems.at[base[2] + a]))
            first.append(copy(a, 0, RELATIONS[0], c, sibling, src=ins[a]))
            for r in (1, 2, 3):
                rel = RELATIONS[r]
                first.append(copy(a, r, RELATIONS[0], c, (_flip(x, rel[0]), _flip(y, rel[1]), c), src=ins[a]))
        return own, first, copy, sibling, c

    def pre(*args):
        own, first, _, _, _ = copies(*args)
        for cp in own + first:
            cp.start()

    def passed_on(copy, sibling, c):
        return [copy(a, 3 + r, RELATIONS[r], c, sibling) for r in (1, 2, 3) for a in range(n)]

    def mid(*args):
        _, _, copy, sibling, c = copies(*args)
        for r in (1, 2, 3):
            for a in range(n):
                copy(a, r, RELATIONS[r], c, sibling).wait_recv()
        for fwd in passed_on(copy, sibling, c):
            fwd.start()

    def post(*args):
        own, first, copy, sibling, c = copies(*args)
        for a in range(n):
            copy(a, 0, RELATIONS[0], 1 - c, sibling).wait_recv()
            for r in (1, 2, 3):
                copy(a, 3 + r, RELATIONS[r], 1 - c, sibling).wait_recv()
        for cp in first + passed_on(copy, sibling, c):
            cp.wait_send()
        for mine in own:
            mine.wait()

    return _Comm(list(shards), [jax.ShapeDtypeStruct((N_DEV,) + s.shape, s.dtype) for s in shards],
                 (7 * n, 7 * n, n), pre, post, mid)


def _exchange_program(arrays, n_slots, source, target, slot):
    n = len(arrays)

    def copies(ins, outs, send_sems, recv_sems, local_sems, base):
        place = _place()
        res = []
        for a in range(n):
            for r in range(n_slots):
                src = source(r, place)
                if src is None:
                    continue
                res.append(pltpu.make_async_remote_copy(
                    src_ref=ins[a].at[src], dst_ref=outs[a].at[slot(r)],
                    send_sem=send_sems.at[base[0] + n_slots * a + r],
                    recv_sem=recv_sems.at[base[1] + n_slots * a + r],
                    device_id=target(r, place), device_id_type=MESH))
        return res

    def pre(*args):
        for cp in copies(*args):
            cp.start()

    def post(*args):
        for cp in copies(*args):
            cp.wait()

    n_out = len({slot(r) for r in range(n_slots) if source(r, (0, 0, 0)) is not None})
    return _Comm(list(arrays), [jax.ShapeDtypeStruct((n_out,) + g.shape[1:], g.dtype) for g in arrays],
                 (n_slots * n, n_slots * n, 0), pre, post)


def _sibling_program(grads):
    def source(r, place):
        x, y, c = place
        return 4 * _flip(x, RELATIONS[r][0]) + 2 * _flip(y, RELATIONS[r][1]) + (1 - c)

    return _exchange_program(grads, 4, source, lambda r, p: (p[0], p[1], 1 - p[2]), lambda r: r)


def _chips_program(sums, relations=(1, 2, 3)):
    def target(r, place):
        x, y, c = place
        return (_flip(x, RELATIONS[r][0]), _flip(y, RELATIONS[r][1]), c)

    return _exchange_program(sums, 4, lambda r, p: r if r in relations else None, target,
                             lambda r: relations.index(r) if r in relations else 0)


def _standalone(name, comm):
    return _call(name, lambda: None, (1,), [], [], [], [], [], comm)[1]


def _pair_sum(name, grad, from_sibling, block_ids):
    _, rows, cols = grad.shape
    tr = _tile(rows, 512, 16)

    def body(ids_ref, mine_ref, sib_ref, o_ref):
        o_ref[...] = (mine_ref[...].astype(F32) + sib_ref[...].astype(F32)).astype(o_ref.dtype)

    return pl.pallas_call(
        body,
        name=name,
        grid_spec=pltpu.PrefetchScalarGridSpec(
            num_scalar_prefetch=1,
            grid=(4, rows // tr),
            in_specs=[pl.BlockSpec((None, tr, cols), lambda r, i, ids: (ids[r], i, 0)),
                      pl.BlockSpec((None, tr, cols), lambda r, i, ids: (r, i, 0))],
            out_specs=pl.BlockSpec((None, tr, cols), lambda r, i, ids: (r, i, 0))),
        out_shape=jax.ShapeDtypeStruct((4, rows, cols), BF16),
        compiler_params=_params(("arbitrary", "arbitrary")),
    )(block_ids, grad, from_sibling)


def _adam(w, g, m, v):
    m = ADAM_B1 * m + (1.0 - ADAM_B1) * g
    v = ADAM_B2 * v + (1.0 - ADAM_B2) * (g * g)
    m_hat = m / (1.0 - ADAM_B1 ** ADAM_STEP)
    v_hat = v / (1.0 - ADAM_B2 ** ADAM_STEP)
    delta = -ADAM_LR * (m_hat / (jnp.sqrt(v_hat) + ADAM_EPS) + ADAM_WD * w)
    return delta, m, v


def _adam_sharded(name, items, comm=None):
    _, rows, cols = items[0][0].shape
    n = len(items)
    tr = _tile(rows, max(64, 256 // n), 8)

    def body(*refs):
        for k in range(n):
            w_ref, m_ref, v_ref, s_ref, r1_ref, r2_ref, r3_ref = refs[7 * k:7 * k + 7]
            g_ref, d_ref, nm_ref, nv_ref = refs[7 * n + 4 * k:7 * n + 4 * k + 4]
            g = ((s_ref[...].astype(F32) + r1_ref[...].astype(F32)) + r2_ref[...].astype(F32)) + r3_ref[...].astype(F32)
            delta, nm, nv = _adam(w_ref[...], g, m_ref[...], v_ref[...])
            g_ref[...] = g
            d_ref[...] = delta
            nm_ref[...] = nm
            nv_ref[...] = nv

    slot = lambda s: pl.BlockSpec((None, tr, cols), lambda i: (s, i, 0))
    flat = slot(0)
    args, specs = [], []
    for w, m, v, sums, from_chips in items:
        args += [w, m, v, sums] + [a for a, _ in from_chips]
        specs += [flat, flat, flat, slot(0)] + [slot(s) for _, s in from_chips]
    res, sent = _call(
        name, body, (rows // tr,), specs, [flat] * (4 * n),
        [jax.ShapeDtypeStruct((1, rows, cols), F32)] * (4 * n), [], args, comm)
    return [res[4 * k:4 * k + 4] for k in range(n)], sent


def _all_reduce_small(part):
    rows = part.shape[0]

    def body(x_ref, o_ref, gathered, send_sems, recv_sems):
        x, y, c = _place()
        me = 4 * x + 2 * y + c
        gathered[me] = x_ref[...]
        copies = []
        for k in range(1, N_DEV):
            bits = (k >> 2 & 1, k >> 1 & 1, k & 1)
            peer = (_flip(x, bits[0]), _flip(y, bits[1]), _flip(c, bits[2]))
            copies.append(pltpu.make_async_remote_copy(
                src_ref=x_ref, dst_ref=gathered.at[me],
                send_sem=send_sems.at[k - 1], recv_sem=recv_sems.at[k - 1],
                device_id=peer, device_id_type=MESH))
        for cp in copies:
            cp.start()
        for cp in copies:
            cp.wait()
        total = gathered[0]
        for b in range(1, N_DEV):
            total = total + gathered[b]
        o_ref[...] = total

    return pl.pallas_call(
        body,
        name="small_all_reduce",
        in_specs=[pl.BlockSpec(memory_space=pltpu.VMEM)],
        out_specs=pl.BlockSpec(memory_space=pltpu.VMEM),
        out_shape=jax.ShapeDtypeStruct((rows, LANES), F32),
        scratch_shapes=[pltpu.VMEM((N_DEV, rows, LANES), F32),
                        pltpu.SemaphoreType.DMA((N_DEV - 1,)), pltpu.SemaphoreType.DMA((N_DEV - 1,))],
    )(part)


def _adam_small(w, g, m, v):
    def fn(wv, gv, mv, vv):
        return _adam(wv, gv, mv, vv)

    shape = (w.shape, F32)
    return _rows("adam_small", fn, [(w, True), (g, True), (m, True), (v, True)], [shape] * 3, w.shape[0])


SMALL = ("ffn1_norm", "mix_norm", "attn_sinks", "hgrn_lower_bound", "hgrn_norm", "ffn2_norm", "ple_norm",
         "final_norm")
LARGE = ("ffn1_w_gate", "ffn1_w_up", "ffn1_w_down", "w_in", "w_up_a", "w_up_b", "w_out",
         "ffn2_w_gate", "ffn2_w_up", "ffn2_w_down", "ple_w_gate", "ple_w_proj")
TRANSPOSED = ("ffn1_w_gate", "ffn1_w_up", "ffn2_w_gate", "ffn2_w_up", "w_in")
WEIGHTS = ("ffn1_norm", "ffn1_w_gate", "ffn1_w_up", "ffn1_w_down", "mix_norm", "w_in", "attn_sinks",
           "hgrn_lower_bound", "hgrn_norm", "w_up_a", "w_up_b", "w_out", "ffn2_norm", "ffn2_w_gate",
           "ffn2_w_up", "ffn2_w_down", "ple_norm", "ple_w_gate", "ple_w_proj", "final_norm")


def _pack_rows(arrays):
    rows = []
    for a in arrays:
        flat = a.reshape(-1).astype(F32)
        pad = -flat.shape[0] % LANES
        rows.append(jnp.pad(flat, (0, pad)).reshape(-1, LANES))
    packed = jnp.concatenate(rows, axis=0)
    return jnp.pad(packed, ((0, -packed.shape[0] % 8), (0, 0)))


def _unpack_rows(packed, like):
    out, at = [], 0
    for a in like:
        size = a.size
        n_rows = -(-size // LANES)
        out.append(packed[at:at + n_rows].reshape(-1)[:size].reshape(a.shape))
        at += n_rows
    return out


def kernel(x, p, positions, ffn1_norm, ffn1_w_gate, ffn1_w_up, ffn1_w_down, mix_norm, w_in, attn_sinks, hgrn_lower_bound, hgrn_norm, w_up_a, w_up_b, w_out, ffn2_norm, ffn2_w_gate, ffn2_w_up, ffn2_w_down, ple_norm, ple_w_gate, ple_w_proj, final_norm, loss_target, m_ffn1_norm, m_ffn1_w_gate, m_ffn1_w_up, m_ffn1_w_down, m_mix_norm, m_w_in, m_attn_sinks, m_hgrn_lower_bound, m_hgrn_norm, m_w_up_a, m_w_up_b, m_w_out, m_ffn2_norm, m_ffn2_w_gate, m_ffn2_w_up, m_ffn2_w_down, m_ple_norm, m_ple_w_gate, m_ple_w_proj, m_final_norm, v_ffn1_norm, v_ffn1_w_gate, v_ffn1_w_up, v_ffn1_w_down, v_mix_norm, v_w_in, v_attn_sinks, v_hgrn_lower_bound, v_hgrn_norm, v_w_up_a, v_w_up_b, v_w_out, v_ffn2_norm, v_ffn2_w_gate, v_ffn2_w_up, v_ffn2_w_down, v_ple_norm, v_ple_w_gate, v_ple_w_proj, v_final_norm):
    given = dict(locals())
    w = {n: given[n] for n in WEIGHTS}
    mom = {n: given["m_" + n] for n in WEIGHTS}
    var = {n: given["v_" + n] for n in WEIGHTS}
    t, d = x.shape[1], x.shape[2]

    x0, p16, target = x[0], p[0, 0].astype(BF16), loss_target[0]
    tables = _rope_tables(positions)
    sinks = attn_sinks.reshape(-1)
    final_w = final_norm.reshape(1, d)
    as_stored = lambda n, a: jnp.swapaxes(a, 1, 2) if n in TRANSPOSED else a
    shard = {n: as_stored(n, w[n])[0].astype(BF16) for n in LARGE}
    shard["w_in_left"], shard["w_in_right"] = shard["w_in"][:, :d // 2], shard["w_in"][:, d // 2:]
    xi, yi, ci = _place()
    block_ids = jnp.stack([4 * _flip(xi, rx) + 2 * _flip(yi, ry) + ci for rx, ry in RELATIONS]).astype(jnp.int32)

    def gather(*names):
        return _gather_program([shard[n] for n in names])

    wg1, wu1 = _standalone("gather_ffn1", gather("ffn1_w_gate", "ffn1_w_up"))
    h1 = _rmsnorm("ffn1_norm", x0, ffn1_norm)
    (g1, u1, a1), (wd1, win_left) = _ffn_up("ffn1", h1, wg1, wu1, gather("ffn1_w_down", "w_in_left"))
    x1, (win_right,) = _ffn_down("ffn1", x0, a1, wd1, gather("w_in_right"))
    win = (win_left.reshape(-1, d // 2), win_right.reshape(-1, d // 2))
    mix_args = (mix_norm, win, sinks, hgrn_lower_bound, hgrn_norm)

    def mixer_weights(sent):
        wua, wub, wout_g = sent["proj"]
        return wua, wub, wout_g.reshape(d, d)

    x2, mix_saved, got_w = _mixer_fwd(
        x1, *mix_args, mixer_weights, tables,
        dict(proj=gather("w_up_a", "w_up_b", "w_out"), attention=gather("ffn2_w_up"),
             hgrn=gather("ffn2_w_gate"), merge=gather("ple_w_gate", "ple_w_proj")))
    wua, wub, wout = mixer_weights(got_w)
    (wu2,), (wg2,) = got_w["attention"], got_w["hgrn"]
    wpg_g, wpp = got_w["merge"]
    wpg = wpg_g.reshape(d, d)
    h3 = _rmsnorm("ffn2_norm", x2, ffn2_norm)
    (g2, u2, a2), (wd2,) = _ffn_up("ffn2", h3, wg2, wu2, gather("ffn2_w_down"))
    x3, _ = _ffn_down("ffn2", x2, a2, wd2)
    x4, ple_saved = _ple_fwd(x3, ple_norm, p16, wpg, wpp)
    dx4, d_final, loss_row = _loss_head(x4, final_w, target)

    sums, from_chips = {}, {}

    def pair_sums(names, grads, from_sibling):
        for n, g, s in zip(names, grads, from_sibling):
            sums[n] = _pair_sum("pair_sum_" + n, g, s, block_ids)

    def to_chips(names, grads, from_sibling):
        pair_sums(names, grads, from_sibling)
        return _chips_program([sums[n] for n in names])

    def arrived(names, results):
        for n, r in zip(names, results):
            from_chips[n] = [(r, 0), (r, 1), (r, 2)]

    def ffn_comms(first, second, names):
        dn, gn, un = ((n,) for n in names)
        return dict(
            dact=first, dwd=second,
            dwg=lambda av: _sibling_program([av["dwd"]]),
            dwu=lambda av: _merge([to_chips(dn, [av["dwd"]], av["sent"]["dwg"]), _sibling_program([av["dwg"]])]),
            dxa=lambda av: _merge([to_chips(gn, [av["dwg"]], av["sent"]["dwu"][1:]), _sibling_program([av["dwu"]])]),
            dxb=lambda av: to_chips(un, [av["dwu"]], av["sent"]["dxa"][1:]))

    def ffn_arrived(names, sent):
        arrived(names, [sent["dwu"][0], sent["dxa"][0], sent["dxb"][0]])

    dx3, dx3_16, dwpp, dwpg, d_ple = _ple_bwd(dx4, x3, ple_norm, p16, wpg, ple_saved)
    ple_names = ("ple_w_proj", "ple_w_gate")
    ple_grads = [dwpp, dwpg.reshape(N_DEV, d // N_DEV, d)]
    ffn2_names = ("ffn2_w_down", "ffn2_w_gate", "ffn2_w_up")
    dx2, dx2_16, _, _, _, d_ffn2, sent2 = _ffn_bwd(
        "ffn2b", dx3, dx3_16, x2, h3, g2, u2, a2, wg2, wu2, wd2, ffn2_norm,
        ffn_comms(_sibling_program(ple_grads), lambda av: to_chips(ple_names, ple_grads, av["sent"]["dact"]),
                  ffn2_names))
    arrived(ple_names, sent2["dwd"])
    ffn_arrived(ffn2_names, sent2)
    mix3 = ("w_out", "w_up_a", "w_up_b")
    dx1, dx1_16, mix_small, mix_made = _mixer_bwd(
        dx2, dx2_16, x1, *mix_args, wua, wub, wout, tables, mix_saved,
        dict(attention=lambda av: _sibling_program([av[n] for n in mix3]),
             hgrn=lambda av: to_chips(mix3, [av[n] for n in mix3], av["sent"]["attention"]),
             dx=lambda av: _sibling_program([av["w_in"]])))
    sent_mix = mix_made["sent"]
    arrived(mix3, sent_mix["hgrn"])
    pair_sums(("w_in",), [mix_made["w_in"]], sent_mix["dx"])
    ffn1_names = ("ffn1_w_down", "ffn1_w_gate", "ffn1_w_up")
    grad_x, _, _, _, _, d_ffn1, sent1 = _ffn_bwd(
        "ffn1b", dx1, dx1_16, x0, h1, g1, u1, a1, wg1, wu1, wd1, ffn1_norm,
        ffn_comms(_chips_program([sums["w_in"]], (1, 2)), _chips_program([sums["w_in"]], (3,)), ffn1_names))
    (near,), (far,) = sent1["dact"], sent1["dwd"]
    from_chips["w_in"] = [(near, 0), (near, 1), (far, 0)]
    ffn_arrived(ffn1_names, sent1)

    out = {}
    for n in LARGE:
        (res,), _ = _adam_sharded("adam_" + n, [(as_stored(n, w[n]), as_stored(n, mom[n]), as_stored(n, var[n]),
                                                  sums[n], from_chips[n])])
        out[n] = [as_stored(n, a) for a in res]

    dlb = mix_small["hgrn_lb"]
    small_grads = dict(
        ffn1_norm=d_ffn1, mix_norm=mix_small["mix_norm"], attn_sinks=mix_small["attn_sinks"][:, :A_HEADS],
        hgrn_lower_bound=jnp.concatenate([dlb, -dlb], axis=0), hgrn_norm=mix_small["hgrn_norm"],
        ffn2_norm=d_ffn2, ple_norm=d_ple, final_norm=d_final)

    small_like = [w[n] for n in SMALL]
    packed = _pack_rows([small_grads[n].reshape(w[n].shape) for n in SMALL] + [loss_row[:, :1]])
    total = _all_reduce_small(packed)
    n_rows = total.shape[0]
    pack_w = _pack_rows(small_like + [jnp.zeros((1, 1), F32)])
    pack_m = _pack_rows([mom[n] for n in SMALL] + [jnp.zeros((1, 1), F32)])
    pack_v = _pack_rows([var[n] for n in SMALL] + [jnp.zeros((1, 1), F32)])
    deltas = _adam_small(pack_w, total, pack_m, pack_v)
    like = small_like + [jnp.zeros((1, 1), F32)]
    g_small = _unpack_rows(total, like)
    d_small, m_small, v_small = (_unpack_rows(a, like) for a in deltas)
    for k, n in enumerate(SMALL):
        out[n] = [g_small[k], d_small[k], m_small[k], v_small[k]]
    loss = g_small[-1].reshape(())

    return (loss, grad_x.reshape(x.shape),
            *[out[n][0] for n in WEIGHTS], *[out[n][1] for n in WEIGHTS],
            *[out[n][2] for n in WEIGHTS], *[out[n][3] for n in WEIGHTS])
```

```python
import functools

import jax
import jax.numpy as jnp
from jax import lax
from jax.experimental import pallas as pl
from jax.experimental.pallas import tpu as pltpu

F32 = jnp.float32
BF16 = jnp.bfloat16
MESH = pl.DeviceIdType.MESH

EPS = 1e-6
N_DEV = 8
A_HEADS = 16
A_HEAD_DIM = 64
A_WIDTH = 1024
A_KV_WIDTH = 256
ATT_BLOCK = 128
ROT_DIM = 16
ROPE_THETA = 500000.0
B_WIDTH = 1024
B_HEAD = 128
CHUNK = 64
SUB = 16
MAX_DECAY_EXP = 60.0
PLE_DIM = 256
LANES = 128

ADAM_LR = 0.001
ADAM_B1 = 0.9
ADAM_B2 = 0.999
ADAM_EPS = 1e-08
ADAM_WD = 0.01
ADAM_STEP = 10

VMEM_LIMIT = 56 * 1024 * 1024
EPILOGUE_ELEMS = 256 * 1024

NN = (((1,), (0,)), ((), ()))
NT = (((1,), (1,)), ((), ()))
TN = (((0,), (0,)), ((), ()))

OFF_QA = 0
OFF_KA = 1024
OFF_VA = 1280
OFF_QB = 1536
OFF_FB = 2560
OFF_IB = 3584
OFF_OG = 4608
OFF_GA = 5632


def _sig(v):
    return 1.0 / (1.0 + jnp.exp(-v))


def _tile(n, pref, mult):
    t = min(n, pref)
    t -= t % mult
    while n % t:
        t -= mult
    return t


def _params(sem):
    return pltpu.CompilerParams(dimension_semantics=sem, vmem_limit_bytes=VMEM_LIMIT)


HBM_SPEC = pl.BlockSpec(memory_space=pltpu.HBM)


class _Comm:
    def __init__(self, ins, out_shapes, n_sems, pre, post, mid=None, mid_at=1.0):
        self.ins, self.out_shapes, self.n_sems, self.pre, self.post = ins, out_shapes, n_sems, pre, post
        self.mid = mid if mid is not None else (lambda *args: None)
        self.mid_at = mid_at
        self.parts = [len(out_shapes)]

    def split(self, outs):
        res, at = [], 0
        for n in self.parts:
            res.append(list(outs[at:at + n]))
            at += n
        return res


def _merge(progs):
    spans, ins, shapes, sems = [], [], [], [0, 0, 0]
    for p in progs:
        spans.append((len(ins), len(shapes), tuple(sems)))
        ins += list(p.ins)
        shapes += list(p.out_shapes)
        sems = [a + b for a, b in zip(sems, p.n_sems)]

    def run(which):
        def go(cin, cout, send, recv, local, base):
            for p, (i0, o0, s0) in zip(progs, spans):
                getattr(p, which)(cin[i0:i0 + len(p.ins)], cout[o0:o0 + len(p.out_shapes)], send, recv, local,
                                  tuple(b + s for b, s in zip(base, s0)))
        return go

    merged = _Comm(ins, shapes, tuple(sems), run("pre"), run("post"), run("mid"), max(p.mid_at for p in progs))
    merged.parts = [len(p.out_shapes) for p in progs]
    return merged


def _call(name, body, grid, in_specs, out_specs, out_shape, scratch, args, comm=None, aliases=None):
    sem = ("arbitrary",) * len(grid)
    aliases = aliases or {}
    if comm is None:
        res = pl.pallas_call(body, name=name, grid=grid, in_specs=in_specs, out_specs=out_specs,
                             out_shape=out_shape, scratch_shapes=scratch, input_output_aliases=aliases,
                             compiler_params=_params(sem))(*args)
        return list(res), []
    n_in, n_out, n_scr = len(in_specs), len(out_specs), len(scratch)
    c_in, c_out = len(comm.ins), len(comm.out_shapes)
    steps = functools.reduce(lambda a, b: a * b, grid)
    mid_step = min(steps - 1, int(comm.mid_at * steps))

    def carrier(*refs):
        at = [0]

        def take(n):
            at[0] += n
            return refs[at[0] - n:at[0]]

        ins, cins, outs, couts, scr, sems = take(n_in), take(c_in), take(n_out), take(c_out), take(n_scr), take(3)
        ids = [pl.program_id(a) for a in range(len(grid))]
        step = functools.reduce(lambda acc, ig: acc * ig[1] + ig[0], zip(ids, grid), 0)

        @pl.when(step == 0)
        def _():
            comm.pre(cins, couts, *sems, (0, 0, 0))

        body(*ins, *outs, *scr)

        @pl.when(step == mid_step)
        def _():
            comm.mid(cins, couts, *sems, (0, 0, 0))

        @pl.when(step == steps - 1)
        def _():
            comm.post(cins, couts, *sems, (0, 0, 0))

    res = pl.pallas_call(
        carrier, name=name, grid=grid,
        in_specs=list(in_specs) + [HBM_SPEC] * c_in,
        out_specs=list(out_specs) + [HBM_SPEC] * c_out,
        out_shape=list(out_shape) + list(comm.out_shapes),
        scratch_shapes=list(scratch) + [pltpu.SemaphoreType.DMA((max(n, 1),)) for n in comm.n_sems],
        input_output_aliases=aliases,
        compiler_params=_params(sem),
    )(*args, *comm.ins)
    return list(res[:n_out]), list(res[n_out:])


def _mm(name, grid, ins, prods, acc_shapes, epi, outs, comm=None, aliases=None):
    n_in, n_out, nk = len(ins), len(outs), grid[2]
    tile_rows, tile_cols = acc_shapes[0]
    chunk = tile_rows
    while chunk * tile_cols > EPILOGUE_ELEMS and chunk % 32 == 0:
        chunk //= 2

    def body(*refs):
        in_refs = refs[:n_in]
        out_refs = refs[n_in:n_in + n_out]
        acc_refs = refs[n_in + n_out:]
        first_tile = jnp.logical_and(pl.program_id(0) == 0, pl.program_id(1) == 0)

        def product(lhs, rhs, dims):
            a = in_refs[lhs][...].astype(BF16)
            b = in_refs[rhs][...].astype(BF16)
            return lax.dot_general(a, b, dims, preferred_element_type=F32)

        def write(accs):
            totals = [None] * n_out
            for c in range(tile_rows // chunk):
                rows = slice(c * chunk, (c + 1) * chunk)
                vals = epi([a[rows] for a in accs], in_refs, rows)
                for idx, (o, v, spec) in enumerate(zip(out_refs, vals, outs)):
                    if spec[4]:
                        totals[idx] = v if totals[idx] is None else totals[idx] + v
                    else:
                        o[rows] = v.astype(o.dtype)
            for o, v, spec in zip(out_refs, totals, outs):
                if spec[4]:
                    @pl.when(first_tile)
                    def _():
                        o[...] = v.astype(o.dtype)

                    @pl.when(jnp.logical_not(first_tile))
                    def _():
                        o[...] += v.astype(o.dtype)

        if nk == 1:
            accs = [None] * len(acc_shapes)
            for lhs, rhs, acc, dims in prods:
                t = product(lhs, rhs, dims)
                accs[acc] = t if accs[acc] is None else accs[acc] + t
            write(accs)
        else:
            k = pl.program_id(2)

            @pl.when(k == 0)
            def _():
                for r in acc_refs:
                    r[...] = jnp.zeros(r.shape, F32)

            for lhs, rhs, acc, dims in prods:
                acc_refs[acc][...] += product(lhs, rhs, dims)

            @pl.when(k == nk - 1)
            def _():
                write(acc_refs)

    res, sent = _call(
        name, body, grid,
        [pl.BlockSpec(memory_space=pl.ANY) if b is None else pl.BlockSpec(b, im) for _, b, im in ins],
        [pl.BlockSpec(o[2], o[3]) for o in outs],
        [jax.ShapeDtypeStruct(o[0], o[1]) for o in outs],
        [] if nk == 1 else [pltpu.VMEM(s, F32) for s in acc_shapes],
        [a for a, _, _ in ins], comm, aliases)
    return res if comm is None else (res, sent)


def _rms_bwd(dh, xv, w, dres):
    r = lax.rsqrt(jnp.mean(xv * xv, axis=-1, keepdims=True) + EPS)
    g = dh * w
    dx = dres + r * (g - xv * (r * r) * jnp.mean(g * xv, axis=-1, keepdims=True))
    dw = jnp.sum(dh * xv * r, axis=0, keepdims=True)
    return dx, dw


def _rows(name, fn, ins, outs, tm):
    n_in = len(ins)
    rows = outs[0][0][0]

    def body(*refs):
        vals = fn(*[r[...] for r in refs[:n_in]])
        for o, v in zip(refs[n_in:], vals):
            o[...] = v.astype(o.dtype)

    in_specs = []
    for a, tiled in ins:
        if tiled:
            in_specs.append(pl.BlockSpec((tm, a.shape[1]), lambda i: (i, 0)))
        else:
            in_specs.append(pl.BlockSpec(a.shape, lambda i, nd=a.ndim: (0,) * nd))
    return pl.pallas_call(
        body,
        name=name,
        grid=(rows // tm,),
        in_specs=in_specs,
        out_specs=[pl.BlockSpec((tm, s[1]), lambda i: (i, 0)) for s, _ in outs],
        out_shape=[jax.ShapeDtypeStruct(s, d) for s, d in outs],
        compiler_params=_params(("arbitrary",)),
    )(*[a for a, _ in ins])


def _rmsnorm(name, x, w):
    def fn(xv, wv):
        r = lax.rsqrt(jnp.mean(xv * xv, axis=-1, keepdims=True) + EPS)
        return (xv * r * wv,)

    return _rows(name, fn, [(x, True), (w, False)], [(x.shape, BF16)], _tile(x.shape[0], 512, 16))[0]


def _mmc(*args, comm=None, aliases=None):
    if comm is None:
        return _mm(*args, aliases=aliases), []
    return _mm(*args, comm=comm, aliases=aliases)


def _carried(comms, key, avail):
    c = comms.get(key) if comms else None
    return c(avail) if callable(c) else c


def _ffn_up(name, h, wg, wu, comm=None):
    t, d = h.shape
    n = wg.shape[1]
    tm = _tile(t, 1024, 16)

    def epi_up(accs, refs, rows):
        g, u = accs
        return [g, u, g * _sig(g) * u]

    hid = ((N_DEV, t, n), BF16, (None, tm, n), lambda i, j, k: (j, i, 0), False)
    return _mmc(
        name + "_up", (t // tm, N_DEV, 1),
        [(h, (tm, d), lambda i, j, k: (i, 0)),
         (wg, (None, n, d), lambda i, j, k: (j, 0, 0)),
         (wu, (None, n, d), lambda i, j, k: (j, 0, 0))],
        [(0, 1, 0, NT), (0, 2, 1, NT)], [(tm, n)] * 2, epi_up, [hid] * 3, comm=comm)


def _ffn_down(name, x, act, wd, comm=None):
    t, d = x.shape
    n = wd.shape[1]
    tm2 = _tile(t, 1024, 8)
    tn = _tile(d, 1024, 128)

    def epi_down(accs, refs, rows):
        return [refs[2][rows] + 0.5 * accs[0]]

    (x_out,), sent = _mmc(
        name + "_down", (t // tm2, d // tn, N_DEV),
        [(act, (None, tm2, n), lambda i, j, k: (k, i, 0)),
         (wd, (None, n, tn), lambda i, j, k: (k, 0, j)),
         (x, (tm2, tn), lambda i, j, k: (i, j))],
        [(0, 1, 0, NN)], [(tm2, tn)], epi_down,
        [((t, d), F32, (tm2, tn), lambda i, j, k: (i, j), False)], comm=comm)
    return x_out, sent


def _ffn_bwd(name, dxo, dxo16, x, h, gate, up, act, wg, wu, wd, norm_w, comms=None):
    t, d = x.shape
    n = wg.shape[1]
    tm = _tile(t, 1024, 16)
    sent = {}
    avail = dict(sent=sent)

    def epi_act(accs, refs, rows):
        da = 0.5 * accs[0]
        g = refs[2][rows].astype(F32)
        u = refs[3][rows].astype(F32)
        s = _sig(g)
        return [da * u * (s * (1.0 + g * (1.0 - s))), da * g * s]

    hid_in = lambda a: (a, (None, tm, n), lambda i, j, k: (j, i, 0))
    hid_out = ((N_DEV, t, n), BF16, (None, tm, n), lambda i, j, k: (j, i, 0), False)
    (dgate, dup), sent["dact"] = _mmc(
        name + "_dact", (t // tm, N_DEV, 1),
        [(dxo16, (tm, d), lambda i, j, k: (i, 0)),
         (wd, (None, n, d), lambda i, j, k: (j, 0, 0)),
         hid_in(gate), hid_in(up)],
        [(0, 1, 0, NT)], [(tm, n)], epi_act, [hid_out] * 2, comm=_carried(comms, "dact", avail))

    tk = _tile(t, 1024, 16)
    w_out = ((N_DEV, n, d), BF16, (None, n, d), lambda i, j, k: (i, 0, 0), False)

    def weight_grad(key, acts, stream, scale):
        (dw,), sent[key] = _mmc(
            name + "_" + key, (N_DEV, 1, t // tk),
            [(acts, (None, tk, n), lambda i, j, k: (i, k, 0)), (stream, (tk, d), lambda i, j, k: (k, 0))],
            [(0, 1, 0, TN)], [(n, d)], lambda accs, refs, rows: [scale * accs[0]], [w_out],
            comm=_carried(comms, key, avail))
        avail[key] = dw
        return dw

    dwd = weight_grad("dwd", act, dxo16, 0.5)
    dwg = weight_grad("dwg", dgate, h, 1.0)
    dwu = weight_grad("dwu", dup, h, 1.0)

    tm3 = _tile(t // 2, 512, 16)
    half = t // 2 // tm3

    def epi_dx(accs, refs, rows):
        dx, dw = _rms_bwd(accs[0], refs[4][rows], refs[5][...], refs[6][rows])
        return [dx, dx, dw]

    def dx_half(key, first, filled):
        at = 0 if first else half
        row = lambda i, j, k: (i + at, 0)
        hid = lambda i, j, k: (k, i + at, 0)
        ins = [(dgate, (None, tm3, n), hid), (wg, (None, n, d), lambda i, j, k: (k, 0, 0)),
               (dup, (None, tm3, n), hid), (wu, (None, n, d), lambda i, j, k: (k, 0, 0)),
               (x, (tm3, d), row), (norm_w, (1, d), lambda i, j, k: (0, 0)), (dxo, (tm3, d), row)]
        ins += [(a, None, None) for a in filled]
        res, sent[key] = _mmc(
            name + "_" + key, (half, 1, N_DEV), ins,
            [(0, 1, 0, NN), (2, 3, 0, NN)], [(tm3, d)], epi_dx,
            [((t, d), F32, (tm3, d), row, False), ((t, d), BF16, (tm3, d), row, False),
             ((1, d), F32, (1, d), lambda i, j, k: (0, 0), True)],
            comm=_carried(comms, key, avail), aliases={7 + k: k for k in range(len(filled))})
        return res

    dx_a, dx16_a, dnorm_a = dx_half("dxa", True, [])
    dx, dx16, dnorm_b = dx_half("dxb", False, [dx_a, dx16_a])
    return dx, dx16, dwg, dwu, dwd, dnorm_a + dnorm_b, sent


def _rope(tv, cos, s1, s2):
    return tv * cos + pltpu.roll(tv, LANES - 8, 1) * s1 + pltpu.roll(tv, 8, 1) * s2


def _rope_bwd(dr, cos, s1, s2):
    return dr * cos + pltpu.roll(dr * s1, 8, 1) + pltpu.roll(dr * s2, LANES - 8, 1)


def _rope_tables(positions):
    inv_freq = jnp.power(jnp.float32(ROPE_THETA), -jnp.arange(0, ROT_DIM, 2, dtype=F32) / ROT_DIM)
    ang = positions.reshape(-1).astype(F32)[:, None] * inv_freq
    cos, sin = jnp.cos(ang), jnp.sin(ang)
    t = ang.shape[0]
    pad = A_HEAD_DIM - ROT_DIM
    cos_t = jnp.concatenate([cos, cos, jnp.ones((t, pad), F32)], axis=1)
    s1_t = jnp.concatenate([-sin, jnp.zeros((t, pad + 8), F32)], axis=1)
    s2_t = jnp.concatenate([jnp.zeros((t, 8), F32), sin, jnp.zeros((t, pad), F32)], axis=1)
    return tuple(jnp.tile(v, (1, 2)) for v in (cos_t, s1_t, s2_t))


def _att_common(i, k_p, k_c, v_p, v_c, tabs_p, tabs_c):
    kcat = jnp.concatenate([k_p, k_c], axis=0)
    vcat = jnp.concatenate([v_p, v_c], axis=0)
    tabs_k = [jnp.concatenate([a, b], axis=0) for a, b in zip(tabs_p, tabs_c)]
    kvar, vvar = [], []
    for ks in range(2):
        sl = slice(ks * LANES, (ks + 1) * LANES)
        kr = _rope(kcat[:, sl], *tabs_k)
        kvar.append([kr.astype(BF16), pltpu.roll(kr, 64, 1).astype(BF16)])
        vvar.append([vcat[:, sl].astype(BF16), pltpu.roll(vcat[:, sl], 64, 1).astype(BF16)])
    qi = lax.broadcasted_iota(jnp.int32, (ATT_BLOCK, 2 * ATT_BLOCK), 0)
    kj = lax.broadcasted_iota(jnp.int32, (ATT_BLOCK, 2 * ATT_BLOCK), 1)
    dist = qi + ATT_BLOCK - kj
    allowed = (dist >= 0) & (dist < ATT_BLOCK) & ((i > 0) | (kj >= ATT_BLOCK))
    lane = lax.broadcasted_iota(jnp.int32, (1, LANES), 1)
    halves = [lane < 64, lane >= 64]
    return kvar, vvar, allowed, halves


def _att_head_probs(qm, kslab, allowed, sink):
    s = lax.dot_general(qm, kslab, NT, preferred_element_type=F32)
    s = jnp.where(allowed, s, -jnp.inf)
    m = jnp.maximum(jnp.max(s, axis=-1, keepdims=True), sink)
    e = jnp.exp(s - m)
    es = jnp.exp(sink - m)
    den = jnp.sum(e, axis=-1, keepdims=True) + es
    inv = 1.0 / den
    return e * inv, es * inv


def _head_slots(h):
    pair, half = h // 2, h % 2
    kvh = h // 4
    return pair, half, kvh // 2, int(kvh % 2 != half)


def _attention_fwd(proj, tables, sinks, comm=None):
    t = proj.shape[0]
    nb = t // ATT_BLOCK
    b = ATT_BLOCK
    scale = A_HEAD_DIM ** -0.5

    def body(sink_ref, q_ref, kp_ref, kc_ref, vp_ref, vc_ref,
             cp_ref, ap_ref, bp_ref, cc_ref, ac_ref, bc_ref, o_ref):
        i = pl.program_id(0)
        tabs_p = (cp_ref[...], ap_ref[...], bp_ref[...])
        tabs_c = (cc_ref[...], ac_ref[...], bc_ref[...])
        kvar, vvar, allowed, halves = _att_common(
            i, kp_ref[...], kc_ref[...], vp_ref[...], vc_ref[...], tabs_p, tabs_c)
        for pair in range(A_HEADS // 2):
            sl = slice(pair * LANES, (pair + 1) * LANES)
            q2 = _rope(q_ref[:, sl], *tabs_c) * scale
            acc = jnp.zeros((b, LANES), F32)
            for half in range(2):
                h = 2 * pair + half
                _, _, ks, var = _head_slots(h)
                qm = jnp.where(halves[half], q2, 0.0).astype(BF16)
                p, _ = _att_head_probs(qm, kvar[ks][var], allowed, sink_ref[h])
                o = lax.dot_general(p.astype(BF16), vvar[ks][var], NN, preferred_element_type=F32)
                acc = acc + jnp.where(halves[half], o, 0.0)
            o_ref[:, sl] = acc.astype(o_ref.dtype)

    cur = lambda col: (lambda i: (i, col))
    prev = lambda col: (lambda i: (jnp.maximum(i - 1, 0), col))
    tab = [pl.BlockSpec((b, LANES), prev(0))] * 3 + [pl.BlockSpec((b, LANES), cur(0))] * 3
    (out,), sent = _call(
        "attention_fwd", body, (nb,),
        [pl.BlockSpec(memory_space=pltpu.SMEM),
         pl.BlockSpec((b, A_WIDTH), cur(0)),
         pl.BlockSpec((b, A_KV_WIDTH), prev(OFF_KA // A_KV_WIDTH)),
         pl.BlockSpec((b, A_KV_WIDTH), cur(OFF_KA // A_KV_WIDTH)),
         pl.BlockSpec((b, A_KV_WIDTH), prev(OFF_VA // A_KV_WIDTH)),
         pl.BlockSpec((b, A_KV_WIDTH), cur(OFF_VA // A_KV_WIDTH))] + tab,
        [pl.BlockSpec((b, A_WIDTH), cur(0))],
        [jax.ShapeDtypeStruct((t, A_WIDTH), BF16)], [],
        [sinks, proj, proj, proj, proj, proj, *tables, *tables], comm)
    return out, sent


def _attention_bwd(proj, tables, sinks, d_out, comm=None):
    t = proj.shape[0]
    nb = t // ATT_BLOCK
    b = ATT_BLOCK
    scale = A_HEAD_DIM ** -0.5

    def body(sink_ref, q_ref, kp_ref, kc_ref, vp_ref, vc_ref,
             cp_ref, ap_ref, bp_ref, cc_ref, ac_ref, bc_ref, do_ref,
             dq_ref, dk_ref, dv_ref, dsink_ref, dk_carry, dv_carry):
        step = pl.program_id(0)
        i = nb - 1 - step

        @pl.when(step == 0)
        def _():
            dk_carry[...] = jnp.zeros(dk_carry.shape, F32)
            dv_carry[...] = jnp.zeros(dv_carry.shape, F32)
            dsink_ref[...] = jnp.zeros(dsink_ref.shape, F32)

        tabs_p = (cp_ref[...], ap_ref[...], bp_ref[...])
        tabs_c = (cc_ref[...], ac_ref[...], bc_ref[...])
        kvar, vvar, allowed, halves = _att_common(
            i, kp_ref[...], kc_ref[...], vp_ref[...], vc_ref[...], tabs_p, tabs_c)
        lane = lax.broadcasted_iota(jnp.int32, (1, LANES), 1)
        dk_slab = [jnp.zeros((2 * b, LANES), F32) for _ in range(2)]
        dv_slab = [jnp.zeros((2 * b, LANES), F32) for _ in range(2)]
        dsink = jnp.zeros((1, LANES), F32)
        for pair in range(A_HEADS // 2):
            sl = slice(pair * LANES, (pair + 1) * LANES)
            q2 = _rope(q_ref[:, sl], *tabs_c) * scale
            do2 = do_ref[:, sl].astype(F32)
            dq2 = jnp.zeros((b, LANES), F32)
            for half in range(2):
                h = 2 * pair + half
                _, _, ks, var = _head_slots(h)
                qm = jnp.where(halves[half], q2, 0.0).astype(BF16)
                dom = jnp.where(halves[half], do2, 0.0).astype(BF16)
                p, p_sink = _att_head_probs(qm, kvar[ks][var], allowed, sink_ref[h])
                dp = lax.dot_general(dom, vvar[ks][var], NT, preferred_element_type=F32)
                dsum = jnp.sum(p * dp, axis=-1, keepdims=True)
                ds = (p * (dp - dsum)).astype(BF16)
                dsink = dsink + jnp.where(lane == h, -jnp.sum(p_sink * dsum), 0.0)
                dqh = lax.dot_general(ds, kvar[ks][var], NN, preferred_element_type=F32)
                dq2 = dq2 + jnp.where(halves[half], dqh, 0.0)
                dkx = lax.dot_general(ds, qm, TN, preferred_element_type=F32)
                dvx = lax.dot_general(p.astype(BF16), dom, TN, preferred_element_type=F32)
                if var:
                    dkx = pltpu.roll(dkx, 64, 1)
                    dvx = pltpu.roll(dvx, 64, 1)
                dk_slab[ks] = dk_slab[ks] + dkx
                dv_slab[ks] = dv_slab[ks] + dvx
            dq_ref[:, sl] = _rope_bwd(dq2 * scale, *tabs_c).astype(dq_ref.dtype)
        for ks in range(2):
            sl = slice(ks * LANES, (ks + 1) * LANES)
            dk_cur = dk_slab[ks][b:] + dk_carry[:, sl]
            dk_ref[:, sl] = _rope_bwd(dk_cur, *tabs_c).astype(dk_ref.dtype)
            dv_ref[:, sl] = (dv_slab[ks][b:] + dv_carry[:, sl]).astype(dv_ref.dtype)
            dk_carry[:, sl] = dk_slab[ks][:b]
            dv_carry[:, sl] = dv_slab[ks][:b]
        dsink_ref[...] += dsink

    cur = lambda col: (lambda s: (nb - 1 - s, col))
    prev = lambda col: (lambda s: (jnp.maximum(nb - 2 - s, 0), col))
    tab = [pl.BlockSpec((b, LANES), prev(0))] * 3 + [pl.BlockSpec((b, LANES), cur(0))] * 3
    return _call(
        "attention_bwd", body, (nb,),
        [pl.BlockSpec(memory_space=pltpu.SMEM),
         pl.BlockSpec((b, A_WIDTH), cur(0)),
         pl.BlockSpec((b, A_KV_WIDTH), prev(OFF_KA // A_KV_WIDTH)),
         pl.BlockSpec((b, A_KV_WIDTH), cur(OFF_KA // A_KV_WIDTH)),
         pl.BlockSpec((b, A_KV_WIDTH), prev(OFF_VA // A_KV_WIDTH)),
         pl.BlockSpec((b, A_KV_WIDTH), cur(OFF_VA // A_KV_WIDTH))] + tab
        + [pl.BlockSpec((b, A_WIDTH), cur(0))],
        [pl.BlockSpec((b, A_WIDTH), cur(0)),
         pl.BlockSpec((b, A_KV_WIDTH), cur(0)),
         pl.BlockSpec((b, A_KV_WIDTH), cur(0)),
         pl.BlockSpec((1, LANES), lambda s: (0, 0))],
        [jax.ShapeDtypeStruct((t, A_WIDTH), BF16),
         jax.ShapeDtypeStruct((t, A_KV_WIDTH), BF16),
         jax.ShapeDtypeStruct((t, A_KV_WIDTH), BF16),
         jax.ShapeDtypeStruct((1, LANES), F32)],
        [pltpu.VMEM((b, A_KV_WIDTH), F32), pltpu.VMEM((b, A_KV_WIDTH), F32)],
        [sinks, proj, proj, proj, proj, proj, *tables, *tables, d_out], comm)


HGRN_HEADS_PER_STEP = 4
HGRN_W = HGRN_HEADS_PER_STEP * B_HEAD
HGRN_ROWS = 2 * CHUNK


def _hgrn_gates(qp, fp, hlb):
    lb = _sig(hlb[0:1] - hlb[1:2])
    sg = _sig(fp)
    big_f = lb + (1.0 - lb) * sg
    kk = (1.0 - lb) * (1.0 - sg)
    sq = _sig(qp)
    return lb, sg, big_f, kk, sq


def _tri(lower):
    r = lax.broadcasted_iota(jnp.int32, (CHUNK, CHUNK), 0)
    c = lax.broadcasted_iota(jnp.int32, (CHUNK, CHUNK), 1)
    return (r >= c) if lower else (r <= c)


def _hgrn_factors(q_ref, k_ref, b_ref, sl, r0):
    out = []
    whole = slice(r0, r0 + CHUNK)
    bh = b_ref[whole, sl]
    for i in range(CHUNK // SUB):
        rows = slice(r0 + i * SUB, r0 + (i + 1) * SUB)
        b0 = b_ref[r0 + i * SUB - 1:r0 + i * SUB, sl] if i else jnp.zeros((1, B_HEAD), F32)
        eq = jnp.exp(b_ref[rows, sl] - b0)
        ek = jnp.exp(jnp.minimum(b0 - bh, MAX_DECAY_EXP))
        out.append((q_ref[rows, sl] * eq, k_ref[whole, sl] * ek, eq, ek))
    return out


def _hgrn_scores(factors):
    rows = [lax.dot_general(qd.astype(BF16), kx.astype(BF16), NT, preferred_element_type=F32)
            for qd, kx, _, _ in factors]
    return jnp.where(_tri(True), jnp.concatenate(rows, axis=0), 0.0)


def _hgrn_specs(steps, reverse):
    w = HGRN_W
    block = (lambda c: steps - 1 - c) if reverse else (lambda c: c)
    col = lambda off: pl.BlockSpec((HGRN_ROWS, w), lambda hh, c: (block(c), off // w + hh))
    return block, col


def _chunk_cumsum(log_f, r0, lower):
    tri = _tri(lower).astype(F32)
    return lax.dot_general(tri, log_f[r0:r0 + CHUNK], NN, precision=lax.Precision.HIGHEST,
                           preferred_element_type=F32)


def _hgrn_fwd(proj, hlb, wn, comm=None):
    t = proj.shape[0]
    nc = t // CHUNK
    steps = t // HGRN_ROWS
    w = HGRN_W
    block, col = _hgrn_specs(steps, False)

    def body(q_ref, f_ref, v_ref, og_ref, hlb_ref, wn_ref, ob_ref, oraw_ref, st_ref,
             state, qs, ks, bs):
        c = pl.program_id(1)

        @pl.when(c == 0)
        def _():
            state[...] = jnp.zeros(state.shape, F32)

        qp = q_ref[...]
        lb, sg, big_f, kk, sq = _hgrn_gates(qp, f_ref[...], hlb_ref[...])
        qs[...] = qp * sq
        ks[...] = kk
        log_f = jnp.log(big_f)
        for sub in range(HGRN_ROWS // CHUNK):
            r0 = sub * CHUNK
            rows = slice(r0, r0 + CHUNK)
            bs[rows] = _chunk_cumsum(log_f, r0, True)
            for h in range(HGRN_HEADS_PER_STEP):
                sl = slice(h * B_HEAD, (h + 1) * B_HEAD)
                bh = bs[rows, sl]
                qh = qs[rows, sl]
                kh = ks[rows, sl]
                vh = v_ref[rows, sl].astype(BF16)
                s0 = state[h]
                st_ref[sub, h] = s0.astype(st_ref.dtype)
                a = _hgrn_scores(_hgrn_factors(qs, ks, bs, sl, r0))
                o = lax.dot_general(a.astype(BF16), vh, NN, preferred_element_type=F32)
                o = o + lax.dot_general((qh * jnp.exp(bh)).astype(BF16), s0.astype(BF16), NT,
                                        preferred_element_type=F32)
                bl = bs[r0 + CHUNK - 1:r0 + CHUNK, sl]
                khat = (kh * jnp.exp(bl - bh)).astype(BF16)
                state[h] = s0 * jnp.exp(bl) + lax.dot_general(vh, khat, TN, preferred_element_type=F32)
                oraw_ref[rows, sl] = o
                r = lax.rsqrt(jnp.mean(o * o, axis=-1, keepdims=True) + EPS)
                og = og_ref[rows, sl]
                ob_ref[rows, sl] = (o * r * wn_ref[:, sl] * (og * _sig(og))).astype(ob_ref.dtype)

    vec = lambda rows: pl.BlockSpec((rows, w), lambda hh, c: (0, hh))
    tile = pl.BlockSpec((HGRN_ROWS, w), lambda hh, c: (c, hh))
    return _call(
        "hgrn_fwd", body, (B_WIDTH // w, steps),
        [col(OFF_QB), col(OFF_FB), col(OFF_IB), col(OFF_OG), vec(2), vec(1)],
        [tile, tile,
         pl.BlockSpec((HGRN_ROWS // CHUNK, HGRN_HEADS_PER_STEP, B_HEAD, B_HEAD), lambda hh, c: (c, hh, 0, 0))],
        [jax.ShapeDtypeStruct((t, B_WIDTH), BF16),
         jax.ShapeDtypeStruct((t, B_WIDTH), F32),
         jax.ShapeDtypeStruct((nc, B_WIDTH // B_HEAD, B_HEAD, B_HEAD), BF16)],
        [pltpu.VMEM((HGRN_HEADS_PER_STEP, B_HEAD, B_HEAD), F32)] + [pltpu.VMEM((HGRN_ROWS, w), F32)] * 3,
        [proj, proj, proj, proj, hlb, wn], comm)


def _hgrn_bwd(proj, hlb, wn, o_raw, states, d_out, comm=None):
    t = proj.shape[0]
    steps = t // HGRN_ROWS
    w = HGRN_W
    block, col = _hgrn_specs(steps, True)

    def body(q_ref, f_ref, v_ref, og_ref, hlb_ref, wn_ref, oraw_ref, st_ref, dout_ref,
             dq_ref, df_ref, dv_ref, dog_ref, dlb_ref, dwn_ref,
             dstate, qs, ks, bs, dbs, dks, dqs, dgs):
        c = pl.program_id(1)

        @pl.when(c == 0)
        def _():
            dstate[...] = jnp.zeros(dstate.shape, F32)
            dlb_ref[...] = jnp.zeros(dlb_ref.shape, F32)
            dwn_ref[...] = jnp.zeros(dwn_ref.shape, F32)

        qp = q_ref[...]
        lb, sg, big_f, kk, sq = _hgrn_gates(qp, f_ref[...], hlb_ref[...])
        qs[...] = qp * sq
        ks[...] = kk
        log_f = jnp.log(big_f)
        last_row = lax.broadcasted_iota(jnp.int32, (CHUNK, 1), 0) == CHUNK - 1
        for sub in reversed(range(HGRN_ROWS // CHUNK)):
            r0 = sub * CHUNK
            rows = slice(r0, r0 + CHUNK)
            bs[rows] = _chunk_cumsum(log_f, r0, True)
            for h in range(HGRN_HEADS_PER_STEP):
                sl = slice(h * B_HEAD, (h + 1) * B_HEAD)
                bh = bs[rows, sl]
                qh = qs[rows, sl]
                kh = ks[rows, sl]
                vh = v_ref[rows, sl].astype(BF16)
                s0 = st_ref[sub, h]
                ds1 = dstate[h]
                o = oraw_ref[rows, sl]
                og = og_ref[rows, sl]
                wnh = wn_ref[:, sl]
                sog = _sig(og)
                r = lax.rsqrt(jnp.mean(o * o, axis=-1, keepdims=True) + EPS)
                dob = dout_ref[rows, sl].astype(F32)
                don = dob * (og * sog)
                dog_ref[rows, sl] = (dob * (o * r * wnh) * (sog * (1.0 + og * (1.0 - sog)))).astype(dog_ref.dtype)
                dwn_ref[:, sl] += jnp.sum(don * o * r, axis=0, keepdims=True)
                gp = don * wnh
                do = r * (gp - o * (r * r) * jnp.mean(gp * o, axis=-1, keepdims=True))
                do16 = do.astype(BF16)
                factors = _hgrn_factors(qs, ks, bs, sl, r0)
                a = _hgrn_scores(factors)
                da = jnp.where(_tri(True), lax.dot_general(do16, vh, NT, preferred_element_type=F32), 0.0)
                dv = lax.dot_general(a.astype(BF16), do16, TN, preferred_element_type=F32)
                dq_rows = []
                dk = jnp.zeros((CHUNK, B_HEAD), F32)
                for i, (qd, kx, eq, ek) in enumerate(factors):
                    da_i = da[i * SUB:(i + 1) * SUB]
                    dq_rows.append(lax.dot_general(da_i, kx, NN, precision=lax.Precision.HIGHEST,
                                                   preferred_element_type=F32) * eq)
                    dk = dk + lax.dot_general(da_i, qd, TN, precision=lax.Precision.HIGHEST,
                                              preferred_element_type=F32) * ek
                eb = jnp.exp(bh)
                bl = bs[r0 + CHUNK - 1:r0 + CHUNK, sl]
                ebl = jnp.exp(bl)
                ekl = jnp.exp(bl - bh)
                qe = (qh * eb).astype(BF16)
                khat = (kh * ekl).astype(BF16)
                ds1_16 = ds1.astype(BF16)
                dq = jnp.concatenate(dq_rows, axis=0) + eb * lax.dot_general(
                    do16, s0, NN, preferred_element_type=F32)
                dv = dv + lax.dot_general(khat, ds1_16, NT, preferred_element_type=F32)
                dk_state = ekl * lax.dot_general(vh, ds1_16, NN, preferred_element_type=F32)
                dk = dk + dk_state
                db_last = (jnp.sum(kh * dk_state, axis=0, keepdims=True)
                           + ebl * jnp.sum(s0.astype(F32) * ds1, axis=0, keepdims=True))
                dbs[rows, sl] = qh * dq - kh * dk + jnp.where(last_row, db_last, 0.0)
                dks[rows, sl] = dk
                dqs[rows, sl] = dq
                dstate[h] = ds1 * ebl + lax.dot_general(do16, qe, TN, preferred_element_type=F32)
                dv_ref[rows, sl] = dv.astype(dv_ref.dtype)
            dgs[rows] = _chunk_cumsum(dbs, r0, False)
        dq_ref[...] = (dqs[...] * (sq * (1.0 + qp * (1.0 - sq)))).astype(dq_ref.dtype)
        df_minus_dk = dgs[...] / big_f - dks[...]
        df_ref[...] = ((1.0 - lb) * df_minus_dk * sg * (1.0 - sg)).astype(df_ref.dtype)
        dlb_ref[...] += jnp.sum((1.0 - sg) * df_minus_dk, axis=0, keepdims=True) * (lb * (1.0 - lb))

    vec = lambda rows: pl.BlockSpec((rows, w), lambda hh, c: (0, hh))
    blk = pl.BlockSpec((HGRN_ROWS, w), lambda hh, c: (block(c), hh))
    act = jax.ShapeDtypeStruct((t, B_WIDTH), BF16)
    row = jax.ShapeDtypeStruct((1, B_WIDTH), F32)
    return _call(
        "hgrn_bwd", body, (B_WIDTH // w, steps),
        [col(OFF_QB), col(OFF_FB), col(OFF_IB), col(OFF_OG), vec(2), vec(1), blk,
         pl.BlockSpec((HGRN_ROWS // CHUNK, HGRN_HEADS_PER_STEP, B_HEAD, B_HEAD),
                      lambda hh, c: (block(c), hh, 0, 0)),
         blk],
        [blk, blk, blk, blk, vec(1), vec(1)],
        [act, act, act, act, row, row],
        [pltpu.VMEM((HGRN_HEADS_PER_STEP, B_HEAD, B_HEAD), F32)] + [pltpu.VMEM((HGRN_ROWS, w), F32)] * 7,
        [proj, proj, proj, proj, hlb, wn, o_raw, states, d_out], comm)


def _mixer_fwd(x1, mix_norm, w_in, sinks, hlb, wn, weights, tables, comms):
    t, d = x1.shape
    n_in = w_in[0].shape[0]
    nb = d // N_DEV
    sent = {}
    h2 = _rmsnorm("mix_norm", x1, mix_norm)
    tm = _tile(t, 1024, 16)
    tn = _tile(n_in, 512, 128)
    (proj,), sent["proj"] = _mmc(
        "mix_proj", (t // tm, n_in // tn, 1),
        [(h2, (tm, d // 2), lambda i, j, k: (i, 0)), (w_in[0], (tn, d // 2), lambda i, j, k: (j, 0)),
         (h2, (tm, d // 2), lambda i, j, k: (i, 1)), (w_in[1], (tn, d // 2), lambda i, j, k: (j, 0))],
        [(0, 1, 0, NT), (2, 3, 0, NT)], [(tm, tn)], lambda accs, refs, rows: accs,
        [((t, n_in), F32, (tm, tn), lambda i, j, k: (i, j), False)], comm=comms.get("proj"))
    out_a, sent["attention"] = _attention_fwd(proj, tables, sinks, comms.get("attention"))
    (out_b, o_raw, states), sent["hgrn"] = _hgrn_fwd(proj, hlb, wn, comms.get("hgrn"))
    wua, wub, w_out = weights(sent)

    def epi_merge(accs, refs, rows):
        ua, ub = accs
        return [_sig(refs[4][rows]) * ua + _sig(refs[5][rows]) * ub, ua, ub]

    wide = ((t, d), BF16, (tm, nb), lambda i, j, k: (i, j), False)
    (merged, ua, ub), sent["merge"] = _mmc(
        "mix_merge", (t // tm, N_DEV, 1),
        [(out_a, (tm, A_WIDTH), lambda i, j, k: (i, 0)),
         (wua, (None, A_WIDTH, nb), lambda i, j, k: (j, 0, 0)),
         (out_b, (tm, B_WIDTH), lambda i, j, k: (i, 0)),
         (wub, (None, B_WIDTH, nb), lambda i, j, k: (j, 0, 0)),
         (proj, (tm, nb), lambda i, j, k: (i, OFF_GA // nb + j)),
         (proj, (tm, nb), lambda i, j, k: (i, (OFF_GA + d) // nb + j))],
        [(0, 1, 0, NN), (2, 3, 1, NN)], [(tm, nb)] * 2, epi_merge, [wide] * 3, comm=comms.get("merge"))

    tn2 = _tile(d, 1024, 128)
    (x2,) = _mm(
        "mix_out", (t // tm, d // tn2, 1),
        [(merged, (tm, d), lambda i, j, k: (i, 0)), (w_out, (d, tn2), lambda i, j, k: (0, j)),
         (x1, (tm, tn2), lambda i, j, k: (i, j))],
        [(0, 1, 0, NN)], [(tm, tn2)], lambda accs, refs, rows: [refs[2][rows] + accs[0]],
        [((t, d), F32, (tm, tn2), lambda i, j, k: (i, j), False)])
    saved = dict(h2=h2, proj=proj, out_a=out_a, out_b=out_b, o_raw=o_raw, states=states,
                 merged=merged, ua=ua, ub=ub)
    return x2, saved, sent


def _mixer_bwd(dx2, dx2_16, x1, mix_norm, w_in, sinks, hlb, wn, wua, wub, w_out, tables, sv, comms):
    t, d = x1.shape
    sent = {}
    avail = dict(sent=sent)
    n_in = w_in[0].shape[0]
    nb = d // N_DEV
    proj = sv["proj"]
    tm = _tile(t, 1024, 16)
    tn = _tile(d, 512, 128)

    def epi_gate(accs, refs, rows):
        dm = accs[0]
        sa = _sig(refs[2][rows])
        sb = _sig(refs[3][rows])
        ua = refs[4][rows].astype(F32)
        ub = refs[5][rows].astype(F32)
        return [dm * sa, dm * sb, dm * ua * sa * (1.0 - sa), dm * ub * sb * (1.0 - sb)]

    wide = ((t, d), BF16, (tm, tn), lambda i, j, k: (i, j), False)
    dua, dub, dga, dgb = _mm(
        "mix_dmerge", (t // tm, d // tn, 1),
        [(dx2_16, (tm, d), lambda i, j, k: (i, 0)),
         (w_out, (tn, d), lambda i, j, k: (j, 0)),
         (proj, (tm, tn), lambda i, j, k: (i, OFF_GA // tn + j)),
         (proj, (tm, tn), lambda i, j, k: (i, (OFF_GA + d) // tn + j)),
         (sv["ua"], (tm, tn), lambda i, j, k: (i, j)),
         (sv["ub"], (tm, tn), lambda i, j, k: (i, j))],
        [(0, 1, 0, NT)], [(tm, tn)], epi_gate, [wide] * 4)

    tk = _tile(t, 1024, 16)
    two = _tile(d, 1024, 128)
    (dw_out,) = _mm(
        "mix_dwout", (d // two, 1, t // tk),
        [(sv["merged"], (tk, two), lambda i, j, k: (k, i)), (dx2_16, (tk, d), lambda i, j, k: (k, 0))],
        [(0, 1, 0, TN)], [(two, d)], lambda accs, refs, rows: accs,
        [((d, d), BF16, (two, d), lambda i, j, k: (i, 0), False)])

    w_spec = lambda rows: ((N_DEV, rows, nb), BF16, (None, rows, nb), lambda i, j, k: (j, 0, 0), False)
    tk2 = _tile(t, 2048, 16)
    dwua, dwub = _mm(
        "mix_dwup", (1, N_DEV, t // tk2),
        [(sv["out_a"], (tk2, A_WIDTH), lambda i, j, k: (k, 0)),
         (dua, (tk2, nb), lambda i, j, k: (k, j)),
         (sv["out_b"], (tk2, B_WIDTH), lambda i, j, k: (k, 0)),
         (dub, (tk2, nb), lambda i, j, k: (k, j))],
        [(0, 1, 0, TN), (2, 3, 1, TN)], [(A_WIDTH, nb), (B_WIDTH, nb)], lambda accs, refs, rows: accs,
        [w_spec(A_WIDTH), w_spec(B_WIDTH)])

    d_out_a, d_out_b = _mm(
        "mix_dup", (t // tm, 1, N_DEV),
        [(dua, (tm, nb), lambda i, j, k: (i, k)),
         (wua, (None, A_WIDTH, nb), lambda i, j, k: (k, 0, 0)),
         (dub, (tm, nb), lambda i, j, k: (i, k)),
         (wub, (None, B_WIDTH, nb), lambda i, j, k: (k, 0, 0))],
        [(0, 1, 0, NT), (2, 3, 1, NT)], [(tm, A_WIDTH), (tm, B_WIDTH)], lambda accs, refs, rows: accs,
        [((t, A_WIDTH), BF16, (tm, A_WIDTH), lambda i, j, k: (i, 0), False),
         ((t, B_WIDTH), BF16, (tm, B_WIDTH), lambda i, j, k: (i, 0), False)])

    avail.update(w_out=dw_out.reshape(N_DEV, d // N_DEV, d), w_up_a=dwua, w_up_b=dwub)
    (dq_a, dk_a, dv_a, dsinks), sent["attention"] = _attention_bwd(
        proj, tables, sinks, d_out_a, _carried(comms, "attention", avail))
    (dq_b, df_b, di_b, dog_b, dlb, dwn), sent["hgrn"] = _hgrn_bwd(
        proj, hlb, wn, sv["o_raw"], sv["states"], d_out_b, _carried(comms, "hgrn", avail))
    dproj = jnp.concatenate([dq_a, dk_a, dv_a, dq_b, df_b, di_b, dog_b, dga, dgb], axis=1)

    tdm = _tile(d, 1024, 128)
    twn = _tile(n_in, 2432, 128)
    tkw = _tile(t, 512, 16)
    (dw_in,) = _mm(
        "mix_dwin", (n_in // twn, d // tdm, t // tkw),
        [(dproj, (tkw, twn), lambda i, j, k: (k, i)), (sv["h2"], (tkw, tdm), lambda i, j, k: (k, j))],
        [(0, 1, 0, TN)], [(twn, tdm)], lambda accs, refs, rows: accs,
        [((n_in, d), BF16, (twn, tdm), lambda i, j, k: (i, j), False)])

    avail.update(w_in=dw_in.reshape(N_DEV, n_in // N_DEV, d))
    tm3 = _tile(t, 512, 16)
    tpn = _tile(n_in, 512, 128)

    def epi_dx(accs, refs, rows):
        dh = jnp.concatenate(accs, axis=1)
        dx, dw = _rms_bwd(dh, refs[3][rows], refs[4][...], refs[5][rows])
        return [dx, dx, dw]

    (dx1, dx1_16, dmix), sent["dx"] = _mmc(
        "mix_dx", (t // tm3, 1, n_in // tpn),
        [(dproj, (tm3, tpn), lambda i, j, k: (i, k)),
         (w_in[0], (tpn, d // 2), lambda i, j, k: (k, 0)), (w_in[1], (tpn, d // 2), lambda i, j, k: (k, 0)),
         (x1, (tm3, d), lambda i, j, k: (i, 0)), (mix_norm, (1, d), lambda i, j, k: (0, 0)),
         (dx2, (tm3, d), lambda i, j, k: (i, 0))],
        [(0, 1, 0, NN), (0, 2, 1, NN)], [(tm3, d // 2)] * 2, epi_dx,
        [((t, d), F32, (tm3, d), lambda i, j, k: (i, 0), False),
         ((t, d), BF16, (tm3, d), lambda i, j, k: (i, 0), False),
         ((1, d), F32, (1, d), lambda i, j, k: (0, 0), True)],
        comm=_carried(comms, "dx", avail))
    small = dict(mix_norm=dmix, attn_sinks=dsinks, hgrn_lb=dlb, hgrn_norm=dwn)
    return dx1, dx1_16, small, avail


def _ple_fwd(x3, ple_norm, p16, wpg, wpp):
    t, d = x3.shape
    nb = d // N_DEV
    h4 = _rmsnorm("ple_norm", x3, ple_norm)
    tm = _tile(t, 1024, 16)

    def epi(accs, refs, rows):
        zg, pp = accs
        return [refs[4][rows] + _sig(zg) * pp, zg, pp]

    blk = lambda dt: ((t, d), dt, (tm, nb), lambda i, j, k: (i, j), False)
    x4, zg, pp = _mm(
        "ple_fwd", (t // tm, N_DEV, 1),
        [(h4, (tm, d), lambda i, j, k: (i, 0)), (wpg, (d, nb), lambda i, j, k: (0, j)),
         (p16, (tm, PLE_DIM), lambda i, j, k: (i, 0)),
         (wpp, (None, PLE_DIM, nb), lambda i, j, k: (j, 0, 0)),
         (x3, (tm, nb), lambda i, j, k: (i, j))],
        [(0, 1, 0, NN), (2, 3, 1, NN)], [(tm, nb)] * 2, epi, [blk(F32), blk(BF16), blk(BF16)])
    return x4, dict(h4=h4, zg=zg, pp=pp)


def _ple_bwd(dx4, x3, ple_norm, p16, wpg, sv):
    t, d = x3.shape
    nb = d // N_DEV

    def gate_grads(dx, zg, pp):
        s = _sig(zg.astype(F32))
        return dx * pp.astype(F32) * s * (1.0 - s), dx * s

    dzg, dpp = _rows("ple_dgate", gate_grads, [(dx4, True), (sv["zg"], True), (sv["pp"], True)],
                     [((t, d), BF16), ((t, d), BF16)], _tile(t, 512, 16))
    (dwpp,) = _mm(
        "ple_dwproj", (1, N_DEV, 1),
        [(p16, (t, PLE_DIM), lambda i, j, k: (0, 0)), (dpp, (t, nb), lambda i, j, k: (0, j))],
        [(0, 1, 0, TN)], [(PLE_DIM, nb)], lambda accs, refs, rows: accs,
        [((N_DEV, PLE_DIM, nb), BF16, (None, PLE_DIM, nb), lambda i, j, k: (j, 0, 0), False)])
    tk = _tile(t, 1024, 16)
    tn = _tile(d, 1024, 128)
    (dwpg,) = _mm(
        "ple_dwgate", (d // tn, 1, t // tk),
        [(sv["h4"], (tk, tn), lambda i, j, k: (k, i)), (dzg, (tk, d), lambda i, j, k: (k, 0))],
        [(0, 1, 0, TN)], [(tn, d)], lambda accs, refs, rows: accs,
        [((d, d), BF16, (tn, d), lambda i, j, k: (i, 0), False)])
    tm3 = _tile(t, 256, 16)

    def epi_dx(accs, refs, rows):
        dx, dw = _rms_bwd(accs[0], refs[2][rows], refs[3][...], refs[4][rows])
        return [dx, dx, dw]

    dx3, dx3_16, dnorm = _mm(
        "ple_dx", (t // tm3, 1, 1),
        [(dzg, (tm3, d), lambda i, j, k: (i, 0)), (wpg, (d, d), lambda i, j, k: (0, 0)),
         (x3, (tm3, d), lambda i, j, k: (i, 0)), (ple_norm, (1, d), lambda i, j, k: (0, 0)),
         (dx4, (tm3, d), lambda i, j, k: (i, 0))],
        [(0, 1, 0, NT)], [(tm3, d)], epi_dx,
        [((t, d), F32, (tm3, d), lambda i, j, k: (i, 0), False),
         ((t, d), BF16, (tm3, d), lambda i, j, k: (i, 0), False),
         ((1, d), F32, (1, d), lambda i, j, k: (0, 0), True)])
    return dx3, dx3_16, dwpp, dwpg, dnorm


def _loss_head(x4, final_norm, target):
    t, d = x4.shape
    tm = _tile(t, 256, 8)

    def body(x_ref, w_ref, tgt_ref, dx_ref, dw_ref, loss_ref):
        i = pl.program_id(0)
        xv = x_ref[...]
        wv = w_ref[...]
        r = lax.rsqrt(jnp.mean(xv * xv, axis=-1, keepdims=True) + EPS)
        err = xv * r * wv - tgt_ref[...]
        part = 0.5 * jnp.sum(jnp.mean(err * err, axis=-1, keepdims=True), axis=0, keepdims=True)
        dx, dw = _rms_bwd(err * (1.0 / d), xv, wv, jnp.zeros_like(xv))
        dx_ref[...] = dx

        @pl.when(i == 0)
        def _():
            dw_ref[...] = dw
            loss_ref[...] = jnp.broadcast_to(part, loss_ref.shape)

        @pl.when(i > 0)
        def _():
            dw_ref[...] += dw
            loss_ref[...] += jnp.broadcast_to(part, loss_ref.shape)

    return pl.pallas_call(
        body,
        name="loss_head",
        grid=(t // tm,),
        in_specs=[pl.BlockSpec((tm, d), lambda i: (i, 0)), pl.BlockSpec((1, d), lambda i: (0, 0)),
                  pl.BlockSpec((tm, d), lambda i: (i, 0))],
        out_specs=[pl.BlockSpec((tm, d), lambda i: (i, 0)), pl.BlockSpec((1, d), lambda i: (0, 0)),
                   pl.BlockSpec((1, LANES), lambda i: (0, 0))],
        out_shape=[jax.ShapeDtypeStruct((t, d), F32), jax.ShapeDtypeStruct((1, d), F32),
                   jax.ShapeDtypeStruct((1, LANES), F32)],
        compiler_params=_params(("arbitrary",)),
    )(x4, final_norm, target)


RELATIONS = ((0, 0), (1, 0), (0, 1), (1, 1))


def _place():
    return lax.axis_index("x"), lax.axis_index("y"), lax.axis_index("c")


def _flip(v, bit):
    return 1 - v if bit else v


def _gather_program(shards, mid_at):
    n = len(shards)

    def copies(ins, outs, send_sems, recv_sems, local_sems, base):
        x, y, c = _place()
        sibling = (x, y, 1 - c)

        def block(a, rel, core):
            return outs[a].at[4 * _flip(x, rel[0]) + 2 * _flip(y, rel[1]) + core]

        def copy(a, slot, rel, core, to, src=None):
            dst = block(a, rel, core)
            return pltpu.make_async_remote_copy(
                src_ref=dst if src is None else src, dst_ref=dst,
                send_sem=send_sems.at[base[0] + 7 * a + slot], recv_sem=recv_sems.at[base[1] + 7 * a + slot],
                device_id=to, device_id_type=MESH)

        own, first = [], []
        for a in range(n):
            own.append(pltpu.make_async_copy(ins[a], block(a, RELATIONS[0], c), local_sems.at[base[2] + a]))
            first.append(copy(a, 0, RELATIONS[0], c, sibling, src=ins[a]))
            for r in (1, 2, 3):
                rel = RELATIONS[r]
                first.append(copy(a, r, RELATIONS[0], c, (_flip(x, rel[0]), _flip(y, rel[1]), c), src=ins[a]))
        return own, first, copy, sibling, c

    def pre(*args):
        own, first, _, _, _ = copies(*args)
        for cp in own + first:
            cp.start()

    def passed_on(copy, sibling, c):
        return [copy(a, 3 + r, RELATIONS[r], c, sibling) for r in (1, 2, 3) for a in range(n)]

    def mid(*args):
        _, _, copy, sibling, c = copies(*args)
        for r in (1, 2, 3):
            for a in range(n):
                copy(a, r, RELATIONS[r], c, sibling).wait_recv()
        for fwd in passed_on(copy, sibling, c):
            fwd.start()

    def post(*args):
        own, first, copy, sibling, c = copies(*args)
        for a in range(n):
            copy(a, 0, RELATIONS[0], 1 - c, sibling).wait_recv()
            for r in (1, 2, 3):
                copy(a, 3 + r, RELATIONS[r], 1 - c, sibling).wait_recv()
        for cp in first + passed_on(copy, sibling, c):
            cp.wait_send()
        for mine in own:
            mine.wait()

    return _Comm(list(shards), [jax.ShapeDtypeStruct((N_DEV,) + s.shape, s.dtype) for s in shards],
                 (7 * n, 7 * n, n), pre, post, mid, mid_at)


def _exchange_program(arrays, n_slots, source, target, slot):
    n = len(arrays)

    def copies(ins, outs, send_sems, recv_sems, local_sems, base):
        place = _place()
        res = []
        for a in range(n):
            for r in range(n_slots):
                src = source(r, place)
                if src is None:
                    continue
                res.append(pltpu.make_async_remote_copy(
                    src_ref=ins[a].at[src], dst_ref=outs[a].at[slot(r)],
                    send_sem=send_sems.at[base[0] + n_slots * a + r],
                    recv_sem=recv_sems.at[base[1] + n_slots * a + r],
                    device_id=target(r, place), device_id_type=MESH))
        return res

    def pre(*args):
        for cp in copies(*args):
            cp.start()

    def post(*args):
        for cp in copies(*args):
            cp.wait()

    n_out = len({slot(r) for r in range(n_slots) if source(r, (0, 0, 0)) is not None})
    return _Comm(list(arrays), [jax.ShapeDtypeStruct((n_out,) + g.shape[1:], g.dtype) for g in arrays],
                 (n_slots * n, n_slots * n, 0), pre, post)


def _sibling_program(grads):
    def source(r, place):
        x, y, c = place
        return 4 * _flip(x, RELATIONS[r][0]) + 2 * _flip(y, RELATIONS[r][1]) + (1 - c)

    return _exchange_program(grads, 4, source, lambda r, p: (p[0], p[1], 1 - p[2]), lambda r: r)


def _chips_program(sums, relations=(1, 2, 3)):
    def target(r, place):
        x, y, c = place
        return (_flip(x, RELATIONS[r][0]), _flip(y, RELATIONS[r][1]), c)

    return _exchange_program(sums, 4, lambda r, p: r if r in relations else None, target,
                             lambda r: relations.index(r) if r in relations else 0)


def _standalone(name, comm):
    return _call(name, lambda: None, (1,), [], [], [], [], [], comm)[1]


def _pair_sum(name, grad, from_sibling, block_ids):
    _, rows, cols = grad.shape
    tr = _tile(rows, 512, 16)

    def body(ids_ref, mine_ref, sib_ref, o_ref):
        o_ref[...] = (mine_ref[...].astype(F32) + sib_ref[...].astype(F32)).astype(o_ref.dtype)

    return pl.pallas_call(
        body,
        name=name,
        grid_spec=pltpu.PrefetchScalarGridSpec(
            num_scalar_prefetch=1,
            grid=(4, rows // tr),
            in_specs=[pl.BlockSpec((None, tr, cols), lambda r, i, ids: (ids[r], i, 0)),
                      pl.BlockSpec((None, tr, cols), lambda r, i, ids: (r, i, 0))],
            out_specs=pl.BlockSpec((None, tr, cols), lambda r, i, ids: (r, i, 0))),
        out_shape=jax.ShapeDtypeStruct((4, rows, cols), BF16),
        compiler_params=_params(("arbitrary", "arbitrary")),
    )(block_ids, grad, from_sibling)


def _adam(w, g, m, v):
    m = ADAM_B1 * m + (1.0 - ADAM_B1) * g
    v = ADAM_B2 * v + (1.0 - ADAM_B2) * (g * g)
    m_hat = m / (1.0 - ADAM_B1 ** ADAM_STEP)
    v_hat = v / (1.0 - ADAM_B2 ** ADAM_STEP)
    delta = -ADAM_LR * (m_hat / (jnp.sqrt(v_hat) + ADAM_EPS) + ADAM_WD * w)
    return delta, m, v


def _adam_sharded(name, items, comm=None):
    _, rows, cols = items[0][0].shape
    n = len(items)
    tr = _tile(rows, max(64, 256 // n), 8)

    def body(*refs):
        for k in range(n):
            w_ref, m_ref, v_ref, s_ref, r1_ref, r2_ref, r3_ref = refs[7 * k:7 * k + 7]
            g_ref, d_ref, nm_ref, nv_ref = refs[7 * n + 4 * k:7 * n + 4 * k + 4]
            g = ((s_ref[...].astype(F32) + r1_ref[...].astype(F32)) + r2_ref[...].astype(F32)) + r3_ref[...].astype(F32)
            delta, nm, nv = _adam(w_ref[...], g, m_ref[...], v_ref[...])
            g_ref[...] = g
            d_ref[...] = delta
            nm_ref[...] = nm
            nv_ref[...] = nv

    slot = lambda s: pl.BlockSpec((None, tr, cols), lambda i: (s, i, 0))
    flat = slot(0)
    args, specs = [], []
    for w, m, v, sums, from_chips in items:
        args += [w, m, v, sums] + [a for a, _ in from_chips]
        specs += [flat, flat, flat, slot(0)] + [slot(s) for _, s in from_chips]
    res, sent = _call(
        name, body, (rows // tr,), specs, [flat] * (4 * n),
        [jax.ShapeDtypeStruct((1, rows, cols), F32)] * (4 * n), [], args, comm)
    return [res[4 * k:4 * k + 4] for k in range(n)], sent


def _all_reduce_small(part):
    rows = part.shape[0]

    def body(x_ref, o_ref, gathered, send_sems, recv_sems):
        x, y, c = _place()
        me = 4 * x + 2 * y + c
        gathered[me] = x_ref[...]
        copies = []
        for k in range(1, N_DEV):
            bits = (k >> 2 & 1, k >> 1 & 1, k & 1)
            peer = (_flip(x, bits[0]), _flip(y, bits[1]), _flip(c, bits[2]))
            copies.append(pltpu.make_async_remote_copy(
                src_ref=x_ref, dst_ref=gathered.at[me],
                send_sem=send_sems.at[k - 1], recv_sem=recv_sems.at[k - 1],
                device_id=peer, device_id_type=MESH))
        for cp in copies:
            cp.start()
        for cp in copies:
            cp.wait()
        total = gathered[0]
        for b in range(1, N_DEV):
            total = total + gathered[b]
        o_ref[...] = total

    return pl.pallas_call(
        body,
        name="small_all_reduce",
        in_specs=[pl.BlockSpec(memory_space=pltpu.VMEM)],
        out_specs=pl.BlockSpec(memory_space=pltpu.VMEM),
        out_shape=jax.ShapeDtypeStruct((rows, LANES), F32),
        scratch_shapes=[pltpu.VMEM((N_DEV, rows, LANES), F32),
                        pltpu.SemaphoreType.DMA((N_DEV - 1,)), pltpu.SemaphoreType.DMA((N_DEV - 1,))],
    )(part)


def _adam_small(w, g, m, v):
    def fn(wv, gv, mv, vv):
        return _adam(wv, gv, mv, vv)

    shape = (w.shape, F32)
    return _rows("adam_small", fn, [(w, True), (g, True), (m, True), (v, True)], [shape] * 3, w.shape[0])


SMALL = ("ffn1_norm", "mix_norm", "attn_sinks", "hgrn_lower_bound", "hgrn_norm", "ffn2_norm", "ple_norm",
         "final_norm")
LARGE = ("ffn1_w_gate", "ffn1_w_up", "ffn1_w_down", "w_in", "w_up_a", "w_up_b", "w_out",
         "ffn2_w_gate", "ffn2_w_up", "ffn2_w_down", "ple_w_gate", "ple_w_proj")
TRANSPOSED = ("ffn1_w_gate", "ffn1_w_up", "ffn2_w_gate", "ffn2_w_up", "w_in")
WEIGHTS = ("ffn1_norm", "ffn1_w_gate", "ffn1_w_up", "ffn1_w_down", "mix_norm", "w_in", "attn_sinks",
           "hgrn_lower_bound", "hgrn_norm", "w_up_a", "w_up_b", "w_out", "ffn2_norm", "ffn2_w_gate",
           "ffn2_w_up", "ffn2_w_down", "ple_norm", "ple_w_gate", "ple_w_proj", "final_norm")


def _pack_rows(arrays):
    rows = []
    for a in arrays:
        flat = a.reshape(-1).astype(F32)
        pad = -flat.shape[0] % LANES
        rows.append(jnp.pad(flat, (0, pad)).reshape(-1, LANES))
    packed = jnp.concatenate(rows, axis=0)
    return jnp.pad(packed, ((0, -packed.shape[0] % 8), (0, 0)))


def _unpack_rows(packed, like):
    out, at = [], 0
    for a in like:
        size = a.size
        n_rows = -(-size // LANES)
        out.append(packed[at:at + n_rows].reshape(-1)[:size].reshape(a.shape))
        at += n_rows
    return out


def kernel(x, p, positions, ffn1_norm, ffn1_w_gate, ffn1_w_up, ffn1_w_down, mix_norm, w_in, attn_sinks, hgrn_lower_bound, hgrn_norm, w_up_a, w_up_b, w_out, ffn2_norm, ffn2_w_gate, ffn2_w_up, ffn2_w_down, ple_norm, ple_w_gate, ple_w_proj, final_norm, loss_target, m_ffn1_norm, m_ffn1_w_gate, m_ffn1_w_up, m_ffn1_w_down, m_mix_norm, m_w_in, m_attn_sinks, m_hgrn_lower_bound, m_hgrn_norm, m_w_up_a, m_w_up_b, m_w_out, m_ffn2_norm, m_ffn2_w_gate, m_ffn2_w_up, m_ffn2_w_down, m_ple_norm, m_ple_w_gate, m_ple_w_proj, m_final_norm, v_ffn1_norm, v_ffn1_w_gate, v_ffn1_w_up, v_ffn1_w_down, v_mix_norm, v_w_in, v_attn_sinks, v_hgrn_lower_bound, v_hgrn_norm, v_w_up_a, v_w_up_b, v_w_out, v_ffn2_norm, v_ffn2_w_gate, v_ffn2_w_up, v_ffn2_w_down, v_ple_norm, v_ple_w_gate, v_ple_w_proj, v_final_norm):
    given = dict(locals())
    w = {n: given[n] for n in WEIGHTS}
    mom = {n: given["m_" + n] for n in WEIGHTS}
    var = {n: given["v_" + n] for n in WEIGHTS}
    t, d = x.shape[1], x.shape[2]

    x0, p16, target = x[0], p[0, 0].astype(BF16), loss_target[0]
    tables = _rope_tables(positions)
    sinks = attn_sinks.reshape(-1)
    final_w = final_norm.reshape(1, d)
    as_stored = lambda n, a: jnp.swapaxes(a, 1, 2) if n in TRANSPOSED else a
    shard = {n: as_stored(n, w[n])[0].astype(BF16) for n in LARGE}
    shard["w_in_left"], shard["w_in_right"] = shard["w_in"][:, :d // 2], shard["w_in"][:, d // 2:]
    xi, yi, ci = _place()
    block_ids = jnp.stack([4 * _flip(xi, rx) + 2 * _flip(yi, ry) + ci for rx, ry in RELATIONS]).astype(jnp.int32)

    def gather(*names, mid_at=0.85):
        return _gather_program([shard[n] for n in names], mid_at)

    wg1, wu1 = _standalone("gather_ffn1", gather("ffn1_w_gate", "ffn1_w_up"))
    h1 = _rmsnorm("ffn1_norm", x0, ffn1_norm)
    (g1, u1, a1), (wd1, win_left) = _ffn_up("ffn1", h1, wg1, wu1,
                                            gather("ffn1_w_down", "w_in_left", mid_at=1.0))
    x1, (win_right,) = _ffn_down("ffn1", x0, a1, wd1, gather("w_in_right"))
    win = (win_left.reshape(-1, d // 2), win_right.reshape(-1, d // 2))
    mix_args = (mix_norm, win, sinks, hgrn_lower_bound, hgrn_norm)

    def mixer_weights(sent):
        wua, wub, wout_g = sent["proj"]
        return wua, wub, wout_g.reshape(d, d)

    x2, mix_saved, got_w = _mixer_fwd(
        x1, *mix_args, mixer_weights, tables,
        dict(proj=gather("w_up_a", "w_up_b", "w_out"), attention=gather("ffn2_w_up", mid_at=1.0),
             hgrn=gather("ffn2_w_gate"), merge=gather("ple_w_gate", "ple_w_proj")))
    wua, wub, wout = mixer_weights(got_w)
    (wu2,), (wg2,) = got_w["attention"], got_w["hgrn"]
    wpg_g, wpp = got_w["merge"]
    wpg = wpg_g.reshape(d, d)
    h3 = _rmsnorm("ffn2_norm", x2, ffn2_norm)
    (g2, u2, a2), (wd2,) = _ffn_up("ffn2", h3, wg2, wu2, gather("ffn2_w_down"))
    x3, _ = _ffn_down("ffn2", x2, a2, wd2)
    x4, ple_saved = _ple_fwd(x3, ple_norm, p16, wpg, wpp)
    dx4, d_final, loss_row = _loss_head(x4, final_w, target)

    sums, from_chips = {}, {}

    def pair_sums(names, grads, from_sibling):
        for n, g, s in zip(names, grads, from_sibling):
            sums[n] = _pair_sum("pair_sum_" + n, g, s, block_ids)

    def to_chips(names, grads, from_sibling):
        pair_sums(names, grads, from_sibling)
        return _chips_program([sums[n] for n in names])

    def arrived(names, results):
        for n, r in zip(names, results):
            from_chips[n] = [(r, 0), (r, 1), (r, 2)]

    def ffn_comms(first, second, names):
        dn, gn, un = ((n,) for n in names)
        return dict(
            dact=first, dwd=second,
            dwg=lambda av: _sibling_program([av["dwd"]]),
            dwu=lambda av: _merge([to_chips(dn, [av["dwd"]], av["sent"]["dwg"]), _sibling_program([av["dwg"]])]),
            dxa=lambda av: _merge([to_chips(gn, [av["dwg"]], av["sent"]["dwu"][1:]), _sibling_program([av["dwu"]])]),
            dxb=lambda av: to_chips(un, [av["dwu"]], av["sent"]["dxa"][1:]))

    def ffn_arrived(names, sent):
        arrived(names, [sent["dwu"][0], sent["dxa"][0], sent["dxb"][0]])

    dx3, dx3_16, dwpp, dwpg, d_ple = _ple_bwd(dx4, x3, ple_norm, p16, wpg, ple_saved)
    ple_names = ("ple_w_proj", "ple_w_gate")
    ple_grads = [dwpp, dwpg.reshape(N_DEV, d // N_DEV, d)]
    ffn2_names = ("ffn2_w_down", "ffn2_w_gate", "ffn2_w_up")
    dx2, dx2_16, _, _, _, d_ffn2, sent2 = _ffn_bwd(
        "ffn2b", dx3, dx3_16, x2, h3, g2, u2, a2, wg2, wu2, wd2, ffn2_norm,
        ffn_comms(_sibling_program(ple_grads), lambda av: to_chips(ple_names, ple_grads, av["sent"]["dact"]),
                  ffn2_names))
    arrived(ple_names, sent2["dwd"])
    ffn_arrived(ffn2_names, sent2)
    mix3 = ("w_out", "w_up_a", "w_up_b")
    dx1, dx1_16, mix_small, mix_made = _mixer_bwd(
        dx2, dx2_16, x1, *mix_args, wua, wub, wout, tables, mix_saved,
        dict(attention=lambda av: _sibling_program([av[n] for n in mix3]),
             hgrn=lambda av: to_chips(mix3, [av[n] for n in mix3], av["sent"]["attention"]),
             dx=lambda av: _sibling_program([av["w_in"]])))
    sent_mix = mix_made["sent"]
    arrived(mix3, sent_mix["hgrn"])
    pair_sums(("w_in",), [mix_made["w_in"]], sent_mix["dx"])
    ffn1_names = ("ffn1_w_down", "ffn1_w_gate", "ffn1_w_up")
    grad_x, _, _, _, _, d_ffn1, sent1 = _ffn_bwd(
        "ffn1b", dx1, dx1_16, x0, h1, g1, u1, a1, wg1, wu1, wd1, ffn1_norm,
        ffn_comms(_chips_program([sums["w_in"]], (1, 2)), _chips_program([sums["w_in"]], (3,)), ffn1_names))
    (near,), (far,) = sent1["dact"], sent1["dwd"]
    from_chips["w_in"] = [(near, 0), (near, 1), (far, 0)]
    ffn_arrived(ffn1_names, sent1)

    out = {}
    for n in LARGE:
        (res,), _ = _adam_sharded("adam_" + n, [(as_stored(n, w[n]), as_stored(n, mom[n]), as_stored(n, var[n]),
                                                  sums[n], from_chips[n])])
        out[n] = [as_stored(n, a) for a in res]

    dlb = mix_small["hgrn_lb"]
    small_grads = dict(
        ffn1_norm=d_ffn1, mix_norm=mix_small["mix_norm"], attn_sinks=mix_small["attn_sinks"][:, :A_HEADS],
        hgrn_lower_bound=jnp.concatenate([dlb, -dlb], axis=0), hgrn_norm=mix_small["hgrn_norm"],
        ffn2_norm=d_ffn2, ple_norm=d_ple, final_norm=d_final)

    small_like = [w[n] for n in SMALL]
    packed = _pack_rows([small_grads[n].reshape(w[n].shape) for n in SMALL] + [loss_row[:, :1]])
    total = _all_reduce_small(packed)
    n_rows = total.shape[0]
    pack_w = _pack_rows(small_like + [jnp.zeros((1, 1), F32)])
    pack_m = _pack_rows([mom[n] for n in SMALL] + [jnp.zeros((1, 1), F32)])
    pack_v = _pack_rows([var[n] for n in SMALL] + [jnp.zeros((1, 1), F32)])
    deltas = _adam_small(pack_w, total, pack_m, pack_v)
    like = small_like + [jnp.zeros((1, 1), F32)]
    g_small = _unpack_rows(total, like)
    d_small, m_small, v_small = (_unpack_rows(a, like) for a in deltas)
    for k, n in enumerate(SMALL):
        out[n] = [g_small[k], d_small[k], m_small[k], v_small[k]]
    loss = g_small[-1].reshape(())

    return (loss, grad_x.reshape(x.shape),
            *[out[n][0] for n in WEIGHTS], *[out[n][1] for n in WEIGHTS],
            *[out[n][2] for n in WEIGHTS], *[out[n][3] for n in WEIGHTS])
```

```python
import functools

import jax
import jax.numpy as jnp
from jax import lax
from jax.experimental import pallas as pl
from jax.experimental.pallas import tpu as pltpu

F32 = jnp.float32
BF16 = jnp.bfloat16
MESH = pl.DeviceIdType.MESH

EPS = 1e-6
N_DEV = 8
A_HEADS = 16
A_HEAD_DIM = 64
A_WIDTH = 1024
A_KV_WIDTH = 256
ATT_BLOCK = 128
ROT_DIM = 16
ROPE_THETA = 500000.0
B_WIDTH = 1024
B_HEAD = 128
CHUNK = 64
SUB = 16
MAX_DECAY_EXP = 60.0
PLE_DIM = 256
LANES = 128

ADAM_LR = 0.001
ADAM_B1 = 0.9
ADAM_B2 = 0.999
ADAM_EPS = 1e-08
ADAM_WD = 0.01
ADAM_STEP = 10

VMEM_LIMIT = 56 * 1024 * 1024
EPILOGUE_ELEMS = 256 * 1024

NN = (((1,), (0,)), ((), ()))
NT = (((1,), (1,)), ((), ()))
TN = (((0,), (0,)), ((), ()))

OFF_QA = 0
OFF_KA = 1024
OFF_VA = 1280
OFF_QB = 1536
OFF_FB = 2560
OFF_IB = 3584
OFF_OG = 4608
OFF_GA = 5632


def _sig(v):
    return 1.0 / (1.0 + jnp.exp(-v))


def _tile(n, pref, mult):
    t = min(n, pref)
    t -= t % mult
    while n % t:
        t -= mult
    return t


def _params(sem):
    return pltpu.CompilerParams(dimension_semantics=sem, vmem_limit_bytes=VMEM_LIMIT)


HBM_SPEC = pl.BlockSpec(memory_space=pltpu.HBM)


class _Comm:
    def __init__(self, ins, out_shapes, n_sems, pre, post, mid=None, mid_at=1.0):
        self.ins, self.out_shapes, self.n_sems, self.pre, self.post = ins, out_shapes, n_sems, pre, post
        self.mid = mid if mid is not None else (lambda *args: None)
        self.mid_at = mid_at
        self.parts = [len(out_shapes)]

    def split(self, outs):
        res, at = [], 0
        for n in self.parts:
            res.append(list(outs[at:at + n]))
            at += n
        return res


def _merge(progs):
    spans, ins, shapes, sems = [], [], [], [0, 0, 0]
    for p in progs:
        spans.append((len(ins), len(shapes), tuple(sems)))
        ins += list(p.ins)
        shapes += list(p.out_shapes)
        sems = [a + b for a, b in zip(sems, p.n_sems)]

    def run(which):
        def go(cin, cout, send, recv, local, base):
            for p, (i0, o0, s0) in zip(progs, spans):
                getattr(p, which)(cin[i0:i0 + len(p.ins)], cout[o0:o0 + len(p.out_shapes)], send, recv, local,
                                  tuple(b + s for b, s in zip(base, s0)))
        return go

    merged = _Comm(ins, shapes, tuple(sems), run("pre"), run("post"), run("mid"), max(p.mid_at for p in progs))
    merged.parts = [len(p.out_shapes) for p in progs]
    return merged


def _call(name, body, grid, in_specs, out_specs, out_shape, scratch, args, comm=None, aliases=None):
    sem = ("arbitrary",) * len(grid)
    aliases = aliases or {}
    if comm is None:
        res = pl.pallas_call(body, name=name, grid=grid, in_specs=in_specs, out_specs=out_specs,
                             out_shape=out_shape, scratch_shapes=scratch, input_output_aliases=aliases,
                             compiler_params=_params(sem))(*args)
        return list(res), []
    n_in, n_out, n_scr = len(in_specs), len(out_specs), len(scratch)
    c_in, c_out = len(comm.ins), len(comm.out_shapes)
    steps = functools.reduce(lambda a, b: a * b, grid)
    mid_step = min(steps - 1, int(comm.mid_at * steps))

    def carrier(*refs):
        at = [0]

        def take(n):
            at[0] += n
            return refs[at[0] - n:at[0]]

        ins, cins, outs, couts, scr, sems = take(n_in), take(c_in), take(n_out), take(c_out), take(n_scr), take(3)
        ids = [pl.program_id(a) for a in range(len(grid))]
        step = functools.reduce(lambda acc, ig: acc * ig[1] + ig[0], zip(ids, grid), 0)

        @pl.when(step == 0)
        def _():
            comm.pre(cins, couts, *sems, (0, 0, 0))

        body(*ins, *outs, *scr)

        @pl.when(step == mid_step)
        def _():
            comm.mid(cins, couts, *sems, (0, 0, 0))

        @pl.when(step == steps - 1)
        def _():
            comm.post(cins, couts, *sems, (0, 0, 0))

    res = pl.pallas_call(
        carrier, name=name, grid=grid,
        in_specs=list(in_specs) + [HBM_SPEC] * c_in,
        out_specs=list(out_specs) + [HBM_SPEC] * c_out,
        out_shape=list(out_shape) + list(comm.out_shapes),
        scratch_shapes=list(scratch) + [pltpu.SemaphoreType.DMA((max(n, 1),)) for n in comm.n_sems],
        input_output_aliases=aliases,
        compiler_params=_params(sem),
    )(*args, *comm.ins)
    return list(res[:n_out]), list(res[n_out:])


def _mm(name, grid, ins, prods, acc_shapes, epi, outs, comm=None, aliases=None):
    n_in, n_out, nk = len(ins), len(outs), grid[2]
    tile_rows, tile_cols = acc_shapes[0]
    chunk = tile_rows
    while chunk * tile_cols > EPILOGUE_ELEMS and chunk % 32 == 0:
        chunk //= 2

    def body(*refs):
        in_refs = refs[:n_in]
        out_refs = refs[n_in:n_in + n_out]
        acc_refs = refs[n_in + n_out:]
        first_tile = jnp.logical_and(pl.program_id(0) == 0, pl.program_id(1) == 0)

        def product(lhs, rhs, dims):
            a = in_refs[lhs][...].astype(BF16)
            b = in_refs[rhs][...].astype(BF16)
            return lax.dot_general(a, b, dims, preferred_element_type=F32)

        def write(accs):
            totals = [None] * n_out
            for c in range(tile_rows // chunk):
                rows = slice(c * chunk, (c + 1) * chunk)
                vals = epi([a[rows] for a in accs], in_refs, rows)
                for idx, (o, v, spec) in enumerate(zip(out_refs, vals, outs)):
                    if spec[4]:
                        totals[idx] = v if totals[idx] is None else totals[idx] + v
                    else:
                        o[rows] = v.astype(o.dtype)
            for o, v, spec in zip(out_refs, totals, outs):
                if spec[4]:
                    @pl.when(first_tile)
                    def _():
                        o[...] = v.astype(o.dtype)

                    @pl.when(jnp.logical_not(first_tile))
                    def _():
                        o[...] += v.astype(o.dtype)

        if nk == 1:
            accs = [None] * len(acc_shapes)
            for lhs, rhs, acc, dims in prods:
                t = product(lhs, rhs, dims)
                accs[acc] = t if accs[acc] is None else accs[acc] + t
            write(accs)
        else:
            k = pl.program_id(2)

            @pl.when(k == 0)
            def _():
                for r in acc_refs:
                    r[...] = jnp.zeros(r.shape, F32)

            for lhs, rhs, acc, dims in prods:
                acc_refs[acc][...] += product(lhs, rhs, dims)

            @pl.when(k == nk - 1)
            def _():
                write(acc_refs)

    res, sent = _call(
        name, body, grid,
        [pl.BlockSpec(memory_space=pl.ANY) if b is None else pl.BlockSpec(b, im) for _, b, im in ins],
        [pl.BlockSpec(o[2], o[3]) for o in outs],
        [jax.ShapeDtypeStruct(o[0], o[1]) for o in outs],
        [] if nk == 1 else [pltpu.VMEM(s, F32) for s in acc_shapes],
        [a for a, _, _ in ins], comm, aliases)
    return res if comm is None else (res, sent)


def _rms_bwd(dh, xv, w, dres):
    r = lax.rsqrt(jnp.mean(xv * xv, axis=-1, keepdims=True) + EPS)
    g = dh * w
    dx = dres + r * (g - xv * (r * r) * jnp.mean(g * xv, axis=-1, keepdims=True))
    dw = jnp.sum(dh * xv * r, axis=0, keepdims=True)
    return dx, dw


def _rows(name, fn, ins, outs, tm):
    n_in = len(ins)
    rows = outs[0][0][0]

    def body(*refs):
        vals = fn(*[r[...] for r in refs[:n_in]])
        for o, v in zip(refs[n_in:], vals):
            o[...] = v.astype(o.dtype)

    in_specs = []
    for a, tiled in ins:
        if tiled:
            in_specs.append(pl.BlockSpec((tm, a.shape[1]), lambda i: (i, 0)))
        else:
            in_specs.append(pl.BlockSpec(a.shape, lambda i, nd=a.ndim: (0,) * nd))
    return pl.pallas_call(
        body,
        name=name,
        grid=(rows // tm,),
        in_specs=in_specs,
        out_specs=[pl.BlockSpec((tm, s[1]), lambda i: (i, 0)) for s, _ in outs],
        out_shape=[jax.ShapeDtypeStruct(s, d) for s, d in outs],
        compiler_params=_params(("arbitrary",)),
    )(*[a for a, _ in ins])


def _rmsnorm(name, x, w):
    def fn(xv, wv):
        r = lax.rsqrt(jnp.mean(xv * xv, axis=-1, keepdims=True) + EPS)
        return (xv * r * wv,)

    return _rows(name, fn, [(x, True), (w, False)], [(x.shape, BF16)], _tile(x.shape[0], 512, 16))[0]


def _mmc(*args, comm=None, aliases=None):
    if comm is None:
        return _mm(*args, aliases=aliases), []
    return _mm(*args, comm=comm, aliases=aliases)


def _carried(comms, key, avail):
    c = comms.get(key) if comms else None
    return c(avail) if callable(c) else c


def _ffn_up(name, h, wg, wu, comm=None):
    t, d = h.shape
    n = wg.shape[1]
    tm = _tile(t, 1024, 16)

    def epi_up(accs, refs, rows):
        g, u = accs
        return [g, u, g * _sig(g) * u]

    hid = ((N_DEV, t, n), BF16, (None, tm, n), lambda i, j, k: (j, i, 0), False)
    return _mmc(
        name + "_up", (t // tm, N_DEV, 1),
        [(h, (tm, d), lambda i, j, k: (i, 0)),
         (wg, (None, n, d), lambda i, j, k: (j, 0, 0)),
         (wu, (None, n, d), lambda i, j, k: (j, 0, 0))],
        [(0, 1, 0, NT), (0, 2, 1, NT)], [(tm, n)] * 2, epi_up, [hid] * 3, comm=comm)


def _ffn_down(name, x, act, wd, comm=None):
    t, d = x.shape
    n = wd.shape[1]
    tm2 = _tile(t, 1024, 8)
    tn = _tile(d, 1024, 128)

    def epi_down(accs, refs, rows):
        return [refs[2][rows] + 0.5 * accs[0]]

    (x_out,), sent = _mmc(
        name + "_down", (t // tm2, d // tn, N_DEV),
        [(act, (None, tm2, n), lambda i, j, k: (k, i, 0)),
         (wd, (None, n, tn), lambda i, j, k: (k, 0, j)),
         (x, (tm2, tn), lambda i, j, k: (i, j))],
        [(0, 1, 0, NN)], [(tm2, tn)], epi_down,
        [((t, d), F32, (tm2, tn), lambda i, j, k: (i, j), False)], comm=comm)
    return x_out, sent


def _ffn_bwd(name, dxo, dxo16, x, h, gate, up, act, wg, wu, wd, norm_w, comms=None):
    t, d = x.shape
    n = wg.shape[1]
    tm = _tile(t, 1024, 16)
    sent = {}
    avail = dict(sent=sent)

    def epi_act(accs, refs, rows):
        da = 0.5 * accs[0]
        g = refs[2][rows].astype(F32)
        u = refs[3][rows].astype(F32)
        s = _sig(g)
        return [da * u * (s * (1.0 + g * (1.0 - s))), da * g * s]

    hid_in = lambda a: (a, (None, tm, n), lambda i, j, k: (j, i, 0))
    hid_out = ((N_DEV, t, n), BF16, (None, tm, n), lambda i, j, k: (j, i, 0), False)
    (dgate, dup), sent["dact"] = _mmc(
        name + "_dact", (t // tm, N_DEV, 1),
        [(dxo16, (tm, d), lambda i, j, k: (i, 0)),
         (wd, (None, n, d), lambda i, j, k: (j, 0, 0)),
         hid_in(gate), hid_in(up)],
        [(0, 1, 0, NT)], [(tm, n)], epi_act, [hid_out] * 2, comm=_carried(comms, "dact", avail))

    tk = _tile(t, 1024, 16)
    w_out = ((N_DEV, n, d), BF16, (None, n, d), lambda i, j, k: (i, 0, 0), False)

    def weight_grad(key, acts, stream, scale):
        (dw,), sent[key] = _mmc(
            name + "_" + key, (N_DEV, 1, t // tk),
            [(acts, (None, tk, n), lambda i, j, k: (i, k, 0)), (stream, (tk, d), lambda i, j, k: (k, 0))],
            [(0, 1, 0, TN)], [(n, d)], lambda accs, refs, rows: [scale * accs[0]], [w_out],
            comm=_carried(comms, key, avail))
        avail[key] = dw
        return dw

    dwd = weight_grad("dwd", act, dxo16, 0.5)
    dwg = weight_grad("dwg", dgate, h, 1.0)
    dwu = weight_grad("dwu", dup, h, 1.0)

    tm3 = _tile(t // 2, 512, 16)
    half = t // 2 // tm3

    def epi_dx(accs, refs, rows):
        dx, dw = _rms_bwd(accs[0], refs[4][rows], refs[5][...], refs[6][rows])
        return [dx, dx, dw]

    def dx_half(key, first, filled):
        at = 0 if first else half
        row = lambda i, j, k: (i + at, 0)
        hid = lambda i, j, k: (k, i + at, 0)
        ins = [(dgate, (None, tm3, n), hid), (wg, (None, n, d), lambda i, j, k: (k, 0, 0)),
               (dup, (None, tm3, n), hid), (wu, (None, n, d), lambda i, j, k: (k, 0, 0)),
               (x, (tm3, d), row), (norm_w, (1, d), lambda i, j, k: (0, 0)), (dxo, (tm3, d), row)]
        ins += [(a, None, None) for a in filled]
        res, sent[key] = _mmc(
            name + "_" + key, (half, 1, N_DEV), ins,
            [(0, 1, 0, NN), (2, 3, 0, NN)], [(tm3, d)], epi_dx,
            [((t, d), F32, (tm3, d), row, False), ((t, d), BF16, (tm3, d), row, False),
             ((1, d), F32, (1, d), lambda i, j, k: (0, 0), True)],
            comm=_carried(comms, key, avail), aliases={7 + k: k for k in range(len(filled))})
        return res

    dx_a, dx16_a, dnorm_a = dx_half("dxa", True, [])
    dx, dx16, dnorm_b = dx_half("dxb", False, [dx_a, dx16_a])
    return dx, dx16, dwg, dwu, dwd, dnorm_a + dnorm_b, sent


def _rope(tv, cos, s1, s2):
    return tv * cos + pltpu.roll(tv, LANES - 8, 1) * s1 + pltpu.roll(tv, 8, 1) * s2


def _rope_bwd(dr, cos, s1, s2):
    return dr * cos + pltpu.roll(dr * s1, 8, 1) + pltpu.roll(dr * s2, LANES - 8, 1)


def _rope_tables(positions):
    inv_freq = jnp.power(jnp.float32(ROPE_THETA), -jnp.arange(0, ROT_DIM, 2, dtype=F32) / ROT_DIM)
    ang = positions.reshape(-1).astype(F32)[:, None] * inv_freq
    cos, sin = jnp.cos(ang), jnp.sin(ang)
    t = ang.shape[0]
    pad = A_HEAD_DIM - ROT_DIM
    cos_t = jnp.concatenate([cos, cos, jnp.ones((t, pad), F32)], axis=1)
    s1_t = jnp.concatenate([-sin, jnp.zeros((t, pad + 8), F32)], axis=1)
    s2_t = jnp.concatenate([jnp.zeros((t, 8), F32), sin, jnp.zeros((t, pad), F32)], axis=1)
    return tuple(jnp.tile(v, (1, 2)) for v in (cos_t, s1_t, s2_t))


def _att_common(i, k_p, k_c, v_p, v_c, tabs_p, tabs_c):
    kcat = jnp.concatenate([k_p, k_c], axis=0)
    vcat = jnp.concatenate([v_p, v_c], axis=0)
    tabs_k = [jnp.concatenate([a, b], axis=0) for a, b in zip(tabs_p, tabs_c)]
    kvar, vvar = [], []
    for ks in range(2):
        sl = slice(ks * LANES, (ks + 1) * LANES)
        kr = _rope(kcat[:, sl], *tabs_k)
        kvar.append([kr.astype(BF16), pltpu.roll(kr, 64, 1).astype(BF16)])
        vvar.append([vcat[:, sl].astype(BF16), pltpu.roll(vcat[:, sl], 64, 1).astype(BF16)])
    qi = lax.broadcasted_iota(jnp.int32, (ATT_BLOCK, 2 * ATT_BLOCK), 0)
    kj = lax.broadcasted_iota(jnp.int32, (ATT_BLOCK, 2 * ATT_BLOCK), 1)
    dist = qi + ATT_BLOCK - kj
    allowed = (dist >= 0) & (dist < ATT_BLOCK) & ((i > 0) | (kj >= ATT_BLOCK))
    lane = lax.broadcasted_iota(jnp.int32, (1, LANES), 1)
    halves = [lane < 64, lane >= 64]
    return kvar, vvar, allowed, halves


def _att_head_probs(qm, kslab, allowed, sink):
    s = lax.dot_general(qm, kslab, NT, preferred_element_type=F32)
    s = jnp.where(allowed, s, -jnp.inf)
    m = jnp.maximum(jnp.max(s, axis=-1, keepdims=True), sink)
    e = jnp.exp(s - m)
    es = jnp.exp(sink - m)
    den = jnp.sum(e, axis=-1, keepdims=True) + es
    inv = 1.0 / den
    return e * inv, es * inv


def _head_slots(h):
    pair, half = h // 2, h % 2
    kvh = h // 4
    return pair, half, kvh // 2, int(kvh % 2 != half)


def _attention_fwd(proj, tables, sinks, comm=None):
    t = proj.shape[0]
    nb = t // ATT_BLOCK
    b = ATT_BLOCK
    scale = A_HEAD_DIM ** -0.5

    def body(sink_ref, q_ref, kp_ref, kc_ref, vp_ref, vc_ref,
             cp_ref, ap_ref, bp_ref, cc_ref, ac_ref, bc_ref, o_ref):
        i = pl.program_id(0)
        tabs_p = (cp_ref[...], ap_ref[...], bp_ref[...])
        tabs_c = (cc_ref[...], ac_ref[...], bc_ref[...])
        kvar, vvar, allowed, halves = _att_common(
            i, kp_ref[...], kc_ref[...], vp_ref[...], vc_ref[...], tabs_p, tabs_c)
        for pair in range(A_HEADS // 2):
            sl = slice(pair * LANES, (pair + 1) * LANES)
            q2 = _rope(q_ref[:, sl], *tabs_c) * scale
            acc = jnp.zeros((b, LANES), F32)
            for half in range(2):
                h = 2 * pair + half
                _, _, ks, var = _head_slots(h)
                qm = jnp.where(halves[half], q2, 0.0).astype(BF16)
                p, _ = _att_head_probs(qm, kvar[ks][var], allowed, sink_ref[h])
                o = lax.dot_general(p.astype(BF16), vvar[ks][var], NN, preferred_element_type=F32)
                acc = acc + jnp.where(halves[half], o, 0.0)
            o_ref[:, sl] = acc.astype(o_ref.dtype)

    cur = lambda col: (lambda i: (i, col))
    prev = lambda col: (lambda i: (jnp.maximum(i - 1, 0), col))
    tab = [pl.BlockSpec((b, LANES), prev(0))] * 3 + [pl.BlockSpec((b, LANES), cur(0))] * 3
    (out,), sent = _call(
        "attention_fwd", body, (nb,),
        [pl.BlockSpec(memory_space=pltpu.SMEM),
         pl.BlockSpec((b, A_WIDTH), cur(0)),
         pl.BlockSpec((b, A_KV_WIDTH), prev(OFF_KA // A_KV_WIDTH)),
         pl.BlockSpec((b, A_KV_WIDTH), cur(OFF_KA // A_KV_WIDTH)),
         pl.BlockSpec((b, A_KV_WIDTH), prev(OFF_VA // A_KV_WIDTH)),
         pl.BlockSpec((b, A_KV_WIDTH), cur(OFF_VA // A_KV_WIDTH))] + tab,
        [pl.BlockSpec((b, A_WIDTH), cur(0))],
        [jax.ShapeDtypeStruct((t, A_WIDTH), BF16)], [],
        [sinks, proj, proj, proj, proj, proj, *tables, *tables], comm)
    return out, sent


def _attention_bwd(proj, tables, sinks, d_out, comm=None):
    t = proj.shape[0]
    nb = t // ATT_BLOCK
    b = ATT_BLOCK
    scale = A_HEAD_DIM ** -0.5

    def body(sink_ref, q_ref, kp_ref, kc_ref, vp_ref, vc_ref,
             cp_ref, ap_ref, bp_ref, cc_ref, ac_ref, bc_ref, do_ref,
             dq_ref, dk_ref, dv_ref, dsink_ref, dk_carry, dv_carry):
        step = pl.program_id(0)
        i = nb - 1 - step

        @pl.when(step == 0)
        def _():
            dk_carry[...] = jnp.zeros(dk_carry.shape, F32)
            dv_carry[...] = jnp.zeros(dv_carry.shape, F32)
            dsink_ref[...] = jnp.zeros(dsink_ref.shape, F32)

        tabs_p = (cp_ref[...], ap_ref[...], bp_ref[...])
        tabs_c = (cc_ref[...], ac_ref[...], bc_ref[...])
        kvar, vvar, allowed, halves = _att_common(
            i, kp_ref[...], kc_ref[...], vp_ref[...], vc_ref[...], tabs_p, tabs_c)
        lane = lax.broadcasted_iota(jnp.int32, (1, LANES), 1)
        dk_slab = [jnp.zeros((2 * b, LANES), F32) for _ in range(2)]
        dv_slab = [jnp.zeros((2 * b, LANES), F32) for _ in range(2)]
        dsink = jnp.zeros((1, LANES), F32)
        for pair in range(A_HEADS // 2):
            sl = slice(pair * LANES, (pair + 1) * LANES)
            q2 = _rope(q_ref[:, sl], *tabs_c) * scale
            do2 = do_ref[:, sl].astype(F32)
            dq2 = jnp.zeros((b, LANES), F32)
            for half in range(2):
                h = 2 * pair + half
                _, _, ks, var = _head_slots(h)
                qm = jnp.where(halves[half], q2, 0.0).astype(BF16)
                dom = jnp.where(halves[half], do2, 0.0).astype(BF16)
                p, p_sink = _att_head_probs(qm, kvar[ks][var], allowed, sink_ref[h])
                dp = lax.dot_general(dom, vvar[ks][var], NT, preferred_element_type=F32)
                dsum = jnp.sum(p * dp, axis=-1, keepdims=True)
                ds = (p * (dp - dsum)).astype(BF16)
                dsink = dsink + jnp.where(lane == h, -jnp.sum(p_sink * dsum), 0.0)
                dqh = lax.dot_general(ds, kvar[ks][var], NN, preferred_element_type=F32)
                dq2 = dq2 + jnp.where(halves[half], dqh, 0.0)
                dkx = lax.dot_general(ds, qm, TN, preferred_element_type=F32)
                dvx = lax.dot_general(p.astype(BF16), dom, TN, preferred_element_type=F32)
                if var:
                    dkx = pltpu.roll(dkx, 64, 1)
                    dvx = pltpu.roll(dvx, 64, 1)
                dk_slab[ks] = dk_slab[ks] + dkx
                dv_slab[ks] = dv_slab[ks] + dvx
            dq_ref[:, sl] = _rope_bwd(dq2 * scale, *tabs_c).astype(dq_ref.dtype)
        for ks in range(2):
            sl = slice(ks * LANES, (ks + 1) * LANES)
            dk_cur = dk_slab[ks][b:] + dk_carry[:, sl]
            dk_ref[:, sl] = _rope_bwd(dk_cur, *tabs_c).astype(dk_ref.dtype)
            dv_ref[:, sl] = (dv_slab[ks][b:] + dv_carry[:, sl]).astype(dv_ref.dtype)
            dk_carry[:, sl] = dk_slab[ks][:b]
            dv_carry[:, sl] = dv_slab[ks][:b]
        dsink_ref[...] += dsink

    cur = lambda col: (lambda s: (nb - 1 - s, col))
    prev = lambda col: (lambda s: (jnp.maximum(nb - 2 - s, 0), col))
    tab = [pl.BlockSpec((b, LANES), prev(0))] * 3 + [pl.BlockSpec((b, LANES), cur(0))] * 3
    return _call(
        "attention_bwd", body, (nb,),
        [pl.BlockSpec(memory_space=pltpu.SMEM),
         pl.BlockSpec((b, A_WIDTH), cur(0)),
         pl.BlockSpec((b, A_KV_WIDTH), prev(OFF_KA // A_KV_WIDTH)),
         pl.BlockSpec((b, A_KV_WIDTH), cur(OFF_KA // A_KV_WIDTH)),
         pl.BlockSpec((b, A_KV_WIDTH), prev(OFF_VA // A_KV_WIDTH)),
         pl.BlockSpec((b, A_KV_WIDTH), cur(OFF_VA // A_KV_WIDTH))] + tab
        + [pl.BlockSpec((b, A_WIDTH), cur(0))],
        [pl.BlockSpec((b, A_WIDTH), cur(0)),
         pl.BlockSpec((b, A_KV_WIDTH), cur(0)),
         pl.BlockSpec((b, A_KV_WIDTH), cur(0)),
         pl.BlockSpec((1, LANES), lambda s: (0, 0))],
        [jax.ShapeDtypeStruct((t, A_WIDTH), BF16),
         jax.ShapeDtypeStruct((t, A_KV_WIDTH), BF16),
         jax.ShapeDtypeStruct((t, A_KV_WIDTH), BF16),
         jax.ShapeDtypeStruct((1, LANES), F32)],
        [pltpu.VMEM((b, A_KV_WIDTH), F32), pltpu.VMEM((b, A_KV_WIDTH), F32)],
        [sinks, proj, proj, proj, proj, proj, *tables, *tables, d_out], comm)


HGRN_HEADS_PER_STEP = 4
HGRN_W = HGRN_HEADS_PER_STEP * B_HEAD
HGRN_ROWS = 2 * CHUNK


def _hgrn_gates(qp, fp, hlb):
    lb = _sig(hlb[0:1] - hlb[1:2])
    sg = _sig(fp)
    big_f = lb + (1.0 - lb) * sg
    kk = (1.0 - lb) * (1.0 - sg)
    sq = _sig(qp)
    return lb, sg, big_f, kk, sq


def _tri(lower):
    r = lax.broadcasted_iota(jnp.int32, (CHUNK, CHUNK), 0)
    c = lax.broadcasted_iota(jnp.int32, (CHUNK, CHUNK), 1)
    return (r >= c) if lower else (r <= c)


def _hgrn_factors(q_ref, k_ref, b_ref, sl, r0):
    out = []
    whole = slice(r0, r0 + CHUNK)
    bh = b_ref[whole, sl]
    for i in range(CHUNK // SUB):
        rows = slice(r0 + i * SUB, r0 + (i + 1) * SUB)
        b0 = b_ref[r0 + i * SUB - 1:r0 + i * SUB, sl] if i else jnp.zeros((1, B_HEAD), F32)
        eq = jnp.exp(b_ref[rows, sl] - b0)
        ek = jnp.exp(jnp.minimum(b0 - bh, MAX_DECAY_EXP))
        out.append((q_ref[rows, sl] * eq, k_ref[whole, sl] * ek, eq, ek))
    return out


def _hgrn_scores(factors):
    rows = [lax.dot_general(qd.astype(BF16), kx.astype(BF16), NT, preferred_element_type=F32)
            for qd, kx, _, _ in factors]
    return jnp.where(_tri(True), jnp.concatenate(rows, axis=0), 0.0)


def _hgrn_specs(steps, reverse):
    w = HGRN_W
    block = (lambda c: steps - 1 - c) if reverse else (lambda c: c)
    col = lambda off: pl.BlockSpec((HGRN_ROWS, w), lambda hh, c: (block(c), off // w + hh))
    return block, col


def _chunk_cumsum(log_f, r0, lower):
    tri = _tri(lower).astype(F32)
    return lax.dot_general(tri, log_f[r0:r0 + CHUNK], NN, precision=lax.Precision.HIGHEST,
                           preferred_element_type=F32)


def _hgrn_fwd(proj, hlb, wn, comm=None):
    t = proj.shape[0]
    nc = t // CHUNK
    steps = t // HGRN_ROWS
    w = HGRN_W
    block, col = _hgrn_specs(steps, False)

    def body(q_ref, f_ref, v_ref, og_ref, hlb_ref, wn_ref, ob_ref, oraw_ref, st_ref,
             state, qs, ks, bs):
        c = pl.program_id(1)

        @pl.when(c == 0)
        def _():
            state[...] = jnp.zeros(state.shape, F32)

        qp = q_ref[...]
        lb, sg, big_f, kk, sq = _hgrn_gates(qp, f_ref[...], hlb_ref[...])
        qs[...] = qp * sq
        ks[...] = kk
        log_f = jnp.log(big_f)
        for sub in range(HGRN_ROWS // CHUNK):
            r0 = sub * CHUNK
            rows = slice(r0, r0 + CHUNK)
            bs[rows] = _chunk_cumsum(log_f, r0, True)
            for h in range(HGRN_HEADS_PER_STEP):
                sl = slice(h * B_HEAD, (h + 1) * B_HEAD)
                bh = bs[rows, sl]
                qh = qs[rows, sl]
                kh = ks[rows, sl]
                vh = v_ref[rows, sl].astype(BF16)
                s0 = state[h]
                st_ref[sub, h] = s0.astype(st_ref.dtype)
                a = _hgrn_scores(_hgrn_factors(qs, ks, bs, sl, r0))
                o = lax.dot_general(a.astype(BF16), vh, NN, preferred_element_type=F32)
                o = o + lax.dot_general((qh * jnp.exp(bh)).astype(BF16), s0.astype(BF16), NT,
                                        preferred_element_type=F32)
                bl = bs[r0 + CHUNK - 1:r0 + CHUNK, sl]
                khat = (kh * jnp.exp(bl - bh)).astype(BF16)
                state[h] = s0 * jnp.exp(bl) + lax.dot_general(vh, khat, TN, preferred_element_type=F32)
                oraw_ref[rows, sl] = o
                r = lax.rsqrt(jnp.mean(o * o, axis=-1, keepdims=True) + EPS)
                og = og_ref[rows, sl]
                ob_ref[rows, sl] = (o * r * wn_ref[:, sl] * (og * _sig(og))).astype(ob_ref.dtype)

    vec = lambda rows: pl.BlockSpec((rows, w), lambda hh, c: (0, hh))
    tile = pl.BlockSpec((HGRN_ROWS, w), lambda hh, c: (c, hh))
    return _call(
        "hgrn_fwd", body, (B_WIDTH // w, steps),
        [col(OFF_QB), col(OFF_FB), col(OFF_IB), col(OFF_OG), vec(2), vec(1)],
        [tile, tile,
         pl.BlockSpec((HGRN_ROWS // CHUNK, HGRN_HEADS_PER_STEP, B_HEAD, B_HEAD), lambda hh, c: (c, hh, 0, 0))],
        [jax.ShapeDtypeStruct((t, B_WIDTH), BF16),
         jax.ShapeDtypeStruct((t, B_WIDTH), F32),
         jax.ShapeDtypeStruct((nc, B_WIDTH // B_HEAD, B_HEAD, B_HEAD), BF16)],
        [pltpu.VMEM((HGRN_HEADS_PER_STEP, B_HEAD, B_HEAD), F32)] + [pltpu.VMEM((HGRN_ROWS, w), F32)] * 3,
        [proj, proj, proj, proj, hlb, wn], comm)


def _hgrn_bwd(proj, hlb, wn, o_raw, states, d_out, comm=None):
    t = proj.shape[0]
    steps = t // HGRN_ROWS
    w = HGRN_W
    block, col = _hgrn_specs(steps, True)

    def body(q_ref, f_ref, v_ref, og_ref, hlb_ref, wn_ref, oraw_ref, st_ref, dout_ref,
             dq_ref, df_ref, dv_ref, dog_ref, dlb_ref, dwn_ref,
             dstate, qs, ks, bs, dbs, dks, dqs, dgs):
        c = pl.program_id(1)

        @pl.when(c == 0)
        def _():
            dstate[...] = jnp.zeros(dstate.shape, F32)
            dlb_ref[...] = jnp.zeros(dlb_ref.shape, F32)
            dwn_ref[...] = jnp.zeros(dwn_ref.shape, F32)

        qp = q_ref[...]
        lb, sg, big_f, kk, sq = _hgrn_gates(qp, f_ref[...], hlb_ref[...])
        qs[...] = qp * sq
        ks[...] = kk
        log_f = jnp.log(big_f)
        last_row = lax.broadcasted_iota(jnp.int32, (CHUNK, 1), 0) == CHUNK - 1
        for sub in reversed(range(HGRN_ROWS // CHUNK)):
            r0 = sub * CHUNK
            rows = slice(r0, r0 + CHUNK)
            bs[rows] = _chunk_cumsum(log_f, r0, True)
            for h in range(HGRN_HEADS_PER_STEP):
                sl = slice(h * B_HEAD, (h + 1) * B_HEAD)
                bh = bs[rows, sl]
                qh = qs[rows, sl]
                kh = ks[rows, sl]
                vh = v_ref[rows, sl].astype(BF16)
                s0 = st_ref[sub, h]
                ds1 = dstate[h]
                o = oraw_ref[rows, sl]
                og = og_ref[rows, sl]
                wnh = wn_ref[:, sl]
                sog = _sig(og)
                r = lax.rsqrt(jnp.mean(o * o, axis=-1, keepdims=True) + EPS)
                dob = dout_ref[rows, sl].astype(F32)
                don = dob * (og * sog)
                dog_ref[rows, sl] = (dob * (o * r * wnh) * (sog * (1.0 + og * (1.0 - sog)))).astype(dog_ref.dtype)
                dwn_ref[:, sl] += jnp.sum(don * o * r, axis=0, keepdims=True)
                gp = don * wnh
                do = r * (gp - o * (r * r) * jnp.mean(gp * o, axis=-1, keepdims=True))
                do16 = do.astype(BF16)
                factors = _hgrn_factors(qs, ks, bs, sl, r0)
                a = _hgrn_scores(factors)
                da = jnp.where(_tri(True), lax.dot_general(do16, vh, NT, preferred_element_type=F32), 0.0)
                dv = lax.dot_general(a.astype(BF16), do16, TN, preferred_element_type=F32)
                dq_rows = []
                dk = jnp.zeros((CHUNK, B_HEAD), F32)
                for i, (qd, kx, eq, ek) in enumerate(factors):
                    da_i = da[i * SUB:(i + 1) * SUB]
                    dq_rows.append(lax.dot_general(da_i, kx, NN, precision=lax.Precision.HIGHEST,
                                                   preferred_element_type=F32) * eq)
                    dk = dk + lax.dot_general(da_i, qd, TN, precision=lax.Precision.HIGHEST,
                                              preferred_element_type=F32) * ek
                eb = jnp.exp(bh)
                bl = bs[r0 + CHUNK - 1:r0 + CHUNK, sl]
                ebl = jnp.exp(bl)
                ekl = jnp.exp(bl - bh)
                qe = (qh * eb).astype(BF16)
                khat = (kh * ekl).astype(BF16)
                ds1_16 = ds1.astype(BF16)
                dq = jnp.concatenate(dq_rows, axis=0) + eb * lax.dot_general(
                    do16, s0, NN, preferred_element_type=F32)
                dv = dv + lax.dot_general(khat, ds1_16, NT, preferred_element_type=F32)
                dk_state = ekl * lax.dot_general(vh, ds1_16, NN, preferred_element_type=F32)
                dk = dk + dk_state
                db_last = (jnp.sum(kh * dk_state, axis=0, keepdims=True)
                           + ebl * jnp.sum(s0.astype(F32) * ds1, axis=0, keepdims=True))
                dbs[rows, sl] = qh * dq - kh * dk + jnp.where(last_row, db_last, 0.0)
                dks[rows, sl] = dk
                dqs[rows, sl] = dq
                dstate[h] = ds1 * ebl + lax.dot_general(do16, qe, TN, preferred_element_type=F32)
                dv_ref[rows, sl] = dv.astype(dv_ref.dtype)
            dgs[rows] = _chunk_cumsum(dbs, r0, False)
        dq_ref[...] = (dqs[...] * (sq * (1.0 + qp * (1.0 - sq)))).astype(dq_ref.dtype)
        df_minus_dk = dgs[...] / big_f - dks[...]
        df_ref[...] = ((1.0 - lb) * df_minus_dk * sg * (1.0 - sg)).astype(df_ref.dtype)
        dlb_ref[...] += jnp.sum((1.0 - sg) * df_minus_dk, axis=0, keepdims=True) * (lb * (1.0 - lb))

    vec = lambda rows: pl.BlockSpec((rows, w), lambda hh, c: (0, hh))
    blk = pl.BlockSpec((HGRN_ROWS, w), lambda hh, c: (block(c), hh))
    act = jax.ShapeDtypeStruct((t, B_WIDTH), BF16)
    row = jax.ShapeDtypeStruct((1, B_WIDTH), F32)
    return _call(
        "hgrn_bwd", body, (B_WIDTH // w, steps),
        [col(OFF_QB), col(OFF_FB), col(OFF_IB), col(OFF_OG), vec(2), vec(1), blk,
         pl.BlockSpec((HGRN_ROWS // CHUNK, HGRN_HEADS_PER_STEP, B_HEAD, B_HEAD),
                      lambda hh, c: (block(c), hh, 0, 0)),
         blk],
        [blk, blk, blk, blk, vec(1), vec(1)],
        [act, act, act, act, row, row],
        [pltpu.VMEM((HGRN_HEADS_PER_STEP, B_HEAD, B_HEAD), F32)] + [pltpu.VMEM((HGRN_ROWS, w), F32)] * 7,
        [proj, proj, proj, proj, hlb, wn, o_raw, states, d_out], comm)


def _mixer_fwd(x1, mix_norm, w_in, sinks, hlb, wn, weights, tables, comms):
    t, d = x1.shape
    n_in = w_in[0].shape[0]
    nb = d // N_DEV
    sent = {}
    h2 = _rmsnorm("mix_norm", x1, mix_norm)
    tm = _tile(t, 1024, 16)
    tn = _tile(n_in, 512, 128)
    (proj,), sent["proj"] = _mmc(
        "mix_proj", (t // tm, n_in // tn, 1),
        [(h2, (tm, d // 2), lambda i, j, k: (i, 0)), (w_in[0], (tn, d // 2), lambda i, j, k: (j, 0)),
         (h2, (tm, d // 2), lambda i, j, k: (i, 1)), (w_in[1], (tn, d // 2), lambda i, j, k: (j, 0))],
        [(0, 1, 0, NT), (2, 3, 0, NT)], [(tm, tn)], lambda accs, refs, rows: accs,
        [((t, n_in), F32, (tm, tn), lambda i, j, k: (i, j), False)], comm=comms.get("proj"))
    out_a, sent["attention"] = _attention_fwd(proj, tables, sinks, comms.get("attention"))
    (out_b, o_raw, states), sent["hgrn"] = _hgrn_fwd(proj, hlb, wn, comms.get("hgrn"))
    wua, wub, w_out = weights(sent)

    def epi_merge(accs, refs, rows):
        ua, ub = accs
        return [_sig(refs[4][rows]) * ua + _sig(refs[5][rows]) * ub, ua, ub]

    wide = ((t, d), BF16, (tm, nb), lambda i, j, k: (i, j), False)
    (merged, ua, ub), sent["merge"] = _mmc(
        "mix_merge", (t // tm, N_DEV, 1),
        [(out_a, (tm, A_WIDTH), lambda i, j, k: (i, 0)),
         (wua, (None, A_WIDTH, nb), lambda i, j, k: (j, 0, 0)),
         (out_b, (tm, B_WIDTH), lambda i, j, k: (i, 0)),
         (wub, (None, B_WIDTH, nb), lambda i, j, k: (j, 0, 0)),
         (proj, (tm, nb), lambda i, j, k: (i, OFF_GA // nb + j)),
         (proj, (tm, nb), lambda i, j, k: (i, (OFF_GA + d) // nb + j))],
        [(0, 1, 0, NN), (2, 3, 1, NN)], [(tm, nb)] * 2, epi_merge, [wide] * 3, comm=comms.get("merge"))

    tn2 = _tile(d, 1024, 128)
    (x2,) = _mm(
        "mix_out", (t // tm, d // tn2, 1),
        [(merged, (tm, d), lambda i, j, k: (i, 0)), (w_out, (d, tn2), lambda i, j, k: (0, j)),
         (x1, (tm, tn2), lambda i, j, k: (i, j))],
        [(0, 1, 0, NN)], [(tm, tn2)], lambda accs, refs, rows: [refs[2][rows] + accs[0]],
        [((t, d), F32, (tm, tn2), lambda i, j, k: (i, j), False)])
    saved = dict(h2=h2, proj=proj, out_a=out_a, out_b=out_b, o_raw=o_raw, states=states,
                 merged=merged, ua=ua, ub=ub)
    return x2, saved, sent


def _mixer_bwd(dx2, dx2_16, x1, mix_norm, w_in, sinks, hlb, wn, wua, wub, w_out, tables, sv, comms):
    t, d = x1.shape
    sent = {}
    avail = dict(sent=sent)
    n_in = w_in[0].shape[0]
    nb = d // N_DEV
    proj = sv["proj"]
    tm = _tile(t, 1024, 16)
    tn = _tile(d, 512, 128)

    def epi_gate(accs, refs, rows):
        dm = accs[0]
        sa = _sig(refs[2][rows])
        sb = _sig(refs[3][rows])
        ua = refs[4][rows].astype(F32)
        ub = refs[5][rows].astype(F32)
        return [dm * sa, dm * sb, dm * ua * sa * (1.0 - sa), dm * ub * sb * (1.0 - sb)]

    wide = ((t, d), BF16, (tm, tn), lambda i, j, k: (i, j), False)
    dua, dub, dga, dgb = _mm(
        "mix_dmerge", (t // tm, d // tn, 1),
        [(dx2_16, (tm, d), lambda i, j, k: (i, 0)),
         (w_out, (tn, d), lambda i, j, k: (j, 0)),
         (proj, (tm, tn), lambda i, j, k: (i, OFF_GA // tn + j)),
         (proj, (tm, tn), lambda i, j, k: (i, (OFF_GA + d) // tn + j)),
         (sv["ua"], (tm, tn), lambda i, j, k: (i, j)),
         (sv["ub"], (tm, tn), lambda i, j, k: (i, j))],
        [(0, 1, 0, NT)], [(tm, tn)], epi_gate, [wide] * 4)

    tk = _tile(t, 1024, 16)
    two = _tile(d, 1024, 128)
    (dw_out,) = _mm(
        "mix_dwout", (d // two, 1, t // tk),
        [(sv["merged"], (tk, two), lambda i, j, k: (k, i)), (dx2_16, (tk, d), lambda i, j, k: (k, 0))],
        [(0, 1, 0, TN)], [(two, d)], lambda accs, refs, rows: accs,
        [((d, d), BF16, (two, d), lambda i, j, k: (i, 0), False)])

    w_spec = lambda rows: ((N_DEV, rows, nb), BF16, (None, rows, nb), lambda i, j, k: (j, 0, 0), False)
    tk2 = _tile(t, 2048, 16)
    dwua, dwub = _mm(
        "mix_dwup", (1, N_DEV, t // tk2),
        [(sv["out_a"], (tk2, A_WIDTH), lambda i, j, k: (k, 0)),
         (dua, (tk2, nb), lambda i, j, k: (k, j)),
         (sv["out_b"], (tk2, B_WIDTH), lambda i, j, k: (k, 0)),
         (dub, (tk2, nb), lambda i, j, k: (k, j))],
        [(0, 1, 0, TN), (2, 3, 1, TN)], [(A_WIDTH, nb), (B_WIDTH, nb)], lambda accs, refs, rows: accs,
        [w_spec(A_WIDTH), w_spec(B_WIDTH)])

    d_out_a, d_out_b = _mm(
        "mix_dup", (t // tm, 1, N_DEV),
        [(dua, (tm, nb), lambda i, j, k: (i, k)),
         (wua, (None, A_WIDTH, nb), lambda i, j, k: (k, 0, 0)),
         (dub, (tm, nb), lambda i, j, k: (i, k)),
         (wub, (None, B_WIDTH, nb), lambda i, j, k: (k, 0, 0))],
        [(0, 1, 0, NT), (2, 3, 1, NT)], [(tm, A_WIDTH), (tm, B_WIDTH)], lambda accs, refs, rows: accs,
        [((t, A_WIDTH), BF16, (tm, A_WIDTH), lambda i, j, k: (i, 0), False),
         ((t, B_WIDTH), BF16, (tm, B_WIDTH), lambda i, j, k: (i, 0), False)])

    avail.update(w_out=dw_out.reshape(N_DEV, d // N_DEV, d), w_up_a=dwua, w_up_b=dwub)
    (dq_a, dk_a, dv_a, dsinks), sent["attention"] = _attention_bwd(
        proj, tables, sinks, d_out_a, _carried(comms, "attention", avail))
    (dq_b, df_b, di_b, dog_b, dlb, dwn), sent["hgrn"] = _hgrn_bwd(
        proj, hlb, wn, sv["o_raw"], sv["states"], d_out_b, _carried(comms, "hgrn", avail))
    dproj = jnp.concatenate([dq_a, dk_a, dv_a, dq_b, df_b, di_b, dog_b, dga, dgb], axis=1)

    tdm = _tile(d, 1024, 128)
    twn = _tile(n_in, 2432, 128)
    tkw = _tile(t, 512, 16)
    (dw_in,) = _mm(
        "mix_dwin", (n_in // twn, d // tdm, t // tkw),
        [(dproj, (tkw, twn), lambda i, j, k: (k, i)), (sv["h2"], (tkw, tdm), lambda i, j, k: (k, j))],
        [(0, 1, 0, TN)], [(twn, tdm)], lambda accs, refs, rows: accs,
        [((n_in, d), BF16, (twn, tdm), lambda i, j, k: (i, j), False)])

    avail.update(w_in=dw_in.reshape(N_DEV, n_in // N_DEV, d))
    tm3 = _tile(t, 512, 16)
    tpn = _tile(n_in, 512, 128)

    def epi_dx(accs, refs, rows):
        dh = jnp.concatenate(accs, axis=1)
        dx, dw = _rms_bwd(dh, refs[3][rows], refs[4][...], refs[5][rows])
        return [dx, dx, dw]

    (dx1, dx1_16, dmix), sent["dx"] = _mmc(
        "mix_dx", (t // tm3, 1, n_in // tpn),
        [(dproj, (tm3, tpn), lambda i, j, k: (i, k)),
         (w_in[0], (tpn, d // 2), lambda i, j, k: (k, 0)), (w_in[1], (tpn, d // 2), lambda i, j, k: (k, 0)),
         (x1, (tm3, d), lambda i, j, k: (i, 0)), (mix_norm, (1, d), lambda i, j, k: (0, 0)),
         (dx2, (tm3, d), lambda i, j, k: (i, 0))],
        [(0, 1, 0, NN), (0, 2, 1, NN)], [(tm3, d // 2)] * 2, epi_dx,
        [((t, d), F32, (tm3, d), lambda i, j, k: (i, 0), False),
         ((t, d), BF16, (tm3, d), lambda i, j, k: (i, 0), False),
         ((1, d), F32, (1, d), lambda i, j, k: (0, 0), True)],
        comm=_carried(comms, "dx", avail))
    small = dict(mix_norm=dmix, attn_sinks=dsinks, hgrn_lb=dlb, hgrn_norm=dwn)
    return dx1, dx1_16, small, avail


def _ple_fwd(x3, ple_norm, p16, wpg, wpp):
    t, d = x3.shape
    nb = d // N_DEV
    h4 = _rmsnorm("ple_norm", x3, ple_norm)
    tm = _tile(t, 1024, 16)

    def epi(accs, refs, rows):
        zg, pp = accs
        return [refs[4][rows] + _sig(zg) * pp, zg, pp]

    blk = lambda dt: ((t, d), dt, (tm, nb), lambda i, j, k: (i, j), False)
    x4, zg, pp = _mm(
        "ple_fwd", (t // tm, N_DEV, 1),
        [(h4, (tm, d), lambda i, j, k: (i, 0)), (wpg, (d, nb), lambda i, j, k: (0, j)),
         (p16, (tm, PLE_DIM), lambda i, j, k: (i, 0)),
         (wpp, (None, PLE_DIM, nb), lambda i, j, k: (j, 0, 0)),
         (x3, (tm, nb), lambda i, j, k: (i, j))],
        [(0, 1, 0, NN), (2, 3, 1, NN)], [(tm, nb)] * 2, epi, [blk(F32), blk(BF16), blk(BF16)])
    return x4, dict(h4=h4, zg=zg, pp=pp)


def _ple_bwd(dx4, x3, ple_norm, p16, wpg, sv):
    t, d = x3.shape
    nb = d // N_DEV

    def gate_grads(dx, zg, pp):
        s = _sig(zg.astype(F32))
        return dx * pp.astype(F32) * s * (1.0 - s), dx * s

    dzg, dpp = _rows("ple_dgate", gate_grads, [(dx4, True), (sv["zg"], True), (sv["pp"], True)],
                     [((t, d), BF16), ((t, d), BF16)], _tile(t, 512, 16))
    (dwpp,) = _mm(
        "ple_dwproj", (1, N_DEV, 1),
        [(p16, (t, PLE_DIM), lambda i, j, k: (0, 0)), (dpp, (t, nb), lambda i, j, k: (0, j))],
        [(0, 1, 0, TN)], [(PLE_DIM, nb)], lambda accs, refs, rows: accs,
        [((N_DEV, PLE_DIM, nb), BF16, (None, PLE_DIM, nb), lambda i, j, k: (j, 0, 0), False)])
    tk = _tile(t, 1024, 16)
    tn = _tile(d, 1024, 128)
    (dwpg,) = _mm(
        "ple_dwgate", (d // tn, 1, t // tk),
        [(sv["h4"], (tk, tn), lambda i, j, k: (k, i)), (dzg, (tk, d), lambda i, j, k: (k, 0))],
        [(0, 1, 0, TN)], [(tn, d)], lambda accs, refs, rows: accs,
        [((d, d), BF16, (tn, d), lambda i, j, k: (i, 0), False)])
    tm3 = _tile(t, 256, 16)

    def epi_dx(accs, refs, rows):
        dx, dw = _rms_bwd(accs[0], refs[2][rows], refs[3][...], refs[4][rows])
        return [dx, dx, dw]

    dx3, dx3_16, dnorm = _mm(
        "ple_dx", (t // tm3, 1, 1),
        [(dzg, (tm3, d), lambda i, j, k: (i, 0)), (wpg, (d, d), lambda i, j, k: (0, 0)),
         (x3, (tm3, d), lambda i, j, k: (i, 0)), (ple_norm, (1, d), lambda i, j, k: (0, 0)),
         (dx4, (tm3, d), lambda i, j, k: (i, 0))],
        [(0, 1, 0, NT)], [(tm3, d)], epi_dx,
        [((t, d), F32, (tm3, d), lambda i, j, k: (i, 0), False),
         ((t, d), BF16, (tm3, d), lambda i, j, k: (i, 0), False),
         ((1, d), F32, (1, d), lambda i, j, k: (0, 0), True)])
    return dx3, dx3_16, dwpp, dwpg, dnorm


def _loss_head(x4, final_norm, target):
    t, d = x4.shape
    tm = _tile(t, 256, 8)

    def body(x_ref, w_ref, tgt_ref, dx_ref, dw_ref, loss_ref):
        i = pl.program_id(0)
        xv = x_ref[...]
        wv = w_ref[...]
        r = lax.rsqrt(jnp.mean(xv * xv, axis=-1, keepdims=True) + EPS)
        err = xv * r * wv - tgt_ref[...]
        part = 0.5 * jnp.sum(jnp.mean(err * err, axis=-1, keepdims=True), axis=0, keepdims=True)
        dx, dw = _rms_bwd(err * (1.0 / d), xv, wv, jnp.zeros_like(xv))
        dx_ref[...] = dx

        @pl.when(i == 0)
        def _():
            dw_ref[...] = dw
            loss_ref[...] = jnp.broadcast_to(part, loss_ref.shape)

        @pl.when(i > 0)
        def _():
            dw_ref[...] += dw
            loss_ref[...] += jnp.broadcast_to(part, loss_ref.shape)

    return pl.pallas_call(
        body,
        name="loss_head",
        grid=(t // tm,),
        in_specs=[pl.BlockSpec((tm, d), lambda i: (i, 0)), pl.BlockSpec((1, d), lambda i: (0, 0)),
                  pl.BlockSpec((tm, d), lambda i: (i, 0))],
        out_specs=[pl.BlockSpec((tm, d), lambda i: (i, 0)), pl.BlockSpec((1, d), lambda i: (0, 0)),
                   pl.BlockSpec((1, LANES), lambda i: (0, 0))],
        out_shape=[jax.ShapeDtypeStruct((t, d), F32), jax.ShapeDtypeStruct((1, d), F32),
                   jax.ShapeDtypeStruct((1, LANES), F32)],
        compiler_params=_params(("arbitrary",)),
    )(x4, final_norm, target)


RELATIONS = ((0, 0), (1, 0), (0, 1), (1, 1))


def _place():
    return lax.axis_index("x"), lax.axis_index("y"), lax.axis_index("c")


def _flip(v, bit):
    return 1 - v if bit else v


def _gather_program(shards, mid_at):
    n = len(shards)
    slots = 8

    def copies(ins, outs, send_sems, recv_sems, local_sems, base):
        x, y, c = _place()
        sibling, x_chip, y_chip = (x, y, 1 - c), (1 - x, y, c), (x, 1 - y, c)

        def block(a, r, core, half=None):
            ref = outs[a].at[4 * _flip(x, RELATIONS[r][0]) + 2 * _flip(y, RELATIONS[r][1]) + core]
            rows = shards[a].shape[0] // 2
            return ref if half is None else ref.at[pl.ds(half * rows, rows)]

        def copy(a, slot, dst, to, src=None):
            return pltpu.make_async_remote_copy(
                src_ref=dst if src is None else src, dst_ref=dst,
                send_sem=send_sems.at[base[0] + slots * a + slot],
                recv_sem=recv_sems.at[base[1] + slots * a + slot], device_id=to, device_id_type=MESH)

        own = [pltpu.make_async_copy(ins[a], block(a, 0, c), local_sems.at[base[2] + a]) for a in range(n)]
        sent_first = [cp for a in range(n) for cp in (
            copy(a, 0, block(a, 0, c), sibling, src=ins[a]),
            copy(a, 1, block(a, 0, c), x_chip, src=ins[a]),
            copy(a, 2, block(a, 0, c), y_chip, src=ins[a]))]
        landed_first = [cp for a in range(n) for cp in (
            copy(a, 1, block(a, 1, c), x_chip), copy(a, 2, block(a, 2, c), y_chip))]
        sent_mid = [cp for a in range(n) for cp in (
            copy(a, 3, block(a, 1, c, 0), y_chip), copy(a, 4, block(a, 2, c, 1), x_chip),
            copy(a, 5, block(a, 1, c), sibling), copy(a, 6, block(a, 2, c), sibling))]
        landed_mid = [cp for a in range(n) for cp in (
            copy(a, 3, block(a, 3, c, 0), y_chip), copy(a, 4, block(a, 3, c, 1), x_chip))]
        sent_last = [copy(a, 7, block(a, 3, c), sibling) for a in range(n)]
        landed_sibling = [cp for a in range(n) for cp in (
            copy(a, 0, block(a, 0, 1 - c), sibling), copy(a, 5, block(a, 1, 1 - c), sibling),
            copy(a, 6, block(a, 2, 1 - c), sibling), copy(a, 7, block(a, 3, 1 - c), sibling))]
        return own, sent_first, landed_first, sent_mid, landed_mid, sent_last, landed_sibling

    def pre(*args):
        own, sent_first = copies(*args)[:2]
        for cp in own + sent_first:
            cp.start()

    def mid(*args):
        _, _, landed_first, sent_mid, _, _, _ = copies(*args)
        for cp in landed_first:
            cp.wait_recv()
        for cp in sent_mid:
            cp.start()

    def post(*args):
        own, sent_first, _, sent_mid, landed_mid, sent_last, landed_sibling = copies(*args)
        for cp in landed_mid:
            cp.wait_recv()
        for cp in sent_last:
            cp.start()
        for cp in landed_sibling:
            cp.wait_recv()
        for cp in sent_first + sent_mid + sent_last:
            cp.wait_send()
        for mine in own:
            mine.wait()

    return _Comm(list(shards), [jax.ShapeDtypeStruct((N_DEV,) + s.shape, s.dtype) for s in shards],
                 (slots * n, slots * n, n), pre, post, mid, mid_at)


def _exchange_program(arrays, n_slots, source, target, slot):
    n = len(arrays)

    def copies(ins, outs, send_sems, recv_sems, local_sems, base):
        place = _place()
        res = []
        for a in range(n):
            for r in range(n_slots):
                src = source(r, place)
                if src is None:
                    continue
                res.append(pltpu.make_async_remote_copy(
                    src_ref=ins[a].at[src], dst_ref=outs[a].at[slot(r)],
                    send_sem=send_sems.at[base[0] + n_slots * a + r],
                    recv_sem=recv_sems.at[base[1] + n_slots * a + r],
                    device_id=target(r, place), device_id_type=MESH))
        return res

    def pre(*args):
        for cp in copies(*args):
            cp.start()

    def post(*args):
        for cp in copies(*args):
            cp.wait()

    n_out = len({slot(r) for r in range(n_slots) if source(r, (0, 0, 0)) is not None})
    return _Comm(list(arrays), [jax.ShapeDtypeStruct((n_out,) + g.shape[1:], g.dtype) for g in arrays],
                 (n_slots * n, n_slots * n, 0), pre, post)


def _sibling_program(grads):
    def source(r, place):
        x, y, c = place
        return 4 * _flip(x, RELATIONS[r][0]) + 2 * _flip(y, RELATIONS[r][1]) + (1 - c)

    return _exchange_program(grads, 4, source, lambda r, p: (p[0], p[1], 1 - p[2]), lambda r: r)


def _chips_program(sums, relations=(1, 2, 3)):
    def target(r, place):
        x, y, c = place
        return (_flip(x, RELATIONS[r][0]), _flip(y, RELATIONS[r][1]), c)

    return _exchange_program(sums, 4, lambda r, p: r if r in relations else None, target,
                             lambda r: relations.index(r) if r in relations else 0)


def _standalone(name, comm):
    return _call(name, lambda: None, (1,), [], [], [], [], [], comm)[1]


def _pair_sum(name, grad, from_sibling, block_ids):
    _, rows, cols = grad.shape
    tr = _tile(rows, 512, 16)

    def body(ids_ref, mine_ref, sib_ref, o_ref):
        o_ref[...] = (mine_ref[...].astype(F32) + sib_ref[...].astype(F32)).astype(o_ref.dtype)

    return pl.pallas_call(
        body,
        name=name,
        grid_spec=pltpu.PrefetchScalarGridSpec(
            num_scalar_prefetch=1,
            grid=(4, rows // tr),
            in_specs=[pl.BlockSpec((None, tr, cols), lambda r, i, ids: (ids[r], i, 0)),
                      pl.BlockSpec((None, tr, cols), lambda r, i, ids: (r, i, 0))],
            out_specs=pl.BlockSpec((None, tr, cols), lambda r, i, ids: (r, i, 0))),
        out_shape=jax.ShapeDtypeStruct((4, rows, cols), BF16),
        compiler_params=_params(("arbitrary", "arbitrary")),
    )(block_ids, grad, from_sibling)


def _adam(w, g, m, v):
    m = ADAM_B1 * m + (1.0 - ADAM_B1) * g
    v = ADAM_B2 * v + (1.0 - ADAM_B2) * (g * g)
    m_hat = m / (1.0 - ADAM_B1 ** ADAM_STEP)
    v_hat = v / (1.0 - ADAM_B2 ** ADAM_STEP)
    delta = -ADAM_LR * (m_hat / (jnp.sqrt(v_hat) + ADAM_EPS) + ADAM_WD * w)
    return delta, m, v


def _adam_sharded(name, items, comm=None):
    _, rows, cols = items[0][0].shape
    n = len(items)
    tr = _tile(rows, max(64, 256 // n), 8)

    def body(*refs):
        for k in range(n):
            w_ref, m_ref, v_ref, s_ref, r1_ref, r2_ref, r3_ref = refs[7 * k:7 * k + 7]
            g_ref, d_ref, nm_ref, nv_ref = refs[7 * n + 4 * k:7 * n + 4 * k + 4]
            g = ((s_ref[...].astype(F32) + r1_ref[...].astype(F32)) + r2_ref[...].astype(F32)) + r3_ref[...].astype(F32)
            delta, nm, nv = _adam(w_ref[...], g, m_ref[...], v_ref[...])
            g_ref[...] = g
            d_ref[...] = delta
            nm_ref[...] = nm
            nv_ref[...] = nv

    slot = lambda s: pl.BlockSpec((None, tr, cols), lambda i: (s, i, 0))
    flat = slot(0)
    args, specs = [], []
    for w, m, v, sums, from_chips in items:
        args += [w, m, v, sums] + [a for a, _ in from_chips]
        specs += [flat, flat, flat, slot(0)] + [slot(s) for _, s in from_chips]
    res, sent = _call(
        name, body, (rows // tr,), specs, [flat] * (4 * n),
        [jax.ShapeDtypeStruct((1, rows, cols), F32)] * (4 * n), [], args, comm)
    return [res[4 * k:4 * k + 4] for k in range(n)], sent


def _all_reduce_small(part):
    rows = part.shape[0]

    def body(x_ref, o_ref, gathered, send_sems, recv_sems):
        x, y, c = _place()
        me = 4 * x + 2 * y + c
        gathered[me] = x_ref[...]
        copies = []
        for k in range(1, N_DEV):
            bits = (k >> 2 & 1, k >> 1 & 1, k & 1)
            peer = (_flip(x, bits[0]), _flip(y, bits[1]), _flip(c, bits[2]))
            copies.append(pltpu.make_async_remote_copy(
                src_ref=x_ref, dst_ref=gathered.at[me],
                send_sem=send_sems.at[k - 1], recv_sem=recv_sems.at[k - 1],
                device_id=peer, device_id_type=MESH))
        for cp in copies:
            cp.start()
        for cp in copies:
            cp.wait()
        total = gathered[0]
        for b in range(1, N_DEV):
            total = total + gathered[b]
        o_ref[...] = total

    return pl.pallas_call(
        body,
        name="small_all_reduce",
        in_specs=[pl.BlockSpec(memory_space=pltpu.VMEM)],
        out_specs=pl.BlockSpec(memory_space=pltpu.VMEM),
        out_shape=jax.ShapeDtypeStruct((rows, LANES), F32),
        scratch_shapes=[pltpu.VMEM((N_DEV, rows, LANES), F32),
                        pltpu.SemaphoreType.DMA((N_DEV - 1,)), pltpu.SemaphoreType.DMA((N_DEV - 1,))],
    )(part)


def _adam_small(w, g, m, v):
    def fn(wv, gv, mv, vv):
        return _adam(wv, gv, mv, vv)

    shape = (w.shape, F32)
    return _rows("adam_small", fn, [(w, True), (g, True), (m, True), (v, True)], [shape] * 3, w.shape[0])


SMALL = ("ffn1_norm", "mix_norm", "attn_sinks", "hgrn_lower_bound", "hgrn_norm", "ffn2_norm", "ple_norm",
         "final_norm")
LARGE = ("ffn1_w_gate", "ffn1_w_up", "ffn1_w_down", "w_in", "w_up_a", "w_up_b", "w_out",
         "ffn2_w_gate", "ffn2_w_up", "ffn2_w_down", "ple_w_gate", "ple_w_proj")
TRANSPOSED = ("ffn1_w_gate", "ffn1_w_up", "ffn2_w_gate", "ffn2_w_up", "w_in")
WEIGHTS = ("ffn1_norm", "ffn1_w_gate", "ffn1_w_up", "ffn1_w_down", "mix_norm", "w_in", "attn_sinks",
           "hgrn_lower_bound", "hgrn_norm", "w_up_a", "w_up_b", "w_out", "ffn2_norm", "ffn2_w_gate",
           "ffn2_w_up", "ffn2_w_down", "ple_norm", "ple_w_gate", "ple_w_proj", "final_norm")


def _pack_rows(arrays):
    rows = []
    for a in arrays:
        flat = a.reshape(-1).astype(F32)
        pad = -flat.shape[0] % LANES
        rows.append(jnp.pad(flat, (0, pad)).reshape(-1, LANES))
    packed = jnp.concatenate(rows, axis=0)
    return jnp.pad(packed, ((0, -packed.shape[0] % 8), (0, 0)))


def _unpack_rows(packed, like):
    out, at = [], 0
    for a in like:
        size = a.size
        n_rows = -(-size // LANES)
        out.append(packed[at:at + n_rows].reshape(-1)[:size].reshape(a.shape))
        at += n_rows
    return out


def kernel(x, p, positions, ffn1_norm, ffn1_w_gate, ffn1_w_up, ffn1_w_down, mix_norm, w_in, attn_sinks, hgrn_lower_bound, hgrn_norm, w_up_a, w_up_b, w_out, ffn2_norm, ffn2_w_gate, ffn2_w_up, ffn2_w_down, ple_norm, ple_w_gate, ple_w_proj, final_norm, loss_target, m_ffn1_norm, m_ffn1_w_gate, m_ffn1_w_up, m_ffn1_w_down, m_mix_norm, m_w_in, m_attn_sinks, m_hgrn_lower_bound, m_hgrn_norm, m_w_up_a, m_w_up_b, m_w_out, m_ffn2_norm, m_ffn2_w_gate, m_ffn2_w_up, m_ffn2_w_down, m_ple_norm, m_ple_w_gate, m_ple_w_proj, m_final_norm, v_ffn1_norm, v_ffn1_w_gate, v_ffn1_w_up, v_ffn1_w_down, v_mix_norm, v_w_in, v_attn_sinks, v_hgrn_lower_bound, v_hgrn_norm, v_w_up_a, v_w_up_b, v_w_out, v_ffn2_norm, v_ffn2_w_gate, v_ffn2_w_up, v_ffn2_w_down, v_ple_norm, v_ple_w_gate, v_ple_w_proj, v_final_norm):
    given = dict(locals())
    w = {n: given[n] for n in WEIGHTS}
    mom = {n: given["m_" + n] for n in WEIGHTS}
    var = {n: given["v_" + n] for n in WEIGHTS}
    t, d = x.shape[1], x.shape[2]

    x0, p16, target = x[0], p[0, 0].astype(BF16), loss_target[0]
    tables = _rope_tables(positions)
    sinks = attn_sinks.reshape(-1)
    final_w = final_norm.reshape(1, d)
    as_stored = lambda n, a: jnp.swapaxes(a, 1, 2) if n in TRANSPOSED else a
    shard = {n: as_stored(n, w[n])[0].astype(BF16) for n in LARGE}
    shard["w_in_left"], shard["w_in_right"] = shard["w_in"][:, :d // 2], shard["w_in"][:, d // 2:]
    xi, yi, ci = _place()
    block_ids = jnp.stack([4 * _flip(xi, rx) + 2 * _flip(yi, ry) + ci for rx, ry in RELATIONS]).astype(jnp.int32)

    def gather(*names):
        return _gather_program([shard[n] for n in names], 0.65)

    wg1, wu1 = _standalone("gather_ffn1", gather("ffn1_w_gate", "ffn1_w_up"))
    h1 = _rmsnorm("ffn1_norm", x0, ffn1_norm)
    (g1, u1, a1), (wd1, win_left) = _ffn_up("ffn1", h1, wg1, wu1, gather("ffn1_w_down", "w_in_left"))
    x1, (win_right,) = _ffn_down("ffn1", x0, a1, wd1, gather("w_in_right"))
    win = (win_left.reshape(-1, d // 2), win_right.reshape(-1, d // 2))
    mix_args = (mix_norm, win, sinks, hgrn_lower_bound, hgrn_norm)

    def mixer_weights(sent):
        wua, wub, wout_g = sent["proj"]
        return wua, wub, wout_g.reshape(d, d)

    x2, mix_saved, got_w = _mixer_fwd(
        x1, *mix_args, mixer_weights, tables,
        dict(proj=gather("w_up_a", "w_up_b", "w_out"), attention=gather("ffn2_w_up"),
             hgrn=gather("ffn2_w_gate"), merge=gather("ple_w_gate", "ple_w_proj")))
    wua, wub, wout = mixer_weights(got_w)
    (wu2,), (wg2,) = got_w["attention"], got_w["hgrn"]
    wpg_g, wpp = got_w["merge"]
    wpg = wpg_g.reshape(d, d)
    h3 = _rmsnorm("ffn2_norm", x2, ffn2_norm)
    (g2, u2, a2), (wd2,) = _ffn_up("ffn2", h3, wg2, wu2, gather("ffn2_w_down"))
    x3, _ = _ffn_down("ffn2", x2, a2, wd2)
    x4, ple_saved = _ple_fwd(x3, ple_norm, p16, wpg, wpp)
    dx4, d_final, loss_row = _loss_head(x4, final_w, target)

    sums, from_chips = {}, {}

    def pair_sums(names, grads, from_sibling):
        for n, g, s in zip(names, grads, from_sibling):
            sums[n] = _pair_sum("pair_sum_" + n, g, s, block_ids)

    def to_chips(names, grads, from_sibling):
        pair_sums(names, grads, from_sibling)
        return _chips_program([sums[n] for n in names])

    def arrived(names, results):
        for n, r in zip(names, results):
            from_chips[n] = [(r, 0), (r, 1), (r, 2)]

    def ffn_comms(first, second, names):
        dn, gn, un = ((n,) for n in names)
        return dict(
            dact=first, dwd=second,
            dwg=lambda av: _sibling_program([av["dwd"]]),
            dwu=lambda av: _merge([to_chips(dn, [av["dwd"]], av["sent"]["dwg"]), _sibling_program([av["dwg"]])]),
            dxa=lambda av: _merge([to_chips(gn, [av["dwg"]], av["sent"]["dwu"][1:]), _sibling_program([av["dwu"]])]),
            dxb=lambda av: to_chips(un, [av["dwu"]], av["sent"]["dxa"][1:]))

    def ffn_arrived(names, sent):
        arrived(names, [sent["dwu"][0], sent["dxa"][0], sent["dxb"][0]])

    dx3, dx3_16, dwpp, dwpg, d_ple = _ple_bwd(dx4, x3, ple_norm, p16, wpg, ple_saved)
    ple_names = ("ple_w_proj", "ple_w_gate")
    ple_grads = [dwpp, dwpg.reshape(N_DEV, d // N_DEV, d)]
    ffn2_names = ("ffn2_w_down", "ffn2_w_gate", "ffn2_w_up")
    dx2, dx2_16, _, _, _, d_ffn2, sent2 = _ffn_bwd(
        "ffn2b", dx3, dx3_16, x2, h3, g2, u2, a2, wg2, wu2, wd2, ffn2_norm,
        ffn_comms(_sibling_program(ple_grads), lambda av: to_chips(ple_names, ple_grads, av["sent"]["dact"]),
                  ffn2_names))
    arrived(ple_names, sent2["dwd"])
    ffn_arrived(ffn2_names, sent2)
    mix3 = ("w_out", "w_up_a", "w_up_b")
    dx1, dx1_16, mix_small, mix_made = _mixer_bwd(
        dx2, dx2_16, x1, *mix_args, wua, wub, wout, tables, mix_saved,
        dict(attention=lambda av: _sibling_program([av[n] for n in mix3]),
             hgrn=lambda av: to_chips(mix3, [av[n] for n in mix3], av["sent"]["attention"]),
             dx=lambda av: _sibling_program([av["w_in"]])))
    sent_mix = mix_made["sent"]
    arrived(mix3, sent_mix["hgrn"])
    pair_sums(("w_in",), [mix_made["w_in"]], sent_mix["dx"])
    ffn1_names = ("ffn1_w_down", "ffn1_w_gate", "ffn1_w_up")
    grad_x, _, _, _, _, d_ffn1, sent1 = _ffn_bwd(
        "ffn1b", dx1, dx1_16, x0, h1, g1, u1, a1, wg1, wu1, wd1, ffn1_norm,
        ffn_comms(_chips_program([sums["w_in"]], (1, 2)), _chips_program([sums["w_in"]], (3,)), ffn1_names))
    (near,), (far,) = sent1["dact"], sent1["dwd"]
    from_chips["w_in"] = [(near, 0), (near, 1), (far, 0)]
    ffn_arrived(ffn1_names, sent1)

    out = {}
    for n in LARGE:
        (res,), _ = _adam_sharded("adam_" + n, [(as_stored(n, w[n]), as_stored(n, mom[n]), as_stored(n, var[n]),
                                                  sums[n], from_chips[n])])
        out[n] = [as_stored(n, a) for a in res]

    dlb = mix_small["hgrn_lb"]
    small_grads = dict(
        ffn1_norm=d_ffn1, mix_norm=mix_small["mix_norm"], attn_sinks=mix_small["attn_sinks"][:, :A_HEADS],
        hgrn_lower_bound=jnp.concatenate([dlb, -dlb], axis=0), hgrn_norm=mix_small["hgrn_norm"],
        ffn2_norm=d_ffn2, ple_norm=d_ple, final_norm=d_final)

    small_like = [w[n] for n in SMALL]
    packed = _pack_rows([small_grads[n].reshape(w[n].shape) for n in SMALL] + [loss_row[:, :1]])
    total = _all_reduce_small(packed)
    n_rows = total.shape[0]
    pack_w = _pack_rows(small_like + [jnp.zeros((1, 1), F32)])
    pack_m = _pack_rows([mom[n] for n in SMALL] + [jnp.zeros((1, 1), F32)])
    pack_v = _pack_rows([var[n] for n in SMALL] + [jnp.zeros((1, 1), F32)])
    deltas = _adam_small(pack_w, total, pack_m, pack_v)
    like = small_like + [jnp.zeros((1, 1), F32)]
    g_small = _unpack_rows(total, like)
    d_small, m_small, v_small = (_unpack_rows(a, like) for a in deltas)
    for k, n in enumerate(SMALL):
        out[n] = [g_small[k], d_small[k], m_small[k], v_small[k]]
    loss = g_small[-1].reshape(())

    return (loss, grad_x.reshape(x.shape),
            *[out[n][0] for n in WEIGHTS], *[out[n][1] for n in WEIGHTS],
            *[out[n][2] for n in WEIGHTS], *[out[n][3] for n in WEIGHTS])
```

```python
import functools

import jax
import jax.numpy as jnp
from jax import lax
from jax.experimental import pallas as pl
from jax.experimental.pallas import tpu as pltpu

F32 = jnp.float32
BF16 = jnp.bfloat16
MESH = pl.DeviceIdType.MESH

EPS = 1e-6
N_DEV = 8
A_HEADS = 16
A_HEAD_DIM = 64
A_WIDTH = 1024
A_KV_WIDTH = 256
ATT_BLOCK = 128
ROT_DIM = 16
ROPE_THETA = 500000.0
B_WIDTH = 1024
B_HEAD = 128
CHUNK = 64
SUB = 16
MAX_DECAY_EXP = 60.0
PLE_DIM = 256
LANES = 128

ADAM_LR = 0.001
ADAM_B1 = 0.9
ADAM_B2 = 0.999
ADAM_EPS = 1e-08
ADAM_WD = 0.01
ADAM_STEP = 10

VMEM_LIMIT = 56 * 1024 * 1024
EPILOGUE_ELEMS = 256 * 1024
LATE_AT = 0.92

NN = (((1,), (0,)), ((), ()))
NT = (((1,), (1,)), ((), ()))
TN = (((0,), (0,)), ((), ()))

OFF_QA = 0
OFF_KA = 1024
OFF_VA = 1280
OFF_QB = 1536
OFF_FB = 2560
OFF_IB = 3584
OFF_OG = 4608
OFF_GA = 5632


def _sig(v):
    return 1.0 / (1.0 + jnp.exp(-v))


def _tile(n, pref, mult):
    t = min(n, pref)
    t -= t % mult
    while n % t:
        t -= mult
    return t


def _params(sem):
    return pltpu.CompilerParams(dimension_semantics=sem, vmem_limit_bytes=VMEM_LIMIT)


HBM_SPEC = pl.BlockSpec(memory_space=pltpu.HBM)


class _Comm:
    def __init__(self, ins, out_shapes, n_sems, pre, post, mid=None, mid_at=1.0, late=None):
        self.ins, self.out_shapes, self.n_sems, self.pre, self.post = ins, out_shapes, n_sems, pre, post
        self.mid = mid if mid is not None else (lambda *args: None)
        self.late = late if late is not None else (lambda *args: None)
        self.mid_at = mid_at
        self.parts = [len(out_shapes)]

    def split(self, outs):
        res, at = [], 0
        for n in self.parts:
            res.append(list(outs[at:at + n]))
            at += n
        return res


def _merge(progs):
    spans, ins, shapes, sems = [], [], [], [0, 0, 0]
    for p in progs:
        spans.append((len(ins), len(shapes), tuple(sems)))
        ins += list(p.ins)
        shapes += list(p.out_shapes)
        sems = [a + b for a, b in zip(sems, p.n_sems)]

    def run(which):
        def go(cin, cout, send, recv, local, base):
            for p, (i0, o0, s0) in zip(progs, spans):
                getattr(p, which)(cin[i0:i0 + len(p.ins)], cout[o0:o0 + len(p.out_shapes)], send, recv, local,
                                  tuple(b + s for b, s in zip(base, s0)))
        return go

    merged = _Comm(ins, shapes, tuple(sems), run("pre"), run("post"), run("mid"), max(p.mid_at for p in progs),
                   run("late"))
    merged.parts = [len(p.out_shapes) for p in progs]
    return merged


def _call(name, body, grid, in_specs, out_specs, out_shape, scratch, args, comm=None, aliases=None):
    sem = ("arbitrary",) * len(grid)
    aliases = aliases or {}
    if comm is None:
        res = pl.pallas_call(body, name=name, grid=grid, in_specs=in_specs, out_specs=out_specs,
                             out_shape=out_shape, scratch_shapes=scratch, input_output_aliases=aliases,
                             compiler_params=_params(sem))(*args)
        return list(res), []
    n_in, n_out, n_scr = len(in_specs), len(out_specs), len(scratch)
    c_in, c_out = len(comm.ins), len(comm.out_shapes)
    steps = functools.reduce(lambda a, b: a * b, grid)
    mid_step = min(steps - 1, int(comm.mid_at * steps))
    late_step = min(steps - 1, max(mid_step, int(LATE_AT * steps)))

    def carrier(*refs):
        at = [0]

        def take(n):
            at[0] += n
            return refs[at[0] - n:at[0]]

        ins, cins, outs, couts, scr, sems = take(n_in), take(c_in), take(n_out), take(c_out), take(n_scr), take(3)
        ids = [pl.program_id(a) for a in range(len(grid))]
        step = functools.reduce(lambda acc, ig: acc * ig[1] + ig[0], zip(ids, grid), 0)

        @pl.when(step == 0)
        def _():
            comm.pre(cins, couts, *sems, (0, 0, 0))

        body(*ins, *outs, *scr)

        @pl.when(step == mid_step)
        def _():
            comm.mid(cins, couts, *sems, (0, 0, 0))

        @pl.when(step == late_step)
        def _():
            comm.late(cins, couts, *sems, (0, 0, 0))

        @pl.when(step == steps - 1)
        def _():
            comm.post(cins, couts, *sems, (0, 0, 0))

    res = pl.pallas_call(
        carrier, name=name, grid=grid,
        in_specs=list(in_specs) + [HBM_SPEC] * c_in,
        out_specs=list(out_specs) + [HBM_SPEC] * c_out,
        out_shape=list(out_shape) + list(comm.out_shapes),
        scratch_shapes=list(scratch) + [pltpu.SemaphoreType.DMA((max(n, 1),)) for n in comm.n_sems],
        input_output_aliases=aliases,
        compiler_params=_params(sem),
    )(*args, *comm.ins)
    return list(res[:n_out]), list(res[n_out:])


def _mm(name, grid, ins, prods, acc_shapes, epi, outs, comm=None, aliases=None):
    n_in, n_out, nk = len(ins), len(outs), grid[2]
    tile_rows, tile_cols = acc_shapes[0]
    chunk = tile_rows
    while chunk * tile_cols > EPILOGUE_ELEMS and chunk % 32 == 0:
        chunk //= 2

    def body(*refs):
        in_refs = refs[:n_in]
        out_refs = refs[n_in:n_in + n_out]
        acc_refs = refs[n_in + n_out:]
        first_tile = jnp.logical_and(pl.program_id(0) == 0, pl.program_id(1) == 0)

        def product(lhs, rhs, dims):
            a = in_refs[lhs][...].astype(BF16)
            b = in_refs[rhs][...].astype(BF16)
            return lax.dot_general(a, b, dims, preferred_element_type=F32)

        def write(accs):
            totals = [None] * n_out
            for c in range(tile_rows // chunk):
                rows = slice(c * chunk, (c + 1) * chunk)
                vals = epi([a[rows] for a in accs], in_refs, rows)
                for idx, (o, v, spec) in enumerate(zip(out_refs, vals, outs)):
                    if spec[4]:
                        totals[idx] = v if totals[idx] is None else totals[idx] + v
                    else:
                        o[rows] = v.astype(o.dtype)
            for o, v, spec in zip(out_refs, totals, outs):
                if spec[4]:
                    @pl.when(first_tile)
                    def _():
                        o[...] = v.astype(o.dtype)

                    @pl.when(jnp.logical_not(first_tile))
                    def _():
                        o[...] += v.astype(o.dtype)

        if nk == 1:
            accs = [None] * len(acc_shapes)
            for lhs, rhs, acc, dims in prods:
                t = product(lhs, rhs, dims)
                accs[acc] = t if accs[acc] is None else accs[acc] + t
            write(accs)
        else:
            k = pl.program_id(2)

            @pl.when(k == 0)
            def _():
                for r in acc_refs:
                    r[...] = jnp.zeros(r.shape, F32)

            for lhs, rhs, acc, dims in prods:
                acc_refs[acc][...] += product(lhs, rhs, dims)

            @pl.when(k == nk - 1)
            def _():
                write(acc_refs)

    res, sent = _call(
        name, body, grid,
        [pl.BlockSpec(memory_space=pl.ANY) if b is None else pl.BlockSpec(b, im) for _, b, im in ins],
        [pl.BlockSpec(o[2], o[3]) for o in outs],
        [jax.ShapeDtypeStruct(o[0], o[1]) for o in outs],
        [] if nk == 1 else [pltpu.VMEM(s, F32) for s in acc_shapes],
        [a for a, _, _ in ins], comm, aliases)
    return res if comm is None else (res, sent)


def _rms_bwd(dh, xv, w, dres):
    r = lax.rsqrt(jnp.mean(xv * xv, axis=-1, keepdims=True) + EPS)
    g = dh * w
    dx = dres + r * (g - xv * (r * r) * jnp.mean(g * xv, axis=-1, keepdims=True))
    dw = jnp.sum(dh * xv * r, axis=0, keepdims=True)
    return dx, dw


def _rows(name, fn, ins, outs, tm):
    n_in = len(ins)
    rows = outs[0][0][0]

    def body(*refs):
        vals = fn(*[r[...] for r in refs[:n_in]])
        for o, v in zip(refs[n_in:], vals):
            o[...] = v.astype(o.dtype)

    in_specs = []
    for a, tiled in ins:
        if tiled:
            in_specs.append(pl.BlockSpec((tm, a.shape[1]), lambda i: (i, 0)))
        else:
            in_specs.append(pl.BlockSpec(a.shape, lambda i, nd=a.ndim: (0,) * nd))
    return pl.pallas_call(
        body,
        name=name,
        grid=(rows // tm,),
        in_specs=in_specs,
        out_specs=[pl.BlockSpec((tm, s[1]), lambda i: (i, 0)) for s, _ in outs],
        out_shape=[jax.ShapeDtypeStruct(s, d) for s, d in outs],
        compiler_params=_params(("arbitrary",)),
    )(*[a for a, _ in ins])


def _rmsnorm(name, x, w):
    def fn(xv, wv):
        r = lax.rsqrt(jnp.mean(xv * xv, axis=-1, keepdims=True) + EPS)
        return (xv * r * wv,)

    return _rows(name, fn, [(x, True), (w, False)], [(x.shape, BF16)], _tile(x.shape[0], 512, 16))[0]


def _mmc(*args, comm=None, aliases=None):
    if comm is None:
        return _mm(*args, aliases=aliases), []
    return _mm(*args, comm=comm, aliases=aliases)


def _carried(comms, key, avail):
    c = comms.get(key) if comms else None
    return c(avail) if callable(c) else c


def _ffn_up(name, h, wg, wu, comm=None):
    t, d = h.shape
    n = wg.shape[1]
    tm = _tile(t, 1024, 16)

    def epi_up(accs, refs, rows):
        g, u = accs
        return [g, u, g * _sig(g) * u]

    hid = ((N_DEV, t, n), BF16, (None, tm, n), lambda i, j, k: (j, i, 0), False)
    return _mmc(
        name + "_up", (t // tm, N_DEV, 1),
        [(h, (tm, d), lambda i, j, k: (i, 0)),
         (wg, (None, n, d), lambda i, j, k: (j, 0, 0)),
         (wu, (None, n, d), lambda i, j, k: (j, 0, 0))],
        [(0, 1, 0, NT), (0, 2, 1, NT)], [(tm, n)] * 2, epi_up, [hid] * 3, comm=comm)


def _ffn_down(name, x, act, wd, comm=None):
    t, d = x.shape
    n = wd.shape[1]
    tm2 = _tile(t, 1024, 8)
    tn = _tile(d, 1024, 128)

    def epi_down(accs, refs, rows):
        return [refs[2][rows] + 0.5 * accs[0]]

    (x_out,), sent = _mmc(
        name + "_down", (t // tm2, d // tn, N_DEV),
        [(act, (None, tm2, n), lambda i, j, k: (k, i, 0)),
         (wd, (None, n, tn), lambda i, j, k: (k, 0, j)),
         (x, (tm2, tn), lambda i, j, k: (i, j))],
        [(0, 1, 0, NN)], [(tm2, tn)], epi_down,
        [((t, d), F32, (tm2, tn), lambda i, j, k: (i, j), False)], comm=comm)
    return x_out, sent


def _ffn_bwd(name, dxo, dxo16, x, h, gate, up, act, wg, wu, wd, norm_w, comms=None):
    t, d = x.shape
    n = wg.shape[1]
    tm = _tile(t, 1024, 16)
    sent = {}
    avail = dict(sent=sent)

    def epi_act(accs, refs, rows):
        da = 0.5 * accs[0]
        g = refs[2][rows].astype(F32)
        u = refs[3][rows].astype(F32)
        s = _sig(g)
        return [da * u * (s * (1.0 + g * (1.0 - s))), da * g * s]

    hid_in = lambda a: (a, (None, tm, n), lambda i, j, k: (j, i, 0))
    hid_out = ((N_DEV, t, n), BF16, (None, tm, n), lambda i, j, k: (j, i, 0), False)
    (dgate, dup), sent["dact"] = _mmc(
        name + "_dact", (t // tm, N_DEV, 1),
        [(dxo16, (tm, d), lambda i, j, k: (i, 0)),
         (wd, (None, n, d), lambda i, j, k: (j, 0, 0)),
         hid_in(gate), hid_in(up)],
        [(0, 1, 0, NT)], [(tm, n)], epi_act, [hid_out] * 2, comm=_carried(comms, "dact", avail))

    tk = _tile(t, 1024, 16)
    w_out = ((N_DEV, n, d), BF16, (None, n, d), lambda i, j, k: (i, 0, 0), False)

    def weight_grad(key, acts, stream, scale):
        (dw,), sent[key] = _mmc(
            name + "_" + key, (N_DEV, 1, t // tk),
            [(acts, (None, tk, n), lambda i, j, k: (i, k, 0)), (stream, (tk, d), lambda i, j, k: (k, 0))],
            [(0, 1, 0, TN)], [(n, d)], lambda accs, refs, rows: [scale * accs[0]], [w_out],
            comm=_carried(comms, key, avail))
        avail[key] = dw
        return dw

    dwd = weight_grad("dwd", act, dxo16, 0.5)
    dwg = weight_grad("dwg", dgate, h, 1.0)
    dwu = weight_grad("dwu", dup, h, 1.0)

    tm3 = _tile(t // 2, 512, 16)
    half = t // 2 // tm3

    def epi_dx(accs, refs, rows):
        dx, dw = _rms_bwd(accs[0], refs[4][rows], refs[5][...], refs[6][rows])
        return [dx, dx, dw]

    def dx_half(key, first, filled):
        at = 0 if first else half
        row = lambda i, j, k: (i + at, 0)
        hid = lambda i, j, k: (k, i + at, 0)
        ins = [(dgate, (None, tm3, n), hid), (wg, (None, n, d), lambda i, j, k: (k, 0, 0)),
               (dup, (None, tm3, n), hid), (wu, (None, n, d), lambda i, j, k: (k, 0, 0)),
               (x, (tm3, d), row), (norm_w, (1, d), lambda i, j, k: (0, 0)), (dxo, (tm3, d), row)]
        ins += [(a, None, None) for a in filled]
        res, sent[key] = _mmc(
            name + "_" + key, (half, 1, N_DEV), ins,
            [(0, 1, 0, NN), (2, 3, 0, NN)], [(tm3, d)], epi_dx,
            [((t, d), F32, (tm3, d), row, False), ((t, d), BF16, (tm3, d), row, False),
             ((1, d), F32, (1, d), lambda i, j, k: (0, 0), True)],
            comm=_carried(comms, key, avail), aliases={7 + k: k for k in range(len(filled))})
        return res

    dx_a, dx16_a, dnorm_a = dx_half("dxa", True, [])
    dx, dx16, dnorm_b = dx_half("dxb", False, [dx_a, dx16_a])
    return dx, dx16, dwg, dwu, dwd, dnorm_a + dnorm_b, sent


def _rope(tv, cos, s1, s2):
    return tv * cos + pltpu.roll(tv, LANES - 8, 1) * s1 + pltpu.roll(tv, 8, 1) * s2


def _rope_bwd(dr, cos, s1, s2):
    return dr * cos + pltpu.roll(dr * s1, 8, 1) + pltpu.roll(dr * s2, LANES - 8, 1)


def _rope_tables(positions):
    inv_freq = jnp.power(jnp.float32(ROPE_THETA), -jnp.arange(0, ROT_DIM, 2, dtype=F32) / ROT_DIM)
    ang = positions.reshape(-1).astype(F32)[:, None] * inv_freq
    cos, sin = jnp.cos(ang), jnp.sin(ang)
    t = ang.shape[0]
    pad = A_HEAD_DIM - ROT_DIM
    cos_t = jnp.concatenate([cos, cos, jnp.ones((t, pad), F32)], axis=1)
    s1_t = jnp.concatenate([-sin, jnp.zeros((t, pad + 8), F32)], axis=1)
    s2_t = jnp.concatenate([jnp.zeros((t, 8), F32), sin, jnp.zeros((t, pad), F32)], axis=1)
    return tuple(jnp.tile(v, (1, 2)) for v in (cos_t, s1_t, s2_t))


def _att_common(i, k_p, k_c, v_p, v_c, tabs_p, tabs_c):
    kcat = jnp.concatenate([k_p, k_c], axis=0)
    vcat = jnp.concatenate([v_p, v_c], axis=0)
    tabs_k = [jnp.concatenate([a, b], axis=0) for a, b in zip(tabs_p, tabs_c)]
    kvar, vvar = [], []
    for ks in range(2):
        sl = slice(ks * LANES, (ks + 1) * LANES)
        kr = _rope(kcat[:, sl], *tabs_k)
        kvar.append([kr.astype(BF16), pltpu.roll(kr, 64, 1).astype(BF16)])
        vvar.append([vcat[:, sl].astype(BF16), pltpu.roll(vcat[:, sl], 64, 1).astype(BF16)])
    qi = lax.broadcasted_iota(jnp.int32, (ATT_BLOCK, 2 * ATT_BLOCK), 0)
    kj = lax.broadcasted_iota(jnp.int32, (ATT_BLOCK, 2 * ATT_BLOCK), 1)
    dist = qi + ATT_BLOCK - kj
    allowed = (dist >= 0) & (dist < ATT_BLOCK) & ((i > 0) | (kj >= ATT_BLOCK))
    lane = lax.broadcasted_iota(jnp.int32, (1, LANES), 1)
    halves = [lane < 64, lane >= 64]
    return kvar, vvar, allowed, halves


def _att_head_probs(qm, kslab, allowed, sink):
    s = lax.dot_general(qm, kslab, NT, preferred_element_type=F32)
    s = jnp.where(allowed, s, -jnp.inf)
    m = jnp.maximum(jnp.max(s, axis=-1, keepdims=True), sink)
    e = jnp.exp(s - m)
    es = jnp.exp(sink - m)
    den = jnp.sum(e, axis=-1, keepdims=True) + es
    inv = 1.0 / den
    return e * inv, es * inv


def _head_slots(h):
    pair, half = h // 2, h % 2
    kvh = h // 4
    return pair, half, kvh // 2, int(kvh % 2 != half)


def _attention_fwd(proj, tables, sinks, comm=None):
    t = proj.shape[0]
    nb = t // ATT_BLOCK
    b = ATT_BLOCK
    scale = A_HEAD_DIM ** -0.5

    def body(sink_ref, q_ref, kp_ref, kc_ref, vp_ref, vc_ref,
             cp_ref, ap_ref, bp_ref, cc_ref, ac_ref, bc_ref, o_ref):
        i = pl.program_id(0)
        tabs_p = (cp_ref[...], ap_ref[...], bp_ref[...])
        tabs_c = (cc_ref[...], ac_ref[...], bc_ref[...])
        kvar, vvar, allowed, halves = _att_common(
            i, kp_ref[...], kc_ref[...], vp_ref[...], vc_ref[...], tabs_p, tabs_c)
        for pair in range(A_HEADS // 2):
            sl = slice(pair * LANES, (pair + 1) * LANES)
            q2 = _rope(q_ref[:, sl], *tabs_c) * scale
            acc = jnp.zeros((b, LANES), F32)
            for half in range(2):
                h = 2 * pair + half
                _, _, ks, var = _head_slots(h)
                qm = jnp.where(halves[half], q2, 0.0).astype(BF16)
                p, _ = _att_head_probs(qm, kvar[ks][var], allowed, sink_ref[h])
                o = lax.dot_general(p.astype(BF16), vvar[ks][var], NN, preferred_element_type=F32)
                acc = acc + jnp.where(halves[half], o, 0.0)
            o_ref[:, sl] = acc.astype(o_ref.dtype)

    cur = lambda col: (lambda i: (i, col))
    prev = lambda col: (lambda i: (jnp.maximum(i - 1, 0), col))
    tab = [pl.BlockSpec((b, LANES), prev(0))] * 3 + [pl.BlockSpec((b, LANES), cur(0))] * 3
    (out,), sent = _call(
        "attention_fwd", body, (nb,),
        [pl.BlockSpec(memory_space=pltpu.SMEM),
         pl.BlockSpec((b, A_WIDTH), cur(0)),
         pl.BlockSpec((b, A_KV_WIDTH), prev(OFF_KA // A_KV_WIDTH)),
         pl.BlockSpec((b, A_KV_WIDTH), cur(OFF_KA // A_KV_WIDTH)),
         pl.BlockSpec((b, A_KV_WIDTH), prev(OFF_VA // A_KV_WIDTH)),
         pl.BlockSpec((b, A_KV_WIDTH), cur(OFF_VA // A_KV_WIDTH))] + tab,
        [pl.BlockSpec((b, A_WIDTH), cur(0))],
        [jax.ShapeDtypeStruct((t, A_WIDTH), BF16)], [],
        [sinks, proj, proj, proj, proj, proj, *tables, *tables], comm)
    return out, sent


def _attention_bwd(proj, tables, sinks, d_out, comm=None):
    t = proj.shape[0]
    nb = t // ATT_BLOCK
    b = ATT_BLOCK
    scale = A_HEAD_DIM ** -0.5

    def body(sink_ref, q_ref, kp_ref, kc_ref, vp_ref, vc_ref,
             cp_ref, ap_ref, bp_ref, cc_ref, ac_ref, bc_ref, do_ref,
             dq_ref, dk_ref, dv_ref, dsink_ref, dk_carry, dv_carry):
        step = pl.program_id(0)
        i = nb - 1 - step

        @pl.when(step == 0)
        def _():
            dk_carry[...] = jnp.zeros(dk_carry.shape, F32)
            dv_carry[...] = jnp.zeros(dv_carry.shape, F32)
            dsink_ref[...] = jnp.zeros(dsink_ref.shape, F32)

        tabs_p = (cp_ref[...], ap_ref[...], bp_ref[...])
        tabs_c = (cc_ref[...], ac_ref[...], bc_ref[...])
        kvar, vvar, allowed, halves = _att_common(
            i, kp_ref[...], kc_ref[...], vp_ref[...], vc_ref[...], tabs_p, tabs_c)
        lane = lax.broadcasted_iota(jnp.int32, (1, LANES), 1)
        dk_slab = [jnp.zeros((2 * b, LANES), F32) for _ in range(2)]
        dv_slab = [jnp.zeros((2 * b, LANES), F32) for _ in range(2)]
        dsink = jnp.zeros((1, LANES), F32)
        for pair in range(A_HEADS // 2):
            sl = slice(pair * LANES, (pair + 1) * LANES)
            q2 = _rope(q_ref[:, sl], *tabs_c) * scale
            do2 = do_ref[:, sl].astype(F32)
            dq2 = jnp.zeros((b, LANES), F32)
            for half in range(2):
                h = 2 * pair + half
                _, _, ks, var = _head_slots(h)
                qm = jnp.where(halves[half], q2, 0.0).astype(BF16)
                dom = jnp.where(halves[half], do2, 0.0).astype(BF16)
                p, p_sink = _att_head_probs(qm, kvar[ks][var], allowed, sink_ref[h])
                dp = lax.dot_general(dom, vvar[ks][var], NT, preferred_element_type=F32)
                dsum = jnp.sum(p * dp, axis=-1, keepdims=True)
                ds = (p * (dp - dsum)).astype(BF16)
                dsink = dsink + jnp.where(lane == h, -jnp.sum(p_sink * dsum), 0.0)
                dqh = lax.dot_general(ds, kvar[ks][var], NN, preferred_element_type=F32)
                dq2 = dq2 + jnp.where(halves[half], dqh, 0.0)
                dkx = lax.dot_general(ds, qm, TN, preferred_element_type=F32)
                dvx = lax.dot_general(p.astype(BF16), dom, TN, preferred_element_type=F32)
                if var:
                    dkx = pltpu.roll(dkx, 64, 1)
                    dvx = pltpu.roll(dvx, 64, 1)
                dk_slab[ks] = dk_slab[ks] + dkx
                dv_slab[ks] = dv_slab[ks] + dvx
            dq_ref[:, sl] = _rope_bwd(dq2 * scale, *tabs_c).astype(dq_ref.dtype)
        for ks in range(2):
            sl = slice(ks * LANES, (ks + 1) * LANES)
            dk_cur = dk_slab[ks][b:] + dk_carry[:, sl]
            dk_ref[:, sl] = _rope_bwd(dk_cur, *tabs_c).astype(dk_ref.dtype)
            dv_ref[:, sl] = (dv_slab[ks][b:] + dv_carry[:, sl]).astype(dv_ref.dtype)
            dk_carry[:, sl] = dk_slab[ks][:b]
            dv_carry[:, sl] = dv_slab[ks][:b]
        dsink_ref[...] += dsink

    cur = lambda col: (lambda s: (nb - 1 - s, col))
    prev = lambda col: (lambda s: (jnp.maximum(nb - 2 - s, 0), col))
    tab = [pl.BlockSpec((b, LANES), prev(0))] * 3 + [pl.BlockSpec((b, LANES), cur(0))] * 3
    return _call(
        "attention_bwd", body, (nb,),
        [pl.BlockSpec(memory_space=pltpu.SMEM),
         pl.BlockSpec((b, A_WIDTH), cur(0)),
         pl.BlockSpec((b, A_KV_WIDTH), prev(OFF_KA // A_KV_WIDTH)),
         pl.BlockSpec((b, A_KV_WIDTH), cur(OFF_KA // A_KV_WIDTH)),
         pl.BlockSpec((b, A_KV_WIDTH), prev(OFF_VA // A_KV_WIDTH)),
         pl.BlockSpec((b, A_KV_WIDTH), cur(OFF_VA // A_KV_WIDTH))] + tab
        + [pl.BlockSpec((b, A_WIDTH), cur(0))],
        [pl.BlockSpec((b, A_WIDTH), cur(0)),
         pl.BlockSpec((b, A_KV_WIDTH), cur(0)),
         pl.BlockSpec((b, A_KV_WIDTH), cur(0)),
         pl.BlockSpec((1, LANES), lambda s: (0, 0))],
        [jax.ShapeDtypeStruct((t, A_WIDTH), BF16),
         jax.ShapeDtypeStruct((t, A_KV_WIDTH), BF16),
         jax.ShapeDtypeStruct((t, A_KV_WIDTH), BF16),
         jax.ShapeDtypeStruct((1, LANES), F32)],
        [pltpu.VMEM((b, A_KV_WIDTH), F32), pltpu.VMEM((b, A_KV_WIDTH), F32)],
        [sinks, proj, proj, proj, proj, proj, *tables, *tables, d_out], comm)


HGRN_HEADS_PER_STEP = 4
HGRN_W = HGRN_HEADS_PER_STEP * B_HEAD
HGRN_ROWS = 2 * CHUNK


def _hgrn_gates(qp, fp, hlb):
    lb = _sig(hlb[0:1] - hlb[1:2])
    sg = _sig(fp)
    big_f = lb + (1.0 - lb) * sg
    kk = (1.0 - lb) * (1.0 - sg)
    sq = _sig(qp)
    return lb, sg, big_f, kk, sq


def _tri(lower):
    r = lax.broadcasted_iota(jnp.int32, (CHUNK, CHUNK), 0)
    c = lax.broadcasted_iota(jnp.int32, (CHUNK, CHUNK), 1)
    return (r >= c) if lower else (r <= c)


def _hgrn_factors(q_ref, k_ref, b_ref, sl, r0):
    out = []
    whole = slice(r0, r0 + CHUNK)
    bh = b_ref[whole, sl]
    for i in range(CHUNK // SUB):
        rows = slice(r0 + i * SUB, r0 + (i + 1) * SUB)
        b0 = b_ref[r0 + i * SUB - 1:r0 + i * SUB, sl] if i else jnp.zeros((1, B_HEAD), F32)
        eq = jnp.exp(b_ref[rows, sl] - b0)
        ek = jnp.exp(jnp.minimum(b0 - bh, MAX_DECAY_EXP))
        out.append((q_ref[rows, sl] * eq, k_ref[whole, sl] * ek, eq, ek))
    return out


def _hgrn_scores(factors):
    rows = [lax.dot_general(qd.astype(BF16), kx.astype(BF16), NT, preferred_element_type=F32)
            for qd, kx, _, _ in factors]
    return jnp.where(_tri(True), jnp.concatenate(rows, axis=0), 0.0)


def _hgrn_specs(steps, reverse):
    w = HGRN_W
    block = (lambda c: steps - 1 - c) if reverse else (lambda c: c)
    col = lambda off: pl.BlockSpec((HGRN_ROWS, w), lambda hh, c: (block(c), off // w + hh))
    return block, col


def _chunk_cumsum(log_f, r0, lower):
    tri = _tri(lower).astype(F32)
    return lax.dot_general(tri, log_f[r0:r0 + CHUNK], NN, precision=lax.Precision.HIGHEST,
                           preferred_element_type=F32)


def _hgrn_fwd(proj, hlb, wn, comm=None):
    t = proj.shape[0]
    nc = t // CHUNK
    steps = t // HGRN_ROWS
    w = HGRN_W
    block, col = _hgrn_specs(steps, False)

    def body(q_ref, f_ref, v_ref, og_ref, hlb_ref, wn_ref, ob_ref, oraw_ref, st_ref,
             state, qs, ks, bs):
        c = pl.program_id(1)

        @pl.when(c == 0)
        def _():
            state[...] = jnp.zeros(state.shape, F32)

        qp = q_ref[...]
        lb, sg, big_f, kk, sq = _hgrn_gates(qp, f_ref[...], hlb_ref[...])
        qs[...] = qp * sq
        ks[...] = kk
        log_f = jnp.log(big_f)
        for sub in range(HGRN_ROWS // CHUNK):
            r0 = sub * CHUNK
            rows = slice(r0, r0 + CHUNK)
            bs[rows] = _chunk_cumsum(log_f, r0, True)
            for h in range(HGRN_HEADS_PER_STEP):
                sl = slice(h * B_HEAD, (h + 1) * B_HEAD)
                bh = bs[rows, sl]
                qh = qs[rows, sl]
                kh = ks[rows, sl]
                vh = v_ref[rows, sl].astype(BF16)
                s0 = state[h]
                st_ref[sub, h] = s0.astype(st_ref.dtype)
                a = _hgrn_scores(_hgrn_factors(qs, ks, bs, sl, r0))
                o = lax.dot_general(a.astype(BF16), vh, NN, preferred_element_type=F32)
                o = o + lax.dot_general((qh * jnp.exp(bh)).astype(BF16), s0.astype(BF16), NT,
                                        preferred_element_type=F32)
                bl = bs[r0 + CHUNK - 1:r0 + CHUNK, sl]
                khat = (kh * jnp.exp(bl - bh)).astype(BF16)
                state[h] = s0 * jnp.exp(bl) + lax.dot_general(vh, khat, TN, preferred_element_type=F32)
                oraw_ref[rows, sl] = o
                r = lax.rsqrt(jnp.mean(o * o, axis=-1, keepdims=True) + EPS)
                og = og_ref[rows, sl]
                ob_ref[rows, sl] = (o * r * wn_ref[:, sl] * (og * _sig(og))).astype(ob_ref.dtype)

    vec = lambda rows: pl.BlockSpec((rows, w), lambda hh, c: (0, hh))
    tile = pl.BlockSpec((HGRN_ROWS, w), lambda hh, c: (c, hh))
    return _call(
        "hgrn_fwd", body, (B_WIDTH // w, steps),
        [col(OFF_QB), col(OFF_FB), col(OFF_IB), col(OFF_OG), vec(2), vec(1)],
        [tile, tile,
         pl.BlockSpec((HGRN_ROWS // CHUNK, HGRN_HEADS_PER_STEP, B_HEAD, B_HEAD), lambda hh, c: (c, hh, 0, 0))],
        [jax.ShapeDtypeStruct((t, B_WIDTH), BF16),
         jax.ShapeDtypeStruct((t, B_WIDTH), F32),
         jax.ShapeDtypeStruct((nc, B_WIDTH // B_HEAD, B_HEAD, B_HEAD), BF16)],
        [pltpu.VMEM((HGRN_HEADS_PER_STEP, B_HEAD, B_HEAD), F32)] + [pltpu.VMEM((HGRN_ROWS, w), F32)] * 3,
        [proj, proj, proj, proj, hlb, wn], comm)


def _hgrn_bwd(proj, hlb, wn, o_raw, states, d_out, comm=None):
    t = proj.shape[0]
    steps = t // HGRN_ROWS
    w = HGRN_W
    block, col = _hgrn_specs(steps, True)

    def body(q_ref, f_ref, v_ref, og_ref, hlb_ref, wn_ref, oraw_ref, st_ref, dout_ref,
             dq_ref, df_ref, dv_ref, dog_ref, dlb_ref, dwn_ref,
             dstate, qs, ks, bs, dbs, dks, dqs, dgs):
        c = pl.program_id(1)

        @pl.when(c == 0)
        def _():
            dstate[...] = jnp.zeros(dstate.shape, F32)
            dlb_ref[...] = jnp.zeros(dlb_ref.shape, F32)
            dwn_ref[...] = jnp.zeros(dwn_ref.shape, F32)

        qp = q_ref[...]
        lb, sg, big_f, kk, sq = _hgrn_gates(qp, f_ref[...], hlb_ref[...])
        qs[...] = qp * sq
        ks[...] = kk
        log_f = jnp.log(big_f)
        last_row = lax.broadcasted_iota(jnp.int32, (CHUNK, 1), 0) == CHUNK - 1
        for sub in reversed(range(HGRN_ROWS // CHUNK)):
            r0 = sub * CHUNK
            rows = slice(r0, r0 + CHUNK)
            bs[rows] = _chunk_cumsum(log_f, r0, True)
            for h in range(HGRN_HEADS_PER_STEP):
                sl = slice(h * B_HEAD, (h + 1) * B_HEAD)
                bh = bs[rows, sl]
                qh = qs[rows, sl]
                kh = ks[rows, sl]
                vh = v_ref[rows, sl].astype(BF16)
                s0 = st_ref[sub, h]
                ds1 = dstate[h]
                o = oraw_ref[rows, sl]
                og = og_ref[rows, sl]
                wnh = wn_ref[:, sl]
                sog = _sig(og)
                r = lax.rsqrt(jnp.mean(o * o, axis=-1, keepdims=True) + EPS)
                dob = dout_ref[rows, sl].astype(F32)
                don = dob * (og * sog)
                dog_ref[rows, sl] = (dob * (o * r * wnh) * (sog * (1.0 + og * (1.0 - sog)))).astype(dog_ref.dtype)
                dwn_ref[:, sl] += jnp.sum(don * o * r, axis=0, keepdims=True)
                gp = don * wnh
                do = r * (gp - o * (r * r) * jnp.mean(gp * o, axis=-1, keepdims=True))
                do16 = do.astype(BF16)
                factors = _hgrn_factors(qs, ks, bs, sl, r0)
                a = _hgrn_scores(factors)
                da = jnp.where(_tri(True), lax.dot_general(do16, vh, NT, preferred_element_type=F32), 0.0)
                dv = lax.dot_general(a.astype(BF16), do16, TN, preferred_element_type=F32)
                dq_rows = []
                dk = jnp.zeros((CHUNK, B_HEAD), F32)
                for i, (qd, kx, eq, ek) in enumerate(factors):
                    da_i = da[i * SUB:(i + 1) * SUB]
                    dq_rows.append(lax.dot_general(da_i, kx, NN, precision=lax.Precision.HIGHEST,
                                                   preferred_element_type=F32) * eq)
                    dk = dk + lax.dot_general(da_i, qd, TN, precision=lax.Precision.HIGHEST,
                                              preferred_element_type=F32) * ek
                eb = jnp.exp(bh)
                bl = bs[r0 + CHUNK - 1:r0 + CHUNK, sl]
                ebl = jnp.exp(bl)
                ekl = jnp.exp(bl - bh)
                qe = (qh * eb).astype(BF16)
                khat = (kh * ekl).astype(BF16)
                ds1_16 = ds1.astype(BF16)
                dq = jnp.concatenate(dq_rows, axis=0) + eb * lax.dot_general(
                    do16, s0, NN, preferred_element_type=F32)
                dv = dv + lax.dot_general(khat, ds1_16, NT, preferred_element_type=F32)
                dk_state = ekl * lax.dot_general(vh, ds1_16, NN, preferred_element_type=F32)
                dk = dk + dk_state
                db_last = (jnp.sum(kh * dk_state, axis=0, keepdims=True)
                           + ebl * jnp.sum(s0.astype(F32) * ds1, axis=0, keepdims=True))
                dbs[rows, sl] = qh * dq - kh * dk + jnp.where(last_row, db_last, 0.0)
                dks[rows, sl] = dk
                dqs[rows, sl] = dq
                dstate[h] = ds1 * ebl + lax.dot_general(do16, qe, TN, preferred_element_type=F32)
                dv_ref[rows, sl] = dv.astype(dv_ref.dtype)
            dgs[rows] = _chunk_cumsum(dbs, r0, False)
        dq_ref[...] = (dqs[...] * (sq * (1.0 + qp * (1.0 - sq)))).astype(dq_ref.dtype)
        df_minus_dk = dgs[...] / big_f - dks[...]
        df_ref[...] = ((1.0 - lb) * df_minus_dk * sg * (1.0 - sg)).astype(df_ref.dtype)
        dlb_ref[...] += jnp.sum((1.0 - sg) * df_minus_dk, axis=0, keepdims=True) * (lb * (1.0 - lb))

    vec = lambda rows: pl.BlockSpec((rows, w), lambda hh, c: (0, hh))
    blk = pl.BlockSpec((HGRN_ROWS, w), lambda hh, c: (block(c), hh))
    act = jax.ShapeDtypeStruct((t, B_WIDTH), BF16)
    row = jax.ShapeDtypeStruct((1, B_WIDTH), F32)
    return _call(
        "hgrn_bwd", body, (B_WIDTH // w, steps),
        [col(OFF_QB), col(OFF_FB), col(OFF_IB), col(OFF_OG), vec(2), vec(1), blk,
         pl.BlockSpec((HGRN_ROWS // CHUNK, HGRN_HEADS_PER_STEP, B_HEAD, B_HEAD),
                      lambda hh, c: (block(c), hh, 0, 0)),
         blk],
        [blk, blk, blk, blk, vec(1), vec(1)],
        [act, act, act, act, row, row],
        [pltpu.VMEM((HGRN_HEADS_PER_STEP, B_HEAD, B_HEAD), F32)] + [pltpu.VMEM((HGRN_ROWS, w), F32)] * 7,
        [proj, proj, proj, proj, hlb, wn, o_raw, states, d_out], comm)


def _mixer_fwd(x1, mix_norm, w_in, sinks, hlb, wn, weights, tables, comms):
    t, d = x1.shape
    n_in = w_in[0].shape[0]
    nb = d // N_DEV
    sent = {}
    h2 = _rmsnorm("mix_norm", x1, mix_norm)
    tm = _tile(t, 1024, 16)
    tn = _tile(n_in, 512, 128)
    (proj,), sent["proj"] = _mmc(
        "mix_proj", (t // tm, n_in // tn, 1),
        [(h2, (tm, d // 2), lambda i, j, k: (i, 0)), (w_in[0], (tn, d // 2), lambda i, j, k: (j, 0)),
         (h2, (tm, d // 2), lambda i, j, k: (i, 1)), (w_in[1], (tn, d // 2), lambda i, j, k: (j, 0))],
        [(0, 1, 0, NT), (2, 3, 0, NT)], [(tm, tn)], lambda accs, refs, rows: accs,
        [((t, n_in), F32, (tm, tn), lambda i, j, k: (i, j), False)], comm=comms.get("proj"))
    out_a, sent["attention"] = _attention_fwd(proj, tables, sinks, comms.get("attention"))
    (out_b, o_raw, states), sent["hgrn"] = _hgrn_fwd(proj, hlb, wn, comms.get("hgrn"))
    wua, wub, w_out = weights(sent)

    def epi_merge(accs, refs, rows):
        ua, ub = accs
        return [_sig(refs[4][rows]) * ua + _sig(refs[5][rows]) * ub, ua, ub]

    wide = ((t, d), BF16, (tm, nb), lambda i, j, k: (i, j), False)
    (merged, ua, ub), sent["merge"] = _mmc(
        "mix_merge", (t // tm, N_DEV, 1),
        [(out_a, (tm, A_WIDTH), lambda i, j, k: (i, 0)),
         (wua, (None, A_WIDTH, nb), lambda i, j, k: (j, 0, 0)),
         (out_b, (tm, B_WIDTH), lambda i, j, k: (i, 0)),
         (wub, (None, B_WIDTH, nb), lambda i, j, k: (j, 0, 0)),
         (proj, (tm, nb), lambda i, j, k: (i, OFF_GA // nb + j)),
         (proj, (tm, nb), lambda i, j, k: (i, (OFF_GA + d) // nb + j))],
        [(0, 1, 0, NN), (2, 3, 1, NN)], [(tm, nb)] * 2, epi_merge, [wide] * 3, comm=comms.get("merge"))

    tn2 = _tile(d, 1024, 128)
    (x2,) = _mm(
        "mix_out", (t // tm, d // tn2, 1),
        [(merged, (tm, d), lambda i, j, k: (i, 0)), (w_out, (d, tn2), lambda i, j, k: (0, j)),
         (x1, (tm, tn2), lambda i, j, k: (i, j))],
        [(0, 1, 0, NN)], [(tm, tn2)], lambda accs, refs, rows: [refs[2][rows] + accs[0]],
        [((t, d), F32, (tm, tn2), lambda i, j, k: (i, j), False)])
    saved = dict(h2=h2, proj=proj, out_a=out_a, out_b=out_b, o_raw=o_raw, states=states,
                 merged=merged, ua=ua, ub=ub)
    return x2, saved, sent


def _mixer_bwd(dx2, dx2_16, x1, mix_norm, w_in, sinks, hlb, wn, wua, wub, w_out, tables, sv, comms):
    t, d = x1.shape
    sent = {}
    avail = dict(sent=sent)
    n_in = w_in[0].shape[0]
    nb = d // N_DEV
    proj = sv["proj"]
    tm = _tile(t, 1024, 16)
    tn = _tile(d, 512, 128)

    def epi_gate(accs, refs, rows):
        dm = accs[0]
        sa = _sig(refs[2][rows])
        sb = _sig(refs[3][rows])
        ua = refs[4][rows].astype(F32)
        ub = refs[5][rows].astype(F32)
        return [dm * sa, dm * sb, dm * ua * sa * (1.0 - sa), dm * ub * sb * (1.0 - sb)]

    wide = ((t, d), BF16, (tm, tn), lambda i, j, k: (i, j), False)
    dua, dub, dga, dgb = _mm(
        "mix_dmerge", (t // tm, d // tn, 1),
        [(dx2_16, (tm, d), lambda i, j, k: (i, 0)),
         (w_out, (tn, d), lambda i, j, k: (j, 0)),
         (proj, (tm, tn), lambda i, j, k: (i, OFF_GA // tn + j)),
         (proj, (tm, tn), lambda i, j, k: (i, (OFF_GA + d) // tn + j)),
         (sv["ua"], (tm, tn), lambda i, j, k: (i, j)),
         (sv["ub"], (tm, tn), lambda i, j, k: (i, j))],
        [(0, 1, 0, NT)], [(tm, tn)], epi_gate, [wide] * 4)

    tk = _tile(t, 1024, 16)
    two = _tile(d, 1024, 128)
    (dw_out,) = _mm(
        "mix_dwout", (d // two, 1, t // tk),
        [(sv["merged"], (tk, two), lambda i, j, k: (k, i)), (dx2_16, (tk, d), lambda i, j, k: (k, 0))],
        [(0, 1, 0, TN)], [(two, d)], lambda accs, refs, rows: accs,
        [((d, d), BF16, (two, d), lambda i, j, k: (i, 0), False)])

    w_spec = lambda rows: ((N_DEV, rows, nb), BF16, (None, rows, nb), lambda i, j, k: (j, 0, 0), False)
    tk2 = _tile(t, 2048, 16)
    dwua, dwub = _mm(
        "mix_dwup", (1, N_DEV, t // tk2),
        [(sv["out_a"], (tk2, A_WIDTH), lambda i, j, k: (k, 0)),
         (dua, (tk2, nb), lambda i, j, k: (k, j)),
         (sv["out_b"], (tk2, B_WIDTH), lambda i, j, k: (k, 0)),
         (dub, (tk2, nb), lambda i, j, k: (k, j))],
        [(0, 1, 0, TN), (2, 3, 1, TN)], [(A_WIDTH, nb), (B_WIDTH, nb)], lambda accs, refs, rows: accs,
        [w_spec(A_WIDTH), w_spec(B_WIDTH)])

    d_out_a, d_out_b = _mm(
        "mix_dup", (t // tm, 1, N_DEV),
        [(dua, (tm, nb), lambda i, j, k: (i, k)),
         (wua, (None, A_WIDTH, nb), lambda i, j, k: (k, 0, 0)),
         (dub, (tm, nb), lambda i, j, k: (i, k)),
         (wub, (None, B_WIDTH, nb), lambda i, j, k: (k, 0, 0))],
        [(0, 1, 0, NT), (2, 3, 1, NT)], [(tm, A_WIDTH), (tm, B_WIDTH)], lambda accs, refs, rows: accs,
        [((t, A_WIDTH), BF16, (tm, A_WIDTH), lambda i, j, k: (i, 0), False),
         ((t, B_WIDTH), BF16, (tm, B_WIDTH), lambda i, j, k: (i, 0), False)])

    avail.update(w_out=dw_out.reshape(N_DEV, d // N_DEV, d), w_up_a=dwua, w_up_b=dwub)
    (dq_a, dk_a, dv_a, dsinks), sent["attention"] = _attention_bwd(
        proj, tables, sinks, d_out_a, _carried(comms, "attention", avail))
    (dq_b, df_b, di_b, dog_b, dlb, dwn), sent["hgrn"] = _hgrn_bwd(
        proj, hlb, wn, sv["o_raw"], sv["states"], d_out_b, _carried(comms, "hgrn", avail))
    dproj = jnp.concatenate([dq_a, dk_a, dv_a, dq_b, df_b, di_b, dog_b, dga, dgb], axis=1)

    tdm = _tile(d, 1024, 128)
    twn = _tile(n_in, 2432, 128)
    tkw = _tile(t, 512, 16)
    (dw_in,) = _mm(
        "mix_dwin", (n_in // twn, d // tdm, t // tkw),
        [(dproj, (tkw, twn), lambda i, j, k: (k, i)), (sv["h2"], (tkw, tdm), lambda i, j, k: (k, j))],
        [(0, 1, 0, TN)], [(twn, tdm)], lambda accs, refs, rows: accs,
        [((n_in, d), BF16, (twn, tdm), lambda i, j, k: (i, j), False)])

    avail.update(w_in=dw_in.reshape(N_DEV, n_in // N_DEV, d))
    tm3 = _tile(t, 512, 16)
    tpn = _tile(n_in, 512, 128)

    def epi_dx(accs, refs, rows):
        dh = jnp.concatenate(accs, axis=1)
        dx, dw = _rms_bwd(dh, refs[3][rows], refs[4][...], refs[5][rows])
        return [dx, dx, dw]

    (dx1, dx1_16, dmix), sent["dx"] = _mmc(
        "mix_dx", (t // tm3, 1, n_in // tpn),
        [(dproj, (tm3, tpn), lambda i, j, k: (i, k)),
         (w_in[0], (tpn, d // 2), lambda i, j, k: (k, 0)), (w_in[1], (tpn, d // 2), lambda i, j, k: (k, 0)),
         (x1, (tm3, d), lambda i, j, k: (i, 0)), (mix_norm, (1, d), lambda i, j, k: (0, 0)),
         (dx2, (tm3, d), lambda i, j, k: (i, 0))],
        [(0, 1, 0, NN), (0, 2, 1, NN)], [(tm3, d // 2)] * 2, epi_dx,
        [((t, d), F32, (tm3, d), lambda i, j, k: (i, 0), False),
         ((t, d), BF16, (tm3, d), lambda i, j, k: (i, 0), False),
         ((1, d), F32, (1, d), lambda i, j, k: (0, 0), True)],
        comm=_carried(comms, "dx", avail))
    small = dict(mix_norm=dmix, attn_sinks=dsinks, hgrn_lb=dlb, hgrn_norm=dwn)
    return dx1, dx1_16, small, avail


def _ple_fwd(x3, ple_norm, p16, wpg, wpp):
    t, d = x3.shape
    nb = d // N_DEV
    h4 = _rmsnorm("ple_norm", x3, ple_norm)
    tm = _tile(t, 1024, 16)

    def epi(accs, refs, rows):
        zg, pp = accs
        return [refs[4][rows] + _sig(zg) * pp, zg, pp]

    blk = lambda dt: ((t, d), dt, (tm, nb), lambda i, j, k: (i, j), False)
    x4, zg, pp = _mm(
        "ple_fwd", (t // tm, N_DEV, 1),
        [(h4, (tm, d), lambda i, j, k: (i, 0)), (wpg, (d, nb), lambda i, j, k: (0, j)),
         (p16, (tm, PLE_DIM), lambda i, j, k: (i, 0)),
         (wpp, (None, PLE_DIM, nb), lambda i, j, k: (j, 0, 0)),
         (x3, (tm, nb), lambda i, j, k: (i, j))],
        [(0, 1, 0, NN), (2, 3, 1, NN)], [(tm, nb)] * 2, epi, [blk(F32), blk(BF16), blk(BF16)])
    return x4, dict(h4=h4, zg=zg, pp=pp)


def _ple_bwd(dx4, x3, ple_norm, p16, wpg, sv):
    t, d = x3.shape
    nb = d // N_DEV

    def gate_grads(dx, zg, pp):
        s = _sig(zg.astype(F32))
        return dx * pp.astype(F32) * s * (1.0 - s), dx * s

    dzg, dpp = _rows("ple_dgate", gate_grads, [(dx4, True), (sv["zg"], True), (sv["pp"], True)],
                     [((t, d), BF16), ((t, d), BF16)], _tile(t, 512, 16))
    (dwpp,) = _mm(
        "ple_dwproj", (1, N_DEV, 1),
        [(p16, (t, PLE_DIM), lambda i, j, k: (0, 0)), (dpp, (t, nb), lambda i, j, k: (0, j))],
        [(0, 1, 0, TN)], [(PLE_DIM, nb)], lambda accs, refs, rows: accs,
        [((N_DEV, PLE_DIM, nb), BF16, (None, PLE_DIM, nb), lambda i, j, k: (j, 0, 0), False)])
    tk = _tile(t, 1024, 16)
    tn = _tile(d, 1024, 128)
    (dwpg,) = _mm(
        "ple_dwgate", (d // tn, 1, t // tk),
        [(sv["h4"], (tk, tn), lambda i, j, k: (k, i)), (dzg, (tk, d), lambda i, j, k: (k, 0))],
        [(0, 1, 0, TN)], [(tn, d)], lambda accs, refs, rows: accs,
        [((d, d), BF16, (tn, d), lambda i, j, k: (i, 0), False)])
    tm3 = _tile(t, 256, 16)

    def epi_dx(accs, refs, rows):
        dx, dw = _rms_bwd(accs[0], refs[2][rows], refs[3][...], refs[4][rows])
        return [dx, dx, dw]

    dx3, dx3_16, dnorm = _mm(
        "ple_dx", (t // tm3, 1, 1),
        [(dzg, (tm3, d), lambda i, j, k: (i, 0)), (wpg, (d, d), lambda i, j, k: (0, 0)),
         (x3, (tm3, d), lambda i, j, k: (i, 0)), (ple_norm, (1, d), lambda i, j, k: (0, 0)),
         (dx4, (tm3, d), lambda i, j, k: (i, 0))],
        [(0, 1, 0, NT)], [(tm3, d)], epi_dx,
        [((t, d), F32, (tm3, d), lambda i, j, k: (i, 0), False),
         ((t, d), BF16, (tm3, d), lambda i, j, k: (i, 0), False),
         ((1, d), F32, (1, d), lambda i, j, k: (0, 0), True)])
    return dx3, dx3_16, dwpp, dwpg, dnorm


def _loss_head(x4, final_norm, target):
    t, d = x4.shape
    tm = _tile(t, 256, 8)

    def body(x_ref, w_ref, tgt_ref, dx_ref, dw_ref, loss_ref):
        i = pl.program_id(0)
        xv = x_ref[...]
        wv = w_ref[...]
        r = lax.rsqrt(jnp.mean(xv * xv, axis=-1, keepdims=True) + EPS)
        err = xv * r * wv - tgt_ref[...]
        part = 0.5 * jnp.sum(jnp.mean(err * err, axis=-1, keepdims=True), axis=0, keepdims=True)
        dx, dw = _rms_bwd(err * (1.0 / d), xv, wv, jnp.zeros_like(xv))
        dx_ref[...] = dx

        @pl.when(i == 0)
        def _():
            dw_ref[...] = dw
            loss_ref[...] = jnp.broadcast_to(part, loss_ref.shape)

        @pl.when(i > 0)
        def _():
            dw_ref[...] += dw
            loss_ref[...] += jnp.broadcast_to(part, loss_ref.shape)

    return pl.pallas_call(
        body,
        name="loss_head",
        grid=(t // tm,),
        in_specs=[pl.BlockSpec((tm, d), lambda i: (i, 0)), pl.BlockSpec((1, d), lambda i: (0, 0)),
                  pl.BlockSpec((tm, d), lambda i: (i, 0))],
        out_specs=[pl.BlockSpec((tm, d), lambda i: (i, 0)), pl.BlockSpec((1, d), lambda i: (0, 0)),
                   pl.BlockSpec((1, LANES), lambda i: (0, 0))],
        out_shape=[jax.ShapeDtypeStruct((t, d), F32), jax.ShapeDtypeStruct((1, d), F32),
                   jax.ShapeDtypeStruct((1, LANES), F32)],
        compiler_params=_params(("arbitrary",)),
    )(x4, final_norm, target)


RELATIONS = ((0, 0), (1, 0), (0, 1), (1, 1))


def _place():
    return lax.axis_index("x"), lax.axis_index("y"), lax.axis_index("c")


def _flip(v, bit):
    return 1 - v if bit else v


def _gather_program(shards, mid_at):
    n = len(shards)
    slots = 8

    def copies(ins, outs, send_sems, recv_sems, local_sems, base):
        x, y, c = _place()
        sibling, x_chip, y_chip = (x, y, 1 - c), (1 - x, y, c), (x, 1 - y, c)

        def block(a, r, core, half=None):
            ref = outs[a].at[4 * _flip(x, RELATIONS[r][0]) + 2 * _flip(y, RELATIONS[r][1]) + core]
            rows = shards[a].shape[0] // 2
            return ref if half is None else ref.at[pl.ds(half * rows, rows)]

        def copy(a, slot, dst, to, src=None):
            return pltpu.make_async_remote_copy(
                src_ref=dst if src is None else src, dst_ref=dst,
                send_sem=send_sems.at[base[0] + slots * a + slot],
                recv_sem=recv_sems.at[base[1] + slots * a + slot], device_id=to, device_id_type=MESH)

        own = [pltpu.make_async_copy(ins[a], block(a, 0, c), local_sems.at[base[2] + a]) for a in range(n)]
        sent_first = [cp for a in range(n) for cp in (
            copy(a, 0, block(a, 0, c), sibling, src=ins[a]),
            copy(a, 1, block(a, 0, c), x_chip, src=ins[a]),
            copy(a, 2, block(a, 0, c), y_chip, src=ins[a]))]
        landed_first = [cp for a in range(n) for cp in (
            copy(a, 1, block(a, 1, c), x_chip), copy(a, 2, block(a, 2, c), y_chip))]
        sent_mid = [cp for a in range(n) for cp in (
            copy(a, 3, block(a, 1, c, 0), y_chip), copy(a, 4, block(a, 2, c, 1), x_chip),
            copy(a, 5, block(a, 1, c), sibling), copy(a, 6, block(a, 2, c), sibling))]
        landed_mid = [cp for a in range(n) for cp in (
            copy(a, 3, block(a, 3, c, 0), y_chip), copy(a, 4, block(a, 3, c, 1), x_chip))]
        sent_last = [copy(a, 7, block(a, 3, c), sibling) for a in range(n)]
        landed_sibling = [cp for a in range(n) for cp in (
            copy(a, 0, block(a, 0, 1 - c), sibling), copy(a, 5, block(a, 1, 1 - c), sibling),
            copy(a, 6, block(a, 2, 1 - c), sibling), copy(a, 7, block(a, 3, 1 - c), sibling))]
        return own, sent_first, landed_first, sent_mid, landed_mid, sent_last, landed_sibling

    def pre(*args):
        own, sent_first = copies(*args)[:2]
        for cp in own + sent_first:
            cp.start()

    def mid(*args):
        _, _, landed_first, sent_mid, _, _, _ = copies(*args)
        for cp in landed_first:
            cp.wait_recv()
        for cp in sent_mid:
            cp.start()

    def late(*args):
        _, _, _, _, landed_mid, sent_last, _ = copies(*args)
        for cp in landed_mid:
            cp.wait_recv()
        for cp in sent_last:
            cp.start()

    def post(*args):
        own, sent_first, _, sent_mid, _, sent_last, landed_sibling = copies(*args)
        for cp in landed_sibling:
            cp.wait_recv()
        for cp in sent_first + sent_mid + sent_last:
            cp.wait_send()
        for mine in own:
            mine.wait()

    return _Comm(list(shards), [jax.ShapeDtypeStruct((N_DEV,) + s.shape, s.dtype) for s in shards],
                 (slots * n, slots * n, n), pre, post, mid, mid_at, late)


def _exchange_program(arrays, n_slots, source, target, slot):
    n = len(arrays)

    def copies(ins, outs, send_sems, recv_sems, local_sems, base):
        place = _place()
        res = []
        for a in range(n):
            for r in range(n_slots):
                src = source(r, place)
                if src is None:
                    continue
                res.append(pltpu.make_async_remote_copy(
                    src_ref=ins[a].at[src], dst_ref=outs[a].at[slot(r)],
                    send_sem=send_sems.at[base[0] + n_slots * a + r],
                    recv_sem=recv_sems.at[base[1] + n_slots * a + r],
                    device_id=target(r, place), device_id_type=MESH))
        return res

    def pre(*args):
        for cp in copies(*args):
            cp.start()

    def post(*args):
        for cp in copies(*args):
            cp.wait()

    n_out = len({slot(r) for r in range(n_slots) if source(r, (0, 0, 0)) is not None})
    return _Comm(list(arrays), [jax.ShapeDtypeStruct((n_out,) + g.shape[1:], g.dtype) for g in arrays],
                 (n_slots * n, n_slots * n, 0), pre, post)


def _sibling_program(grads):
    def source(r, place):
        x, y, c = place
        return 4 * _flip(x, RELATIONS[r][0]) + 2 * _flip(y, RELATIONS[r][1]) + (1 - c)

    return _exchange_program(grads, 4, source, lambda r, p: (p[0], p[1], 1 - p[2]), lambda r: r)


def _chips_program(sums, relations=(1, 2, 3)):
    def target(r, place):
        x, y, c = place
        return (_flip(x, RELATIONS[r][0]), _flip(y, RELATIONS[r][1]), c)

    return _exchange_program(sums, 4, lambda r, p: r if r in relations else None, target,
                             lambda r: relations.index(r) if r in relations else 0)


def _standalone(name, comm):
    return _call(name, lambda: None, (1,), [], [], [], [], [], comm)[1]


def _pair_sum(name, grad, from_sibling, block_ids):
    _, rows, cols = grad.shape
    tr = _tile(rows, 512, 16)

    def body(ids_ref, mine_ref, sib_ref, o_ref):
        o_ref[...] = (mine_ref[...].astype(F32) + sib_ref[...].astype(F32)).astype(o_ref.dtype)

    return pl.pallas_call(
        body,
        name=name,
        grid_spec=pltpu.PrefetchScalarGridSpec(
            num_scalar_prefetch=1,
            grid=(4, rows // tr),
            in_specs=[pl.BlockSpec((None, tr, cols), lambda r, i, ids: (ids[r], i, 0)),
                      pl.BlockSpec((None, tr, cols), lambda r, i, ids: (r, i, 0))],
            out_specs=pl.BlockSpec((None, tr, cols), lambda r, i, ids: (r, i, 0))),
        out_shape=jax.ShapeDtypeStruct((4, rows, cols), BF16),
        compiler_params=_params(("arbitrary", "arbitrary")),
    )(block_ids, grad, from_sibling)


def _adam(w, g, m, v):
    m = ADAM_B1 * m + (1.0 - ADAM_B1) * g
    v = ADAM_B2 * v + (1.0 - ADAM_B2) * (g * g)
    m_hat = m / (1.0 - ADAM_B1 ** ADAM_STEP)
    v_hat = v / (1.0 - ADAM_B2 ** ADAM_STEP)
    delta = -ADAM_LR * (m_hat / (jnp.sqrt(v_hat) + ADAM_EPS) + ADAM_WD * w)
    return delta, m, v


def _adam_sharded(name, items, comm=None):
    _, rows, cols = items[0][0].shape
    n = len(items)
    tr = _tile(rows, max(64, 256 // n), 8)

    def body(*refs):
        for k in range(n):
            w_ref, m_ref, v_ref, s_ref, r1_ref, r2_ref, r3_ref = refs[7 * k:7 * k + 7]
            g_ref, d_ref, nm_ref, nv_ref = refs[7 * n + 4 * k:7 * n + 4 * k + 4]
            g = ((s_ref[...].astype(F32) + r1_ref[...].astype(F32)) + r2_ref[...].astype(F32)) + r3_ref[...].astype(F32)
            delta, nm, nv = _adam(w_ref[...], g, m_ref[...], v_ref[...])
            g_ref[...] = g
            d_ref[...] = delta
            nm_ref[...] = nm
            nv_ref[...] = nv

    slot = lambda s: pl.BlockSpec((None, tr, cols), lambda i: (s, i, 0))
    flat = slot(0)
    args, specs = [], []
    for w, m, v, sums, from_chips in items:
        args += [w, m, v, sums] + [a for a, _ in from_chips]
        specs += [flat, flat, flat, slot(0)] + [slot(s) for _, s in from_chips]
    res, sent = _call(
        name, body, (rows // tr,), specs, [flat] * (4 * n),
        [jax.ShapeDtypeStruct((1, rows, cols), F32)] * (4 * n), [], args, comm)
    return [res[4 * k:4 * k + 4] for k in range(n)], sent


def _all_reduce_small(part):
    rows = part.shape[0]

    def body(x_ref, o_ref, gathered, send_sems, recv_sems):
        x, y, c = _place()
        me = 4 * x + 2 * y + c
        gathered[me] = x_ref[...]
        copies = []
        for k in range(1, N_DEV):
            bits = (k >> 2 & 1, k >> 1 & 1, k & 1)
            peer = (_flip(x, bits[0]), _flip(y, bits[1]), _flip(c, bits[2]))
            copies.append(pltpu.make_async_remote_copy(
                src_ref=x_ref, dst_ref=gathered.at[me],
                send_sem=send_sems.at[k - 1], recv_sem=recv_sems.at[k - 1],
                device_id=peer, device_id_type=MESH))
        for cp in copies:
            cp.start()
        for cp in copies:
            cp.wait()
        total = gathered[0]
        for b in range(1, N_DEV):
            total = total + gathered[b]
        o_ref[...] = total

    return pl.pallas_call(
        body,
        name="small_all_reduce",
        in_specs=[pl.BlockSpec(memory_space=pltpu.VMEM)],
        out_specs=pl.BlockSpec(memory_space=pltpu.VMEM),
        out_shape=jax.ShapeDtypeStruct((rows, LANES), F32),
        scratch_shapes=[pltpu.VMEM((N_DEV, rows, LANES), F32),
                        pltpu.SemaphoreType.DMA((N_DEV - 1,)), pltpu.SemaphoreType.DMA((N_DEV - 1,))],
    )(part)


def _adam_small(w, g, m, v):
    def fn(wv, gv, mv, vv):
        return _adam(wv, gv, mv, vv)

    shape = (w.shape, F32)
    return _rows("adam_small", fn, [(w, True), (g, True), (m, True), (v, True)], [shape] * 3, w.shape[0])


SMALL = ("ffn1_norm", "mix_norm", "attn_sinks", "hgrn_lower_bound", "hgrn_norm", "ffn2_norm", "ple_norm",
         "final_norm")
LARGE = ("ffn1_w_gate", "ffn1_w_up", "ffn1_w_down", "w_in", "w_up_a", "w_up_b", "w_out",
         "ffn2_w_gate", "ffn2_w_up", "ffn2_w_down", "ple_w_gate", "ple_w_proj")
TRANSPOSED = ("ffn1_w_gate", "ffn1_w_up", "ffn2_w_gate", "ffn2_w_up", "w_in")
WEIGHTS = ("ffn1_norm", "ffn1_w_gate", "ffn1_w_up", "ffn1_w_down", "mix_norm", "w_in", "attn_sinks",
           "hgrn_lower_bound", "hgrn_norm", "w_up_a", "w_up_b", "w_out", "ffn2_norm", "ffn2_w_gate",
           "ffn2_w_up", "ffn2_w_down", "ple_norm", "ple_w_gate", "ple_w_proj", "final_norm")


def _pack_rows(arrays):
    rows = []
    for a in arrays:
        flat = a.reshape(-1).astype(F32)
        pad = -flat.shape[0] % LANES
        rows.append(jnp.pad(flat, (0, pad)).reshape(-1, LANES))
    packed = jnp.concatenate(rows, axis=0)
    return jnp.pad(packed, ((0, -packed.shape[0] % 8), (0, 0)))


def _unpack_rows(packed, like):
    out, at = [], 0
    for a in like:
        size = a.size
        n_rows = -(-size // LANES)
        out.append(packed[at:at + n_rows].reshape(-1)[:size].reshape(a.shape))
        at += n_rows
    return out


def kernel(x, p, positions, ffn1_norm, ffn1_w_gate, ffn1_w_up, ffn1_w_down, mix_norm, w_in, attn_sinks, hgrn_lower_bound, hgrn_norm, w_up_a, w_up_b, w_out, ffn2_norm, ffn2_w_gate, ffn2_w_up, ffn2_w_down, ple_norm, ple_w_gate, ple_w_proj, final_norm, loss_target, m_ffn1_norm, m_ffn1_w_gate, m_ffn1_w_up, m_ffn1_w_down, m_mix_norm, m_w_in, m_attn_sinks, m_hgrn_lower_bound, m_hgrn_norm, m_w_up_a, m_w_up_b, m_w_out, m_ffn2_norm, m_ffn2_w_gate, m_ffn2_w_up, m_ffn2_w_down, m_ple_norm, m_ple_w_gate, m_ple_w_proj, m_final_norm, v_ffn1_norm, v_ffn1_w_gate, v_ffn1_w_up, v_ffn1_w_down, v_mix_norm, v_w_in, v_attn_sinks, v_hgrn_lower_bound, v_hgrn_norm, v_w_up_a, v_w_up_b, v_w_out, v_ffn2_norm, v_ffn2_w_gate, v_ffn2_w_up, v_ffn2_w_down, v_ple_norm, v_ple_w_gate, v_ple_w_proj, v_final_norm):
    given = dict(locals())
    w = {n: given[n] for n in WEIGHTS}
    mom = {n: given["m_" + n] for n in WEIGHTS}
    var = {n: given["v_" + n] for n in WEIGHTS}
    t, d = x.shape[1], x.shape[2]

    x0, p16, target = x[0], p[0, 0].astype(BF16), loss_target[0]
    tables = _rope_tables(positions)
    sinks = attn_sinks.reshape(-1)
    final_w = final_norm.reshape(1, d)
    as_stored = lambda n, a: jnp.swapaxes(a, 1, 2) if n in TRANSPOSED else a
    shard = {n: as_stored(n, w[n])[0].astype(BF16) for n in LARGE}
    shard["w_in_left"], shard["w_in_right"] = shard["w_in"][:, :d // 2], shard["w_in"][:, d // 2:]
    xi, yi, ci = _place()
    block_ids = jnp.stack([4 * _flip(xi, rx) + 2 * _flip(yi, ry) + ci for rx, ry in RELATIONS]).astype(jnp.int32)

    def gather(*names):
        return _gather_program([shard[n] for n in names], 0.65)

    wg1, wu1 = _standalone("gather_ffn1", gather("ffn1_w_gate", "ffn1_w_up"))
    h1 = _rmsnorm("ffn1_norm", x0, ffn1_norm)
    (g1, u1, a1), (wd1, win_left) = _ffn_up("ffn1", h1, wg1, wu1, gather("ffn1_w_down", "w_in_left"))
    x1, (win_right,) = _ffn_down("ffn1", x0, a1, wd1, gather("w_in_right"))
    win = (win_left.reshape(-1, d // 2), win_right.reshape(-1, d // 2))
    mix_args = (mix_norm, win, sinks, hgrn_lower_bound, hgrn_norm)

    def mixer_weights(sent):
        wua, wub, wout_g, _ = sent["proj"]
        return wua, wub, wout_g.reshape(d, d)

    x2, mix_saved, got_w = _mixer_fwd(
        x1, *mix_args, mixer_weights, tables,
        dict(proj=gather("w_up_a", "w_up_b", "w_out", "ffn2_w_gate"), attention=gather("ffn2_w_up"),
             hgrn=gather("ple_w_gate", "ple_w_proj")))
    wua, wub, wout = mixer_weights(got_w)
    wg2, (wu2,) = got_w["proj"][3], got_w["attention"]
    wpg_g, wpp = got_w["hgrn"]
    wpg = wpg_g.reshape(d, d)
    h3 = _rmsnorm("ffn2_norm", x2, ffn2_norm)
    (g2, u2, a2), (wd2,) = _ffn_up("ffn2", h3, wg2, wu2, gather("ffn2_w_down"))
    x3, _ = _ffn_down("ffn2", x2, a2, wd2)
    x4, ple_saved = _ple_fwd(x3, ple_norm, p16, wpg, wpp)
    dx4, d_final, loss_row = _loss_head(x4, final_w, target)

    sums, from_chips = {}, {}

    def pair_sums(names, grads, from_sibling):
        for n, g, s in zip(names, grads, from_sibling):
            sums[n] = _pair_sum("pair_sum_" + n, g, s, block_ids)

    def to_chips(names, grads, from_sibling):
        pair_sums(names, grads, from_sibling)
        return _chips_program([sums[n] for n in names])

    def arrived(names, results):
        for n, r in zip(names, results):
            from_chips[n] = [(r, 0), (r, 1), (r, 2)]

    def ffn_comms(first, second, names):
        dn, gn, un = ((n,) for n in names)
        return dict(
            dact=first, dwd=second,
            dwg=lambda av: _sibling_program([av["dwd"]]),
            dwu=lambda av: _merge([to_chips(dn, [av["dwd"]], av["sent"]["dwg"]), _sibling_program([av["dwg"]])]),
            dxa=lambda av: _merge([to_chips(gn, [av["dwg"]], av["sent"]["dwu"][1:]), _sibling_program([av["dwu"]])]),
            dxb=lambda av: to_chips(un, [av["dwu"]], av["sent"]["dxa"][1:]))

    def ffn_arrived(names, sent):
        arrived(names, [sent["dwu"][0], sent["dxa"][0], sent["dxb"][0]])

    dx3, dx3_16, dwpp, dwpg, d_ple = _ple_bwd(dx4, x3, ple_norm, p16, wpg, ple_saved)
    ple_names = ("ple_w_proj", "ple_w_gate")
    ple_grads = [dwpp, dwpg.reshape(N_DEV, d // N_DEV, d)]
    ffn2_names = ("ffn2_w_down", "ffn2_w_gate", "ffn2_w_up")
    dx2, dx2_16, _, _, _, d_ffn2, sent2 = _ffn_bwd(
        "ffn2b", dx3, dx3_16, x2, h3, g2, u2, a2, wg2, wu2, wd2, ffn2_norm,
        ffn_comms(_sibling_program(ple_grads), lambda av: to_chips(ple_names, ple_grads, av["sent"]["dact"]),
                  ffn2_names))
    arrived(ple_names, sent2["dwd"])
    ffn_arrived(ffn2_names, sent2)
    mix3 = ("w_out", "w_up_a", "w_up_b")
    dx1, dx1_16, mix_small, mix_made = _mixer_bwd(
        dx2, dx2_16, x1, *mix_args, wua, wub, wout, tables, mix_saved,
        dict(attention=lambda av: _sibling_program([av[n] for n in mix3]),
             hgrn=lambda av: to_chips(mix3, [av[n] for n in mix3], av["sent"]["attention"]),
             dx=lambda av: _sibling_program([av["w_in"]])))
    sent_mix = mix_made["sent"]
    arrived(mix3, sent_mix["hgrn"])
    pair_sums(("w_in",), [mix_made["w_in"]], sent_mix["dx"])
    ffn1_names = ("ffn1_w_down", "ffn1_w_gate", "ffn1_w_up")
    grad_x, _, _, _, _, d_ffn1, sent1 = _ffn_bwd(
        "ffn1b", dx1, dx1_16, x0, h1, g1, u1, a1, wg1, wu1, wd1, ffn1_norm,
        ffn_comms(_chips_program([sums["w_in"]], (1, 2)), _chips_program([sums["w_in"]], (3,)), ffn1_names))
    (near,), (far,) = sent1["dact"], sent1["dwd"]
    from_chips["w_in"] = [(near, 0), (near, 1), (far, 0)]
    ffn_arrived(ffn1_names, sent1)

    out = {}
    for n in LARGE:
        (res,), _ = _adam_sharded("adam_" + n, [(as_stored(n, w[n]), as_stored(n, mom[n]), as_stored(n, var[n]),
                                                  sums[n], from_chips[n])])
        out[n] = [as_stored(n, a) for a in res]

    dlb = mix_small["hgrn_lb"]
    small_grads = dict(
        ffn1_norm=d_ffn1, mix_norm=mix_small["mix_norm"], attn_sinks=mix_small["attn_sinks"][:, :A_HEADS],
        hgrn_lower_bound=jnp.concatenate([dlb, -dlb], axis=0), hgrn_norm=mix_small["hgrn_norm"],
        ffn2_norm=d_ffn2, ple_norm=d_ple, final_norm=d_final)

    small_like = [w[n] for n in SMALL]
    packed = _pack_rows([small_grads[n].reshape(w[n].shape) for n in SMALL] + [loss_row[:, :1]])
    total = _all_reduce_small(packed)
    n_rows = total.shape[0]
    pack_w = _pack_rows(small_like + [jnp.zeros((1, 1), F32)])
    pack_m = _pack_rows([mom[n] for n in SMALL] + [jnp.zeros((1, 1), F32)])
    pack_v = _pack_rows([var[n] for n in SMALL] + [jnp.zeros((1, 1), F32)])
    deltas = _adam_small(pack_w, total, pack_m, pack_v)
    like = small_like + [jnp.zeros((1, 1), F32)]
    g_small = _unpack_rows(total, like)
    d_small, m_small, v_small = (_unpack_rows(a, like) for a in deltas)
    for k, n in enumerate(SMALL):
        out[n] = [g_small[k], d_small[k], m_small[k], v_small[k]]
    loss = g_small[-1].reshape(())

    return (loss, grad_x.reshape(x.shape),
            *[out[n][0] for n in WEIGHTS], *[out[n][1] for n in WEIGHTS],
            *[out[n][2] for n in WEIGHTS], *[out[n][3] for n in WEIGHTS])
```

```python
import functools

import jax
import jax.numpy as jnp
from jax import lax
from jax.experimental import pallas as pl
from jax.experimental.pallas import tpu as pltpu

F32 = jnp.float32
BF16 = jnp.bfloat16
MESH = pl.DeviceIdType.MESH

EPS = 1e-6
N_DEV = 8
A_HEADS = 16
A_HEAD_DIM = 64
A_WIDTH = 1024
A_KV_WIDTH = 256
ATT_BLOCK = 128
ROT_DIM = 16
ROPE_THETA = 500000.0
B_WIDTH = 1024
B_HEAD = 128
CHUNK = 64
SUB = 16
MAX_DECAY_EXP = 60.0
PLE_DIM = 256
LANES = 128

ADAM_LR = 0.001
ADAM_B1 = 0.9
ADAM_B2 = 0.999
ADAM_EPS = 1e-08
ADAM_WD = 0.01
ADAM_STEP = 10

VMEM_LIMIT = 56 * 1024 * 1024
EPILOGUE_ELEMS = 256 * 1024
LATE_AT = 0.92

NN = (((1,), (0,)), ((), ()))
NT = (((1,), (1,)), ((), ()))
TN = (((0,), (0,)), ((), ()))

OFF_QA = 0
OFF_KA = 1024
OFF_VA = 1280
OFF_QB = 1536
OFF_FB = 2560
OFF_IB = 3584
OFF_OG = 4608
OFF_GA = 5632


def _sig(v):
    return 1.0 / (1.0 + jnp.exp(-v))


def _tile(n, pref, mult):
    t = min(n, pref)
    t -= t % mult
    while n % t:
        t -= mult
    return t


def _params(sem):
    return pltpu.CompilerParams(dimension_semantics=sem, vmem_limit_bytes=VMEM_LIMIT)


HBM_SPEC = pl.BlockSpec(memory_space=pltpu.HBM)


class _Comm:
    def __init__(self, ins, out_shapes, n_sems, pre, post, mid=None, mid_at=1.0, late=None):
        self.ins, self.out_shapes, self.n_sems, self.pre, self.post = ins, out_shapes, n_sems, pre, post
        self.mid = mid if mid is not None else (lambda *args: None)
        self.late = late if late is not None else (lambda *args: None)
        self.mid_at = mid_at
        self.parts = [len(out_shapes)]

    def split(self, outs):
        res, at = [], 0
        for n in self.parts:
            res.append(list(outs[at:at + n]))
            at += n
        return res


def _merge(progs):
    spans, ins, shapes, sems = [], [], [], [0, 0, 0]
    for p in progs:
        spans.append((len(ins), len(shapes), tuple(sems)))
        ins += list(p.ins)
        shapes += list(p.out_shapes)
        sems = [a + b for a, b in zip(sems, p.n_sems)]

    def run(which):
        def go(cin, cout, send, recv, local, base):
            for p, (i0, o0, s0) in zip(progs, spans):
                getattr(p, which)(cin[i0:i0 + len(p.ins)], cout[o0:o0 + len(p.out_shapes)], send, recv, local,
                                  tuple(b + s for b, s in zip(base, s0)))
        return go

    merged = _Comm(ins, shapes, tuple(sems), run("pre"), run("post"), run("mid"), max(p.mid_at for p in progs),
                   run("late"))
    merged.parts = [len(p.out_shapes) for p in progs]
    return merged


def _call(name, body, grid, in_specs, out_specs, out_shape, scratch, args, comm=None, aliases=None):
    sem = ("arbitrary",) * len(grid)
    aliases = aliases or {}
    if comm is None:
        res = pl.pallas_call(body, name=name, grid=grid, in_specs=in_specs, out_specs=out_specs,
                             out_shape=out_shape, scratch_shapes=scratch, input_output_aliases=aliases,
                             compiler_params=_params(sem))(*args)
        return list(res), []
    n_in, n_out, n_scr = len(in_specs), len(out_specs), len(scratch)
    c_in, c_out = len(comm.ins), len(comm.out_shapes)
    steps = functools.reduce(lambda a, b: a * b, grid)
    mid_step = min(steps - 1, int(comm.mid_at * steps))
    late_step = min(steps - 1, max(mid_step, int(LATE_AT * steps)))

    def carrier(*refs):
        at = [0]

        def take(n):
            at[0] += n
            return refs[at[0] - n:at[0]]

        ins, cins, outs, couts, scr, sems = take(n_in), take(c_in), take(n_out), take(c_out), take(n_scr), take(3)
        ids = [pl.program_id(a) for a in range(len(grid))]
        step = functools.reduce(lambda acc, ig: acc * ig[1] + ig[0], zip(ids, grid), 0)

        @pl.when(step == 0)
        def _():
            comm.pre(cins, couts, *sems, (0, 0, 0))

        body(*ins, *outs, *scr)

        @pl.when(step == mid_step)
        def _():
            comm.mid(cins, couts, *sems, (0, 0, 0))

        @pl.when(step == late_step)
        def _():
            comm.late(cins, couts, *sems, (0, 0, 0))

        @pl.when(step == steps - 1)
        def _():
            comm.post(cins, couts, *sems, (0, 0, 0))

    res = pl.pallas_call(
        carrier, name=name, grid=grid,
        in_specs=list(in_specs) + [HBM_SPEC] * c_in,
        out_specs=list(out_specs) + [HBM_SPEC] * c_out,
        out_shape=list(out_shape) + list(comm.out_shapes),
        scratch_shapes=list(scratch) + [pltpu.SemaphoreType.DMA((max(n, 1),)) for n in comm.n_sems],
        input_output_aliases=aliases,
        compiler_params=_params(sem),
    )(*args, *comm.ins)
    return list(res[:n_out]), list(res[n_out:])


def _mm(name, grid, ins, prods, acc_shapes, epi, outs, comm=None, aliases=None):
    n_in, n_out, nk = len(ins), len(outs), grid[2]
    tile_rows, tile_cols = acc_shapes[0]
    chunk = tile_rows
    while chunk * tile_cols > EPILOGUE_ELEMS and chunk % 32 == 0:
        chunk //= 2

    def body(*refs):
        in_refs = refs[:n_in]
        out_refs = refs[n_in:n_in + n_out]
        acc_refs = refs[n_in + n_out:]
        first_tile = jnp.logical_and(pl.program_id(0) == 0, pl.program_id(1) == 0)

        def product(lhs, rhs, dims):
            a = in_refs[lhs][...].astype(BF16)
            b = in_refs[rhs][...].astype(BF16)
            return lax.dot_general(a, b, dims, preferred_element_type=F32)

        def write(accs):
            totals = [None] * n_out
            for c in range(tile_rows // chunk):
                rows = slice(c * chunk, (c + 1) * chunk)
                vals = epi([a[rows] for a in accs], in_refs, rows)
                for idx, (o, v, spec) in enumerate(zip(out_refs, vals, outs)):
                    if spec[4]:
                        totals[idx] = v if totals[idx] is None else totals[idx] + v
                    else:
                        o[rows] = v.astype(o.dtype)
            for o, v, spec in zip(out_refs, totals, outs):
                if spec[4]:
                    @pl.when(first_tile)
                    def _():
                        o[...] = v.astype(o.dtype)

                    @pl.when(jnp.logical_not(first_tile))
                    def _():
                        o[...] += v.astype(o.dtype)

        if nk == 1:
            accs = [None] * len(acc_shapes)
            for lhs, rhs, acc, dims in prods:
                t = product(lhs, rhs, dims)
                accs[acc] = t if accs[acc] is None else accs[acc] + t
            write(accs)
        else:
            k = pl.program_id(2)

            @pl.when(k == 0)
            def _():
                for r in acc_refs:
                    r[...] = jnp.zeros(r.shape, F32)

            for lhs, rhs, acc, dims in prods:
                acc_refs[acc][...] += product(lhs, rhs, dims)

            @pl.when(k == nk - 1)
            def _():
                write(acc_refs)

    res, sent = _call(
        name, body, grid,
        [pl.BlockSpec(memory_space=pl.ANY) if b is None else pl.BlockSpec(b, im) for _, b, im in ins],
        [pl.BlockSpec(o[2], o[3]) for o in outs],
        [jax.ShapeDtypeStruct(o[0], o[1]) for o in outs],
        [] if nk == 1 else [pltpu.VMEM(s, F32) for s in acc_shapes],
        [a for a, _, _ in ins], comm, aliases)
    return res if comm is None else (res, sent)


def _rms_bwd(dh, xv, w, dres):
    r = lax.rsqrt(jnp.mean(xv * xv, axis=-1, keepdims=True) + EPS)
    g = dh * w
    dx = dres + r * (g - xv * (r * r) * jnp.mean(g * xv, axis=-1, keepdims=True))
    dw = jnp.sum(dh * xv * r, axis=0, keepdims=True)
    return dx, dw


def _rows(name, fn, ins, outs, tm):
    n_in = len(ins)
    rows = outs[0][0][0]

    def body(*refs):
        vals = fn(*[r[...] for r in refs[:n_in]])
        for o, v in zip(refs[n_in:], vals):
            o[...] = v.astype(o.dtype)

    in_specs = []
    for a, tiled in ins:
        if tiled:
            in_specs.append(pl.BlockSpec((tm, a.shape[1]), lambda i: (i, 0)))
        else:
            in_specs.append(pl.BlockSpec(a.shape, lambda i, nd=a.ndim: (0,) * nd))
    return pl.pallas_call(
        body,
        name=name,
        grid=(rows // tm,),
        in_specs=in_specs,
        out_specs=[pl.BlockSpec((tm, s[1]), lambda i: (i, 0)) for s, _ in outs],
        out_shape=[jax.ShapeDtypeStruct(s, d) for s, d in outs],
        compiler_params=_params(("arbitrary",)),
    )(*[a for a, _ in ins])


def _rmsnorm(name, x, w):
    def fn(xv, wv):
        r = lax.rsqrt(jnp.mean(xv * xv, axis=-1, keepdims=True) + EPS)
        return (xv * r * wv,)

    return _rows(name, fn, [(x, True), (w, False)], [(x.shape, BF16)], _tile(x.shape[0], 512, 16))[0]


def _mmc(*args, comm=None, aliases=None):
    if comm is None:
        return _mm(*args, aliases=aliases), []
    return _mm(*args, comm=comm, aliases=aliases)


def _carried(comms, key, avail):
    c = comms.get(key) if comms else None
    return c(avail) if callable(c) else c


def _ffn_up(name, h, wg, wu, comm=None):
    t, d = h.shape
    n = wg.shape[1]
    tm = _tile(t, 1024, 16)

    def epi_up(accs, refs, rows):
        g, u = accs
        return [g, u, g * _sig(g) * u]

    hid = ((N_DEV, t, n), BF16, (None, tm, n), lambda i, j, k: (j, i, 0), False)
    return _mmc(
        name + "_up", (t // tm, N_DEV, 1),
        [(h, (tm, d), lambda i, j, k: (i, 0)),
         (wg, (None, n, d), lambda i, j, k: (j, 0, 0)),
         (wu, (None, n, d), lambda i, j, k: (j, 0, 0))],
        [(0, 1, 0, NT), (0, 2, 1, NT)], [(tm, n)] * 2, epi_up, [hid] * 3, comm=comm)


def _ffn_down(name, x, act, wd, comm=None):
    t, d = x.shape
    n = wd.shape[1]
    tm2 = _tile(t, 1024, 8)
    tn = _tile(d, 1024, 128)

    def epi_down(accs, refs, rows):
        return [refs[2][rows] + 0.5 * accs[0]]

    (x_out,), sent = _mmc(
        name + "_down", (t // tm2, d // tn, N_DEV),
        [(act, (None, tm2, n), lambda i, j, k: (k, i, 0)),
         (wd, (None, n, tn), lambda i, j, k: (k, 0, j)),
         (x, (tm2, tn), lambda i, j, k: (i, j))],
        [(0, 1, 0, NN)], [(tm2, tn)], epi_down,
        [((t, d), F32, (tm2, tn), lambda i, j, k: (i, j), False)], comm=comm)
    return x_out, sent


def _ffn_bwd(name, dxo, dxo16, x, h, gate, up, act, wg, wu, wd, norm_w, comms=None):
    t, d = x.shape
    n = wg.shape[1]
    tm = _tile(t, 1024, 16)
    sent = {}
    avail = dict(sent=sent)

    def epi_act(accs, refs, rows):
        da = 0.5 * accs[0]
        g = refs[2][rows].astype(F32)
        u = refs[3][rows].astype(F32)
        s = _sig(g)
        return [da * u * (s * (1.0 + g * (1.0 - s))), da * g * s]

    hid_in = lambda a: (a, (None, tm, n), lambda i, j, k: (j, i, 0))
    hid_out = ((N_DEV, t, n), BF16, (None, tm, n), lambda i, j, k: (j, i, 0), False)
    (dgate, dup), sent["dact"] = _mmc(
        name + "_dact", (t // tm, N_DEV, 1),
        [(dxo16, (tm, d), lambda i, j, k: (i, 0)),
         (wd, (None, n, d), lambda i, j, k: (j, 0, 0)),
         hid_in(gate), hid_in(up)],
        [(0, 1, 0, NT)], [(tm, n)], epi_act, [hid_out] * 2, comm=_carried(comms, "dact", avail))

    tk = _tile(t, 1024, 16)
    w_out = ((N_DEV, n, d), BF16, (None, n, d), lambda i, j, k: (i, 0, 0), False)

    def weight_grad(key, acts, stream, scale):
        (dw,), sent[key] = _mmc(
            name + "_" + key, (N_DEV, 1, t // tk),
            [(acts, (None, tk, n), lambda i, j, k: (i, k, 0)), (stream, (tk, d), lambda i, j, k: (k, 0))],
            [(0, 1, 0, TN)], [(n, d)], lambda accs, refs, rows: [scale * accs[0]], [w_out],
            comm=_carried(comms, key, avail))
        avail[key] = dw
        return dw

    dwd = weight_grad("dwd", act, dxo16, 0.5)
    dwg = weight_grad("dwg", dgate, h, 1.0)
    dwu = weight_grad("dwu", dup, h, 1.0)

    tm3 = _tile(t // 2, 512, 16)
    half = t // 2 // tm3

    def epi_dx(accs, refs, rows):
        dx, dw = _rms_bwd(accs[0], refs[4][rows], refs[5][...], refs[6][rows])
        return [dx, dx, dw]

    def dx_half(key, first, filled):
        at = 0 if first else half
        row = lambda i, j, k: (i + at, 0)
        hid = lambda i, j, k: (k, i + at, 0)
        ins = [(dgate, (None, tm3, n), hid), (wg, (None, n, d), lambda i, j, k: (k, 0, 0)),
               (dup, (None, tm3, n), hid), (wu, (None, n, d), lambda i, j, k: (k, 0, 0)),
               (x, (tm3, d), row), (norm_w, (1, d), lambda i, j, k: (0, 0)), (dxo, (tm3, d), row)]
        ins += [(a, None, None) for a in filled]
        res, sent[key] = _mmc(
            name + "_" + key, (half, 1, N_DEV), ins,
            [(0, 1, 0, NN), (2, 3, 0, NN)], [(tm3, d)], epi_dx,
            [((t, d), F32, (tm3, d), row, False), ((t, d), BF16, (tm3, d), row, False),
             ((1, d), F32, (1, d), lambda i, j, k: (0, 0), True)],
            comm=_carried(comms, key, avail), aliases={7 + k: k for k in range(len(filled))})
        return res

    dx_a, dx16_a, dnorm_a = dx_half("dxa", True, [])
    dx, dx16, dnorm_b = dx_half("dxb", False, [dx_a, dx16_a])
    return dx, dx16, dwg, dwu, dwd, dnorm_a + dnorm_b, sent


def _rope(tv, cos, s1, s2):
    return tv * cos + pltpu.roll(tv, LANES - 8, 1) * s1 + pltpu.roll(tv, 8, 1) * s2


def _rope_bwd(dr, cos, s1, s2):
    return dr * cos + pltpu.roll(dr * s1, 8, 1) + pltpu.roll(dr * s2, LANES - 8, 1)


def _rope_tables(positions):
    inv_freq = jnp.power(jnp.float32(ROPE_THETA), -jnp.arange(0, ROT_DIM, 2, dtype=F32) / ROT_DIM)
    ang = positions.reshape(-1).astype(F32)[:, None] * inv_freq
    cos, sin = jnp.cos(ang), jnp.sin(ang)
    t = ang.shape[0]
    pad = A_HEAD_DIM - ROT_DIM
    cos_t = jnp.concatenate([cos, cos, jnp.ones((t, pad), F32)], axis=1)
    s1_t = jnp.concatenate([-sin, jnp.zeros((t, pad + 8), F32)], axis=1)
    s2_t = jnp.concatenate([jnp.zeros((t, 8), F32), sin, jnp.zeros((t, pad), F32)], axis=1)
    return tuple(jnp.tile(v, (1, 2)) for v in (cos_t, s1_t, s2_t))


def _att_common(i, k_p, k_c, v_p, v_c, tabs_p, tabs_c):
    kcat = jnp.concatenate([k_p, k_c], axis=0)
    vcat = jnp.concatenate([v_p, v_c], axis=0)
    tabs_k = [jnp.concatenate([a, b], axis=0) for a, b in zip(tabs_p, tabs_c)]
    kvar, vvar = [], []
    for ks in range(2):
        sl = slice(ks * LANES, (ks + 1) * LANES)
        kr = _rope(kcat[:, sl], *tabs_k)
        kvar.append([kr.astype(BF16), pltpu.roll(kr, 64, 1).astype(BF16)])
        vvar.append([vcat[:, sl].astype(BF16), pltpu.roll(vcat[:, sl], 64, 1).astype(BF16)])
    qi = lax.broadcasted_iota(jnp.int32, (ATT_BLOCK, 2 * ATT_BLOCK), 0)
    kj = lax.broadcasted_iota(jnp.int32, (ATT_BLOCK, 2 * ATT_BLOCK), 1)
    dist = qi + ATT_BLOCK - kj
    allowed = (dist >= 0) & (dist < ATT_BLOCK) & ((i > 0) | (kj >= ATT_BLOCK))
    lane = lax.broadcasted_iota(jnp.int32, (1, LANES), 1)
    halves = [lane < 64, lane >= 64]
    return kvar, vvar, allowed, halves


def _att_head_probs(qm, kslab, allowed, sink):
    s = lax.dot_general(qm, kslab, NT, preferred_element_type=F32)
    s = jnp.where(allowed, s, -jnp.inf)
    m = jnp.maximum(jnp.max(s, axis=-1, keepdims=True), sink)
    e = jnp.exp(s - m)
    es = jnp.exp(sink - m)
    den = jnp.sum(e, axis=-1, keepdims=True) + es
    inv = 1.0 / den
    return e * inv, es * inv


def _head_slots(h):
    pair, half = h // 2, h % 2
    kvh = h // 4
    return pair, half, kvh // 2, int(kvh % 2 != half)


def _attention_fwd(proj, tables, sinks, comm=None):
    t = proj.shape[0]
    nb = t // ATT_BLOCK
    b = ATT_BLOCK
    scale = A_HEAD_DIM ** -0.5

    def body(sink_ref, q_ref, kp_ref, kc_ref, vp_ref, vc_ref,
             cp_ref, ap_ref, bp_ref, cc_ref, ac_ref, bc_ref, o_ref):
        i = pl.program_id(0)
        tabs_p = (cp_ref[...], ap_ref[...], bp_ref[...])
        tabs_c = (cc_ref[...], ac_ref[...], bc_ref[...])
        kvar, vvar, allowed, halves = _att_common(
            i, kp_ref[...], kc_ref[...], vp_ref[...], vc_ref[...], tabs_p, tabs_c)
        for pair in range(A_HEADS // 2):
            sl = slice(pair * LANES, (pair + 1) * LANES)
            q2 = _rope(q_ref[:, sl], *tabs_c) * scale
            acc = jnp.zeros((b, LANES), F32)
            for half in range(2):
                h = 2 * pair + half
                _, _, ks, var = _head_slots(h)
                qm = jnp.where(halves[half], q2, 0.0).astype(BF16)
                p, _ = _att_head_probs(qm, kvar[ks][var], allowed, sink_ref[h])
                o = lax.dot_general(p.astype(BF16), vvar[ks][var], NN, preferred_element_type=F32)
                acc = acc + jnp.where(halves[half], o, 0.0)
            o_ref[:, sl] = acc.astype(o_ref.dtype)

    cur = lambda col: (lambda i: (i, col))
    prev = lambda col: (lambda i: (jnp.maximum(i - 1, 0), col))
    tab = [pl.BlockSpec((b, LANES), prev(0))] * 3 + [pl.BlockSpec((b, LANES), cur(0))] * 3
    (out,), sent = _call(
        "attention_fwd", body, (nb,),
        [pl.BlockSpec(memory_space=pltpu.SMEM),
         pl.BlockSpec((b, A_WIDTH), cur(0)),
         pl.BlockSpec((b, A_KV_WIDTH), prev(OFF_KA // A_KV_WIDTH)),
         pl.BlockSpec((b, A_KV_WIDTH), cur(OFF_KA // A_KV_WIDTH)),
         pl.BlockSpec((b, A_KV_WIDTH), prev(OFF_VA // A_KV_WIDTH)),
         pl.BlockSpec((b, A_KV_WIDTH), cur(OFF_VA // A_KV_WIDTH))] + tab,
        [pl.BlockSpec((b, A_WIDTH), cur(0))],
        [jax.ShapeDtypeStruct((t, A_WIDTH), BF16)], [],
        [sinks, proj, proj, proj, proj, proj, *tables, *tables], comm)
    return out, sent


def _attention_bwd(proj, tables, sinks, d_out, comm=None):
    t = proj.shape[0]
    nb = t // ATT_BLOCK
    b = ATT_BLOCK
    scale = A_HEAD_DIM ** -0.5

    def body(sink_ref, q_ref, kp_ref, kc_ref, vp_ref, vc_ref,
             cp_ref, ap_ref, bp_ref, cc_ref, ac_ref, bc_ref, do_ref,
             dq_ref, dk_ref, dv_ref, dsink_ref, dk_carry, dv_carry):
        step = pl.program_id(0)
        i = nb - 1 - step

        @pl.when(step == 0)
        def _():
            dk_carry[...] = jnp.zeros(dk_carry.shape, F32)
            dv_carry[...] = jnp.zeros(dv_carry.shape, F32)
            dsink_ref[...] = jnp.zeros(dsink_ref.shape, F32)

        tabs_p = (cp_ref[...], ap_ref[...], bp_ref[...])
        tabs_c = (cc_ref[...], ac_ref[...], bc_ref[...])
        kvar, vvar, allowed, halves = _att_common(
            i, kp_ref[...], kc_ref[...], vp_ref[...], vc_ref[...], tabs_p, tabs_c)
        lane = lax.broadcasted_iota(jnp.int32, (1, LANES), 1)
        dk_slab = [jnp.zeros((2 * b, LANES), F32) for _ in range(2)]
        dv_slab = [jnp.zeros((2 * b, LANES), F32) for _ in range(2)]
        dsink = jnp.zeros((1, LANES), F32)
        for pair in range(A_HEADS // 2):
            sl = slice(pair * LANES, (pair + 1) * LANES)
            q2 = _rope(q_ref[:, sl], *tabs_c) * scale
            do2 = do_ref[:, sl].astype(F32)
            dq2 = jnp.zeros((b, LANES), F32)
            for half in range(2):
                h = 2 * pair + half
                _, _, ks, var = _head_slots(h)
                qm = jnp.where(halves[half], q2, 0.0).astype(BF16)
                dom = jnp.where(halves[half], do2, 0.0).astype(BF16)
                p, p_sink = _att_head_probs(qm, kvar[ks][var], allowed, sink_ref[h])
                dp = lax.dot_general(dom, vvar[ks][var], NT, preferred_element_type=F32)
                dsum = jnp.sum(p * dp, axis=-1, keepdims=True)
                ds = (p * (dp - dsum)).astype(BF16)
                dsink = dsink + jnp.where(lane == h, -jnp.sum(p_sink * dsum), 0.0)
                dqh = lax.dot_general(ds, kvar[ks][var], NN, preferred_element_type=F32)
                dq2 = dq2 + jnp.where(halves[half], dqh, 0.0)
                dkx = lax.dot_general(ds, qm, TN, preferred_element_type=F32)
                dvx = lax.dot_general(p.astype(BF16), dom, TN, preferred_element_type=F32)
                if var:
                    dkx = pltpu.roll(dkx, 64, 1)
                    dvx = pltpu.roll(dvx, 64, 1)
                dk_slab[ks] = dk_slab[ks] + dkx
                dv_slab[ks] = dv_slab[ks] + dvx
            dq_ref[:, sl] = _rope_bwd(dq2 * scale, *tabs_c).astype(dq_ref.dtype)
        for ks in range(2):
            sl = slice(ks * LANES, (ks + 1) * LANES)
            dk_cur = dk_slab[ks][b:] + dk_carry[:, sl]
            dk_ref[:, sl] = _rope_bwd(dk_cur, *tabs_c).astype(dk_ref.dtype)
            dv_ref[:, sl] = (dv_slab[ks][b:] + dv_carry[:, sl]).astype(dv_ref.dtype)
            dk_carry[:, sl] = dk_slab[ks][:b]
            dv_carry[:, sl] = dv_slab[ks][:b]
        dsink_ref[...] += dsink

    cur = lambda col: (lambda s: (nb - 1 - s, col))
    prev = lambda col: (lambda s: (jnp.maximum(nb - 2 - s, 0), col))
    tab = [pl.BlockSpec((b, LANES), prev(0))] * 3 + [pl.BlockSpec((b, LANES), cur(0))] * 3
    return _call(
        "attention_bwd", body, (nb,),
        [pl.BlockSpec(memory_space=pltpu.SMEM),
         pl.BlockSpec((b, A_WIDTH), cur(0)),
         pl.BlockSpec((b, A_KV_WIDTH), prev(OFF_KA // A_KV_WIDTH)),
         pl.BlockSpec((b, A_KV_WIDTH), cur(OFF_KA // A_KV_WIDTH)),
         pl.BlockSpec((b, A_KV_WIDTH), prev(OFF_VA // A_KV_WIDTH)),
         pl.BlockSpec((b, A_KV_WIDTH), cur(OFF_VA // A_KV_WIDTH))] + tab
        + [pl.BlockSpec((b, A_WIDTH), cur(0))],
        [pl.BlockSpec((b, A_WIDTH), cur(0)),
         pl.BlockSpec((b, A_KV_WIDTH), cur(0)),
         pl.BlockSpec((b, A_KV_WIDTH), cur(0)),
         pl.BlockSpec((1, LANES), lambda s: (0, 0))],
        [jax.ShapeDtypeStruct((t, A_WIDTH), BF16),
         jax.ShapeDtypeStruct((t, A_KV_WIDTH), BF16),
         jax.ShapeDtypeStruct((t, A_KV_WIDTH), BF16),
         jax.ShapeDtypeStruct((1, LANES), F32)],
        [pltpu.VMEM((b, A_KV_WIDTH), F32), pltpu.VMEM((b, A_KV_WIDTH), F32)],
        [sinks, proj, proj, proj, proj, proj, *tables, *tables, d_out], comm)


HGRN_HEADS_PER_STEP = 4
HGRN_W = HGRN_HEADS_PER_STEP * B_HEAD
HGRN_ROWS = 4 * CHUNK


def _hgrn_gates(qp, fp, hlb):
    lb = _sig(hlb[0:1] - hlb[1:2])
    sg = _sig(fp)
    big_f = lb + (1.0 - lb) * sg
    kk = (1.0 - lb) * (1.0 - sg)
    sq = _sig(qp)
    return lb, sg, big_f, kk, sq


def _tri(lower):
    r = lax.broadcasted_iota(jnp.int32, (CHUNK, CHUNK), 0)
    c = lax.broadcasted_iota(jnp.int32, (CHUNK, CHUNK), 1)
    return (r >= c) if lower else (r <= c)


def _hgrn_factors(q_ref, k_ref, b_ref, sl, r0):
    out = []
    whole = slice(r0, r0 + CHUNK)
    bh = b_ref[whole, sl]
    for i in range(CHUNK // SUB):
        rows = slice(r0 + i * SUB, r0 + (i + 1) * SUB)
        b0 = b_ref[r0 + i * SUB - 1:r0 + i * SUB, sl] if i else jnp.zeros((1, B_HEAD), F32)
        eq = jnp.exp(b_ref[rows, sl] - b0)
        ek = jnp.exp(jnp.minimum(b0 - bh, MAX_DECAY_EXP))
        out.append((q_ref[rows, sl] * eq, k_ref[whole, sl] * ek, eq, ek))
    return out


def _hgrn_scores(factors):
    rows = [lax.dot_general(qd.astype(BF16), kx.astype(BF16), NT, preferred_element_type=F32)
            for qd, kx, _, _ in factors]
    return jnp.where(_tri(True), jnp.concatenate(rows, axis=0), 0.0)


def _hgrn_specs(steps, reverse):
    w = HGRN_W
    block = (lambda c: steps - 1 - c) if reverse else (lambda c: c)
    col = lambda off: pl.BlockSpec((HGRN_ROWS, w), lambda hh, c: (block(c), off // w + hh))
    return block, col


def _chunk_cumsum(log_f, r0, lower):
    tri = _tri(lower).astype(F32)
    return lax.dot_general(tri, log_f[r0:r0 + CHUNK], NN, precision=lax.Precision.HIGHEST,
                           preferred_element_type=F32)


def _hgrn_fwd(proj, hlb, wn, comm=None):
    t = proj.shape[0]
    nc = t // CHUNK
    steps = t // HGRN_ROWS
    w = HGRN_W
    block, col = _hgrn_specs(steps, False)

    def body(q_ref, f_ref, v_ref, og_ref, hlb_ref, wn_ref, ob_ref, oraw_ref, st_ref,
             state, qs, ks, bs):
        c = pl.program_id(1)

        @pl.when(c == 0)
        def _():
            state[...] = jnp.zeros(state.shape, F32)

        qp = q_ref[...]
        lb, sg, big_f, kk, sq = _hgrn_gates(qp, f_ref[...], hlb_ref[...])
        qs[...] = qp * sq
        ks[...] = kk
        log_f = jnp.log(big_f)
        for sub in range(HGRN_ROWS // CHUNK):
            r0 = sub * CHUNK
            rows = slice(r0, r0 + CHUNK)
            bs[rows] = _chunk_cumsum(log_f, r0, True)
            for h in range(HGRN_HEADS_PER_STEP):
                sl = slice(h * B_HEAD, (h + 1) * B_HEAD)
                bh = bs[rows, sl]
                qh = qs[rows, sl]
                kh = ks[rows, sl]
                vh = v_ref[rows, sl].astype(BF16)
                s0 = state[h]
                st_ref[sub, h] = s0.astype(st_ref.dtype)
                a = _hgrn_scores(_hgrn_factors(qs, ks, bs, sl, r0))
                o = lax.dot_general(a.astype(BF16), vh, NN, preferred_element_type=F32)
                o = o + lax.dot_general((qh * jnp.exp(bh)).astype(BF16), s0.astype(BF16), NT,
                                        preferred_element_type=F32)
                bl = bs[r0 + CHUNK - 1:r0 + CHUNK, sl]
                khat = (kh * jnp.exp(bl - bh)).astype(BF16)
                state[h] = s0 * jnp.exp(bl) + lax.dot_general(vh, khat, TN, preferred_element_type=F32)
                oraw_ref[rows, sl] = o
                r = lax.rsqrt(jnp.mean(o * o, axis=-1, keepdims=True) + EPS)
                og = og_ref[rows, sl]
                ob_ref[rows, sl] = (o * r * wn_ref[:, sl] * (og * _sig(og))).astype(ob_ref.dtype)

    vec = lambda rows: pl.BlockSpec((rows, w), lambda hh, c: (0, hh))
    tile = pl.BlockSpec((HGRN_ROWS, w), lambda hh, c: (c, hh))
    return _call(
        "hgrn_fwd", body, (B_WIDTH // w, steps),
        [col(OFF_QB), col(OFF_FB), col(OFF_IB), col(OFF_OG), vec(2), vec(1)],
        [tile, tile,
         pl.BlockSpec((HGRN_ROWS // CHUNK, HGRN_HEADS_PER_STEP, B_HEAD, B_HEAD), lambda hh, c: (c, hh, 0, 0))],
        [jax.ShapeDtypeStruct((t, B_WIDTH), BF16),
         jax.ShapeDtypeStruct((t, B_WIDTH), F32),
         jax.ShapeDtypeStruct((nc, B_WIDTH // B_HEAD, B_HEAD, B_HEAD), BF16)],
        [pltpu.VMEM((HGRN_HEADS_PER_STEP, B_HEAD, B_HEAD), F32)] + [pltpu.VMEM((HGRN_ROWS, w), F32)] * 3,
        [proj, proj, proj, proj, hlb, wn], comm)


def _hgrn_bwd(proj, hlb, wn, o_raw, states, d_out, comm=None):
    t = proj.shape[0]
    steps = t // HGRN_ROWS
    w = HGRN_W
    block, col = _hgrn_specs(steps, True)

    def body(q_ref, f_ref, v_ref, og_ref, hlb_ref, wn_ref, oraw_ref, st_ref, dout_ref,
             dq_ref, df_ref, dv_ref, dog_ref, dlb_ref, dwn_ref,
             dstate, qs, ks, bs, dbs, dks, dqs, dgs):
        c = pl.program_id(1)

        @pl.when(c == 0)
        def _():
            dstate[...] = jnp.zeros(dstate.shape, F32)
            dlb_ref[...] = jnp.zeros(dlb_ref.shape, F32)
            dwn_ref[...] = jnp.zeros(dwn_ref.shape, F32)

        qp = q_ref[...]
        lb, sg, big_f, kk, sq = _hgrn_gates(qp, f_ref[...], hlb_ref[...])
        qs[...] = qp * sq
        ks[...] = kk
        log_f = jnp.log(big_f)
        last_row = lax.broadcasted_iota(jnp.int32, (CHUNK, 1), 0) == CHUNK - 1
        for sub in reversed(range(HGRN_ROWS // CHUNK)):
            r0 = sub * CHUNK
            rows = slice(r0, r0 + CHUNK)
            bs[rows] = _chunk_cumsum(log_f, r0, True)
            for h in range(HGRN_HEADS_PER_STEP):
                sl = slice(h * B_HEAD, (h + 1) * B_HEAD)
                bh = bs[rows, sl]
                qh = qs[rows, sl]
                kh = ks[rows, sl]
                vh = v_ref[rows, sl].astype(BF16)
                s0 = st_ref[sub, h]
                ds1 = dstate[h]
                o = oraw_ref[rows, sl]
                og = og_ref[rows, sl]
                wnh = wn_ref[:, sl]
                sog = _sig(og)
                r = lax.rsqrt(jnp.mean(o * o, axis=-1, keepdims=True) + EPS)
                dob = dout_ref[rows, sl].astype(F32)
                don = dob * (og * sog)
                dog_ref[rows, sl] = (dob * (o * r * wnh) * (sog * (1.0 + og * (1.0 - sog)))).astype(dog_ref.dtype)
                dwn_ref[:, sl] += jnp.sum(don * o * r, axis=0, keepdims=True)
                gp = don * wnh
                do = r * (gp - o * (r * r) * jnp.mean(gp * o, axis=-1, keepdims=True))
                do16 = do.astype(BF16)
                factors = _hgrn_factors(qs, ks, bs, sl, r0)
                a = _hgrn_scores(factors)
                da = jnp.where(_tri(True), lax.dot_general(do16, vh, NT, preferred_element_type=F32), 0.0)
                dv = lax.dot_general(a.astype(BF16), do16, TN, preferred_element_type=F32)
                dq_rows = []
                dk = jnp.zeros((CHUNK, B_HEAD), F32)
                for i, (qd, kx, eq, ek) in enumerate(factors):
                    da_i = da[i * SUB:(i + 1) * SUB]
                    dq_rows.append(lax.dot_general(da_i, kx, NN, precision=lax.Precision.HIGHEST,
                                                   preferred_element_type=F32) * eq)
                    dk = dk + lax.dot_general(da_i, qd, TN, precision=lax.Precision.HIGHEST,
                                              preferred_element_type=F32) * ek
                eb = jnp.exp(bh)
                bl = bs[r0 + CHUNK - 1:r0 + CHUNK, sl]
                ebl = jnp.exp(bl)
                ekl = jnp.exp(bl - bh)
                qe = (qh * eb).astype(BF16)
                khat = (kh * ekl).astype(BF16)
                ds1_16 = ds1.astype(BF16)
                dq = jnp.concatenate(dq_rows, axis=0) + eb * lax.dot_general(
                    do16, s0, NN, preferred_element_type=F32)
                dv = dv + lax.dot_general(khat, ds1_16, NT, preferred_element_type=F32)
                dk_state = ekl * lax.dot_general(vh, ds1_16, NN, preferred_element_type=F32)
                dk = dk + dk_state
                db_last = (jnp.sum(kh * dk_state, axis=0, keepdims=True)
                           + ebl * jnp.sum(s0.astype(F32) * ds1, axis=0, keepdims=True))
                dbs[rows, sl] = qh * dq - kh * dk + jnp.where(last_row, db_last, 0.0)
                dks[rows, sl] = dk
                dqs[rows, sl] = dq
                dstate[h] = ds1 * ebl + lax.dot_general(do16, qe, TN, preferred_element_type=F32)
                dv_ref[rows, sl] = dv.astype(dv_ref.dtype)
            dgs[rows] = _chunk_cumsum(dbs, r0, False)
        dq_ref[...] = (dqs[...] * (sq * (1.0 + qp * (1.0 - sq)))).astype(dq_ref.dtype)
        df_minus_dk = dgs[...] / big_f - dks[...]
        df_ref[...] = ((1.0 - lb) * df_minus_dk * sg * (1.0 - sg)).astype(df_ref.dtype)
        dlb_ref[...] += jnp.sum((1.0 - sg) * df_minus_dk, axis=0, keepdims=True) * (lb * (1.0 - lb))

    vec = lambda rows: pl.BlockSpec((rows, w), lambda hh, c: (0, hh))
    blk = pl.BlockSpec((HGRN_ROWS, w), lambda hh, c: (block(c), hh))
    act = jax.ShapeDtypeStruct((t, B_WIDTH), BF16)
    row = jax.ShapeDtypeStruct((1, B_WIDTH), F32)
    return _call(
        "hgrn_bwd", body, (B_WIDTH // w, steps),
        [col(OFF_QB), col(OFF_FB), col(OFF_IB), col(OFF_OG), vec(2), vec(1), blk,
         pl.BlockSpec((HGRN_ROWS // CHUNK, HGRN_HEADS_PER_STEP, B_HEAD, B_HEAD),
                      lambda hh, c: (block(c), hh, 0, 0)),
         blk],
        [blk, blk, blk, blk, vec(1), vec(1)],
        [act, act, act, act, row, row],
        [pltpu.VMEM((HGRN_HEADS_PER_STEP, B_HEAD, B_HEAD), F32)] + [pltpu.VMEM((HGRN_ROWS, w), F32)] * 7,
        [proj, proj, proj, proj, hlb, wn, o_raw, states, d_out], comm)


def _mixer_fwd(x1, mix_norm, w_in, sinks, hlb, wn, weights, tables, comms):
    t, d = x1.shape
    n_in = w_in[0].shape[0]
    nb = d // N_DEV
    sent = {}
    h2 = _rmsnorm("mix_norm", x1, mix_norm)
    tm = _tile(t, 1024, 16)
    tn = _tile(n_in, 512, 128)
    (proj,), sent["proj"] = _mmc(
        "mix_proj", (t // tm, n_in // tn, 1),
        [(h2, (tm, d // 2), lambda i, j, k: (i, 0)), (w_in[0], (tn, d // 2), lambda i, j, k: (j, 0)),
         (h2, (tm, d // 2), lambda i, j, k: (i, 1)), (w_in[1], (tn, d // 2), lambda i, j, k: (j, 0))],
        [(0, 1, 0, NT), (2, 3, 0, NT)], [(tm, tn)], lambda accs, refs, rows: accs,
        [((t, n_in), F32, (tm, tn), lambda i, j, k: (i, j), False)], comm=comms.get("proj"))
    out_a, sent["attention"] = _attention_fwd(proj, tables, sinks, comms.get("attention"))
    (out_b, o_raw, states), sent["hgrn"] = _hgrn_fwd(proj, hlb, wn, comms.get("hgrn"))
    wua, wub, w_out = weights(sent)

    def epi_merge(accs, refs, rows):
        ua, ub = accs
        return [_sig(refs[4][rows]) * ua + _sig(refs[5][rows]) * ub, ua, ub]

    wide = ((t, d), BF16, (tm, nb), lambda i, j, k: (i, j), False)
    (merged, ua, ub), sent["merge"] = _mmc(
        "mix_merge", (t // tm, N_DEV, 1),
        [(out_a, (tm, A_WIDTH), lambda i, j, k: (i, 0)),
         (wua, (None, A_WIDTH, nb), lambda i, j, k: (j, 0, 0)),
         (out_b, (tm, B_WIDTH), lambda i, j, k: (i, 0)),
         (wub, (None, B_WIDTH, nb), lambda i, j, k: (j, 0, 0)),
         (proj, (tm, nb), lambda i, j, k: (i, OFF_GA // nb + j)),
         (proj, (tm, nb), lambda i, j, k: (i, (OFF_GA + d) // nb + j))],
        [(0, 1, 0, NN), (2, 3, 1, NN)], [(tm, nb)] * 2, epi_merge, [wide] * 3, comm=comms.get("merge"))

    tn2 = _tile(d, 1024, 128)
    (x2,) = _mm(
        "mix_out", (t // tm, d // tn2, 1),
        [(merged, (tm, d), lambda i, j, k: (i, 0)), (w_out, (d, tn2), lambda i, j, k: (0, j)),
         (x1, (tm, tn2), lambda i, j, k: (i, j))],
        [(0, 1, 0, NN)], [(tm, tn2)], lambda accs, refs, rows: [refs[2][rows] + accs[0]],
        [((t, d), F32, (tm, tn2), lambda i, j, k: (i, j), False)])
    saved = dict(h2=h2, proj=proj, out_a=out_a, out_b=out_b, o_raw=o_raw, states=states,
                 merged=merged, ua=ua, ub=ub)
    return x2, saved, sent


def _mixer_bwd(dx2, dx2_16, x1, mix_norm, w_in, sinks, hlb, wn, wua, wub, w_out, tables, sv, comms):
    t, d = x1.shape
    sent = {}
    avail = dict(sent=sent)
    n_in = w_in[0].shape[0]
    nb = d // N_DEV
    proj = sv["proj"]
    tm = _tile(t, 1024, 16)
    tn = _tile(d, 512, 128)

    def epi_gate(accs, refs, rows):
        dm = accs[0]
        sa = _sig(refs[2][rows])
        sb = _sig(refs[3][rows])
        ua = refs[4][rows].astype(F32)
        ub = refs[5][rows].astype(F32)
        return [dm * sa, dm * sb, dm * ua * sa * (1.0 - sa), dm * ub * sb * (1.0 - sb)]

    wide = ((t, d), BF16, (tm, tn), lambda i, j, k: (i, j), False)
    dua, dub, dga, dgb = _mm(
        "mix_dmerge", (t // tm, d // tn, 1),
        [(dx2_16, (tm, d), lambda i, j, k: (i, 0)),
         (w_out, (tn, d), lambda i, j, k: (j, 0)),
         (proj, (tm, tn), lambda i, j, k: (i, OFF_GA // tn + j)),
         (proj, (tm, tn), lambda i, j, k: (i, (OFF_GA + d) // tn + j)),
         (sv["ua"], (tm, tn), lambda i, j, k: (i, j)),
         (sv["ub"], (tm, tn), lambda i, j, k: (i, j))],
        [(0, 1, 0, NT)], [(tm, tn)], epi_gate, [wide] * 4)

    tk = _tile(t, 1024, 16)
    two = _tile(d, 1024, 128)
    (dw_out,) = _mm(
        "mix_dwout", (d // two, 1, t // tk),
        [(sv["merged"], (tk, two), lambda i, j, k: (k, i)), (dx2_16, (tk, d), lambda i, j, k: (k, 0))],
        [(0, 1, 0, TN)], [(two, d)], lambda accs, refs, rows: accs,
        [((d, d), BF16, (two, d), lambda i, j, k: (i, 0), False)])

    w_spec = lambda rows: ((N_DEV, rows, nb), BF16, (None, rows, nb), lambda i, j, k: (j, 0, 0), False)
    tk2 = _tile(t, 2048, 16)
    dwua, dwub = _mm(
        "mix_dwup", (1, N_DEV, t // tk2),
        [(sv["out_a"], (tk2, A_WIDTH), lambda i, j, k: (k, 0)),
         (dua, (tk2, nb), lambda i, j, k: (k, j)),
         (sv["out_b"], (tk2, B_WIDTH), lambda i, j, k: (k, 0)),
         (dub, (tk2, nb), lambda i, j, k: (k, j))],
        [(0, 1, 0, TN), (2, 3, 1, TN)], [(A_WIDTH, nb), (B_WIDTH, nb)], lambda accs, refs, rows: accs,
        [w_spec(A_WIDTH), w_spec(B_WIDTH)])

    d_out_a, d_out_b = _mm(
        "mix_dup", (t // tm, 1, N_DEV),
        [(dua, (tm, nb), lambda i, j, k: (i, k)),
         (wua, (None, A_WIDTH, nb), lambda i, j, k: (k, 0, 0)),
         (dub, (tm, nb), lambda i, j, k: (i, k)),
         (wub, (None, B_WIDTH, nb), lambda i, j, k: (k, 0, 0))],
        [(0, 1, 0, NT), (2, 3, 1, NT)], [(tm, A_WIDTH), (tm, B_WIDTH)], lambda accs, refs, rows: accs,
        [((t, A_WIDTH), BF16, (tm, A_WIDTH), lambda i, j, k: (i, 0), False),
         ((t, B_WIDTH), BF16, (tm, B_WIDTH), lambda i, j, k: (i, 0), False)])

    avail.update(w_out=dw_out.reshape(N_DEV, d // N_DEV, d), w_up_a=dwua, w_up_b=dwub)
    (dq_a, dk_a, dv_a, dsinks), sent["attention"] = _attention_bwd(
        proj, tables, sinks, d_out_a, _carried(comms, "attention", avail))
    (dq_b, df_b, di_b, dog_b, dlb, dwn), sent["hgrn"] = _hgrn_bwd(
        proj, hlb, wn, sv["o_raw"], sv["states"], d_out_b, _carried(comms, "hgrn", avail))
    dproj = jnp.concatenate([dq_a, dk_a, dv_a, dq_b, df_b, di_b, dog_b, dga, dgb], axis=1)

    tdm = _tile(d, 1024, 128)
    twn = _tile(n_in, 2432, 128)
    tkw = _tile(t, 512, 16)
    (dw_in,) = _mm(
        "mix_dwin", (n_in // twn, d // tdm, t // tkw),
        [(dproj, (tkw, twn), lambda i, j, k: (k, i)), (sv["h2"], (tkw, tdm), lambda i, j, k: (k, j))],
        [(0, 1, 0, TN)], [(twn, tdm)], lambda accs, refs, rows: accs,
        [((n_in, d), BF16, (twn, tdm), lambda i, j, k: (i, j), False)])

    avail.update(w_in=dw_in.reshape(N_DEV, n_in // N_DEV, d))
    tm3 = _tile(t, 512, 16)
    tpn = _tile(n_in, 512, 128)

    def epi_dx(accs, refs, rows):
        dh = jnp.concatenate(accs, axis=1)
        dx, dw = _rms_bwd(dh, refs[3][rows], refs[4][...], refs[5][rows])
        return [dx, dx, dw]

    (dx1, dx1_16, dmix), sent["dx"] = _mmc(
        "mix_dx", (t // tm3, 1, n_in // tpn),
        [(dproj, (tm3, tpn), lambda i, j, k: (i, k)),
         (w_in[0], (tpn, d // 2), lambda i, j, k: (k, 0)), (w_in[1], (tpn, d // 2), lambda i, j, k: (k, 0)),
         (x1, (tm3, d), lambda i, j, k: (i, 0)), (mix_norm, (1, d), lambda i, j, k: (0, 0)),
         (dx2, (tm3, d), lambda i, j, k: (i, 0))],
        [(0, 1, 0, NN), (0, 2, 1, NN)], [(tm3, d // 2)] * 2, epi_dx,
        [((t, d), F32, (tm3, d), lambda i, j, k: (i, 0), False),
         ((t, d), BF16, (tm3, d), lambda i, j, k: (i, 0), False),
         ((1, d), F32, (1, d), lambda i, j, k: (0, 0), True)],
        comm=_carried(comms, "dx", avail))
    small = dict(mix_norm=dmix, attn_sinks=dsinks, hgrn_lb=dlb, hgrn_norm=dwn)
    return dx1, dx1_16, small, avail


def _ple_fwd(x3, ple_norm, p16, wpg, wpp):
    t, d = x3.shape
    nb = d // N_DEV
    h4 = _rmsnorm("ple_norm", x3, ple_norm)
    tm = _tile(t, 1024, 16)

    def epi(accs, refs, rows):
        zg, pp = accs
        return [refs[4][rows] + _sig(zg) * pp, zg, pp]

    blk = lambda dt: ((t, d), dt, (tm, nb), lambda i, j, k: (i, j), False)
    x4, zg, pp = _mm(
        "ple_fwd", (t // tm, N_DEV, 1),
        [(h4, (tm, d), lambda i, j, k: (i, 0)), (wpg, (d, nb), lambda i, j, k: (0, j)),
         (p16, (tm, PLE_DIM), lambda i, j, k: (i, 0)),
         (wpp, (None, PLE_DIM, nb), lambda i, j, k: (j, 0, 0)),
         (x3, (tm, nb), lambda i, j, k: (i, j))],
        [(0, 1, 0, NN), (2, 3, 1, NN)], [(tm, nb)] * 2, epi, [blk(F32), blk(BF16), blk(BF16)])
    return x4, dict(h4=h4, zg=zg, pp=pp)


def _ple_bwd(dx4, x3, ple_norm, p16, wpg, sv):
    t, d = x3.shape
    nb = d // N_DEV

    def gate_grads(dx, zg, pp):
        s = _sig(zg.astype(F32))
        return dx * pp.astype(F32) * s * (1.0 - s), dx * s

    dzg, dpp = _rows("ple_dgate", gate_grads, [(dx4, True), (sv["zg"], True), (sv["pp"], True)],
                     [((t, d), BF16), ((t, d), BF16)], _tile(t, 512, 16))
    (dwpp,) = _mm(
        "ple_dwproj", (1, N_DEV, 1),
        [(p16, (t, PLE_DIM), lambda i, j, k: (0, 0)), (dpp, (t, nb), lambda i, j, k: (0, j))],
        [(0, 1, 0, TN)], [(PLE_DIM, nb)], lambda accs, refs, rows: accs,
        [((N_DEV, PLE_DIM, nb), BF16, (None, PLE_DIM, nb), lambda i, j, k: (j, 0, 0), False)])
    tk = _tile(t, 1024, 16)
    tn = _tile(d, 1024, 128)
    (dwpg,) = _mm(
        "ple_dwgate", (d // tn, 1, t // tk),
        [(sv["h4"], (tk, tn), lambda i, j, k: (k, i)), (dzg, (tk, d), lambda i, j, k: (k, 0))],
        [(0, 1, 0, TN)], [(tn, d)], lambda accs, refs, rows: accs,
        [((d, d), BF16, (tn, d), lambda i, j, k: (i, 0), False)])
    tm3 = _tile(t, 256, 16)

    def epi_dx(accs, refs, rows):
        dx, dw = _rms_bwd(accs[0], refs[2][rows], refs[3][...], refs[4][rows])
        return [dx, dx, dw]

    dx3, dx3_16, dnorm = _mm(
        "ple_dx", (t // tm3, 1, 1),
        [(dzg, (tm3, d), lambda i, j, k: (i, 0)), (wpg, (d, d), lambda i, j, k: (0, 0)),
         (x3, (tm3, d), lambda i, j, k: (i, 0)), (ple_norm, (1, d), lambda i, j, k: (0, 0)),
         (dx4, (tm3, d), lambda i, j, k: (i, 0))],
        [(0, 1, 0, NT)], [(tm3, d)], epi_dx,
        [((t, d), F32, (tm3, d), lambda i, j, k: (i, 0), False),
         ((t, d), BF16, (tm3, d), lambda i, j, k: (i, 0), False),
         ((1, d), F32, (1, d), lambda i, j, k: (0, 0), True)])
    return dx3, dx3_16, dwpp, dwpg, dnorm


def _loss_head(x4, final_norm, target):
    t, d = x4.shape
    tm = _tile(t, 256, 8)

    def body(x_ref, w_ref, tgt_ref, dx_ref, dw_ref, loss_ref):
        i = pl.program_id(0)
        xv = x_ref[...]
        wv = w_ref[...]
        r = lax.rsqrt(jnp.mean(xv * xv, axis=-1, keepdims=True) + EPS)
        err = xv * r * wv - tgt_ref[...]
        part = 0.5 * jnp.sum(jnp.mean(err * err, axis=-1, keepdims=True), axis=0, keepdims=True)
        dx, dw = _rms_bwd(err * (1.0 / d), xv, wv, jnp.zeros_like(xv))
        dx_ref[...] = dx

        @pl.when(i == 0)
        def _():
            dw_ref[...] = dw
            loss_ref[...] = jnp.broadcast_to(part, loss_ref.shape)

        @pl.when(i > 0)
        def _():
            dw_ref[...] += dw
            loss_ref[...] += jnp.broadcast_to(part, loss_ref.shape)

    return pl.pallas_call(
        body,
        name="loss_head",
        grid=(t // tm,),
        in_specs=[pl.BlockSpec((tm, d), lambda i: (i, 0)), pl.BlockSpec((1, d), lambda i: (0, 0)),
                  pl.BlockSpec((tm, d), lambda i: (i, 0))],
        out_specs=[pl.BlockSpec((tm, d), lambda i: (i, 0)), pl.BlockSpec((1, d), lambda i: (0, 0)),
                   pl.BlockSpec((1, LANES), lambda i: (0, 0))],
        out_shape=[jax.ShapeDtypeStruct((t, d), F32), jax.ShapeDtypeStruct((1, d), F32),
                   jax.ShapeDtypeStruct((1, LANES), F32)],
        compiler_params=_params(("arbitrary",)),
    )(x4, final_norm, target)


RELATIONS = ((0, 0), (1, 0), (0, 1), (1, 1))


def _place():
    return lax.axis_index("x"), lax.axis_index("y"), lax.axis_index("c")


def _flip(v, bit):
    return 1 - v if bit else v


def _gather_program(shards, mid_at):
    n = len(shards)
    slots = 8

    def copies(ins, outs, send_sems, recv_sems, local_sems, base):
        x, y, c = _place()
        sibling, x_chip, y_chip = (x, y, 1 - c), (1 - x, y, c), (x, 1 - y, c)

        def block(a, r, core, half=None):
            ref = outs[a].at[4 * _flip(x, RELATIONS[r][0]) + 2 * _flip(y, RELATIONS[r][1]) + core]
            rows = shards[a].shape[0] // 2
            return ref if half is None else ref.at[pl.ds(half * rows, rows)]

        def copy(a, slot, dst, to, src=None):
            return pltpu.make_async_remote_copy(
                src_ref=dst if src is None else src, dst_ref=dst,
                send_sem=send_sems.at[base[0] + slots * a + slot],
                recv_sem=recv_sems.at[base[1] + slots * a + slot], device_id=to, device_id_type=MESH)

        own = [pltpu.make_async_copy(ins[a], block(a, 0, c), local_sems.at[base[2] + a]) for a in range(n)]
        sent_first = [cp for a in range(n) for cp in (
            copy(a, 0, block(a, 0, c), sibling, src=ins[a]),
            copy(a, 1, block(a, 0, c), x_chip, src=ins[a]),
            copy(a, 2, block(a, 0, c), y_chip, src=ins[a]))]
        landed_first = [cp for a in range(n) for cp in (
            copy(a, 1, block(a, 1, c), x_chip), copy(a, 2, block(a, 2, c), y_chip))]
        sent_mid = [cp for a in range(n) for cp in (
            copy(a, 3, block(a, 1, c, 0), y_chip), copy(a, 4, block(a, 2, c, 1), x_chip),
            copy(a, 5, block(a, 1, c), sibling), copy(a, 6, block(a, 2, c), sibling))]
        landed_mid = [cp for a in range(n) for cp in (
            copy(a, 3, block(a, 3, c, 0), y_chip), copy(a, 4, block(a, 3, c, 1), x_chip))]
        sent_last = [copy(a, 7, block(a, 3, c), sibling) for a in range(n)]
        landed_sibling = [cp for a in range(n) for cp in (
            copy(a, 0, block(a, 0, 1 - c), sibling), copy(a, 5, block(a, 1, 1 - c), sibling),
            copy(a, 6, block(a, 2, 1 - c), sibling), copy(a, 7, block(a, 3, 1 - c), sibling))]
        return own, sent_first, landed_first, sent_mid, landed_mid, sent_last, landed_sibling

    def pre(*args):
        own, sent_first = copies(*args)[:2]
        for cp in own + sent_first:
            cp.start()

    def mid(*args):
        _, _, landed_first, sent_mid, _, _, _ = copies(*args)
        for cp in landed_first:
            cp.wait_recv()
        for cp in sent_mid:
            cp.start()

    def late(*args):
        _, _, _, _, landed_mid, sent_last, _ = copies(*args)
        for cp in landed_mid:
            cp.wait_recv()
        for cp in sent_last:
            cp.start()

    def post(*args):
        own, sent_first, _, sent_mid, _, sent_last, landed_sibling = copies(*args)
        for cp in landed_sibling:
            cp.wait_recv()
        for cp in sent_first + sent_mid + sent_last:
            cp.wait_send()
        for mine in own:
            mine.wait()

    return _Comm(list(shards), [jax.ShapeDtypeStruct((N_DEV,) + s.shape, s.dtype) for s in shards],
                 (slots * n, slots * n, n), pre, post, mid, mid_at, late)


def _exchange_program(arrays, n_slots, source, target, slot):
    n = len(arrays)

    def copies(ins, outs, send_sems, recv_sems, local_sems, base):
        place = _place()
        res = []
        for a in range(n):
            for r in range(n_slots):
                src = source(r, place)
                if src is None:
                    continue
                res.append(pltpu.make_async_remote_copy(
                    src_ref=ins[a].at[src], dst_ref=outs[a].at[slot(r)],
                    send_sem=send_sems.at[base[0] + n_slots * a + r],
                    recv_sem=recv_sems.at[base[1] + n_slots * a + r],
                    device_id=target(r, place), device_id_type=MESH))
        return res

    def pre(*args):
        for cp in copies(*args):
            cp.start()

    def post(*args):
        for cp in copies(*args):
            cp.wait()

    n_out = len({slot(r) for r in range(n_slots) if source(r, (0, 0, 0)) is not None})
    return _Comm(list(arrays), [jax.ShapeDtypeStruct((n_out,) + g.shape[1:], g.dtype) for g in arrays],
                 (n_slots * n, n_slots * n, 0), pre, post)


def _sibling_program(grads):
    def source(r, place):
        x, y, c = place
        return 4 * _flip(x, RELATIONS[r][0]) + 2 * _flip(y, RELATIONS[r][1]) + (1 - c)

    return _exchange_program(grads, 4, source, lambda r, p: (p[0], p[1], 1 - p[2]), lambda r: r)


def _chips_program(sums, relations=(1, 2, 3)):
    def target(r, place):
        x, y, c = place
        return (_flip(x, RELATIONS[r][0]), _flip(y, RELATIONS[r][1]), c)

    return _exchange_program(sums, 4, lambda r, p: r if r in relations else None, target,
                             lambda r: relations.index(r) if r in relations else 0)


def _standalone(name, comm):
    return _call(name, lambda: None, (1,), [], [], [], [], [], comm)[1]


def _pair_sum(name, grad, from_sibling, block_ids):
    _, rows, cols = grad.shape
    tr = _tile(rows, 512, 16)

    def body(ids_ref, mine_ref, sib_ref, o_ref):
        o_ref[...] = (mine_ref[...].astype(F32) + sib_ref[...].astype(F32)).astype(o_ref.dtype)

    return pl.pallas_call(
        body,
        name=name,
        grid_spec=pltpu.PrefetchScalarGridSpec(
            num_scalar_prefetch=1,
            grid=(4, rows // tr),
            in_specs=[pl.BlockSpec((None, tr, cols), lambda r, i, ids: (ids[r], i, 0)),
                      pl.BlockSpec((None, tr, cols), lambda r, i, ids: (r, i, 0))],
            out_specs=pl.BlockSpec((None, tr, cols), lambda r, i, ids: (r, i, 0))),
        out_shape=jax.ShapeDtypeStruct((4, rows, cols), BF16),
        compiler_params=_params(("arbitrary", "arbitrary")),
    )(block_ids, grad, from_sibling)


def _adam(w, g, m, v):
    m = ADAM_B1 * m + (1.0 - ADAM_B1) * g
    v = ADAM_B2 * v + (1.0 - ADAM_B2) * (g * g)
    m_hat = m / (1.0 - ADAM_B1 ** ADAM_STEP)
    v_hat = v / (1.0 - ADAM_B2 ** ADAM_STEP)
    delta = -ADAM_LR * (m_hat / (jnp.sqrt(v_hat) + ADAM_EPS) + ADAM_WD * w)
    return delta, m, v


def _adam_sharded(name, items, comm=None):
    _, rows, cols = items[0][0].shape
    n = len(items)
    tr = _tile(rows, max(64, 256 // n), 8)

    def body(*refs):
        for k in range(n):
            w_ref, m_ref, v_ref, s_ref, r1_ref, r2_ref, r3_ref = refs[7 * k:7 * k + 7]
            g_ref, d_ref, nm_ref, nv_ref = refs[7 * n + 4 * k:7 * n + 4 * k + 4]
            g = ((s_ref[...].astype(F32) + r1_ref[...].astype(F32)) + r2_ref[...].astype(F32)) + r3_ref[...].astype(F32)
            delta, nm, nv = _adam(w_ref[...], g, m_ref[...], v_ref[...])
            g_ref[...] = g
            d_ref[...] = delta
            nm_ref[...] = nm
            nv_ref[...] = nv

    slot = lambda s: pl.BlockSpec((None, tr, cols), lambda i: (s, i, 0))
    flat = slot(0)
    args, specs = [], []
    for w, m, v, sums, from_chips in items:
        args += [w, m, v, sums] + [a for a, _ in from_chips]
        specs += [flat, flat, flat, slot(0)] + [slot(s) for _, s in from_chips]
    res, sent = _call(
        name, body, (rows // tr,), specs, [flat] * (4 * n),
        [jax.ShapeDtypeStruct((1, rows, cols), F32)] * (4 * n), [], args, comm)
    return [res[4 * k:4 * k + 4] for k in range(n)], sent


def _all_reduce_small(part):
    rows = part.shape[0]

    def body(x_ref, o_ref, gathered, send_sems, recv_sems):
        x, y, c = _place()
        me = 4 * x + 2 * y + c
        gathered[me] = x_ref[...]
        copies = []
        for k in range(1, N_DEV):
            bits = (k >> 2 & 1, k >> 1 & 1, k & 1)
            peer = (_flip(x, bits[0]), _flip(y, bits[1]), _flip(c, bits[2]))
            copies.append(pltpu.make_async_remote_copy(
                src_ref=x_ref, dst_ref=gathered.at[me],
                send_sem=send_sems.at[k - 1], recv_sem=recv_sems.at[k - 1],
                device_id=peer, device_id_type=MESH))
        for cp in copies:
            cp.start()
        for cp in copies:
            cp.wait()
        total = gathered[0]
        for b in range(1, N_DEV):
            total = total + gathered[b]
        o_ref[...] = total

    return pl.pallas_call(
        body,
        name="small_all_reduce",
        in_specs=[pl.BlockSpec(memory_space=pltpu.VMEM)],
        out_specs=pl.BlockSpec(memory_space=pltpu.VMEM),
        out_shape=jax.ShapeDtypeStruct((rows, LANES), F32),
        scratch_shapes=[pltpu.VMEM((N_DEV, rows, LANES), F32),
                        pltpu.SemaphoreType.DMA((N_DEV - 1,)), pltpu.SemaphoreType.DMA((N_DEV - 1,))],
    )(part)


def _adam_small(w, g, m, v):
    def fn(wv, gv, mv, vv):
        return _adam(wv, gv, mv, vv)

    shape = (w.shape, F32)
    return _rows("adam_small", fn, [(w, True), (g, True), (m, True), (v, True)], [shape] * 3, w.shape[0])


SMALL = ("ffn1_norm", "mix_norm", "attn_sinks", "hgrn_lower_bound", "hgrn_norm", "ffn2_norm", "ple_norm",
         "final_norm")
LARGE = ("ffn1_w_gate", "ffn1_w_up", "ffn1_w_down", "w_in", "w_up_a", "w_up_b", "w_out",
         "ffn2_w_gate", "ffn2_w_up", "ffn2_w_down", "ple_w_gate", "ple_w_proj")
TRANSPOSED = ("ffn1_w_gate", "ffn1_w_up", "ffn2_w_gate", "ffn2_w_up", "w_in")
ADAM_GROUPS = (("ffn2_w_down", "ffn2_w_gate", "ffn2_w_up"), ("ple_w_gate", "w_out"), ("w_up_a", "w_up_b"),
               ("ple_w_proj",), ("w_in",), ("ffn1_w_down", "ffn1_w_gate", "ffn1_w_up"))
WEIGHTS = ("ffn1_norm", "ffn1_w_gate", "ffn1_w_up", "ffn1_w_down", "mix_norm", "w_in", "attn_sinks",
           "hgrn_lower_bound", "hgrn_norm", "w_up_a", "w_up_b", "w_out", "ffn2_norm", "ffn2_w_gate",
           "ffn2_w_up", "ffn2_w_down", "ple_norm", "ple_w_gate", "ple_w_proj", "final_norm")


def _pack_rows(arrays):
    rows = []
    for a in arrays:
        flat = a.reshape(-1).astype(F32)
        pad = -flat.shape[0] % LANES
        rows.append(jnp.pad(flat, (0, pad)).reshape(-1, LANES))
    packed = jnp.concatenate(rows, axis=0)
    return jnp.pad(packed, ((0, -packed.shape[0] % 8), (0, 0)))


def _unpack_rows(packed, like):
    out, at = [], 0
    for a in like:
        size = a.size
        n_rows = -(-size // LANES)
        out.append(packed[at:at + n_rows].reshape(-1)[:size].reshape(a.shape))
        at += n_rows
    return out


def kernel(x, p, positions, ffn1_norm, ffn1_w_gate, ffn1_w_up, ffn1_w_down, mix_norm, w_in, attn_sinks, hgrn_lower_bound, hgrn_norm, w_up_a, w_up_b, w_out, ffn2_norm, ffn2_w_gate, ffn2_w_up, ffn2_w_down, ple_norm, ple_w_gate, ple_w_proj, final_norm, loss_target, m_ffn1_norm, m_ffn1_w_gate, m_ffn1_w_up, m_ffn1_w_down, m_mix_norm, m_w_in, m_attn_sinks, m_hgrn_lower_bound, m_hgrn_norm, m_w_up_a, m_w_up_b, m_w_out, m_ffn2_norm, m_ffn2_w_gate, m_ffn2_w_up, m_ffn2_w_down, m_ple_norm, m_ple_w_gate, m_ple_w_proj, m_final_norm, v_ffn1_norm, v_ffn1_w_gate, v_ffn1_w_up, v_ffn1_w_down, v_mix_norm, v_w_in, v_attn_sinks, v_hgrn_lower_bound, v_hgrn_norm, v_w_up_a, v_w_up_b, v_w_out, v_ffn2_norm, v_ffn2_w_gate, v_ffn2_w_up, v_ffn2_w_down, v_ple_norm, v_ple_w_gate, v_ple_w_proj, v_final_norm):
    given = dict(locals())
    w = {n: given[n] for n in WEIGHTS}
    mom = {n: given["m_" + n] for n in WEIGHTS}
    var = {n: given["v_" + n] for n in WEIGHTS}
    t, d = x.shape[1], x.shape[2]

    x0, p16, target = x[0], p[0, 0].astype(BF16), loss_target[0]
    tables = _rope_tables(positions)
    sinks = attn_sinks.reshape(-1)
    final_w = final_norm.reshape(1, d)
    as_stored = lambda n, a: jnp.swapaxes(a, 1, 2) if n in TRANSPOSED else a
    shard = {n: as_stored(n, w[n])[0].astype(BF16) for n in LARGE}
    shard["w_in_left"], shard["w_in_right"] = shard["w_in"][:, :d // 2], shard["w_in"][:, d // 2:]
    xi, yi, ci = _place()
    block_ids = jnp.stack([4 * _flip(xi, rx) + 2 * _flip(yi, ry) + ci for rx, ry in RELATIONS]).astype(jnp.int32)

    def gather(*names):
        return _gather_program([shard[n] for n in names], 0.65)

    wg1, wu1 = _standalone("gather_ffn1", gather("ffn1_w_gate", "ffn1_w_up"))
    h1 = _rmsnorm("ffn1_norm", x0, ffn1_norm)
    (g1, u1, a1), (wd1, win_left) = _ffn_up("ffn1", h1, wg1, wu1, gather("ffn1_w_down", "w_in_left"))
    x1, (win_right,) = _ffn_down("ffn1", x0, a1, wd1, gather("w_in_right"))
    win = (win_left.reshape(-1, d // 2), win_right.reshape(-1, d // 2))
    mix_args = (mix_norm, win, sinks, hgrn_lower_bound, hgrn_norm)

    def mixer_weights(sent):
        wua, wub, wout_g, _ = sent["proj"]
        return wua, wub, wout_g.reshape(d, d)

    x2, mix_saved, got_w = _mixer_fwd(
        x1, *mix_args, mixer_weights, tables,
        dict(proj=gather("w_up_a", "w_up_b", "w_out", "ffn2_w_gate"), attention=gather("ffn2_w_up"),
             hgrn=gather("ple_w_gate", "ple_w_proj")))
    wua, wub, wout = mixer_weights(got_w)
    wg2, (wu2,) = got_w["proj"][3], got_w["attention"]
    wpg_g, wpp = got_w["hgrn"]
    wpg = wpg_g.reshape(d, d)
    h3 = _rmsnorm("ffn2_norm", x2, ffn2_norm)
    (g2, u2, a2), (wd2,) = _ffn_up("ffn2", h3, wg2, wu2, gather("ffn2_w_down"))
    x3, _ = _ffn_down("ffn2", x2, a2, wd2)
    x4, ple_saved = _ple_fwd(x3, ple_norm, p16, wpg, wpp)
    dx4, d_final, loss_row = _loss_head(x4, final_w, target)

    sums, from_chips = {}, {}

    def pair_sums(names, grads, from_sibling):
        for n, g, s in zip(names, grads, from_sibling):
            sums[n] = _pair_sum("pair_sum_" + n, g, s, block_ids)

    def to_chips(names, grads, from_sibling):
        pair_sums(names, grads, from_sibling)
        return _chips_program([sums[n] for n in names])

    def arrived(names, results):
        for n, r in zip(names, results):
            from_chips[n] = [(r, 0), (r, 1), (r, 2)]

    def ffn_comms(first, second, names):
        dn, gn, un = ((n,) for n in names)
        return dict(
            dact=first, dwd=second,
            dwg=lambda av: _sibling_program([av["dwd"]]),
            dwu=lambda av: _merge([to_chips(dn, [av["dwd"]], av["sent"]["dwg"]), _sibling_program([av["dwg"]])]),
            dxa=lambda av: _merge([to_chips(gn, [av["dwg"]], av["sent"]["dwu"][1:]), _sibling_program([av["dwu"]])]),
            dxb=lambda av: to_chips(un, [av["dwu"]], av["sent"]["dxa"][1:]))

    def ffn_arrived(names, sent):
        arrived(names, [sent["dwu"][0], sent["dxa"][0], sent["dxb"][0]])

    dx3, dx3_16, dwpp, dwpg, d_ple = _ple_bwd(dx4, x3, ple_norm, p16, wpg, ple_saved)
    ple_names = ("ple_w_proj", "ple_w_gate")
    ple_grads = [dwpp, dwpg.reshape(N_DEV, d // N_DEV, d)]
    ffn2_names = ("ffn2_w_down", "ffn2_w_gate", "ffn2_w_up")
    dx2, dx2_16, _, _, _, d_ffn2, sent2 = _ffn_bwd(
        "ffn2b", dx3, dx3_16, x2, h3, g2, u2, a2, wg2, wu2, wd2, ffn2_norm,
        ffn_comms(_sibling_program(ple_grads), lambda av: to_chips(ple_names, ple_grads, av["sent"]["dact"]),
                  ffn2_names))
    arrived(ple_names, sent2["dwd"])
    ffn_arrived(ffn2_names, sent2)
    mix3 = ("w_out", "w_up_a", "w_up_b")
    dx1, dx1_16, mix_small, mix_made = _mixer_bwd(
        dx2, dx2_16, x1, *mix_args, wua, wub, wout, tables, mix_saved,
        dict(attention=lambda av: _sibling_program([av[n] for n in mix3]),
             hgrn=lambda av: to_chips(mix3, [av[n] for n in mix3], av["sent"]["attention"]),
             dx=lambda av: _sibling_program([av["w_in"]])))
    sent_mix = mix_made["sent"]
    arrived(mix3, sent_mix["hgrn"])
    pair_sums(("w_in",), [mix_made["w_in"]], sent_mix["dx"])
    ffn1_names = ("ffn1_w_down", "ffn1_w_gate", "ffn1_w_up")
    grad_x, _, _, _, _, d_ffn1, sent1 = _ffn_bwd(
        "ffn1b", dx1, dx1_16, x0, h1, g1, u1, a1, wg1, wu1, wd1, ffn1_norm,
        ffn_comms(_chips_program([sums["w_in"]], (1, 2)), _chips_program([sums["w_in"]], (3,)), ffn1_names))
    (near,), (far,) = sent1["dact"], sent1["dwd"]
    from_chips["w_in"] = [(near, 0), (near, 1), (far, 0)]
    ffn_arrived(ffn1_names, sent1)

    out = {}
    for group in ADAM_GROUPS:
        items = [(as_stored(n, w[n]), as_stored(n, mom[n]), as_stored(n, var[n]), sums[n], from_chips[n])
                 for n in group]
        results, _ = _adam_sharded("adam_" + group[0], items)
        for n, res in zip(group, results):
            out[n] = [as_stored(n, a) for a in res]

    dlb = mix_small["hgrn_lb"]
    small_grads = dict(
        ffn1_norm=d_ffn1, mix_norm=mix_small["mix_norm"], attn_sinks=mix_small["attn_sinks"][:, :A_HEADS],
        hgrn_lower_bound=jnp.concatenate([dlb, -dlb], axis=0), hgrn_norm=mix_small["hgrn_norm"],
        ffn2_norm=d_ffn2, ple_norm=d_ple, final_norm=d_final)

    small_like = [w[n] for n in SMALL]
    packed = _pack_rows([small_grads[n].reshape(w[n].shape) for n in SMALL] + [loss_row[:, :1]])
    total = _all_reduce_small(packed)
    n_rows = total.shape[0]
    pack_w = _pack_rows(small_like + [jnp.zeros((1, 1), F32)])
    pack_m = _pack_rows([mom[n] for n in SMALL] + [jnp.zeros((1, 1), F32)])
    pack_v = _pack_rows([var[n] for n in SMALL] + [jnp.zeros((1, 1), F32)])
    deltas = _adam_small(pack_w, total, pack_m, pack_v)
    like = small_like + [jnp.zeros((1, 1), F32)]
    g_small = _unpack_rows(total, like)
    d_small, m_small, v_small = (_unpack_rows(a, like) for a in deltas)
    for k, n in enumerate(SMALL):
        out[n] = [g_small[k], d_small[k], m_small[k], v_small[k]]
    loss = g_small[-1].reshape(())

    return (loss, grad_x.reshape(x.shape),
            *[out[n][0] for n in WEIGHTS], *[out[n][1] for n in WEIGHTS],
            *[out[n][2] for n in WEIGHTS], *[out[n][3] for n in WEIGHTS])
```

```python
import functools

import jax
import jax.numpy as jnp
from jax import lax
from jax.experimental import pallas as pl
from jax.experimental.pallas import tpu as pltpu

F32 = jnp.float32
BF16 = jnp.bfloat16
MESH = pl.DeviceIdType.MESH

EPS = 1e-6
N_DEV = 8
A_HEADS = 16
A_HEAD_DIM = 64
A_WIDTH = 1024
A_KV_WIDTH = 256
ATT_BLOCK = 128
ROT_DIM = 16
ROPE_THETA = 500000.0
B_WIDTH = 1024
B_HEAD = 128
CHUNK = 64
SUB = 16
MAX_DECAY_EXP = 60.0
PLE_DIM = 256
LANES = 128

ADAM_LR = 0.001
ADAM_B1 = 0.9
ADAM_B2 = 0.999
ADAM_EPS = 1e-08
ADAM_WD = 0.01
ADAM_STEP = 10

VMEM_LIMIT = 56 * 1024 * 1024
EPILOGUE_ELEMS = 256 * 1024
LATE_AT = 0.92

NN = (((1,), (0,)), ((), ()))
NT = (((1,), (1,)), ((), ()))
TN = (((0,), (0,)), ((), ()))

OFF_QA = 0
OFF_KA = 1024
OFF_VA = 1280
OFF_QB = 1536
OFF_FB = 2560
OFF_IB = 3584
OFF_OG = 4608
OFF_GA = 5632


def _sig(v):
    return 1.0 / (1.0 + jnp.exp(-v))


def _tile(n, pref, mult):
    t = min(n, pref)
    t -= t % mult
    while n % t:
        t -= mult
    return t


def _params(sem):
    return pltpu.CompilerParams(dimension_semantics=sem, vmem_limit_bytes=VMEM_LIMIT)


HBM_SPEC = pl.BlockSpec(memory_space=pltpu.HBM)


class _Comm:
    def __init__(self, ins, out_shapes, n_sems, pre, post, mid=None, mid_at=1.0, late=None):
        self.ins, self.out_shapes, self.n_sems, self.pre, self.post = ins, out_shapes, n_sems, pre, post
        self.mid = mid if mid is not None else (lambda *args: None)
        self.late = late if late is not None else (lambda *args: None)
        self.mid_at = mid_at
        self.parts = [len(out_shapes)]

    def split(self, outs):
        res, at = [], 0
        for n in self.parts:
            res.append(list(outs[at:at + n]))
            at += n
        return res


def _merge(progs):
    spans, ins, shapes, sems = [], [], [], [0, 0, 0]
    for p in progs:
        spans.append((len(ins), len(shapes), tuple(sems)))
        ins += list(p.ins)
        shapes += list(p.out_shapes)
        sems = [a + b for a, b in zip(sems, p.n_sems)]

    def run(which):
        def go(cin, cout, send, recv, local, base):
            for p, (i0, o0, s0) in zip(progs, spans):
                getattr(p, which)(cin[i0:i0 + len(p.ins)], cout[o0:o0 + len(p.out_shapes)], send, recv, local,
                                  tuple(b + s for b, s in zip(base, s0)))
        return go

    merged = _Comm(ins, shapes, tuple(sems), run("pre"), run("post"), run("mid"), max(p.mid_at for p in progs),
                   run("late"))
    merged.parts = [len(p.out_shapes) for p in progs]
    return merged


def _call(name, body, grid, in_specs, out_specs, out_shape, scratch, args, comm=None, aliases=None):
    sem = ("arbitrary",) * len(grid)
    aliases = aliases or {}
    if comm is None:
        res = pl.pallas_call(body, name=name, grid=grid, in_specs=in_specs, out_specs=out_specs,
                             out_shape=out_shape, scratch_shapes=scratch, input_output_aliases=aliases,
                             compiler_params=_params(sem))(*args)
        return list(res), []
    n_in, n_out, n_scr = len(in_specs), len(out_specs), len(scratch)
    c_in, c_out = len(comm.ins), len(comm.out_shapes)
    steps = functools.reduce(lambda a, b: a * b, grid)
    mid_step = min(steps - 1, int(comm.mid_at * steps))
    late_step = min(steps - 1, max(mid_step, int(LATE_AT * steps)))

    def carrier(*refs):
        at = [0]

        def take(n):
            at[0] += n
            return refs[at[0] - n:at[0]]

        ins, cins, outs, couts, scr, sems = take(n_in), take(c_in), take(n_out), take(c_out), take(n_scr), take(3)
        ids = [pl.program_id(a) for a in range(len(grid))]
        step = functools.reduce(lambda acc, ig: acc * ig[1] + ig[0], zip(ids, grid), 0)

        @pl.when(step == 0)
        def _():
            comm.pre(cins, couts, *sems, (0, 0, 0))

        body(*ins, *outs, *scr)

        @pl.when(step == mid_step)
        def _():
            comm.mid(cins, couts, *sems, (0, 0, 0))

        @pl.when(step == late_step)
        def _():
            comm.late(cins, couts, *sems, (0, 0, 0))

        @pl.when(step == steps - 1)
        def _():
            comm.post(cins, couts, *sems, (0, 0, 0))

    res = pl.pallas_call(
        carrier, name=name, grid=grid,
        in_specs=list(in_specs) + [HBM_SPEC] * c_in,
        out_specs=list(out_specs) + [HBM_SPEC] * c_out,
        out_shape=list(out_shape) + list(comm.out_shapes),
        scratch_shapes=list(scratch) + [pltpu.SemaphoreType.DMA((max(n, 1),)) for n in comm.n_sems],
        input_output_aliases=aliases,
        compiler_params=_params(sem),
    )(*args, *comm.ins)
    return list(res[:n_out]), list(res[n_out:])


def _mm(name, grid, ins, prods, acc_shapes, epi, outs, comm=None, aliases=None):
    n_in, n_out, nk = len(ins), len(outs), grid[2]
    tile_rows, tile_cols = acc_shapes[0]
    chunk = tile_rows
    while chunk * tile_cols > EPILOGUE_ELEMS and chunk % 32 == 0:
        chunk //= 2

    def body(*refs):
        in_refs = refs[:n_in]
        out_refs = refs[n_in:n_in + n_out]
        acc_refs = refs[n_in + n_out:]
        first_tile = jnp.logical_and(pl.program_id(0) == 0, pl.program_id(1) == 0)

        def product(lhs, rhs, dims):
            a = in_refs[lhs][...].astype(BF16)
            b = in_refs[rhs][...].astype(BF16)
            return lax.dot_general(a, b, dims, preferred_element_type=F32)

        def write(accs):
            totals = [None] * n_out
            for c in range(tile_rows // chunk):
                rows = slice(c * chunk, (c + 1) * chunk)
                vals = epi([a[rows] for a in accs], in_refs, rows)
                for idx, (o, v, spec) in enumerate(zip(out_refs, vals, outs)):
                    if spec[4]:
                        totals[idx] = v if totals[idx] is None else totals[idx] + v
                    else:
                        o[rows] = v.astype(o.dtype)
            for o, v, spec in zip(out_refs, totals, outs):
                if spec[4]:
                    @pl.when(first_tile)
                    def _():
                        o[...] = v.astype(o.dtype)

                    @pl.when(jnp.logical_not(first_tile))
                    def _():
                        o[...] += v.astype(o.dtype)

        if nk == 1:
            accs = [None] * len(acc_shapes)
            for lhs, rhs, acc, dims in prods:
                t = product(lhs, rhs, dims)
                accs[acc] = t if accs[acc] is None else accs[acc] + t
            write(accs)
        else:
            k = pl.program_id(2)

            @pl.when(k == 0)
            def _():
                for r in acc_refs:
                    r[...] = jnp.zeros(r.shape, F32)

            for lhs, rhs, acc, dims in prods:
                acc_refs[acc][...] += product(lhs, rhs, dims)

            @pl.when(k == nk - 1)
            def _():
                write(acc_refs)

    res, sent = _call(
        name, body, grid,
        [pl.BlockSpec(memory_space=pl.ANY) if b is None else pl.BlockSpec(b, im) for _, b, im in ins],
        [pl.BlockSpec(o[2], o[3]) for o in outs],
        [jax.ShapeDtypeStruct(o[0], o[1]) for o in outs],
        [] if nk == 1 else [pltpu.VMEM(s, F32) for s in acc_shapes],
        [a for a, _, _ in ins], comm, aliases)
    return res if comm is None else (res, sent)


def _rms_bwd(dh, xv, w, dres):
    r = lax.rsqrt(jnp.mean(xv * xv, axis=-1, keepdims=True) + EPS)
    g = dh * w
    dx = dres + r * (g - xv * (r * r) * jnp.mean(g * xv, axis=-1, keepdims=True))
    dw = jnp.sum(dh * xv * r, axis=0, keepdims=True)
    return dx, dw


def _rows(name, fn, ins, outs, tm):
    n_in = len(ins)
    rows = outs[0][0][0]

    def body(*refs):
        vals = fn(*[r[...] for r in refs[:n_in]])
        for o, v in zip(refs[n_in:], vals):
            o[...] = v.astype(o.dtype)

    in_specs = []
    for a, tiled in ins:
        if tiled:
            in_specs.append(pl.BlockSpec((tm, a.shape[1]), lambda i: (i, 0)))
        else:
            in_specs.append(pl.BlockSpec(a.shape, lambda i, nd=a.ndim: (0,) * nd))
    return pl.pallas_call(
        body,
        name=name,
        grid=(rows // tm,),
        in_specs=in_specs,
        out_specs=[pl.BlockSpec((tm, s[1]), lambda i: (i, 0)) for s, _ in outs],
        out_shape=[jax.ShapeDtypeStruct(s, d) for s, d in outs],
        compiler_params=_params(("arbitrary",)),
    )(*[a for a, _ in ins])


def _rmsnorm(name, x, w):
    def fn(xv, wv):
        r = lax.rsqrt(jnp.mean(xv * xv, axis=-1, keepdims=True) + EPS)
        return (xv * r * wv,)

    return _rows(name, fn, [(x, True), (w, False)], [(x.shape, BF16)], _tile(x.shape[0], 512, 16))[0]


def _mmc(*args, comm=None, aliases=None):
    if comm is None:
        return _mm(*args, aliases=aliases), []
    return _mm(*args, comm=comm, aliases=aliases)


def _carried(comms, key, avail):
    c = comms.get(key) if comms else None
    return c(avail) if callable(c) else c


def _ffn_up(name, h, wg, wu, comm=None):
    t, d = h.shape
    n = wg.shape[1]
    tm = _tile(t, 1024, 16)

    def epi_up(accs, refs, rows):
        g, u = accs
        return [g, u, g * _sig(g) * u]

    hid = ((N_DEV, t, n), BF16, (None, tm, n), lambda i, j, k: (j, i, 0), False)
    return _mmc(
        name + "_up", (t // tm, N_DEV, 1),
        [(h, (tm, d), lambda i, j, k: (i, 0)),
         (wg, (None, n, d), lambda i, j, k: (j, 0, 0)),
         (wu, (None, n, d), lambda i, j, k: (j, 0, 0))],
        [(0, 1, 0, NT), (0, 2, 1, NT)], [(tm, n)] * 2, epi_up, [hid] * 3, comm=comm)


def _ffn_down(name, x, act, wd, comm=None):
    t, d = x.shape
    n = wd.shape[1]
    tm2 = _tile(t, 1024, 8)
    tn = _tile(d, 1024, 128)

    def epi_down(accs, refs, rows):
        return [refs[2][rows] + 0.5 * accs[0]]

    (x_out,), sent = _mmc(
        name + "_down", (t // tm2, d // tn, N_DEV),
        [(act, (None, tm2, n), lambda i, j, k: (k, i, 0)),
         (wd, (None, n, tn), lambda i, j, k: (k, 0, j)),
         (x, (tm2, tn), lambda i, j, k: (i, j))],
        [(0, 1, 0, NN)], [(tm2, tn)], epi_down,
        [((t, d), F32, (tm2, tn), lambda i, j, k: (i, j), False)], comm=comm)
    return x_out, sent


def _ffn_bwd(name, dxo, dxo16, x, h, gate, up, act, wg, wu, wd, norm_w, comms=None):
    t, d = x.shape
    n = wg.shape[1]
    tm = _tile(t, 1024, 16)
    sent = {}
    avail = dict(sent=sent)

    def epi_act(accs, refs, rows):
        da = 0.5 * accs[0]
        g = refs[2][rows].astype(F32)
        u = refs[3][rows].astype(F32)
        s = _sig(g)
        return [da * u * (s * (1.0 + g * (1.0 - s))), da * g * s]

    hid_in = lambda a: (a, (None, tm, n), lambda i, j, k: (j, i, 0))
    hid_out = ((N_DEV, t, n), BF16, (None, tm, n), lambda i, j, k: (j, i, 0), False)
    (dgate, dup), sent["dact"] = _mmc(
        name + "_dact", (t // tm, N_DEV, 1),
        [(dxo16, (tm, d), lambda i, j, k: (i, 0)),
         (wd, (None, n, d), lambda i, j, k: (j, 0, 0)),
         hid_in(gate), hid_in(up)],
        [(0, 1, 0, NT)], [(tm, n)], epi_act, [hid_out] * 2, comm=_carried(comms, "dact", avail))

    tk = _tile(t, 1024, 16)
    w_out = ((N_DEV, n, d), BF16, (None, n, d), lambda i, j, k: (i, 0, 0), False)

    def weight_grad(key, acts, stream, scale):
        (dw,), sent[key] = _mmc(
            name + "_" + key, (N_DEV, 1, t // tk),
            [(acts, (None, tk, n), lambda i, j, k: (i, k, 0)), (stream, (tk, d), lambda i, j, k: (k, 0))],
            [(0, 1, 0, TN)], [(n, d)], lambda accs, refs, rows: [scale * accs[0]], [w_out],
            comm=_carried(comms, key, avail))
        avail[key] = dw
        return dw

    dwd = weight_grad("dwd", act, dxo16, 0.5)
    dwg = weight_grad("dwg", dgate, h, 1.0)
    dwu = weight_grad("dwu", dup, h, 1.0)

    tm3 = _tile(t // 2, 512, 16)
    half = t // 2 // tm3

    def epi_dx(accs, refs, rows):
        dx, dw = _rms_bwd(accs[0], refs[4][rows], refs[5][...], refs[6][rows])
        return [dx, dx, dw]

    def dx_half(key, first, filled):
        at = 0 if first else half
        row = lambda i, j, k: (i + at, 0)
        hid = lambda i, j, k: (k, i + at, 0)
        ins = [(dgate, (None, tm3, n), hid), (wg, (None, n, d), lambda i, j, k: (k, 0, 0)),
               (dup, (None, tm3, n), hid), (wu, (None, n, d), lambda i, j, k: (k, 0, 0)),
               (x, (tm3, d), row), (norm_w, (1, d), lambda i, j, k: (0, 0)), (dxo, (tm3, d), row)]
        ins += [(a, None, None) for a in filled]
        res, sent[key] = _mmc(
            name + "_" + key, (half, 1, N_DEV), ins,
            [(0, 1, 0, NN), (2, 3, 0, NN)], [(tm3, d)], epi_dx,
            [((t, d), F32, (tm3, d), row, False), ((t, d), BF16, (tm3, d), row, False),
             ((1, d), F32, (1, d), lambda i, j, k: (0, 0), True)],
            comm=_carried(comms, key, avail), aliases={7 + k: k for k in range(len(filled))})
        return res

    dx_a, dx16_a, dnorm_a = dx_half("dxa", True, [])
    dx, dx16, dnorm_b = dx_half("dxb", False, [dx_a, dx16_a])
    return dx, dx16, dwg, dwu, dwd, dnorm_a + dnorm_b, sent


def _rope(tv, cos, s1, s2):
    return tv * cos + pltpu.roll(tv, LANES - 8, 1) * s1 + pltpu.roll(tv, 8, 1) * s2


def _rope_bwd(dr, cos, s1, s2):
    return dr * cos + pltpu.roll(dr * s1, 8, 1) + pltpu.roll(dr * s2, LANES - 8, 1)


def _rope_tables(positions):
    inv_freq = jnp.power(jnp.float32(ROPE_THETA), -jnp.arange(0, ROT_DIM, 2, dtype=F32) / ROT_DIM)
    ang = positions.reshape(-1).astype(F32)[:, None] * inv_freq
    cos, sin = jnp.cos(ang), jnp.sin(ang)
    t = ang.shape[0]
    pad = A_HEAD_DIM - ROT_DIM
    cos_t = jnp.concatenate([cos, cos, jnp.ones((t, pad), F32)], axis=1)
    s1_t = jnp.concatenate([-sin, jnp.zeros((t, pad + 8), F32)], axis=1)
    s2_t = jnp.concatenate([jnp.zeros((t, 8), F32), sin, jnp.zeros((t, pad), F32)], axis=1)
    return tuple(jnp.tile(v, (1, 2)) for v in (cos_t, s1_t, s2_t))


def _att_common(i, k_p, k_c, v_p, v_c, tabs_p, tabs_c):
    kcat = jnp.concatenate([k_p, k_c], axis=0)
    vcat = jnp.concatenate([v_p, v_c], axis=0)
    tabs_k = [jnp.concatenate([a, b], axis=0) for a, b in zip(tabs_p, tabs_c)]
    kvar, vvar = [], []
    for ks in range(2):
        sl = slice(ks * LANES, (ks + 1) * LANES)
        kr = _rope(kcat[:, sl], *tabs_k)
        kvar.append([kr.astype(BF16), pltpu.roll(kr, 64, 1).astype(BF16)])
        vvar.append([vcat[:, sl].astype(BF16), pltpu.roll(vcat[:, sl], 64, 1).astype(BF16)])
    qi = lax.broadcasted_iota(jnp.int32, (ATT_BLOCK, 2 * ATT_BLOCK), 0)
    kj = lax.broadcasted_iota(jnp.int32, (ATT_BLOCK, 2 * ATT_BLOCK), 1)
    dist = qi + ATT_BLOCK - kj
    allowed = (dist >= 0) & (dist < ATT_BLOCK) & ((i > 0) | (kj >= ATT_BLOCK))
    lane = lax.broadcasted_iota(jnp.int32, (1, LANES), 1)
    halves = [lane < 64, lane >= 64]
    return kvar, vvar, allowed, halves


def _att_head_probs(qm, kslab, allowed, sink):
    s = lax.dot_general(qm, kslab, NT, preferred_element_type=F32)
    s = jnp.where(allowed, s, -jnp.inf)
    m = jnp.maximum(jnp.max(s, axis=-1, keepdims=True), sink)
    e = jnp.exp(s - m)
    es = jnp.exp(sink - m)
    den = jnp.sum(e, axis=-1, keepdims=True) + es
    inv = 1.0 / den
    return e * inv, es * inv


def _head_slots(h):
    pair, half = h // 2, h % 2
    kvh = h // 4
    return pair, half, kvh // 2, int(kvh % 2 != half)


def _attention_fwd(proj, tables, sinks, comm=None):
    t = proj.shape[0]
    nb = t // ATT_BLOCK
    b = ATT_BLOCK
    scale = A_HEAD_DIM ** -0.5

    def body(sink_ref, q_ref, kp_ref, kc_ref, vp_ref, vc_ref,
             cp_ref, ap_ref, bp_ref, cc_ref, ac_ref, bc_ref, o_ref):
        i = pl.program_id(0)
        tabs_p = (cp_ref[...], ap_ref[...], bp_ref[...])
        tabs_c = (cc_ref[...], ac_ref[...], bc_ref[...])
        kvar, vvar, allowed, halves = _att_common(
            i, kp_ref[...], kc_ref[...], vp_ref[...], vc_ref[...], tabs_p, tabs_c)
        for pair in range(A_HEADS // 2):
            sl = slice(pair * LANES, (pair + 1) * LANES)
            q2 = _rope(q_ref[:, sl], *tabs_c) * scale
            acc = jnp.zeros((b, LANES), F32)
            for half in range(2):
                h = 2 * pair + half
                _, _, ks, var = _head_slots(h)
                qm = jnp.where(halves[half], q2, 0.0).astype(BF16)
                p, _ = _att_head_probs(qm, kvar[ks][var], allowed, sink_ref[h])
                o = lax.dot_general(p.astype(BF16), vvar[ks][var], NN, preferred_element_type=F32)
                acc = acc + jnp.where(halves[half], o, 0.0)
            o_ref[:, sl] = acc.astype(o_ref.dtype)

    cur = lambda col: (lambda i: (i, col))
    prev = lambda col: (lambda i: (jnp.maximum(i - 1, 0), col))
    tab = [pl.BlockSpec((b, LANES), prev(0))] * 3 + [pl.BlockSpec((b, LANES), cur(0))] * 3
    (out,), sent = _call(
        "attention_fwd", body, (nb,),
        [pl.BlockSpec(memory_space=pltpu.SMEM),
         pl.BlockSpec((b, A_WIDTH), cur(0)),
         pl.BlockSpec((b, A_KV_WIDTH), prev(OFF_KA // A_KV_WIDTH)),
         pl.BlockSpec((b, A_KV_WIDTH), cur(OFF_KA // A_KV_WIDTH)),
         pl.BlockSpec((b, A_KV_WIDTH), prev(OFF_VA // A_KV_WIDTH)),
         pl.BlockSpec((b, A_KV_WIDTH), cur(OFF_VA // A_KV_WIDTH))] + tab,
        [pl.BlockSpec((b, A_WIDTH), cur(0))],
        [jax.ShapeDtypeStruct((t, A_WIDTH), BF16)], [],
        [sinks, proj, proj, proj, proj, proj, *tables, *tables], comm)
    return out, sent


def _attention_bwd(proj, tables, sinks, d_out, comm=None):
    t = proj.shape[0]
    nb = t // ATT_BLOCK
    b = ATT_BLOCK
    scale = A_HEAD_DIM ** -0.5

    def body(sink_ref, q_ref, kp_ref, kc_ref, vp_ref, vc_ref,
             cp_ref, ap_ref, bp_ref, cc_ref, ac_ref, bc_ref, do_ref,
             dq_ref, dk_ref, dv_ref, dsink_ref, dk_carry, dv_carry):
        step = pl.program_id(0)
        i = nb - 1 - step

        @pl.when(step == 0)
        def _():
            dk_carry[...] = jnp.zeros(dk_carry.shape, F32)
            dv_carry[...] = jnp.zeros(dv_carry.shape, F32)
            dsink_ref[...] = jnp.zeros(dsink_ref.shape, F32)

        tabs_p = (cp_ref[...], ap_ref[...], bp_ref[...])
        tabs_c = (cc_ref[...], ac_ref[...], bc_ref[...])
        kvar, vvar, allowed, halves = _att_common(
            i, kp_ref[...], kc_ref[...], vp_ref[...], vc_ref[...], tabs_p, tabs_c)
        lane = lax.broadcasted_iota(jnp.int32, (1, LANES), 1)
        dk_slab = [jnp.zeros((2 * b, LANES), F32) for _ in range(2)]
        dv_slab = [jnp.zeros((2 * b, LANES), F32) for _ in range(2)]
        dsink = jnp.zeros((1, LANES), F32)
        for pair in range(A_HEADS // 2):
            sl = slice(pair * LANES, (pair + 1) * LANES)
            q2 = _rope(q_ref[:, sl], *tabs_c) * scale
            do2 = do_ref[:, sl].astype(F32)
            dq2 = jnp.zeros((b, LANES), F32)
            for half in range(2):
                h = 2 * pair + half
                _, _, ks, var = _head_slots(h)
                qm = jnp.where(halves[half], q2, 0.0).astype(BF16)
                dom = jnp.where(halves[half], do2, 0.0).astype(BF16)
                p, p_sink = _att_head_probs(qm, kvar[ks][var], allowed, sink_ref[h])
                dp = lax.dot_general(dom, vvar[ks][var], NT, preferred_element_type=F32)
                dsum = jnp.sum(p * dp, axis=-1, keepdims=True)
                ds = (p * (dp - dsum)).astype(BF16)
                dsink = dsink + jnp.where(lane == h, -jnp.sum(p_sink * dsum), 0.0)
                dqh = lax.dot_general(ds, kvar[ks][var], NN, preferred_element_type=F32)
                dq2 = dq2 + jnp.where(halves[half], dqh, 0.0)
                dkx = lax.dot_general(ds, qm, TN, preferred_element_type=F32)
                dvx = lax.dot_general(p.astype(BF16), dom, TN, preferred_element_type=F32)
                if var:
                    dkx = pltpu.roll(dkx, 64, 1)
                    dvx = pltpu.roll(dvx, 64, 1)
                dk_slab[ks] = dk_slab[ks] + dkx
                dv_slab[ks] = dv_slab[ks] + dvx
            dq_ref[:, sl] = _rope_bwd(dq2 * scale, *tabs_c).astype(dq_ref.dtype)
        for ks in range(2):
            sl = slice(ks * LANES, (ks + 1) * LANES)
            dk_cur = dk_slab[ks][b:] + dk_carry[:, sl]
            dk_ref[:, sl] = _rope_bwd(dk_cur, *tabs_c).astype(dk_ref.dtype)
            dv_ref[:, sl] = (dv_slab[ks][b:] + dv_carry[:, sl]).astype(dv_ref.dtype)
            dk_carry[:, sl] = dk_slab[ks][:b]
            dv_carry[:, sl] = dv_slab[ks][:b]
        dsink_ref[...] += dsink

    cur = lambda col: (lambda s: (nb - 1 - s, col))
    prev = lambda col: (lambda s: (jnp.maximum(nb - 2 - s, 0), col))
    tab = [pl.BlockSpec((b, LANES), prev(0))] * 3 + [pl.BlockSpec((b, LANES), cur(0))] * 3
    return _call(
        "attention_bwd", body, (nb,),
        [pl.BlockSpec(memory_space=pltpu.SMEM),
         pl.BlockSpec((b, A_WIDTH), cur(0)),
         pl.BlockSpec((b, A_KV_WIDTH), prev(OFF_KA // A_KV_WIDTH)),
         pl.BlockSpec((b, A_KV_WIDTH), cur(OFF_KA // A_KV_WIDTH)),
         pl.BlockSpec((b, A_KV_WIDTH), prev(OFF_VA // A_KV_WIDTH)),
         pl.BlockSpec((b, A_KV_WIDTH), cur(OFF_VA // A_KV_WIDTH))] + tab
        + [pl.BlockSpec((b, A_WIDTH), cur(0))],
        [pl.BlockSpec((b, A_WIDTH), cur(0)),
         pl.BlockSpec((b, A_KV_WIDTH), cur(0)),
         pl.BlockSpec((b, A_KV_WIDTH), cur(0)),
         pl.BlockSpec((1, LANES), lambda s: (0, 0))],
        [jax.ShapeDtypeStruct((t, A_WIDTH), BF16),
         jax.ShapeDtypeStruct((t, A_KV_WIDTH), BF16),
         jax.ShapeDtypeStruct((t, A_KV_WIDTH), BF16),
         jax.ShapeDtypeStruct((1, LANES), F32)],
        [pltpu.VMEM((b, A_KV_WIDTH), F32), pltpu.VMEM((b, A_KV_WIDTH), F32)],
        [sinks, proj, proj, proj, proj, proj, *tables, *tables, d_out], comm)


HGRN_HEADS_PER_STEP = 4
HGRN_W = HGRN_HEADS_PER_STEP * B_HEAD
HGRN_ROWS = 8 * CHUNK


def _hgrn_gates(qp, fp, hlb):
    lb = _sig(hlb[0:1] - hlb[1:2])
    sg = _sig(fp)
    big_f = lb + (1.0 - lb) * sg
    kk = (1.0 - lb) * (1.0 - sg)
    sq = _sig(qp)
    return lb, sg, big_f, kk, sq


def _tri(lower):
    r = lax.broadcasted_iota(jnp.int32, (CHUNK, CHUNK), 0)
    c = lax.broadcasted_iota(jnp.int32, (CHUNK, CHUNK), 1)
    return (r >= c) if lower else (r <= c)


def _hgrn_factors(q_ref, k_ref, b_ref, sl, r0):
    out = []
    whole = slice(r0, r0 + CHUNK)
    bh = b_ref[whole, sl]
    for i in range(CHUNK // SUB):
        rows = slice(r0 + i * SUB, r0 + (i + 1) * SUB)
        b0 = b_ref[r0 + i * SUB - 1:r0 + i * SUB, sl] if i else jnp.zeros((1, B_HEAD), F32)
        eq = jnp.exp(b_ref[rows, sl] - b0)
        ek = jnp.exp(jnp.minimum(b0 - bh, MAX_DECAY_EXP))
        out.append((q_ref[rows, sl] * eq, k_ref[whole, sl] * ek, eq, ek))
    return out


def _hgrn_scores(factors):
    rows = [lax.dot_general(qd.astype(BF16), kx.astype(BF16), NT, preferred_element_type=F32)
            for qd, kx, _, _ in factors]
    return jnp.where(_tri(True), jnp.concatenate(rows, axis=0), 0.0)


def _hgrn_specs(steps, reverse):
    w = HGRN_W
    block = (lambda c: steps - 1 - c) if reverse else (lambda c: c)
    col = lambda off: pl.BlockSpec((HGRN_ROWS, w), lambda hh, c: (block(c), off // w + hh))
    return block, col


def _chunk_cumsum(log_f, r0, lower):
    tri = _tri(lower).astype(F32)
    return lax.dot_general(tri, log_f[r0:r0 + CHUNK], NN, precision=lax.Precision.HIGHEST,
                           preferred_element_type=F32)


def _hgrn_fwd(proj, hlb, wn, comm=None):
    t = proj.shape[0]
    nc = t // CHUNK
    steps = t // HGRN_ROWS
    w = HGRN_W
    block, col = _hgrn_specs(steps, False)

    def body(q_ref, f_ref, v_ref, og_ref, hlb_ref, wn_ref, ob_ref, oraw_ref, st_ref,
             state, qs, ks, bs):
        c = pl.program_id(1)

        @pl.when(c == 0)
        def _():
            state[...] = jnp.zeros(state.shape, F32)

        qp = q_ref[...]
        lb, sg, big_f, kk, sq = _hgrn_gates(qp, f_ref[...], hlb_ref[...])
        qs[...] = qp * sq
        ks[...] = kk
        log_f = jnp.log(big_f)
        for sub in range(HGRN_ROWS // CHUNK):
            r0 = sub * CHUNK
            rows = slice(r0, r0 + CHUNK)
            bs[rows] = _chunk_cumsum(log_f, r0, True)
            for h in range(HGRN_HEADS_PER_STEP):
                sl = slice(h * B_HEAD, (h + 1) * B_HEAD)
                bh = bs[rows, sl]
                qh = qs[rows, sl]
                kh = ks[rows, sl]
                vh = v_ref[rows, sl].astype(BF16)
                s0 = state[h]
                st_ref[sub, h] = s0.astype(st_ref.dtype)
                a = _hgrn_scores(_hgrn_factors(qs, ks, bs, sl, r0))
                o = lax.dot_general(a.astype(BF16), vh, NN, preferred_element_type=F32)
                o = o + lax.dot_general((qh * jnp.exp(bh)).astype(BF16), s0.astype(BF16), NT,
                                        preferred_element_type=F32)
                bl = bs[r0 + CHUNK - 1:r0 + CHUNK, sl]
                khat = (kh * jnp.exp(bl - bh)).astype(BF16)
                state[h] = s0 * jnp.exp(bl) + lax.dot_general(vh, khat, TN, preferred_element_type=F32)
                oraw_ref[rows, sl] = o
                r = lax.rsqrt(jnp.mean(o * o, axis=-1, keepdims=True) + EPS)
                og = og_ref[rows, sl]
                ob_ref[rows, sl] = (o * r * wn_ref[:, sl] * (og * _sig(og))).astype(ob_ref.dtype)

    vec = lambda rows: pl.BlockSpec((rows, w), lambda hh, c: (0, hh))
    tile = pl.BlockSpec((HGRN_ROWS, w), lambda hh, c: (c, hh))
    return _call(
        "hgrn_fwd", body, (B_WIDTH // w, steps),
        [col(OFF_QB), col(OFF_FB), col(OFF_IB), col(OFF_OG), vec(2), vec(1)],
        [tile, tile,
         pl.BlockSpec((HGRN_ROWS // CHUNK, HGRN_HEADS_PER_STEP, B_HEAD, B_HEAD), lambda hh, c: (c, hh, 0, 0))],
        [jax.ShapeDtypeStruct((t, B_WIDTH), BF16),
         jax.ShapeDtypeStruct((t, B_WIDTH), F32),
         jax.ShapeDtypeStruct((nc, B_WIDTH // B_HEAD, B_HEAD, B_HEAD), BF16)],
        [pltpu.VMEM((HGRN_HEADS_PER_STEP, B_HEAD, B_HEAD), F32)] + [pltpu.VMEM((HGRN_ROWS, w), F32)] * 3,
        [proj, proj, proj, proj, hlb, wn], comm)


def _hgrn_bwd(proj, hlb, wn, o_raw, states, d_out, comm=None):
    t = proj.shape[0]
    steps = t // HGRN_ROWS
    w = HGRN_W
    block, col = _hgrn_specs(steps, True)

    def body(q_ref, f_ref, v_ref, og_ref, hlb_ref, wn_ref, oraw_ref, st_ref, dout_ref,
             dq_ref, df_ref, dv_ref, dog_ref, dlb_ref, dwn_ref,
             dstate, qs, ks, bs, dbs, dks, dqs, dgs):
        c = pl.program_id(1)

        @pl.when(c == 0)
        def _():
            dstate[...] = jnp.zeros(dstate.shape, F32)
            dlb_ref[...] = jnp.zeros(dlb_ref.shape, F32)
            dwn_ref[...] = jnp.zeros(dwn_ref.shape, F32)

        qp = q_ref[...]
        lb, sg, big_f, kk, sq = _hgrn_gates(qp, f_ref[...], hlb_ref[...])
        qs[...] = qp * sq
        ks[...] = kk
        log_f = jnp.log(big_f)
        last_row = lax.broadcasted_iota(jnp.int32, (CHUNK, 1), 0) == CHUNK - 1
        for sub in reversed(range(HGRN_ROWS // CHUNK)):
            r0 = sub * CHUNK
            rows = slice(r0, r0 + CHUNK)
            bs[rows] = _chunk_cumsum(log_f, r0, True)
            for h in range(HGRN_HEADS_PER_STEP):
                sl = slice(h * B_HEAD, (h + 1) * B_HEAD)
                bh = bs[rows, sl]
                qh = qs[rows, sl]
                kh = ks[rows, sl]
                vh = v_ref[rows, sl].astype(BF16)
                s0 = st_ref[sub, h]
                ds1 = dstate[h]
                o = oraw_ref[rows, sl]
                og = og_ref[rows, sl]
                wnh = wn_ref[:, sl]
                sog = _sig(og)
                r = lax.rsqrt(jnp.mean(o * o, axis=-1, keepdims=True) + EPS)
                dob = dout_ref[rows, sl].astype(F32)
                don = dob * (og * sog)
                dog_ref[rows, sl] = (dob * (o * r * wnh) * (sog * (1.0 + og * (1.0 - sog)))).astype(dog_ref.dtype)
                dwn_ref[:, sl] += jnp.sum(don * o * r, axis=0, keepdims=True)
                gp = don * wnh
                do = r * (gp - o * (r * r) * jnp.mean(gp * o, axis=-1, keepdims=True))
                do16 = do.astype(BF16)
                factors = _hgrn_factors(qs, ks, bs, sl, r0)
                a = _hgrn_scores(factors)
                da = jnp.where(_tri(True), lax.dot_general(do16, vh, NT, preferred_element_type=F32), 0.0)
                dv = lax.dot_general(a.astype(BF16), do16, TN, preferred_element_type=F32)
                dq_rows = []
                dk = jnp.zeros((CHUNK, B_HEAD), F32)
                for i, (qd, kx, eq, ek) in enumerate(factors):
                    da_i = da[i * SUB:(i + 1) * SUB]
                    dq_rows.append(lax.dot_general(da_i, kx, NN, precision=lax.Precision.HIGHEST,
                                                   preferred_element_type=F32) * eq)
                    dk = dk + lax.dot_general(da_i, qd, TN, precision=lax.Precision.HIGHEST,
                                              preferred_element_type=F32) * ek
                eb = jnp.exp(bh)
                bl = bs[r0 + CHUNK - 1:r0 + CHUNK, sl]
                ebl = jnp.exp(bl)
                ekl = jnp.exp(bl - bh)
                qe = (qh * eb).astype(BF16)
                khat = (kh * ekl).astype(BF16)
                ds1_16 = ds1.astype(BF16)
                dq = jnp.concatenate(dq_rows, axis=0) + eb * lax.dot_general(
                    do16, s0, NN, preferred_element_type=F32)
                dv = dv + lax.dot_general(khat, ds1_16, NT, preferred_element_type=F32)
                dk_state = ekl * lax.dot_general(vh, ds1_16, NN, preferred_element_type=F32)
                dk = dk + dk_state
                db_last = (jnp.sum(kh * dk_state, axis=0, keepdims=True)
                           + ebl * jnp.sum(s0.astype(F32) * ds1, axis=0, keepdims=True))
                dbs[rows, sl] = qh * dq - kh * dk + jnp.where(last_row, db_last, 0.0)
                dks[rows, sl] = dk
                dqs[rows, sl] = dq
                dstate[h] = ds1 * ebl + lax.dot_general(do16, qe, TN, preferred_element_type=F32)
                dv_ref[rows, sl] = dv.astype(dv_ref.dtype)
            dgs[rows] = _chunk_cumsum(dbs, r0, False)
        dq_ref[...] = (dqs[...] * (sq * (1.0 + qp * (1.0 - sq)))).astype(dq_ref.dtype)
        df_minus_dk = dgs[...] / big_f - dks[...]
        df_ref[...] = ((1.0 - lb) * df_minus_dk * sg * (1.0 - sg)).astype(df_ref.dtype)
        dlb_ref[...] += jnp.sum((1.0 - sg) * df_minus_dk, axis=0, keepdims=True) * (lb * (1.0 - lb))

    vec = lambda rows: pl.BlockSpec((rows, w), lambda hh, c: (0, hh))
    blk = pl.BlockSpec((HGRN_ROWS, w), lambda hh, c: (block(c), hh))
    act = jax.ShapeDtypeStruct((t, B_WIDTH), BF16)
    row = jax.ShapeDtypeStruct((1, B_WIDTH), F32)
    return _call(
        "hgrn_bwd", body, (B_WIDTH // w, steps),
        [col(OFF_QB), col(OFF_FB), col(OFF_IB), col(OFF_OG), vec(2), vec(1), blk,
         pl.BlockSpec((HGRN_ROWS // CHUNK, HGRN_HEADS_PER_STEP, B_HEAD, B_HEAD),
                      lambda hh, c: (block(c), hh, 0, 0)),
         blk],
        [blk, blk, blk, blk, vec(1), vec(1)],
        [act, act, act, act, row, row],
        [pltpu.VMEM((HGRN_HEADS_PER_STEP, B_HEAD, B_HEAD), F32)] + [pltpu.VMEM((HGRN_ROWS, w), F32)] * 7,
        [proj, proj, proj, proj, hlb, wn, o_raw, states, d_out], comm)


def _mixer_fwd(x1, mix_norm, w_in, sinks, hlb, wn, weights, tables, comms):
    t, d = x1.shape
    n_in = w_in[0].shape[0]
    nb = d // N_DEV
    sent = {}
    h2 = _rmsnorm("mix_norm", x1, mix_norm)
    tm = _tile(t, 1024, 16)
    tn = _tile(n_in, 512, 128)
    (proj,), sent["proj"] = _mmc(
        "mix_proj", (t // tm, n_in // tn, 1),
        [(h2, (tm, d // 2), lambda i, j, k: (i, 0)), (w_in[0], (tn, d // 2), lambda i, j, k: (j, 0)),
         (h2, (tm, d // 2), lambda i, j, k: (i, 1)), (w_in[1], (tn, d // 2), lambda i, j, k: (j, 0))],
        [(0, 1, 0, NT), (2, 3, 0, NT)], [(tm, tn)], lambda accs, refs, rows: accs,
        [((t, n_in), F32, (tm, tn), lambda i, j, k: (i, j), False)], comm=comms.get("proj"))
    out_a, sent["attention"] = _attention_fwd(proj, tables, sinks, comms.get("attention"))
    (out_b, o_raw, states), sent["hgrn"] = _hgrn_fwd(proj, hlb, wn, comms.get("hgrn"))
    wua, wub, w_out = weights(sent)

    def epi_merge(accs, refs, rows):
        ua, ub = accs
        return [_sig(refs[4][rows]) * ua + _sig(refs[5][rows]) * ub, ua, ub]

    wide = ((t, d), BF16, (tm, nb), lambda i, j, k: (i, j), False)
    (merged, ua, ub), sent["merge"] = _mmc(
        "mix_merge", (t // tm, N_DEV, 1),
        [(out_a, (tm, A_WIDTH), lambda i, j, k: (i, 0)),
         (wua, (None, A_WIDTH, nb), lambda i, j, k: (j, 0, 0)),
         (out_b, (tm, B_WIDTH), lambda i, j, k: (i, 0)),
         (wub, (None, B_WIDTH, nb), lambda i, j, k: (j, 0, 0)),
         (proj, (tm, nb), lambda i, j, k: (i, OFF_GA // nb + j)),
         (proj, (tm, nb), lambda i, j, k: (i, (OFF_GA + d) // nb + j))],
        [(0, 1, 0, NN), (2, 3, 1, NN)], [(tm, nb)] * 2, epi_merge, [wide] * 3, comm=comms.get("merge"))

    tn2 = _tile(d, 1024, 128)
    (x2,) = _mm(
        "mix_out", (t // tm, d // tn2, 1),
        [(merged, (tm, d), lambda i, j, k: (i, 0)), (w_out, (d, tn2), lambda i, j, k: (0, j)),
         (x1, (tm, tn2), lambda i, j, k: (i, j))],
        [(0, 1, 0, NN)], [(tm, tn2)], lambda accs, refs, rows: [refs[2][rows] + accs[0]],
        [((t, d), F32, (tm, tn2), lambda i, j, k: (i, j), False)])
    saved = dict(h2=h2, proj=proj, out_a=out_a, out_b=out_b, o_raw=o_raw, states=states,
                 merged=merged, ua=ua, ub=ub)
    return x2, saved, sent


def _mixer_bwd(dx2, dx2_16, x1, mix_norm, w_in, sinks, hlb, wn, wua, wub, w_out, tables, sv, comms):
    t, d = x1.shape
    sent = {}
    avail = dict(sent=sent)
    n_in = w_in[0].shape[0]
    nb = d // N_DEV
    proj = sv["proj"]
    tm = _tile(t, 1024, 16)
    tn = _tile(d, 512, 128)

    def epi_gate(accs, refs, rows):
        dm = accs[0]
        sa = _sig(refs[2][rows])
        sb = _sig(refs[3][rows])
        ua = refs[4][rows].astype(F32)
        ub = refs[5][rows].astype(F32)
        return [dm * sa, dm * sb, dm * ua * sa * (1.0 - sa), dm * ub * sb * (1.0 - sb)]

    wide = ((t, d), BF16, (tm, tn), lambda i, j, k: (i, j), False)
    dua, dub, dga, dgb = _mm(
        "mix_dmerge", (t // tm, d // tn, 1),
        [(dx2_16, (tm, d), lambda i, j, k: (i, 0)),
         (w_out, (tn, d), lambda i, j, k: (j, 0)),
         (proj, (tm, tn), lambda i, j, k: (i, OFF_GA // tn + j)),
         (proj, (tm, tn), lambda i, j, k: (i, (OFF_GA + d) // tn + j)),
         (sv["ua"], (tm, tn), lambda i, j, k: (i, j)),
         (sv["ub"], (tm, tn), lambda i, j, k: (i, j))],
        [(0, 1, 0, NT)], [(tm, tn)], epi_gate, [wide] * 4)

    tk = _tile(t, 1024, 16)
    two = _tile(d, 1024, 128)
    (dw_out,) = _mm(
        "mix_dwout", (d // two, 1, t // tk),
        [(sv["merged"], (tk, two), lambda i, j, k: (k, i)), (dx2_16, (tk, d), lambda i, j, k: (k, 0))],
        [(0, 1, 0, TN)], [(two, d)], lambda accs, refs, rows: accs,
        [((d, d), BF16, (two, d), lambda i, j, k: (i, 0), False)])

    w_spec = lambda rows: ((N_DEV, rows, nb), BF16, (None, rows, nb), lambda i, j, k: (j, 0, 0), False)
    tk2 = _tile(t, 2048, 16)
    dwua, dwub = _mm(
        "mix_dwup", (1, N_DEV, t // tk2),
        [(sv["out_a"], (tk2, A_WIDTH), lambda i, j, k: (k, 0)),
         (dua, (tk2, nb), lambda i, j, k: (k, j)),
         (sv["out_b"], (tk2, B_WIDTH), lambda i, j, k: (k, 0)),
         (dub, (tk2, nb), lambda i, j, k: (k, j))],
        [(0, 1, 0, TN), (2, 3, 1, TN)], [(A_WIDTH, nb), (B_WIDTH, nb)], lambda accs, refs, rows: accs,
        [w_spec(A_WIDTH), w_spec(B_WIDTH)])

    d_out_a, d_out_b = _mm(
        "mix_dup", (t // tm, 1, N_DEV),
        [(dua, (tm, nb), lambda i, j, k: (i, k)),
         (wua, (None, A_WIDTH, nb), lambda i, j, k: (k, 0, 0)),
         (dub, (tm, nb), lambda i, j, k: (i, k)),
         (wub, (None, B_WIDTH, nb), lambda i, j, k: (k, 0, 0))],
        [(0, 1, 0, NT), (2, 3, 1, NT)], [(tm, A_WIDTH), (tm, B_WIDTH)], lambda accs, refs, rows: accs,
        [((t, A_WIDTH), BF16, (tm, A_WIDTH), lambda i, j, k: (i, 0), False),
         ((t, B_WIDTH), BF16, (tm, B_WIDTH), lambda i, j, k: (i, 0), False)])

    avail.update(w_out=dw_out.reshape(N_DEV, d // N_DEV, d), w_up_a=dwua, w_up_b=dwub)
    (dq_a, dk_a, dv_a, dsinks), sent["attention"] = _attention_bwd(
        proj, tables, sinks, d_out_a, _carried(comms, "attention", avail))
    (dq_b, df_b, di_b, dog_b, dlb, dwn), sent["hgrn"] = _hgrn_bwd(
        proj, hlb, wn, sv["o_raw"], sv["states"], d_out_b, _carried(comms, "hgrn", avail))
    dproj = jnp.concatenate([dq_a, dk_a, dv_a, dq_b, df_b, di_b, dog_b, dga, dgb], axis=1)

    tdm = _tile(d, 1024, 128)
    twn = _tile(n_in, 2432, 128)
    tkw = _tile(t, 512, 16)
    (dw_in,) = _mm(
        "mix_dwin", (n_in // twn, d // tdm, t // tkw),
        [(dproj, (tkw, twn), lambda i, j, k: (k, i)), (sv["h2"], (tkw, tdm), lambda i, j, k: (k, j))],
        [(0, 1, 0, TN)], [(twn, tdm)], lambda accs, refs, rows: accs,
        [((n_in, d), BF16, (twn, tdm), lambda i, j, k: (i, j), False)])

    avail.update(w_in=dw_in.reshape(N_DEV, n_in // N_DEV, d))
    tm3 = _tile(t, 512, 16)
    tpn = _tile(n_in, 512, 128)

    def epi_dx(accs, refs, rows):
        dh = jnp.concatenate(accs, axis=1)
        dx, dw = _rms_bwd(dh, refs[3][rows], refs[4][...], refs[5][rows])
        return [dx, dx, dw]

    (dx1, dx1_16, dmix), sent["dx"] = _mmc(
        "mix_dx", (t // tm3, 1, n_in // tpn),
        [(dproj, (tm3, tpn), lambda i, j, k: (i, k)),
         (w_in[0], (tpn, d // 2), lambda i, j, k: (k, 0)), (w_in[1], (tpn, d // 2), lambda i, j, k: (k, 0)),
         (x1, (tm3, d), lambda i, j, k: (i, 0)), (mix_norm, (1, d), lambda i, j, k: (0, 0)),
         (dx2, (tm3, d), lambda i, j, k: (i, 0))],
        [(0, 1, 0, NN), (0, 2, 1, NN)], [(tm3, d // 2)] * 2, epi_dx,
        [((t, d), F32, (tm3, d), lambda i, j, k: (i, 0), False),
         ((t, d), BF16, (tm3, d), lambda i, j, k: (i, 0), False),
         ((1, d), F32, (1, d), lambda i, j, k: (0, 0), True)],
        comm=_carried(comms, "dx", avail))
    small = dict(mix_norm=dmix, attn_sinks=dsinks, hgrn_lb=dlb, hgrn_norm=dwn)
    return dx1, dx1_16, small, avail


def _ple_fwd(x3, ple_norm, p16, wpg, wpp):
    t, d = x3.shape
    nb = d // N_DEV
    h4 = _rmsnorm("ple_norm", x3, ple_norm)
    tm = _tile(t, 1024, 16)

    def epi(accs, refs, rows):
        zg, pp = accs
        return [refs[4][rows] + _sig(zg) * pp, zg, pp]

    blk = lambda dt: ((t, d), dt, (tm, nb), lambda i, j, k: (i, j), False)
    x4, zg, pp = _mm(
        "ple_fwd", (t // tm, N_DEV, 1),
        [(h4, (tm, d), lambda i, j, k: (i, 0)), (wpg, (d, nb), lambda i, j, k: (0, j)),
         (p16, (tm, PLE_DIM), lambda i, j, k: (i, 0)),
         (wpp, (None, PLE_DIM, nb), lambda i, j, k: (j, 0, 0)),
         (x3, (tm, nb), lambda i, j, k: (i, j))],
        [(0, 1, 0, NN), (2, 3, 1, NN)], [(tm, nb)] * 2, epi, [blk(F32), blk(BF16), blk(BF16)])
    return x4, dict(h4=h4, zg=zg, pp=pp)


def _ple_bwd(dx4, x3, ple_norm, p16, wpg, sv):
    t, d = x3.shape
    nb = d // N_DEV

    def gate_grads(dx, zg, pp):
        s = _sig(zg.astype(F32))
        return dx * pp.astype(F32) * s * (1.0 - s), dx * s

    dzg, dpp = _rows("ple_dgate", gate_grads, [(dx4, True), (sv["zg"], True), (sv["pp"], True)],
                     [((t, d), BF16), ((t, d), BF16)], _tile(t, 512, 16))
    (dwpp,) = _mm(
        "ple_dwproj", (1, N_DEV, 1),
        [(p16, (t, PLE_DIM), lambda i, j, k: (0, 0)), (dpp, (t, nb), lambda i, j, k: (0, j))],
        [(0, 1, 0, TN)], [(PLE_DIM, nb)], lambda accs, refs, rows: accs,
        [((N_DEV, PLE_DIM, nb), BF16, (None, PLE_DIM, nb), lambda i, j, k: (j, 0, 0), False)])
    tk = _tile(t, 1024, 16)
    tn = _tile(d, 1024, 128)
    (dwpg,) = _mm(
        "ple_dwgate", (d // tn, 1, t // tk),
        [(sv["h4"], (tk, tn), lambda i, j, k: (k, i)), (dzg, (tk, d), lambda i, j, k: (k, 0))],
        [(0, 1, 0, TN)], [(tn, d)], lambda accs, refs, rows: accs,
        [((d, d), BF16, (tn, d), lambda i, j, k: (i, 0), False)])
    tm3 = _tile(t, 256, 16)

    def epi_dx(accs, refs, rows):
        dx, dw = _rms_bwd(accs[0], refs[2][rows], refs[3][...], refs[4][rows])
        return [dx, dx, dw]

    dx3, dx3_16, dnorm = _mm(
        "ple_dx", (t // tm3, 1, 1),
        [(dzg, (tm3, d), lambda i, j, k: (i, 0)), (wpg, (d, d), lambda i, j, k: (0, 0)),
         (x3, (tm3, d), lambda i, j, k: (i, 0)), (ple_norm, (1, d), lambda i, j, k: (0, 0)),
         (dx4, (tm3, d), lambda i, j, k: (i, 0))],
        [(0, 1, 0, NT)], [(tm3, d)], epi_dx,
        [((t, d), F32, (tm3, d), lambda i, j, k: (i, 0), False),
         ((t, d), BF16, (tm3, d), lambda i, j, k: (i, 0), False),
         ((1, d), F32, (1, d), lambda i, j, k: (0, 0), True)])
    return dx3, dx3_16, dwpp, dwpg, dnorm


def _loss_head(x4, final_norm, target):
    t, d = x4.shape
    tm = _tile(t, 256, 8)

    def body(x_ref, w_ref, tgt_ref, dx_ref, dw_ref, loss_ref):
        i = pl.program_id(0)
        xv = x_ref[...]
        wv = w_ref[...]
        r = lax.rsqrt(jnp.mean(xv * xv, axis=-1, keepdims=True) + EPS)
        err = xv * r * wv - tgt_ref[...]
        part = 0.5 * jnp.sum(jnp.mean(err * err, axis=-1, keepdims=True), axis=0, keepdims=True)
        dx, dw = _rms_bwd(err * (1.0 / d), xv, wv, jnp.zeros_like(xv))
        dx_ref[...] = dx

        @pl.when(i == 0)
        def _():
            dw_ref[...] = dw
            loss_ref[...] = jnp.broadcast_to(part, loss_ref.shape)

        @pl.when(i > 0)
        def _():
            dw_ref[...] += dw
            loss_ref[...] += jnp.broadcast_to(part, loss_ref.shape)

    return pl.pallas_call(
        body,
        name="loss_head",
        grid=(t // tm,),
        in_specs=[pl.BlockSpec((tm, d), lambda i: (i, 0)), pl.BlockSpec((1, d), lambda i: (0, 0)),
                  pl.BlockSpec((tm, d), lambda i: (i, 0))],
        out_specs=[pl.BlockSpec((tm, d), lambda i: (i, 0)), pl.BlockSpec((1, d), lambda i: (0, 0)),
                   pl.BlockSpec((1, LANES), lambda i: (0, 0))],
        out_shape=[jax.ShapeDtypeStruct((t, d), F32), jax.ShapeDtypeStruct((1, d), F32),
                   jax.ShapeDtypeStruct((1, LANES), F32)],
        compiler_params=_params(("arbitrary",)),
    )(x4, final_norm, target)


RELATIONS = ((0, 0), (1, 0), (0, 1), (1, 1))


def _place():
    return lax.axis_index("x"), lax.axis_index("y"), lax.axis_index("c")


def _flip(v, bit):
    return 1 - v if bit else v


def _gather_program(shards, mid_at):
    n = len(shards)
    slots = 8

    def copies(ins, outs, send_sems, recv_sems, local_sems, base):
        x, y, c = _place()
        sibling, x_chip, y_chip = (x, y, 1 - c), (1 - x, y, c), (x, 1 - y, c)

        def block(a, r, core, half=None):
            ref = outs[a].at[4 * _flip(x, RELATIONS[r][0]) + 2 * _flip(y, RELATIONS[r][1]) + core]
            rows = shards[a].shape[0] // 2
            return ref if half is None else ref.at[pl.ds(half * rows, rows)]

        def copy(a, slot, dst, to, src=None):
            return pltpu.make_async_remote_copy(
                src_ref=dst if src is None else src, dst_ref=dst,
                send_sem=send_sems.at[base[0] + slots * a + slot],
                recv_sem=recv_sems.at[base[1] + slots * a + slot], device_id=to, device_id_type=MESH)

        own = [pltpu.make_async_copy(ins[a], block(a, 0, c), local_sems.at[base[2] + a]) for a in range(n)]
        sent_first = [cp for a in range(n) for cp in (
            copy(a, 0, block(a, 0, c), sibling, src=ins[a]),
            copy(a, 1, block(a, 0, c), x_chip, src=ins[a]),
            copy(a, 2, block(a, 0, c), y_chip, src=ins[a]))]
        landed_first = [cp for a in range(n) for cp in (
            copy(a, 1, block(a, 1, c), x_chip), copy(a, 2, block(a, 2, c), y_chip))]
        sent_mid = [cp for a in range(n) for cp in (
            copy(a, 3, block(a, 1, c, 0), y_chip), copy(a, 4, block(a, 2, c, 1), x_chip),
            copy(a, 5, block(a, 1, c), sibling), copy(a, 6, block(a, 2, c), sibling))]
        landed_mid = [cp for a in range(n) for cp in (
            copy(a, 3, block(a, 3, c, 0), y_chip), copy(a, 4, block(a, 3, c, 1), x_chip))]
        sent_last = [copy(a, 7, block(a, 3, c), sibling) for a in range(n)]
        landed_sibling = [cp for a in range(n) for cp in (
            copy(a, 0, block(a, 0, 1 - c), sibling), copy(a, 5, block(a, 1, 1 - c), sibling),
            copy(a, 6, block(a, 2, 1 - c), sibling), copy(a, 7, block(a, 3, 1 - c), sibling))]
        return own, sent_first, landed_first, sent_mid, landed_mid, sent_last, landed_sibling

    def pre(*args):
        own, sent_first = copies(*args)[:2]
        for cp in own + sent_first:
            cp.start()

    def mid(*args):
        _, _, landed_first, sent_mid, _, _, _ = copies(*args)
        for cp in landed_first:
            cp.wait_recv()
        for cp in sent_mid:
            cp.start()

    def late(*args):
        _, _, _, _, landed_mid, sent_last, _ = copies(*args)
        for cp in landed_mid:
            cp.wait_recv()
        for cp in sent_last:
            cp.start()

    def post(*args):
        own, sent_first, _, sent_mid, _, sent_last, landed_sibling = copies(*args)
        for cp in landed_sibling:
            cp.wait_recv()
        for cp in sent_first + sent_mid + sent_last:
            cp.wait_send()
        for mine in own:
            mine.wait()

    return _Comm(list(shards), [jax.ShapeDtypeStruct((N_DEV,) + s.shape, s.dtype) for s in shards],
                 (slots * n, slots * n, n), pre, post, mid, mid_at, late)


def _exchange_program(arrays, n_slots, source, target, slot):
    n = len(arrays)

    def copies(ins, outs, send_sems, recv_sems, local_sems, base):
        place = _place()
        res = []
        for a in range(n):
            for r in range(n_slots):
                src = source(r, place)
                if src is None:
                    continue
                res.append(pltpu.make_async_remote_copy(
                    src_ref=ins[a].at[src], dst_ref=outs[a].at[slot(r)],
                    send_sem=send_sems.at[base[0] + n_slots * a + r],
                    recv_sem=recv_sems.at[base[1] + n_slots * a + r],
                    device_id=target(r, place), device_id_type=MESH))
        return res

    def pre(*args):
        for cp in copies(*args):
            cp.start()

    def post(*args):
        for cp in copies(*args):
            cp.wait()

    n_out = len({slot(r) for r in range(n_slots) if source(r, (0, 0, 0)) is not None})
    return _Comm(list(arrays), [jax.ShapeDtypeStruct((n_out,) + g.shape[1:], g.dtype) for g in arrays],
                 (n_slots * n, n_slots * n, 0), pre, post)


def _sibling_program(grads):
    def source(r, place):
        x, y, c = place
        return 4 * _flip(x, RELATIONS[r][0]) + 2 * _flip(y, RELATIONS[r][1]) + (1 - c)

    return _exchange_program(grads, 4, source, lambda r, p: (p[0], p[1], 1 - p[2]), lambda r: r)


def _chips_program(sums, relations=(1, 2, 3)):
    def target(r, place):
        x, y, c = place
        return (_flip(x, RELATIONS[r][0]), _flip(y, RELATIONS[r][1]), c)

    return _exchange_program(sums, 4, lambda r, p: r if r in relations else None, target,
                             lambda r: relations.index(r) if r in relations else 0)


def _standalone(name, comm):
    return _call(name, lambda: None, (1,), [], [], [], [], [], comm)[1]


def _pair_sum(name, grad, from_sibling, block_ids):
    _, rows, cols = grad.shape
    tr = _tile(rows, 512, 16)

    def body(ids_ref, mine_ref, sib_ref, o_ref):
        o_ref[...] = (mine_ref[...].astype(F32) + sib_ref[...].astype(F32)).astype(o_ref.dtype)

    return pl.pallas_call(
        body,
        name=name,
        grid_spec=pltpu.PrefetchScalarGridSpec(
            num_scalar_prefetch=1,
            grid=(4, rows // tr),
            in_specs=[pl.BlockSpec((None, tr, cols), lambda r, i, ids: (ids[r], i, 0)),
                      pl.BlockSpec((None, tr, cols), lambda r, i, ids: (r, i, 0))],
            out_specs=pl.BlockSpec((None, tr, cols), lambda r, i, ids: (r, i, 0))),
        out_shape=jax.ShapeDtypeStruct((4, rows, cols), BF16),
        compiler_params=_params(("arbitrary", "arbitrary")),
    )(block_ids, grad, from_sibling)


def _adam(w, g, m, v):
    m = ADAM_B1 * m + (1.0 - ADAM_B1) * g
    v = ADAM_B2 * v + (1.0 - ADAM_B2) * (g * g)
    m_hat = m / (1.0 - ADAM_B1 ** ADAM_STEP)
    v_hat = v / (1.0 - ADAM_B2 ** ADAM_STEP)
    delta = -ADAM_LR * (m_hat / (jnp.sqrt(v_hat) + ADAM_EPS) + ADAM_WD * w)
    return delta, m, v


def _adam_sharded(name, items, comm=None):
    _, rows, cols = items[0][0].shape
    n = len(items)
    tr = _tile(rows, max(64, 256 // n), 8)

    def body(*refs):
        for k in range(n):
            w_ref, m_ref, v_ref, s_ref, r1_ref, r2_ref, r3_ref = refs[7 * k:7 * k + 7]
            g_ref, d_ref, nm_ref, nv_ref = refs[7 * n + 4 * k:7 * n + 4 * k + 4]
            g = ((s_ref[...].astype(F32) + r1_ref[...].astype(F32)) + r2_ref[...].astype(F32)) + r3_ref[...].astype(F32)
            delta, nm, nv = _adam(w_ref[...], g, m_ref[...], v_ref[...])
            g_ref[...] = g
            d_ref[...] = delta
            nm_ref[...] = nm
            nv_ref[...] = nv

    slot = lambda s: pl.BlockSpec((None, tr, cols), lambda i: (s, i, 0))
    flat = slot(0)
    args, specs = [], []
    for w, m, v, sums, from_chips in items:
        args += [w, m, v, sums] + [a for a, _ in from_chips]
        specs += [flat, flat, flat, slot(0)] + [slot(s) for _, s in from_chips]
    res, sent = _call(
        name, body, (rows // tr,), specs, [flat] * (4 * n),
        [jax.ShapeDtypeStruct((1, rows, cols), F32)] * (4 * n), [], args, comm)
    return [res[4 * k:4 * k + 4] for k in range(n)], sent


def _all_reduce_small(part):
    rows = part.shape[0]

    def body(x_ref, o_ref, gathered, send_sems, recv_sems):
        x, y, c = _place()
        me = 4 * x + 2 * y + c
        gathered[me] = x_ref[...]
        copies = []
        for k in range(1, N_DEV):
            bits = (k >> 2 & 1, k >> 1 & 1, k & 1)
            peer = (_flip(x, bits[0]), _flip(y, bits[1]), _flip(c, bits[2]))
            copies.append(pltpu.make_async_remote_copy(
                src_ref=x_ref, dst_ref=gathered.at[me],
                send_sem=send_sems.at[k - 1], recv_sem=recv_sems.at[k - 1],
                device_id=peer, device_id_type=MESH))
        for cp in copies:
            cp.start()
        for cp in copies:
            cp.wait()
        total = gathered[0]
        for b in range(1, N_DEV):
            total = total + gathered[b]
        o_ref[...] = total

    return pl.pallas_call(
        body,
        name="small_all_reduce",
        in_specs=[pl.BlockSpec(memory_space=pltpu.VMEM)],
        out_specs=pl.BlockSpec(memory_space=pltpu.VMEM),
        out_shape=jax.ShapeDtypeStruct((rows, LANES), F32),
        scratch_shapes=[pltpu.VMEM((N_DEV, rows, LANES), F32),
                        pltpu.SemaphoreType.DMA((N_DEV - 1,)), pltpu.SemaphoreType.DMA((N_DEV - 1,))],
    )(part)


def _adam_small(w, g, m, v):
    def fn(wv, gv, mv, vv):
        return _adam(wv, gv, mv, vv)

    shape = (w.shape, F32)
    return _rows("adam_small", fn, [(w, True), (g, True), (m, True), (v, True)], [shape] * 3, w.shape[0])


SMALL = ("ffn1_norm", "mix_norm", "attn_sinks", "hgrn_lower_bound", "hgrn_norm", "ffn2_norm", "ple_norm",
         "final_norm")
LARGE = ("ffn1_w_gate", "ffn1_w_up", "ffn1_w_down", "w_in", "w_up_a", "w_up_b", "w_out",
         "ffn2_w_gate", "ffn2_w_up", "ffn2_w_down", "ple_w_gate", "ple_w_proj")
TRANSPOSED = ("ffn1_w_gate", "ffn1_w_up", "ffn2_w_gate", "ffn2_w_up", "w_in")
ADAM_GROUPS = (("ffn2_w_down", "ffn2_w_gate", "ffn2_w_up"), ("ple_w_gate", "w_out"), ("w_up_a", "w_up_b"),
               ("ple_w_proj",), ("w_in",), ("ffn1_w_down", "ffn1_w_gate", "ffn1_w_up"))
WEIGHTS = ("ffn1_norm", "ffn1_w_gate", "ffn1_w_up", "ffn1_w_down", "mix_norm", "w_in", "attn_sinks",
           "hgrn_lower_bound", "hgrn_norm", "w_up_a", "w_up_b", "w_out", "ffn2_norm", "ffn2_w_gate",
           "ffn2_w_up", "ffn2_w_down", "ple_norm", "ple_w_gate", "ple_w_proj", "final_norm")


def _pack_rows(arrays):
    rows = []
    for a in arrays:
        flat = a.reshape(-1).astype(F32)
        pad = -flat.shape[0] % LANES
        rows.append(jnp.pad(flat, (0, pad)).reshape(-1, LANES))
    packed = jnp.concatenate(rows, axis=0)
    return jnp.pad(packed, ((0, -packed.shape[0] % 8), (0, 0)))


def _unpack_rows(packed, like):
    out, at = [], 0
    for a in like:
        size = a.size
        n_rows = -(-size // LANES)
        out.append(packed[at:at + n_rows].reshape(-1)[:size].reshape(a.shape))
        at += n_rows
    return out


def kernel(x, p, positions, ffn1_norm, ffn1_w_gate, ffn1_w_up, ffn1_w_down, mix_norm, w_in, attn_sinks, hgrn_lower_bound, hgrn_norm, w_up_a, w_up_b, w_out, ffn2_norm, ffn2_w_gate, ffn2_w_up, ffn2_w_down, ple_norm, ple_w_gate, ple_w_proj, final_norm, loss_target, m_ffn1_norm, m_ffn1_w_gate, m_ffn1_w_up, m_ffn1_w_down, m_mix_norm, m_w_in, m_attn_sinks, m_hgrn_lower_bound, m_hgrn_norm, m_w_up_a, m_w_up_b, m_w_out, m_ffn2_norm, m_ffn2_w_gate, m_ffn2_w_up, m_ffn2_w_down, m_ple_norm, m_ple_w_gate, m_ple_w_proj, m_final_norm, v_ffn1_norm, v_ffn1_w_gate, v_ffn1_w_up, v_ffn1_w_down, v_mix_norm, v_w_in, v_attn_sinks, v_hgrn_lower_bound, v_hgrn_norm, v_w_up_a, v_w_up_b, v_w_out, v_ffn2_norm, v_ffn2_w_gate, v_ffn2_w_up, v_ffn2_w_down, v_ple_norm, v_ple_w_gate, v_ple_w_proj, v_final_norm):
    given = dict(locals())
    w = {n: given[n] for n in WEIGHTS}
    mom = {n: given["m_" + n] for n in WEIGHTS}
    var = {n: given["v_" + n] for n in WEIGHTS}
    t, d = x.shape[1], x.shape[2]

    x0, p16, target = x[0], p[0, 0].astype(BF16), loss_target[0]
    tables = _rope_tables(positions)
    sinks = attn_sinks.reshape(-1)
    final_w = final_norm.reshape(1, d)
    as_stored = lambda n, a: jnp.swapaxes(a, 1, 2) if n in TRANSPOSED else a
    shard = {n: as_stored(n, w[n])[0].astype(BF16) for n in LARGE}
    shard["w_in_left"], shard["w_in_right"] = shard["w_in"][:, :d // 2], shard["w_in"][:, d // 2:]
    xi, yi, ci = _place()
    block_ids = jnp.stack([4 * _flip(xi, rx) + 2 * _flip(yi, ry) + ci for rx, ry in RELATIONS]).astype(jnp.int32)

    def gather(*names):
        return _gather_program([shard[n] for n in names], 0.65)

    wg1, wu1 = _standalone("gather_ffn1", gather("ffn1_w_gate", "ffn1_w_up"))
    h1 = _rmsnorm("ffn1_norm", x0, ffn1_norm)
    (g1, u1, a1), (wd1, win_left) = _ffn_up("ffn1", h1, wg1, wu1, gather("ffn1_w_down", "w_in_left"))
    x1, (win_right,) = _ffn_down("ffn1", x0, a1, wd1, gather("w_in_right"))
    win = (win_left.reshape(-1, d // 2), win_right.reshape(-1, d // 2))
    mix_args = (mix_norm, win, sinks, hgrn_lower_bound, hgrn_norm)

    def mixer_weights(sent):
        wua, wub, wout_g, _ = sent["proj"]
        return wua, wub, wout_g.reshape(d, d)

    x2, mix_saved, got_w = _mixer_fwd(
        x1, *mix_args, mixer_weights, tables,
        dict(proj=gather("w_up_a", "w_up_b", "w_out", "ffn2_w_gate"), attention=gather("ffn2_w_up"),
             hgrn=gather("ple_w_gate", "ple_w_proj")))
    wua, wub, wout = mixer_weights(got_w)
    wg2, (wu2,) = got_w["proj"][3], got_w["attention"]
    wpg_g, wpp = got_w["hgrn"]
    wpg = wpg_g.reshape(d, d)
    h3 = _rmsnorm("ffn2_norm", x2, ffn2_norm)
    (g2, u2, a2), (wd2,) = _ffn_up("ffn2", h3, wg2, wu2, gather("ffn2_w_down"))
    x3, _ = _ffn_down("ffn2", x2, a2, wd2)
    x4, ple_saved = _ple_fwd(x3, ple_norm, p16, wpg, wpp)
    dx4, d_final, loss_row = _loss_head(x4, final_w, target)

    sums, from_chips = {}, {}

    def pair_sums(names, grads, from_sibling):
        for n, g, s in zip(names, grads, from_sibling):
            sums[n] = _pair_sum("pair_sum_" + n, g, s, block_ids)

    def to_chips(names, grads, from_sibling):
        pair_sums(names, grads, from_sibling)
        return _chips_program([sums[n] for n in names])

    def arrived(names, results):
        for n, r in zip(names, results):
            from_chips[n] = [(r, 0), (r, 1), (r, 2)]

    def ffn_comms(first, second, names):
        dn, gn, un = ((n,) for n in names)
        return dict(
            dact=first, dwd=second,
            dwg=lambda av: _sibling_program([av["dwd"]]),
            dwu=lambda av: _merge([to_chips(dn, [av["dwd"]], av["sent"]["dwg"]), _sibling_program([av["dwg"]])]),
            dxa=lambda av: _merge([to_chips(gn, [av["dwg"]], av["sent"]["dwu"][1:]), _sibling_program([av["dwu"]])]),
            dxb=lambda av: to_chips(un, [av["dwu"]], av["sent"]["dxa"][1:]))

    def ffn_arrived(names, sent):
        arrived(names, [sent["dwu"][0], sent["dxa"][0], sent["dxb"][0]])

    dx3, dx3_16, dwpp, dwpg, d_ple = _ple_bwd(dx4, x3, ple_norm, p16, wpg, ple_saved)
    ple_names = ("ple_w_proj", "ple_w_gate")
    ple_grads = [dwpp, dwpg.reshape(N_DEV, d // N_DEV, d)]
    ffn2_names = ("ffn2_w_down", "ffn2_w_gate", "ffn2_w_up")
    dx2, dx2_16, _, _, _, d_ffn2, sent2 = _ffn_bwd(
        "ffn2b", dx3, dx3_16, x2, h3, g2, u2, a2, wg2, wu2, wd2, ffn2_norm,
        ffn_comms(_sibling_program(ple_grads), lambda av: to_chips(ple_names, ple_grads, av["sent"]["dact"]),
                  ffn2_names))
    arrived(ple_names, sent2["dwd"])
    ffn_arrived(ffn2_names, sent2)
    mix3 = ("w_out", "w_up_a", "w_up_b")
    dx1, dx1_16, mix_small, mix_made = _mixer_bwd(
        dx2, dx2_16, x1, *mix_args, wua, wub, wout, tables, mix_saved,
        dict(attention=lambda av: _sibling_program([av[n] for n in mix3]),
             hgrn=lambda av: to_chips(mix3, [av[n] for n in mix3], av["sent"]["attention"]),
             dx=lambda av: _sibling_program([av["w_in"]])))
    sent_mix = mix_made["sent"]
    arrived(mix3, sent_mix["hgrn"])
    pair_sums(("w_in",), [mix_made["w_in"]], sent_mix["dx"])
    ffn1_names = ("ffn1_w_down", "ffn1_w_gate", "ffn1_w_up")
    grad_x, _, _, _, _, d_ffn1, sent1 = _ffn_bwd(
        "ffn1b", dx1, dx1_16, x0, h1, g1, u1, a1, wg1, wu1, wd1, ffn1_norm,
        ffn_comms(_chips_program([sums["w_in"]], (1, 2)), _chips_program([sums["w_in"]], (3,)), ffn1_names))
    (near,), (far,) = sent1["dact"], sent1["dwd"]
    from_chips["w_in"] = [(near, 0), (near, 1), (far, 0)]
    ffn_arrived(ffn1_names, sent1)

    out = {}
    for group in ADAM_GROUPS:
        items = [(as_stored(n, w[n]), as_stored(n, mom[n]), as_stored(n, var[n]), sums[n], from_chips[n])
                 for n in group]
        results, _ = _adam_sharded("adam_" + group[0], items)
        for n, res in zip(group, results):
            out[n] = [as_stored(n, a) for a in res]

    dlb = mix_small["hgrn_lb"]
    small_grads = dict(
        ffn1_norm=d_ffn1, mix_norm=mix_small["mix_norm"], attn_sinks=mix_small["attn_sinks"][:, :A_HEADS],
        hgrn_lower_bound=jnp.concatenate([dlb, -dlb], axis=0), hgrn_norm=mix_small["hgrn_norm"],
        ffn2_norm=d_ffn2, ple_norm=d_ple, final_norm=d_final)

    small_like = [w[n] for n in SMALL]
    packed = _pack_rows([small_grads[n].reshape(w[n].shape) for n in SMALL] + [loss_row[:, :1]])
    total = _all_reduce_small(packed)
    n_rows = total.shape[0]
    pack_w = _pack_rows(small_like + [jnp.zeros((1, 1), F32)])
    pack_m = _pack_rows([mom[n] for n in SMALL] + [jnp.zeros((1, 1), F32)])
    pack_v = _pack_rows([var[n] for n in SMALL] + [jnp.zeros((1, 1), F32)])
    deltas = _adam_small(pack_w, total, pack_m, pack_v)
    like = small_like + [jnp.zeros((1, 1), F32)]
    g_small = _unpack_rows(total, like)
    d_small, m_small, v_small = (_unpack_rows(a, like) for a in deltas)
    for k, n in enumerate(SMALL):
        out[n] = [g_small[k], d_small[k], m_small[k], v_small[k]]
    loss = g_small[-1].reshape(())

    return (loss, grad_x.reshape(x.shape),
            *[out[n][0] for n in WEIGHTS], *[out[n][1] for n in WEIGHTS],
            *[out[n][2] for n in WEIGHTS], *[out[n][3] for n in WEIGHTS])
```

```python
import functools

import jax
import jax.numpy as jnp
from jax import lax
from jax.experimental import pallas as pl
from jax.experimental.pallas import tpu as pltpu

F32 = jnp.float32
BF16 = jnp.bfloat16
MESH = pl.DeviceIdType.MESH

EPS = 1e-6
N_DEV = 8
A_HEADS = 16
A_HEAD_DIM = 64
A_WIDTH = 1024
A_KV_WIDTH = 256
ATT_BLOCK = 128
ROT_DIM = 16
ROPE_THETA = 500000.0
B_WIDTH = 1024
B_HEAD = 128
CHUNK = 64
SUB = 16
MAX_DECAY_EXP = 60.0
PLE_DIM = 256
LANES = 128

ADAM_LR = 0.001
ADAM_B1 = 0.9
ADAM_B2 = 0.999
ADAM_EPS = 1e-08
ADAM_WD = 0.01
ADAM_STEP = 10

VMEM_LIMIT = 56 * 1024 * 1024
EPILOGUE_ELEMS = 256 * 1024
LATE_AT = 0.92

NN = (((1,), (0,)), ((), ()))
NT = (((1,), (1,)), ((), ()))
TN = (((0,), (0,)), ((), ()))

OFF_QA = 0
OFF_KA = 1024
OFF_VA = 1280
OFF_QB = 1536
OFF_FB = 2560
OFF_IB = 3584
OFF_OG = 4608
OFF_GA = 5632


def _sig(v):
    return 1.0 / (1.0 + jnp.exp(-v))


def _tile(n, pref, mult):
    t = min(n, pref)
    t -= t % mult
    while n % t:
        t -= mult
    return t


def _params(sem):
    return pltpu.CompilerParams(dimension_semantics=sem, vmem_limit_bytes=VMEM_LIMIT)


HBM_SPEC = pl.BlockSpec(memory_space=pltpu.HBM)


class _Comm:
    def __init__(self, ins, out_shapes, n_sems, pre, post, mid=None, mid_at=1.0, late=None):
        self.ins, self.out_shapes, self.n_sems, self.pre, self.post = ins, out_shapes, n_sems, pre, post
        self.mid = mid if mid is not None else (lambda *args: None)
        self.late = late if late is not None else (lambda *args: None)
        self.mid_at = mid_at
        self.parts = [len(out_shapes)]

    def split(self, outs):
        res, at = [], 0
        for n in self.parts:
            res.append(list(outs[at:at + n]))
            at += n
        return res


def _merge(progs):
    spans, ins, shapes, sems = [], [], [], [0, 0, 0]
    for p in progs:
        spans.append((len(ins), len(shapes), tuple(sems)))
        ins += list(p.ins)
        shapes += list(p.out_shapes)
        sems = [a + b for a, b in zip(sems, p.n_sems)]

    def run(which):
        def go(cin, cout, send, recv, local, base):
            for p, (i0, o0, s0) in zip(progs, spans):
                getattr(p, which)(cin[i0:i0 + len(p.ins)], cout[o0:o0 + len(p.out_shapes)], send, recv, local,
                                  tuple(b + s for b, s in zip(base, s0)))
        return go

    merged = _Comm(ins, shapes, tuple(sems), run("pre"), run("post"), run("mid"), max(p.mid_at for p in progs),
                   run("late"))
    merged.parts = [len(p.out_shapes) for p in progs]
    return merged


def _call(name, body, grid, in_specs, out_specs, out_shape, scratch, args, comm=None, aliases=None):
    sem = ("arbitrary",) * len(grid)
    aliases = aliases or {}
    if comm is None:
        res = pl.pallas_call(body, name=name, grid=grid, in_specs=in_specs, out_specs=out_specs,
                             out_shape=out_shape, scratch_shapes=scratch, input_output_aliases=aliases,
                             compiler_params=_params(sem))(*args)
        return list(res), []
    n_in, n_out, n_scr = len(in_specs), len(out_specs), len(scratch)
    c_in, c_out = len(comm.ins), len(comm.out_shapes)
    steps = functools.reduce(lambda a, b: a * b, grid)
    mid_step = min(steps - 1, int(comm.mid_at * steps))
    late_step = min(steps - 1, max(mid_step, int(LATE_AT * steps)))

    def carrier(*refs):
        at = [0]

        def take(n):
            at[0] += n
            return refs[at[0] - n:at[0]]

        ins, cins, outs, couts, scr, sems = take(n_in), take(c_in), take(n_out), take(c_out), take(n_scr), take(3)
        ids = [pl.program_id(a) for a in range(len(grid))]
        step = functools.reduce(lambda acc, ig: acc * ig[1] + ig[0], zip(ids, grid), 0)

        @pl.when(step == 0)
        def _():
            comm.pre(cins, couts, *sems, (0, 0, 0))

        body(*ins, *outs, *scr)

        @pl.when(step == mid_step)
        def _():
            comm.mid(cins, couts, *sems, (0, 0, 0))

        @pl.when(step == late_step)
        def _():
            comm.late(cins, couts, *sems, (0, 0, 0))

        @pl.when(step == steps - 1)
        def _():
            comm.post(cins, couts, *sems, (0, 0, 0))

    res = pl.pallas_call(
        carrier, name=name, grid=grid,
        in_specs=list(in_specs) + [HBM_SPEC] * c_in,
        out_specs=list(out_specs) + [HBM_SPEC] * c_out,
        out_shape=list(out_shape) + list(comm.out_shapes),
        scratch_shapes=list(scratch) + [pltpu.SemaphoreType.DMA((max(n, 1),)) for n in comm.n_sems],
        input_output_aliases=aliases,
        compiler_params=_params(sem),
    )(*args, *comm.ins)
    return list(res[:n_out]), list(res[n_out:])


def _mm(name, grid, ins, prods, acc_shapes, epi, outs, comm=None, aliases=None):
    n_in, n_out, nk = len(ins), len(outs), grid[2]
    tile_rows, tile_cols = acc_shapes[0]
    chunk = tile_rows
    while chunk * tile_cols > EPILOGUE_ELEMS and chunk % 32 == 0:
        chunk //= 2

    def body(*refs):
        in_refs = refs[:n_in]
        out_refs = refs[n_in:n_in + n_out]
        acc_refs = refs[n_in + n_out:]
        first_tile = jnp.logical_and(pl.program_id(0) == 0, pl.program_id(1) == 0)

        def product(lhs, rhs, dims):
            a = in_refs[lhs][...].astype(BF16)
            b = in_refs[rhs][...].astype(BF16)
            return lax.dot_general(a, b, dims, preferred_element_type=F32)

        def write(accs):
            totals = [None] * n_out
            for c in range(tile_rows // chunk):
                rows = slice(c * chunk, (c + 1) * chunk)
                vals = epi([a[rows] for a in accs], in_refs, rows)
                for idx, (o, v, spec) in enumerate(zip(out_refs, vals, outs)):
                    if spec[4]:
                        totals[idx] = v if totals[idx] is None else totals[idx] + v
                    else:
                        o[rows] = v.astype(o.dtype)
            for o, v, spec in zip(out_refs, totals, outs):
                if spec[4]:
                    @pl.when(first_tile)
                    def _():
                        o[...] = v.astype(o.dtype)

                    @pl.when(jnp.logical_not(first_tile))
                    def _():
                        o[...] += v.astype(o.dtype)

        if nk == 1:
            accs = [None] * len(acc_shapes)
            for lhs, rhs, acc, dims in prods:
                t = product(lhs, rhs, dims)
                accs[acc] = t if accs[acc] is None else accs[acc] + t
            write(accs)
        else:
            k = pl.program_id(2)
            sums = {}
            for lhs, rhs, acc, dims in prods:
                t = product(lhs, rhs, dims)
                sums[acc] = t if acc not in sums else sums[acc] + t

            @pl.when(k == 0)
            def _():
                for acc, t in sums.items():
                    acc_refs[acc][...] = t

            @pl.when(k > 0)
            def _():
                for acc, t in sums.items():
                    acc_refs[acc][...] += t

            @pl.when(k == nk - 1)
            def _():
                write(acc_refs)

    res, sent = _call(
        name, body, grid,
        [pl.BlockSpec(memory_space=pl.ANY) if b is None else pl.BlockSpec(b, im) for _, b, im in ins],
        [pl.BlockSpec(o[2], o[3]) for o in outs],
        [jax.ShapeDtypeStruct(o[0], o[1]) for o in outs],
        [] if nk == 1 else [pltpu.VMEM(s, F32) for s in acc_shapes],
        [a for a, _, _ in ins], comm, aliases)
    return res if comm is None else (res, sent)


def _rms_bwd(dh, xv, w, dres):
    r = lax.rsqrt(jnp.mean(xv * xv, axis=-1, keepdims=True) + EPS)
    g = dh * w
    dx = dres + r * (g - xv * (r * r) * jnp.mean(g * xv, axis=-1, keepdims=True))
    dw = jnp.sum(dh * xv * r, axis=0, keepdims=True)
    return dx, dw


def _rows(name, fn, ins, outs, tm):
    n_in = len(ins)
    rows = outs[0][0][0]

    def body(*refs):
        vals = fn(*[r[...] for r in refs[:n_in]])
        for o, v in zip(refs[n_in:], vals):
            o[...] = v.astype(o.dtype)

    in_specs = []
    for a, tiled in ins:
        if tiled:
            in_specs.append(pl.BlockSpec((tm, a.shape[1]), lambda i: (i, 0)))
        else:
            in_specs.append(pl.BlockSpec(a.shape, lambda i, nd=a.ndim: (0,) * nd))
    return pl.pallas_call(
        body,
        name=name,
        grid=(rows // tm,),
        in_specs=in_specs,
        out_specs=[pl.BlockSpec((tm, s[1]), lambda i: (i, 0)) for s, _ in outs],
        out_shape=[jax.ShapeDtypeStruct(s, d) for s, d in outs],
        compiler_params=_params(("arbitrary",)),
    )(*[a for a, _ in ins])


def _rmsnorm(name, x, w):
    def fn(xv, wv):
        r = lax.rsqrt(jnp.mean(xv * xv, axis=-1, keepdims=True) + EPS)
        return (xv * r * wv,)

    return _rows(name, fn, [(x, True), (w, False)], [(x.shape, BF16)], _tile(x.shape[0], 512, 16))[0]


def _mmc(*args, comm=None, aliases=None):
    if comm is None:
        return _mm(*args, aliases=aliases), []
    return _mm(*args, comm=comm, aliases=aliases)


def _carried(comms, key, avail):
    c = comms.get(key) if comms else None
    return c(avail) if callable(c) else c


def _ffn_up(name, h, wg, wu, comm=None):
    t, d = h.shape
    n = wg.shape[1]
    tm = _tile(t, 1024, 16)

    def epi_up(accs, refs, rows):
        g, u = accs
        return [g, u, g * _sig(g) * u]

    hid = ((N_DEV, t, n), BF16, (None, tm, n), lambda i, j, k: (j, i, 0), False)
    return _mmc(
        name + "_up", (t // tm, N_DEV, 1),
        [(h, (tm, d), lambda i, j, k: (i, 0)),
         (wg, (None, n, d), lambda i, j, k: (j, 0, 0)),
         (wu, (None, n, d), lambda i, j, k: (j, 0, 0))],
        [(0, 1, 0, NT), (0, 2, 1, NT)], [(tm, n)] * 2, epi_up, [hid] * 3, comm=comm)


def _ffn_down(name, x, act, wd, comm=None):
    t, d = x.shape
    n = wd.shape[1]
    tm2 = _tile(t, 1024, 8)
    tn = _tile(d, 1024, 128)

    def epi_down(accs, refs, rows):
        return [refs[2][rows] + 0.5 * accs[0]]

    (x_out,), sent = _mmc(
        name + "_down", (t // tm2, d // tn, N_DEV),
        [(act, (None, tm2, n), lambda i, j, k: (k, i, 0)),
         (wd, (None, n, tn), lambda i, j, k: (k, 0, j)),
         (x, (tm2, tn), lambda i, j, k: (i, j))],
        [(0, 1, 0, NN)], [(tm2, tn)], epi_down,
        [((t, d), F32, (tm2, tn), lambda i, j, k: (i, j), False)], comm=comm)
    return x_out, sent


def _ffn_bwd(name, dxo, dxo16, x, h, gate, up, act, wg, wu, wd, norm_w, comms=None):
    t, d = x.shape
    n = wg.shape[1]
    tm = _tile(t, 1024, 16)
    sent = {}
    avail = dict(sent=sent)

    def epi_act(accs, refs, rows):
        da = 0.5 * accs[0]
        g = refs[2][rows].astype(F32)
        u = refs[3][rows].astype(F32)
        s = _sig(g)
        return [da * u * (s * (1.0 + g * (1.0 - s))), da * g * s]

    hid_in = lambda a: (a, (None, tm, n), lambda i, j, k: (j, i, 0))
    hid_out = ((N_DEV, t, n), BF16, (None, tm, n), lambda i, j, k: (j, i, 0), False)
    (dgate, dup), sent["dact"] = _mmc(
        name + "_dact", (t // tm, N_DEV, 1),
        [(dxo16, (tm, d), lambda i, j, k: (i, 0)),
         (wd, (None, n, d), lambda i, j, k: (j, 0, 0)),
         hid_in(gate), hid_in(up)],
        [(0, 1, 0, NT)], [(tm, n)], epi_act, [hid_out] * 2, comm=_carried(comms, "dact", avail))

    tk = _tile(t, 1024, 16)
    w_out = ((N_DEV, n, d), BF16, (None, n, d), lambda i, j, k: (i, 0, 0), False)

    def weight_grad(key, acts, stream, scale):
        (dw,), sent[key] = _mmc(
            name + "_" + key, (N_DEV, 1, t // tk),
            [(acts, (None, tk, n), lambda i, j, k: (i, k, 0)), (stream, (tk, d), lambda i, j, k: (k, 0))],
            [(0, 1, 0, TN)], [(n, d)], lambda accs, refs, rows: [scale * accs[0]], [w_out],
            comm=_carried(comms, key, avail))
        avail[key] = dw
        return dw

    dwd = weight_grad("dwd", act, dxo16, 0.5)
    dwg = weight_grad("dwg", dgate, h, 1.0)
    dwu = weight_grad("dwu", dup, h, 1.0)

    tm3 = _tile(t // 2, 512, 16)
    half = t // 2 // tm3

    def epi_dx(accs, refs, rows):
        dx, dw = _rms_bwd(accs[0], refs[4][rows], refs[5][...], refs[6][rows])
        return [dx, dx, dw]

    def dx_half(key, first, filled):
        at = 0 if first else half
        row = lambda i, j, k: (i + at, 0)
        hid = lambda i, j, k: (k, i + at, 0)
        ins = [(dgate, (None, tm3, n), hid), (wg, (None, n, d), lambda i, j, k: (k, 0, 0)),
               (dup, (None, tm3, n), hid), (wu, (None, n, d), lambda i, j, k: (k, 0, 0)),
               (x, (tm3, d), row), (norm_w, (1, d), lambda i, j, k: (0, 0)), (dxo, (tm3, d), row)]
        ins += [(a, None, None) for a in filled]
        res, sent[key] = _mmc(
            name + "_" + key, (half, 1, N_DEV), ins,
            [(0, 1, 0, NN), (2, 3, 0, NN)], [(tm3, d)], epi_dx,
            [((t, d), F32, (tm3, d), row, False), ((t, d), BF16, (tm3, d), row, False),
             ((1, d), F32, (1, d), lambda i, j, k: (0, 0), True)],
            comm=_carried(comms, key, avail), aliases={7 + k: k for k in range(len(filled))})
        return res

    dx_a, dx16_a, dnorm_a = dx_half("dxa", True, [])
    dx, dx16, dnorm_b = dx_half("dxb", False, [dx_a, dx16_a])
    return dx, dx16, dwg, dwu, dwd, dnorm_a + dnorm_b, sent


def _rope(tv, cos, s1, s2):
    return tv * cos + pltpu.roll(tv, LANES - 8, 1) * s1 + pltpu.roll(tv, 8, 1) * s2


def _rope_bwd(dr, cos, s1, s2):
    return dr * cos + pltpu.roll(dr * s1, 8, 1) + pltpu.roll(dr * s2, LANES - 8, 1)


def _rope_tables(positions):
    inv_freq = jnp.power(jnp.float32(ROPE_THETA), -jnp.arange(0, ROT_DIM, 2, dtype=F32) / ROT_DIM)
    ang = positions.reshape(-1).astype(F32)[:, None] * inv_freq
    cos, sin = jnp.cos(ang), jnp.sin(ang)
    t = ang.shape[0]
    pad = A_HEAD_DIM - ROT_DIM
    cos_t = jnp.concatenate([cos, cos, jnp.ones((t, pad), F32)], axis=1)
    s1_t = jnp.concatenate([-sin, jnp.zeros((t, pad + 8), F32)], axis=1)
    s2_t = jnp.concatenate([jnp.zeros((t, 8), F32), sin, jnp.zeros((t, pad), F32)], axis=1)
    return tuple(jnp.tile(v, (1, 2)) for v in (cos_t, s1_t, s2_t))


def _att_common(i, k_p, k_c, v_p, v_c, tabs_p, tabs_c):
    kcat = jnp.concatenate([k_p, k_c], axis=0)
    vcat = jnp.concatenate([v_p, v_c], axis=0)
    tabs_k = [jnp.concatenate([a, b], axis=0) for a, b in zip(tabs_p, tabs_c)]
    kvar, vvar = [], []
    for ks in range(2):
        sl = slice(ks * LANES, (ks + 1) * LANES)
        kr = _rope(kcat[:, sl], *tabs_k)
        kvar.append([kr.astype(BF16), pltpu.roll(kr, 64, 1).astype(BF16)])
        vvar.append([vcat[:, sl].astype(BF16), pltpu.roll(vcat[:, sl], 64, 1).astype(BF16)])
    qi = lax.broadcasted_iota(jnp.int32, (ATT_BLOCK, 2 * ATT_BLOCK), 0)
    kj = lax.broadcasted_iota(jnp.int32, (ATT_BLOCK, 2 * ATT_BLOCK), 1)
    dist = qi + ATT_BLOCK - kj
    allowed = (dist >= 0) & (dist < ATT_BLOCK) & ((i > 0) | (kj >= ATT_BLOCK))
    lane = lax.broadcasted_iota(jnp.int32, (1, LANES), 1)
    halves = [lane < 64, lane >= 64]
    return kvar, vvar, allowed, halves


def _att_head_probs(qm, kslab, allowed, sink):
    s = lax.dot_general(qm, kslab, NT, preferred_element_type=F32)
    s = jnp.where(allowed, s, -jnp.inf)
    m = jnp.maximum(jnp.max(s, axis=-1, keepdims=True), sink)
    e = jnp.exp(s - m)
    es = jnp.exp(sink - m)
    den = jnp.sum(e, axis=-1, keepdims=True) + es
    inv = 1.0 / den
    return e * inv, es * inv


def _head_slots(h):
    pair, half = h // 2, h % 2
    kvh = h // 4
    return pair, half, kvh // 2, int(kvh % 2 != half)


def _attention_fwd(proj, tables, sinks, comm=None):
    t = proj.shape[0]
    nb = t // ATT_BLOCK
    b = ATT_BLOCK
    scale = A_HEAD_DIM ** -0.5

    def body(sink_ref, q_ref, kp_ref, kc_ref, vp_ref, vc_ref,
             cp_ref, ap_ref, bp_ref, cc_ref, ac_ref, bc_ref, o_ref):
        i = pl.program_id(0)
        tabs_p = (cp_ref[...], ap_ref[...], bp_ref[...])
        tabs_c = (cc_ref[...], ac_ref[...], bc_ref[...])
        kvar, vvar, allowed, halves = _att_common(
            i, kp_ref[...], kc_ref[...], vp_ref[...], vc_ref[...], tabs_p, tabs_c)
        for pair in range(A_HEADS // 2):
            sl = slice(pair * LANES, (pair + 1) * LANES)
            q2 = _rope(q_ref[:, sl], *tabs_c) * scale
            acc = jnp.zeros((b, LANES), F32)
            for half in range(2):
                h = 2 * pair + half
                _, _, ks, var = _head_slots(h)
                qm = jnp.where(halves[half], q2, 0.0).astype(BF16)
                p, _ = _att_head_probs(qm, kvar[ks][var], allowed, sink_ref[h])
                o = lax.dot_general(p.astype(BF16), vvar[ks][var], NN, preferred_element_type=F32)
                acc = acc + jnp.where(halves[half], o, 0.0)
            o_ref[:, sl] = acc.astype(o_ref.dtype)

    cur = lambda col: (lambda i: (i, col))
    prev = lambda col: (lambda i: (jnp.maximum(i - 1, 0), col))
    tab = [pl.BlockSpec((b, LANES), prev(0))] * 3 + [pl.BlockSpec((b, LANES), cur(0))] * 3
    (out,), sent = _call(
        "attention_fwd", body, (nb,),
        [pl.BlockSpec(memory_space=pltpu.SMEM),
         pl.BlockSpec((b, A_WIDTH), cur(0)),
         pl.BlockSpec((b, A_KV_WIDTH), prev(OFF_KA // A_KV_WIDTH)),
         pl.BlockSpec((b, A_KV_WIDTH), cur(OFF_KA // A_KV_WIDTH)),
         pl.BlockSpec((b, A_KV_WIDTH), prev(OFF_VA // A_KV_WIDTH)),
         pl.BlockSpec((b, A_KV_WIDTH), cur(OFF_VA // A_KV_WIDTH))] + tab,
        [pl.BlockSpec((b, A_WIDTH), cur(0))],
        [jax.ShapeDtypeStruct((t, A_WIDTH), BF16)], [],
        [sinks, proj, proj, proj, proj, proj, *tables, *tables], comm)
    return out, sent


def _attention_bwd(proj, tables, sinks, d_out, comm=None):
    t = proj.shape[0]
    nb = t // ATT_BLOCK
    b = ATT_BLOCK
    scale = A_HEAD_DIM ** -0.5

    def body(sink_ref, q_ref, kp_ref, kc_ref, vp_ref, vc_ref,
             cp_ref, ap_ref, bp_ref, cc_ref, ac_ref, bc_ref, do_ref,
             dq_ref, dk_ref, dv_ref, dsink_ref, dk_carry, dv_carry):
        step = pl.program_id(0)
        i = nb - 1 - step

        @pl.when(step == 0)
        def _():
            dk_carry[...] = jnp.zeros(dk_carry.shape, F32)
            dv_carry[...] = jnp.zeros(dv_carry.shape, F32)
            dsink_ref[...] = jnp.zeros(dsink_ref.shape, F32)

        tabs_p = (cp_ref[...], ap_ref[...], bp_ref[...])
        tabs_c = (cc_ref[...], ac_ref[...], bc_ref[...])
        kvar, vvar, allowed, halves = _att_common(
            i, kp_ref[...], kc_ref[...], vp_ref[...], vc_ref[...], tabs_p, tabs_c)
        lane = lax.broadcasted_iota(jnp.int32, (1, LANES), 1)
        dk_slab = [jnp.zeros((2 * b, LANES), F32) for _ in range(2)]
        dv_slab = [jnp.zeros((2 * b, LANES), F32) for _ in range(2)]
        dsink = jnp.zeros((1, LANES), F32)
        for pair in range(A_HEADS // 2):
            sl = slice(pair * LANES, (pair + 1) * LANES)
            q2 = _rope(q_ref[:, sl], *tabs_c) * scale
            do2 = do_ref[:, sl].astype(F32)
            dq2 = jnp.zeros((b, LANES), F32)
            for half in range(2):
                h = 2 * pair + half
                _, _, ks, var = _head_slots(h)
                qm = jnp.where(halves[half], q2, 0.0).astype(BF16)
                dom = jnp.where(halves[half], do2, 0.0).astype(BF16)
                p, p_sink = _att_head_probs(qm, kvar[ks][var], allowed, sink_ref[h])
                dp = lax.dot_general(dom, vvar[ks][var], NT, preferred_element_type=F32)
                dsum = jnp.sum(p * dp, axis=-1, keepdims=True)
                ds = (p * (dp - dsum)).astype(BF16)
                dsink = dsink + jnp.where(lane == h, -jnp.sum(p_sink * dsum), 0.0)
                dqh = lax.dot_general(ds, kvar[ks][var], NN, preferred_element_type=F32)
                dq2 = dq2 + jnp.where(halves[half], dqh, 0.0)
                dkx = lax.dot_general(ds, qm, TN, preferred_element_type=F32)
                dvx = lax.dot_general(p.astype(BF16), dom, TN, preferred_element_type=F32)
                if var:
                    dkx = pltpu.roll(dkx, 64, 1)
                    dvx = pltpu.roll(dvx, 64, 1)
                dk_slab[ks] = dk_slab[ks] + dkx
                dv_slab[ks] = dv_slab[ks] + dvx
            dq_ref[:, sl] = _rope_bwd(dq2 * scale, *tabs_c).astype(dq_ref.dtype)
        for ks in range(2):
            sl = slice(ks * LANES, (ks + 1) * LANES)
            dk_cur = dk_slab[ks][b:] + dk_carry[:, sl]
            dk_ref[:, sl] = _rope_bwd(dk_cur, *tabs_c).astype(dk_ref.dtype)
            dv_ref[:, sl] = (dv_slab[ks][b:] + dv_carry[:, sl]).astype(dv_ref.dtype)
            dk_carry[:, sl] = dk_slab[ks][:b]
            dv_carry[:, sl] = dv_slab[ks][:b]
        dsink_ref[...] += dsink

    cur = lambda col: (lambda s: (nb - 1 - s, col))
    prev = lambda col: (lambda s: (jnp.maximum(nb - 2 - s, 0), col))
    tab = [pl.BlockSpec((b, LANES), prev(0))] * 3 + [pl.BlockSpec((b, LANES), cur(0))] * 3
    return _call(
        "attention_bwd", body, (nb,),
        [pl.BlockSpec(memory_space=pltpu.SMEM),
         pl.BlockSpec((b, A_WIDTH), cur(0)),
         pl.BlockSpec((b, A_KV_WIDTH), prev(OFF_KA // A_KV_WIDTH)),
         pl.BlockSpec((b, A_KV_WIDTH), cur(OFF_KA // A_KV_WIDTH)),
         pl.BlockSpec((b, A_KV_WIDTH), prev(OFF_VA // A_KV_WIDTH)),
         pl.BlockSpec((b, A_KV_WIDTH), cur(OFF_VA // A_KV_WIDTH))] + tab
        + [pl.BlockSpec((b, A_WIDTH), cur(0))],
        [pl.BlockSpec((b, A_WIDTH), cur(0)),
         pl.BlockSpec((b, A_KV_WIDTH), cur(0)),
         pl.BlockSpec((b, A_KV_WIDTH), cur(0)),
         pl.BlockSpec((1, LANES), lambda s: (0, 0))],
        [jax.ShapeDtypeStruct((t, A_WIDTH), BF16),
         jax.ShapeDtypeStruct((t, A_KV_WIDTH), BF16),
         jax.ShapeDtypeStruct((t, A_KV_WIDTH), BF16),
         jax.ShapeDtypeStruct((1, LANES), F32)],
        [pltpu.VMEM((b, A_KV_WIDTH), F32), pltpu.VMEM((b, A_KV_WIDTH), F32)],
        [sinks, proj, proj, proj, proj, proj, *tables, *tables, d_out], comm)


HGRN_HEADS_PER_STEP = 4
HGRN_W = HGRN_HEADS_PER_STEP * B_HEAD
HGRN_ROWS = 4 * CHUNK


def _hgrn_gates(qp, fp, hlb):
    lb = _sig(hlb[0:1] - hlb[1:2])
    sg = _sig(fp)
    big_f = lb + (1.0 - lb) * sg
    kk = (1.0 - lb) * (1.0 - sg)
    sq = _sig(qp)
    return lb, sg, big_f, kk, sq


def _tri(lower):
    r = lax.broadcasted_iota(jnp.int32, (CHUNK, CHUNK), 0)
    c = lax.broadcasted_iota(jnp.int32, (CHUNK, CHUNK), 1)
    return (r >= c) if lower else (r <= c)


def _hgrn_factors(q_ref, k_ref, b_ref, sl, r0):
    out = []
    whole = slice(r0, r0 + CHUNK)
    bh = b_ref[whole, sl]
    for i in range(CHUNK // SUB):
        rows = slice(r0 + i * SUB, r0 + (i + 1) * SUB)
        b0 = b_ref[r0 + i * SUB - 1:r0 + i * SUB, sl] if i else jnp.zeros((1, B_HEAD), F32)
        eq = jnp.exp(b_ref[rows, sl] - b0)
        ek = jnp.exp(jnp.minimum(b0 - bh, MAX_DECAY_EXP))
        out.append((q_ref[rows, sl] * eq, k_ref[whole, sl] * ek, eq, ek))
    return out


def _hgrn_scores(factors):
    rows = [lax.dot_general(qd.astype(BF16), kx.astype(BF16), NT, preferred_element_type=F32)
            for qd, kx, _, _ in factors]
    return jnp.where(_tri(True), jnp.concatenate(rows, axis=0), 0.0)


def _hgrn_specs(steps, reverse):
    w = HGRN_W
    block = (lambda c: steps - 1 - c) if reverse else (lambda c: c)
    col = lambda off: pl.BlockSpec((HGRN_ROWS, w), lambda hh, c: (block(c), off // w + hh))
    return block, col


def _chunk_cumsum(log_f, r0, lower):
    tri = _tri(lower).astype(F32)
    return lax.dot_general(tri, log_f[r0:r0 + CHUNK], NN, precision=lax.Precision.HIGHEST,
                           preferred_element_type=F32)


def _hgrn_fwd(proj, hlb, wn, comm=None):
    t = proj.shape[0]
    nc = t // CHUNK
    steps = t // HGRN_ROWS
    w = HGRN_W
    block, col = _hgrn_specs(steps, False)

    def body(q_ref, f_ref, v_ref, og_ref, hlb_ref, wn_ref, ob_ref, oraw_ref, st_ref,
             state, qs, ks, bs):
        c = pl.program_id(1)

        @pl.when(c == 0)
        def _():
            state[...] = jnp.zeros(state.shape, F32)

        qp = q_ref[...]
        lb, sg, big_f, kk, sq = _hgrn_gates(qp, f_ref[...], hlb_ref[...])
        qs[...] = qp * sq
        ks[...] = kk
        log_f = jnp.log(big_f)
        for sub in range(HGRN_ROWS // CHUNK):
            r0 = sub * CHUNK
            rows = slice(r0, r0 + CHUNK)
            bs[rows] = _chunk_cumsum(log_f, r0, True)
            for h in range(HGRN_HEADS_PER_STEP):
                sl = slice(h * B_HEAD, (h + 1) * B_HEAD)
                bh = bs[rows, sl]
                qh = qs[rows, sl]
                kh = ks[rows, sl]
                vh = v_ref[rows, sl].astype(BF16)
                s0 = state[h]
                st_ref[sub, h] = s0.astype(st_ref.dtype)
                a = _hgrn_scores(_hgrn_factors(qs, ks, bs, sl, r0))
                o = lax.dot_general(a.astype(BF16), vh, NN, preferred_element_type=F32)
                o = o + lax.dot_general((qh * jnp.exp(bh)).astype(BF16), s0.astype(BF16), NT,
                                        preferred_element_type=F32)
                bl = bs[r0 + CHUNK - 1:r0 + CHUNK, sl]
                khat = (kh * jnp.exp(bl - bh)).astype(BF16)
                state[h] = s0 * jnp.exp(bl) + lax.dot_general(vh, khat, TN, preferred_element_type=F32)
                oraw_ref[rows, sl] = o
                r = lax.rsqrt(jnp.mean(o * o, axis=-1, keepdims=True) + EPS)
                og = og_ref[rows, sl]
                ob_ref[rows, sl] = (o * r * wn_ref[:, sl] * (og * _sig(og))).astype(ob_ref.dtype)

    vec = lambda rows: pl.BlockSpec((rows, w), lambda hh, c: (0, hh))
    tile = pl.BlockSpec((HGRN_ROWS, w), lambda hh, c: (c, hh))
    return _call(
        "hgrn_fwd", body, (B_WIDTH // w, steps),
        [col(OFF_QB), col(OFF_FB), col(OFF_IB), col(OFF_OG), vec(2), vec(1)],
        [tile, tile,
         pl.BlockSpec((HGRN_ROWS // CHUNK, HGRN_HEADS_PER_STEP, B_HEAD, B_HEAD), lambda hh, c: (c, hh, 0, 0))],
        [jax.ShapeDtypeStruct((t, B_WIDTH), BF16),
         jax.ShapeDtypeStruct((t, B_WIDTH), F32),
         jax.ShapeDtypeStruct((nc, B_WIDTH // B_HEAD, B_HEAD, B_HEAD), BF16)],
        [pltpu.VMEM((HGRN_HEADS_PER_STEP, B_HEAD, B_HEAD), F32)] + [pltpu.VMEM((HGRN_ROWS, w), F32)] * 3,
        [proj, proj, proj, proj, hlb, wn], comm)


def _hgrn_bwd(proj, hlb, wn, o_raw, states, d_out, comm=None):
    t = proj.shape[0]
    steps = t // HGRN_ROWS
    w = HGRN_W
    block, col = _hgrn_specs(steps, True)

    def body(q_ref, f_ref, v_ref, og_ref, hlb_ref, wn_ref, oraw_ref, st_ref, dout_ref,
             dq_ref, df_ref, dv_ref, dog_ref, dlb_ref, dwn_ref,
             dstate, qs, ks, bs, dbs, dks, dqs, dgs):
        c = pl.program_id(1)

        @pl.when(c == 0)
        def _():
            dstate[...] = jnp.zeros(dstate.shape, F32)
            dlb_ref[...] = jnp.zeros(dlb_ref.shape, F32)
            dwn_ref[...] = jnp.zeros(dwn_ref.shape, F32)

        qp = q_ref[...]
        lb, sg, big_f, kk, sq = _hgrn_gates(qp, f_ref[...], hlb_ref[...])
        qs[...] = qp * sq
        ks[...] = kk
        log_f = jnp.log(big_f)
        last_row = lax.broadcasted_iota(jnp.int32, (CHUNK, 1), 0) == CHUNK - 1
        for sub in reversed(range(HGRN_ROWS // CHUNK)):
            r0 = sub * CHUNK
            rows = slice(r0, r0 + CHUNK)
            bs[rows] = _chunk_cumsum(log_f, r0, True)
            for h in range(HGRN_HEADS_PER_STEP):
                sl = slice(h * B_HEAD, (h + 1) * B_HEAD)
                bh = bs[rows, sl]
                qh = qs[rows, sl]
                kh = ks[rows, sl]
                vh = v_ref[rows, sl].astype(BF16)
                s0 = st_ref[sub, h]
                ds1 = dstate[h]
                o = oraw_ref[rows, sl]
                og = og_ref[rows, sl]
                wnh = wn_ref[:, sl]
                sog = _sig(og)
                r = lax.rsqrt(jnp.mean(o * o, axis=-1, keepdims=True) + EPS)
                dob = dout_ref[rows, sl].astype(F32)
                don = dob * (og * sog)
                dog_ref[rows, sl] = (dob * (o * r * wnh) * (sog * (1.0 + og * (1.0 - sog)))).astype(dog_ref.dtype)
                dwn_ref[:, sl] += jnp.sum(don * o * r, axis=0, keepdims=True)
                gp = don * wnh
                do = r * (gp - o * (r * r) * jnp.mean(gp * o, axis=-1, keepdims=True))
                do16 = do.astype(BF16)
                factors = _hgrn_factors(qs, ks, bs, sl, r0)
                a = _hgrn_scores(factors)
                da = jnp.where(_tri(True), lax.dot_general(do16, vh, NT, preferred_element_type=F32), 0.0)
                dv = lax.dot_general(a.astype(BF16), do16, TN, preferred_element_type=F32)
                dq_rows = []
                dk = jnp.zeros((CHUNK, B_HEAD), F32)
                for i, (qd, kx, eq, ek) in enumerate(factors):
                    da_i = da[i * SUB:(i + 1) * SUB]
                    dq_rows.append(lax.dot_general(da_i, kx, NN, precision=lax.Precision.HIGHEST,
                                                   preferred_element_type=F32) * eq)
                    dk = dk + lax.dot_general(da_i, qd, TN, precision=lax.Precision.HIGHEST,
                                              preferred_element_type=F32) * ek
                eb = jnp.exp(bh)
                bl = bs[r0 + CHUNK - 1:r0 + CHUNK, sl]
                ebl = jnp.exp(bl)
                ekl = jnp.exp(bl - bh)
                qe = (qh * eb).astype(BF16)
                khat = (kh * ekl).astype(BF16)
                ds1_16 = ds1.astype(BF16)
                dq = jnp.concatenate(dq_rows, axis=0) + eb * lax.dot_general(
                    do16, s0, NN, preferred_element_type=F32)
                dv = dv + lax.dot_general(khat, ds1_16, NT, preferred_element_type=F32)
                dk_state = ekl * lax.dot_general(vh, ds1_16, NN, preferred_element_type=F32)
                dk = dk + dk_state
                db_last = (jnp.sum(kh * dk_state, axis=0, keepdims=True)
                           + ebl * jnp.sum(s0.astype(F32) * ds1, axis=0, keepdims=True))
                dbs[rows, sl] = qh * dq - kh * dk + jnp.where(last_row, db_last, 0.0)
                dks[rows, sl] = dk
                dqs[rows, sl] = dq
                dstate[h] = ds1 * ebl + lax.dot_general(do16, qe, TN, preferred_element_type=F32)
                dv_ref[rows, sl] = dv.astype(dv_ref.dtype)
            dgs[rows] = _chunk_cumsum(dbs, r0, False)
        dq_ref[...] = (dqs[...] * (sq * (1.0 + qp * (1.0 - sq)))).astype(dq_ref.dtype)
        df_minus_dk = dgs[...] / big_f - dks[...]
        df_ref[...] = ((1.0 - lb) * df_minus_dk * sg * (1.0 - sg)).astype(df_ref.dtype)
        dlb_ref[...] += jnp.sum((1.0 - sg) * df_minus_dk, axis=0, keepdims=True) * (lb * (1.0 - lb))

    vec = lambda rows: pl.BlockSpec((rows, w), lambda hh, c: (0, hh))
    blk = pl.BlockSpec((HGRN_ROWS, w), lambda hh, c: (block(c), hh))
    act = jax.ShapeDtypeStruct((t, B_WIDTH), BF16)
    row = jax.ShapeDtypeStruct((1, B_WIDTH), F32)
    return _call(
        "hgrn_bwd", body, (B_WIDTH // w, steps),
        [col(OFF_QB), col(OFF_FB), col(OFF_IB), col(OFF_OG), vec(2), vec(1), blk,
         pl.BlockSpec((HGRN_ROWS // CHUNK, HGRN_HEADS_PER_STEP, B_HEAD, B_HEAD),
                      lambda hh, c: (block(c), hh, 0, 0)),
         blk],
        [blk, blk, blk, blk, vec(1), vec(1)],
        [act, act, act, act, row, row],
        [pltpu.VMEM((HGRN_HEADS_PER_STEP, B_HEAD, B_HEAD), F32)] + [pltpu.VMEM((HGRN_ROWS, w), F32)] * 7,
        [proj, proj, proj, proj, hlb, wn, o_raw, states, d_out], comm)


def _mixer_fwd(x1, mix_norm, w_in, sinks, hlb, wn, weights, tables, comms):
    t, d = x1.shape
    n_in = w_in[0].shape[0]
    nb = d // N_DEV
    sent = {}
    h2 = _rmsnorm("mix_norm", x1, mix_norm)
    tm = _tile(t, 1024, 16)
    tn = _tile(n_in, 512, 128)
    (proj,), sent["proj"] = _mmc(
        "mix_proj", (t // tm, n_in // tn, 1),
        [(h2, (tm, d // 2), lambda i, j, k: (i, 0)), (w_in[0], (tn, d // 2), lambda i, j, k: (j, 0)),
         (h2, (tm, d // 2), lambda i, j, k: (i, 1)), (w_in[1], (tn, d // 2), lambda i, j, k: (j, 0))],
        [(0, 1, 0, NT), (2, 3, 0, NT)], [(tm, tn)], lambda accs, refs, rows: accs,
        [((t, n_in), F32, (tm, tn), lambda i, j, k: (i, j), False)], comm=comms.get("proj"))
    out_a, sent["attention"] = _attention_fwd(proj, tables, sinks, comms.get("attention"))
    (out_b, o_raw, states), sent["hgrn"] = _hgrn_fwd(proj, hlb, wn, comms.get("hgrn"))
    wua, wub, w_out = weights(sent)

    def epi_merge(accs, refs, rows):
        ua, ub = accs
        return [_sig(refs[4][rows]) * ua + _sig(refs[5][rows]) * ub, ua, ub]

    wide = ((t, d), BF16, (tm, nb), lambda i, j, k: (i, j), False)
    (merged, ua, ub), sent["merge"] = _mmc(
        "mix_merge", (t // tm, N_DEV, 1),
        [(out_a, (tm, A_WIDTH), lambda i, j, k: (i, 0)),
         (wua, (None, A_WIDTH, nb), lambda i, j, k: (j, 0, 0)),
         (out_b, (tm, B_WIDTH), lambda i, j, k: (i, 0)),
         (wub, (None, B_WIDTH, nb), lambda i, j, k: (j, 0, 0)),
         (proj, (tm, nb), lambda i, j, k: (i, OFF_GA // nb + j)),
         (proj, (tm, nb), lambda i, j, k: (i, (OFF_GA + d) // nb + j))],
        [(0, 1, 0, NN), (2, 3, 1, NN)], [(tm, nb)] * 2, epi_merge, [wide] * 3, comm=comms.get("merge"))

    tn2 = _tile(d, 1024, 128)
    (x2,) = _mm(
        "mix_out", (t // tm, d // tn2, 1),
        [(merged, (tm, d), lambda i, j, k: (i, 0)), (w_out, (d, tn2), lambda i, j, k: (0, j)),
         (x1, (tm, tn2), lambda i, j, k: (i, j))],
        [(0, 1, 0, NN)], [(tm, tn2)], lambda accs, refs, rows: [refs[2][rows] + accs[0]],
        [((t, d), F32, (tm, tn2), lambda i, j, k: (i, j), False)])
    saved = dict(h2=h2, proj=proj, out_a=out_a, out_b=out_b, o_raw=o_raw, states=states,
                 merged=merged, ua=ua, ub=ub)
    return x2, saved, sent


def _mixer_bwd(dx2, dx2_16, x1, mix_norm, w_in, sinks, hlb, wn, wua, wub, w_out, tables, sv, comms):
    t, d = x1.shape
    sent = {}
    avail = dict(sent=sent)
    n_in = w_in[0].shape[0]
    nb = d // N_DEV
    proj = sv["proj"]
    tm = _tile(t, 1024, 16)
    tn = _tile(d, 512, 128)

    def epi_gate(accs, refs, rows):
        dm = accs[0]
        sa = _sig(refs[2][rows])
        sb = _sig(refs[3][rows])
        ua = refs[4][rows].astype(F32)
        ub = refs[5][rows].astype(F32)
        return [dm * sa, dm * sb, dm * ua * sa * (1.0 - sa), dm * ub * sb * (1.0 - sb)]

    wide = ((t, d), BF16, (tm, tn), lambda i, j, k: (i, j), False)
    dua, dub, dga, dgb = _mm(
        "mix_dmerge", (t // tm, d // tn, 1),
        [(dx2_16, (tm, d), lambda i, j, k: (i, 0)),
         (w_out, (tn, d), lambda i, j, k: (j, 0)),
         (proj, (tm, tn), lambda i, j, k: (i, OFF_GA // tn + j)),
         (proj, (tm, tn), lambda i, j, k: (i, (OFF_GA + d) // tn + j)),
         (sv["ua"], (tm, tn), lambda i, j, k: (i, j)),
         (sv["ub"], (tm, tn), lambda i, j, k: (i, j))],
        [(0, 1, 0, NT)], [(tm, tn)], epi_gate, [wide] * 4)

    tk = _tile(t, 1024, 16)
    two = _tile(d, 1024, 128)
    (dw_out,) = _mm(
        "mix_dwout", (d // two, 1, t // tk),
        [(sv["merged"], (tk, two), lambda i, j, k: (k, i)), (dx2_16, (tk, d), lambda i, j, k: (k, 0))],
        [(0, 1, 0, TN)], [(two, d)], lambda accs, refs, rows: accs,
        [((d, d), BF16, (two, d), lambda i, j, k: (i, 0), False)])

    w_spec = lambda rows: ((N_DEV, rows, nb), BF16, (None, rows, nb), lambda i, j, k: (j, 0, 0), False)
    tk2 = _tile(t, 2048, 16)
    dwua, dwub = _mm(
        "mix_dwup", (1, N_DEV, t // tk2),
        [(sv["out_a"], (tk2, A_WIDTH), lambda i, j, k: (k, 0)),
         (dua, (tk2, nb), lambda i, j, k: (k, j)),
         (sv["out_b"], (tk2, B_WIDTH), lambda i, j, k: (k, 0)),
         (dub, (tk2, nb), lambda i, j, k: (k, j))],
        [(0, 1, 0, TN), (2, 3, 1, TN)], [(A_WIDTH, nb), (B_WIDTH, nb)], lambda accs, refs, rows: accs,
        [w_spec(A_WIDTH), w_spec(B_WIDTH)])

    d_out_a, d_out_b = _mm(
        "mix_dup", (t // tm, 1, N_DEV),
        [(dua, (tm, nb), lambda i, j, k: (i, k)),
         (wua, (None, A_WIDTH, nb), lambda i, j, k: (k, 0, 0)),
         (dub, (tm, nb), lambda i, j, k: (i, k)),
         (wub, (None, B_WIDTH, nb), lambda i, j, k: (k, 0, 0))],
        [(0, 1, 0, NT), (2, 3, 1, NT)], [(tm, A_WIDTH), (tm, B_WIDTH)], lambda accs, refs, rows: accs,
        [((t, A_WIDTH), BF16, (tm, A_WIDTH), lambda i, j, k: (i, 0), False),
         ((t, B_WIDTH), BF16, (tm, B_WIDTH), lambda i, j, k: (i, 0), False)])

    avail.update(w_out=dw_out.reshape(N_DEV, d // N_DEV, d), w_up_a=dwua, w_up_b=dwub)
    (dq_a, dk_a, dv_a, dsinks), sent["attention"] = _attention_bwd(
        proj, tables, sinks, d_out_a, _carried(comms, "attention", avail))
    (dq_b, df_b, di_b, dog_b, dlb, dwn), sent["hgrn"] = _hgrn_bwd(
        proj, hlb, wn, sv["o_raw"], sv["states"], d_out_b, _carried(comms, "hgrn", avail))
    dproj = jnp.concatenate([dq_a, dk_a, dv_a, dq_b, df_b, di_b, dog_b, dga, dgb], axis=1)

    tdm = _tile(d, 1024, 128)
    twn = _tile(n_in, 2432, 128)
    tkw = _tile(t, 512, 16)
    (dw_in,) = _mm(
        "mix_dwin", (n_in // twn, d // tdm, t // tkw),
        [(dproj, (tkw, twn), lambda i, j, k: (k, i)), (sv["h2"], (tkw, tdm), lambda i, j, k: (k, j))],
        [(0, 1, 0, TN)], [(twn, tdm)], lambda accs, refs, rows: accs,
        [((n_in, d), BF16, (twn, tdm), lambda i, j, k: (i, j), False)])

    avail.update(w_in=dw_in.reshape(N_DEV, n_in // N_DEV, d))
    tm3 = _tile(t, 512, 16)
    tpn = _tile(n_in, 512, 128)

    def epi_dx(accs, refs, rows):
        dh = jnp.concatenate(accs, axis=1)
        dx, dw = _rms_bwd(dh, refs[3][rows], refs[4][...], refs[5][rows])
        return [dx, dx, dw]

    (dx1, dx1_16, dmix), sent["dx"] = _mmc(
        "mix_dx", (t // tm3, 1, n_in // tpn),
        [(dproj, (tm3, tpn), lambda i, j, k: (i, k)),
         (w_in[0], (tpn, d // 2), lambda i, j, k: (k, 0)), (w_in[1], (tpn, d // 2), lambda i, j, k: (k, 0)),
         (x1, (tm3, d), lambda i, j, k: (i, 0)), (mix_norm, (1, d), lambda i, j, k: (0, 0)),
         (dx2, (tm3, d), lambda i, j, k: (i, 0))],
        [(0, 1, 0, NN), (0, 2, 1, NN)], [(tm3, d // 2)] * 2, epi_dx,
        [((t, d), F32, (tm3, d), lambda i, j, k: (i, 0), False),
         ((t, d), BF16, (tm3, d), lambda i, j, k: (i, 0), False),
         ((1, d), F32, (1, d), lambda i, j, k: (0, 0), True)],
        comm=_carried(comms, "dx", avail))
    small = dict(mix_norm=dmix, attn_sinks=dsinks, hgrn_lb=dlb, hgrn_norm=dwn)
    return dx1, dx1_16, small, avail


def _ple_fwd(x3, ple_norm, p16, wpg, wpp):
    t, d = x3.shape
    nb = d // N_DEV
    h4 = _rmsnorm("ple_norm", x3, ple_norm)
    tm = _tile(t, 1024, 16)

    def epi(accs, refs, rows):
        zg, pp = accs
        return [refs[4][rows] + _sig(zg) * pp, zg, pp]

    blk = lambda dt: ((t, d), dt, (tm, nb), lambda i, j, k: (i, j), False)
    x4, zg, pp = _mm(
        "ple_fwd", (t // tm, N_DEV, 1),
        [(h4, (tm, d), lambda i, j, k: (i, 0)), (wpg, (d, nb), lambda i, j, k: (0, j)),
         (p16, (tm, PLE_DIM), lambda i, j, k: (i, 0)),
         (wpp, (None, PLE_DIM, nb), lambda i, j, k: (j, 0, 0)),
         (x3, (tm, nb), lambda i, j, k: (i, j))],
        [(0, 1, 0, NN), (2, 3, 1, NN)], [(tm, nb)] * 2, epi, [blk(F32), blk(BF16), blk(BF16)])
    return x4, dict(h4=h4, zg=zg, pp=pp)


def _ple_bwd(dx4, x3, ple_norm, p16, wpg, sv):
    t, d = x3.shape
    nb = d // N_DEV

    def gate_grads(dx, zg, pp):
        s = _sig(zg.astype(F32))
        return dx * pp.astype(F32) * s * (1.0 - s), dx * s

    dzg, dpp = _rows("ple_dgate", gate_grads, [(dx4, True), (sv["zg"], True), (sv["pp"], True)],
                     [((t, d), BF16), ((t, d), BF16)], _tile(t, 512, 16))
    (dwpp,) = _mm(
        "ple_dwproj", (1, N_DEV, 1),
        [(p16, (t, PLE_DIM), lambda i, j, k: (0, 0)), (dpp, (t, nb), lambda i, j, k: (0, j))],
        [(0, 1, 0, TN)], [(PLE_DIM, nb)], lambda accs, refs, rows: accs,
        [((N_DEV, PLE_DIM, nb), BF16, (None, PLE_DIM, nb), lambda i, j, k: (j, 0, 0), False)])
    tk = _tile(t, 1024, 16)
    tn = _tile(d, 1024, 128)
    (dwpg,) = _mm(
        "ple_dwgate", (d // tn, 1, t // tk),
        [(sv["h4"], (tk, tn), lambda i, j, k: (k, i)), (dzg, (tk, d), lambda i, j, k: (k, 0))],
        [(0, 1, 0, TN)], [(tn, d)], lambda accs, refs, rows: accs,
        [((d, d), BF16, (tn, d), lambda i, j, k: (i, 0), False)])
    tm3 = _tile(t, 256, 16)

    def epi_dx(accs, refs, rows):
        dx, dw = _rms_bwd(accs[0], refs[2][rows], refs[3][...], refs[4][rows])
        return [dx, dx, dw]

    dx3, dx3_16, dnorm = _mm(
        "ple_dx", (t // tm3, 1, 1),
        [(dzg, (tm3, d), lambda i, j, k: (i, 0)), (wpg, (d, d), lambda i, j, k: (0, 0)),
         (x3, (tm3, d), lambda i, j, k: (i, 0)), (ple_norm, (1, d), lambda i, j, k: (0, 0)),
         (dx4, (tm3, d), lambda i, j, k: (i, 0))],
        [(0, 1, 0, NT)], [(tm3, d)], epi_dx,
        [((t, d), F32, (tm3, d), lambda i, j, k: (i, 0), False),
         ((t, d), BF16, (tm3, d), lambda i, j, k: (i, 0), False),
         ((1, d), F32, (1, d), lambda i, j, k: (0, 0), True)])
    return dx3, dx3_16, dwpp, dwpg, dnorm


def _loss_head(x4, final_norm, target):
    t, d = x4.shape
    tm = _tile(t, 256, 8)

    def body(x_ref, w_ref, tgt_ref, dx_ref, dw_ref, loss_ref):
        i = pl.program_id(0)
        xv = x_ref[...]
        wv = w_ref[...]
        r = lax.rsqrt(jnp.mean(xv * xv, axis=-1, keepdims=True) + EPS)
        err = xv * r * wv - tgt_ref[...]
        part = 0.5 * jnp.sum(jnp.mean(err * err, axis=-1, keepdims=True), axis=0, keepdims=True)
        dx, dw = _rms_bwd(err * (1.0 / d), xv, wv, jnp.zeros_like(xv))
        dx_ref[...] = dx

        @pl.when(i == 0)
        def _():
            dw_ref[...] = dw
            loss_ref[...] = jnp.broadcast_to(part, loss_ref.shape)

        @pl.when(i > 0)
        def _():
            dw_ref[...] += dw
            loss_ref[...] += jnp.broadcast_to(part, loss_ref.shape)

    return pl.pallas_call(
        body,
        name="loss_head",
        grid=(t // tm,),
        in_specs=[pl.BlockSpec((tm, d), lambda i: (i, 0)), pl.BlockSpec((1, d), lambda i: (0, 0)),
                  pl.BlockSpec((tm, d), lambda i: (i, 0))],
        out_specs=[pl.BlockSpec((tm, d), lambda i: (i, 0)), pl.BlockSpec((1, d), lambda i: (0, 0)),
                   pl.BlockSpec((1, LANES), lambda i: (0, 0))],
        out_shape=[jax.ShapeDtypeStruct((t, d), F32), jax.ShapeDtypeStruct((1, d), F32),
                   jax.ShapeDtypeStruct((1, LANES), F32)],
        compiler_params=_params(("arbitrary",)),
    )(x4, final_norm, target)


RELATIONS = ((0, 0), (1, 0), (0, 1), (1, 1))


def _place():
    return lax.axis_index("x"), lax.axis_index("y"), lax.axis_index("c")


def _flip(v, bit):
    return 1 - v if bit else v


def _gather_program(shards, mid_at):
    n = len(shards)
    slots = 8

    def copies(ins, outs, send_sems, recv_sems, local_sems, base):
        x, y, c = _place()
        sibling, x_chip, y_chip = (x, y, 1 - c), (1 - x, y, c), (x, 1 - y, c)

        def block(a, r, core, half=None):
            ref = outs[a].at[4 * _flip(x, RELATIONS[r][0]) + 2 * _flip(y, RELATIONS[r][1]) + core]
            rows = shards[a].shape[0] // 2
            return ref if half is None else ref.at[pl.ds(half * rows, rows)]

        def copy(a, slot, dst, to, src=None):
            return pltpu.make_async_remote_copy(
                src_ref=dst if src is None else src, dst_ref=dst,
                send_sem=send_sems.at[base[0] + slots * a + slot],
                recv_sem=recv_sems.at[base[1] + slots * a + slot], device_id=to, device_id_type=MESH)

        own = [pltpu.make_async_copy(ins[a], block(a, 0, c), local_sems.at[base[2] + a]) for a in range(n)]
        sent_first = [cp for a in range(n) for cp in (
            copy(a, 0, block(a, 0, c), sibling, src=ins[a]),
            copy(a, 1, block(a, 0, c), x_chip, src=ins[a]),
            copy(a, 2, block(a, 0, c), y_chip, src=ins[a]))]
        landed_first = [cp for a in range(n) for cp in (
            copy(a, 1, block(a, 1, c), x_chip), copy(a, 2, block(a, 2, c), y_chip))]
        sent_mid = [cp for a in range(n) for cp in (
            copy(a, 3, block(a, 1, c, 0), y_chip), copy(a, 4, block(a, 2, c, 1), x_chip),
            copy(a, 5, block(a, 1, c), sibling), copy(a, 6, block(a, 2, c), sibling))]
        landed_mid = [cp for a in range(n) for cp in (
            copy(a, 3, block(a, 3, c, 0), y_chip), copy(a, 4, block(a, 3, c, 1), x_chip))]
        sent_last = [copy(a, 7, block(a, 3, c), sibling) for a in range(n)]
        landed_sibling = [cp for a in range(n) for cp in (
            copy(a, 0, block(a, 0, 1 - c), sibling), copy(a, 5, block(a, 1, 1 - c), sibling),
            copy(a, 6, block(a, 2, 1 - c), sibling), copy(a, 7, block(a, 3, 1 - c), sibling))]
        return own, sent_first, landed_first, sent_mid, landed_mid, sent_last, landed_sibling

    def pre(*args):
        own, sent_first = copies(*args)[:2]
        for cp in own + sent_first:
            cp.start()

    def mid(*args):
        _, _, landed_first, sent_mid, _, _, _ = copies(*args)
        for cp in landed_first:
            cp.wait_recv()
        for cp in sent_mid:
            cp.start()

    def late(*args):
        _, _, _, _, landed_mid, sent_last, _ = copies(*args)
        for cp in landed_mid:
            cp.wait_recv()
        for cp in sent_last:
            cp.start()

    def post(*args):
        own, sent_first, _, sent_mid, _, sent_last, landed_sibling = copies(*args)
        for cp in landed_sibling:
            cp.wait_recv()
        for cp in sent_first + sent_mid + sent_last:
            cp.wait_send()
        for mine in own:
            mine.wait()

    return _Comm(list(shards), [jax.ShapeDtypeStruct((N_DEV,) + s.shape, s.dtype) for s in shards],
                 (slots * n, slots * n, n), pre, post, mid, mid_at, late)


def _exchange_program(arrays, n_slots, source, target, slot):
    n = len(arrays)

    def copies(ins, outs, send_sems, recv_sems, local_sems, base):
        place = _place()
        res = []
        for a in range(n):
            for r in range(n_slots):
                src = source(r, place)
                if src is None:
                    continue
                res.append(pltpu.make_async_remote_copy(
                    src_ref=ins[a].at[src], dst_ref=outs[a].at[slot(r)],
                    send_sem=send_sems.at[base[0] + n_slots * a + r],
                    recv_sem=recv_sems.at[base[1] + n_slots * a + r],
                    device_id=target(r, place), device_id_type=MESH))
        return res

    def pre(*args):
        for cp in copies(*args):
            cp.start()

    def post(*args):
        for cp in copies(*args):
            cp.wait()

    n_out = len({slot(r) for r in range(n_slots) if source(r, (0, 0, 0)) is not None})
    return _Comm(list(arrays), [jax.ShapeDtypeStruct((n_out,) + g.shape[1:], g.dtype) for g in arrays],
                 (n_slots * n, n_slots * n, 0), pre, post)


def _sibling_program(grads):
    def source(r, place):
        x, y, c = place
        return 4 * _flip(x, RELATIONS[r][0]) + 2 * _flip(y, RELATIONS[r][1]) + (1 - c)

    return _exchange_program(grads, 4, source, lambda r, p: (p[0], p[1], 1 - p[2]), lambda r: r)


def _chips_program(sums, relations=(1, 2, 3)):
    def target(r, place):
        x, y, c = place
        return (_flip(x, RELATIONS[r][0]), _flip(y, RELATIONS[r][1]), c)

    return _exchange_program(sums, 4, lambda r, p: r if r in relations else None, target,
                             lambda r: relations.index(r) if r in relations else 0)


def _standalone(name, comm):
    return _call(name, lambda: None, (1,), [], [], [], [], [], comm)[1]


def _pair_sum(name, grad, from_sibling, block_ids):
    _, rows, cols = grad.shape
    tr = _tile(rows, 512, 16)

    def body(ids_ref, mine_ref, sib_ref, o_ref):
        o_ref[...] = (mine_ref[...].astype(F32) + sib_ref[...].astype(F32)).astype(o_ref.dtype)

    return pl.pallas_call(
        body,
        name=name,
        grid_spec=pltpu.PrefetchScalarGridSpec(
            num_scalar_prefetch=1,
            grid=(4, rows // tr),
            in_specs=[pl.BlockSpec((None, tr, cols), lambda r, i, ids: (ids[r], i, 0)),
                      pl.BlockSpec((None, tr, cols), lambda r, i, ids: (r, i, 0))],
            out_specs=pl.BlockSpec((None, tr, cols), lambda r, i, ids: (r, i, 0))),
        out_shape=jax.ShapeDtypeStruct((4, rows, cols), BF16),
        compiler_params=_params(("arbitrary", "arbitrary")),
    )(block_ids, grad, from_sibling)


def _adam(w, g, m, v):
    m = ADAM_B1 * m + (1.0 - ADAM_B1) * g
    v = ADAM_B2 * v + (1.0 - ADAM_B2) * (g * g)
    m_hat = m / (1.0 - ADAM_B1 ** ADAM_STEP)
    v_hat = v / (1.0 - ADAM_B2 ** ADAM_STEP)
    delta = -ADAM_LR * (m_hat / (jnp.sqrt(v_hat) + ADAM_EPS) + ADAM_WD * w)
    return delta, m, v


def _adam_sharded(name, items, comm=None):
    _, rows, cols = items[0][0].shape
    n = len(items)
    tr = _tile(rows, max(64, 256 // n), 8)

    def body(*refs):
        for k in range(n):
            w_ref, m_ref, v_ref, s_ref, r1_ref, r2_ref, r3_ref = refs[7 * k:7 * k + 7]
            g_ref, d_ref, nm_ref, nv_ref = refs[7 * n + 4 * k:7 * n + 4 * k + 4]
            g = ((s_ref[...].astype(F32) + r1_ref[...].astype(F32)) + r2_ref[...].astype(F32)) + r3_ref[...].astype(F32)
            delta, nm, nv = _adam(w_ref[...], g, m_ref[...], v_ref[...])
            g_ref[...] = g
            d_ref[...] = delta
            nm_ref[...] = nm
            nv_ref[...] = nv

    slot = lambda s: pl.BlockSpec((None, tr, cols), lambda i: (s, i, 0))
    flat = slot(0)
    args, specs = [], []
    for w, m, v, sums, from_chips in items:
        args += [w, m, v, sums] + [a for a, _ in from_chips]
        specs += [flat, flat, flat, slot(0)] + [slot(s) for _, s in from_chips]
    res, sent = _call(
        name, body, (rows // tr,), specs, [flat] * (4 * n),
        [jax.ShapeDtypeStruct((1, rows, cols), F32)] * (4 * n), [], args, comm)
    return [res[4 * k:4 * k + 4] for k in range(n)], sent


def _all_reduce_small(part):
    rows = part.shape[0]

    def body(x_ref, o_ref, gathered, send_sems, recv_sems):
        x, y, c = _place()
        me = 4 * x + 2 * y + c
        gathered[me] = x_ref[...]
        copies = []
        for k in range(1, N_DEV):
            bits = (k >> 2 & 1, k >> 1 & 1, k & 1)
            peer = (_flip(x, bits[0]), _flip(y, bits[1]), _flip(c, bits[2]))
            copies.append(pltpu.make_async_remote_copy(
                src_ref=x_ref, dst_ref=gathered.at[me],
                send_sem=send_sems.at[k - 1], recv_sem=recv_sems.at[k - 1],
                device_id=peer, device_id_type=MESH))
        for cp in copies:
            cp.start()
        for cp in copies:
            cp.wait()
        total = gathered[0]
        for b in range(1, N_DEV):
            total = total + gathered[b]
        o_ref[...] = total

    return pl.pallas_call(
        body,
        name="small_all_reduce",
        in_specs=[pl.BlockSpec(memory_space=pltpu.VMEM)],
        out_specs=pl.BlockSpec(memory_space=pltpu.VMEM),
        out_shape=jax.ShapeDtypeStruct((rows, LANES), F32),
        scratch_shapes=[pltpu.VMEM((N_DEV, rows, LANES), F32),
                        pltpu.SemaphoreType.DMA((N_DEV - 1,)), pltpu.SemaphoreType.DMA((N_DEV - 1,))],
    )(part)


def _adam_small(w, g, m, v):
    def fn(wv, gv, mv, vv):
        return _adam(wv, gv, mv, vv)

    shape = (w.shape, F32)
    return _rows("adam_small", fn, [(w, True), (g, True), (m, True), (v, True)], [shape] * 3, w.shape[0])


SMALL = ("ffn1_norm", "mix_norm", "attn_sinks", "hgrn_lower_bound", "hgrn_norm", "ffn2_norm", "ple_norm",
         "final_norm")
LARGE = ("ffn1_w_gate", "ffn1_w_up", "ffn1_w_down", "w_in", "w_up_a", "w_up_b", "w_out",
         "ffn2_w_gate", "ffn2_w_up", "ffn2_w_down", "ple_w_gate", "ple_w_proj")
TRANSPOSED = ("ffn1_w_gate", "ffn1_w_up", "ffn2_w_gate", "ffn2_w_up", "w_in")
ADAM_GROUPS = (("ffn2_w_down", "ffn2_w_gate", "ffn2_w_up"), ("ple_w_gate", "w_out"), ("w_up_a", "w_up_b"),
               ("ple_w_proj",), ("w_in",), ("ffn1_w_down", "ffn1_w_gate", "ffn1_w_up"))
WEIGHTS = ("ffn1_norm", "ffn1_w_gate", "ffn1_w_up", "ffn1_w_down", "mix_norm", "w_in", "attn_sinks",
           "hgrn_lower_bound", "hgrn_norm", "w_up_a", "w_up_b", "w_out", "ffn2_norm", "ffn2_w_gate",
           "ffn2_w_up", "ffn2_w_down", "ple_norm", "ple_w_gate", "ple_w_proj", "final_norm")


def _pack_rows(arrays):
    rows = []
    for a in arrays:
        flat = a.reshape(-1).astype(F32)
        pad = -flat.shape[0] % LANES
        rows.append(jnp.pad(flat, (0, pad)).reshape(-1, LANES))
    packed = jnp.concatenate(rows, axis=0)
    return jnp.pad(packed, ((0, -packed.shape[0] % 8), (0, 0)))


def _unpack_rows(packed, like):
    out, at = [], 0
    for a in like:
        size = a.size
        n_rows = -(-size // LANES)
        out.append(packed[at:at + n_rows].reshape(-1)[:size].reshape(a.shape))
        at += n_rows
    return out


def kernel(x, p, positions, ffn1_norm, ffn1_w_gate, ffn1_w_up, ffn1_w_down, mix_norm, w_in, attn_sinks, hgrn_lower_bound, hgrn_norm, w_up_a, w_up_b, w_out, ffn2_norm, ffn2_w_gate, ffn2_w_up, ffn2_w_down, ple_norm, ple_w_gate, ple_w_proj, final_norm, loss_target, m_ffn1_norm, m_ffn1_w_gate, m_ffn1_w_up, m_ffn1_w_down, m_mix_norm, m_w_in, m_attn_sinks, m_hgrn_lower_bound, m_hgrn_norm, m_w_up_a, m_w_up_b, m_w_out, m_ffn2_norm, m_ffn2_w_gate, m_ffn2_w_up, m_ffn2_w_down, m_ple_norm, m_ple_w_gate, m_ple_w_proj, m_final_norm, v_ffn1_norm, v_ffn1_w_gate, v_ffn1_w_up, v_ffn1_w_down, v_mix_norm, v_w_in, v_attn_sinks, v_hgrn_lower_bound, v_hgrn_norm, v_w_up_a, v_w_up_b, v_w_out, v_ffn2_norm, v_ffn2_w_gate, v_ffn2_w_up, v_ffn2_w_down, v_ple_norm, v_ple_w_gate, v_ple_w_proj, v_final_norm):
    given = dict(locals())
    w = {n: given[n] for n in WEIGHTS}
    mom = {n: given["m_" + n] for n in WEIGHTS}
    var = {n: given["v_" + n] for n in WEIGHTS}
    t, d = x.shape[1], x.shape[2]

    x0, p16, target = x[0], p[0, 0].astype(BF16), loss_target[0]
    tables = _rope_tables(positions)
    sinks = attn_sinks.reshape(-1)
    final_w = final_norm.reshape(1, d)
    as_stored = lambda n, a: jnp.swapaxes(a, 1, 2) if n in TRANSPOSED else a
    shard = {n: as_stored(n, w[n])[0].astype(BF16) for n in LARGE}
    shard["w_in_left"], shard["w_in_right"] = shard["w_in"][:, :d // 2], shard["w_in"][:, d // 2:]
    xi, yi, ci = _place()
    block_ids = jnp.stack([4 * _flip(xi, rx) + 2 * _flip(yi, ry) + ci for rx, ry in RELATIONS]).astype(jnp.int32)

    def gather(*names):
        return _gather_program([shard[n] for n in names], 0.65)

    wg1, wu1 = _standalone("gather_ffn1", gather("ffn1_w_gate", "ffn1_w_up"))
    h1 = _rmsnorm("ffn1_norm", x0, ffn1_norm)
    (g1, u1, a1), (wd1, win_left) = _ffn_up("ffn1", h1, wg1, wu1, gather("ffn1_w_down", "w_in_left"))
    x1, (win_right,) = _ffn_down("ffn1", x0, a1, wd1, gather("w_in_right"))
    win = (win_left.reshape(-1, d // 2), win_right.reshape(-1, d // 2))
    mix_args = (mix_norm, win, sinks, hgrn_lower_bound, hgrn_norm)

    def mixer_weights(sent):
        wua, wub, wout_g, _ = sent["proj"]
        return wua, wub, wout_g.reshape(d, d)

    x2, mix_saved, got_w = _mixer_fwd(
        x1, *mix_args, mixer_weights, tables,
        dict(proj=gather("w_up_a", "w_up_b", "w_out", "ffn2_w_gate"), attention=gather("ffn2_w_up"),
             hgrn=gather("ple_w_gate", "ple_w_proj")))
    wua, wub, wout = mixer_weights(got_w)
    wg2, (wu2,) = got_w["proj"][3], got_w["attention"]
    wpg_g, wpp = got_w["hgrn"]
    wpg = wpg_g.reshape(d, d)
    h3 = _rmsnorm("ffn2_norm", x2, ffn2_norm)
    (g2, u2, a2), (wd2,) = _ffn_up("ffn2", h3, wg2, wu2, gather("ffn2_w_down"))
    x3, _ = _ffn_down("ffn2", x2, a2, wd2)
    x4, ple_saved = _ple_fwd(x3, ple_norm, p16, wpg, wpp)
    dx4, d_final, loss_row = _loss_head(x4, final_w, target)

    sums, from_chips = {}, {}

    def pair_sums(names, grads, from_sibling):
        for n, g, s in zip(names, grads, from_sibling):
            sums[n] = _pair_sum("pair_sum_" + n, g, s, block_ids)

    def to_chips(names, grads, from_sibling):
        pair_sums(names, grads, from_sibling)
        return _chips_program([sums[n] for n in names])

    def arrived(names, results):
        for n, r in zip(names, results):
            from_chips[n] = [(r, 0), (r, 1), (r, 2)]

    def ffn_comms(first, second, names):
        dn, gn, un = ((n,) for n in names)
        return dict(
            dact=first, dwd=second,
            dwg=lambda av: _sibling_program([av["dwd"]]),
            dwu=lambda av: _merge([to_chips(dn, [av["dwd"]], av["sent"]["dwg"]), _sibling_program([av["dwg"]])]),
            dxa=lambda av: _merge([to_chips(gn, [av["dwg"]], av["sent"]["dwu"][1:]), _sibling_program([av["dwu"]])]),
            dxb=lambda av: to_chips(un, [av["dwu"]], av["sent"]["dxa"][1:]))

    def ffn_arrived(names, sent):
        arrived(names, [sent["dwu"][0], sent["dxa"][0], sent["dxb"][0]])

    dx3, dx3_16, dwpp, dwpg, d_ple = _ple_bwd(dx4, x3, ple_norm, p16, wpg, ple_saved)
    ple_names = ("ple_w_proj", "ple_w_gate")
    ple_grads = [dwpp, dwpg.reshape(N_DEV, d // N_DEV, d)]
    ffn2_names = ("ffn2_w_down", "ffn2_w_gate", "ffn2_w_up")
    dx2, dx2_16, _, _, _, d_ffn2, sent2 = _ffn_bwd(
        "ffn2b", dx3, dx3_16, x2, h3, g2, u2, a2, wg2, wu2, wd2, ffn2_norm,
        ffn_comms(_sibling_program(ple_grads), lambda av: to_chips(ple_names, ple_grads, av["sent"]["dact"]),
                  ffn2_names))
    arrived(ple_names, sent2["dwd"])
    ffn_arrived(ffn2_names, sent2)
    mix3 = ("w_out", "w_up_a", "w_up_b")
    dx1, dx1_16, mix_small, mix_made = _mixer_bwd(
        dx2, dx2_16, x1, *mix_args, wua, wub, wout, tables, mix_saved,
        dict(attention=lambda av: _sibling_program([av[n] for n in mix3]),
             hgrn=lambda av: to_chips(mix3, [av[n] for n in mix3], av["sent"]["attention"]),
             dx=lambda av: _sibling_program([av["w_in"]])))
    sent_mix = mix_made["sent"]
    arrived(mix3, sent_mix["hgrn"])
    pair_sums(("w_in",), [mix_made["w_in"]], sent_mix["dx"])
    ffn1_names = ("ffn1_w_down", "ffn1_w_gate", "ffn1_w_up")
    grad_x, _, _, _, _, d_ffn1, sent1 = _ffn_bwd(
        "ffn1b", dx1, dx1_16, x0, h1, g1, u1, a1, wg1, wu1, wd1, ffn1_norm,
        ffn_comms(_chips_program([sums["w_in"]], (1, 2)), _chips_program([sums["w_in"]], (3,)), ffn1_names))
    (near,), (far,) = sent1["dact"], sent1["dwd"]
    from_chips["w_in"] = [(near, 0), (near, 1), (far, 0)]
    ffn_arrived(ffn1_names, sent1)

    out = {}
    for group in ADAM_GROUPS:
        items = [(as_stored(n, w[n]), as_stored(n, mom[n]), as_stored(n, var[n]), sums[n], from_chips[n])
                 for n in group]
        results, _ = _adam_sharded("adam_" + group[0], items)
        for n, res in zip(group, results):
            out[n] = [as_stored(n, a) for a in res]

    dlb = mix_small["hgrn_lb"]
    small_grads = dict(
        ffn1_norm=d_ffn1, mix_norm=mix_small["mix_norm"], attn_sinks=mix_small["attn_sinks"][:, :A_HEADS],
        hgrn_lower_bound=jnp.concatenate([dlb, -dlb], axis=0), hgrn_norm=mix_small["hgrn_norm"],
        ffn2_norm=d_ffn2, ple_norm=d_ple, final_norm=d_final)

    small_like = [w[n] for n in SMALL]
    packed = _pack_rows([small_grads[n].reshape(w[n].shape) for n in SMALL] + [loss_row[:, :1]])
    total = _all_reduce_small(packed)
    n_rows = total.shape[0]
    pack_w = _pack_rows(small_like + [jnp.zeros((1, 1), F32)])
    pack_m = _pack_rows([mom[n] for n in SMALL] + [jnp.zeros((1, 1), F32)])
    pack_v = _pack_rows([var[n] for n in SMALL] + [jnp.zeros((1, 1), F32)])
    deltas = _adam_small(pack_w, total, pack_m, pack_v)
    like = small_like + [jnp.zeros((1, 1), F32)]
    g_small = _unpack_rows(total, like)
    d_small, m_small, v_small = (_unpack_rows(a, like) for a in deltas)
    for k, n in enumerate(SMALL):
        out[n] = [g_small[k], d_small[k], m_small[k], v_small[k]]
    loss = g_small[-1].reshape(())

    return (loss, grad_x.reshape(x.shape),
            *[out[n][0] for n in WEIGHTS], *[out[n][1] for n in WEIGHTS],
            *[out[n][2] for n in WEIGHTS], *[out[n][3] for n in WEIGHTS])
```

```python
import functools

import jax
import jax.numpy as jnp
from jax import lax
from jax.experimental import pallas as pl
from jax.experimental.pallas import tpu as pltpu

F32 = jnp.float32
BF16 = jnp.bfloat16
MESH = pl.DeviceIdType.MESH

EPS = 1e-6
N_DEV = 8
A_HEADS = 16
A_HEAD_DIM = 64
A_WIDTH = 1024
A_KV_WIDTH = 256
ATT_BLOCK = 128
ROT_DIM = 16
ROPE_THETA = 500000.0
B_WIDTH = 1024
B_HEAD = 128
CHUNK = 64
SUB = 16
MAX_DECAY_EXP = 60.0
PLE_DIM = 256
LANES = 128

ADAM_LR = 0.001
ADAM_B1 = 0.9
ADAM_B2 = 0.999
ADAM_EPS = 1e-08
ADAM_WD = 0.01
ADAM_STEP = 10

VMEM_LIMIT = 56 * 1024 * 1024
EPILOGUE_ELEMS = 256 * 1024
LATE_AT = 0.92

NN = (((1,), (0,)), ((), ()))
NT = (((1,), (1,)), ((), ()))
TN = (((0,), (0,)), ((), ()))

OFF_QA = 0
OFF_KA = 1024
OFF_VA = 1280
OFF_QB = 1536
OFF_FB = 2560
OFF_IB = 3584
OFF_OG = 4608
OFF_GA = 5632


def _sig(v):
    return 1.0 / (1.0 + jnp.exp(-v))


def _tile(n, pref, mult):
    t = min(n, pref)
    t -= t % mult
    while n % t:
        t -= mult
    return t


def _params(sem):
    return pltpu.CompilerParams(dimension_semantics=sem, vmem_limit_bytes=VMEM_LIMIT)


HBM_SPEC = pl.BlockSpec(memory_space=pltpu.HBM)


class _Comm:
    def __init__(self, ins, out_shapes, n_sems, pre, post, mid=None, mid_at=1.0, late=None):
        self.ins, self.out_shapes, self.n_sems, self.pre, self.post = ins, out_shapes, n_sems, pre, post
        self.mid = mid if mid is not None else (lambda *args: None)
        self.late = late if late is not None else (lambda *args: None)
        self.mid_at = mid_at
        self.parts = [len(out_shapes)]

    def split(self, outs):
        res, at = [], 0
        for n in self.parts:
            res.append(list(outs[at:at + n]))
            at += n
        return res


def _merge(progs):
    spans, ins, shapes, sems = [], [], [], [0, 0, 0]
    for p in progs:
        spans.append((len(ins), len(shapes), tuple(sems)))
        ins += list(p.ins)
        shapes += list(p.out_shapes)
        sems = [a + b for a, b in zip(sems, p.n_sems)]

    def run(which):
        def go(cin, cout, send, recv, local, base):
            for p, (i0, o0, s0) in zip(progs, spans):
                getattr(p, which)(cin[i0:i0 + len(p.ins)], cout[o0:o0 + len(p.out_shapes)], send, recv, local,
                                  tuple(b + s for b, s in zip(base, s0)))
        return go

    merged = _Comm(ins, shapes, tuple(sems), run("pre"), run("post"), run("mid"), max(p.mid_at for p in progs),
                   run("late"))
    merged.parts = [len(p.out_shapes) for p in progs]
    return merged


def _call(name, body, grid, in_specs, out_specs, out_shape, scratch, args, comm=None, aliases=None):
    sem = ("arbitrary",) * len(grid)
    aliases = aliases or {}
    if comm is None:
        res = pl.pallas_call(body, name=name, grid=grid, in_specs=in_specs, out_specs=out_specs,
                             out_shape=out_shape, scratch_shapes=scratch, input_output_aliases=aliases,
                             compiler_params=_params(sem))(*args)
        return list(res), []
    n_in, n_out, n_scr = len(in_specs), len(out_specs), len(scratch)
    c_in, c_out = len(comm.ins), len(comm.out_shapes)
    steps = functools.reduce(lambda a, b: a * b, grid)
    mid_step = min(steps - 1, int(comm.mid_at * steps))
    late_step = min(steps - 1, max(mid_step, int(LATE_AT * steps)))

    def carrier(*refs):
        at = [0]

        def take(n):
            at[0] += n
            return refs[at[0] - n:at[0]]

        ins, cins, outs, couts, scr, sems = take(n_in), take(c_in), take(n_out), take(c_out), take(n_scr), take(3)
        ids = [pl.program_id(a) for a in range(len(grid))]
        step = functools.reduce(lambda acc, ig: acc * ig[1] + ig[0], zip(ids, grid), 0)

        @pl.when(step == 0)
        def _():
            comm.pre(cins, couts, *sems, (0, 0, 0))

        body(*ins, *outs, *scr)

        @pl.when(step == mid_step)
        def _():
            comm.mid(cins, couts, *sems, (0, 0, 0))

        @pl.when(step == late_step)
        def _():
            comm.late(cins, couts, *sems, (0, 0, 0))

        @pl.when(step == steps - 1)
        def _():
            comm.post(cins, couts, *sems, (0, 0, 0))

    res = pl.pallas_call(
        carrier, name=name, grid=grid,
        in_specs=list(in_specs) + [HBM_SPEC] * c_in,
        out_specs=list(out_specs) + [HBM_SPEC] * c_out,
        out_shape=list(out_shape) + list(comm.out_shapes),
        scratch_shapes=list(scratch) + [pltpu.SemaphoreType.DMA((max(n, 1),)) for n in comm.n_sems],
        input_output_aliases=aliases,
        compiler_params=_params(sem),
    )(*args, *comm.ins)
    return list(res[:n_out]), list(res[n_out:])


def _mm(name, grid, ins, prods, acc_shapes, epi, outs, comm=None, aliases=None):
    n_in, n_out, nk = len(ins), len(outs), grid[2]
    tile_rows, tile_cols = acc_shapes[0]
    chunk = tile_rows
    while chunk * tile_cols > EPILOGUE_ELEMS and chunk % 32 == 0:
        chunk //= 2

    def body(*refs):
        in_refs = refs[:n_in]
        out_refs = refs[n_in:n_in + n_out]
        acc_refs = refs[n_in + n_out:]
        first_tile = jnp.logical_and(pl.program_id(0) == 0, pl.program_id(1) == 0)

        def product(lhs, rhs, dims):
            a = in_refs[lhs][...].astype(BF16)
            b = in_refs[rhs][...].astype(BF16)
            return lax.dot_general(a, b, dims, preferred_element_type=F32)

        def write(accs):
            totals = [None] * n_out
            for c in range(tile_rows // chunk):
                rows = slice(c * chunk, (c + 1) * chunk)
                vals = epi([a[rows] for a in accs], in_refs, rows)
                for idx, (o, v, spec) in enumerate(zip(out_refs, vals, outs)):
                    if spec[4]:
                        totals[idx] = v if totals[idx] is None else totals[idx] + v
                    else:
                        o[rows] = v.astype(o.dtype)
            for o, v, spec in zip(out_refs, totals, outs):
                if spec[4]:
                    @pl.when(first_tile)
                    def _():
                        o[...] = v.astype(o.dtype)

                    @pl.when(jnp.logical_not(first_tile))
                    def _():
                        o[...] += v.astype(o.dtype)

        if nk == 1:
            accs = [None] * len(acc_shapes)
            for lhs, rhs, acc, dims in prods:
                t = product(lhs, rhs, dims)
                accs[acc] = t if accs[acc] is None else accs[acc] + t
            write(accs)
        else:
            k = pl.program_id(2)

            @pl.when(k == 0)
            def _():
                for r in acc_refs:
                    r[...] = jnp.zeros(r.shape, F32)

            for lhs, rhs, acc, dims in prods:
                acc_refs[acc][...] += product(lhs, rhs, dims)

            @pl.when(k == nk - 1)
            def _():
                write(acc_refs)

    res, sent = _call(
        name, body, grid,
        [pl.BlockSpec(memory_space=pl.ANY) if b is None else pl.BlockSpec(b, im) for _, b, im in ins],
        [pl.BlockSpec(o[2], o[3]) for o in outs],
        [jax.ShapeDtypeStruct(o[0], o[1]) for o in outs],
        [] if nk == 1 else [pltpu.VMEM(s, F32) for s in acc_shapes],
        [a for a, _, _ in ins], comm, aliases)
    return res if comm is None else (res, sent)


def _rms_bwd(dh, xv, w, dres):
    r = lax.rsqrt(jnp.mean(xv * xv, axis=-1, keepdims=True) + EPS)
    g = dh * w
    dx = dres + r * (g - xv * (r * r) * jnp.mean(g * xv, axis=-1, keepdims=True))
    dw = jnp.sum(dh * xv * r, axis=0, keepdims=True)
    return dx, dw


def _rows(name, fn, ins, outs, tm, comm=None):
    n_in = len(ins)
    rows = outs[0][0][0]

    def body(*refs):
        vals = fn(*[r[...] for r in refs[:n_in]])
        for o, v in zip(refs[n_in:], vals):
            o[...] = v.astype(o.dtype)

    in_specs = []
    for a, tiled in ins:
        if tiled:
            in_specs.append(pl.BlockSpec((tm, a.shape[1]), lambda i: (i, 0)))
        else:
            in_specs.append(pl.BlockSpec(a.shape, lambda i, nd=a.ndim: (0,) * nd))
    res, sent = _call(
        name, body, (rows // tm,), in_specs,
        [pl.BlockSpec((tm, s[1]), lambda i: (i, 0)) for s, _ in outs],
        [jax.ShapeDtypeStruct(s, d) for s, d in outs], [], [a for a, _ in ins], comm)
    return res if comm is None else (res, sent)


def _rmsnorm(name, x, w, comm=None):
    def fn(xv, wv):
        r = lax.rsqrt(jnp.mean(xv * xv, axis=-1, keepdims=True) + EPS)
        return (xv * r * wv,)

    res = _rows(name, fn, [(x, True), (w, False)], [(x.shape, BF16)], _tile(x.shape[0], 512, 16), comm)
    return res[0] if comm is None else (res[0][0], res[1])


def _mmc(*args, comm=None, aliases=None):
    if comm is None:
        return _mm(*args, aliases=aliases), []
    return _mm(*args, comm=comm, aliases=aliases)


def _carried(comms, key, avail):
    c = comms.get(key) if comms else None
    return c(avail) if callable(c) else c


def _ffn_up(name, h, wg, wu, comm=None):
    t, d = h.shape
    n = wg.shape[1]
    tm = _tile(t, 1024, 16)

    def epi_up(accs, refs, rows):
        g, u = accs
        return [g, u, g * _sig(g) * u]

    hid = ((N_DEV, t, n), BF16, (None, tm, n), lambda i, j, k: (j, i, 0), False)
    return _mmc(
        name + "_up", (t // tm, N_DEV, 1),
        [(h, (tm, d), lambda i, j, k: (i, 0)),
         (wg, (None, n, d), lambda i, j, k: (j, 0, 0)),
         (wu, (None, n, d), lambda i, j, k: (j, 0, 0))],
        [(0, 1, 0, NT), (0, 2, 1, NT)], [(tm, n)] * 2, epi_up, [hid] * 3, comm=comm)


def _ffn_down(name, x, act, wd, comm=None):
    t, d = x.shape
    n = wd.shape[1]
    tm2 = _tile(t, 1024, 8)
    tn = _tile(d, 1024, 128)

    def epi_down(accs, refs, rows):
        return [refs[2][rows] + 0.5 * accs[0]]

    (x_out,), sent = _mmc(
        name + "_down", (t // tm2, d // tn, N_DEV),
        [(act, (None, tm2, n), lambda i, j, k: (k, i, 0)),
         (wd, (None, n, tn), lambda i, j, k: (k, 0, j)),
         (x, (tm2, tn), lambda i, j, k: (i, j))],
        [(0, 1, 0, NN)], [(tm2, tn)], epi_down,
        [((t, d), F32, (tm2, tn), lambda i, j, k: (i, j), False)], comm=comm)
    return x_out, sent


def _ffn_bwd(name, dxo, dxo16, x, h, gate, up, act, wg, wu, wd, norm_w, comms=None):
    t, d = x.shape
    n = wg.shape[1]
    tm = _tile(t, 1024, 16)
    sent = {}
    avail = dict(sent=sent)

    def epi_act(accs, refs, rows):
        da = 0.5 * accs[0]
        g = refs[2][rows].astype(F32)
        u = refs[3][rows].astype(F32)
        s = _sig(g)
        return [da * u * (s * (1.0 + g * (1.0 - s))), da * g * s]

    hid_in = lambda a: (a, (None, tm, n), lambda i, j, k: (j, i, 0))
    hid_out = ((N_DEV, t, n), BF16, (None, tm, n), lambda i, j, k: (j, i, 0), False)
    (dgate, dup), sent["dact"] = _mmc(
        name + "_dact", (t // tm, N_DEV, 1),
        [(dxo16, (tm, d), lambda i, j, k: (i, 0)),
         (wd, (None, n, d), lambda i, j, k: (j, 0, 0)),
         hid_in(gate), hid_in(up)],
        [(0, 1, 0, NT)], [(tm, n)], epi_act, [hid_out] * 2, comm=_carried(comms, "dact", avail))

    tk = _tile(t, 1024, 16)
    w_out = ((N_DEV, n, d), BF16, (None, n, d), lambda i, j, k: (i, 0, 0), False)

    def weight_grad(key, acts, stream, scale):
        (dw,), sent[key] = _mmc(
            name + "_" + key, (N_DEV, 1, t // tk),
            [(acts, (None, tk, n), lambda i, j, k: (i, k, 0)), (stream, (tk, d), lambda i, j, k: (k, 0))],
            [(0, 1, 0, TN)], [(n, d)], lambda accs, refs, rows: [scale * accs[0]], [w_out],
            comm=_carried(comms, key, avail))
        avail[key] = dw
        return dw

    dwd = weight_grad("dwd", act, dxo16, 0.5)
    dwg = weight_grad("dwg", dgate, h, 1.0)
    dwu = weight_grad("dwu", dup, h, 1.0)

    tm3 = _tile(t // 2, 512, 16)
    half = t // 2 // tm3

    def epi_dx(accs, refs, rows):
        dx, dw = _rms_bwd(accs[0], refs[4][rows], refs[5][...], refs[6][rows])
        return [dx, dx, dw]

    def dx_half(key, first, filled):
        at = 0 if first else half
        row = lambda i, j, k: (i + at, 0)
        hid = lambda i, j, k: (k, i + at, 0)
        ins = [(dgate, (None, tm3, n), hid), (wg, (None, n, d), lambda i, j, k: (k, 0, 0)),
               (dup, (None, tm3, n), hid), (wu, (None, n, d), lambda i, j, k: (k, 0, 0)),
               (x, (tm3, d), row), (norm_w, (1, d), lambda i, j, k: (0, 0)), (dxo, (tm3, d), row)]
        ins += [(a, None, None) for a in filled]
        res, sent[key] = _mmc(
            name + "_" + key, (half, 1, N_DEV), ins,
            [(0, 1, 0, NN), (2, 3, 0, NN)], [(tm3, d)], epi_dx,
            [((t, d), F32, (tm3, d), row, False), ((t, d), BF16, (tm3, d), row, False),
             ((1, d), F32, (1, d), lambda i, j, k: (0, 0), True)],
            comm=_carried(comms, key, avail), aliases={7 + k: k for k in range(len(filled))})
        return res

    dx_a, dx16_a, dnorm_a = dx_half("dxa", True, [])
    dx, dx16, dnorm_b = dx_half("dxb", False, [dx_a, dx16_a])
    return dx, dx16, dwg, dwu, dwd, dnorm_a + dnorm_b, sent


def _rope(tv, cos, s1, s2):
    return tv * cos + pltpu.roll(tv, LANES - 8, 1) * s1 + pltpu.roll(tv, 8, 1) * s2


def _rope_bwd(dr, cos, s1, s2):
    return dr * cos + pltpu.roll(dr * s1, 8, 1) + pltpu.roll(dr * s2, LANES - 8, 1)


def _rope_tables(positions):
    inv_freq = jnp.power(jnp.float32(ROPE_THETA), -jnp.arange(0, ROT_DIM, 2, dtype=F32) / ROT_DIM)
    ang = positions.reshape(-1).astype(F32)[:, None] * inv_freq
    cos, sin = jnp.cos(ang), jnp.sin(ang)
    t = ang.shape[0]
    pad = A_HEAD_DIM - ROT_DIM
    cos_t = jnp.concatenate([cos, cos, jnp.ones((t, pad), F32)], axis=1)
    s1_t = jnp.concatenate([-sin, jnp.zeros((t, pad + 8), F32)], axis=1)
    s2_t = jnp.concatenate([jnp.zeros((t, 8), F32), sin, jnp.zeros((t, pad), F32)], axis=1)
    return tuple(jnp.tile(v, (1, 2)) for v in (cos_t, s1_t, s2_t))


def _att_common(i, k_p, k_c, v_p, v_c, tabs_p, tabs_c):
    kcat = jnp.concatenate([k_p, k_c], axis=0)
    vcat = jnp.concatenate([v_p, v_c], axis=0)
    tabs_k = [jnp.concatenate([a, b], axis=0) for a, b in zip(tabs_p, tabs_c)]
    kvar, vvar = [], []
    for ks in range(2):
        sl = slice(ks * LANES, (ks + 1) * LANES)
        kr = _rope(kcat[:, sl], *tabs_k)
        kvar.append([kr.astype(BF16), pltpu.roll(kr, 64, 1).astype(BF16)])
        vvar.append([vcat[:, sl].astype(BF16), pltpu.roll(vcat[:, sl], 64, 1).astype(BF16)])
    qi = lax.broadcasted_iota(jnp.int32, (ATT_BLOCK, 2 * ATT_BLOCK), 0)
    kj = lax.broadcasted_iota(jnp.int32, (ATT_BLOCK, 2 * ATT_BLOCK), 1)
    dist = qi + ATT_BLOCK - kj
    allowed = (dist >= 0) & (dist < ATT_BLOCK) & ((i > 0) | (kj >= ATT_BLOCK))
    lane = lax.broadcasted_iota(jnp.int32, (1, LANES), 1)
    halves = [lane < 64, lane >= 64]
    return kvar, vvar, allowed, halves


def _att_head_probs(qm, kslab, allowed, sink):
    s = lax.dot_general(qm, kslab, NT, preferred_element_type=F32)
    s = jnp.where(allowed, s, -jnp.inf)
    m = jnp.maximum(jnp.max(s, axis=-1, keepdims=True), sink)
    e = jnp.exp(s - m)
    es = jnp.exp(sink - m)
    den = jnp.sum(e, axis=-1, keepdims=True) + es
    inv = 1.0 / den
    return e * inv, es * inv


def _head_slots(h):
    pair, half = h // 2, h % 2
    kvh = h // 4
    return pair, half, kvh // 2, int(kvh % 2 != half)


def _attention_fwd(proj, tables, sinks, comm=None):
    t = proj.shape[0]
    nb = t // ATT_BLOCK
    b = ATT_BLOCK
    scale = A_HEAD_DIM ** -0.5

    def body(sink_ref, q_ref, kp_ref, kc_ref, vp_ref, vc_ref,
             cp_ref, ap_ref, bp_ref, cc_ref, ac_ref, bc_ref, o_ref):
        i = pl.program_id(0)
        tabs_p = (cp_ref[...], ap_ref[...], bp_ref[...])
        tabs_c = (cc_ref[...], ac_ref[...], bc_ref[...])
        kvar, vvar, allowed, halves = _att_common(
            i, kp_ref[...], kc_ref[...], vp_ref[...], vc_ref[...], tabs_p, tabs_c)
        for pair in range(A_HEADS // 2):
            sl = slice(pair * LANES, (pair + 1) * LANES)
            q2 = _rope(q_ref[:, sl], *tabs_c) * scale
            acc = jnp.zeros((b, LANES), F32)
            for half in range(2):
                h = 2 * pair + half
                _, _, ks, var = _head_slots(h)
                qm = jnp.where(halves[half], q2, 0.0).astype(BF16)
                p, _ = _att_head_probs(qm, kvar[ks][var], allowed, sink_ref[h])
                o = lax.dot_general(p.astype(BF16), vvar[ks][var], NN, preferred_element_type=F32)
                acc = acc + jnp.where(halves[half], o, 0.0)
            o_ref[:, sl] = acc.astype(o_ref.dtype)

    cur = lambda col: (lambda i: (i, col))
    prev = lambda col: (lambda i: (jnp.maximum(i - 1, 0), col))
    tab = [pl.BlockSpec((b, LANES), prev(0))] * 3 + [pl.BlockSpec((b, LANES), cur(0))] * 3
    (out,), sent = _call(
        "attention_fwd", body, (nb,),
        [pl.BlockSpec(memory_space=pltpu.SMEM),
         pl.BlockSpec((b, A_WIDTH), cur(0)),
         pl.BlockSpec((b, A_KV_WIDTH), prev(OFF_KA // A_KV_WIDTH)),
         pl.BlockSpec((b, A_KV_WIDTH), cur(OFF_KA // A_KV_WIDTH)),
         pl.BlockSpec((b, A_KV_WIDTH), prev(OFF_VA // A_KV_WIDTH)),
         pl.BlockSpec((b, A_KV_WIDTH), cur(OFF_VA // A_KV_WIDTH))] + tab,
        [pl.BlockSpec((b, A_WIDTH), cur(0))],
        [jax.ShapeDtypeStruct((t, A_WIDTH), BF16)], [],
        [sinks, proj, proj, proj, proj, proj, *tables, *tables], comm)
    return out, sent


def _attention_bwd(proj, tables, sinks, d_out, comm=None):
    t = proj.shape[0]
    nb = t // ATT_BLOCK
    b = ATT_BLOCK
    scale = A_HEAD_DIM ** -0.5

    def body(sink_ref, q_ref, kp_ref, kc_ref, vp_ref, vc_ref,
             cp_ref, ap_ref, bp_ref, cc_ref, ac_ref, bc_ref, do_ref,
             dq_ref, dk_ref, dv_ref, dsink_ref, dk_carry, dv_carry):
        step = pl.program_id(0)
        i = nb - 1 - step

        @pl.when(step == 0)
        def _():
            dk_carry[...] = jnp.zeros(dk_carry.shape, F32)
            dv_carry[...] = jnp.zeros(dv_carry.shape, F32)
            dsink_ref[...] = jnp.zeros(dsink_ref.shape, F32)

        tabs_p = (cp_ref[...], ap_ref[...], bp_ref[...])
        tabs_c = (cc_ref[...], ac_ref[...], bc_ref[...])
        kvar, vvar, allowed, halves = _att_common(
            i, kp_ref[...], kc_ref[...], vp_ref[...], vc_ref[...], tabs_p, tabs_c)
        lane = lax.broadcasted_iota(jnp.int32, (1, LANES), 1)
        dk_slab = [jnp.zeros((2 * b, LANES), F32) for _ in range(2)]
        dv_slab = [jnp.zeros((2 * b, LANES), F32) for _ in range(2)]
        dsink = jnp.zeros((1, LANES), F32)
        for pair in range(A_HEADS // 2):
            sl = slice(pair * LANES, (pair + 1) * LANES)
            q2 = _rope(q_ref[:, sl], *tabs_c) * scale
            do2 = do_ref[:, sl].astype(F32)
            dq2 = jnp.zeros((b, LANES), F32)
            for half in range(2):
                h = 2 * pair + half
                _, _, ks, var = _head_slots(h)
                qm = jnp.where(halves[half], q2, 0.0).astype(BF16)
                dom = jnp.where(halves[half], do2, 0.0).astype(BF16)
                p, p_sink = _att_head_probs(qm, kvar[ks][var], allowed, sink_ref[h])
                dp = lax.dot_general(dom, vvar[ks][var], NT, preferred_element_type=F32)
                dsum = jnp.sum(p * dp, axis=-1, keepdims=True)
                ds = (p * (dp - dsum)).astype(BF16)
                dsink = dsink + jnp.where(lane == h, -jnp.sum(p_sink * dsum), 0.0)
                dqh = lax.dot_general(ds, kvar[ks][var], NN, preferred_element_type=F32)
                dq2 = dq2 + jnp.where(halves[half], dqh, 0.0)
                dkx = lax.dot_general(ds, qm, TN, preferred_element_type=F32)
                dvx = lax.dot_general(p.astype(BF16), dom, TN, preferred_element_type=F32)
                if var:
                    dkx = pltpu.roll(dkx, 64, 1)
                    dvx = pltpu.roll(dvx, 64, 1)
                dk_slab[ks] = dk_slab[ks] + dkx
                dv_slab[ks] = dv_slab[ks] + dvx
            dq_ref[:, sl] = _rope_bwd(dq2 * scale, *tabs_c).astype(dq_ref.dtype)
        for ks in range(2):
            sl = slice(ks * LANES, (ks + 1) * LANES)
            dk_cur = dk_slab[ks][b:] + dk_carry[:, sl]
            dk_ref[:, sl] = _rope_bwd(dk_cur, *tabs_c).astype(dk_ref.dtype)
            dv_ref[:, sl] = (dv_slab[ks][b:] + dv_carry[:, sl]).astype(dv_ref.dtype)
            dk_carry[:, sl] = dk_slab[ks][:b]
            dv_carry[:, sl] = dv_slab[ks][:b]
        dsink_ref[...] += dsink

    cur = lambda col: (lambda s: (nb - 1 - s, col))
    prev = lambda col: (lambda s: (jnp.maximum(nb - 2 - s, 0), col))
    tab = [pl.BlockSpec((b, LANES), prev(0))] * 3 + [pl.BlockSpec((b, LANES), cur(0))] * 3
    return _call(
        "attention_bwd", body, (nb,),
        [pl.BlockSpec(memory_space=pltpu.SMEM),
         pl.BlockSpec((b, A_WIDTH), cur(0)),
         pl.BlockSpec((b, A_KV_WIDTH), prev(OFF_KA // A_KV_WIDTH)),
         pl.BlockSpec((b, A_KV_WIDTH), cur(OFF_KA // A_KV_WIDTH)),
         pl.BlockSpec((b, A_KV_WIDTH), prev(OFF_VA // A_KV_WIDTH)),
         pl.BlockSpec((b, A_KV_WIDTH), cur(OFF_VA // A_KV_WIDTH))] + tab
        + [pl.BlockSpec((b, A_WIDTH), cur(0))],
        [pl.BlockSpec((b, A_WIDTH), cur(0)),
         pl.BlockSpec((b, A_KV_WIDTH), cur(0)),
         pl.BlockSpec((b, A_KV_WIDTH), cur(0)),
         pl.BlockSpec((1, LANES), lambda s: (0, 0))],
        [jax.ShapeDtypeStruct((t, A_WIDTH), BF16),
         jax.ShapeDtypeStruct((t, A_KV_WIDTH), BF16),
         jax.ShapeDtypeStruct((t, A_KV_WIDTH), BF16),
         jax.ShapeDtypeStruct((1, LANES), F32)],
        [pltpu.VMEM((b, A_KV_WIDTH), F32), pltpu.VMEM((b, A_KV_WIDTH), F32)],
        [sinks, proj, proj, proj, proj, proj, *tables, *tables, d_out], comm)


HGRN_HEADS_PER_STEP = 4
HGRN_W = HGRN_HEADS_PER_STEP * B_HEAD
HGRN_ROWS = 4 * CHUNK


def _hgrn_gates(qp, fp, hlb):
    lb = _sig(hlb[0:1] - hlb[1:2])
    sg = _sig(fp)
    big_f = lb + (1.0 - lb) * sg
    kk = (1.0 - lb) * (1.0 - sg)
    sq = _sig(qp)
    return lb, sg, big_f, kk, sq


def _tri(lower):
    r = lax.broadcasted_iota(jnp.int32, (CHUNK, CHUNK), 0)
    c = lax.broadcasted_iota(jnp.int32, (CHUNK, CHUNK), 1)
    return (r >= c) if lower else (r <= c)


def _hgrn_factors(q_ref, k_ref, b_ref, sl, r0):
    out = []
    whole = slice(r0, r0 + CHUNK)
    bh = b_ref[whole, sl]
    for i in range(CHUNK // SUB):
        rows = slice(r0 + i * SUB, r0 + (i + 1) * SUB)
        b0 = b_ref[r0 + i * SUB - 1:r0 + i * SUB, sl] if i else jnp.zeros((1, B_HEAD), F32)
        eq = jnp.exp(b_ref[rows, sl] - b0)
        ek = jnp.exp(jnp.minimum(b0 - bh, MAX_DECAY_EXP))
        out.append((q_ref[rows, sl] * eq, k_ref[whole, sl] * ek, eq, ek))
    return out


def _hgrn_scores(factors):
    rows = [lax.dot_general(qd.astype(BF16), kx.astype(BF16), NT, preferred_element_type=F32)
            for qd, kx, _, _ in factors]
    return jnp.where(_tri(True), jnp.concatenate(rows, axis=0), 0.0)


def _hgrn_specs(steps, reverse):
    w = HGRN_W
    block = (lambda c: steps - 1 - c) if reverse else (lambda c: c)
    col = lambda off: pl.BlockSpec((HGRN_ROWS, w), lambda hh, c: (block(c), off // w + hh))
    return block, col


def _chunk_cumsum(log_f, r0, lower):
    tri = _tri(lower).astype(F32)
    return lax.dot_general(tri, log_f[r0:r0 + CHUNK], NN, precision=lax.Precision.HIGHEST,
                           preferred_element_type=F32)


def _hgrn_fwd(proj, hlb, wn, comm=None):
    t = proj.shape[0]
    nc = t // CHUNK
    steps = t // HGRN_ROWS
    w = HGRN_W
    block, col = _hgrn_specs(steps, False)

    def body(q_ref, f_ref, v_ref, og_ref, hlb_ref, wn_ref, ob_ref, oraw_ref, st_ref,
             state, qs, ks, bs):
        c = pl.program_id(1)

        @pl.when(c == 0)
        def _():
            state[...] = jnp.zeros(state.shape, F32)

        qp = q_ref[...]
        lb, sg, big_f, kk, sq = _hgrn_gates(qp, f_ref[...], hlb_ref[...])
        qs[...] = qp * sq
        ks[...] = kk
        log_f = jnp.log(big_f)
        for sub in range(HGRN_ROWS // CHUNK):
            r0 = sub * CHUNK
            rows = slice(r0, r0 + CHUNK)
            bs[rows] = _chunk_cumsum(log_f, r0, True)
            for h in range(HGRN_HEADS_PER_STEP):
                sl = slice(h * B_HEAD, (h + 1) * B_HEAD)
                bh = bs[rows, sl]
                qh = qs[rows, sl]
                kh = ks[rows, sl]
                vh = v_ref[rows, sl].astype(BF16)
                s0 = state[h]
                st_ref[sub, h] = s0.astype(st_ref.dtype)
                a = _hgrn_scores(_hgrn_factors(qs, ks, bs, sl, r0))
                o = lax.dot_general(a.astype(BF16), vh, NN, preferred_element_type=F32)
                o = o + lax.dot_general((qh * jnp.exp(bh)).astype(BF16), s0.astype(BF16), NT,
                                        preferred_element_type=F32)
                bl = bs[r0 + CHUNK - 1:r0 + CHUNK, sl]
                khat = (kh * jnp.exp(bl - bh)).astype(BF16)
                state[h] = s0 * jnp.exp(bl) + lax.dot_general(vh, khat, TN, preferred_element_type=F32)
                oraw_ref[rows, sl] = o
                r = lax.rsqrt(jnp.mean(o * o, axis=-1, keepdims=True) + EPS)
                og = og_ref[rows, sl]
                ob_ref[rows, sl] = (o * r * wn_ref[:, sl] * (og * _sig(og))).astype(ob_ref.dtype)

    vec = lambda rows: pl.BlockSpec((rows, w), lambda hh, c: (0, hh))
    tile = pl.BlockSpec((HGRN_ROWS, w), lambda hh, c: (c, hh))
    return _call(
        "hgrn_fwd", body, (B_WIDTH // w, steps),
        [col(OFF_QB), col(OFF_FB), col(OFF_IB), col(OFF_OG), vec(2), vec(1)],
        [tile, tile,
         pl.BlockSpec((HGRN_ROWS // CHUNK, HGRN_HEADS_PER_STEP, B_HEAD, B_HEAD), lambda hh, c: (c, hh, 0, 0))],
        [jax.ShapeDtypeStruct((t, B_WIDTH), BF16),
         jax.ShapeDtypeStruct((t, B_WIDTH), F32),
         jax.ShapeDtypeStruct((nc, B_WIDTH // B_HEAD, B_HEAD, B_HEAD), BF16)],
        [pltpu.VMEM((HGRN_HEADS_PER_STEP, B_HEAD, B_HEAD), F32)] + [pltpu.VMEM((HGRN_ROWS, w), F32)] * 3,
        [proj, proj, proj, proj, hlb, wn], comm)


def _hgrn_bwd(proj, hlb, wn, o_raw, states, d_out, comm=None):
    t = proj.shape[0]
    steps = t // HGRN_ROWS
    w = HGRN_W
    block, col = _hgrn_specs(steps, True)

    def body(q_ref, f_ref, v_ref, og_ref, hlb_ref, wn_ref, oraw_ref, st_ref, dout_ref,
             dq_ref, df_ref, dv_ref, dog_ref, dlb_ref, dwn_ref,
             dstate, qs, ks, bs, dbs, dks, dqs, dgs):
        c = pl.program_id(1)

        @pl.when(c == 0)
        def _():
            dstate[...] = jnp.zeros(dstate.shape, F32)
            dlb_ref[...] = jnp.zeros(dlb_ref.shape, F32)
            dwn_ref[...] = jnp.zeros(dwn_ref.shape, F32)

        qp = q_ref[...]
        lb, sg, big_f, kk, sq = _hgrn_gates(qp, f_ref[...], hlb_ref[...])
        qs[...] = qp * sq
        ks[...] = kk
        log_f = jnp.log(big_f)
        last_row = lax.broadcasted_iota(jnp.int32, (CHUNK, 1), 0) == CHUNK - 1
        for sub in reversed(range(HGRN_ROWS // CHUNK)):
            r0 = sub * CHUNK
            rows = slice(r0, r0 + CHUNK)
            bs[rows] = _chunk_cumsum(log_f, r0, True)
            for h in range(HGRN_HEADS_PER_STEP):
                sl = slice(h * B_HEAD, (h + 1) * B_HEAD)
                bh = bs[rows, sl]
                qh = qs[rows, sl]
                kh = ks[rows, sl]
                vh = v_ref[rows, sl].astype(BF16)
                s0 = st_ref[sub, h]
                ds1 = dstate[h]
                o = oraw_ref[rows, sl]
                og = og_ref[rows, sl]
                wnh = wn_ref[:, sl]
                sog = _sig(og)
                r = lax.rsqrt(jnp.mean(o * o, axis=-1, keepdims=True) + EPS)
                dob = dout_ref[rows, sl].astype(F32)
                don = dob * (og * sog)
                dog_ref[rows, sl] = (dob * (o * r * wnh) * (sog * (1.0 + og * (1.0 - sog)))).astype(dog_ref.dtype)
                dwn_ref[:, sl] += jnp.sum(don * o * r, axis=0, keepdims=True)
                gp = don * wnh
                do = r * (gp - o * (r * r) * jnp.mean(gp * o, axis=-1, keepdims=True))
                do16 = do.astype(BF16)
                factors = _hgrn_factors(qs, ks, bs, sl, r0)
                a = _hgrn_scores(factors)
                da = jnp.where(_tri(True), lax.dot_general(do16, vh, NT, preferred_element_type=F32), 0.0)
                dv = lax.dot_general(a.astype(BF16), do16, TN, preferred_element_type=F32)
                dq_rows = []
                dk = jnp.zeros((CHUNK, B_HEAD), F32)
                for i, (qd, kx, eq, ek) in enumerate(factors):
                    da_i = da[i * SUB:(i + 1) * SUB]
                    dq_rows.append(lax.dot_general(da_i, kx, NN, precision=lax.Precision.HIGHEST,
                                                   preferred_element_type=F32) * eq)
                    dk = dk + lax.dot_general(da_i, qd, TN, precision=lax.Precision.HIGHEST,
                                              preferred_element_type=F32) * ek
                eb = jnp.exp(bh)
                bl = bs[r0 + CHUNK - 1:r0 + CHUNK, sl]
                ebl = jnp.exp(bl)
                ekl = jnp.exp(bl - bh)
                qe = (qh * eb).astype(BF16)
                khat = (kh * ekl).astype(BF16)
                ds1_16 = ds1.astype(BF16)
                dq = jnp.concatenate(dq_rows, axis=0) + eb * lax.dot_general(
                    do16, s0, NN, preferred_element_type=F32)
                dv = dv + lax.dot_general(khat, ds1_16, NT, preferred_element_type=F32)
                dk_state = ekl * lax.dot_general(vh, ds1_16, NN, preferred_element_type=F32)
                dk = dk + dk_state
                db_last = (jnp.sum(kh * dk_state, axis=0, keepdims=True)
                           + ebl * jnp.sum(s0.astype(F32) * ds1, axis=0, keepdims=True))
                dbs[rows, sl] = qh * dq - kh * dk + jnp.where(last_row, db_last, 0.0)
                dks[rows, sl] = dk
                dqs[rows, sl] = dq
                dstate[h] = ds1 * ebl + lax.dot_general(do16, qe, TN, preferred_element_type=F32)
                dv_ref[rows, sl] = dv.astype(dv_ref.dtype)
            dgs[rows] = _chunk_cumsum(dbs, r0, False)
        dq_ref[...] = (dqs[...] * (sq * (1.0 + qp * (1.0 - sq)))).astype(dq_ref.dtype)
        df_minus_dk = dgs[...] / big_f - dks[...]
        df_ref[...] = ((1.0 - lb) * df_minus_dk * sg * (1.0 - sg)).astype(df_ref.dtype)
        dlb_ref[...] += jnp.sum((1.0 - sg) * df_minus_dk, axis=0, keepdims=True) * (lb * (1.0 - lb))

    vec = lambda rows: pl.BlockSpec((rows, w), lambda hh, c: (0, hh))
    blk = pl.BlockSpec((HGRN_ROWS, w), lambda hh, c: (block(c), hh))
    act = jax.ShapeDtypeStruct((t, B_WIDTH), BF16)
    row = jax.ShapeDtypeStruct((1, B_WIDTH), F32)
    return _call(
        "hgrn_bwd", body, (B_WIDTH // w, steps),
        [col(OFF_QB), col(OFF_FB), col(OFF_IB), col(OFF_OG), vec(2), vec(1), blk,
         pl.BlockSpec((HGRN_ROWS // CHUNK, HGRN_HEADS_PER_STEP, B_HEAD, B_HEAD),
                      lambda hh, c: (block(c), hh, 0, 0)),
         blk],
        [blk, blk, blk, blk, vec(1), vec(1)],
        [act, act, act, act, row, row],
        [pltpu.VMEM((HGRN_HEADS_PER_STEP, B_HEAD, B_HEAD), F32)] + [pltpu.VMEM((HGRN_ROWS, w), F32)] * 7,
        [proj, proj, proj, proj, hlb, wn, o_raw, states, d_out], comm)


def _mixer_fwd(x1, mix_norm, w_in, sinks, hlb, wn, weights, tables, comms):
    t, d = x1.shape
    n_in = w_in[0].shape[0]
    nb = d // N_DEV
    sent = {}
    h2 = _rmsnorm("mix_norm", x1, mix_norm)
    tm = _tile(t, 1024, 16)
    tn = _tile(n_in, 512, 128)
    (proj,), sent["proj"] = _mmc(
        "mix_proj", (t // tm, n_in // tn, 1),
        [(h2, (tm, d // 2), lambda i, j, k: (i, 0)), (w_in[0], (tn, d // 2), lambda i, j, k: (j, 0)),
         (h2, (tm, d // 2), lambda i, j, k: (i, 1)), (w_in[1], (tn, d // 2), lambda i, j, k: (j, 0))],
        [(0, 1, 0, NT), (2, 3, 0, NT)], [(tm, tn)], lambda accs, refs, rows: accs,
        [((t, n_in), F32, (tm, tn), lambda i, j, k: (i, j), False)], comm=comms.get("proj"))
    out_a, sent["attention"] = _attention_fwd(proj, tables, sinks, comms.get("attention"))
    (out_b, o_raw, states), sent["hgrn"] = _hgrn_fwd(proj, hlb, wn, comms.get("hgrn"))
    wua, wub, w_out = weights(sent)

    def epi_merge(accs, refs, rows):
        ua, ub = accs
        return [_sig(refs[4][rows]) * ua + _sig(refs[5][rows]) * ub, ua, ub]

    wide = ((t, d), BF16, (tm, nb), lambda i, j, k: (i, j), False)
    (merged, ua, ub), sent["merge"] = _mmc(
        "mix_merge", (t // tm, N_DEV, 1),
        [(out_a, (tm, A_WIDTH), lambda i, j, k: (i, 0)),
         (wua, (None, A_WIDTH, nb), lambda i, j, k: (j, 0, 0)),
         (out_b, (tm, B_WIDTH), lambda i, j, k: (i, 0)),
         (wub, (None, B_WIDTH, nb), lambda i, j, k: (j, 0, 0)),
         (proj, (tm, nb), lambda i, j, k: (i, OFF_GA // nb + j)),
         (proj, (tm, nb), lambda i, j, k: (i, (OFF_GA + d) // nb + j))],
        [(0, 1, 0, NN), (2, 3, 1, NN)], [(tm, nb)] * 2, epi_merge, [wide] * 3, comm=comms.get("merge"))

    tn2 = _tile(d, 1024, 128)
    (x2,) = _mm(
        "mix_out", (t // tm, d // tn2, 1),
        [(merged, (tm, d), lambda i, j, k: (i, 0)), (w_out, (d, tn2), lambda i, j, k: (0, j)),
         (x1, (tm, tn2), lambda i, j, k: (i, j))],
        [(0, 1, 0, NN)], [(tm, tn2)], lambda accs, refs, rows: [refs[2][rows] + accs[0]],
        [((t, d), F32, (tm, tn2), lambda i, j, k: (i, j), False)])
    saved = dict(h2=h2, proj=proj, out_a=out_a, out_b=out_b, o_raw=o_raw, states=states,
                 merged=merged, ua=ua, ub=ub)
    return x2, saved, sent


def _mixer_bwd(dx2, dx2_16, x1, mix_norm, w_in, sinks, hlb, wn, wua, wub, w_out, tables, sv, comms):
    t, d = x1.shape
    sent = {}
    avail = dict(sent=sent)
    n_in = w_in[0].shape[0]
    nb = d // N_DEV
    proj = sv["proj"]
    tm = _tile(t, 1024, 16)
    tn = _tile(d, 512, 128)

    def epi_gate(accs, refs, rows):
        dm = accs[0]
        sa = _sig(refs[2][rows])
        sb = _sig(refs[3][rows])
        ua = refs[4][rows].astype(F32)
        ub = refs[5][rows].astype(F32)
        return [dm * sa, dm * sb, dm * ua * sa * (1.0 - sa), dm * ub * sb * (1.0 - sb)]

    wide = ((t, d), BF16, (tm, tn), lambda i, j, k: (i, j), False)
    dua, dub, dga, dgb = _mm(
        "mix_dmerge", (t // tm, d // tn, 1),
        [(dx2_16, (tm, d), lambda i, j, k: (i, 0)),
         (w_out, (tn, d), lambda i, j, k: (j, 0)),
         (proj, (tm, tn), lambda i, j, k: (i, OFF_GA // tn + j)),
         (proj, (tm, tn), lambda i, j, k: (i, (OFF_GA + d) // tn + j)),
         (sv["ua"], (tm, tn), lambda i, j, k: (i, j)),
         (sv["ub"], (tm, tn), lambda i, j, k: (i, j))],
        [(0, 1, 0, NT)], [(tm, tn)], epi_gate, [wide] * 4)

    tk = _tile(t, 1024, 16)
    two = _tile(d, 1024, 128)
    (dw_out,) = _mm(
        "mix_dwout", (d // two, 1, t // tk),
        [(sv["merged"], (tk, two), lambda i, j, k: (k, i)), (dx2_16, (tk, d), lambda i, j, k: (k, 0))],
        [(0, 1, 0, TN)], [(two, d)], lambda accs, refs, rows: accs,
        [((d, d), BF16, (two, d), lambda i, j, k: (i, 0), False)])

    w_spec = lambda rows: ((N_DEV, rows, nb), BF16, (None, rows, nb), lambda i, j, k: (j, 0, 0), False)
    tk2 = _tile(t, 2048, 16)
    dwua, dwub = _mm(
        "mix_dwup", (1, N_DEV, t // tk2),
        [(sv["out_a"], (tk2, A_WIDTH), lambda i, j, k: (k, 0)),
         (dua, (tk2, nb), lambda i, j, k: (k, j)),
         (sv["out_b"], (tk2, B_WIDTH), lambda i, j, k: (k, 0)),
         (dub, (tk2, nb), lambda i, j, k: (k, j))],
        [(0, 1, 0, TN), (2, 3, 1, TN)], [(A_WIDTH, nb), (B_WIDTH, nb)], lambda accs, refs, rows: accs,
        [w_spec(A_WIDTH), w_spec(B_WIDTH)])

    d_out_a, d_out_b = _mm(
        "mix_dup", (t // tm, 1, N_DEV),
        [(dua, (tm, nb), lambda i, j, k: (i, k)),
         (wua, (None, A_WIDTH, nb), lambda i, j, k: (k, 0, 0)),
         (dub, (tm, nb), lambda i, j, k: (i, k)),
         (wub, (None, B_WIDTH, nb), lambda i, j, k: (k, 0, 0))],
        [(0, 1, 0, NT), (2, 3, 1, NT)], [(tm, A_WIDTH), (tm, B_WIDTH)], lambda accs, refs, rows: accs,
        [((t, A_WIDTH), BF16, (tm, A_WIDTH), lambda i, j, k: (i, 0), False),
         ((t, B_WIDTH), BF16, (tm, B_WIDTH), lambda i, j, k: (i, 0), False)])

    avail.update(w_out=dw_out.reshape(N_DEV, d // N_DEV, d), w_up_a=dwua, w_up_b=dwub)
    (dq_a, dk_a, dv_a, dsinks), sent["attention"] = _attention_bwd(
        proj, tables, sinks, d_out_a, _carried(comms, "attention", avail))
    (dq_b, df_b, di_b, dog_b, dlb, dwn), sent["hgrn"] = _hgrn_bwd(
        proj, hlb, wn, sv["o_raw"], sv["states"], d_out_b, _carried(comms, "hgrn", avail))
    dproj = jnp.concatenate([dq_a, dk_a, dv_a, dq_b, df_b, di_b, dog_b, dga, dgb], axis=1)

    tdm = _tile(d, 1024, 128)
    twn = _tile(n_in, 2432, 128)
    tkw = _tile(t, 512, 16)
    (dw_in,) = _mm(
        "mix_dwin", (n_in // twn, d // tdm, t // tkw),
        [(dproj, (tkw, twn), lambda i, j, k: (k, i)), (sv["h2"], (tkw, tdm), lambda i, j, k: (k, j))],
        [(0, 1, 0, TN)], [(twn, tdm)], lambda accs, refs, rows: accs,
        [((n_in, d), BF16, (twn, tdm), lambda i, j, k: (i, j), False)])

    avail.update(w_in=dw_in.reshape(N_DEV, n_in // N_DEV, d))
    tm3 = _tile(t, 512, 16)
    tpn = _tile(n_in, 512, 128)

    def epi_dx(accs, refs, rows):
        dh = jnp.concatenate(accs, axis=1)
        dx, dw = _rms_bwd(dh, refs[3][rows], refs[4][...], refs[5][rows])
        return [dx, dx, dw]

    (dx1, dx1_16, dmix), sent["dx"] = _mmc(
        "mix_dx", (t // tm3, 1, n_in // tpn),
        [(dproj, (tm3, tpn), lambda i, j, k: (i, k)),
         (w_in[0], (tpn, d // 2), lambda i, j, k: (k, 0)), (w_in[1], (tpn, d // 2), lambda i, j, k: (k, 0)),
         (x1, (tm3, d), lambda i, j, k: (i, 0)), (mix_norm, (1, d), lambda i, j, k: (0, 0)),
         (dx2, (tm3, d), lambda i, j, k: (i, 0))],
        [(0, 1, 0, NN), (0, 2, 1, NN)], [(tm3, d // 2)] * 2, epi_dx,
        [((t, d), F32, (tm3, d), lambda i, j, k: (i, 0), False),
         ((t, d), BF16, (tm3, d), lambda i, j, k: (i, 0), False),
         ((1, d), F32, (1, d), lambda i, j, k: (0, 0), True)],
        comm=_carried(comms, "dx", avail))
    small = dict(mix_norm=dmix, attn_sinks=dsinks, hgrn_lb=dlb, hgrn_norm=dwn)
    return dx1, dx1_16, small, avail


def _ple_fwd(x3, ple_norm, p16, wpg, wpp):
    t, d = x3.shape
    nb = d // N_DEV
    h4 = _rmsnorm("ple_norm", x3, ple_norm)
    tm = _tile(t, 1024, 16)

    def epi(accs, refs, rows):
        zg, pp = accs
        return [refs[4][rows] + _sig(zg) * pp, zg, pp]

    blk = lambda dt: ((t, d), dt, (tm, nb), lambda i, j, k: (i, j), False)
    x4, zg, pp = _mm(
        "ple_fwd", (t // tm, N_DEV, 1),
        [(h4, (tm, d), lambda i, j, k: (i, 0)), (wpg, (d, nb), lambda i, j, k: (0, j)),
         (p16, (tm, PLE_DIM), lambda i, j, k: (i, 0)),
         (wpp, (None, PLE_DIM, nb), lambda i, j, k: (j, 0, 0)),
         (x3, (tm, nb), lambda i, j, k: (i, j))],
        [(0, 1, 0, NN), (2, 3, 1, NN)], [(tm, nb)] * 2, epi, [blk(F32), blk(BF16), blk(BF16)])
    return x4, dict(h4=h4, zg=zg, pp=pp)


def _ple_bwd(dx4, x3, ple_norm, p16, wpg, sv):
    t, d = x3.shape
    nb = d // N_DEV

    def gate_grads(dx, zg, pp):
        s = _sig(zg.astype(F32))
        return dx * pp.astype(F32) * s * (1.0 - s), dx * s

    dzg, dpp = _rows("ple_dgate", gate_grads, [(dx4, True), (sv["zg"], True), (sv["pp"], True)],
                     [((t, d), BF16), ((t, d), BF16)], _tile(t, 512, 16))
    (dwpp,) = _mm(
        "ple_dwproj", (1, N_DEV, 1),
        [(p16, (t, PLE_DIM), lambda i, j, k: (0, 0)), (dpp, (t, nb), lambda i, j, k: (0, j))],
        [(0, 1, 0, TN)], [(PLE_DIM, nb)], lambda accs, refs, rows: accs,
        [((N_DEV, PLE_DIM, nb), BF16, (None, PLE_DIM, nb), lambda i, j, k: (j, 0, 0), False)])
    tk = _tile(t, 1024, 16)
    tn = _tile(d, 1024, 128)
    (dwpg,) = _mm(
        "ple_dwgate", (d // tn, 1, t // tk),
        [(sv["h4"], (tk, tn), lambda i, j, k: (k, i)), (dzg, (tk, d), lambda i, j, k: (k, 0))],
        [(0, 1, 0, TN)], [(tn, d)], lambda accs, refs, rows: accs,
        [((d, d), BF16, (tn, d), lambda i, j, k: (i, 0), False)])
    tm3 = _tile(t, 256, 16)

    def epi_dx(accs, refs, rows):
        dx, dw = _rms_bwd(accs[0], refs[2][rows], refs[3][...], refs[4][rows])
        return [dx, dx, dw]

    dx3, dx3_16, dnorm = _mm(
        "ple_dx", (t // tm3, 1, 1),
        [(dzg, (tm3, d), lambda i, j, k: (i, 0)), (wpg, (d, d), lambda i, j, k: (0, 0)),
         (x3, (tm3, d), lambda i, j, k: (i, 0)), (ple_norm, (1, d), lambda i, j, k: (0, 0)),
         (dx4, (tm3, d), lambda i, j, k: (i, 0))],
        [(0, 1, 0, NT)], [(tm3, d)], epi_dx,
        [((t, d), F32, (tm3, d), lambda i, j, k: (i, 0), False),
         ((t, d), BF16, (tm3, d), lambda i, j, k: (i, 0), False),
         ((1, d), F32, (1, d), lambda i, j, k: (0, 0), True)])
    return dx3, dx3_16, dwpp, dwpg, dnorm


def _loss_head(x4, final_norm, target):
    t, d = x4.shape
    tm = _tile(t, 256, 8)

    def body(x_ref, w_ref, tgt_ref, dx_ref, dw_ref, loss_ref):
        i = pl.program_id(0)
        xv = x_ref[...]
        wv = w_ref[...]
        r = lax.rsqrt(jnp.mean(xv * xv, axis=-1, keepdims=True) + EPS)
        err = xv * r * wv - tgt_ref[...]
        part = 0.5 * jnp.sum(jnp.mean(err * err, axis=-1, keepdims=True), axis=0, keepdims=True)
        dx, dw = _rms_bwd(err * (1.0 / d), xv, wv, jnp.zeros_like(xv))
        dx_ref[...] = dx

        @pl.when(i == 0)
        def _():
            dw_ref[...] = dw
            loss_ref[...] = jnp.broadcast_to(part, loss_ref.shape)

        @pl.when(i > 0)
        def _():
            dw_ref[...] += dw
            loss_ref[...] += jnp.broadcast_to(part, loss_ref.shape)

    return pl.pallas_call(
        body,
        name="loss_head",
        grid=(t // tm,),
        in_specs=[pl.BlockSpec((tm, d), lambda i: (i, 0)), pl.BlockSpec((1, d), lambda i: (0, 0)),
                  pl.BlockSpec((tm, d), lambda i: (i, 0))],
        out_specs=[pl.BlockSpec((tm, d), lambda i: (i, 0)), pl.BlockSpec((1, d), lambda i: (0, 0)),
                   pl.BlockSpec((1, LANES), lambda i: (0, 0))],
        out_shape=[jax.ShapeDtypeStruct((t, d), F32), jax.ShapeDtypeStruct((1, d), F32),
                   jax.ShapeDtypeStruct((1, LANES), F32)],
        compiler_params=_params(("arbitrary",)),
    )(x4, final_norm, target)


RELATIONS = ((0, 0), (1, 0), (0, 1), (1, 1))


def _place():
    return lax.axis_index("x"), lax.axis_index("y"), lax.axis_index("c")


def _flip(v, bit):
    return 1 - v if bit else v


def _gather_program(shards, mid_at):
    n = len(shards)
    slots = 8

    def copies(ins, outs, send_sems, recv_sems, local_sems, base):
        x, y, c = _place()
        sibling, x_chip, y_chip = (x, y, 1 - c), (1 - x, y, c), (x, 1 - y, c)

        def block(a, r, core, half=None):
            ref = outs[a].at[4 * _flip(x, RELATIONS[r][0]) + 2 * _flip(y, RELATIONS[r][1]) + core]
            rows = shards[a].shape[0] // 2
            return ref if half is None else ref.at[pl.ds(half * rows, rows)]

        def copy(a, slot, dst, to, src=None):
            return pltpu.make_async_remote_copy(
                src_ref=dst if src is None else src, dst_ref=dst,
                send_sem=send_sems.at[base[0] + slots * a + slot],
                recv_sem=recv_sems.at[base[1] + slots * a + slot], device_id=to, device_id_type=MESH)

        own = [pltpu.make_async_copy(ins[a], block(a, 0, c), local_sems.at[base[2] + a]) for a in range(n)]
        sent_first = [cp for a in range(n) for cp in (
            copy(a, 0, block(a, 0, c), sibling, src=ins[a]),
            copy(a, 1, block(a, 0, c), x_chip, src=ins[a]),
            copy(a, 2, block(a, 0, c), y_chip, src=ins[a]))]
        landed_first = [cp for a in range(n) for cp in (
            copy(a, 1, block(a, 1, c), x_chip), copy(a, 2, block(a, 2, c), y_chip))]
        sent_mid = [cp for a in range(n) for cp in (
            copy(a, 3, block(a, 1, c, 0), y_chip), copy(a, 4, block(a, 2, c, 1), x_chip),
            copy(a, 5, block(a, 1, c), sibling), copy(a, 6, block(a, 2, c), sibling))]
        landed_mid = [cp for a in range(n) for cp in (
            copy(a, 3, block(a, 3, c, 0), y_chip), copy(a, 4, block(a, 3, c, 1), x_chip))]
        sent_last = [copy(a, 7, block(a, 3, c), sibling) for a in range(n)]
        landed_sibling = [cp for a in range(n) for cp in (
            copy(a, 0, block(a, 0, 1 - c), sibling), copy(a, 5, block(a, 1, 1 - c), sibling),
            copy(a, 6, block(a, 2, 1 - c), sibling), copy(a, 7, block(a, 3, 1 - c), sibling))]
        return own, sent_first, landed_first, sent_mid, landed_mid, sent_last, landed_sibling

    def pre(*args):
        own, sent_first = copies(*args)[:2]
        for cp in own + sent_first:
            cp.start()

    def mid(*args):
        _, _, landed_first, sent_mid, _, _, _ = copies(*args)
        for cp in landed_first:
            cp.wait_recv()
        for cp in sent_mid:
            cp.start()

    def late(*args):
        _, _, _, _, landed_mid, sent_last, _ = copies(*args)
        for cp in landed_mid:
            cp.wait_recv()
        for cp in sent_last:
            cp.start()

    def post(*args):
        own, sent_first, _, sent_mid, _, sent_last, landed_sibling = copies(*args)
        for cp in landed_sibling:
            cp.wait_recv()
        for cp in sent_first + sent_mid + sent_last:
            cp.wait_send()
        for mine in own:
            mine.wait()

    return _Comm(list(shards), [jax.ShapeDtypeStruct((N_DEV,) + s.shape, s.dtype) for s in shards],
                 (slots * n, slots * n, n), pre, post, mid, mid_at, late)


def _exchange_program(arrays, n_slots, source, target, slot):
    n = len(arrays)

    def copies(ins, outs, send_sems, recv_sems, local_sems, base):
        place = _place()
        res = []
        for a in range(n):
            for r in range(n_slots):
                src = source(r, place)
                if src is None:
                    continue
                res.append(pltpu.make_async_remote_copy(
                    src_ref=ins[a].at[src], dst_ref=outs[a].at[slot(r)],
                    send_sem=send_sems.at[base[0] + n_slots * a + r],
                    recv_sem=recv_sems.at[base[1] + n_slots * a + r],
                    device_id=target(r, place), device_id_type=MESH))
        return res

    def pre(*args):
        for cp in copies(*args):
            cp.start()

    def post(*args):
        for cp in copies(*args):
            cp.wait()

    n_out = len({slot(r) for r in range(n_slots) if source(r, (0, 0, 0)) is not None})
    return _Comm(list(arrays), [jax.ShapeDtypeStruct((n_out,) + g.shape[1:], g.dtype) for g in arrays],
                 (n_slots * n, n_slots * n, 0), pre, post)


def _sibling_program(grads):
    def source(r, place):
        x, y, c = place
        return 4 * _flip(x, RELATIONS[r][0]) + 2 * _flip(y, RELATIONS[r][1]) + (1 - c)

    return _exchange_program(grads, 4, source, lambda r, p: (p[0], p[1], 1 - p[2]), lambda r: r)


def _chips_program(sums, relations=(1, 2, 3)):
    def target(r, place):
        x, y, c = place
        return (_flip(x, RELATIONS[r][0]), _flip(y, RELATIONS[r][1]), c)

    return _exchange_program(sums, 4, lambda r, p: r if r in relations else None, target,
                             lambda r: relations.index(r) if r in relations else 0)


def _standalone(name, comm):
    return _call(name, lambda: None, (1,), [], [], [], [], [], comm)[1]


def _pair_sum(name, grad, from_sibling, block_ids):
    _, rows, cols = grad.shape
    tr = _tile(rows, 512, 16)

    def body(ids_ref, mine_ref, sib_ref, o_ref):
        o_ref[...] = (mine_ref[...].astype(F32) + sib_ref[...].astype(F32)).astype(o_ref.dtype)

    return pl.pallas_call(
        body,
        name=name,
        grid_spec=pltpu.PrefetchScalarGridSpec(
            num_scalar_prefetch=1,
            grid=(4, rows // tr),
            in_specs=[pl.BlockSpec((None, tr, cols), lambda r, i, ids: (ids[r], i, 0)),
                      pl.BlockSpec((None, tr, cols), lambda r, i, ids: (r, i, 0))],
            out_specs=pl.BlockSpec((None, tr, cols), lambda r, i, ids: (r, i, 0))),
        out_shape=jax.ShapeDtypeStruct((4, rows, cols), BF16),
        compiler_params=_params(("arbitrary", "arbitrary")),
    )(block_ids, grad, from_sibling)


def _adam(w, g, m, v):
    m = ADAM_B1 * m + (1.0 - ADAM_B1) * g
    v = ADAM_B2 * v + (1.0 - ADAM_B2) * (g * g)
    m_hat = m / (1.0 - ADAM_B1 ** ADAM_STEP)
    v_hat = v / (1.0 - ADAM_B2 ** ADAM_STEP)
    delta = -ADAM_LR * (m_hat / (jnp.sqrt(v_hat) + ADAM_EPS) + ADAM_WD * w)
    return delta, m, v


def _adam_sharded(name, items, comm=None):
    _, rows, cols = items[0][0].shape
    n = len(items)
    tr = _tile(rows, max(64, 256 // n), 8)

    def body(*refs):
        for k in range(n):
            w_ref, m_ref, v_ref, s_ref, r1_ref, r2_ref, r3_ref = refs[7 * k:7 * k + 7]
            g_ref, d_ref, nm_ref, nv_ref = refs[7 * n + 4 * k:7 * n + 4 * k + 4]
            g = ((s_ref[...].astype(F32) + r1_ref[...].astype(F32)) + r2_ref[...].astype(F32)) + r3_ref[...].astype(F32)
            delta, nm, nv = _adam(w_ref[...], g, m_ref[...], v_ref[...])
            g_ref[...] = g
            d_ref[...] = delta
            nm_ref[...] = nm
            nv_ref[...] = nv

    slot = lambda s: pl.BlockSpec((None, tr, cols), lambda i: (s, i, 0))
    flat = slot(0)
    args, specs = [], []
    for w, m, v, sums, from_chips in items:
        args += [w, m, v, sums] + [a for a, _ in from_chips]
        specs += [flat, flat, flat, slot(0)] + [slot(s) for _, s in from_chips]
    res, sent = _call(
        name, body, (rows // tr,), specs, [flat] * (4 * n),
        [jax.ShapeDtypeStruct((1, rows, cols), F32)] * (4 * n), [], args, comm)
    return [res[4 * k:4 * k + 4] for k in range(n)], sent


def _all_reduce_small(part):
    rows = part.shape[0]

    def body(x_ref, o_ref, gathered, send_sems, recv_sems):
        x, y, c = _place()
        me = 4 * x + 2 * y + c
        gathered[me] = x_ref[...]
        copies = []
        for k in range(1, N_DEV):
            bits = (k >> 2 & 1, k >> 1 & 1, k & 1)
            peer = (_flip(x, bits[0]), _flip(y, bits[1]), _flip(c, bits[2]))
            copies.append(pltpu.make_async_remote_copy(
                src_ref=x_ref, dst_ref=gathered.at[me],
                send_sem=send_sems.at[k - 1], recv_sem=recv_sems.at[k - 1],
                device_id=peer, device_id_type=MESH))
        for cp in copies:
            cp.start()
        for cp in copies:
            cp.wait()
        total = gathered[0]
        for b in range(1, N_DEV):
            total = total + gathered[b]
        o_ref[...] = total

    return pl.pallas_call(
        body,
        name="small_all_reduce",
        in_specs=[pl.BlockSpec(memory_space=pltpu.VMEM)],
        out_specs=pl.BlockSpec(memory_space=pltpu.VMEM),
        out_shape=jax.ShapeDtypeStruct((rows, LANES), F32),
        scratch_shapes=[pltpu.VMEM((N_DEV, rows, LANES), F32),
                        pltpu.SemaphoreType.DMA((N_DEV - 1,)), pltpu.SemaphoreType.DMA((N_DEV - 1,))],
    )(part)


def _adam_small(w, g, m, v):
    def fn(wv, gv, mv, vv):
        return _adam(wv, gv, mv, vv)

    shape = (w.shape, F32)
    return _rows("adam_small", fn, [(w, True), (g, True), (m, True), (v, True)], [shape] * 3, w.shape[0])


SMALL = ("ffn1_norm", "mix_norm", "attn_sinks", "hgrn_lower_bound", "hgrn_norm", "ffn2_norm", "ple_norm",
         "final_norm")
LARGE = ("ffn1_w_gate", "ffn1_w_up", "ffn1_w_down", "w_in", "w_up_a", "w_up_b", "w_out",
         "ffn2_w_gate", "ffn2_w_up", "ffn2_w_down", "ple_w_gate", "ple_w_proj")
TRANSPOSED = ("ffn1_w_gate", "ffn1_w_up", "ffn2_w_gate", "ffn2_w_up", "w_in")
ADAM_GROUPS = (("ffn2_w_down", "ffn2_w_gate", "ffn2_w_up"), ("ple_w_gate", "w_out"), ("w_up_a", "w_up_b"),
               ("ple_w_proj",), ("w_in",), ("ffn1_w_down", "ffn1_w_gate", "ffn1_w_up"))
WEIGHTS = ("ffn1_norm", "ffn1_w_gate", "ffn1_w_up", "ffn1_w_down", "mix_norm", "w_in", "attn_sinks",
           "hgrn_lower_bound", "hgrn_norm", "w_up_a", "w_up_b", "w_out", "ffn2_norm", "ffn2_w_gate",
           "ffn2_w_up", "ffn2_w_down", "ple_norm", "ple_w_gate", "ple_w_proj", "final_norm")


def _pack_rows(arrays):
    rows = []
    for a in arrays:
        flat = a.reshape(-1).astype(F32)
        pad = -flat.shape[0] % LANES
        rows.append(jnp.pad(flat, (0, pad)).reshape(-1, LANES))
    packed = jnp.concatenate(rows, axis=0)
    return jnp.pad(packed, ((0, -packed.shape[0] % 8), (0, 0)))


def _unpack_rows(packed, like):
    out, at = [], 0
    for a in like:
        size = a.size
        n_rows = -(-size // LANES)
        out.append(packed[at:at + n_rows].reshape(-1)[:size].reshape(a.shape))
        at += n_rows
    return out


def kernel(x, p, positions, ffn1_norm, ffn1_w_gate, ffn1_w_up, ffn1_w_down, mix_norm, w_in, attn_sinks, hgrn_lower_bound, hgrn_norm, w_up_a, w_up_b, w_out, ffn2_norm, ffn2_w_gate, ffn2_w_up, ffn2_w_down, ple_norm, ple_w_gate, ple_w_proj, final_norm, loss_target, m_ffn1_norm, m_ffn1_w_gate, m_ffn1_w_up, m_ffn1_w_down, m_mix_norm, m_w_in, m_attn_sinks, m_hgrn_lower_bound, m_hgrn_norm, m_w_up_a, m_w_up_b, m_w_out, m_ffn2_norm, m_ffn2_w_gate, m_ffn2_w_up, m_ffn2_w_down, m_ple_norm, m_ple_w_gate, m_ple_w_proj, m_final_norm, v_ffn1_norm, v_ffn1_w_gate, v_ffn1_w_up, v_ffn1_w_down, v_mix_norm, v_w_in, v_attn_sinks, v_hgrn_lower_bound, v_hgrn_norm, v_w_up_a, v_w_up_b, v_w_out, v_ffn2_norm, v_ffn2_w_gate, v_ffn2_w_up, v_ffn2_w_down, v_ple_norm, v_ple_w_gate, v_ple_w_proj, v_final_norm):
    given = dict(locals())
    w = {n: given[n] for n in WEIGHTS}
    mom = {n: given["m_" + n] for n in WEIGHTS}
    var = {n: given["v_" + n] for n in WEIGHTS}
    t, d = x.shape[1], x.shape[2]

    x0, p16, target = x[0], p[0, 0].astype(BF16), loss_target[0]
    tables = _rope_tables(positions)
    sinks = attn_sinks.reshape(-1)
    final_w = final_norm.reshape(1, d)
    as_stored = lambda n, a: jnp.swapaxes(a, 1, 2) if n in TRANSPOSED else a
    shard = {n: as_stored(n, w[n])[0].astype(BF16) for n in LARGE}
    shard["w_in_left"], shard["w_in_right"] = shard["w_in"][:, :d // 2], shard["w_in"][:, d // 2:]
    xi, yi, ci = _place()
    block_ids = jnp.stack([4 * _flip(xi, rx) + 2 * _flip(yi, ry) + ci for rx, ry in RELATIONS]).astype(jnp.int32)

    def gather(*names):
        return _gather_program([shard[n] for n in names], 0.65)

    h1, (wg1, wu1) = _rmsnorm("ffn1_norm", x0, ffn1_norm, gather("ffn1_w_gate", "ffn1_w_up"))
    (g1, u1, a1), (wd1, win_left) = _ffn_up("ffn1", h1, wg1, wu1, gather("ffn1_w_down", "w_in_left"))
    x1, (win_right,) = _ffn_down("ffn1", x0, a1, wd1, gather("w_in_right"))
    win = (win_left.reshape(-1, d // 2), win_right.reshape(-1, d // 2))
    mix_args = (mix_norm, win, sinks, hgrn_lower_bound, hgrn_norm)

    def mixer_weights(sent):
        wua, wub, wout_g, _ = sent["proj"]
        return wua, wub, wout_g.reshape(d, d)

    x2, mix_saved, got_w = _mixer_fwd(
        x1, *mix_args, mixer_weights, tables,
        dict(proj=gather("w_up_a", "w_up_b", "w_out", "ffn2_w_gate"), attention=gather("ffn2_w_up"),
             hgrn=gather("ple_w_gate", "ple_w_proj")))
    wua, wub, wout = mixer_weights(got_w)
    wg2, (wu2,) = got_w["proj"][3], got_w["attention"]
    wpg_g, wpp = got_w["hgrn"]
    wpg = wpg_g.reshape(d, d)
    h3 = _rmsnorm("ffn2_norm", x2, ffn2_norm)
    (g2, u2, a2), (wd2,) = _ffn_up("ffn2", h3, wg2, wu2, gather("ffn2_w_down"))
    x3, _ = _ffn_down("ffn2", x2, a2, wd2)
    x4, ple_saved = _ple_fwd(x3, ple_norm, p16, wpg, wpp)
    dx4, d_final, loss_row = _loss_head(x4, final_w, target)

    sums, from_chips = {}, {}

    def pair_sums(names, grads, from_sibling):
        for n, g, s in zip(names, grads, from_sibling):
            sums[n] = _pair_sum("pair_sum_" + n, g, s, block_ids)

    def to_chips(names, grads, from_sibling):
        pair_sums(names, grads, from_sibling)
        return _chips_program([sums[n] for n in names])

    def arrived(names, results):
        for n, r in zip(names, results):
            from_chips[n] = [(r, 0), (r, 1), (r, 2)]

    def ffn_comms(first, second, names):
        dn, gn, un = ((n,) for n in names)
        return dict(
            dact=first, dwd=second,
            dwg=lambda av: _sibling_program([av["dwd"]]),
            dwu=lambda av: _merge([to_chips(dn, [av["dwd"]], av["sent"]["dwg"]), _sibling_program([av["dwg"]])]),
            dxa=lambda av: _merge([to_chips(gn, [av["dwg"]], av["sent"]["dwu"][1:]), _sibling_program([av["dwu"]])]),
            dxb=lambda av: to_chips(un, [av["dwu"]], av["sent"]["dxa"][1:]))

    def ffn_arrived(names, sent):
        arrived(names, [sent["dwu"][0], sent["dxa"][0], sent["dxb"][0]])

    dx3, dx3_16, dwpp, dwpg, d_ple = _ple_bwd(dx4, x3, ple_norm, p16, wpg, ple_saved)
    ple_names = ("ple_w_proj", "ple_w_gate")
    ple_grads = [dwpp, dwpg.reshape(N_DEV, d // N_DEV, d)]
    ffn2_names = ("ffn2_w_down", "ffn2_w_gate", "ffn2_w_up")
    dx2, dx2_16, _, _, _, d_ffn2, sent2 = _ffn_bwd(
        "ffn2b", dx3, dx3_16, x2, h3, g2, u2, a2, wg2, wu2, wd2, ffn2_norm,
        ffn_comms(_sibling_program(ple_grads), lambda av: to_chips(ple_names, ple_grads, av["sent"]["dact"]),
                  ffn2_names))
    arrived(ple_names, sent2["dwd"])
    ffn_arrived(ffn2_names, sent2)
    mix3 = ("w_out", "w_up_a", "w_up_b")
    dx1, dx1_16, mix_small, mix_made = _mixer_bwd(
        dx2, dx2_16, x1, *mix_args, wua, wub, wout, tables, mix_saved,
        dict(attention=lambda av: _sibling_program([av[n] for n in mix3]),
             hgrn=lambda av: to_chips(mix3, [av[n] for n in mix3], av["sent"]["attention"]),
             dx=lambda av: _sibling_program([av["w_in"]])))
    sent_mix = mix_made["sent"]
    arrived(mix3, sent_mix["hgrn"])
    pair_sums(("w_in",), [mix_made["w_in"]], sent_mix["dx"])
    ffn1_names = ("ffn1_w_down", "ffn1_w_gate", "ffn1_w_up")
    grad_x, _, _, _, _, d_ffn1, sent1 = _ffn_bwd(
        "ffn1b", dx1, dx1_16, x0, h1, g1, u1, a1, wg1, wu1, wd1, ffn1_norm,
        ffn_comms(_chips_program([sums["w_in"]], (1, 2)), _chips_program([sums["w_in"]], (3,)), ffn1_names))
    (near,), (far,) = sent1["dact"], sent1["dwd"]
    from_chips["w_in"] = [(near, 0), (near, 1), (far, 0)]
    ffn_arrived(ffn1_names, sent1)

    out = {}
    for group in ADAM_GROUPS:
        items = [(as_stored(n, w[n]), as_stored(n, mom[n]), as_stored(n, var[n]), sums[n], from_chips[n])
                 for n in group]
        results, _ = _adam_sharded("adam_" + group[0], items)
        for n, res in zip(group, results):
            out[n] = [as_stored(n, a) for a in res]

    dlb = mix_small["hgrn_lb"]
    small_grads = dict(
        ffn1_norm=d_ffn1, mix_norm=mix_small["mix_norm"], attn_sinks=mix_small["attn_sinks"][:, :A_HEADS],
        hgrn_lower_bound=jnp.concatenate([dlb, -dlb], axis=0), hgrn_norm=mix_small["hgrn_norm"],
        ffn2_norm=d_ffn2, ple_norm=d_ple, final_norm=d_final)

    small_like = [w[n] for n in SMALL]
    packed = _pack_rows([small_grads[n].reshape(w[n].shape) for n in SMALL] + [loss_row[:, :1]])
    total = _all_reduce_small(packed)
    n_rows = total.shape[0]
    pack_w = _pack_rows(small_like + [jnp.zeros((1, 1), F32)])
    pack_m = _pack_rows([mom[n] for n in SMALL] + [jnp.zeros((1, 1), F32)])
    pack_v = _pack_rows([var[n] for n in SMALL] + [jnp.zeros((1, 1), F32)])
    deltas = _adam_small(pack_w, total, pack_m, pack_v)
    like = small_like + [jnp.zeros((1, 1), F32)]
    g_small = _unpack_rows(total, like)
    d_small, m_small, v_small = (_unpack_rows(a, like) for a in deltas)
    for k, n in enumerate(SMALL):
        out[n] = [g_small[k], d_small[k], m_small[k], v_small[k]]
    loss = g_small[-1].reshape(())

    return (loss, grad_x.reshape(x.shape),
            *[out[n][0] for n in WEIGHTS], *[out[n][1] for n in WEIGHTS],
            *[out[n][2] for n in WEIGHTS], *[out[n][3] for n in WEIGHTS])
```
